```python
import math
import jax
import jax.numpy as jnp
from jax import lax
import numpy as np

D_MODEL = 1024
BATCH = 8
SEQ = 4096
DEPTH = 1

CHUNK = 64
PLE_DIM = 256
D_FF = 2816
NORM_EPS = 1e-6

S5_WIDTH = 512
S5_GROUP = 16
S5_GROUPS = S5_WIDTH // S5_GROUP
S5_STATE = 64
S5_DT_MIN = 1e-3
S5_DT_MAX = 1e-1

HG_HEADS = 8
HG_EXPAND = 128
HG_WIDTH = HG_HEADS * HG_EXPAND

N_BRANCHES = 2
IN_SPLITS = (S5_WIDTH, HG_WIDTH, HG_WIDTH, HG_WIDTH, HG_WIDTH, D_MODEL, D_MODEL)
IN_COLS = S5_WIDTH + 4 * HG_WIDTH + N_BRANCHES * D_MODEL

kernel_name = 'hybrid_s5_hgrn2_macaron_block'


def rmsnorm(x, gain):
    xf = x.astype(jnp.float32)
    y = xf * lax.rsqrt(jnp.mean(jnp.square(xf), axis=-1, keepdims=True) + NORM_EPS)
    return (y * gain.astype(jnp.float32)).astype(x.dtype)


def swiglu(x, w_gate, w_up, w_down):
    return (jax.nn.silu(x @ w_gate) * (x @ w_up)) @ w_down


def split_columns(proj):
    pieces, start = [], 0
    for width in IN_SPLITS:
        pieces.append(proj[..., start:start + width])
        start += width
    return pieces


def s5_mixer(u, lam_re, lam_im, log_dt, b_re, b_im, c_re, c_im, d_skip):
    f32 = jnp.float32
    bsz, seqlen, _ = u.shape
    uf = u.astype(f32).reshape(bsz, seqlen, S5_GROUPS, S5_GROUP)
    lam = lax.complex(lam_re.astype(f32), lam_im.astype(f32))
    dt = jnp.exp(log_dt.astype(f32))[:, None]
    lam_bar = jnp.exp(lam * dt)
    b = lax.complex(b_re.astype(f32), b_im.astype(f32))
    b_bar = ((lam_bar - 1.0) / lam)[..., None] * b
    bu = jnp.einsum('blgh,gph->blgp', uf, b_bar)
    a = jnp.broadcast_to(lam_bar, (1, seqlen, S5_GROUPS, S5_STATE))

    def combine(left, right):
        a_l, b_l = left
        a_r, b_r = right
        return a_r * a_l, a_r * b_l + b_r

    _, states = lax.associative_scan(combine, (a, bu), axis=1)
    c = lax.complex(c_re.astype(f32), c_im.astype(f32))
    y = jnp.einsum('blgp,ghp->blgh', states, c).real
    y = y + d_skip.astype(f32).reshape(S5_GROUPS, S5_GROUP) * uf
    return y.reshape(bsz, seqlen, S5_WIDTH)


def hgrn2_mixer(q, f_logit, i_in, lower_bound):
    f32 = jnp.float32
    bsz, seqlen, _ = q.shape
    n_chunks = seqlen // CHUNK
    lb = lower_bound.astype(f32)
    z = f_logit.astype(f32)
    log_f = jnp.log(lb + (1.0 - lb) * jax.nn.sigmoid(z))
    k = (1.0 - lb) * jax.nn.sigmoid(-z)

    def to_chunks(t):
        return t.reshape(bsz, n_chunks, CHUNK, HG_HEADS, HG_EXPAND).transpose(1, 0, 3, 2, 4)

    qc = to_chunks(q.astype(f32) * (HG_EXPAND ** -0.5))
    kc = to_chunks(k)
    vc = to_chunks(i_in.astype(f32))
    gc = to_chunks(log_f)
    causal = jnp.tril(jnp.ones((CHUNK, CHUNK), dtype=bool))[:, :, None]

    def step(state, inp):
        q_c, k_c, v_c, g_c = inp
        gcum = jnp.cumsum(g_c, axis=2)
        o_inter = jnp.einsum('bhck,bhkv->bhcv', q_c * jnp.exp(gcum), state)
        diff = gcum[:, :, :, None, :] - gcum[:, :, None, :, :]
        decay = jnp.exp(jnp.where(causal, diff, -jnp.inf))
        scores = jnp.einsum('bhtk,bhsk,bhtsk->bhts', q_c, k_c, decay)
        o_intra = jnp.einsum('bhts,bhsv->bhtv', scores, v_c)
        g_last = gcum[:, :, -1:, :]
        k_dec = k_c * jnp.exp(g_last - gcum)
        new_state = (jnp.exp(g_last[:, :, 0, :])[..., None] * state
                     + jnp.einsum('bhsk,bhsv->bhkv', k_dec, v_c))
        return new_state, o_inter + o_intra

    s0 = jnp.zeros((bsz, HG_HEADS, HG_EXPAND, HG_EXPAND), f32)
    _, o = lax.scan(step, s0, (qc, kc, vc, gc))
    return o.transpose(1, 0, 3, 2, 4).reshape(bsz, seqlen, HG_WIDTH)


def head_rmsnorm(o, gain):
    bsz, seqlen, _ = o.shape
    oh = o.reshape(bsz, seqlen, HG_HEADS, HG_EXPAND)
    return rmsnorm(oh, gain.reshape(HG_HEADS, HG_EXPAND)).reshape(bsz, seqlen, HG_WIDTH)


def _fwd_setup_inputs(seed: int = 0) -> dict:
    key = jax.random.key(seed)
    ks = jax.random.split(key, 40)
    f32 = jnp.float32

    def nrm(k, shape, scale):
        return jax.random.normal(k, shape, f32) * scale

    def gain(k, shape):
        return 1.0 + 0.01 * jax.random.normal(k, shape, f32)

    n_idx = jnp.arange(S5_STATE, dtype=f32)
    return {
        'x': nrm(ks[0], (BATCH, SEQ, D_MODEL), 1.0),
        'p': nrm(ks[1], (DEPTH, BATCH, SEQ, PLE_DIM), 1.0),
        'ffn1_norm': gain(ks[2], (DEPTH, D_MODEL)),
        'ffn1_w_gate': nrm(ks[3], (DEPTH, D_MODEL, D_FF), D_MODEL ** -0.5),
        'ffn1_w_up': nrm(ks[4], (DEPTH, D_MODEL, D_FF), D_MODEL ** -0.5),
        'ffn1_w_down': nrm(ks[5], (DEPTH, D_FF, D_MODEL), D_FF ** -0.5),
        'mix_norm': gain(ks[6], (DEPTH, D_MODEL)),
        'w_in': nrm(ks[7], (DEPTH, D_MODEL, IN_COLS), D_MODEL ** -0.5),
        's5_lam_re': -0.5 + 0.01 * jax.random.normal(ks[8], (DEPTH, S5_GROUPS, S5_STATE), f32),
        's5_lam_im': math.pi * n_idx + 0.01 * jax.random.normal(ks[9], (DEPTH, S5_GROUPS, S5_STATE), f32),
        's5_log_dt': jax.random.uniform(ks[10], (DEPTH, S5_GROUPS), f32,
                                        minval=math.log(S5_DT_MIN), maxval=math.log(S5_DT_MAX)),
        's5_b_re': nrm(ks[11], (DEPTH, S5_GROUPS, S5_STATE, S5_GROUP), (2.0 * S5_GROUP) ** -0.5),
        's5_b_im': nrm(ks[12], (DEPTH, S5_GROUPS, S5_STATE, S5_GROUP), (2.0 * S5_GROUP) ** -0.5),
        's5_c_re': nrm(ks[13], (DEPTH, S5_GROUPS, S5_GROUP, S5_STATE), 0.5),
        's5_c_im': nrm(ks[14], (DEPTH, S5_GROUPS, S5_GROUP, S5_STATE), 0.5),
        's5_d': nrm(ks[15], (DEPTH, S5_WIDTH), 1.0),
        's5_glu_val': nrm(ks[16], (DEPTH, S5_WIDTH, D_MODEL), S5_WIDTH ** -0.5),
        's5_glu_gate': nrm(ks[17], (DEPTH, S5_WIDTH, D_MODEL), S5_WIDTH ** -0.5),
        'hg_lower_bound': nrm(ks[18], (DEPTH + 1, HG_WIDTH), 0.1),
        'hg_out_norm': gain(ks[19], (DEPTH, HG_WIDTH)),
        'hg_w_out': nrm(ks[20], (DEPTH, HG_WIDTH, D_MODEL), HG_WIDTH ** -0.5),
        'w_merge_out': nrm(ks[21], (DEPTH, D_MODEL, D_MODEL), D_MODEL ** -0.5),
        'ffn2_norm': gain(ks[22], (DEPTH, D_MODEL)),
        'ffn2_w_gate': nrm(ks[23], (DEPTH, D_MODEL, D_FF), D_MODEL ** -0.5),
        'ffn2_w_up': nrm(ks[24], (DEPTH, D_MODEL, D_FF), D_MODEL ** -0.5),
        'ffn2_w_down': nrm(ks[25], (DEPTH, D_FF, D_MODEL), D_FF ** -0.5),
        'ple_norm': gain(ks[26], (DEPTH, D_MODEL)),
        'ple_w_gate': nrm(ks[27], (DEPTH, D_MODEL, D_MODEL), D_MODEL ** -0.5),
        'ple_w_proj': nrm(ks[28], (DEPTH, PLE_DIM, D_MODEL), PLE_DIM ** -0.5),
        'final_norm': gain(ks[29], (D_MODEL,)),
    }


def _fwd_reference(x, p, ffn1_norm, ffn1_w_gate, ffn1_w_up, ffn1_w_down, mix_norm, w_in,
              s5_lam_re, s5_lam_im, s5_log_dt, s5_b_re, s5_b_im, s5_c_re, s5_c_im, s5_d,
              s5_glu_val, s5_glu_gate, hg_lower_bound, hg_out_norm, hg_w_out, w_merge_out,
              ffn2_norm, ffn2_w_gate, ffn2_w_up, ffn2_w_down, ple_norm, ple_w_gate,
              ple_w_proj, final_norm):
    lower_bounds = jnp.cumsum(jax.nn.softmax(hg_lower_bound.astype(jnp.float32), axis=0), axis=0)
    h = x
    for layer in range(DEPTH):
        h = h + 0.5 * swiglu(rmsnorm(h, ffn1_norm[layer]),
                             ffn1_w_gate[layer], ffn1_w_up[layer], ffn1_w_down[layer])

        u = rmsnorm(h, mix_norm[layer])
        s5_in, hg_q, hg_f, hg_i, hg_g, gate_a, gate_b = split_columns(u @ w_in[layer])

        y_s5 = s5_mixer(s5_in, s5_lam_re[layer], s5_lam_im[layer], s5_log_dt[layer],
                        s5_b_re[layer], s5_b_im[layer], s5_c_re[layer], s5_c_im[layer],
                        s5_d[layer])
        y_s5 = jax.nn.gelu(y_s5).astype(h.dtype)
        y_a = (y_s5 @ s5_glu_val[layer]) * jax.nn.sigmoid(y_s5 @ s5_glu_gate[layer])

        o = hgrn2_mixer(hg_q, hg_f, hg_i, lower_bounds[layer]).astype(h.dtype)
        o = head_rmsnorm(o, hg_out_norm[layer]) * jax.nn.silu(hg_g)
        y_b = o @ hg_w_out[layer]

        mixed = jax.nn.sigmoid(gate_a) * y_a + jax.nn.sigmoid(gate_b) * y_b
        h = h + mixed @ w_merge_out[layer]

        h = h + 0.5 * swiglu(rmsnorm(h, ffn2_norm[layer]),
                             ffn2_w_gate[layer], ffn2_w_up[layer], ffn2_w_down[layer])

        ple_gate = jax.nn.sigmoid(rmsnorm(h, ple_norm[layer]) @ ple_w_gate[layer])
        h = h + ple_gate * (p[layer] @ ple_w_proj[layer])
    return rmsnorm(h, final_norm)


import jax as _jax
import jax.numpy as _jnp

TWIN_FORMAT = 'train_step'
FWD_PARAMS = ['x', 'p', 'ffn1_norm', 'ffn1_w_gate', 'ffn1_w_up', 'ffn1_w_down', 'mix_norm', 'w_in', 's5_lam_re', 's5_lam_im', 's5_log_dt', 's5_b_re', 's5_b_im', 's5_c_re', 's5_c_im', 's5_d', 's5_glu_val', 's5_glu_gate', 'hg_lower_bound', 'hg_out_norm', 'hg_w_out', 'w_merge_out', 'ffn2_norm', 'ffn2_w_gate', 'ffn2_w_up', 'ffn2_w_down', 'ple_norm', 'ple_w_gate', 'ple_w_proj', 'final_norm']
TWIN_WEIGHTS = ['ffn1_norm', 'ffn1_w_gate', 'ffn1_w_up', 'ffn1_w_down', 'mix_norm', 'w_in', 's5_lam_re', 's5_lam_im', 's5_log_dt', 's5_b_re', 's5_b_im', 's5_c_re', 's5_c_im', 's5_d', 's5_glu_val', 's5_glu_gate', 'hg_lower_bound', 'hg_out_norm', 'hg_w_out', 'w_merge_out', 'ffn2_norm', 'ffn2_w_gate', 'ffn2_w_up', 'ffn2_w_down', 'ple_norm', 'ple_w_gate', 'ple_w_proj', 'final_norm']
TWIN_DIFF_INPUT = 'x'
TWIN_INPUTS = ['x', 'p', 'ffn1_norm', 'ffn1_w_gate', 'ffn1_w_up', 'ffn1_w_down', 'mix_norm', 'w_in', 's5_lam_re', 's5_lam_im', 's5_log_dt', 's5_b_re', 's5_b_im', 's5_c_re', 's5_c_im', 's5_d', 's5_glu_val', 's5_glu_gate', 'hg_lower_bound', 'hg_out_norm', 'hg_w_out', 'w_merge_out', 'ffn2_norm', 'ffn2_w_gate', 'ffn2_w_up', 'ffn2_w_down', 'ple_norm', 'ple_w_gate', 'ple_w_proj', 'final_norm', 'loss_target', 'm_ffn1_norm', 'm_ffn1_w_gate', 'm_ffn1_w_up', 'm_ffn1_w_down', 'm_mix_norm', 'm_w_in', 'm_s5_lam_re', 'm_s5_lam_im', 'm_s5_log_dt', 'm_s5_b_re', 'm_s5_b_im', 'm_s5_c_re', 'm_s5_c_im', 'm_s5_d', 'm_s5_glu_val', 'm_s5_glu_gate', 'm_hg_lower_bound', 'm_hg_out_norm', 'm_hg_w_out', 'm_w_merge_out', 'm_ffn2_norm', 'm_ffn2_w_gate', 'm_ffn2_w_up', 'm_ffn2_w_down', 'm_ple_norm', 'm_ple_w_gate', 'm_ple_w_proj', 'm_final_norm', 'v_ffn1_norm', 'v_ffn1_w_gate', 'v_ffn1_w_up', 'v_ffn1_w_down', 'v_mix_norm', 'v_w_in', 'v_s5_lam_re', 'v_s5_lam_im', 'v_s5_log_dt', 'v_s5_b_re', 'v_s5_b_im', 'v_s5_c_re', 'v_s5_c_im', 'v_s5_d', 'v_s5_glu_val', 'v_s5_glu_gate', 'v_hg_lower_bound', 'v_hg_out_norm', 'v_hg_w_out', 'v_w_merge_out', 'v_ffn2_norm', 'v_ffn2_w_gate', 'v_ffn2_w_up', 'v_ffn2_w_down', 'v_ple_norm', 'v_ple_w_gate', 'v_ple_w_proj', 'v_final_norm']
TWIN_OUTPUTS = ['loss', 'grad_x', 'grad_ffn1_norm', 'grad_ffn1_w_gate', 'grad_ffn1_w_up', 'grad_ffn1_w_down', 'grad_mix_norm', 'grad_w_in', 'grad_s5_lam_re', 'grad_s5_lam_im', 'grad_s5_log_dt', 'grad_s5_b_re', 'grad_s5_b_im', 'grad_s5_c_re', 'grad_s5_c_im', 'grad_s5_d', 'grad_s5_glu_val', 'grad_s5_glu_gate', 'grad_hg_lower_bound', 'grad_hg_out_norm', 'grad_hg_w_out', 'grad_w_merge_out', 'grad_ffn2_norm', 'grad_ffn2_w_gate', 'grad_ffn2_w_up', 'grad_ffn2_w_down', 'grad_ple_norm', 'grad_ple_w_gate', 'grad_ple_w_proj', 'grad_final_norm', 'delta_ffn1_norm', 'delta_ffn1_w_gate', 'delta_ffn1_w_up', 'delta_ffn1_w_down', 'delta_mix_norm', 'delta_w_in', 'delta_s5_lam_re', 'delta_s5_lam_im', 'delta_s5_log_dt', 'delta_s5_b_re', 'delta_s5_b_im', 'delta_s5_c_re', 'delta_s5_c_im', 'delta_s5_d', 'delta_s5_glu_val', 'delta_s5_glu_gate', 'delta_hg_lower_bound', 'delta_hg_out_norm', 'delta_hg_w_out', 'delta_w_merge_out', 'delta_ffn2_norm', 'delta_ffn2_w_gate', 'delta_ffn2_w_up', 'delta_ffn2_w_down', 'delta_ple_norm', 'delta_ple_w_gate', 'delta_ple_w_proj', 'delta_final_norm', 'new_m_ffn1_norm', 'new_m_ffn1_w_gate', 'new_m_ffn1_w_up', 'new_m_ffn1_w_down', 'new_m_mix_norm', 'new_m_w_in', 'new_m_s5_lam_re', 'new_m_s5_lam_im', 'new_m_s5_log_dt', 'new_m_s5_b_re', 'new_m_s5_b_im', 'new_m_s5_c_re', 'new_m_s5_c_im', 'new_m_s5_d', 'new_m_s5_glu_val', 'new_m_s5_glu_gate', 'new_m_hg_lower_bound', 'new_m_hg_out_norm', 'new_m_hg_w_out', 'new_m_w_merge_out', 'new_m_ffn2_norm', 'new_m_ffn2_w_gate', 'new_m_ffn2_w_up', 'new_m_ffn2_w_down', 'new_m_ple_norm', 'new_m_ple_w_gate', 'new_m_ple_w_proj', 'new_m_final_norm', 'new_v_ffn1_norm', 'new_v_ffn1_w_gate', 'new_v_ffn1_w_up', 'new_v_ffn1_w_down', 'new_v_mix_norm', 'new_v_w_in', 'new_v_s5_lam_re', 'new_v_s5_lam_im', 'new_v_s5_log_dt', 'new_v_s5_b_re', 'new_v_s5_b_im', 'new_v_s5_c_re', 'new_v_s5_c_im', 'new_v_s5_d', 'new_v_s5_glu_val', 'new_v_s5_glu_gate', 'new_v_hg_lower_bound', 'new_v_hg_out_norm', 'new_v_hg_w_out', 'new_v_w_merge_out', 'new_v_ffn2_norm', 'new_v_ffn2_w_gate', 'new_v_ffn2_w_up', 'new_v_ffn2_w_down', 'new_v_ple_norm', 'new_v_ple_w_gate', 'new_v_ple_w_proj', 'new_v_final_norm']
TWIN_LEAF_KINDS = {'loss': 'loss', 'grad_x': 'grad_x', 'grad_ffn1_norm': 'grad_w', 'grad_ffn1_w_gate': 'grad_w', 'grad_ffn1_w_up': 'grad_w', 'grad_ffn1_w_down': 'grad_w', 'grad_mix_norm': 'grad_w', 'grad_w_in': 'grad_w', 'grad_s5_lam_re': 'grad_w', 'grad_s5_lam_im': 'grad_w', 'grad_s5_log_dt': 'grad_w', 'grad_s5_b_re': 'grad_w', 'grad_s5_b_im': 'grad_w', 'grad_s5_c_re': 'grad_w', 'grad_s5_c_im': 'grad_w', 'grad_s5_d': 'grad_w', 'grad_s5_glu_val': 'grad_w', 'grad_s5_glu_gate': 'grad_w', 'grad_hg_lower_bound': 'grad_w', 'grad_hg_out_norm': 'grad_w', 'grad_hg_w_out': 'grad_w', 'grad_w_merge_out': 'grad_w', 'grad_ffn2_norm': 'grad_w', 'grad_ffn2_w_gate': 'grad_w', 'grad_ffn2_w_up': 'grad_w', 'grad_ffn2_w_down': 'grad_w', 'grad_ple_norm': 'grad_w', 'grad_ple_w_gate': 'grad_w', 'grad_ple_w_proj': 'grad_w', 'grad_final_norm': 'grad_w', 'delta_ffn1_norm': 'delta_w', 'delta_ffn1_w_gate': 'delta_w', 'delta_ffn1_w_up': 'delta_w', 'delta_ffn1_w_down': 'delta_w', 'delta_mix_norm': 'delta_w', 'delta_w_in': 'delta_w', 'delta_s5_lam_re': 'delta_w', 'delta_s5_lam_im': 'delta_w', 'delta_s5_log_dt': 'delta_w', 'delta_s5_b_re': 'delta_w', 'delta_s5_b_im': 'delta_w', 'delta_s5_c_re': 'delta_w', 'delta_s5_c_im': 'delta_w', 'delta_s5_d': 'delta_w', 'delta_s5_glu_val': 'delta_w', 'delta_s5_glu_gate': 'delta_w', 'delta_hg_lower_bound': 'delta_w', 'delta_hg_out_norm': 'delta_w', 'delta_hg_w_out': 'delta_w', 'delta_w_merge_out': 'delta_w', 'delta_ffn2_norm': 'delta_w', 'delta_ffn2_w_gate': 'delta_w', 'delta_ffn2_w_up': 'delta_w', 'delta_ffn2_w_down': 'delta_w', 'delta_ple_norm': 'delta_w', 'delta_ple_w_gate': 'delta_w', 'delta_ple_w_proj': 'delta_w', 'delta_final_norm': 'delta_w', 'new_m_ffn1_norm': 'new_m', 'new_m_ffn1_w_gate': 'new_m', 'new_m_ffn1_w_up': 'new_m', 'new_m_ffn1_w_down': 'new_m', 'new_m_mix_norm': 'new_m', 'new_m_w_in': 'new_m', 'new_m_s5_lam_re': 'new_m', 'new_m_s5_lam_im': 'new_m', 'new_m_s5_log_dt': 'new_m', 'new_m_s5_b_re': 'new_m', 'new_m_s5_b_im': 'new_m', 'new_m_s5_c_re': 'new_m', 'new_m_s5_c_im': 'new_m', 'new_m_s5_d': 'new_m', 'new_m_s5_glu_val': 'new_m', 'new_m_s5_glu_gate': 'new_m', 'new_m_hg_lower_bound': 'new_m', 'new_m_hg_out_norm': 'new_m', 'new_m_hg_w_out': 'new_m', 'new_m_w_merge_out': 'new_m', 'new_m_ffn2_norm': 'new_m', 'new_m_ffn2_w_gate': 'new_m', 'new_m_ffn2_w_up': 'new_m', 'new_m_ffn2_w_down': 'new_m', 'new_m_ple_norm': 'new_m', 'new_m_ple_w_gate': 'new_m', 'new_m_ple_w_proj': 'new_m', 'new_m_final_norm': 'new_m', 'new_v_ffn1_norm': 'new_v', 'new_v_ffn1_w_gate': 'new_v', 'new_v_ffn1_w_up': 'new_v', 'new_v_ffn1_w_down': 'new_v', 'new_v_mix_norm': 'new_v', 'new_v_w_in': 'new_v', 'new_v_s5_lam_re': 'new_v', 'new_v_s5_lam_im': 'new_v', 'new_v_s5_log_dt': 'new_v', 'new_v_s5_b_re': 'new_v', 'new_v_s5_b_im': 'new_v', 'new_v_s5_c_re': 'new_v', 'new_v_s5_c_im': 'new_v', 'new_v_s5_d': 'new_v', 'new_v_s5_glu_val': 'new_v', 'new_v_s5_glu_gate': 'new_v', 'new_v_hg_lower_bound': 'new_v', 'new_v_hg_out_norm': 'new_v', 'new_v_hg_w_out': 'new_v', 'new_v_w_merge_out': 'new_v', 'new_v_ffn2_norm': 'new_v', 'new_v_ffn2_w_gate': 'new_v', 'new_v_ffn2_w_up': 'new_v', 'new_v_ffn2_w_down': 'new_v', 'new_v_ple_norm': 'new_v', 'new_v_ple_w_gate': 'new_v', 'new_v_ple_w_proj': 'new_v', 'new_v_final_norm': 'new_v'}


def _forward(args):
    return _fwd_reference(*[args[k] for k in FWD_PARAMS])


def _output_shape():
    def fwd():
        inp = _fwd_setup_inputs(0)
        return _fwd_reference(*[inp[k] for k in FWD_PARAMS])
    out = _jax.eval_shape(fwd)
    return out.shape, out.dtype

N_MICROBATCH = 1
ADAM_LR = 0.001
ADAM_B1 = 0.9
ADAM_B2 = 0.999
ADAM_EPS = 1e-08
ADAM_WD = 0.01
ADAM_STEP = 10
PER_EXAMPLE_BATCH_AXIS = {'x': 0, 'p': 1, 'loss_target': 0}
SHARED_INPUTS = []
_WEIGHT_DTYPES = {'ffn1_norm': _jnp.float32, 'ffn1_w_gate': _jnp.float32, 'ffn1_w_up': _jnp.float32, 'ffn1_w_down': _jnp.float32, 'mix_norm': _jnp.float32, 'w_in': _jnp.float32, 's5_lam_re': _jnp.float32, 's5_lam_im': _jnp.float32, 's5_log_dt': _jnp.float32, 's5_b_re': _jnp.float32, 's5_b_im': _jnp.float32, 's5_c_re': _jnp.float32, 's5_c_im': _jnp.float32, 's5_d': _jnp.float32, 's5_glu_val': _jnp.float32, 's5_glu_gate': _jnp.float32, 'hg_lower_bound': _jnp.float32, 'hg_out_norm': _jnp.float32, 'hg_w_out': _jnp.float32, 'w_merge_out': _jnp.float32, 'ffn2_norm': _jnp.float32, 'ffn2_w_gate': _jnp.float32, 'ffn2_w_up': _jnp.float32, 'ffn2_w_down': _jnp.float32, 'ple_norm': _jnp.float32, 'ple_w_gate': _jnp.float32, 'ple_w_proj': _jnp.float32, 'final_norm': _jnp.float32}
MOMENT_SCALE = {'ffn1_norm': 8.201344e-02, 'ffn1_w_gate': 3.460941e-02, 'ffn1_w_up': 3.358261e-02, 'ffn1_w_down': 5.561633e-02, 'mix_norm': 1.166666e-01, 'w_in': 4.467450e-02, 's5_lam_re': 1.275040e-02, 's5_lam_im': 1.042065e-02, 's5_log_dt': 1.365944e+01, 's5_b_re': 7.743928e-03, 's5_b_im': 7.643379e-03, 's5_c_re': 2.747153e-03, 's5_c_im': 2.683486e-03, 's5_d': 4.521472e-02, 's5_glu_val': 3.158428e-02, 's5_glu_gate': 9.775049e-03, 'hg_lower_bound': 3.216799e-02, 'hg_out_norm': 4.882607e-02, 'hg_w_out': 4.911549e-02, 'w_merge_out': 5.797764e-02, 'ffn2_norm': 6.047825e-02, 'ffn2_w_gate': 2.625762e-02, 'ffn2_w_up': 2.543574e-02, 'ffn2_w_down': 4.219894e-02, 'ple_norm': 2.989916e-02, 'ple_w_gate': 2.950201e-02, 'ple_w_proj': 7.544634e-02, 'final_norm': 3.199388e+01}


def _to_microbatches(a, axis):
    t = _jnp.moveaxis(a, axis, 0)
    t = t.reshape((N_MICROBATCH, t.shape[0] // N_MICROBATCH) + t.shape[1:])
    return _jnp.moveaxis(t, 1, axis + 1)


def setup_inputs(seed: int = 0) -> dict:
    inp = _fwd_setup_inputs(seed)
    key = _jax.random.fold_in(_jax.random.key(seed), 7919)
    shape, _ = _output_shape()
    out = dict(inp)
    out["loss_target"] = _jax.random.normal(_jax.random.fold_in(key, 0), shape, _jnp.float32)
    for i, name in enumerate(TWIN_WEIGHTS):
        w = inp[name].astype(_jnp.float32)
        if MOMENT_SCALE is None:
            s = _jnp.sqrt(_jnp.mean(_jnp.square(w)) + 1e-30)
        else:
            s = MOMENT_SCALE[name]
        km, kv = _jax.random.split(_jax.random.fold_in(key, i + 1))
        out[name] = w
        out["m_" + name] = s * _jax.random.normal(km, w.shape, _jnp.float32)
        out["v_" + name] = (s * s) * _jax.random.uniform(kv, w.shape, _jnp.float32, 0.5, 1.5)
    if N_MICROBATCH > 1:
        for name, axis in PER_EXAMPLE_BATCH_AXIS.items():
            out[name] = _to_microbatches(out[name], axis)
    return {'x': out['x'], 'p': out['p'], 'ffn1_norm': out['ffn1_norm'], 'ffn1_w_gate': out['ffn1_w_gate'], 'ffn1_w_up': out['ffn1_w_up'], 'ffn1_w_down': out['ffn1_w_down'], 'mix_norm': out['mix_norm'], 'w_in': out['w_in'], 's5_lam_re': out['s5_lam_re'], 's5_lam_im': out['s5_lam_im'], 's5_log_dt': out['s5_log_dt'], 's5_b_re': out['s5_b_re'], 's5_b_im': out['s5_b_im'], 's5_c_re': out['s5_c_re'], 's5_c_im': out['s5_c_im'], 's5_d': out['s5_d'], 's5_glu_val': out['s5_glu_val'], 's5_glu_gate': out['s5_glu_gate'], 'hg_lower_bound': out['hg_lower_bound'], 'hg_out_norm': out['hg_out_norm'], 'hg_w_out': out['hg_w_out'], 'w_merge_out': out['w_merge_out'], 'ffn2_norm': out['ffn2_norm'], 'ffn2_w_gate': out['ffn2_w_gate'], 'ffn2_w_up': out['ffn2_w_up'], 'ffn2_w_down': out['ffn2_w_down'], 'ple_norm': out['ple_norm'], 'ple_w_gate': out['ple_w_gate'], 'ple_w_proj': out['ple_w_proj'], 'final_norm': out['final_norm'], 'loss_target': out['loss_target'], 'm_ffn1_norm': out['m_ffn1_norm'], 'm_ffn1_w_gate': out['m_ffn1_w_gate'], 'm_ffn1_w_up': out['m_ffn1_w_up'], 'm_ffn1_w_down': out['m_ffn1_w_down'], 'm_mix_norm': out['m_mix_norm'], 'm_w_in': out['m_w_in'], 'm_s5_lam_re': out['m_s5_lam_re'], 'm_s5_lam_im': out['m_s5_lam_im'], 'm_s5_log_dt': out['m_s5_log_dt'], 'm_s5_b_re': out['m_s5_b_re'], 'm_s5_b_im': out['m_s5_b_im'], 'm_s5_c_re': out['m_s5_c_re'], 'm_s5_c_im': out['m_s5_c_im'], 'm_s5_d': out['m_s5_d'], 'm_s5_glu_val': out['m_s5_glu_val'], 'm_s5_glu_gate': out['m_s5_glu_gate'], 'm_hg_lower_bound': out['m_hg_lower_bound'], 'm_hg_out_norm': out['m_hg_out_norm'], 'm_hg_w_out': out['m_hg_w_out'], 'm_w_merge_out': out['m_w_merge_out'], 'm_ffn2_norm': out['m_ffn2_norm'], 'm_ffn2_w_gate': out['m_ffn2_w_gate'], 'm_ffn2_w_up': out['m_ffn2_w_up'], 'm_ffn2_w_down': out['m_ffn2_w_down'], 'm_ple_norm': out['m_ple_norm'], 'm_ple_w_gate': out['m_ple_w_gate'], 'm_ple_w_proj': out['m_ple_w_proj'], 'm_final_norm': out['m_final_norm'], 'v_ffn1_norm': out['v_ffn1_norm'], 'v_ffn1_w_gate': out['v_ffn1_w_gate'], 'v_ffn1_w_up': out['v_ffn1_w_up'], 'v_ffn1_w_down': out['v_ffn1_w_down'], 'v_mix_norm': out['v_mix_norm'], 'v_w_in': out['v_w_in'], 'v_s5_lam_re': out['v_s5_lam_re'], 'v_s5_lam_im': out['v_s5_lam_im'], 'v_s5_log_dt': out['v_s5_log_dt'], 'v_s5_b_re': out['v_s5_b_re'], 'v_s5_b_im': out['v_s5_b_im'], 'v_s5_c_re': out['v_s5_c_re'], 'v_s5_c_im': out['v_s5_c_im'], 'v_s5_d': out['v_s5_d'], 'v_s5_glu_val': out['v_s5_glu_val'], 'v_s5_glu_gate': out['v_s5_glu_gate'], 'v_hg_lower_bound': out['v_hg_lower_bound'], 'v_hg_out_norm': out['v_hg_out_norm'], 'v_hg_w_out': out['v_hg_w_out'], 'v_w_merge_out': out['v_w_merge_out'], 'v_ffn2_norm': out['v_ffn2_norm'], 'v_ffn2_w_gate': out['v_ffn2_w_gate'], 'v_ffn2_w_up': out['v_ffn2_w_up'], 'v_ffn2_w_down': out['v_ffn2_w_down'], 'v_ple_norm': out['v_ple_norm'], 'v_ple_w_gate': out['v_ple_w_gate'], 'v_ple_w_proj': out['v_ple_w_proj'], 'v_final_norm': out['v_final_norm']}


def _loss(weights, diff, rest, loss_target):
    with _jax.named_scope("forward"):
        args = {**rest, TWIN_DIFF_INPUT: diff, **{k: w.astype(_WEIGHT_DTYPES[k]) for k, w in weights.items()}}
        y = _forward(args)
    with _jax.named_scope("loss_head"):
        err = _jnp.square(y.astype(_jnp.float32) - loss_target)
        return 0.5 * _jnp.sum(_jnp.mean(err, axis=-1)) if err.ndim else 0.5 * err


def _adamw(w, g, m, v):
    m = ADAM_B1 * m + (1.0 - ADAM_B1) * g
    v = ADAM_B2 * v + (1.0 - ADAM_B2) * _jnp.square(g)
    m_hat = m / (1.0 - ADAM_B1 ** ADAM_STEP)
    v_hat = v / (1.0 - ADAM_B2 ** ADAM_STEP)
    delta = -ADAM_LR * (m_hat / (_jnp.sqrt(v_hat) + ADAM_EPS) + ADAM_WD * w)
    return delta, m, v


def reference(x, p, ffn1_norm, ffn1_w_gate, ffn1_w_up, ffn1_w_down, mix_norm, w_in, s5_lam_re, s5_lam_im, s5_log_dt, s5_b_re, s5_b_im, s5_c_re, s5_c_im, s5_d, s5_glu_val, s5_glu_gate, hg_lower_bound, hg_out_norm, hg_w_out, w_merge_out, ffn2_norm, ffn2_w_gate, ffn2_w_up, ffn2_w_down, ple_norm, ple_w_gate, ple_w_proj, final_norm, loss_target, m_ffn1_norm, m_ffn1_w_gate, m_ffn1_w_up, m_ffn1_w_down, m_mix_norm, m_w_in, m_s5_lam_re, m_s5_lam_im, m_s5_log_dt, m_s5_b_re, m_s5_b_im, m_s5_c_re, m_s5_c_im, m_s5_d, m_s5_glu_val, m_s5_glu_gate, m_hg_lower_bound, m_hg_out_norm, m_hg_w_out, m_w_merge_out, m_ffn2_norm, m_ffn2_w_gate, m_ffn2_w_up, m_ffn2_w_down, m_ple_norm, m_ple_w_gate, m_ple_w_proj, m_final_norm, v_ffn1_norm, v_ffn1_w_gate, v_ffn1_w_up, v_ffn1_w_down, v_mix_norm, v_w_in, v_s5_lam_re, v_s5_lam_im, v_s5_log_dt, v_s5_b_re, v_s5_b_im, v_s5_c_re, v_s5_c_im, v_s5_d, v_s5_glu_val, v_s5_glu_gate, v_hg_lower_bound, v_hg_out_norm, v_hg_w_out, v_w_merge_out, v_ffn2_norm, v_ffn2_w_gate, v_ffn2_w_up, v_ffn2_w_down, v_ple_norm, v_ple_w_gate, v_ple_w_proj, v_final_norm):
    given = dict(x=x, p=p, ffn1_norm=ffn1_norm, ffn1_w_gate=ffn1_w_gate, ffn1_w_up=ffn1_w_up, ffn1_w_down=ffn1_w_down, mix_norm=mix_norm, w_in=w_in, s5_lam_re=s5_lam_re, s5_lam_im=s5_lam_im, s5_log_dt=s5_log_dt, s5_b_re=s5_b_re, s5_b_im=s5_b_im, s5_c_re=s5_c_re, s5_c_im=s5_c_im, s5_d=s5_d, s5_glu_val=s5_glu_val, s5_glu_gate=s5_glu_gate, hg_lower_bound=hg_lower_bound, hg_out_norm=hg_out_norm, hg_w_out=hg_w_out, w_merge_out=w_merge_out, ffn2_norm=ffn2_norm, ffn2_w_gate=ffn2_w_gate, ffn2_w_up=ffn2_w_up, ffn2_w_down=ffn2_w_down, ple_norm=ple_norm, ple_w_gate=ple_w_gate, ple_w_proj=ple_w_proj, final_norm=final_norm, loss_target=loss_target, m_ffn1_norm=m_ffn1_norm, m_ffn1_w_gate=m_ffn1_w_gate, m_ffn1_w_up=m_ffn1_w_up, m_ffn1_w_down=m_ffn1_w_down, m_mix_norm=m_mix_norm, m_w_in=m_w_in, m_s5_lam_re=m_s5_lam_re, m_s5_lam_im=m_s5_lam_im, m_s5_log_dt=m_s5_log_dt, m_s5_b_re=m_s5_b_re, m_s5_b_im=m_s5_b_im, m_s5_c_re=m_s5_c_re, m_s5_c_im=m_s5_c_im, m_s5_d=m_s5_d, m_s5_glu_val=m_s5_glu_val, m_s5_glu_gate=m_s5_glu_gate, m_hg_lower_bound=m_hg_lower_bound, m_hg_out_norm=m_hg_out_norm, m_hg_w_out=m_hg_w_out, m_w_merge_out=m_w_merge_out, m_ffn2_norm=m_ffn2_norm, m_ffn2_w_gate=m_ffn2_w_gate, m_ffn2_w_up=m_ffn2_w_up, m_ffn2_w_down=m_ffn2_w_down, m_ple_norm=m_ple_norm, m_ple_w_gate=m_ple_w_gate, m_ple_w_proj=m_ple_w_proj, m_final_norm=m_final_norm, v_ffn1_norm=v_ffn1_norm, v_ffn1_w_gate=v_ffn1_w_gate, v_ffn1_w_up=v_ffn1_w_up, v_ffn1_w_down=v_ffn1_w_down, v_mix_norm=v_mix_norm, v_w_in=v_w_in, v_s5_lam_re=v_s5_lam_re, v_s5_lam_im=v_s5_lam_im, v_s5_log_dt=v_s5_log_dt, v_s5_b_re=v_s5_b_re, v_s5_b_im=v_s5_b_im, v_s5_c_re=v_s5_c_re, v_s5_c_im=v_s5_c_im, v_s5_d=v_s5_d, v_s5_glu_val=v_s5_glu_val, v_s5_glu_gate=v_s5_glu_gate, v_hg_lower_bound=v_hg_lower_bound, v_hg_out_norm=v_hg_out_norm, v_hg_w_out=v_hg_w_out, v_w_merge_out=v_w_merge_out, v_ffn2_norm=v_ffn2_norm, v_ffn2_w_gate=v_ffn2_w_gate, v_ffn2_w_up=v_ffn2_w_up, v_ffn2_w_down=v_ffn2_w_down, v_ple_norm=v_ple_norm, v_ple_w_gate=v_ple_w_gate, v_ple_w_proj=v_ple_w_proj, v_final_norm=v_final_norm)
    weights = {n: given[n] for n in TWIN_WEIGHTS}
    shared = {n: given[n] for n in SHARED_INPUTS}
    per_example = {n: given[n] for n in ['x', 'p']}
    grad_fn = _jax.value_and_grad(_loss, argnums=(0, 1))

    def one_microbatch(ex, loss_target):
        ex = dict(ex)
        diff = ex.pop(TWIN_DIFF_INPUT)
        return grad_fn(weights, diff, {**shared, **ex}, loss_target)

    if N_MICROBATCH == 1:
        loss, (grad_w, grad_x) = one_microbatch(per_example, given["loss_target"])
    else:
        def body(carry, xs):
            loss_sum, grad_sum = carry
            l_k, (gw_k, gx_k) = one_microbatch(xs[0], xs[1])
            with _jax.named_scope("update"):
                return (loss_sum + l_k, _jax.tree.map(_jnp.add, grad_sum, gw_k)), gx_k

        init = (_jnp.zeros((), _jnp.float32), _jax.tree.map(_jnp.zeros_like, weights))
        (loss, grad_w), grad_x = _jax.lax.scan(body, init, (per_example, given["loss_target"]))
    with _jax.named_scope("update"):
        delta_w, new_m, new_v = {}, {}, {}
        for n in TWIN_WEIGHTS:
            delta_w[n], new_m[n], new_v[n] = _adamw(weights[n], grad_w[n], given["m_" + n], given["v_" + n])
    return (loss, grad_x, *[grad_w[n] for n in TWIN_WEIGHTS], *[delta_w[n] for n in TWIN_WEIGHTS],
            *[new_m[n] for n in TWIN_WEIGHTS], *[new_v[n] for n in TWIN_WEIGHTS])
```

```python
import math

import jax
import jax.numpy as jnp
from jax import lax
from jax.experimental import pallas as pl
from jax.experimental.pallas import tpu as pltpu

F32 = jnp.float32
BF16 = jnp.bfloat16

D_MODEL = 1024
D_FF = 2816
N_SHARD = 4
FF_SHARD = D_FF // N_SHARD
FF_PAD = 768
NORM_EPS = 1e-6
PLE_DIM = 256

S5_WIDTH = 512
S5_GROUPS = 32
S5_GROUP = 16
S5_STATE = 64
S5_N = S5_GROUPS * S5_STATE
S5_KT = 2

HG_HEADS = 8
HG_E = 128
HG_WIDTH = 1024
CHUNK = 64
IN_COLS = S5_WIDTH + 4 * HG_WIDTH + 2 * D_MODEL
IN_SPLITS = (0, 512, 1536, 2560, 3584, 4608, 5632, 6656)

ADAM_LR = 0.001
ADAM_B1 = 0.9
ADAM_B2 = 0.999
ADAM_EPS = 1e-08
ADAM_WD = 0.01
ADAM_STEP = 10

VMEM_LIMIT = 60 * 1024 * 1024
HIGHEST = lax.Precision.HIGHEST


def _cparams(sem=None, **kw):
    return pltpu.CompilerParams(dimension_semantics=sem, vmem_limit_bytes=VMEM_LIMIT, **kw)


def _const_spec(shape):
    nd = len(shape)
    return pl.BlockSpec(shape, lambda *_: (0,) * nd, pipeline_mode=pl.Buffered(1))


def _dot(a, b):
    return jnp.dot(a, b, preferred_element_type=F32)


def _dot_nt(a, b):
    return lax.dot_general(a, b, (((1,), (1,)), ((), ())), preferred_element_type=F32)


def _dot_tn(a, b):
    return lax.dot_general(a, b, (((0,), (0,)), ((), ())), preferred_element_type=F32)


def _sigmoid(x):
    return 1.0 / (1.0 + jnp.exp(-x))


def _rms_fwd(x, g):
    r = lax.rsqrt(jnp.mean(x * x, axis=-1, keepdims=True) + NORM_EPS)
    return x * r * g, r


def _rms_bwd(x, r, g, dy):
    xh = x * r
    dyg = dy * g
    m = jnp.mean(dyg * xh, axis=-1, keepdims=True)
    return r * (dyg - xh * m), jnp.sum(dy * xh, axis=0, keepdims=True)


def _accum(ref, val, first):
    @pl.when(first)
    def _():
        ref[...] = val

    @pl.when(jnp.logical_not(first))
    def _():
        ref[...] += val


def ffn_fwd(h, gain, wg, wu, wd, name, tm=256):
    t = h.shape[0]

    def body(h_ref, g_ref, wg_ref, wu_ref, wd_ref, o_ref, a_ref, b_ref):
        hv = h_ref[...]
        n, _ = _rms_fwd(hv, g_ref[...])
        nb = n.astype(BF16)
        acc = jnp.zeros((tm, D_MODEL), F32)
        for s in range(N_SHARD):
            a = _dot(nb, wg_ref[s])
            b = _dot(nb, wu_ref[s])
            a_ref[s] = a.astype(BF16)
            b_ref[s] = b.astype(BF16)
            sv = (a * _sigmoid(a) * b).astype(BF16)
            acc = acc + _dot(sv, wd_ref[s])
        o_ref[...] = hv + 0.5 * acc

    return pl.pallas_call(
        body, name=name, grid=(t // tm,),
        out_shape=(jax.ShapeDtypeStruct((t, D_MODEL), F32),
                   jax.ShapeDtypeStruct((N_SHARD, t, FF_PAD), BF16),
                   jax.ShapeDtypeStruct((N_SHARD, t, FF_PAD), BF16)),
        in_specs=[pl.BlockSpec((tm, D_MODEL), lambda i: (i, 0)),
                  _const_spec((1, D_MODEL)),
                  _const_spec((N_SHARD, D_MODEL, FF_PAD)),
                  _const_spec((N_SHARD, D_MODEL, FF_PAD)),
                  _const_spec((N_SHARD, FF_PAD, D_MODEL))],
        out_specs=(pl.BlockSpec((tm, D_MODEL), lambda i: (i, 0)),
                   pl.BlockSpec((N_SHARD, tm, FF_PAD), lambda i: (0, i, 0)),
                   pl.BlockSpec((N_SHARD, tm, FF_PAD), lambda i: (0, i, 0))),
        compiler_params=_cparams(("arbitrary",)),
    )(h, gain, wg, wu, wd)


def ffn_bwd(dho, h, a, b, gain, wg, wu, wd, name, tm=256):
    t = h.shape[0]

    def body(dho_ref, h_ref, a_ref, b_ref, g_ref, wg_ref, wu_ref, wd_ref,
             dh_ref, dg_ref, nb_ref, dhb_ref, da_ref, db_ref, s_ref):
        hv = h_ref[...]
        g = g_ref[...]
        n, r = _rms_fwd(hv, g)
        nb_ref[...] = n.astype(BF16)
        dhalf = (0.5 * dho_ref[...]).astype(BF16)
        dhb_ref[...] = dhalf
        dn = jnp.zeros((tm, D_MODEL), F32)
        for s in range(N_SHARD):
            av = a_ref[s].astype(F32)
            bv = b_ref[s].astype(F32)
            sg = _sigmoid(av)
            sil = av * sg
            s_ref[s] = (sil * bv).astype(BF16)
            ds = _dot_nt(dhalf, wd_ref[s])
            da = (ds * bv * (sg * (1.0 + av * (1.0 - sg)))).astype(BF16)
            db = (ds * sil).astype(BF16)
            da_ref[s] = da
            db_ref[s] = db
            dn = dn + _dot_nt(da, wg_ref[s]) + _dot_nt(db, wu_ref[s])
        dx, dg = _rms_bwd(hv, r, g, dn)
        dh_ref[...] = dho_ref[...] + dx
        _accum(dg_ref, dg, pl.program_id(0) == 0)

    tok = pl.BlockSpec((tm, D_MODEL), lambda i: (i, 0))
    hid = pl.BlockSpec((N_SHARD, tm, FF_PAD), lambda i: (0, i, 0))
    return pl.pallas_call(
        body, name=name, grid=(t // tm,),
        out_shape=(jax.ShapeDtypeStruct((t, D_MODEL), F32),
                   jax.ShapeDtypeStruct((1, D_MODEL), F32),
                   jax.ShapeDtypeStruct((t, D_MODEL), BF16),
                   jax.ShapeDtypeStruct((t, D_MODEL), BF16),
                   jax.ShapeDtypeStruct((N_SHARD, t, FF_PAD), BF16),
                   jax.ShapeDtypeStruct((N_SHARD, t, FF_PAD), BF16),
                   jax.ShapeDtypeStruct((N_SHARD, t, FF_PAD), BF16)),
        in_specs=[tok, tok, hid, hid, _const_spec((1, D_MODEL)),
                  _const_spec((N_SHARD, D_MODEL, FF_PAD)),
                  _const_spec((N_SHARD, D_MODEL, FF_PAD)),
                  _const_spec((N_SHARD, FF_PAD, D_MODEL))],
        out_specs=(tok, pl.BlockSpec((1, D_MODEL), lambda i: (0, 0)), tok, tok, hid, hid, hid),
        compiler_params=_cparams(("arbitrary",)),
    )(dho, h, a, b, gain, wg, wu, wd)


def tn_matmul(x, y, name, shard, tk=512):
    x3, y3 = x.ndim == 3, y.ndim == 3
    t = x.shape[-2]
    m = x.shape[-1] // (N_SHARD if (shard == "rows" and not x3) else 1)
    n = y.shape[-1] // (N_SHARD if (shard == "cols" and not y3) else 1)
    nk = t // tk

    def body(x_ref, y_ref, o_ref):
        _accum(o_ref, _dot_tn(x_ref[...].astype(BF16), y_ref[...].astype(BF16)), pl.program_id(1) == 0)

    if x3:
        x_spec = pl.BlockSpec((None, tk, m), lambda s, k: (s, k, 0))
    elif shard == "rows":
        x_spec = pl.BlockSpec((tk, m), lambda s, k: (k, s))
    else:
        x_spec = pl.BlockSpec((tk, m), lambda s, k: (k, 0))
    if y3:
        y_spec = pl.BlockSpec((None, tk, n), lambda s, k: (s, k, 0))
    elif shard == "cols":
        y_spec = pl.BlockSpec((tk, n), lambda s, k: (k, s))
    else:
        y_spec = pl.BlockSpec((tk, n), lambda s, k: (k, 0))
    return pl.pallas_call(
        body, name=name, grid=(N_SHARD, nk),
        out_shape=jax.ShapeDtypeStruct((N_SHARD, m, n), F32),
        in_specs=[x_spec, y_spec],
        out_specs=pl.BlockSpec((None, m, n), lambda s, k: (s, 0, 0)),
        compiler_params=_cparams(("arbitrary", "arbitrary")),
    )(x, y)


def inproj_fwd(h, gain, w_in, tm=256):
    t = h.shape[0]
    widths = [IN_SPLITS[j + 1] - IN_SPLITS[j] for j in range(7)]
    sh_cols = IN_COLS // N_SHARD

    def body(h_ref, g_ref, w_ref, *outs):
        n, _ = _rms_fwd(h_ref[...], g_ref[...])
        nb = n.astype(BF16)
        proj = jnp.concatenate([_dot(nb, w_ref[s]) for s in range(N_SHARD)], axis=1)
        for j, o_ref in enumerate(outs):
            o_ref[...] = proj[:, IN_SPLITS[j]:IN_SPLITS[j + 1]]

    return pl.pallas_call(
        body, name="inproj_fwd", grid=(t // tm,),
        out_shape=tuple(jax.ShapeDtypeStruct((t, w), F32) for w in widths),
        in_specs=[pl.BlockSpec((tm, D_MODEL), lambda i: (i, 0)),
                  _const_spec((1, D_MODEL)),
                  _const_spec((N_SHARD, D_MODEL, sh_cols))],
        out_specs=tuple(pl.BlockSpec((tm, w), lambda i: (i, 0)) for w in widths),
        compiler_params=_cparams(("arbitrary",)),
    )(h, gain, w_in)


def inproj_bwd(dres, h, gain, w_in, dparts, tm=256):
    t = h.shape[0]
    widths = [IN_SPLITS[j + 1] - IN_SPLITS[j] for j in range(7)]
    sh_cols = IN_COLS // N_SHARD

    def body(dres_ref, h_ref, g_ref, w_ref, d0, d1, d2, d3, d4, d5, d6, dh_ref, dg_ref, nb_ref, dp_ref):
        hv = h_ref[...]
        g = g_ref[...]
        n, r = _rms_fwd(hv, g)
        nb_ref[...] = n.astype(BF16)
        dproj = jnp.concatenate([d[...] for d in (d0, d1, d2, d3, d4, d5, d6)], axis=1).astype(BF16)
        dp_ref[...] = dproj
        dn = jnp.zeros((tm, D_MODEL), F32)
        for s in range(N_SHARD):
            dn = dn + _dot_nt(dproj[:, s * sh_cols:(s + 1) * sh_cols], w_ref[s])
        dx, dg = _rms_bwd(hv, r, g, dn)
        dh_ref[...] = dres_ref[...] + dx
        _accum(dg_ref, dg, pl.program_id(0) == 0)

    tok = pl.BlockSpec((tm, D_MODEL), lambda i: (i, 0))
    return pl.pallas_call(
        body, name="inproj_bwd", grid=(t // tm,),
        out_shape=(jax.ShapeDtypeStruct((t, D_MODEL), F32),
                   jax.ShapeDtypeStruct((1, D_MODEL), F32),
                   jax.ShapeDtypeStruct((t, D_MODEL), BF16),
                   jax.ShapeDtypeStruct((t, IN_COLS), BF16)),
        in_specs=[tok, tok, _const_spec((1, D_MODEL)), _const_spec((N_SHARD, D_MODEL, sh_cols))]
                 + [pl.BlockSpec((tm, w), lambda i: (i, 0)) for w in widths],
        out_specs=(tok, pl.BlockSpec((1, D_MODEL), lambda i: (0, 0)), tok,
                   pl.BlockSpec((tm, IN_COLS), lambda i: (i, 0))),
        compiler_params=_cparams(("arbitrary",)),
    )(dres, h, gain, w_in, *dparts)


def s5_prep(lam_re, lam_im, log_dt, b_re, b_im, c_re, c_im):
    dt = jnp.exp(log_dt)[:, None]
    mag = jnp.exp(lam_re * dt)
    lbr = mag * jnp.cos(lam_im * dt)
    lbi = mag * jnp.sin(lam_im * dt)
    den = lam_re * lam_re + lam_im * lam_im
    nr, ni = lbr - 1.0, lbi
    kr = (nr * lam_re + ni * lam_im) / den
    ki = (ni * lam_re - nr * lam_im) / den
    bbr = kr[..., None] * b_re - ki[..., None] * b_im
    bbi = kr[..., None] * b_im + ki[..., None] * b_re
    eye = jnp.eye(16, dtype=F32)

    def bm(bp):
        return jnp.einsum('kgph,gG->kghGp', bp.reshape(S5_KT, 16, S5_STATE, S5_GROUP), eye).reshape(S5_KT, 256, 1024)

    def cm(cp):
        return jnp.einsum('kghp,gG->kgpGh', cp.reshape(S5_KT, 16, S5_GROUP, S5_STATE), eye).reshape(S5_KT, 1024, 256)

    lam_bar = jnp.stack([lbr.reshape(S5_N), lbi.reshape(S5_N)])
    bmat = jnp.stack([bm(bbr), bm(bbi)])
    cmat = jnp.stack([cm(c_re), -cm(c_im)])
    return lam_bar, bmat, cmat


def _lam_powers(lam_bar):
    lr, li = lam_bar[0], lam_bar[1]
    pr, pi = [lr], [li]
    for _ in range(7):
        pr, pi = pr + [pr[-1] * lr - pi[-1] * li], pi + [pr[-1] * li + pi[-1] * lr]
    return jnp.stack(pr), jnp.stack(pi)


def s5_fwd(u, pw_r, pw_i, bmat, cmat, dvec, tm=256):
    t = u.shape[0]
    nch = tm // 8

    def body(u_ref, pwr_ref, pwi_ref, b_ref, c_ref, d_ref, y_ref, xp_ref, x_scr, carry):
        @pl.when(pl.program_id(0) == 0)
        def _():
            carry[...] = jnp.zeros_like(carry)

        uv = u_ref[...]
        ub = uv.astype(BF16)
        for part in range(2):
            for kt in range(S5_KT):
                x_scr[:, pl.ds(part * S5_N + kt * 1024, 1024)] = _dot(ub[:, kt * 256:(kt + 1) * 256], b_ref[part, kt])
        row = lax.broadcasted_iota(jnp.int32, (8, S5_N), 0)
        pwr = pwr_ref[...]
        pwi = pwi_ref[...]

        def chunk(i, c):
            cr, ci = c
            r0 = pl.multiple_of(i * 8, 8)
            xr = x_scr[pl.ds(r0, 8), pl.ds(0, S5_N)]
            xi = x_scr[pl.ds(r0, 8), pl.ds(S5_N, S5_N)]
            for sh, idx in ((1, 0), (2, 1), (4, 3)):
                sr = jnp.where(row < sh, 0.0, pltpu.roll(xr, sh, 0))
                si = jnp.where(row < sh, 0.0, pltpu.roll(xi, sh, 0))
                lr = pwr[idx:idx + 1, :]
                li = pwi[idx:idx + 1, :]
                xr, xi = xr + lr * sr - li * si, xi + lr * si + li * sr
            xr, xi = xr + pwr * cr - pwi * ci, xi + pwr * ci + pwi * cr
            x_scr[pl.ds(r0, 8), pl.ds(0, S5_N)] = xr
            x_scr[pl.ds(r0, 8), pl.ds(S5_N, S5_N)] = xi
            xp_ref[pl.ds(r0, 8), pl.ds(0, S5_N)] = jnp.where(row == 0, cr, pltpu.roll(xr, 1, 0))
            xp_ref[pl.ds(r0, 8), pl.ds(S5_N, S5_N)] = jnp.where(row == 0, ci, pltpu.roll(xi, 1, 0))
            return xr[7:8, :], xi[7:8, :]

        cr, ci = lax.fori_loop(0, nch, chunk, (carry[0:1, :], carry[1:2, :]))
        carry[0:1, :] = cr
        carry[1:2, :] = ci
        for kt in range(S5_KT):
            acc = jnp.zeros((tm, 256), F32)
            for part in range(2):
                acc = acc + _dot(x_scr[:, pl.ds(part * S5_N + kt * 1024, 1024)].astype(BF16), c_ref[part, kt])
            y_ref[:, pl.ds(kt * 256, 256)] = acc + d_ref[:, pl.ds(kt * 256, 256)] * uv[:, kt * 256:(kt + 1) * 256]

    return pl.pallas_call(
        body, name="s5_fwd", grid=(t // tm,),
        out_shape=(jax.ShapeDtypeStruct((t, S5_WIDTH), F32),
                   jax.ShapeDtypeStruct((t, 2 * S5_N), F32)),
        in_specs=[pl.BlockSpec((tm, S5_WIDTH), lambda i: (i, 0)),
                  _const_spec((8, S5_N)), _const_spec((8, S5_N)),
                  _const_spec((2, S5_KT, 256, 1024)), _const_spec((2, S5_KT, 1024, 256)),
                  _const_spec((1, S5_WIDTH))],
        out_specs=(pl.BlockSpec((tm, S5_WIDTH), lambda i: (i, 0)),
                   pl.BlockSpec((tm, 2 * S5_N), lambda i: (i, 0))),
        scratch_shapes=[pltpu.VMEM((tm, 2 * S5_N), F32), pltpu.VMEM((8, S5_N), F32)],
        compiler_params=_cparams(("arbitrary",)),
    )(u, pw_r, pw_i, bmat, cmat, dvec)


def s5_bwd(dy, u, xp, pw_r, pw_i, pwrev_r, pwrev_i, bmat, bmat_t, cmat_t, dvec, tm=256):
    t = u.shape[0]
    nt = t // tm
    nch = tm // 8

    def body(dy_ref, u_ref, xp_ref, pwr_ref, pwi_ref, pvr_ref, pvi_ref, b_ref, bt_ref, ct_ref, d_ref,
             du_ref, db_ref, dc_ref, dl_ref, dd_ref, g_scr, x_scr, carry):
        first = pl.program_id(0) == 0

        @pl.when(first)
        def _():
            carry[...] = jnp.zeros_like(carry)
            dl_ref[...] = jnp.zeros_like(dl_ref)

        dyv = dy_ref[...]
        uv = u_ref[...]
        dyb = dyv.astype(BF16)
        ub = uv.astype(BF16)
        lr1 = pwr_ref[0:1, :]
        li1 = pwi_ref[0:1, :]
        for kt in range(S5_KT):
            cols = pl.ds(kt * 1024, 1024)
            colsi = pl.ds(S5_N + kt * 1024, 1024)
            g_scr[:, cols] = _dot(dyb[:, kt * 256:(kt + 1) * 256], ct_ref[0, kt])
            g_scr[:, colsi] = _dot(dyb[:, kt * 256:(kt + 1) * 256], ct_ref[1, kt])
            bur = _dot(ub[:, kt * 256:(kt + 1) * 256], b_ref[0, kt])
            bui = _dot(ub[:, kt * 256:(kt + 1) * 256], b_ref[1, kt])
            xpr = xp_ref[:, cols]
            xpi = xp_ref[:, colsi]
            lrk = lr1[:, kt * 1024:(kt + 1) * 1024]
            lik = li1[:, kt * 1024:(kt + 1) * 1024]
            x_scr[:, cols] = lrk * xpr - lik * xpi + bur
            x_scr[:, colsi] = lrk * xpi + lik * xpr + bui

        row = lax.broadcasted_iota(jnp.int32, (8, S5_N), 0)
        pwr = pwr_ref[...]
        pwi = pwi_ref[...]
        pvr = pvr_ref[...]
        pvi = pvi_ref[...]

        def chunk(j, c):
            cr, ci = c
            r0 = pl.multiple_of((nch - 1 - j) * 8, 8)
            gr = g_scr[pl.ds(r0, 8), pl.ds(0, S5_N)]
            gi = g_scr[pl.ds(r0, 8), pl.ds(S5_N, S5_N)]
            for sh, idx in ((1, 0), (2, 1), (4, 3)):
                sr = jnp.where(row >= 8 - sh, 0.0, pltpu.roll(gr, 8 - sh, 0))
                si = jnp.where(row >= 8 - sh, 0.0, pltpu.roll(gi, 8 - sh, 0))
                lr = pwr[idx:idx + 1, :]
                li = pwi[idx:idx + 1, :]
                gr, gi = gr + lr * sr + li * si, gi + lr * si - li * sr
            gr, gi = gr + pvr * cr + pvi * ci, gi + pvr * ci - pvi * cr
            g_scr[pl.ds(r0, 8), pl.ds(0, S5_N)] = gr
            g_scr[pl.ds(r0, 8), pl.ds(S5_N, S5_N)] = gi
            xpr = xp_ref[pl.ds(r0, 8), pl.ds(0, S5_N)]
            xpi = xp_ref[pl.ds(r0, 8), pl.ds(S5_N, S5_N)]
            dl_ref[0] += gr * xpr + gi * xpi
            dl_ref[1] += gi * xpr - gr * xpi
            return gr[0:1, :], gi[0:1, :]

        cr, ci = lax.fori_loop(0, nch, chunk, (carry[0:1, :], carry[1:2, :]))
        carry[0:1, :] = cr
        carry[1:2, :] = ci

        for kt in range(S5_KT):
            du = jnp.zeros((tm, 256), F32)
            ukt = ub[:, kt * 256:(kt + 1) * 256]
            dykt = dyb[:, kt * 256:(kt + 1) * 256]
            for part in range(2):
                gb = g_scr[:, pl.ds(part * S5_N + kt * 1024, 1024)].astype(BF16)
                xb = x_scr[:, pl.ds(part * S5_N + kt * 1024, 1024)].astype(BF16)
                du = du + _dot(gb, bt_ref[part, kt])
                dbv = _dot_tn(ukt, gb)
                dcv = _dot_tn(xb, dykt)

                @pl.when(first)
                def _():
                    db_ref[part, kt] = dbv
                    dc_ref[part, kt] = dcv

                @pl.when(jnp.logical_not(first))
                def _():
                    db_ref[part, kt] += dbv
                    dc_ref[part, kt] += dcv
            du_ref[:, pl.ds(kt * 256, 256)] = du + d_ref[:, pl.ds(kt * 256, 256)] * dyv[:, kt * 256:(kt + 1) * 256]
        _accum(dd_ref, jnp.sum(dyv * uv, axis=0, keepdims=True), first)

    rev = lambda i: (nt - 1 - i, 0)
    return pl.pallas_call(
        body, name="s5_bwd", grid=(nt,),
        out_shape=(jax.ShapeDtypeStruct((t, S5_WIDTH), F32),
                   jax.ShapeDtypeStruct((2, S5_KT, 256, 1024), F32),
                   jax.ShapeDtypeStruct((2, S5_KT, 1024, 256), F32),
                   jax.ShapeDtypeStruct((2, 8, S5_N), F32),
                   jax.ShapeDtypeStruct((1, S5_WIDTH), F32)),
        in_specs=[pl.BlockSpec((tm, S5_WIDTH), rev), pl.BlockSpec((tm, S5_WIDTH), rev),
                  pl.BlockSpec((tm, 2 * S5_N), rev),
                  _const_spec((8, S5_N)), _const_spec((8, S5_N)), _const_spec((8, S5_N)), _const_spec((8, S5_N)),
                  _const_spec((2, S5_KT, 256, 1024)), _const_spec((2, S5_KT, 1024, 256)),
                  _const_spec((2, S5_KT, 256, 1024)), _const_spec((1, S5_WIDTH))],
        out_specs=(pl.BlockSpec((tm, S5_WIDTH), rev),
                   pl.BlockSpec((2, S5_KT, 256, 1024), lambda i: (0, 0, 0, 0)),
                   pl.BlockSpec((2, S5_KT, 1024, 256), lambda i: (0, 0, 0, 0)),
                   pl.BlockSpec((2, 8, S5_N), lambda i: (0, 0, 0)),
                   pl.BlockSpec((1, S5_WIDTH), lambda i: (0, 0))),
        scratch_shapes=[pltpu.VMEM((tm, 2 * S5_N), F32), pltpu.VMEM((tm, 2 * S5_N), F32),
                        pltpu.VMEM((8, S5_N), F32)],
        compiler_params=_cparams(("arbitrary",)),
    )(dy, u, xp, pw_r, pw_i, pwrev_r, pwrev_i, bmat, bmat_t, cmat_t, dvec)


def _hg_gates(z, lb):
    sg = _sigmoid(z)
    sgn = _sigmoid(-z)
    fg = lb + (1.0 - lb) * sg
    return sg, sgn, fg, jnp.log(fg), (1.0 - lb) * sgn


def _hg_decays(g, tril):
    gc = jnp.dot(tril, g, precision=HIGHEST, preferred_element_type=F32)
    mid = gc[CHUNK // 2 - 1:CHUNK // 2, :]
    last = gc[CHUNK - 1:CHUNK, :]
    return jnp.exp(gc), jnp.exp(gc - mid), jnp.exp(mid - gc), jnp.exp(last - gc), jnp.exp(last)


def _split_bf16(x):
    hi = x.astype(BF16)
    return hi, (x - hi.astype(F32)).astype(BF16)


def _hg_scores(qt, qlo, kt, klo, sl, causal):
    a = _dot_nt(qt[:, sl], kt[:, sl]) + _dot_nt(qt[:, sl], klo[:, sl]) + _dot_nt(qlo[:, sl], kt[:, sl])
    return jnp.where(causal, a, 0.0).astype(BF16)


def hgrn_fwd(q, f, v, lb):
    t = q.shape[0]
    nc = t // CHUNK
    scale = HG_E ** -0.5

    def body(q_ref, f_ref, v_ref, lb_ref, o_ref, st_ref, state):
        @pl.when(pl.program_id(0) == 0)
        def _():
            state[...] = jnp.zeros_like(state)

        ri = lax.broadcasted_iota(jnp.int32, (CHUNK, CHUNK), 0)
        ci = lax.broadcasted_iota(jnp.int32, (CHUNK, CHUNK), 1)
        causal = ri >= ci
        tril = causal.astype(F32)
        _, _, _, g, k = _hg_gates(f_ref[...], lb_ref[...])
        eg, eq, ek, ed, el = _hg_decays(g, tril)
        qs = q_ref[...] * scale
        qg = (qs * eg).astype(BF16)
        qt, qlo = _split_bf16(qs * eq)
        kt, klo = _split_bf16(k * ek)
        kd = (k * ed).astype(BF16)
        vb = v_ref[...].astype(BF16)
        for h in range(HG_HEADS):
            sl = slice(h * HG_E, (h + 1) * HG_E)
            st = state[h]
            a = _hg_scores(qt, qlo, kt, klo, sl, causal)
            o_ref[:, sl] = _dot(a, vb[:, sl]) + _dot_nt(qg[:, sl], st.astype(BF16))
            st_new = st * el[:, sl] + _dot_tn(vb[:, sl], kd[:, sl])
            state[h] = st_new
            st_ref[0, h] = st_new

    tok = pl.BlockSpec((CHUNK, HG_WIDTH), lambda i: (i, 0))
    return pl.pallas_call(
        body, name="hgrn_fwd", grid=(nc,),
        out_shape=(jax.ShapeDtypeStruct((t, HG_WIDTH), F32),
                   jax.ShapeDtypeStruct((nc, HG_HEADS, HG_E, HG_E), F32)),
        in_specs=[tok, tok, tok, _const_spec((1, HG_WIDTH))],
        out_specs=(tok, pl.BlockSpec((1, HG_HEADS, HG_E, HG_E), lambda i: (i, 0, 0, 0))),
        scratch_shapes=[pltpu.VMEM((HG_HEADS, HG_E, HG_E), F32)],
        compiler_params=_cparams(("arbitrary",)),
    )(q, f, v, lb)


def hgrn_bwd(do, q, f, v, lb, states):
    t = q.shape[0]
    nc = t // CHUNK
    scale = HG_E ** -0.5

    def body(do_ref, q_ref, f_ref, v_ref, lb_ref, s0_ref, dq_ref, df_ref, dv_ref, dlb_ref, dstate):
        first = pl.program_id(0) == 0
        c_idx = nc - 1 - pl.program_id(0)

        @pl.when(first)
        def _():
            dstate[...] = jnp.zeros_like(dstate)

        ri = lax.broadcasted_iota(jnp.int32, (CHUNK, CHUNK), 0)
        ci = lax.broadcasted_iota(jnp.int32, (CHUNK, CHUNK), 1)
        causal = ri >= ci
        tril = causal.astype(F32)
        triu = (ri <= ci).astype(F32)
        lb = lb_ref[...]
        sg, sgn, fg, g, k = _hg_gates(f_ref[...], lb)
        eg, eq, ek, ed, el = _hg_decays(g, tril)
        qs = q_ref[...] * scale
        qg = (qs * eg).astype(BF16)
        qt, qlo = _split_bf16(qs * eq)
        kt, klo = _split_bf16(k * ek)
        kd = (k * ed).astype(BF16)
        vb = v_ref[...].astype(BF16)
        dob = do_ref[...].astype(BF16)
        has_prev = jnp.where(c_idx > 0, 1.0, 0.0)
        dqs_l, dk_l, dgc_l, dgl_l = [], [], [], []
        for h in range(HG_HEADS):
            sl = slice(h * HG_E, (h + 1) * HG_E)
            s0 = s0_ref[0, h] * has_prev
            ds1 = dstate[h]
            ds1b = ds1.astype(BF16)
            a = _hg_scores(qt, qlo, kt, klo, sl, causal)
            da = jnp.where(causal, _dot_nt(dob[:, sl], vb[:, sl]), 0.0).astype(BF16)
            dv_ref[:, sl] = _dot_tn(a, dob[:, sl]) + _dot_nt(kd[:, sl], ds1b)
            dkd = _dot(vb[:, sl], ds1b)
            dqt = _dot(da, kt[:, sl])
            dkt = _dot_tn(da, qt[:, sl])
            dqg = _dot(dob[:, sl], s0.astype(BF16))
            dqs_l.append(dqt * eq[:, sl] + dqg * eg[:, sl])
            dk_l.append(dkt * ek[:, sl] + dkd * ed[:, sl])
            kd_dkd = kd[:, sl].astype(F32) * dkd
            dgc_l.append(qt[:, sl].astype(F32) * dqt - kt[:, sl].astype(F32) * dkt
                         + qg[:, sl].astype(F32) * dqg - kd_dkd)
            dgl_l.append(el[:, sl] * jnp.sum(ds1 * s0, axis=0, keepdims=True)
                         + jnp.sum(kd_dkd, axis=0, keepdims=True))
            dstate[h] = ds1 * el[:, sl] + _dot_tn(dob[:, sl], qg[:, sl])
        dqs = jnp.concatenate(dqs_l, axis=1)
        dk = jnp.concatenate(dk_l, axis=1)
        dgl = jnp.concatenate(dgl_l, axis=1)
        dq_ref[...] = dqs * scale
        rowc = lax.broadcasted_iota(jnp.int32, (CHUNK, HG_WIDTH), 0)
        dgc = jnp.concatenate(dgc_l, axis=1) + jnp.where(rowc == CHUNK - 1, dgl, 0.0)
        dg = jnp.dot(triu, dgc, precision=HIGHEST, preferred_element_type=F32)
        w = dg / fg - dk
        df_ref[...] = w * (1.0 - lb) * sg * sgn
        _accum(dlb_ref, jnp.sum(w * sgn, axis=0, keepdims=True), first)

    rev = lambda i: (nc - 1 - i, 0)
    tok = pl.BlockSpec((CHUNK, HG_WIDTH), rev)
    return pl.pallas_call(
        body, name="hgrn_bwd", grid=(nc,),
        out_shape=(jax.ShapeDtypeStruct((t, HG_WIDTH), F32),
                   jax.ShapeDtypeStruct((t, HG_WIDTH), F32),
                   jax.ShapeDtypeStruct((t, HG_WIDTH), F32),
                   jax.ShapeDtypeStruct((1, HG_WIDTH), F32)),
        in_specs=[tok, tok, tok, tok, _const_spec((1, HG_WIDTH)),
                  pl.BlockSpec((1, HG_HEADS, HG_E, HG_E), lambda i: (jnp.maximum(nc - 2 - i, 0), 0, 0, 0))],
        out_specs=(tok, tok, tok, pl.BlockSpec((1, HG_WIDTH), lambda i: (0, 0))),
        scratch_shapes=[pltpu.VMEM((HG_HEADS, HG_E, HG_E), F32)],
        compiler_params=_cparams(("arbitrary",)),
    )(do, q, f, v, lb, states)


GELU_C = math.sqrt(2.0 / math.pi)


def _gelu(x):
    th = jnp.tanh(GELU_C * (x + 0.044715 * x * x * x))
    return 0.5 * x * (1.0 + th), th


def _merge_core(ys5, o, og, ga, gb, wv_ref, wt_ref, ghg, who_ref):
    ys, th = _gelu(ys5)
    ysb = ys.astype(BF16)
    va = jnp.concatenate([_dot(ysb, wv_ref[s]) for s in range(N_SHARD)], axis=1)
    vt = jnp.concatenate([_dot(ysb, wt_ref[s]) for s in range(N_SHARD)], axis=1)
    svt = _sigmoid(vt)
    ya = va * svt
    rs, ons = [], []
    for h in range(HG_HEADS):
        oh = o[:, h * HG_E:(h + 1) * HG_E]
        r = lax.rsqrt(jnp.mean(oh * oh, axis=-1, keepdims=True) + NORM_EPS)
        rs.append(r)
        ons.append(oh * r)
    on = jnp.concatenate(ons, axis=1)
    sgo = _sigmoid(og)
    o2 = on * ghg * (og * sgo)
    o2b = o2.astype(BF16)
    yb = _dot(o2b, who_ref[...])
    sa = _sigmoid(ga)
    sb = _sigmoid(gb)
    mixed = sa * ya + sb * yb
    return dict(ys=ys, th=th, ysb=ysb, va=va, svt=svt, ya=ya, rs=rs, on=on, sgo=sgo, o2b=o2b, yb=yb,
                sa=sa, sb=sb, mixed=mixed)


def merge_fwd(h, ys5, o, og, ga, gb, wv, wt, ghg, who, wmo, tm=256):
    t = h.shape[0]

    def body(h_ref, ys5_ref, o_ref, og_ref, ga_ref, gb_ref, wv_ref, wt_ref, ghg_ref, who_ref, wmo_ref, out_ref):
        c = _merge_core(ys5_ref[...], o_ref[...], og_ref[...], ga_ref[...], gb_ref[...],
                        wv_ref, wt_ref, ghg_ref[...], who_ref)
        out_ref[...] = h_ref[...] + _dot(c["mixed"].astype(BF16), wmo_ref[...])

    tok = pl.BlockSpec((tm, D_MODEL), lambda i: (i, 0))
    return pl.pallas_call(
        body, name="merge_fwd", grid=(t // tm,),
        out_shape=jax.ShapeDtypeStruct((t, D_MODEL), F32),
        in_specs=[tok, pl.BlockSpec((tm, S5_WIDTH), lambda i: (i, 0)), tok, tok, tok, tok,
                  _const_spec((N_SHARD, S5_WIDTH, 256)), _const_spec((N_SHARD, S5_WIDTH, 256)),
                  _const_spec((1, HG_WIDTH)), _const_spec((HG_WIDTH, D_MODEL)), _const_spec((D_MODEL, D_MODEL))],
        out_specs=tok,
        compiler_params=_cparams(("arbitrary",)),
    )(h, ys5, o, og, ga, gb, wv, wt, ghg, who, wmo)


def merge_bwd(dh, ys5, o, og, ga, gb, wv, wt, ghg, who, wmo, tm=256):
    t = dh.shape[0]

    def body(dh_ref, ys5_ref, o_ref, og_ref, ga_ref, gb_ref, wv_ref, wt_ref, ghg_ref, who_ref, wmo_ref,
             dys5_ref, do_ref, dog_ref, dga_ref, dgb_ref, dghg_ref,
             mixb_ref, dhb_ref, ysb_ref, dvab_ref, dvtb_ref, o2b_ref, dybb_ref):
        ys5 = ys5_ref[...]
        o = o_ref[...]
        og = og_ref[...]
        ghg = ghg_ref[...]
        c = _merge_core(ys5, o, og, ga_ref[...], gb_ref[...], wv_ref, wt_ref, ghg, who_ref)
        dhb = dh_ref[...].astype(BF16)
        dhb_ref[...] = dhb
        mixb_ref[...] = c["mixed"].astype(BF16)
        ysb_ref[...] = c["ysb"]
        o2b_ref[...] = c["o2b"]
        dmix = _dot_nt(dhb, wmo_ref[...])
        sa, sb = c["sa"], c["sb"]
        dya = dmix * sa
        dyb = dmix * sb
        dga_ref[...] = dmix * c["ya"] * sa * (1.0 - sa)
        dgb_ref[...] = dmix * c["yb"] * sb * (1.0 - sb)
        svt = c["svt"]
        dva = (dya * svt).astype(BF16)
        dvt = (dya * c["va"] * svt * (1.0 - svt)).astype(BF16)
        dvab_ref[...] = dva
        dvtb_ref[...] = dvt
        dys = jnp.zeros((tm, S5_WIDTH), F32)
        for s in range(N_SHARD):
            dys = dys + _dot_nt(dva[:, s * 256:(s + 1) * 256], wv_ref[s]) + _dot_nt(dvt[:, s * 256:(s + 1) * 256], wt_ref[s])
        th = c["th"]
        dgelu = 0.5 * (1.0 + th) + 0.5 * ys5 * (1.0 - th * th) * GELU_C * (1.0 + 3.0 * 0.044715 * ys5 * ys5)
        dys5_ref[...] = dys * dgelu
        dybb = dyb.astype(BF16)
        dybb_ref[...] = dybb
        do2 = _dot_nt(dybb, who_ref[...])
        sgo = c["sgo"]
        sil = og * sgo
        on = c["on"]
        dog_ref[...] = do2 * on * ghg * (sgo * (1.0 + og * (1.0 - sgo)))
        _accum(dghg_ref, jnp.sum(do2 * on * sil, axis=0, keepdims=True), pl.program_id(0) == 0)
        don = do2 * ghg * sil
        dos = []
        for h in range(HG_HEADS):
            sl = slice(h * HG_E, (h + 1) * HG_E)
            m = jnp.mean(don[:, sl] * on[:, sl], axis=-1, keepdims=True)
            dos.append(c["rs"][h] * (don[:, sl] - on[:, sl] * m))
        do_ref[...] = jnp.concatenate(dos, axis=1)

    tok = pl.BlockSpec((tm, D_MODEL), lambda i: (i, 0))
    s5b = pl.BlockSpec((tm, S5_WIDTH), lambda i: (i, 0))
    f32t = jax.ShapeDtypeStruct((t, D_MODEL), F32)
    bft = jax.ShapeDtypeStruct((t, D_MODEL), BF16)
    return pl.pallas_call(
        body, name="merge_bwd", grid=(t // tm,),
        out_shape=(jax.ShapeDtypeStruct((t, S5_WIDTH), F32), f32t, f32t, f32t, f32t,
                   jax.ShapeDtypeStruct((1, HG_WIDTH), F32),
                   bft, bft, jax.ShapeDtypeStruct((t, S5_WIDTH), BF16), bft, bft, bft, bft),
        in_specs=[tok, s5b, tok, tok, tok, tok,
                  _const_spec((N_SHARD, S5_WIDTH, 256)), _const_spec((N_SHARD, S5_WIDTH, 256)),
                  _const_spec((1, HG_WIDTH)), _const_spec((HG_WIDTH, D_MODEL)), _const_spec((D_MODEL, D_MODEL))],
        out_specs=(s5b, tok, tok, tok, tok, pl.BlockSpec((1, HG_WIDTH), lambda i: (0, 0)),
                   tok, tok, s5b, tok, tok, tok, tok),
        compiler_params=_cparams(("arbitrary",)),
    )(dh, ys5, o, og, ga, gb, wv, wt, ghg, who, wmo)


def head_fwd_bwd(h, p, tgt, gple, wpg, wpp, gfin, tm=256):
    t = h.shape[0]

    def body(h_ref, p_ref, tgt_ref, gple_ref, wpg_ref, wpp_ref, gfin_ref,
             loss_ref, dh_ref, dgple_ref, dgfin_ref, nb_ref, dlb_ref, dppb_ref):
        first = pl.program_id(0) == 0
        hv = h_ref[...]
        gple = gple_ref[...]
        gfin = gfin_ref[...]
        n, r3 = _rms_fwd(hv, gple)
        nb = n.astype(BF16)
        nb_ref[...] = nb
        pg = _sigmoid(_dot(nb, wpg_ref[...]))
        pb = p_ref[...].astype(BF16)
        pp = jnp.concatenate([_dot(pb, wpp_ref[s]) for s in range(N_SHARD)], axis=1)
        h4 = hv + pg * pp
        y, r4 = _rms_fwd(h4, gfin)
        err = y - tgt_ref[...]
        lsum = 0.5 * jnp.sum(jnp.sum(err * err, axis=-1, keepdims=True), axis=0, keepdims=True) / D_MODEL
        _accum(loss_ref, jnp.broadcast_to(lsum, (8, 128)), first)
        dy = err * (1.0 / D_MODEL)
        dh4, dgf = _rms_bwd(h4, r4, gfin, dy)
        _accum(dgfin_ref, dgf, first)
        dpp = dh4 * pg
        dppb_ref[...] = dpp.astype(BF16)
        dl = (dh4 * pp * pg * (1.0 - pg)).astype(BF16)
        dlb_ref[...] = dl
        dn = _dot_nt(dl, wpg_ref[...])
        dx, dgp = _rms_bwd(hv, r3, gple, dn)
        _accum(dgple_ref, dgp, first)
        dh_ref[...] = dh4 + dx

    tok = pl.BlockSpec((tm, D_MODEL), lambda i: (i, 0))
    vec = pl.BlockSpec((1, D_MODEL), lambda i: (0, 0))
    bft = jax.ShapeDtypeStruct((t, D_MODEL), BF16)
    return pl.pallas_call(
        body, name="head_fwd_bwd", grid=(t // tm,),
        out_shape=(jax.ShapeDtypeStruct((8, 128), F32), jax.ShapeDtypeStruct((t, D_MODEL), F32),
                   jax.ShapeDtypeStruct((1, D_MODEL), F32), jax.ShapeDtypeStruct((1, D_MODEL), F32),
                   bft, bft, bft),
        in_specs=[tok, pl.BlockSpec((tm, PLE_DIM), lambda i: (i, 0)), tok,
                  _const_spec((1, D_MODEL)), _const_spec((D_MODEL, D_MODEL)),
                  _const_spec((N_SHARD, PLE_DIM, 256)), _const_spec((1, D_MODEL))],
        out_specs=(pl.BlockSpec((8, 128), lambda i: (0, 0)), tok, vec, vec, tok, tok, tok),
        compiler_params=_cparams(("arbitrary",)),
    )(h, p, tgt, gple, wpg, wpp, gfin)


BIG = ("ffn1_w_gate", "ffn1_w_up", "ffn1_w_down", "w_in", "s5_glu_val", "s5_glu_gate", "hg_w_out",
       "w_merge_out", "ffn2_w_gate", "ffn2_w_up", "ffn2_w_down", "ple_w_gate", "ple_w_proj")
BIG_SHARD = {
    "ffn1_w_gate": (D_MODEL, FF_PAD), "ffn1_w_up": (D_MODEL, FF_PAD), "ffn1_w_down": (FF_PAD, D_MODEL),
    "ffn2_w_gate": (D_MODEL, FF_PAD), "ffn2_w_up": (D_MODEL, FF_PAD), "ffn2_w_down": (FF_PAD, D_MODEL),
    "w_in": (D_MODEL, IN_COLS // N_SHARD), "s5_glu_val": (S5_WIDTH, 256), "s5_glu_gate": (S5_WIDTH, 256),
    "hg_w_out": (256, D_MODEL), "w_merge_out": (256, D_MODEL), "ple_w_gate": (256, D_MODEL),
    "ple_w_proj": (PLE_DIM, 256),
}


def _lower_bound(hb):
    return jax.nn.softmax(hb, axis=0)[0:1]


def local_step(x, p, tgt, wts, sm):
    rows_full = lambda w: w.reshape(N_SHARD * w.shape[1], w.shape[2])
    lb, lb_vjp = jax.vjp(_lower_bound, sm["hg_lower_bound"])
    s5_names = ("s5_lam_re", "s5_lam_im", "s5_log_dt", "s5_b_re", "s5_b_im", "s5_c_re", "s5_c_im")
    (lam_bar, bmat, cmat), s5_vjp = jax.vjp(s5_prep, *[sm[k] for k in s5_names])
    pw_r, pw_i = _lam_powers(lam_bar)
    bmat_b = bmat.astype(BF16)
    cmat_b = cmat.astype(BF16)
    bmat_t = jnp.swapaxes(bmat, -1, -2).astype(BF16)
    cmat_t = jnp.swapaxes(cmat, -1, -2).astype(BF16)
    who = rows_full(wts["hg_w_out"])
    wmo = rows_full(wts["w_merge_out"])
    wpg = rows_full(wts["ple_w_gate"])

    h1, a1, b1 = ffn_fwd(x, sm["ffn1_norm"], wts["ffn1_w_gate"], wts["ffn1_w_up"], wts["ffn1_w_down"], "ffn1_fwd")
    s5in, q, f, v, og, ga, gb = inproj_fwd(h1, sm["mix_norm"], wts["w_in"])
    ys5, xp = s5_fwd(s5in, pw_r, pw_i, bmat_b, cmat_b, sm["s5_d"])
    o, states = hgrn_fwd(q, f, v, lb)
    h2 = merge_fwd(h1, ys5, o, og, ga, gb, wts["s5_glu_val"], wts["s5_glu_gate"], sm["hg_out_norm"], who, wmo)
    h3, a2, b2 = ffn_fwd(h2, sm["ffn2_norm"], wts["ffn2_w_gate"], wts["ffn2_w_up"], wts["ffn2_w_down"], "ffn2_fwd")
    loss, dh3, d_ple_norm, d_final_norm, npb, dlgb, dppb = head_fwd_bwd(
        h3, p, tgt, sm["ple_norm"], wpg, wts["ple_w_proj"], sm["final_norm"])

    gb_ = {}
    gs = {"ple_norm": d_ple_norm, "final_norm": d_final_norm}
    gb_["ple_w_gate"] = tn_matmul(npb, dlgb, "g_ple_w_gate", "rows")
    gb_["ple_w_proj"] = tn_matmul(p, dppb, "g_ple_w_proj", "cols")

    dh2, gs["ffn2_norm"], n2b, dhb2, da2, db2, s2 = ffn_bwd(
        dh3, h2, a2, b2, sm["ffn2_norm"], wts["ffn2_w_gate"], wts["ffn2_w_up"], wts["ffn2_w_down"], "ffn2_bwd")
    gb_["ffn2_w_gate"] = tn_matmul(n2b, da2, "g_ffn2_w_gate", "cols")
    gb_["ffn2_w_up"] = tn_matmul(n2b, db2, "g_ffn2_w_up", "cols")
    gb_["ffn2_w_down"] = tn_matmul(s2, dhb2, "g_ffn2_w_down", "rows")

    dys5, do, dog, dga, dgb, gs["hg_out_norm"], mixb, dh2b, ysb, dvab, dvtb, o2b, dybb = merge_bwd(
        dh2, ys5, o, og, ga, gb, wts["s5_glu_val"], wts["s5_glu_gate"], sm["hg_out_norm"], who, wmo)
    gb_["w_merge_out"] = tn_matmul(mixb, dh2b, "g_w_merge_out", "rows")
    gb_["s5_glu_val"] = tn_matmul(ysb, dvab, "g_s5_glu_val", "cols")
    gb_["s5_glu_gate"] = tn_matmul(ysb, dvtb, "g_s5_glu_gate", "cols")
    gb_["hg_w_out"] = tn_matmul(o2b, dybb, "g_hg_w_out", "rows")

    dq, df, dv, dlb = hgrn_bwd(do, q, f, v, lb, states)
    (gs["hg_lower_bound"],) = lb_vjp(dlb)
    du, dbmat, dcmat, dlam8, gs["s5_d"] = s5_bwd(
        dys5, s5in, xp, pw_r, pw_i, pw_r[::-1], pw_i[::-1], bmat_b, bmat_t, cmat_t, sm["s5_d"])
    for k, g in zip(s5_names, s5_vjp((jnp.sum(dlam8, axis=1), dbmat, dcmat))):
        gs[k] = g

    dh1, gs["mix_norm"], nmb, dprojb = inproj_bwd(dh2, h1, sm["mix_norm"], wts["w_in"], (du, dq, df, dv, dog, dga, dgb))
    gb_["w_in"] = tn_matmul(nmb, dprojb, "g_w_in", "cols")

    dx, gs["ffn1_norm"], n1b, dhb1, da1, db1, s1 = ffn_bwd(
        dh1, x, a1, b1, sm["ffn1_norm"], wts["ffn1_w_gate"], wts["ffn1_w_up"], wts["ffn1_w_down"], "ffn1_bwd")
    gb_["ffn1_w_gate"] = tn_matmul(n1b, da1, "g_ffn1_w_gate", "cols")
    gb_["ffn1_w_up"] = tn_matmul(n1b, db1, "g_ffn1_w_up", "cols")
    gb_["ffn1_w_down"] = tn_matmul(s1, dhb1, "g_ffn1_w_down", "rows")
    return loss, dx, gb_, gs


MESH = pl.DeviceIdType.MESH
ANY = pl.BlockSpec(memory_space=pl.ANY)


def _place():
    x, y, c = lax.axis_index("x"), lax.axis_index("y"), lax.axis_index("c")
    return x, y, c


def _remote(src, dst, ssem, rsem, dev):
    return pltpu.make_async_remote_copy(src_ref=src, dst_ref=dst, send_sem=ssem, recv_sem=rsem,
                                        device_id=dev, device_id_type=MESH)


def allgather_weights(shards):
    n = len(shards)

    def body(*refs):
        ins, outs = refs[:n], refs[n:2 * n]
        loc_sem, s_own, r_own, s_fwd, r_fwd, s_sib, r_sib = refs[2 * n:]
        x, y, c = _place()
        me = 2 * x + y
        nbr = ((1 - x, y), (x, 1 - y))
        nbr_id = (2 * (1 - x) + y, 2 * x + (1 - y))
        diag_id = 2 * (1 - x) + (1 - y)
        sib = (x, y, 1 - c)

        def rows(w, q=None):
            r = ins[w].shape[0]
            if q is None:
                return pl.ds(pl.multiple_of(c * (r // 2), 16), r // 2)
            return pl.ds(pl.multiple_of(c * (r // 2) + q * (r // 4), 16), r // 4)

        def own(w, j):
            return _remote(ins[w].at[rows(w)], outs[w].at[me, rows(w)], s_own.at[w, j], r_own.at[w, j],
                           (nbr[j][0], nbr[j][1], c))

        def from_nbr(w, j):
            return _remote(ins[w].at[rows(w)], outs[w].at[nbr_id[j], rows(w)], s_own.at[w, j], r_own.at[w, j],
                           (nbr[j][0], nbr[j][1], c))

        def fwd(w, j):
            piece = outs[w].at[nbr_id[j], rows(w, j)]
            return _remote(piece, piece, s_fwd.at[w, j], r_fwd.at[w, j], (nbr[1 - j][0], nbr[1 - j][1], c))

        def from_diag(w, j):
            piece = outs[w].at[diag_id, rows(w, j)]
            return _remote(piece, piece, s_fwd.at[w, j], r_fwd.at[w, j], (nbr[1 - j][0], nbr[1 - j][1], c))

        def to_sib(w, k):
            piece = (outs[w].at[nbr_id[k], rows(w)] if k < 2 else outs[w].at[diag_id, rows(w, k - 2)])
            return _remote(piece, piece, s_sib.at[w, k], r_sib.at[w, k], sib)

        def from_sib(w, k):
            r = ins[w].shape[0]
            if k < 2:
                piece = outs[w].at[nbr_id[k], pl.ds(pl.multiple_of((1 - c) * (r // 2), 16), r // 2)]
            else:
                piece = outs[w].at[diag_id, pl.ds(pl.multiple_of((1 - c) * (r // 2) + (k - 2) * (r // 4), 16), r // 4)]
            return _remote(piece, piece, s_sib.at[w, k], r_sib.at[w, k], sib)

        local = [pltpu.make_async_copy(ins[w], outs[w].at[me], loc_sem.at[w]) for w in range(n)]
        for w in range(n):
            local[w].start()
            own(w, 0).start()
            own(w, 1).start()
        for w in range(n):
            for j in range(2):
                from_nbr(w, j).wait_recv()
                fwd(w, j).start()
                to_sib(w, j).start()
        for w in range(n):
            for j in range(2):
                from_diag(w, j).wait_recv()
                to_sib(w, 2 + j).start()
        for w in range(n):
            for k in range(4):
                from_sib(w, k).wait_recv()
        for w in range(n):
            local[w].wait()
            for j in range(2):
                own(w, j).wait_send()
                fwd(w, j).wait_send()
            for k in range(4):
                to_sib(w, k).wait_send()

    dma = pltpu.SemaphoreType.DMA
    return pl.pallas_call(
        body, name="allgather_weights",
        out_shape=tuple(jax.ShapeDtypeStruct((N_SHARD,) + s.shape, s.dtype) for s in shards),
        in_specs=[ANY] * n, out_specs=tuple([ANY] * n),
        scratch_shapes=[dma((n,)), dma((n, 2)), dma((n, 2)), dma((n, 2)), dma((n, 2)), dma((n, 4)), dma((n, 4))],
    )(*shards)


def exchange_halves(grads):
    n = len(grads)

    def body(*refs):
        ins, outs = refs[:n], refs[n:2 * n]
        ssem, rsem = refs[2 * n:]
        x, y, c = _place()
        cps = []
        for w in range(n):
            half = ins[w].shape[1] // 2
            src = ins[w].at[:, pl.ds(pl.multiple_of((1 - c) * half, 8), half), :]
            cps.append(_remote(src, outs[w], ssem.at[w], rsem.at[w], (x, y, 1 - c)))
            cps[w].start()
        for w in range(n):
            cps[w].wait()

    dma = pltpu.SemaphoreType.DMA
    return pl.pallas_call(
        body, name="exchange_halves",
        out_shape=tuple(jax.ShapeDtypeStruct((N_SHARD, g.shape[1] // 2, g.shape[2]), g.dtype) for g in grads),
        in_specs=[ANY] * n, out_specs=tuple([ANY] * n),
        scratch_shapes=[dma((n,)), dma((n,))],
    )(*grads)


def scatter_chip_sums(sums):
    n = len(sums)

    def body(*refs):
        ins, outs = refs[:n], refs[n:2 * n]
        ssem, rsem = refs[2 * n:]
        x, y, c = _place()
        chips = ((1 - x, y), (x, 1 - y), (1 - x, 1 - y))
        cps = []
        for w in range(n):
            for j, ch in enumerate(chips):
                cp = _remote(ins[w].at[2 * ch[0] + ch[1]], outs[w].at[j], ssem.at[w, j], rsem.at[w, j],
                             (ch[0], ch[1], c))
                cp.start()
                cps.append(cp)
        for cp in cps:
            cp.wait()

    dma = pltpu.SemaphoreType.DMA
    return pl.pallas_call(
        body, name="scatter_chip_sums",
        out_shape=tuple(jax.ShapeDtypeStruct((3,) + s.shape[1:], s.dtype) for s in sums),
        in_specs=[ANY] * n, out_specs=tuple([ANY] * n),
        scratch_shapes=[dma((n, 3)), dma((n, 3))],
    )(*sums)


def join_halves(halves):
    n = len(halves)

    def body(*refs):
        ins, outs = refs[:n], refs[n:2 * n]
        lsem, ssem, rsem = refs[2 * n:]
        x, y, c = _place()
        cps, loc = [], []
        for w in range(n):
            half = ins[w].shape[0]
            dst = outs[w].at[pl.ds(pl.multiple_of(c * half, 8), half), :]
            loc.append(pltpu.make_async_copy(ins[w], dst, lsem.at[w]))
            loc[w].start()
            cps.append(_remote(ins[w], dst, ssem.at[w], rsem.at[w], (x, y, 1 - c)))
            cps[w].start()
        for w in range(n):
            loc[w].wait()
            cps[w].wait()

    dma = pltpu.SemaphoreType.DMA
    return pl.pallas_call(
        body, name="join_halves",
        out_shape=tuple(jax.ShapeDtypeStruct((2 * h.shape[0], h.shape[1]), h.dtype) for h in halves),
        in_specs=[ANY] * n, out_specs=tuple([ANY] * n),
        scratch_shapes=[dma((n,)), dma((n,)), dma((n,))],
    )(*halves)


def allreduce_small(vec):
    rows = vec.shape[0]

    def body(v_ref, o_ref, buf, ssem, rsem):
        x, y, c = _place()
        me = 4 * x + 2 * y + c
        cps = []
        for k in range(1, 8):
            fx, fy, fc = (k >> 2) & 1, (k >> 1) & 1, k & 1
            dev = (x ^ fx, y ^ fy, c ^ fc)
            cp = _remote(v_ref, buf.at[k], ssem.at[k], rsem.at[k], dev)
            cp.start()
            cps.append(cp)
        buf[0] = v_ref[...]
        for cp in cps:
            cp.wait()
        acc = buf[me]
        for a in range(1, 8):
            acc = acc + buf[a ^ me]
        o_ref[...] = acc

    return pl.pallas_call(
        body, name="allreduce_small",
        out_shape=jax.ShapeDtypeStruct(vec.shape, F32),
        in_specs=[pl.BlockSpec(memory_space=pltpu.VMEM)],
        out_specs=pl.BlockSpec(memory_space=pltpu.VMEM),
        scratch_shapes=[pltpu.VMEM((8, rows, 128), F32), pltpu.SemaphoreType.DMA((8,)), pltpu.SemaphoreType.DMA((8,))],
        compiler_params=pltpu.CompilerParams(vmem_limit_bytes=VMEM_LIMIT),
    )(vec)


ROW_TILE = 128


def add_own_half(core, g, recv, name):
    _, r, cc = g.shape
    half = r // 2
    nb = half // ROW_TILE

    def body(c_ref, g_ref, r_ref, o_ref, ob_ref):
        del c_ref
        s = g_ref[...] + r_ref[...]
        o_ref[...] = s
        ob_ref[...] = s.astype(BF16)

    blk = (None, ROW_TILE, cc)
    return pl.pallas_call(
        body, name=name,
        grid_spec=pltpu.PrefetchScalarGridSpec(
            num_scalar_prefetch=1, grid=(N_SHARD, nb),
            in_specs=[pl.BlockSpec(blk, lambda s, i, c_ref: (s, c_ref[0] * nb + i, 0)),
                      pl.BlockSpec(blk, lambda s, i, c_ref: (s, i, 0))],
            out_specs=(pl.BlockSpec(blk, lambda s, i, c_ref: (s, i, 0)),
                       pl.BlockSpec(blk, lambda s, i, c_ref: (s, i, 0)))),
        out_shape=(jax.ShapeDtypeStruct((N_SHARD, half, cc), F32),
                   jax.ShapeDtypeStruct((N_SHARD, half, cc), BF16)),
        compiler_params=_cparams(("arbitrary", "arbitrary")),
    )(core, g, recv)


def add_chip_sums(chip, own, recv, name):
    _, half, cc = own.shape
    nb = half // ROW_TILE

    def body(s_ref, o_ref, r_ref, out_ref):
        del s_ref
        acc = o_ref[...] + r_ref[0].astype(F32)
        acc = acc + r_ref[1].astype(F32)
        out_ref[...] = acc + r_ref[2].astype(F32)

    return pl.pallas_call(
        body, name=name,
        grid_spec=pltpu.PrefetchScalarGridSpec(
            num_scalar_prefetch=1, grid=(nb,),
            in_specs=[pl.BlockSpec((None, ROW_TILE, cc), lambda i, s_ref: (s_ref[0], i, 0)),
                      pl.BlockSpec((3, ROW_TILE, cc), lambda i, s_ref: (0, i, 0))],
            out_specs=pl.BlockSpec((ROW_TILE, cc), lambda i, s_ref: (i, 0))),
        out_shape=jax.ShapeDtypeStruct((half, cc), F32),
        compiler_params=_cparams(("arbitrary",)),
    )(chip, own, recv)


def adamw(w, m, v, g, name):
    r, cc = w.shape
    tr = ROW_TILE if r % ROW_TILE == 0 else r
    bc1 = 1.0 / (1.0 - ADAM_B1 ** ADAM_STEP)
    bc2 = 1.0 / (1.0 - ADAM_B2 ** ADAM_STEP)

    def body(w_ref, m_ref, v_ref, g_ref, d_ref, mo_ref, vo_ref):
        gv = g_ref[...]
        mn = ADAM_B1 * m_ref[...] + (1.0 - ADAM_B1) * gv
        vn = ADAM_B2 * v_ref[...] + (1.0 - ADAM_B2) * (gv * gv)
        mo_ref[...] = mn
        vo_ref[...] = vn
        d_ref[...] = -ADAM_LR * ((mn * bc1) / (jnp.sqrt(vn * bc2) + ADAM_EPS) + ADAM_WD * w_ref[...])

    blk = pl.BlockSpec((tr, cc), lambda i: (i, 0))
    shp = jax.ShapeDtypeStruct((r, cc), F32)
    return pl.pallas_call(
        body, name=name, grid=(r // tr,),
        out_shape=(shp, shp, shp), in_specs=[blk] * 4, out_specs=(blk, blk, blk),
        compiler_params=_cparams(("arbitrary",)),
    )(w, m, v, g)


SMALL = ("ffn1_norm", "mix_norm", "s5_lam_re", "s5_lam_im", "s5_log_dt", "s5_b_re", "s5_b_im", "s5_c_re",
         "s5_c_im", "s5_d", "hg_lower_bound", "hg_out_norm", "ffn2_norm", "ple_norm", "final_norm")
WEIGHTS = ("ffn1_norm", "ffn1_w_gate", "ffn1_w_up", "ffn1_w_down", "mix_norm", "w_in", "s5_lam_re", "s5_lam_im",
           "s5_log_dt", "s5_b_re", "s5_b_im", "s5_c_re", "s5_c_im", "s5_d", "s5_glu_val", "s5_glu_gate",
           "hg_lower_bound", "hg_out_norm", "hg_w_out", "w_merge_out", "ffn2_norm", "ffn2_w_gate", "ffn2_w_up",
           "ffn2_w_down", "ple_norm", "ple_w_gate", "ple_w_proj", "final_norm")


def _pad_shard(name, w):
    w = w[0]
    r, c = BIG_SHARD[name]
    return jnp.pad(w, ((0, r - w.shape[0]), (0, c - w.shape[1])))


def _unpad_shard(name, w, like):
    return w[:like.shape[1], :like.shape[2]][None]


def _pack(parts):
    flat = jnp.concatenate([jnp.zeros((128,), F32)] + [a.reshape(-1) for a in parts])
    rows = -(-flat.shape[0] // 1024) * 8
    return jnp.pad(flat, (0, rows * 128 - flat.shape[0])).reshape(rows, 128)


def _unpack(vec, likes):
    flat = vec.reshape(-1)
    out, off = [], 128
    for a in likes:
        out.append(flat[off:off + a.size].reshape(a.shape))
        off += a.size
    return out


def _small_view(name, w):
    if name.startswith("s5_") and name != "s5_d":
        return w[0]
    if name == "final_norm":
        return w.reshape(1, D_MODEL)
    return w


def kernel(x, p, ffn1_norm, ffn1_w_gate, ffn1_w_up, ffn1_w_down, mix_norm, w_in, s5_lam_re, s5_lam_im, s5_log_dt, s5_b_re, s5_b_im, s5_c_re, s5_c_im, s5_d, s5_glu_val, s5_glu_gate, hg_lower_bound, hg_out_norm, hg_w_out, w_merge_out, ffn2_norm, ffn2_w_gate, ffn2_w_up, ffn2_w_down, ple_norm, ple_w_gate, ple_w_proj, final_norm, loss_target, m_ffn1_norm, m_ffn1_w_gate, m_ffn1_w_up, m_ffn1_w_down, m_mix_norm, m_w_in, m_s5_lam_re, m_s5_lam_im, m_s5_log_dt, m_s5_b_re, m_s5_b_im, m_s5_c_re, m_s5_c_im, m_s5_d, m_s5_glu_val, m_s5_glu_gate, m_hg_lower_bound, m_hg_out_norm, m_hg_w_out, m_w_merge_out, m_ffn2_norm, m_ffn2_w_gate, m_ffn2_w_up, m_ffn2_w_down, m_ple_norm, m_ple_w_gate, m_ple_w_proj, m_final_norm, v_ffn1_norm, v_ffn1_w_gate, v_ffn1_w_up, v_ffn1_w_down, v_mix_norm, v_w_in, v_s5_lam_re, v_s5_lam_im, v_s5_log_dt, v_s5_b_re, v_s5_b_im, v_s5_c_re, v_s5_c_im, v_s5_d, v_s5_glu_val, v_s5_glu_gate, v_hg_lower_bound, v_hg_out_norm, v_hg_w_out, v_w_merge_out, v_ffn2_norm, v_ffn2_w_gate, v_ffn2_w_up, v_ffn2_w_down, v_ple_norm, v_ple_w_gate, v_ple_w_proj, v_final_norm):
    given = dict(locals())
    wv = {k: given[k] for k in WEIGHTS}
    mv = {k: given["m_" + k] for k in WEIGHTS}
    vv = {k: given["v_" + k] for k in WEIGHTS}

    shards_f32 = {k: _pad_shard(k, wv[k]) for k in BIG}
    gathered = allgather_weights([shards_f32[k].astype(BF16) for k in BIG])
    wts = dict(zip(BIG, gathered))
    sm = {k: _small_view(k, wv[k]) for k in SMALL}

    loss_blk, dx, gbig, gsm = local_step(x[0], p[0, 0], loss_target[0], wts, sm)

    core = lax.axis_index("c").astype(jnp.int32).reshape(1)
    chip = (2 * lax.axis_index("x") + lax.axis_index("y")).astype(jnp.int32).reshape(1)
    glist = [gbig[k] for k in BIG]
    recv1 = exchange_halves(glist)
    sums = [add_own_half(core, g, r, "pair_sum_" + k) for k, g, r in zip(BIG, glist, recv1)]
    recv2 = scatter_chip_sums([s[1] for s in sums])
    halves = [add_chip_sums(chip, s[0], r, "chip_sum_" + k) for k, s, r in zip(BIG, sums, recv2)]
    full = dict(zip(BIG, join_halves(halves)))

    small_likes = [wv[k] for k in SMALL]
    packed = _pack([gsm[k] for k in SMALL])
    packed = packed.at[0, 0].set(loss_blk[0, 0])
    total = allreduce_small(packed)
    loss = total[0, 0]
    gsmall = dict(zip(SMALL, _unpack(total, small_likes)))

    grads, deltas, new_m, new_v = {}, {}, {}, {}
    for k in BIG:
        d, mn, vn = adamw(shards_f32[k], _pad_shard(k, mv[k]), _pad_shard(k, vv[k]), full[k], "adamw_" + k)
        grads[k] = _unpad_shard(k, full[k], wv[k])
        deltas[k] = _unpad_shard(k, d, wv[k])
        new_m[k] = _unpad_shard(k, mn, wv[k])
        new_v[k] = _unpad_shard(k, vn, wv[k])
    sw = _pack([wv[k] for k in SMALL])
    smm = _pack([mv[k] for k in SMALL])
    svv = _pack([vv[k] for k in SMALL])
    sd, smn, svn = adamw(sw, smm, svv, total, "adamw_small")
    for k, d, mn, vn in zip(SMALL, _unpack(sd, small_likes), _unpack(smn, small_likes), _unpack(svn, small_likes)):
        grads[k], deltas[k], new_m[k], new_v[k] = gsmall[k], d, mn, vn

    return (loss, dx[None], *[grads[k] for k in WEIGHTS], *[deltas[k] for k in WEIGHTS],
            *[new_m[k] for k in WEIGHTS], *[new_v[k] for k in WEIGHTS])
```

```python
import math

import jax
import jax.numpy as jnp
from jax import lax
from jax.experimental import pallas as pl
from jax.experimental.pallas import tpu as pltpu

F32 = jnp.float32
BF16 = jnp.bfloat16

D_MODEL = 1024
D_FF = 2816
N_SHARD = 4
FF_SHARD = D_FF // N_SHARD
FF_PAD = 768
NORM_EPS = 1e-6
PLE_DIM = 256

S5_WIDTH = 512
S5_GROUPS = 32
S5_GROUP = 16
S5_STATE = 64
S5_N = S5_GROUPS * S5_STATE
S5_KT = 2

HG_HEADS = 8
HG_E = 128
HG_WIDTH = 1024
CHUNK = 64
IN_COLS = S5_WIDTH + 4 * HG_WIDTH + 2 * D_MODEL
IN_SPLITS = (0, 512, 1536, 2560, 3584, 4608, 5632, 6656)

ADAM_LR = 0.001
ADAM_B1 = 0.9
ADAM_B2 = 0.999
ADAM_EPS = 1e-08
ADAM_WD = 0.01
ADAM_STEP = 10

VMEM_LIMIT = 60 * 1024 * 1024
HIGHEST = lax.Precision.HIGHEST


def _cparams(sem=None, **kw):
    return pltpu.CompilerParams(dimension_semantics=sem, vmem_limit_bytes=VMEM_LIMIT, **kw)


def _const_spec(shape):
    nd = len(shape)
    return pl.BlockSpec(shape, lambda *_: (0,) * nd, pipeline_mode=pl.Buffered(1))


def _dot(a, b):
    return jnp.dot(a, b, preferred_element_type=F32)


def _dot_nt(a, b):
    return lax.dot_general(a, b, (((1,), (1,)), ((), ())), preferred_element_type=F32)


def _dot_tn(a, b):
    return lax.dot_general(a, b, (((0,), (0,)), ((), ())), preferred_element_type=F32)


def _sigmoid(x):
    return 1.0 / (1.0 + jnp.exp(-x))


def _rms_fwd(x, g):
    r = lax.rsqrt(jnp.mean(x * x, axis=-1, keepdims=True) + NORM_EPS)
    return x * r * g, r


def _rms_bwd(x, r, g, dy):
    xh = x * r
    dyg = dy * g
    m = jnp.mean(dyg * xh, axis=-1, keepdims=True)
    return r * (dyg - xh * m), jnp.sum(dy * xh, axis=0, keepdims=True)


def _accum(ref, val, first):
    @pl.when(first)
    def _():
        ref[...] = val

    @pl.when(jnp.logical_not(first))
    def _():
        ref[...] += val


def ffn_fwd(h, gain, wg, wu, wd, name, tm=256):
    t = h.shape[0]

    def body(h_ref, g_ref, wg_ref, wu_ref, wd_ref, o_ref, a_ref, b_ref):
        hv = h_ref[...]
        n, _ = _rms_fwd(hv, g_ref[...])
        nb = n.astype(BF16)
        acc = jnp.zeros((tm, D_MODEL), F32)
        for s in range(N_SHARD):
            a = _dot_nt(nb, wg_ref[s])
            b = _dot_nt(nb, wu_ref[s])
            a_ref[s] = a.astype(BF16)
            b_ref[s] = b.astype(BF16)
            sv = (a * _sigmoid(a) * b).astype(BF16)
            acc = acc + _dot(sv, wd_ref[s])
        o_ref[...] = hv + 0.5 * acc

    return pl.pallas_call(
        body, name=name, grid=(t // tm,),
        out_shape=(jax.ShapeDtypeStruct((t, D_MODEL), F32),
                   jax.ShapeDtypeStruct((N_SHARD, t, FF_PAD), BF16),
                   jax.ShapeDtypeStruct((N_SHARD, t, FF_PAD), BF16)),
        in_specs=[pl.BlockSpec((tm, D_MODEL), lambda i: (i, 0)),
                  _const_spec((1, D_MODEL)),
                  _const_spec((N_SHARD, FF_PAD, D_MODEL)),
                  _const_spec((N_SHARD, FF_PAD, D_MODEL)),
                  _const_spec((N_SHARD, FF_PAD, D_MODEL))],
        out_specs=(pl.BlockSpec((tm, D_MODEL), lambda i: (i, 0)),
                   pl.BlockSpec((N_SHARD, tm, FF_PAD), lambda i: (0, i, 0)),
                   pl.BlockSpec((N_SHARD, tm, FF_PAD), lambda i: (0, i, 0))),
        compiler_params=_cparams(("arbitrary",)),
    )(h, gain, wg, wu, wd)


def ffn_bwd(dho, h, a, b, gain, wg, wu, wd, name, tm=256):
    t = h.shape[0]

    def body(dho_ref, h_ref, a_ref, b_ref, g_ref, wg_ref, wu_ref, wd_ref,
             dh_ref, dg_ref, nb_ref, dhb_ref, da_ref, db_ref, s_ref):
        hv = h_ref[...]
        g = g_ref[...]
        n, r = _rms_fwd(hv, g)
        nb_ref[...] = n.astype(BF16)
        dhalf = (0.5 * dho_ref[...]).astype(BF16)
        dhb_ref[...] = dhalf
        dn = jnp.zeros((tm, D_MODEL), F32)
        for s in range(N_SHARD):
            av = a_ref[s].astype(F32)
            bv = b_ref[s].astype(F32)
            sg = _sigmoid(av)
            sil = av * sg
            s_ref[s] = (sil * bv).astype(BF16)
            ds = _dot_nt(dhalf, wd_ref[s])
            da = (ds * bv * (sg * (1.0 + av * (1.0 - sg)))).astype(BF16)
            db = (ds * sil).astype(BF16)
            da_ref[s] = da
            db_ref[s] = db
            dn = dn + _dot(da, wg_ref[s]) + _dot(db, wu_ref[s])
        dx, dg = _rms_bwd(hv, r, g, dn)
        dh_ref[...] = dho_ref[...] + dx
        _accum(dg_ref, dg, pl.program_id(0) == 0)

    tok = pl.BlockSpec((tm, D_MODEL), lambda i: (i, 0))
    hid = pl.BlockSpec((N_SHARD, tm, FF_PAD), lambda i: (0, i, 0))
    return pl.pallas_call(
        body, name=name, grid=(t // tm,),
        out_shape=(jax.ShapeDtypeStruct((t, D_MODEL), F32),
                   jax.ShapeDtypeStruct((1, D_MODEL), F32),
                   jax.ShapeDtypeStruct((t, D_MODEL), BF16),
                   jax.ShapeDtypeStruct((t, D_MODEL), BF16),
                   jax.ShapeDtypeStruct((N_SHARD, t, FF_PAD), BF16),
                   jax.ShapeDtypeStruct((N_SHARD, t, FF_PAD), BF16),
                   jax.ShapeDtypeStruct((N_SHARD, t, FF_PAD), BF16)),
        in_specs=[tok, tok, hid, hid, _const_spec((1, D_MODEL)),
                  _const_spec((N_SHARD, FF_PAD, D_MODEL)),
                  _const_spec((N_SHARD, FF_PAD, D_MODEL)),
                  _const_spec((N_SHARD, FF_PAD, D_MODEL))],
        out_specs=(tok, pl.BlockSpec((1, D_MODEL), lambda i: (0, 0)), tok, tok, hid, hid, hid),
        compiler_params=_cparams(("arbitrary",)),
    )(dho, h, a, b, gain, wg, wu, wd)


def tn_matmul(x, y, name, shard, tk=512):
    x3, y3 = x.ndim == 3, y.ndim == 3
    t = x.shape[-2]
    m = x.shape[-1] // (N_SHARD if (shard == "rows" and not x3) else 1)
    n = y.shape[-1] // (N_SHARD if (shard == "cols" and not y3) else 1)
    nk = t // tk

    def body(x_ref, y_ref, o_ref):
        _accum(o_ref, _dot_tn(x_ref[...].astype(BF16), y_ref[...].astype(BF16)), pl.program_id(1) == 0)

    if x3:
        x_spec = pl.BlockSpec((None, tk, m), lambda s, k: (s, k, 0))
    elif shard == "rows":
        x_spec = pl.BlockSpec((tk, m), lambda s, k: (k, s))
    else:
        x_spec = pl.BlockSpec((tk, m), lambda s, k: (k, 0))
    if y3:
        y_spec = pl.BlockSpec((None, tk, n), lambda s, k: (s, k, 0))
    elif shard == "cols":
        y_spec = pl.BlockSpec((tk, n), lambda s, k: (k, s))
    else:
        y_spec = pl.BlockSpec((tk, n), lambda s, k: (k, 0))
    return pl.pallas_call(
        body, name=name, grid=(N_SHARD, nk),
        out_shape=jax.ShapeDtypeStruct((N_SHARD, m, n), F32),
        in_specs=[x_spec, y_spec],
        out_specs=pl.BlockSpec((None, m, n), lambda s, k: (s, 0, 0)),
        compiler_params=_cparams(("arbitrary", "arbitrary")),
    )(x, y)


def inproj_fwd(h, gain, w_in, tm=256):
    t = h.shape[0]
    widths = [IN_SPLITS[j + 1] - IN_SPLITS[j] for j in range(7)]
    sh_cols = IN_COLS // N_SHARD

    def body(h_ref, g_ref, w_ref, *outs):
        n, _ = _rms_fwd(h_ref[...], g_ref[...])
        nb = n.astype(BF16)
        proj = jnp.concatenate([_dot(nb, w_ref[s]) for s in range(N_SHARD)], axis=1)
        for j, o_ref in enumerate(outs):
            o_ref[...] = proj[:, IN_SPLITS[j]:IN_SPLITS[j + 1]]

    return pl.pallas_call(
        body, name="inproj_fwd", grid=(t // tm,),
        out_shape=tuple(jax.ShapeDtypeStruct((t, w), F32) for w in widths),
        in_specs=[pl.BlockSpec((tm, D_MODEL), lambda i: (i, 0)),
                  _const_spec((1, D_MODEL)),
                  _const_spec((N_SHARD, D_MODEL, sh_cols))],
        out_specs=tuple(pl.BlockSpec((tm, w), lambda i: (i, 0)) for w in widths),
        compiler_params=_cparams(("arbitrary",)),
    )(h, gain, w_in)


def inproj_bwd(dres, h, gain, w_in, dparts, tm=256):
    t = h.shape[0]
    widths = [IN_SPLITS[j + 1] - IN_SPLITS[j] for j in range(7)]
    sh_cols = IN_COLS // N_SHARD

    def body(dres_ref, h_ref, g_ref, w_ref, d0, d1, d2, d3, d4, d5, d6, dh_ref, dg_ref, nb_ref, dp_ref):
        hv = h_ref[...]
        g = g_ref[...]
        n, r = _rms_fwd(hv, g)
        nb_ref[...] = n.astype(BF16)
        dproj = jnp.concatenate([d[...] for d in (d0, d1, d2, d3, d4, d5, d6)], axis=1).astype(BF16)
        dp_ref[...] = dproj
        dn = jnp.zeros((tm, D_MODEL), F32)
        for s in range(N_SHARD):
            dn = dn + _dot_nt(dproj[:, s * sh_cols:(s + 1) * sh_cols], w_ref[s])
        dx, dg = _rms_bwd(hv, r, g, dn)
        dh_ref[...] = dres_ref[...] + dx
        _accum(dg_ref, dg, pl.program_id(0) == 0)

    tok = pl.BlockSpec((tm, D_MODEL), lambda i: (i, 0))
    return pl.pallas_call(
        body, name="inproj_bwd", grid=(t // tm,),
        out_shape=(jax.ShapeDtypeStruct((t, D_MODEL), F32),
                   jax.ShapeDtypeStruct((1, D_MODEL), F32),
                   jax.ShapeDtypeStruct((t, D_MODEL), BF16),
                   jax.ShapeDtypeStruct((t, IN_COLS), BF16)),
        in_specs=[tok, tok, _const_spec((1, D_MODEL)), _const_spec((N_SHARD, D_MODEL, sh_cols))]
                 + [pl.BlockSpec((tm, w), lambda i: (i, 0)) for w in widths],
        out_specs=(tok, pl.BlockSpec((1, D_MODEL), lambda i: (0, 0)), tok,
                   pl.BlockSpec((tm, IN_COLS), lambda i: (i, 0))),
        compiler_params=_cparams(("arbitrary",)),
    )(dres, h, gain, w_in, *dparts)


def s5_prep(lam_re, lam_im, log_dt, b_re, b_im, c_re, c_im):
    dt = jnp.exp(log_dt)[:, None]
    mag = jnp.exp(lam_re * dt)
    lbr = mag * jnp.cos(lam_im * dt)
    lbi = mag * jnp.sin(lam_im * dt)
    den = lam_re * lam_re + lam_im * lam_im
    nr, ni = lbr - 1.0, lbi
    kr = (nr * lam_re + ni * lam_im) / den
    ki = (ni * lam_re - nr * lam_im) / den
    bbr = kr[..., None] * b_re - ki[..., None] * b_im
    bbi = kr[..., None] * b_im + ki[..., None] * b_re
    eye = jnp.eye(16, dtype=F32)

    def bm(bp):
        return jnp.einsum('kgph,gG->kghGp', bp.reshape(S5_KT, 16, S5_STATE, S5_GROUP), eye).reshape(S5_KT, 256, 1024)

    def cm(cp):
        return jnp.einsum('kghp,gG->kgpGh', cp.reshape(S5_KT, 16, S5_GROUP, S5_STATE), eye).reshape(S5_KT, 1024, 256)

    lam_bar = jnp.stack([lbr.reshape(S5_N), lbi.reshape(S5_N)])
    bmat = jnp.stack([bm(bbr), bm(bbi)])
    cmat = jnp.stack([cm(c_re), -cm(c_im)])
    return lam_bar, bmat, cmat


def _lam_powers(lam_bar):
    lr, li = lam_bar[0], lam_bar[1]
    pr, pi = [lr], [li]
    for _ in range(7):
        pr, pi = pr + [pr[-1] * lr - pi[-1] * li], pi + [pr[-1] * li + pi[-1] * lr]
    return jnp.stack(pr), jnp.stack(pi)


def s5_fwd(u, pw_r, pw_i, bmat, cmat, dvec, tm=256):
    t = u.shape[0]
    nch = tm // 8

    def body(u_ref, pwr_ref, pwi_ref, b_ref, c_ref, d_ref, y_ref, xp_ref, x_scr, carry):
        @pl.when(pl.program_id(0) == 0)
        def _():
            carry[...] = jnp.zeros_like(carry)

        uv = u_ref[...]
        ub = uv.astype(BF16)
        for part in range(2):
            for kt in range(S5_KT):
                x_scr[:, pl.ds(part * S5_N + kt * 1024, 1024)] = _dot(ub[:, kt * 256:(kt + 1) * 256], b_ref[part, kt])
        row = lax.broadcasted_iota(jnp.int32, (8, S5_N), 0)
        pwr = pwr_ref[...]
        pwi = pwi_ref[...]

        def chunk(i, c):
            cr, ci = c
            r0 = pl.multiple_of(i * 8, 8)
            xr = x_scr[pl.ds(r0, 8), pl.ds(0, S5_N)]
            xi = x_scr[pl.ds(r0, 8), pl.ds(S5_N, S5_N)]
            for sh, idx in ((1, 0), (2, 1), (4, 3)):
                sr = jnp.where(row < sh, 0.0, pltpu.roll(xr, sh, 0))
                si = jnp.where(row < sh, 0.0, pltpu.roll(xi, sh, 0))
                lr = pwr[idx:idx + 1, :]
                li = pwi[idx:idx + 1, :]
                xr, xi = xr + lr * sr - li * si, xi + lr * si + li * sr
            xr, xi = xr + pwr * cr - pwi * ci, xi + pwr * ci + pwi * cr
            x_scr[pl.ds(r0, 8), pl.ds(0, S5_N)] = xr
            x_scr[pl.ds(r0, 8), pl.ds(S5_N, S5_N)] = xi
            xp_ref[pl.ds(r0, 8), pl.ds(0, S5_N)] = jnp.where(row == 0, cr, pltpu.roll(xr, 1, 0))
            xp_ref[pl.ds(r0, 8), pl.ds(S5_N, S5_N)] = jnp.where(row == 0, ci, pltpu.roll(xi, 1, 0))
            return xr[7:8, :], xi[7:8, :]

        cr, ci = lax.fori_loop(0, nch, chunk, (carry[0:1, :], carry[1:2, :]))
        carry[0:1, :] = cr
        carry[1:2, :] = ci
        for kt in range(S5_KT):
            acc = jnp.zeros((tm, 256), F32)
            for part in range(2):
                acc = acc + _dot(x_scr[:, pl.ds(part * S5_N + kt * 1024, 1024)].astype(BF16), c_ref[part, kt])
            y_ref[:, pl.ds(kt * 256, 256)] = acc + d_ref[:, pl.ds(kt * 256, 256)] * uv[:, kt * 256:(kt + 1) * 256]

    return pl.pallas_call(
        body, name="s5_fwd", grid=(t // tm,),
        out_shape=(jax.ShapeDtypeStruct((t, S5_WIDTH), F32),
                   jax.ShapeDtypeStruct((t, 2 * S5_N), F32)),
        in_specs=[pl.BlockSpec((tm, S5_WIDTH), lambda i: (i, 0)),
                  _const_spec((8, S5_N)), _const_spec((8, S5_N)),
                  _const_spec((2, S5_KT, 256, 1024)), _const_spec((2, S5_KT, 1024, 256)),
                  _const_spec((1, S5_WIDTH))],
        out_specs=(pl.BlockSpec((tm, S5_WIDTH), lambda i: (i, 0)),
                   pl.BlockSpec((tm, 2 * S5_N), lambda i: (i, 0))),
        scratch_shapes=[pltpu.VMEM((tm, 2 * S5_N), F32), pltpu.VMEM((8, S5_N), F32)],
        compiler_params=_cparams(("arbitrary",)),
    )(u, pw_r, pw_i, bmat, cmat, dvec)


def s5_bwd(dy, u, xp, pw_r, pw_i, pwrev_r, pwrev_i, bmat, bmat_t, cmat_t, dvec, tm=256):
    t = u.shape[0]
    nt = t // tm
    nch = tm // 8

    def body(dy_ref, u_ref, xp_ref, pwr_ref, pwi_ref, pvr_ref, pvi_ref, b_ref, bt_ref, ct_ref, d_ref,
             du_ref, db_ref, dc_ref, dl_ref, dd_ref, g_scr, x_scr, carry):
        first = pl.program_id(0) == 0

        @pl.when(first)
        def _():
            carry[...] = jnp.zeros_like(carry)
            dl_ref[...] = jnp.zeros_like(dl_ref)

        dyv = dy_ref[...]
        uv = u_ref[...]
        dyb = dyv.astype(BF16)
        ub = uv.astype(BF16)
        lr1 = pwr_ref[0:1, :]
        li1 = pwi_ref[0:1, :]
        for kt in range(S5_KT):
            cols = pl.ds(kt * 1024, 1024)
            colsi = pl.ds(S5_N + kt * 1024, 1024)
            g_scr[:, cols] = _dot(dyb[:, kt * 256:(kt + 1) * 256], ct_ref[0, kt])
            g_scr[:, colsi] = _dot(dyb[:, kt * 256:(kt + 1) * 256], ct_ref[1, kt])
            bur = _dot(ub[:, kt * 256:(kt + 1) * 256], b_ref[0, kt])
            bui = _dot(ub[:, kt * 256:(kt + 1) * 256], b_ref[1, kt])
            xpr = xp_ref[:, cols]
            xpi = xp_ref[:, colsi]
            lrk = lr1[:, kt * 1024:(kt + 1) * 1024]
            lik = li1[:, kt * 1024:(kt + 1) * 1024]
            x_scr[:, cols] = lrk * xpr - lik * xpi + bur
            x_scr[:, colsi] = lrk * xpi + lik * xpr + bui

        row = lax.broadcasted_iota(jnp.int32, (8, S5_N), 0)
        pwr = pwr_ref[...]
        pwi = pwi_ref[...]
        pvr = pvr_ref[...]
        pvi = pvi_ref[...]

        def chunk(j, c):
            cr, ci = c
            r0 = pl.multiple_of((nch - 1 - j) * 8, 8)
            gr = g_scr[pl.ds(r0, 8), pl.ds(0, S5_N)]
            gi = g_scr[pl.ds(r0, 8), pl.ds(S5_N, S5_N)]
            for sh, idx in ((1, 0), (2, 1), (4, 3)):
                sr = jnp.where(row >= 8 - sh, 0.0, pltpu.roll(gr, 8 - sh, 0))
                si = jnp.where(row >= 8 - sh, 0.0, pltpu.roll(gi, 8 - sh, 0))
                lr = pwr[idx:idx + 1, :]
                li = pwi[idx:idx + 1, :]
                gr, gi = gr + lr * sr + li * si, gi + lr * si - li * sr
            gr, gi = gr + pvr * cr + pvi * ci, gi + pvr * ci - pvi * cr
            g_scr[pl.ds(r0, 8), pl.ds(0, S5_N)] = gr
            g_scr[pl.ds(r0, 8), pl.ds(S5_N, S5_N)] = gi
            xpr = xp_ref[pl.ds(r0, 8), pl.ds(0, S5_N)]
            xpi = xp_ref[pl.ds(r0, 8), pl.ds(S5_N, S5_N)]
            dl_ref[0] += gr * xpr + gi * xpi
            dl_ref[1] += gi * xpr - gr * xpi
            return gr[0:1, :], gi[0:1, :]

        cr, ci = lax.fori_loop(0, nch, chunk, (carry[0:1, :], carry[1:2, :]))
        carry[0:1, :] = cr
        carry[1:2, :] = ci

        for kt in range(S5_KT):
            du = jnp.zeros((tm, 256), F32)
            ukt = ub[:, kt * 256:(kt + 1) * 256]
            dykt = dyb[:, kt * 256:(kt + 1) * 256]
            for part in range(2):
                gb = g_scr[:, pl.ds(part * S5_N + kt * 1024, 1024)].astype(BF16)
                xb = x_scr[:, pl.ds(part * S5_N + kt * 1024, 1024)].astype(BF16)
                du = du + _dot(gb, bt_ref[part, kt])
                dbv = _dot_tn(ukt, gb)
                dcv = _dot_tn(xb, dykt)

                @pl.when(first)
                def _():
                    db_ref[part, kt] = dbv
                    dc_ref[part, kt] = dcv

                @pl.when(jnp.logical_not(first))
                def _():
                    db_ref[part, kt] += dbv
                    dc_ref[part, kt] += dcv
            du_ref[:, pl.ds(kt * 256, 256)] = du + d_ref[:, pl.ds(kt * 256, 256)] * dyv[:, kt * 256:(kt + 1) * 256]
        _accum(dd_ref, jnp.sum(dyv * uv, axis=0, keepdims=True), first)

    rev = lambda i: (nt - 1 - i, 0)
    return pl.pallas_call(
        body, name="s5_bwd", grid=(nt,),
        out_shape=(jax.ShapeDtypeStruct((t, S5_WIDTH), F32),
                   jax.ShapeDtypeStruct((2, S5_KT, 256, 1024), F32),
                   jax.ShapeDtypeStruct((2, S5_KT, 1024, 256), F32),
                   jax.ShapeDtypeStruct((2, 8, S5_N), F32),
                   jax.ShapeDtypeStruct((1, S5_WIDTH), F32)),
        in_specs=[pl.BlockSpec((tm, S5_WIDTH), rev), pl.BlockSpec((tm, S5_WIDTH), rev),
                  pl.BlockSpec((tm, 2 * S5_N), rev),
                  _const_spec((8, S5_N)), _const_spec((8, S5_N)), _const_spec((8, S5_N)), _const_spec((8, S5_N)),
                  _const_spec((2, S5_KT, 256, 1024)), _const_spec((2, S5_KT, 1024, 256)),
                  _const_spec((2, S5_KT, 256, 1024)), _const_spec((1, S5_WIDTH))],
        out_specs=(pl.BlockSpec((tm, S5_WIDTH), rev),
                   pl.BlockSpec((2, S5_KT, 256, 1024), lambda i: (0, 0, 0, 0)),
                   pl.BlockSpec((2, S5_KT, 1024, 256), lambda i: (0, 0, 0, 0)),
                   pl.BlockSpec((2, 8, S5_N), lambda i: (0, 0, 0)),
                   pl.BlockSpec((1, S5_WIDTH), lambda i: (0, 0))),
        scratch_shapes=[pltpu.VMEM((tm, 2 * S5_N), F32), pltpu.VMEM((tm, 2 * S5_N), F32),
                        pltpu.VMEM((8, S5_N), F32)],
        compiler_params=_cparams(("arbitrary",)),
    )(dy, u, xp, pw_r, pw_i, pwrev_r, pwrev_i, bmat, bmat_t, cmat_t, dvec)


def _hg_gates(z, lb):
    sg = _sigmoid(z)
    sgn = _sigmoid(-z)
    fg = lb + (1.0 - lb) * sg
    return sg, sgn, fg, jnp.log(fg), (1.0 - lb) * sgn


def _hg_decays(g, tril):
    gc = jnp.dot(tril, g, precision=HIGHEST, preferred_element_type=F32)
    mid = gc[CHUNK // 2 - 1:CHUNK // 2, :]
    last = gc[CHUNK - 1:CHUNK, :]
    return jnp.exp(gc), jnp.exp(gc - mid), jnp.exp(mid - gc), jnp.exp(last - gc), jnp.exp(last)


def _split_bf16(x):
    hi = x.astype(BF16)
    return hi, (x - hi.astype(F32)).astype(BF16)


def _hg_scores(qt, qlo, kt, klo, sl, causal):
    a = _dot_nt(qt[:, sl], kt[:, sl]) + _dot_nt(qt[:, sl], klo[:, sl]) + _dot_nt(qlo[:, sl], kt[:, sl])
    return jnp.where(causal, a, 0.0).astype(BF16)


def hgrn_fwd(q, f, v, lb):
    t = q.shape[0]
    nc = t // CHUNK
    scale = HG_E ** -0.5

    def body(q_ref, f_ref, v_ref, lb_ref, o_ref, st_ref, state):
        @pl.when(pl.program_id(0) == 0)
        def _():
            state[...] = jnp.zeros_like(state)

        ri = lax.broadcasted_iota(jnp.int32, (CHUNK, CHUNK), 0)
        ci = lax.broadcasted_iota(jnp.int32, (CHUNK, CHUNK), 1)
        causal = ri >= ci
        tril = causal.astype(F32)
        _, _, _, g, k = _hg_gates(f_ref[...], lb_ref[...])
        eg, eq, ek, ed, el = _hg_decays(g, tril)
        qs = q_ref[...] * scale
        qg = (qs * eg).astype(BF16)
        qt, qlo = _split_bf16(qs * eq)
        kt, klo = _split_bf16(k * ek)
        kd = (k * ed).astype(BF16)
        vb = v_ref[...].astype(BF16)
        for h in range(HG_HEADS):
            sl = slice(h * HG_E, (h + 1) * HG_E)
            st = state[h]
            a = _hg_scores(qt, qlo, kt, klo, sl, causal)
            o_ref[:, sl] = _dot(a, vb[:, sl]) + _dot_nt(qg[:, sl], st.astype(BF16))
            st_new = st * el[:, sl] + _dot_tn(vb[:, sl], kd[:, sl])
            state[h] = st_new
            st_ref[0, h] = st_new

    tok = pl.BlockSpec((CHUNK, HG_WIDTH), lambda i: (i, 0))
    return pl.pallas_call(
        body, name="hgrn_fwd", grid=(nc,),
        out_shape=(jax.ShapeDtypeStruct((t, HG_WIDTH), F32),
                   jax.ShapeDtypeStruct((nc, HG_HEADS, HG_E, HG_E), F32)),
        in_specs=[tok, tok, tok, _const_spec((1, HG_WIDTH))],
        out_specs=(tok, pl.BlockSpec((1, HG_HEADS, HG_E, HG_E), lambda i: (i, 0, 0, 0))),
        scratch_shapes=[pltpu.VMEM((HG_HEADS, HG_E, HG_E), F32)],
        compiler_params=_cparams(("arbitrary",)),
    )(q, f, v, lb)


def hgrn_bwd(do, q, f, v, lb, states):
    t = q.shape[0]
    nc = t // CHUNK
    scale = HG_E ** -0.5

    def body(do_ref, q_ref, f_ref, v_ref, lb_ref, s0_ref, dq_ref, df_ref, dv_ref, dlb_ref, dstate):
        first = pl.program_id(0) == 0
        c_idx = nc - 1 - pl.program_id(0)

        @pl.when(first)
        def _():
            dstate[...] = jnp.zeros_like(dstate)

        ri = lax.broadcasted_iota(jnp.int32, (CHUNK, CHUNK), 0)
        ci = lax.broadcasted_iota(jnp.int32, (CHUNK, CHUNK), 1)
        causal = ri >= ci
        tril = causal.astype(F32)
        triu = (ri <= ci).astype(F32)
        lb = lb_ref[...]
        sg, sgn, fg, g, k = _hg_gates(f_ref[...], lb)
        eg, eq, ek, ed, el = _hg_decays(g, tril)
        qs = q_ref[...] * scale
        qg = (qs * eg).astype(BF16)
        qt, qlo = _split_bf16(qs * eq)
        kt, klo = _split_bf16(k * ek)
        kd = (k * ed).astype(BF16)
        vb = v_ref[...].astype(BF16)
        dob = do_ref[...].astype(BF16)
        has_prev = jnp.where(c_idx > 0, 1.0, 0.0)
        dqs_l, dk_l, dgc_l, dgl_l = [], [], [], []
        for h in range(HG_HEADS):
            sl = slice(h * HG_E, (h + 1) * HG_E)
            s0 = s0_ref[0, h] * has_prev
            ds1 = dstate[h]
            ds1b = ds1.astype(BF16)
            a = _hg_scores(qt, qlo, kt, klo, sl, causal)
            da = jnp.where(causal, _dot_nt(dob[:, sl], vb[:, sl]), 0.0).astype(BF16)
            dv_ref[:, sl] = _dot_tn(a, dob[:, sl]) + _dot_nt(kd[:, sl], ds1b)
            dkd = _dot(vb[:, sl], ds1b)
            dqt = _dot(da, kt[:, sl])
            dkt = _dot_tn(da, qt[:, sl])
            dqg = _dot(dob[:, sl], s0.astype(BF16))
            dqs_l.append(dqt * eq[:, sl] + dqg * eg[:, sl])
            dk_l.append(dkt * ek[:, sl] + dkd * ed[:, sl])
            kd_dkd = kd[:, sl].astype(F32) * dkd
            dgc_l.append(qt[:, sl].astype(F32) * dqt - kt[:, sl].astype(F32) * dkt
                         + qg[:, sl].astype(F32) * dqg - kd_dkd)
            dgl_l.append(el[:, sl] * jnp.sum(ds1 * s0, axis=0, keepdims=True)
                         + jnp.sum(kd_dkd, axis=0, keepdims=True))
            dstate[h] = ds1 * el[:, sl] + _dot_tn(dob[:, sl], qg[:, sl])
        dqs = jnp.concatenate(dqs_l, axis=1)
        dk = jnp.concatenate(dk_l, axis=1)
        dgl = jnp.concatenate(dgl_l, axis=1)
        dq_ref[...] = dqs * scale
        rowc = lax.broadcasted_iota(jnp.int32, (CHUNK, HG_WIDTH), 0)
        dgc = jnp.concatenate(dgc_l, axis=1) + jnp.where(rowc == CHUNK - 1, dgl, 0.0)
        dg = jnp.dot(triu, dgc, precision=HIGHEST, preferred_element_type=F32)
        w = dg / fg - dk
        df_ref[...] = w * (1.0 - lb) * sg * sgn
        _accum(dlb_ref, jnp.sum(w * sgn, axis=0, keepdims=True), first)

    rev = lambda i: (nc - 1 - i, 0)
    tok = pl.BlockSpec((CHUNK, HG_WIDTH), rev)
    return pl.pallas_call(
        body, name="hgrn_bwd", grid=(nc,),
        out_shape=(jax.ShapeDtypeStruct((t, HG_WIDTH), F32),
                   jax.ShapeDtypeStruct((t, HG_WIDTH), F32),
                   jax.ShapeDtypeStruct((t, HG_WIDTH), F32),
                   jax.ShapeDtypeStruct((1, HG_WIDTH), F32)),
        in_specs=[tok, tok, tok, tok, _const_spec((1, HG_WIDTH)),
                  pl.BlockSpec((1, HG_HEADS, HG_E, HG_E), lambda i: (jnp.maximum(nc - 2 - i, 0), 0, 0, 0))],
        out_specs=(tok, tok, tok, pl.BlockSpec((1, HG_WIDTH), lambda i: (0, 0))),
        scratch_shapes=[pltpu.VMEM((HG_HEADS, HG_E, HG_E), F32)],
        compiler_params=_cparams(("arbitrary",)),
    )(do, q, f, v, lb, states)


GELU_C = math.sqrt(2.0 / math.pi)


def _gelu(x):
    th = jnp.tanh(GELU_C * (x + 0.044715 * x * x * x))
    return 0.5 * x * (1.0 + th), th


def _merge_core(ys5, o, og, ga, gb, wv_ref, wt_ref, ghg, who_ref):
    ys, th = _gelu(ys5)
    ysb = ys.astype(BF16)
    va = jnp.concatenate([_dot(ysb, wv_ref[s]) for s in range(N_SHARD)], axis=1)
    vt = jnp.concatenate([_dot(ysb, wt_ref[s]) for s in range(N_SHARD)], axis=1)
    svt = _sigmoid(vt)
    ya = va * svt
    rs, ons = [], []
    for h in range(HG_HEADS):
        oh = o[:, h * HG_E:(h + 1) * HG_E]
        r = lax.rsqrt(jnp.mean(oh * oh, axis=-1, keepdims=True) + NORM_EPS)
        rs.append(r)
        ons.append(oh * r)
    on = jnp.concatenate(ons, axis=1)
    sgo = _sigmoid(og)
    o2 = on * ghg * (og * sgo)
    o2b = o2.astype(BF16)
    yb = _dot(o2b, who_ref[...])
    sa = _sigmoid(ga)
    sb = _sigmoid(gb)
    mixed = sa * ya + sb * yb
    return dict(ys=ys, th=th, ysb=ysb, va=va, svt=svt, ya=ya, rs=rs, on=on, sgo=sgo, o2b=o2b, yb=yb,
                sa=sa, sb=sb, mixed=mixed)


def merge_fwd(h, ys5, o, og, ga, gb, wv, wt, ghg, who, wmo, tm=256):
    t = h.shape[0]

    def body(h_ref, ys5_ref, o_ref, og_ref, ga_ref, gb_ref, wv_ref, wt_ref, ghg_ref, who_ref, wmo_ref, out_ref):
        c = _merge_core(ys5_ref[...], o_ref[...], og_ref[...], ga_ref[...], gb_ref[...],
                        wv_ref, wt_ref, ghg_ref[...], who_ref)
        out_ref[...] = h_ref[...] + _dot(c["mixed"].astype(BF16), wmo_ref[...])

    tok = pl.BlockSpec((tm, D_MODEL), lambda i: (i, 0))
    return pl.pallas_call(
        body, name="merge_fwd", grid=(t // tm,),
        out_shape=jax.ShapeDtypeStruct((t, D_MODEL), F32),
        in_specs=[tok, pl.BlockSpec((tm, S5_WIDTH), lambda i: (i, 0)), tok, tok, tok, tok,
                  _const_spec((N_SHARD, S5_WIDTH, 256)), _const_spec((N_SHARD, S5_WIDTH, 256)),
                  _const_spec((1, HG_WIDTH)), _const_spec((HG_WIDTH, D_MODEL)), _const_spec((D_MODEL, D_MODEL))],
        out_specs=tok,
        compiler_params=_cparams(("arbitrary",)),
    )(h, ys5, o, og, ga, gb, wv, wt, ghg, who, wmo)


def merge_bwd(dh, ys5, o, og, ga, gb, wv, wt, ghg, who, wmo, tm=256):
    t = dh.shape[0]

    def body(dh_ref, ys5_ref, o_ref, og_ref, ga_ref, gb_ref, wv_ref, wt_ref, ghg_ref, who_ref, wmo_ref,
             dys5_ref, do_ref, dog_ref, dga_ref, dgb_ref, dghg_ref,
             mixb_ref, dhb_ref, ysb_ref, dvab_ref, dvtb_ref, o2b_ref, dybb_ref):
        ys5 = ys5_ref[...]
        o = o_ref[...]
        og = og_ref[...]
        ghg = ghg_ref[...]
        c = _merge_core(ys5, o, og, ga_ref[...], gb_ref[...], wv_ref, wt_ref, ghg, who_ref)
        dhb = dh_ref[...].astype(BF16)
        dhb_ref[...] = dhb
        mixb_ref[...] = c["mixed"].astype(BF16)
        ysb_ref[...] = c["ysb"]
        o2b_ref[...] = c["o2b"]
        dmix = _dot_nt(dhb, wmo_ref[...])
        sa, sb = c["sa"], c["sb"]
        dya = dmix * sa
        dyb = dmix * sb
        dga_ref[...] = dmix * c["ya"] * sa * (1.0 - sa)
        dgb_ref[...] = dmix * c["yb"] * sb * (1.0 - sb)
        svt = c["svt"]
        dva = (dya * svt).astype(BF16)
        dvt = (dya * c["va"] * svt * (1.0 - svt)).astype(BF16)
        dvab_ref[...] = dva
        dvtb_ref[...] = dvt
        dys = jnp.zeros((tm, S5_WIDTH), F32)
        for s in range(N_SHARD):
            dys = dys + _dot_nt(dva[:, s * 256:(s + 1) * 256], wv_ref[s]) + _dot_nt(dvt[:, s * 256:(s + 1) * 256], wt_ref[s])
        th = c["th"]
        dgelu = 0.5 * (1.0 + th) + 0.5 * ys5 * (1.0 - th * th) * GELU_C * (1.0 + 3.0 * 0.044715 * ys5 * ys5)
        dys5_ref[...] = dys * dgelu
        dybb = dyb.astype(BF16)
        dybb_ref[...] = dybb
        do2 = _dot_nt(dybb, who_ref[...])
        sgo = c["sgo"]
        sil = og * sgo
        on = c["on"]
        dog_ref[...] = do2 * on * ghg * (sgo * (1.0 + og * (1.0 - sgo)))
        _accum(dghg_ref, jnp.sum(do2 * on * sil, axis=0, keepdims=True), pl.program_id(0) == 0)
        don = do2 * ghg * sil
        dos = []
        for h in range(HG_HEADS):
            sl = slice(h * HG_E, (h + 1) * HG_E)
            m = jnp.mean(don[:, sl] * on[:, sl], axis=-1, keepdims=True)
            dos.append(c["rs"][h] * (don[:, sl] - on[:, sl] * m))
        do_ref[...] = jnp.concatenate(dos, axis=1)

    tok = pl.BlockSpec((tm, D_MODEL), lambda i: (i, 0))
    s5b = pl.BlockSpec((tm, S5_WIDTH), lambda i: (i, 0))
    f32t = jax.ShapeDtypeStruct((t, D_MODEL), F32)
    bft = jax.ShapeDtypeStruct((t, D_MODEL), BF16)
    return pl.pallas_call(
        body, name="merge_bwd", grid=(t // tm,),
        out_shape=(jax.ShapeDtypeStruct((t, S5_WIDTH), F32), f32t, f32t, f32t, f32t,
                   jax.ShapeDtypeStruct((1, HG_WIDTH), F32),
                   bft, bft, jax.ShapeDtypeStruct((t, S5_WIDTH), BF16), bft, bft, bft, bft),
        in_specs=[tok, s5b, tok, tok, tok, tok,
                  _const_spec((N_SHARD, S5_WIDTH, 256)), _const_spec((N_SHARD, S5_WIDTH, 256)),
                  _const_spec((1, HG_WIDTH)), _const_spec((HG_WIDTH, D_MODEL)), _const_spec((D_MODEL, D_MODEL))],
        out_specs=(s5b, tok, tok, tok, tok, pl.BlockSpec((1, HG_WIDTH), lambda i: (0, 0)),
                   tok, tok, s5b, tok, tok, tok, tok),
        compiler_params=_cparams(("arbitrary",)),
    )(dh, ys5, o, og, ga, gb, wv, wt, ghg, who, wmo)


def head_fwd_bwd(h, p, tgt, gple, wpg, wpp, gfin, tm=256):
    t = h.shape[0]

    def body(h_ref, p_ref, tgt_ref, gple_ref, wpg_ref, wpp_ref, gfin_ref,
             loss_ref, dh_ref, dgple_ref, dgfin_ref, nb_ref, dlb_ref, dppb_ref):
        first = pl.program_id(0) == 0
        hv = h_ref[...]
        gple = gple_ref[...]
        gfin = gfin_ref[...]
        n, r3 = _rms_fwd(hv, gple)
        nb = n.astype(BF16)
        nb_ref[...] = nb
        pg = _sigmoid(_dot(nb, wpg_ref[...]))
        pb = p_ref[...].astype(BF16)
        pp = jnp.concatenate([_dot(pb, wpp_ref[s]) for s in range(N_SHARD)], axis=1)
        h4 = hv + pg * pp
        y, r4 = _rms_fwd(h4, gfin)
        err = y - tgt_ref[...]
        lsum = 0.5 * jnp.sum(jnp.sum(err * err, axis=-1, keepdims=True), axis=0, keepdims=True) / D_MODEL
        _accum(loss_ref, jnp.broadcast_to(lsum, (8, 128)), first)
        dy = err * (1.0 / D_MODEL)
        dh4, dgf = _rms_bwd(h4, r4, gfin, dy)
        _accum(dgfin_ref, dgf, first)
        dpp = dh4 * pg
        dppb_ref[...] = dpp.astype(BF16)
        dl = (dh4 * pp * pg * (1.0 - pg)).astype(BF16)
        dlb_ref[...] = dl
        dn = _dot_nt(dl, wpg_ref[...])
        dx, dgp = _rms_bwd(hv, r3, gple, dn)
        _accum(dgple_ref, dgp, first)
        dh_ref[...] = dh4 + dx

    tok = pl.BlockSpec((tm, D_MODEL), lambda i: (i, 0))
    vec = pl.BlockSpec((1, D_MODEL), lambda i: (0, 0))
    bft = jax.ShapeDtypeStruct((t, D_MODEL), BF16)
    return pl.pallas_call(
        body, name="head_fwd_bwd", grid=(t // tm,),
        out_shape=(jax.ShapeDtypeStruct((8, 128), F32), jax.ShapeDtypeStruct((t, D_MODEL), F32),
                   jax.ShapeDtypeStruct((1, D_MODEL), F32), jax.ShapeDtypeStruct((1, D_MODEL), F32),
                   bft, bft, bft),
        in_specs=[tok, pl.BlockSpec((tm, PLE_DIM), lambda i: (i, 0)), tok,
                  _const_spec((1, D_MODEL)), _const_spec((D_MODEL, D_MODEL)),
                  _const_spec((N_SHARD, PLE_DIM, 256)), _const_spec((1, D_MODEL))],
        out_specs=(pl.BlockSpec((8, 128), lambda i: (0, 0)), tok, vec, vec, tok, tok, tok),
        compiler_params=_cparams(("arbitrary",)),
    )(h, p, tgt, gple, wpg, wpp, gfin)


BIG = ("ffn1_w_gate", "ffn1_w_up", "ffn1_w_down", "w_in", "s5_glu_val", "s5_glu_gate", "hg_w_out",
       "w_merge_out", "ffn2_w_gate", "ffn2_w_up", "ffn2_w_down", "ple_w_gate", "ple_w_proj")
FFN_T = ("ffn1_w_gate", "ffn1_w_up", "ffn2_w_gate", "ffn2_w_up")
BIG_SHARD = {
    "ffn1_w_gate": (FF_PAD, D_MODEL), "ffn1_w_up": (FF_PAD, D_MODEL), "ffn1_w_down": (FF_PAD, D_MODEL),
    "ffn2_w_gate": (FF_PAD, D_MODEL), "ffn2_w_up": (FF_PAD, D_MODEL), "ffn2_w_down": (FF_PAD, D_MODEL),
    "w_in": (D_MODEL, IN_COLS // N_SHARD), "s5_glu_val": (S5_WIDTH, 256), "s5_glu_gate": (S5_WIDTH, 256),
    "hg_w_out": (256, D_MODEL), "w_merge_out": (256, D_MODEL), "ple_w_gate": (256, D_MODEL),
    "ple_w_proj": (PLE_DIM, 256),
}


def _lower_bound(hb):
    return jax.nn.softmax(hb, axis=0)[0:1]


def local_step(x, p, tgt, wts, sm):
    rows_full = lambda w: w.reshape(N_SHARD * w.shape[1], w.shape[2])
    lb, lb_vjp = jax.vjp(_lower_bound, sm["hg_lower_bound"])
    s5_names = ("s5_lam_re", "s5_lam_im", "s5_log_dt", "s5_b_re", "s5_b_im", "s5_c_re", "s5_c_im")
    (lam_bar, bmat, cmat), s5_vjp = jax.vjp(s5_prep, *[sm[k] for k in s5_names])
    pw_r, pw_i = _lam_powers(lam_bar)
    bmat_b = bmat.astype(BF16)
    cmat_b = cmat.astype(BF16)
    bmat_t = jnp.swapaxes(bmat, -1, -2).astype(BF16)
    cmat_t = jnp.swapaxes(cmat, -1, -2).astype(BF16)
    who = rows_full(wts["hg_w_out"])
    wmo = rows_full(wts["w_merge_out"])
    wpg = rows_full(wts["ple_w_gate"])

    h1, a1, b1 = ffn_fwd(x, sm["ffn1_norm"], wts["ffn1_w_gate"], wts["ffn1_w_up"], wts["ffn1_w_down"], "ffn1_fwd")
    s5in, q, f, v, og, ga, gb = inproj_fwd(h1, sm["mix_norm"], wts["w_in"])
    ys5, xp = s5_fwd(s5in, pw_r, pw_i, bmat_b, cmat_b, sm["s5_d"])
    o, states = hgrn_fwd(q, f, v, lb)
    h2 = merge_fwd(h1, ys5, o, og, ga, gb, wts["s5_glu_val"], wts["s5_glu_gate"], sm["hg_out_norm"], who, wmo)
    h3, a2, b2 = ffn_fwd(h2, sm["ffn2_norm"], wts["ffn2_w_gate"], wts["ffn2_w_up"], wts["ffn2_w_down"], "ffn2_fwd")
    loss, dh3, d_ple_norm, d_final_norm, npb, dlgb, dppb = head_fwd_bwd(
        h3, p, tgt, sm["ple_norm"], wpg, wts["ple_w_proj"], sm["final_norm"])

    gb_ = {}
    gs = {"ple_norm": d_ple_norm, "final_norm": d_final_norm}
    gb_["ple_w_gate"] = tn_matmul(npb, dlgb, "g_ple_w_gate", "rows")
    gb_["ple_w_proj"] = tn_matmul(p, dppb, "g_ple_w_proj", "cols")

    dh2, gs["ffn2_norm"], n2b, dhb2, da2, db2, s2 = ffn_bwd(
        dh3, h2, a2, b2, sm["ffn2_norm"], wts["ffn2_w_gate"], wts["ffn2_w_up"], wts["ffn2_w_down"], "ffn2_bwd")
    gb_["ffn2_w_gate"] = tn_matmul(da2, n2b, "g_ffn2_w_gate", "rows")
    gb_["ffn2_w_up"] = tn_matmul(db2, n2b, "g_ffn2_w_up", "rows")
    gb_["ffn2_w_down"] = tn_matmul(s2, dhb2, "g_ffn2_w_down", "rows")

    dys5, do, dog, dga, dgb, gs["hg_out_norm"], mixb, dh2b, ysb, dvab, dvtb, o2b, dybb = merge_bwd(
        dh2, ys5, o, og, ga, gb, wts["s5_glu_val"], wts["s5_glu_gate"], sm["hg_out_norm"], who, wmo)
    gb_["w_merge_out"] = tn_matmul(mixb, dh2b, "g_w_merge_out", "rows")
    gb_["s5_glu_val"] = tn_matmul(ysb, dvab, "g_s5_glu_val", "cols")
    gb_["s5_glu_gate"] = tn_matmul(ysb, dvtb, "g_s5_glu_gate", "cols")
    gb_["hg_w_out"] = tn_matmul(o2b, dybb, "g_hg_w_out", "rows")

    dq, df, dv, dlb = hgrn_bwd(do, q, f, v, lb, states)
    (gs["hg_lower_bound"],) = lb_vjp(dlb)
    du, dbmat, dcmat, dlam8, gs["s5_d"] = s5_bwd(
        dys5, s5in, xp, pw_r, pw_i, pw_r[::-1], pw_i[::-1], bmat_b, bmat_t, cmat_t, sm["s5_d"])
    for k, g in zip(s5_names, s5_vjp((jnp.sum(dlam8, axis=1), dbmat, dcmat))):
        gs[k] = g

    dh1, gs["mix_norm"], nmb, dprojb = inproj_bwd(dh2, h1, sm["mix_norm"], wts["w_in"], (du, dq, df, dv, dog, dga, dgb))
    gb_["w_in"] = tn_matmul(nmb, dprojb, "g_w_in", "cols")

    dx, gs["ffn1_norm"], n1b, dhb1, da1, db1, s1 = ffn_bwd(
        dh1, x, a1, b1, sm["ffn1_norm"], wts["ffn1_w_gate"], wts["ffn1_w_up"], wts["ffn1_w_down"], "ffn1_bwd")
    gb_["ffn1_w_gate"] = tn_matmul(da1, n1b, "g_ffn1_w_gate", "rows")
    gb_["ffn1_w_up"] = tn_matmul(db1, n1b, "g_ffn1_w_up", "rows")
    gb_["ffn1_w_down"] = tn_matmul(s1, dhb1, "g_ffn1_w_down", "rows")
    return loss, dx, gb_, gs


MESH = pl.DeviceIdType.MESH
ANY = pl.BlockSpec(memory_space=pl.ANY)


def _place():
    x, y, c = lax.axis_index("x"), lax.axis_index("y"), lax.axis_index("c")
    return x, y, c


def _remote(src, dst, ssem, rsem, dev):
    return pltpu.make_async_remote_copy(src_ref=src, dst_ref=dst, send_sem=ssem, recv_sem=rsem,
                                        device_id=dev, device_id_type=MESH)


def allgather_weights(bufs):
    n = len(bufs)

    def body(*refs):
        outs = refs[n:2 * n]
        s_own, r_own, s_fwd, r_fwd, s_sib, r_sib = refs[2 * n:]
        x, y, c = _place()
        me = 2 * x + y
        nbr = ((1 - x, y), (x, 1 - y))
        nbr_id = (2 * (1 - x) + y, 2 * x + (1 - y))
        diag_id = 2 * (1 - x) + (1 - y)
        sib = (x, y, 1 - c)

        def rows(w, q=None):
            r = outs[w].shape[1]
            if q is None:
                return pl.ds(pl.multiple_of(c * (r // 2), 16), r // 2)
            return pl.ds(pl.multiple_of(c * (r // 2) + q * (r // 4), 16), r // 4)

        def own(w, j):
            piece = outs[w].at[me, rows(w)]
            return _remote(piece, piece, s_own.at[w, j], r_own.at[w, j], (nbr[j][0], nbr[j][1], c))

        def from_nbr(w, j):
            piece = outs[w].at[nbr_id[j], rows(w)]
            return _remote(piece, piece, s_own.at[w, j], r_own.at[w, j], (nbr[j][0], nbr[j][1], c))

        def fwd(w, j):
            piece = outs[w].at[nbr_id[j], rows(w, j)]
            return _remote(piece, piece, s_fwd.at[w, j], r_fwd.at[w, j], (nbr[1 - j][0], nbr[1 - j][1], c))

        def from_diag(w, j):
            piece = outs[w].at[diag_id, rows(w, j)]
            return _remote(piece, piece, s_fwd.at[w, j], r_fwd.at[w, j], (nbr[1 - j][0], nbr[1 - j][1], c))

        def to_sib(w, k):
            piece = (outs[w].at[nbr_id[k], rows(w)] if k < 2 else outs[w].at[diag_id, rows(w, k - 2)])
            return _remote(piece, piece, s_sib.at[w, k], r_sib.at[w, k], sib)

        def from_sib(w, k):
            r = outs[w].shape[1]
            if k < 2:
                piece = outs[w].at[nbr_id[k], pl.ds(pl.multiple_of((1 - c) * (r // 2), 16), r // 2)]
            else:
                piece = outs[w].at[diag_id, pl.ds(pl.multiple_of((1 - c) * (r // 2) + (k - 2) * (r // 4), 16), r // 4)]
            return _remote(piece, piece, s_sib.at[w, k], r_sib.at[w, k], sib)

        for w in range(n):
            own(w, 0).start()
            own(w, 1).start()
        for w in range(n):
            for j in range(2):
                from_nbr(w, j).wait_recv()
                fwd(w, j).start()
                to_sib(w, j).start()
        for w in range(n):
            for j in range(2):
                from_diag(w, j).wait_recv()
                to_sib(w, 2 + j).start()
        for w in range(n):
            for k in range(4):
                from_sib(w, k).wait_recv()
        for w in range(n):
            for j in range(2):
                own(w, j).wait_send()
                fwd(w, j).wait_send()
            for k in range(4):
                to_sib(w, k).wait_send()

    dma = pltpu.SemaphoreType.DMA
    return pl.pallas_call(
        body, name="allgather_weights",
        out_shape=tuple(jax.ShapeDtypeStruct(b.shape, b.dtype) for b in bufs),
        in_specs=[ANY] * n, out_specs=tuple([ANY] * n),
        input_output_aliases={w: w for w in range(n)},
        scratch_shapes=[dma((n, 2)), dma((n, 2)), dma((n, 2)), dma((n, 2)), dma((n, 4)), dma((n, 4))],
    )(*bufs)


def exchange_halves(grads):
    n = len(grads)

    def body(*refs):
        ins, outs = refs[:n], refs[n:2 * n]
        ssem, rsem = refs[2 * n:]
        x, y, c = _place()
        cps = []
        for w in range(n):
            half = ins[w].shape[1] // 2
            src = ins[w].at[:, pl.ds(pl.multiple_of((1 - c) * half, 8), half), :]
            cps.append(_remote(src, outs[w], ssem.at[w], rsem.at[w], (x, y, 1 - c)))
            cps[w].start()
        for w in range(n):
            cps[w].wait()

    dma = pltpu.SemaphoreType.DMA
    return pl.pallas_call(
        body, name="exchange_halves",
        out_shape=tuple(jax.ShapeDtypeStruct((N_SHARD, g.shape[1] // 2, g.shape[2]), g.dtype) for g in grads),
        in_specs=[ANY] * n, out_specs=tuple([ANY] * n),
        scratch_shapes=[dma((n,)), dma((n,))],
    )(*grads)


def scatter_chip_sums(sums):
    n = len(sums)

    def body(*refs):
        ins, outs = refs[:n], refs[n:2 * n]
        ssem, rsem = refs[2 * n:]
        x, y, c = _place()
        chips = ((1 - x, y), (x, 1 - y), (1 - x, 1 - y))
        cps = []
        for w in range(n):
            for j, ch in enumerate(chips):
                cp = _remote(ins[w].at[2 * ch[0] + ch[1]], outs[w].at[j], ssem.at[w, j], rsem.at[w, j],
                             (ch[0], ch[1], c))
                cp.start()
                cps.append(cp)
        for cp in cps:
            cp.wait()

    dma = pltpu.SemaphoreType.DMA
    return pl.pallas_call(
        body, name="scatter_chip_sums",
        out_shape=tuple(jax.ShapeDtypeStruct((3,) + s.shape[1:], s.dtype) for s in sums),
        in_specs=[ANY] * n, out_specs=tuple([ANY] * n),
        scratch_shapes=[dma((n, 3)), dma((n, 3))],
    )(*sums)


def join_halves(shards):
    n = len(shards)

    def body(*refs):
        outs = refs[n:2 * n]
        ssem, rsem = refs[2 * n:]
        x, y, c = _place()
        cps = []
        for w in range(n):
            half = outs[w].shape[0] // 2
            mine = outs[w].at[pl.ds(pl.multiple_of(c * half, 8), half), :]
            cps.append(_remote(mine, mine, ssem.at[w], rsem.at[w], (x, y, 1 - c)))
            cps[w].start()
        for w in range(n):
            cps[w].wait()

    dma = pltpu.SemaphoreType.DMA
    return pl.pallas_call(
        body, name="join_halves",
        out_shape=tuple(jax.ShapeDtypeStruct(s.shape, s.dtype) for s in shards),
        in_specs=[ANY] * n, out_specs=tuple([ANY] * n),
        input_output_aliases={w: w for w in range(n)},
        scratch_shapes=[dma((n,)), dma((n,))],
    )(*shards)


def allreduce_small(vec):
    rows = vec.shape[0]

    def body(v_ref, o_ref, buf, ssem, rsem):
        x, y, c = _place()
        me = 4 * x + 2 * y + c
        cps = []
        for k in range(1, 8):
            fx, fy, fc = (k >> 2) & 1, (k >> 1) & 1, k & 1
            dev = (x ^ fx, y ^ fy, c ^ fc)
            cp = _remote(v_ref, buf.at[k], ssem.at[k], rsem.at[k], dev)
            cp.start()
            cps.append(cp)
        buf[0] = v_ref[...]
        for cp in cps:
            cp.wait()
        acc = buf[me]
        for a in range(1, 8):
            acc = acc + buf[a ^ me]
        o_ref[...] = acc

    return pl.pallas_call(
        body, name="allreduce_small",
        out_shape=jax.ShapeDtypeStruct(vec.shape, F32),
        in_specs=[pl.BlockSpec(memory_space=pltpu.VMEM)],
        out_specs=pl.BlockSpec(memory_space=pltpu.VMEM),
        scratch_shapes=[pltpu.VMEM((8, rows, 128), F32), pltpu.SemaphoreType.DMA((8,)), pltpu.SemaphoreType.DMA((8,))],
        compiler_params=pltpu.CompilerParams(vmem_limit_bytes=VMEM_LIMIT),
    )(vec)


ROW_TILE = 128


def add_own_half(core, g, recv, name):
    _, r, cc = g.shape
    half = r // 2
    nb = half // ROW_TILE

    def body(c_ref, g_ref, r_ref, o_ref, ob_ref):
        del c_ref
        s = g_ref[...] + r_ref[...]
        o_ref[...] = s
        ob_ref[...] = s.astype(BF16)

    blk = (None, ROW_TILE, cc)
    return pl.pallas_call(
        body, name=name,
        grid_spec=pltpu.PrefetchScalarGridSpec(
            num_scalar_prefetch=1, grid=(N_SHARD, nb),
            in_specs=[pl.BlockSpec(blk, lambda s, i, c_ref: (s, c_ref[0] * nb + i, 0)),
                      pl.BlockSpec(blk, lambda s, i, c_ref: (s, i, 0))],
            out_specs=(pl.BlockSpec(blk, lambda s, i, c_ref: (s, i, 0)),
                       pl.BlockSpec(blk, lambda s, i, c_ref: (s, i, 0)))),
        out_shape=(jax.ShapeDtypeStruct((N_SHARD, half, cc), F32),
                   jax.ShapeDtypeStruct((N_SHARD, half, cc), BF16)),
        compiler_params=_cparams(("arbitrary", "arbitrary")),
    )(core, g, recv)


def add_chip_sums(place, own, recv, name):
    _, half, cc = own.shape
    nb = half // ROW_TILE

    def body(s_ref, o_ref, r_ref, out_ref):
        del s_ref
        acc = o_ref[...] + r_ref[0].astype(F32)
        acc = acc + r_ref[1].astype(F32)
        out_ref[...] = acc + r_ref[2].astype(F32)

    return pl.pallas_call(
        body, name=name,
        grid_spec=pltpu.PrefetchScalarGridSpec(
            num_scalar_prefetch=1, grid=(nb,),
            in_specs=[pl.BlockSpec((None, ROW_TILE, cc), lambda i, s_ref: (s_ref[0], i, 0)),
                      pl.BlockSpec((3, ROW_TILE, cc), lambda i, s_ref: (0, i, 0))],
            out_specs=pl.BlockSpec((ROW_TILE, cc), lambda i, s_ref: (s_ref[1] * nb + i, 0))),
        out_shape=jax.ShapeDtypeStruct((2 * half, cc), F32),
        compiler_params=_cparams(("arbitrary",)),
    )(place, own, recv)


def adamw(w, m, v, g, name, copy_g=False):
    r, cc = w.shape
    tr = next(t for t in (256, 352, r) if r % t == 0)
    bc1 = 1.0 / (1.0 - ADAM_B1 ** ADAM_STEP)
    bc2 = 1.0 / (1.0 - ADAM_B2 ** ADAM_STEP)

    def body(w_ref, m_ref, v_ref, g_ref, d_ref, mo_ref, vo_ref, *go_ref):
        gv = g_ref[...]
        mn = ADAM_B1 * m_ref[...] + (1.0 - ADAM_B1) * gv
        vn = ADAM_B2 * v_ref[...] + (1.0 - ADAM_B2) * (gv * gv)
        mo_ref[...] = mn
        vo_ref[...] = vn
        d_ref[...] = -ADAM_LR * ((mn * bc1) / (jnp.sqrt(vn * bc2) + ADAM_EPS) + ADAM_WD * w_ref[...])
        if copy_g:
            go_ref[0][...] = gv

    blk = pl.BlockSpec((tr, cc), lambda i: (i, 0))
    shp = jax.ShapeDtypeStruct((r, cc), F32)
    nout = 4 if copy_g else 3
    return pl.pallas_call(
        body, name=name, grid=(r // tr,),
        out_shape=(shp,) * nout, in_specs=[blk] * 4, out_specs=(blk,) * nout,
        compiler_params=_cparams(("arbitrary",)),
    )(w, m, v, g)


SMALL = ("ffn1_norm", "mix_norm", "s5_lam_re", "s5_lam_im", "s5_log_dt", "s5_b_re", "s5_b_im", "s5_c_re",
         "s5_c_im", "s5_d", "hg_lower_bound", "hg_out_norm", "ffn2_norm", "ple_norm", "final_norm")
WEIGHTS = ("ffn1_norm", "ffn1_w_gate", "ffn1_w_up", "ffn1_w_down", "mix_norm", "w_in", "s5_lam_re", "s5_lam_im",
           "s5_log_dt", "s5_b_re", "s5_b_im", "s5_c_re", "s5_c_im", "s5_d", "s5_glu_val", "s5_glu_gate",
           "hg_lower_bound", "hg_out_norm", "hg_w_out", "w_merge_out", "ffn2_norm", "ffn2_w_gate", "ffn2_w_up",
           "ffn2_w_down", "ple_norm", "ple_w_gate", "ple_w_proj", "final_norm")


def _as_rows(name, w):
    return jnp.swapaxes(w[0], 0, 1) if name in FFN_T else w[0]


def _from_rows(name, w):
    return (jnp.swapaxes(w, 0, 1) if name in FFN_T else w)[None]


def _gather_buffer(name, w_rows, chip):
    r, c = BIG_SHARD[name]
    shard = jnp.pad(w_rows.astype(BF16), ((0, r - w_rows.shape[0]), (0, 0)))
    return lax.dynamic_update_slice(jnp.zeros((N_SHARD, r, c), BF16), shard[None], (chip, 0, 0))


def _pack(parts):
    flat = jnp.concatenate([jnp.zeros((128,), F32)] + [a.reshape(-1) for a in parts])
    rows = -(-flat.shape[0] // 1024) * 8
    return jnp.pad(flat, (0, rows * 128 - flat.shape[0])).reshape(rows, 128)


def _unpack(vec, likes):
    flat = vec.reshape(-1)
    out, off = [], 128
    for a in likes:
        out.append(flat[off:off + a.size].reshape(a.shape))
        off += a.size
    return out


def _small_view(name, w):
    if name.startswith("s5_") and name != "s5_d":
        return w[0]
    if name == "final_norm":
        return w.reshape(1, D_MODEL)
    return w


def kernel(x, p, ffn1_norm, ffn1_w_gate, ffn1_w_up, ffn1_w_down, mix_norm, w_in, s5_lam_re, s5_lam_im, s5_log_dt, s5_b_re, s5_b_im, s5_c_re, s5_c_im, s5_d, s5_glu_val, s5_glu_gate, hg_lower_bound, hg_out_norm, hg_w_out, w_merge_out, ffn2_norm, ffn2_w_gate, ffn2_w_up, ffn2_w_down, ple_norm, ple_w_gate, ple_w_proj, final_norm, loss_target, m_ffn1_norm, m_ffn1_w_gate, m_ffn1_w_up, m_ffn1_w_down, m_mix_norm, m_w_in, m_s5_lam_re, m_s5_lam_im, m_s5_log_dt, m_s5_b_re, m_s5_b_im, m_s5_c_re, m_s5_c_im, m_s5_d, m_s5_glu_val, m_s5_glu_gate, m_hg_lower_bound, m_hg_out_norm, m_hg_w_out, m_w_merge_out, m_ffn2_norm, m_ffn2_w_gate, m_ffn2_w_up, m_ffn2_w_down, m_ple_norm, m_ple_w_gate, m_ple_w_proj, m_final_norm, v_ffn1_norm, v_ffn1_w_gate, v_ffn1_w_up, v_ffn1_w_down, v_mix_norm, v_w_in, v_s5_lam_re, v_s5_lam_im, v_s5_log_dt, v_s5_b_re, v_s5_b_im, v_s5_c_re, v_s5_c_im, v_s5_d, v_s5_glu_val, v_s5_glu_gate, v_hg_lower_bound, v_hg_out_norm, v_hg_w_out, v_w_merge_out, v_ffn2_norm, v_ffn2_w_gate, v_ffn2_w_up, v_ffn2_w_down, v_ple_norm, v_ple_w_gate, v_ple_w_proj, v_final_norm):
    given = dict(locals())
    wv = {k: given[k] for k in WEIGHTS}
    mv = {k: given["m_" + k] for k in WEIGHTS}
    vv = {k: given["v_" + k] for k in WEIGHTS}

    core = lax.axis_index("c").astype(jnp.int32)
    chip = (2 * lax.axis_index("x") + lax.axis_index("y")).astype(jnp.int32)
    w_rows = {k: _as_rows(k, wv[k]) for k in BIG}
    gathered = allgather_weights([_gather_buffer(k, w_rows[k], chip) for k in BIG])
    wts = dict(zip(BIG, gathered))
    sm = {k: _small_view(k, wv[k]) for k in SMALL}

    loss_blk, dx, gbig, gsm = local_step(x[0], p[0, 0], loss_target[0], wts, sm)

    glist = [gbig[k] for k in BIG]
    recv1 = exchange_halves(glist)
    sums = [add_own_half(core.reshape(1), g, r, "pair_sum_" + k) for k, g, r in zip(BIG, glist, recv1)]
    recv2 = scatter_chip_sums([s[1] for s in sums])
    place = jnp.stack([chip, core])
    halves = [add_chip_sums(place, s[0], r, "chip_sum_" + k) for k, s, r in zip(BIG, sums, recv2)]
    full = dict(zip(BIG, join_halves(halves)))

    small_likes = [wv[k] for k in SMALL]
    packed = _pack([gsm[k] for k in SMALL])
    packed = packed.at[0, 0].set(loss_blk[0, 0])
    total = allreduce_small(packed)
    loss = total[0, 0]
    gsmall = dict(zip(SMALL, _unpack(total, small_likes)))

    grads, deltas, new_m, new_v = {}, {}, {}, {}
    for k in BIG:
        padded = full[k].shape != w_rows[k].shape
        res = adamw(w_rows[k], _as_rows(k, mv[k]), _as_rows(k, vv[k]), full[k], "adamw_" + k, copy_g=padded)
        grads[k] = _from_rows(k, res[3] if padded else full[k])
        deltas[k], new_m[k], new_v[k] = (_from_rows(k, a) for a in res[:3])
    sw = _pack([wv[k] for k in SMALL])
    smm = _pack([mv[k] for k in SMALL])
    svv = _pack([vv[k] for k in SMALL])
    sd, smn, svn = adamw(sw, smm, svv, total, "adamw_small")
    for k, d, mn, vn in zip(SMALL, _unpack(sd, small_likes), _unpack(smn, small_likes), _unpack(svn, small_likes)):
        grads[k], deltas[k], new_m[k], new_v[k] = gsmall[k], d, mn, vn

    return (loss, dx[None], *[grads[k] for k in WEIGHTS], *[deltas[k] for k in WEIGHTS],
            *[new_m[k] for k in WEIGHTS], *[new_v[k] for k in WEIGHTS])
```

```python
import math

import jax
import jax.numpy as jnp
from jax import lax
from jax.experimental import pallas as pl
from jax.experimental.pallas import tpu as pltpu

F32 = jnp.float32
BF16 = jnp.bfloat16

D_MODEL = 1024
D_FF = 2816
N_SHARD = 4
FF_SHARD = D_FF // N_SHARD
FF_PAD = 768
NORM_EPS = 1e-6
PLE_DIM = 256

S5_WIDTH = 512
S5_GROUPS = 32
S5_GROUP = 16
S5_STATE = 64
S5_N = S5_GROUPS * S5_STATE
S5_KT = 2

HG_HEADS = 8
HG_E = 128
HG_WIDTH = 1024
CHUNK = 64
IN_COLS = S5_WIDTH + 4 * HG_WIDTH + 2 * D_MODEL
IN_SPLITS = (0, 512, 1536, 2560, 3584, 4608, 5632, 6656)

ADAM_LR = 0.001
ADAM_B1 = 0.9
ADAM_B2 = 0.999
ADAM_EPS = 1e-08
ADAM_WD = 0.01
ADAM_STEP = 10

VMEM_LIMIT = 60 * 1024 * 1024
HIGHEST = lax.Precision.HIGHEST


def _cparams(sem=None, **kw):
    return pltpu.CompilerParams(dimension_semantics=sem, vmem_limit_bytes=VMEM_LIMIT, **kw)


def _const_spec(shape):
    nd = len(shape)
    return pl.BlockSpec(shape, lambda *_: (0,) * nd, pipeline_mode=pl.Buffered(1))


def _dot(a, b):
    return jnp.dot(a, b, preferred_element_type=F32)


def _dot_nt(a, b):
    return lax.dot_general(a, b, (((1,), (1,)), ((), ())), preferred_element_type=F32)


def _dot_tn(a, b):
    return lax.dot_general(a, b, (((0,), (0,)), ((), ())), preferred_element_type=F32)


def _sigmoid(x):
    return 1.0 / (1.0 + jnp.exp(-x))


def _rms_fwd(x, g):
    r = lax.rsqrt(jnp.mean(x * x, axis=-1, keepdims=True) + NORM_EPS)
    return x * r * g, r


def _rms_bwd(x, r, g, dy):
    xh = x * r
    dyg = dy * g
    m = jnp.mean(dyg * xh, axis=-1, keepdims=True)
    return r * (dyg - xh * m), jnp.sum(dy * xh, axis=0, keepdims=True)


def _accum(ref, val, first):
    @pl.when(first)
    def _():
        ref[...] = val

    @pl.when(jnp.logical_not(first))
    def _():
        ref[...] += val


def ffn_fwd(h, gain, wg, wu, wd, name, comm=None, tm=256):
    t = h.shape[0]

    def body(h_ref, g_ref, wg_ref, wu_ref, wd_ref, o_ref, a_ref, b_ref):
        hv = h_ref[...]
        n, _ = _rms_fwd(hv, g_ref[...])
        nb = n.astype(BF16)
        acc = jnp.zeros((tm, D_MODEL), F32)
        for s in range(N_SHARD):
            a = _dot_nt(nb, wg_ref[s])
            b = _dot_nt(nb, wu_ref[s])
            a_ref[s] = a.astype(BF16)
            b_ref[s] = b.astype(BF16)
            sv = (a * _sigmoid(a) * b).astype(BF16)
            acc = acc + _dot(sv, wd_ref[s])
        o_ref[...] = hv + 0.5 * acc

    return _carry(
        body, comm, name=name, steps=t // tm,
        out_shape=(jax.ShapeDtypeStruct((t, D_MODEL), F32),
                   jax.ShapeDtypeStruct((N_SHARD, t, FF_PAD), BF16),
                   jax.ShapeDtypeStruct((N_SHARD, t, FF_PAD), BF16)),
        in_specs=[pl.BlockSpec((tm, D_MODEL), lambda i: (i, 0)),
                  _const_spec((1, D_MODEL)),
                  _const_spec((N_SHARD, FF_PAD, D_MODEL)),
                  _const_spec((N_SHARD, FF_PAD, D_MODEL)),
                  _const_spec((N_SHARD, FF_PAD, D_MODEL))],
        out_specs=(pl.BlockSpec((tm, D_MODEL), lambda i: (i, 0)),
                   pl.BlockSpec((N_SHARD, tm, FF_PAD), lambda i: (0, i, 0)),
                   pl.BlockSpec((N_SHARD, tm, FF_PAD), lambda i: (0, i, 0))),
        args=(h, gain, wg, wu, wd),
    )


def ffn_bwd(dho, h, a, b, gain, wg, wu, wd, name, comm=None, tm=256):
    t = h.shape[0]

    def body(dho_ref, h_ref, a_ref, b_ref, g_ref, wg_ref, wu_ref, wd_ref,
             dh_ref, dg_ref, nb_ref, dhb_ref, da_ref, db_ref, s_ref):
        hv = h_ref[...]
        g = g_ref[...]
        n, r = _rms_fwd(hv, g)
        nb_ref[...] = n.astype(BF16)
        dhalf = (0.5 * dho_ref[...]).astype(BF16)
        dhb_ref[...] = dhalf
        dn = jnp.zeros((tm, D_MODEL), F32)
        for s in range(N_SHARD):
            av = a_ref[s].astype(F32)
            bv = b_ref[s].astype(F32)
            sg = _sigmoid(av)
            sil = av * sg
            s_ref[s] = (sil * bv).astype(BF16)
            ds = _dot_nt(dhalf, wd_ref[s])
            da = (ds * bv * (sg * (1.0 + av * (1.0 - sg)))).astype(BF16)
            db = (ds * sil).astype(BF16)
            da_ref[s] = da
            db_ref[s] = db
            dn = dn + _dot(da, wg_ref[s]) + _dot(db, wu_ref[s])
        dx, dg = _rms_bwd(hv, r, g, dn)
        dh_ref[...] = dho_ref[...] + dx
        _accum(dg_ref, dg, pl.program_id(0) == 0)

    tok = pl.BlockSpec((tm, D_MODEL), lambda i: (i, 0))
    hid = pl.BlockSpec((N_SHARD, tm, FF_PAD), lambda i: (0, i, 0))
    return _carry(
        body, comm, name=name, steps=t // tm,
        out_shape=(jax.ShapeDtypeStruct((t, D_MODEL), F32),
                   jax.ShapeDtypeStruct((1, D_MODEL), F32),
                   jax.ShapeDtypeStruct((t, D_MODEL), BF16),
                   jax.ShapeDtypeStruct((t, D_MODEL), BF16),
                   jax.ShapeDtypeStruct((N_SHARD, t, FF_PAD), BF16),
                   jax.ShapeDtypeStruct((N_SHARD, t, FF_PAD), BF16),
                   jax.ShapeDtypeStruct((N_SHARD, t, FF_PAD), BF16)),
        in_specs=[tok, tok, hid, hid, _const_spec((1, D_MODEL)),
                  _const_spec((N_SHARD, FF_PAD, D_MODEL)),
                  _const_spec((N_SHARD, FF_PAD, D_MODEL)),
                  _const_spec((N_SHARD, FF_PAD, D_MODEL))],
        out_specs=(tok, pl.BlockSpec((1, D_MODEL), lambda i: (0, 0)), tok, tok, hid, hid, hid),
        args=(dho, h, a, b, gain, wg, wu, wd),
    )


def tn_matmul(x, y, name, shard, tk=512):
    x3, y3 = x.ndim == 3, y.ndim == 3
    t = x.shape[-2]
    m = x.shape[-1] // (N_SHARD if (shard == "rows" and not x3) else 1)
    n = y.shape[-1] // (N_SHARD if (shard == "cols" and not y3) else 1)
    nk = t // tk

    def body(x_ref, y_ref, o_ref):
        _accum(o_ref, _dot_tn(x_ref[...].astype(BF16), y_ref[...].astype(BF16)), pl.program_id(1) == 0)

    if x3:
        x_spec = pl.BlockSpec((None, tk, m), lambda s, k: (s, k, 0))
    elif shard == "rows":
        x_spec = pl.BlockSpec((tk, m), lambda s, k: (k, s))
    else:
        x_spec = pl.BlockSpec((tk, m), lambda s, k: (k, 0))
    if y3:
        y_spec = pl.BlockSpec((None, tk, n), lambda s, k: (s, k, 0))
    elif shard == "cols":
        y_spec = pl.BlockSpec((tk, n), lambda s, k: (k, s))
    else:
        y_spec = pl.BlockSpec((tk, n), lambda s, k: (k, 0))
    return pl.pallas_call(
        body, name=name, grid=(N_SHARD, nk),
        out_shape=jax.ShapeDtypeStruct((N_SHARD, m, n), F32),
        in_specs=[x_spec, y_spec],
        out_specs=pl.BlockSpec((None, m, n), lambda s, k: (s, 0, 0)),
        compiler_params=_cparams(("arbitrary", "arbitrary")),
    )(x, y)


def inproj_fwd(h, gain, w_in, tm=256):
    t = h.shape[0]
    widths = [IN_SPLITS[j + 1] - IN_SPLITS[j] for j in range(7)]
    sh_cols = IN_COLS // N_SHARD

    def body(h_ref, g_ref, w_ref, *outs):
        n, _ = _rms_fwd(h_ref[...], g_ref[...])
        nb = n.astype(BF16)
        proj = jnp.concatenate([_dot(nb, w_ref[s]) for s in range(N_SHARD)], axis=1)
        for j, o_ref in enumerate(outs):
            o_ref[...] = proj[:, IN_SPLITS[j]:IN_SPLITS[j + 1]]

    return pl.pallas_call(
        body, name="inproj_fwd", grid=(t // tm,),
        out_shape=tuple(jax.ShapeDtypeStruct((t, w), F32) for w in widths),
        in_specs=[pl.BlockSpec((tm, D_MODEL), lambda i: (i, 0)),
                  _const_spec((1, D_MODEL)),
                  _const_spec((N_SHARD, D_MODEL, sh_cols))],
        out_specs=tuple(pl.BlockSpec((tm, w), lambda i: (i, 0)) for w in widths),
        compiler_params=_cparams(("arbitrary",)),
    )(h, gain, w_in)


def inproj_bwd(dres, h, gain, w_in, dparts, comm=None, tm=256):
    t = h.shape[0]
    widths = [IN_SPLITS[j + 1] - IN_SPLITS[j] for j in range(7)]
    sh_cols = IN_COLS // N_SHARD

    def body(dres_ref, h_ref, g_ref, w_ref, d0, d1, d2, d3, d4, d5, d6, dh_ref, dg_ref, nb_ref, dp_ref):
        hv = h_ref[...]
        g = g_ref[...]
        n, r = _rms_fwd(hv, g)
        nb_ref[...] = n.astype(BF16)
        dproj = jnp.concatenate([d[...] for d in (d0, d1, d2, d3, d4, d5, d6)], axis=1).astype(BF16)
        dp_ref[...] = dproj
        dn = jnp.zeros((tm, D_MODEL), F32)
        for s in range(N_SHARD):
            dn = dn + _dot_nt(dproj[:, s * sh_cols:(s + 1) * sh_cols], w_ref[s])
        dx, dg = _rms_bwd(hv, r, g, dn)
        dh_ref[...] = dres_ref[...] + dx
        _accum(dg_ref, dg, pl.program_id(0) == 0)

    tok = pl.BlockSpec((tm, D_MODEL), lambda i: (i, 0))
    return _carry(
        body, comm, name="inproj_bwd", steps=t // tm,
        out_shape=(jax.ShapeDtypeStruct((t, D_MODEL), F32),
                   jax.ShapeDtypeStruct((1, D_MODEL), F32),
                   jax.ShapeDtypeStruct((t, D_MODEL), BF16),
                   jax.ShapeDtypeStruct((t, IN_COLS), BF16)),
        in_specs=[tok, tok, _const_spec((1, D_MODEL)), _const_spec((N_SHARD, D_MODEL, sh_cols))]
                 + [pl.BlockSpec((tm, w), lambda i: (i, 0)) for w in widths],
        out_specs=(tok, pl.BlockSpec((1, D_MODEL), lambda i: (0, 0)), tok,
                   pl.BlockSpec((tm, IN_COLS), lambda i: (i, 0))),
        args=(dres, h, gain, w_in, *dparts),
    )


def s5_prep(lam_re, lam_im, log_dt, b_re, b_im, c_re, c_im):
    dt = jnp.exp(log_dt)[:, None]
    mag = jnp.exp(lam_re * dt)
    lbr = mag * jnp.cos(lam_im * dt)
    lbi = mag * jnp.sin(lam_im * dt)
    den = lam_re * lam_re + lam_im * lam_im
    nr, ni = lbr - 1.0, lbi
    kr = (nr * lam_re + ni * lam_im) / den
    ki = (ni * lam_re - nr * lam_im) / den
    bbr = kr[..., None] * b_re - ki[..., None] * b_im
    bbi = kr[..., None] * b_im + ki[..., None] * b_re
    eye = jnp.eye(16, dtype=F32)

    def bm(bp):
        return jnp.einsum('kgph,gG->kghGp', bp.reshape(S5_KT, 16, S5_STATE, S5_GROUP), eye).reshape(S5_KT, 256, 1024)

    def cm(cp):
        return jnp.einsum('kghp,gG->kgpGh', cp.reshape(S5_KT, 16, S5_GROUP, S5_STATE), eye).reshape(S5_KT, 1024, 256)

    lam_bar = jnp.stack([lbr.reshape(S5_N), lbi.reshape(S5_N)])
    bmat = jnp.stack([bm(bbr), bm(bbi)])
    cmat = jnp.stack([cm(c_re), -cm(c_im)])
    return lam_bar, bmat, cmat


def _lam_powers(lam_bar):
    lr, li = lam_bar[0], lam_bar[1]
    pr, pi = [lr], [li]
    for _ in range(7):
        pr, pi = pr + [pr[-1] * lr - pi[-1] * li], pi + [pr[-1] * li + pi[-1] * lr]
    return jnp.stack(pr), jnp.stack(pi)


def s5_fwd(u, pw_r, pw_i, bmat, cmat, dvec, comm=None, tm=256):
    t = u.shape[0]
    nch = tm // 8

    def body(u_ref, pwr_ref, pwi_ref, b_ref, c_ref, d_ref, y_ref, xp_ref, x_scr, carry):
        @pl.when(pl.program_id(0) == 0)
        def _():
            carry[...] = jnp.zeros_like(carry)

        uv = u_ref[...]
        ub = uv.astype(BF16)
        for part in range(2):
            for kt in range(S5_KT):
                x_scr[:, pl.ds(part * S5_N + kt * 1024, 1024)] = _dot(ub[:, kt * 256:(kt + 1) * 256], b_ref[part, kt])
        row = lax.broadcasted_iota(jnp.int32, (8, S5_N), 0)
        pwr = pwr_ref[...]
        pwi = pwi_ref[...]

        def chunk(i, c):
            cr, ci = c
            r0 = pl.multiple_of(i * 8, 8)
            xr = x_scr[pl.ds(r0, 8), pl.ds(0, S5_N)]
            xi = x_scr[pl.ds(r0, 8), pl.ds(S5_N, S5_N)]
            for sh, idx in ((1, 0), (2, 1), (4, 3)):
                sr = jnp.where(row < sh, 0.0, pltpu.roll(xr, sh, 0))
                si = jnp.where(row < sh, 0.0, pltpu.roll(xi, sh, 0))
                lr = pwr[idx:idx + 1, :]
                li = pwi[idx:idx + 1, :]
                xr, xi = xr + lr * sr - li * si, xi + lr * si + li * sr
            xr, xi = xr + pwr * cr - pwi * ci, xi + pwr * ci + pwi * cr
            x_scr[pl.ds(r0, 8), pl.ds(0, S5_N)] = xr
            x_scr[pl.ds(r0, 8), pl.ds(S5_N, S5_N)] = xi
            xp_ref[pl.ds(r0, 8), pl.ds(0, S5_N)] = jnp.where(row == 0, cr, pltpu.roll(xr, 1, 0))
            xp_ref[pl.ds(r0, 8), pl.ds(S5_N, S5_N)] = jnp.where(row == 0, ci, pltpu.roll(xi, 1, 0))
            return xr[7:8, :], xi[7:8, :]

        cr, ci = lax.fori_loop(0, nch, chunk, (carry[0:1, :], carry[1:2, :]))
        carry[0:1, :] = cr
        carry[1:2, :] = ci
        for kt in range(S5_KT):
            acc = jnp.zeros((tm, 256), F32)
            for part in range(2):
                acc = acc + _dot(x_scr[:, pl.ds(part * S5_N + kt * 1024, 1024)].astype(BF16), c_ref[part, kt])
            y_ref[:, pl.ds(kt * 256, 256)] = acc + d_ref[:, pl.ds(kt * 256, 256)] * uv[:, kt * 256:(kt + 1) * 256]

    return _carry(
        body, comm, name="s5_fwd", steps=t // tm,
        out_shape=(jax.ShapeDtypeStruct((t, S5_WIDTH), F32),
                   jax.ShapeDtypeStruct((t, 2 * S5_N), F32)),
        in_specs=[pl.BlockSpec((tm, S5_WIDTH), lambda i: (i, 0)),
                  _const_spec((8, S5_N)), _const_spec((8, S5_N)),
                  _const_spec((2, S5_KT, 256, 1024)), _const_spec((2, S5_KT, 1024, 256)),
                  _const_spec((1, S5_WIDTH))],
        out_specs=(pl.BlockSpec((tm, S5_WIDTH), lambda i: (i, 0)),
                   pl.BlockSpec((tm, 2 * S5_N), lambda i: (i, 0))),
        scratch_shapes=[pltpu.VMEM((tm, 2 * S5_N), F32), pltpu.VMEM((8, S5_N), F32)],
        args=(u, pw_r, pw_i, bmat, cmat, dvec),
    )


def s5_bwd(dy, u, xp, pw_r, pw_i, pwrev_r, pwrev_i, bmat, bmat_t, cmat_t, dvec, comm=None, tm=256):
    t = u.shape[0]
    nt = t // tm
    nch = tm // 8

    def body(dy_ref, u_ref, xp_ref, pwr_ref, pwi_ref, pvr_ref, pvi_ref, b_ref, bt_ref, ct_ref, d_ref,
             du_ref, db_ref, dc_ref, dl_ref, dd_ref, g_scr, x_scr, carry):
        first = pl.program_id(0) == 0

        @pl.when(first)
        def _():
            carry[...] = jnp.zeros_like(carry)
            dl_ref[...] = jnp.zeros_like(dl_ref)

        dyv = dy_ref[...]
        uv = u_ref[...]
        dyb = dyv.astype(BF16)
        ub = uv.astype(BF16)
        lr1 = pwr_ref[0:1, :]
        li1 = pwi_ref[0:1, :]
        for kt in range(S5_KT):
            cols = pl.ds(kt * 1024, 1024)
            colsi = pl.ds(S5_N + kt * 1024, 1024)
            g_scr[:, cols] = _dot(dyb[:, kt * 256:(kt + 1) * 256], ct_ref[0, kt])
            g_scr[:, colsi] = _dot(dyb[:, kt * 256:(kt + 1) * 256], ct_ref[1, kt])
            bur = _dot(ub[:, kt * 256:(kt + 1) * 256], b_ref[0, kt])
            bui = _dot(ub[:, kt * 256:(kt + 1) * 256], b_ref[1, kt])
            xpr = xp_ref[:, cols]
            xpi = xp_ref[:, colsi]
            lrk = lr1[:, kt * 1024:(kt + 1) * 1024]
            lik = li1[:, kt * 1024:(kt + 1) * 1024]
            x_scr[:, cols] = lrk * xpr - lik * xpi + bur
            x_scr[:, colsi] = lrk * xpi + lik * xpr + bui

        row = lax.broadcasted_iota(jnp.int32, (8, S5_N), 0)
        pwr = pwr_ref[...]
        pwi = pwi_ref[...]
        pvr = pvr_ref[...]
        pvi = pvi_ref[...]

        def chunk(j, c):
            cr, ci = c
            r0 = pl.multiple_of((nch - 1 - j) * 8, 8)
            gr = g_scr[pl.ds(r0, 8), pl.ds(0, S5_N)]
            gi = g_scr[pl.ds(r0, 8), pl.ds(S5_N, S5_N)]
            for sh, idx in ((1, 0), (2, 1), (4, 3)):
                sr = jnp.where(row >= 8 - sh, 0.0, pltpu.roll(gr, 8 - sh, 0))
                si = jnp.where(row >= 8 - sh, 0.0, pltpu.roll(gi, 8 - sh, 0))
                lr = pwr[idx:idx + 1, :]
                li = pwi[idx:idx + 1, :]
                gr, gi = gr + lr * sr + li * si, gi + lr * si - li * sr
            gr, gi = gr + pvr * cr + pvi * ci, gi + pvr * ci - pvi * cr
            g_scr[pl.ds(r0, 8), pl.ds(0, S5_N)] = gr
            g_scr[pl.ds(r0, 8), pl.ds(S5_N, S5_N)] = gi
            xpr = xp_ref[pl.ds(r0, 8), pl.ds(0, S5_N)]
            xpi = xp_ref[pl.ds(r0, 8), pl.ds(S5_N, S5_N)]
            dl_ref[0] += gr * xpr + gi * xpi
            dl_ref[1] += gi * xpr - gr * xpi
            return gr[0:1, :], gi[0:1, :]

        cr, ci = lax.fori_loop(0, nch, chunk, (carry[0:1, :], carry[1:2, :]))
        carry[0:1, :] = cr
        carry[1:2, :] = ci

        for kt in range(S5_KT):
            du = jnp.zeros((tm, 256), F32)
            ukt = ub[:, kt * 256:(kt + 1) * 256]
            dykt = dyb[:, kt * 256:(kt + 1) * 256]
            for part in range(2):
                gb = g_scr[:, pl.ds(part * S5_N + kt * 1024, 1024)].astype(BF16)
                xb = x_scr[:, pl.ds(part * S5_N + kt * 1024, 1024)].astype(BF16)
                du = du + _dot(gb, bt_ref[part, kt])
                dbv = _dot_tn(ukt, gb)
                dcv = _dot_tn(xb, dykt)

                @pl.when(first)
                def _():
                    db_ref[part, kt] = dbv
                    dc_ref[part, kt] = dcv

                @pl.when(jnp.logical_not(first))
                def _():
                    db_ref[part, kt] += dbv
                    dc_ref[part, kt] += dcv
            du_ref[:, pl.ds(kt * 256, 256)] = du + d_ref[:, pl.ds(kt * 256, 256)] * dyv[:, kt * 256:(kt + 1) * 256]
        _accum(dd_ref, jnp.sum(dyv * uv, axis=0, keepdims=True), first)

    rev = lambda i: (nt - 1 - i, 0)
    return _carry(
        body, comm, name="s5_bwd", steps=nt,
        out_shape=(jax.ShapeDtypeStruct((t, S5_WIDTH), F32),
                   jax.ShapeDtypeStruct((2, S5_KT, 256, 1024), F32),
                   jax.ShapeDtypeStruct((2, S5_KT, 1024, 256), F32),
                   jax.ShapeDtypeStruct((2, 8, S5_N), F32),
                   jax.ShapeDtypeStruct((1, S5_WIDTH), F32)),
        in_specs=[pl.BlockSpec((tm, S5_WIDTH), rev), pl.BlockSpec((tm, S5_WIDTH), rev),
                  pl.BlockSpec((tm, 2 * S5_N), rev),
                  _const_spec((8, S5_N)), _const_spec((8, S5_N)), _const_spec((8, S5_N)), _const_spec((8, S5_N)),
                  _const_spec((2, S5_KT, 256, 1024)), _const_spec((2, S5_KT, 1024, 256)),
                  _const_spec((2, S5_KT, 256, 1024)), _const_spec((1, S5_WIDTH))],
        out_specs=(pl.BlockSpec((tm, S5_WIDTH), rev),
                   pl.BlockSpec((2, S5_KT, 256, 1024), lambda i: (0, 0, 0, 0)),
                   pl.BlockSpec((2, S5_KT, 1024, 256), lambda i: (0, 0, 0, 0)),
                   pl.BlockSpec((2, 8, S5_N), lambda i: (0, 0, 0)),
                   pl.BlockSpec((1, S5_WIDTH), lambda i: (0, 0))),
        scratch_shapes=[pltpu.VMEM((tm, 2 * S5_N), F32), pltpu.VMEM((tm, 2 * S5_N), F32),
                        pltpu.VMEM((8, S5_N), F32)],
        args=(dy, u, xp, pw_r, pw_i, pwrev_r, pwrev_i, bmat, bmat_t, cmat_t, dvec),
    )


def _hg_gates(z, lb):
    sg = _sigmoid(z)
    sgn = _sigmoid(-z)
    fg = lb + (1.0 - lb) * sg
    return sg, sgn, fg, jnp.log(fg), (1.0 - lb) * sgn


def _hg_decays(g, tril):
    gc = jnp.dot(tril, g, precision=HIGHEST, preferred_element_type=F32)
    mid = gc[CHUNK // 2 - 1:CHUNK // 2, :]
    last = gc[CHUNK - 1:CHUNK, :]
    return jnp.exp(gc), jnp.exp(gc - mid), jnp.exp(mid - gc), jnp.exp(last - gc), jnp.exp(last)


def _split_bf16(x):
    hi = x.astype(BF16)
    return hi, (x - hi.astype(F32)).astype(BF16)


def _hg_scores(qt, qlo, kt, klo, sl, causal):
    a = _dot_nt(qt[:, sl], kt[:, sl]) + _dot_nt(qt[:, sl], klo[:, sl]) + _dot_nt(qlo[:, sl], kt[:, sl])
    return jnp.where(causal, a, 0.0).astype(BF16)


def hgrn_fwd(q, f, v, lb):
    t = q.shape[0]
    nc = t // CHUNK
    scale = HG_E ** -0.5

    def body(q_ref, f_ref, v_ref, lb_ref, o_ref, st_ref, state):
        @pl.when(pl.program_id(0) == 0)
        def _():
            state[...] = jnp.zeros_like(state)

        ri = lax.broadcasted_iota(jnp.int32, (CHUNK, CHUNK), 0)
        ci = lax.broadcasted_iota(jnp.int32, (CHUNK, CHUNK), 1)
        causal = ri >= ci
        tril = causal.astype(F32)
        _, _, _, g, k = _hg_gates(f_ref[...], lb_ref[...])
        eg, eq, ek, ed, el = _hg_decays(g, tril)
        qs = q_ref[...] * scale
        qg = (qs * eg).astype(BF16)
        qt, qlo = _split_bf16(qs * eq)
        kt, klo = _split_bf16(k * ek)
        kd = (k * ed).astype(BF16)
        vb = v_ref[...].astype(BF16)
        for h in range(HG_HEADS):
            sl = slice(h * HG_E, (h + 1) * HG_E)
            st = state[h]
            a = _hg_scores(qt, qlo, kt, klo, sl, causal)
            o_ref[:, sl] = _dot(a, vb[:, sl]) + _dot_nt(qg[:, sl], st.astype(BF16))
            st_new = st * el[:, sl] + _dot_tn(vb[:, sl], kd[:, sl])
            state[h] = st_new
            st_ref[0, h] = st_new

    tok = pl.BlockSpec((CHUNK, HG_WIDTH), lambda i: (i, 0))
    return pl.pallas_call(
        body, name="hgrn_fwd", grid=(nc,),
        out_shape=(jax.ShapeDtypeStruct((t, HG_WIDTH), F32),
                   jax.ShapeDtypeStruct((nc, HG_HEADS, HG_E, HG_E), F32)),
        in_specs=[tok, tok, tok, _const_spec((1, HG_WIDTH))],
        out_specs=(tok, pl.BlockSpec((1, HG_HEADS, HG_E, HG_E), lambda i: (i, 0, 0, 0))),
        scratch_shapes=[pltpu.VMEM((HG_HEADS, HG_E, HG_E), F32)],
        compiler_params=_cparams(("arbitrary",)),
    )(q, f, v, lb)


def hgrn_bwd(do, q, f, v, lb, states, comm=None):
    t = q.shape[0]
    nc = t // CHUNK
    scale = HG_E ** -0.5

    def body(do_ref, q_ref, f_ref, v_ref, lb_ref, s0_ref, dq_ref, df_ref, dv_ref, dlb_ref, dstate):
        first = pl.program_id(0) == 0
        c_idx = nc - 1 - pl.program_id(0)

        @pl.when(first)
        def _():
            dstate[...] = jnp.zeros_like(dstate)

        ri = lax.broadcasted_iota(jnp.int32, (CHUNK, CHUNK), 0)
        ci = lax.broadcasted_iota(jnp.int32, (CHUNK, CHUNK), 1)
        causal = ri >= ci
        tril = causal.astype(F32)
        triu = (ri <= ci).astype(F32)
        lb = lb_ref[...]
        sg, sgn, fg, g, k = _hg_gates(f_ref[...], lb)
        eg, eq, ek, ed, el = _hg_decays(g, tril)
        qs = q_ref[...] * scale
        qg = (qs * eg).astype(BF16)
        qt, qlo = _split_bf16(qs * eq)
        kt, klo = _split_bf16(k * ek)
        kd = (k * ed).astype(BF16)
        vb = v_ref[...].astype(BF16)
        dob = do_ref[...].astype(BF16)
        has_prev = jnp.where(c_idx > 0, 1.0, 0.0)
        dqs_l, dk_l, dgc_l, dgl_l = [], [], [], []
        for h in range(HG_HEADS):
            sl = slice(h * HG_E, (h + 1) * HG_E)
            s0 = s0_ref[0, h] * has_prev
            ds1 = dstate[h]
            ds1b = ds1.astype(BF16)
            a = _hg_scores(qt, qlo, kt, klo, sl, causal)
            da = jnp.where(causal, _dot_nt(dob[:, sl], vb[:, sl]), 0.0).astype(BF16)
            dv_ref[:, sl] = _dot_tn(a, dob[:, sl]) + _dot_nt(kd[:, sl], ds1b)
            dkd = _dot(vb[:, sl], ds1b)
            dqt = _dot(da, kt[:, sl])
            dkt = _dot_tn(da, qt[:, sl])
            dqg = _dot(dob[:, sl], s0.astype(BF16))
            dqs_l.append(dqt * eq[:, sl] + dqg * eg[:, sl])
            dk_l.append(dkt * ek[:, sl] + dkd * ed[:, sl])
            kd_dkd = kd[:, sl].astype(F32) * dkd
            dgc_l.append(qt[:, sl].astype(F32) * dqt - kt[:, sl].astype(F32) * dkt
                         + qg[:, sl].astype(F32) * dqg - kd_dkd)
            dgl_l.append(el[:, sl] * jnp.sum(ds1 * s0, axis=0, keepdims=True)
                         + jnp.sum(kd_dkd, axis=0, keepdims=True))
            dstate[h] = ds1 * el[:, sl] + _dot_tn(dob[:, sl], qg[:, sl])
        dqs = jnp.concatenate(dqs_l, axis=1)
        dk = jnp.concatenate(dk_l, axis=1)
        dgl = jnp.concatenate(dgl_l, axis=1)
        dq_ref[...] = dqs * scale
        rowc = lax.broadcasted_iota(jnp.int32, (CHUNK, HG_WIDTH), 0)
        dgc = jnp.concatenate(dgc_l, axis=1) + jnp.where(rowc == CHUNK - 1, dgl, 0.0)
        dg = jnp.dot(triu, dgc, precision=HIGHEST, preferred_element_type=F32)
        w = dg / fg - dk
        df_ref[...] = w * (1.0 - lb) * sg * sgn
        _accum(dlb_ref, jnp.sum(w * sgn, axis=0, keepdims=True), first)

    rev = lambda i: (nc - 1 - i, 0)
    tok = pl.BlockSpec((CHUNK, HG_WIDTH), rev)
    return _carry(
        body, comm, name="hgrn_bwd", steps=nc,
        out_shape=(jax.ShapeDtypeStruct((t, HG_WIDTH), F32),
                   jax.ShapeDtypeStruct((t, HG_WIDTH), F32),
                   jax.ShapeDtypeStruct((t, HG_WIDTH), F32),
                   jax.ShapeDtypeStruct((1, HG_WIDTH), F32)),
        in_specs=[tok, tok, tok, tok, _const_spec((1, HG_WIDTH)),
                  pl.BlockSpec((1, HG_HEADS, HG_E, HG_E), lambda i: (jnp.maximum(nc - 2 - i, 0), 0, 0, 0))],
        out_specs=(tok, tok, tok, pl.BlockSpec((1, HG_WIDTH), lambda i: (0, 0))),
        scratch_shapes=[pltpu.VMEM((HG_HEADS, HG_E, HG_E), F32)],
        args=(do, q, f, v, lb, states),
    )


GELU_C = math.sqrt(2.0 / math.pi)


def _gelu(x):
    th = jnp.tanh(GELU_C * (x + 0.044715 * x * x * x))
    return 0.5 * x * (1.0 + th), th


def _merge_core(ys5, o, og, ga, gb, wv_ref, wt_ref, ghg, who_ref):
    ys, th = _gelu(ys5)
    ysb = ys.astype(BF16)
    va = jnp.concatenate([_dot(ysb, wv_ref[s]) for s in range(N_SHARD)], axis=1)
    vt = jnp.concatenate([_dot(ysb, wt_ref[s]) for s in range(N_SHARD)], axis=1)
    svt = _sigmoid(vt)
    ya = va * svt
    rs, ons = [], []
    for h in range(HG_HEADS):
        oh = o[:, h * HG_E:(h + 1) * HG_E]
        r = lax.rsqrt(jnp.mean(oh * oh, axis=-1, keepdims=True) + NORM_EPS)
        rs.append(r)
        ons.append(oh * r)
    on = jnp.concatenate(ons, axis=1)
    sgo = _sigmoid(og)
    o2 = on * ghg * (og * sgo)
    o2b = o2.astype(BF16)
    yb = _dot(o2b, who_ref[...])
    sa = _sigmoid(ga)
    sb = _sigmoid(gb)
    mixed = sa * ya + sb * yb
    return dict(ys=ys, th=th, ysb=ysb, va=va, svt=svt, ya=ya, rs=rs, on=on, sgo=sgo, o2b=o2b, yb=yb,
                sa=sa, sb=sb, mixed=mixed)


def merge_fwd(h, ys5, o, og, ga, gb, wv, wt, ghg, who, wmo, tm=256):
    t = h.shape[0]

    def body(h_ref, ys5_ref, o_ref, og_ref, ga_ref, gb_ref, wv_ref, wt_ref, ghg_ref, who_ref, wmo_ref, out_ref):
        c = _merge_core(ys5_ref[...], o_ref[...], og_ref[...], ga_ref[...], gb_ref[...],
                        wv_ref, wt_ref, ghg_ref[...], who_ref)
        out_ref[...] = h_ref[...] + _dot(c["mixed"].astype(BF16), wmo_ref[...])

    tok = pl.BlockSpec((tm, D_MODEL), lambda i: (i, 0))
    return pl.pallas_call(
        body, name="merge_fwd", grid=(t // tm,),
        out_shape=jax.ShapeDtypeStruct((t, D_MODEL), F32),
        in_specs=[tok, pl.BlockSpec((tm, S5_WIDTH), lambda i: (i, 0)), tok, tok, tok, tok,
                  _const_spec((N_SHARD, S5_WIDTH, 256)), _const_spec((N_SHARD, S5_WIDTH, 256)),
                  _const_spec((1, HG_WIDTH)), _const_spec((HG_WIDTH, D_MODEL)), _const_spec((D_MODEL, D_MODEL))],
        out_specs=tok,
        compiler_params=_cparams(("arbitrary",)),
    )(h, ys5, o, og, ga, gb, wv, wt, ghg, who, wmo)


def merge_bwd(dh, ys5, o, og, ga, gb, wv, wt, ghg, who, wmo, comm=None, tm=256):
    t = dh.shape[0]

    def body(dh_ref, ys5_ref, o_ref, og_ref, ga_ref, gb_ref, wv_ref, wt_ref, ghg_ref, who_ref, wmo_ref,
             dys5_ref, do_ref, dog_ref, dga_ref, dgb_ref, dghg_ref,
             mixb_ref, dhb_ref, ysb_ref, dvab_ref, dvtb_ref, o2b_ref, dybb_ref):
        ys5 = ys5_ref[...]
        o = o_ref[...]
        og = og_ref[...]
        ghg = ghg_ref[...]
        c = _merge_core(ys5, o, og, ga_ref[...], gb_ref[...], wv_ref, wt_ref, ghg, who_ref)
        dhb = dh_ref[...].astype(BF16)
        dhb_ref[...] = dhb
        mixb_ref[...] = c["mixed"].astype(BF16)
        ysb_ref[...] = c["ysb"]
        o2b_ref[...] = c["o2b"]
        dmix = _dot_nt(dhb, wmo_ref[...])
        sa, sb = c["sa"], c["sb"]
        dya = dmix * sa
        dyb = dmix * sb
        dga_ref[...] = dmix * c["ya"] * sa * (1.0 - sa)
        dgb_ref[...] = dmix * c["yb"] * sb * (1.0 - sb)
        svt = c["svt"]
        dva = (dya * svt).astype(BF16)
        dvt = (dya * c["va"] * svt * (1.0 - svt)).astype(BF16)
        dvab_ref[...] = dva
        dvtb_ref[...] = dvt
        dys = jnp.zeros((tm, S5_WIDTH), F32)
        for s in range(N_SHARD):
            dys = dys + _dot_nt(dva[:, s * 256:(s + 1) * 256], wv_ref[s]) + _dot_nt(dvt[:, s * 256:(s + 1) * 256], wt_ref[s])
        th = c["th"]
        dgelu = 0.5 * (1.0 + th) + 0.5 * ys5 * (1.0 - th * th) * GELU_C * (1.0 + 3.0 * 0.044715 * ys5 * ys5)
        dys5_ref[...] = dys * dgelu
        dybb = dyb.astype(BF16)
        dybb_ref[...] = dybb
        do2 = _dot_nt(dybb, who_ref[...])
        sgo = c["sgo"]
        sil = og * sgo
        on = c["on"]
        dog_ref[...] = do2 * on * ghg * (sgo * (1.0 + og * (1.0 - sgo)))
        _accum(dghg_ref, jnp.sum(do2 * on * sil, axis=0, keepdims=True), pl.program_id(0) == 0)
        don = do2 * ghg * sil
        dos = []
        for h in range(HG_HEADS):
            sl = slice(h * HG_E, (h + 1) * HG_E)
            m = jnp.mean(don[:, sl] * on[:, sl], axis=-1, keepdims=True)
            dos.append(c["rs"][h] * (don[:, sl] - on[:, sl] * m))
        do_ref[...] = jnp.concatenate(dos, axis=1)

    tok = pl.BlockSpec((tm, D_MODEL), lambda i: (i, 0))
    s5b = pl.BlockSpec((tm, S5_WIDTH), lambda i: (i, 0))
    f32t = jax.ShapeDtypeStruct((t, D_MODEL), F32)
    bft = jax.ShapeDtypeStruct((t, D_MODEL), BF16)
    return _carry(
        body, comm, name="merge_bwd", steps=t // tm,
        out_shape=(jax.ShapeDtypeStruct((t, S5_WIDTH), F32), f32t, f32t, f32t, f32t,
                   jax.ShapeDtypeStruct((1, HG_WIDTH), F32),
                   bft, bft, jax.ShapeDtypeStruct((t, S5_WIDTH), BF16), bft, bft, bft, bft),
        in_specs=[tok, s5b, tok, tok, tok, tok,
                  _const_spec((N_SHARD, S5_WIDTH, 256)), _const_spec((N_SHARD, S5_WIDTH, 256)),
                  _const_spec((1, HG_WIDTH)), _const_spec((HG_WIDTH, D_MODEL)), _const_spec((D_MODEL, D_MODEL))],
        out_specs=(s5b, tok, tok, tok, tok, pl.BlockSpec((1, HG_WIDTH), lambda i: (0, 0)),
                   tok, tok, s5b, tok, tok, tok, tok),
        args=(dh, ys5, o, og, ga, gb, wv, wt, ghg, who, wmo),
    )


def head_fwd_bwd(h, p, tgt, gple, wpg, wpp, gfin, tm=256):
    t = h.shape[0]

    def body(h_ref, p_ref, tgt_ref, gple_ref, wpg_ref, wpp_ref, gfin_ref,
             loss_ref, dh_ref, dgple_ref, dgfin_ref, nb_ref, dlb_ref, dppb_ref):
        first = pl.program_id(0) == 0
        hv = h_ref[...]
        gple = gple_ref[...]
        gfin = gfin_ref[...]
        n, r3 = _rms_fwd(hv, gple)
        nb = n.astype(BF16)
        nb_ref[...] = nb
        pg = _sigmoid(_dot(nb, wpg_ref[...]))
        pb = p_ref[...].astype(BF16)
        pp = jnp.concatenate([_dot(pb, wpp_ref[s]) for s in range(N_SHARD)], axis=1)
        h4 = hv + pg * pp
        y, r4 = _rms_fwd(h4, gfin)
        err = y - tgt_ref[...]
        lsum = 0.5 * jnp.sum(jnp.sum(err * err, axis=-1, keepdims=True), axis=0, keepdims=True) / D_MODEL
        _accum(loss_ref, jnp.broadcast_to(lsum, (8, 128)), first)
        dy = err * (1.0 / D_MODEL)
        dh4, dgf = _rms_bwd(h4, r4, gfin, dy)
        _accum(dgfin_ref, dgf, first)
        dpp = dh4 * pg
        dppb_ref[...] = dpp.astype(BF16)
        dl = (dh4 * pp * pg * (1.0 - pg)).astype(BF16)
        dlb_ref[...] = dl
        dn = _dot_nt(dl, wpg_ref[...])
        dx, dgp = _rms_bwd(hv, r3, gple, dn)
        _accum(dgple_ref, dgp, first)
        dh_ref[...] = dh4 + dx

    tok = pl.BlockSpec((tm, D_MODEL), lambda i: (i, 0))
    vec = pl.BlockSpec((1, D_MODEL), lambda i: (0, 0))
    bft = jax.ShapeDtypeStruct((t, D_MODEL), BF16)
    return pl.pallas_call(
        body, name="head_fwd_bwd", grid=(t // tm,),
        out_shape=(jax.ShapeDtypeStruct((8, 128), F32), jax.ShapeDtypeStruct((t, D_MODEL), F32),
                   jax.ShapeDtypeStruct((1, D_MODEL), F32), jax.ShapeDtypeStruct((1, D_MODEL), F32),
                   bft, bft, bft),
        in_specs=[tok, pl.BlockSpec((tm, PLE_DIM), lambda i: (i, 0)), tok,
                  _const_spec((1, D_MODEL)), _const_spec((D_MODEL, D_MODEL)),
                  _const_spec((N_SHARD, PLE_DIM, 256)), _const_spec((1, D_MODEL))],
        out_specs=(pl.BlockSpec((8, 128), lambda i: (0, 0)), tok, vec, vec, tok, tok, tok),
        compiler_params=_cparams(("arbitrary",)),
    )(h, p, tgt, gple, wpg, wpp, gfin)


BIG = ("ffn1_w_gate", "ffn1_w_up", "ffn1_w_down", "w_in", "s5_glu_val", "s5_glu_gate", "hg_w_out",
       "w_merge_out", "ffn2_w_gate", "ffn2_w_up", "ffn2_w_down", "ple_w_gate", "ple_w_proj")
FFN_T = ("ffn1_w_gate", "ffn1_w_up", "ffn2_w_gate", "ffn2_w_up")
BIG_SHARD = {
    "ffn1_w_gate": (FF_PAD, D_MODEL), "ffn1_w_up": (FF_PAD, D_MODEL), "ffn1_w_down": (FF_PAD, D_MODEL),
    "ffn2_w_gate": (FF_PAD, D_MODEL), "ffn2_w_up": (FF_PAD, D_MODEL), "ffn2_w_down": (FF_PAD, D_MODEL),
    "w_in": (D_MODEL, IN_COLS // N_SHARD), "s5_glu_val": (S5_WIDTH, 256), "s5_glu_gate": (S5_WIDTH, 256),
    "hg_w_out": (256, D_MODEL), "w_merge_out": (256, D_MODEL), "ple_w_gate": (256, D_MODEL),
    "ple_w_proj": (PLE_DIM, 256),
}


def _lower_bound(hb):
    return jax.nn.softmax(hb, axis=0)[0:1]


class Schedule:
    def __init__(self, wts):
        self.wts = dict(wts)
        self.grads = {}

    def before(self, kernel_name):
        return None

    def after(self, kernel_name, results):
        pass

    def grad(self, name, g):
        self.grads[name] = g


def local_step(x, p, tgt, sched, sm):
    wts = sched.wts
    rows_full = lambda w: w.reshape(N_SHARD * w.shape[1], w.shape[2])

    def carried(kernel_name, fn, *args):
        outs, results = fn(*args, comm=sched.before(kernel_name))
        sched.after(kernel_name, results)
        return outs

    lb, lb_vjp = jax.vjp(_lower_bound, sm["hg_lower_bound"])
    s5_names = ("s5_lam_re", "s5_lam_im", "s5_log_dt", "s5_b_re", "s5_b_im", "s5_c_re", "s5_c_im")
    (lam_bar, bmat, cmat), s5_vjp = jax.vjp(s5_prep, *[sm[k] for k in s5_names])
    pw_r, pw_i = _lam_powers(lam_bar)
    bmat_b = bmat.astype(BF16)
    cmat_b = cmat.astype(BF16)
    bmat_t = jnp.swapaxes(bmat, -1, -2).astype(BF16)
    cmat_t = jnp.swapaxes(cmat, -1, -2).astype(BF16)

    h1, a1, b1 = carried("ffn1_fwd", ffn_fwd, x, sm["ffn1_norm"], wts["ffn1_w_gate"], wts["ffn1_w_up"],
                         wts["ffn1_w_down"], "ffn1_fwd")
    s5in, q, f, v, og, ga, gb = inproj_fwd(h1, sm["mix_norm"], wts["w_in"])
    ys5, xp = carried("s5_fwd", s5_fwd, s5in, pw_r, pw_i, bmat_b, cmat_b, sm["s5_d"])
    o, states = hgrn_fwd(q, f, v, lb)
    who = rows_full(wts["hg_w_out"])
    wmo = rows_full(wts["w_merge_out"])
    h2 = merge_fwd(h1, ys5, o, og, ga, gb, wts["s5_glu_val"], wts["s5_glu_gate"], sm["hg_out_norm"], who, wmo)
    (h3, a2, b2), _ = ffn_fwd(h2, sm["ffn2_norm"], wts["ffn2_w_gate"], wts["ffn2_w_up"], wts["ffn2_w_down"], "ffn2_fwd")
    loss, dh3, d_ple_norm, d_final_norm, npb, dlgb, dppb = head_fwd_bwd(
        h3, p, tgt, sm["ple_norm"], rows_full(wts["ple_w_gate"]), wts["ple_w_proj"], sm["final_norm"])

    gs = {"ple_norm": d_ple_norm, "final_norm": d_final_norm}
    sched.grad("ple_w_gate", tn_matmul(npb, dlgb, "g_ple_w_gate", "rows"))
    sched.grad("ple_w_proj", tn_matmul(p, dppb, "g_ple_w_proj", "cols"))

    (dh2, gs["ffn2_norm"], n2b, dhb2, da2, db2, s2), _ = ffn_bwd(
        dh3, h2, a2, b2, sm["ffn2_norm"], wts["ffn2_w_gate"], wts["ffn2_w_up"], wts["ffn2_w_down"], "ffn2_bwd")
    sched.grad("ffn2_w_gate", tn_matmul(da2, n2b, "g_ffn2_w_gate", "rows"))
    sched.grad("ffn2_w_up", tn_matmul(db2, n2b, "g_ffn2_w_up", "rows"))
    sched.grad("ffn2_w_down", tn_matmul(s2, dhb2, "g_ffn2_w_down", "rows"))

    dys5, do, dog, dga, dgb, gs["hg_out_norm"], mixb, dh2b, ysb, dvab, dvtb, o2b, dybb = carried(
        "merge_bwd", merge_bwd,
        dh2, ys5, o, og, ga, gb, wts["s5_glu_val"], wts["s5_glu_gate"], sm["hg_out_norm"], who, wmo)
    sched.grad("w_merge_out", tn_matmul(mixb, dh2b, "g_w_merge_out", "rows"))
    sched.grad("s5_glu_val", tn_matmul(ysb, dvab, "g_s5_glu_val", "cols"))
    sched.grad("s5_glu_gate", tn_matmul(ysb, dvtb, "g_s5_glu_gate", "cols"))
    sched.grad("hg_w_out", tn_matmul(o2b, dybb, "g_hg_w_out", "rows"))

    dq, df, dv, dlb = carried("hgrn_bwd", hgrn_bwd, do, q, f, v, lb, states)
    (gs["hg_lower_bound"],) = lb_vjp(dlb)
    du, dbmat, dcmat, dlam8, gs["s5_d"] = carried(
        "s5_bwd", s5_bwd,
        dys5, s5in, xp, pw_r, pw_i, pw_r[::-1], pw_i[::-1], bmat_b, bmat_t, cmat_t, sm["s5_d"])
    for k, g in zip(s5_names, s5_vjp((jnp.sum(dlam8, axis=1), dbmat, dcmat))):
        gs[k] = g

    dh1, gs["mix_norm"], nmb, dprojb = carried(
        "inproj_bwd", inproj_bwd, dh2, h1, sm["mix_norm"], wts["w_in"], (du, dq, df, dv, dog, dga, dgb))
    sched.grad("w_in", tn_matmul(nmb, dprojb, "g_w_in", "cols"))

    dx, gs["ffn1_norm"], n1b, dhb1, da1, db1, s1 = carried(
        "ffn1_bwd", ffn_bwd,
        dh1, x, a1, b1, sm["ffn1_norm"], wts["ffn1_w_gate"], wts["ffn1_w_up"], wts["ffn1_w_down"], "ffn1_bwd")
    sched.grad("ffn1_w_gate", tn_matmul(da1, n1b, "g_ffn1_w_gate", "rows"))
    sched.grad("ffn1_w_up", tn_matmul(db1, n1b, "g_ffn1_w_up", "rows"))
    sched.grad("ffn1_w_down", tn_matmul(s1, dhb1, "g_ffn1_w_down", "rows"))
    return loss, dx, gs


MESH = pl.DeviceIdType.MESH
ANY = pl.BlockSpec(memory_space=pl.ANY)


def _place():
    x, y, c = lax.axis_index("x"), lax.axis_index("y"), lax.axis_index("c")
    return x, y, c


def _remote(src, dst, ssem, rsem, dev):
    return pltpu.make_async_remote_copy(src_ref=src, dst_ref=dst, send_sem=ssem, recv_sem=rsem,
                                        device_id=dev, device_id_type=MESH)


class Comm:
    def __init__(self, bufs, outs, alias, sems, hooks):
        self.bufs, self.outs, self.alias, self.sems, self.hooks = list(bufs), list(outs), alias, list(sems), hooks


def run_comm(comm, name):
    nb, no = len(comm.bufs), len(comm.outs)

    def body(*refs):
        for which in ("first", "mid", "last"):
            if which in comm.hooks:
                comm.hooks[which](refs[:nb], refs[nb:nb + no], refs[nb + no:])

    return pl.pallas_call(
        body, name=name, out_shape=tuple(comm.outs), in_specs=[ANY] * nb, out_specs=tuple([ANY] * no),
        input_output_aliases=dict(comm.alias), scratch_shapes=comm.sems,
    )(*comm.bufs)


def _carry(body, comm, *, name, steps, out_shape, in_specs, out_specs, args, scratch_shapes=()):
    out_shape, out_specs, scratch_shapes = tuple(out_shape), tuple(out_specs), list(scratch_shapes)
    if comm is None:
        res = pl.pallas_call(body, name=name, grid=(steps,), out_shape=out_shape, in_specs=list(in_specs),
                             out_specs=out_specs, scratch_shapes=scratch_shapes,
                             compiler_params=_cparams(("arbitrary",)))(*args)
        return tuple(res), ()
    n_in, n_out, n_scr = len(args), len(out_shape), len(scratch_shapes)
    nb, no = len(comm.bufs), len(comm.outs)

    def wrapped(*refs):
        ins, cb = refs[:n_in], refs[n_in:n_in + nb]
        o0 = n_in + nb
        outs, co = refs[o0:o0 + n_out], refs[o0 + n_out:o0 + n_out + no]
        s0 = o0 + n_out + no
        scr, cs = refs[s0:s0 + n_scr], refs[s0 + n_scr:]
        step = pl.program_id(0)

        def hook(which, at):
            if which in comm.hooks:
                pl.when(step == at)(lambda: comm.hooks[which](cb, co, cs))

        hook("first", 0)
        hook("mid", steps // 2)
        body(*ins, *outs, *scr)
        hook("last", steps - 1)

    res = pl.pallas_call(
        wrapped, name=name, grid=(steps,), out_shape=out_shape + tuple(comm.outs),
        in_specs=list(in_specs) + [ANY] * nb, out_specs=out_specs + (ANY,) * no,
        scratch_shapes=scratch_shapes + comm.sems,
        input_output_aliases={n_in + i: n_out + o for i, o in comm.alias.items()},
        compiler_params=_cparams(("arbitrary",)),
    )(*args, *comm.bufs)
    return tuple(res[:n_out]), tuple(res[n_out:])


def gather_comm(bufs):
    n = len(bufs)

    def copies(outs, sems):
        s_own, r_own, s_fwd, r_fwd, s_sib, r_sib = sems
        x, y, c = _place()
        me = 2 * x + y
        nbr = ((1 - x, y), (x, 1 - y))
        nbr_id = (2 * (1 - x) + y, 2 * x + (1 - y))
        diag_id = 2 * (1 - x) + (1 - y)
        sib = (x, y, 1 - c)

        def rows(w, q=None):
            r = outs[w].shape[1]
            if q is None:
                return pl.ds(pl.multiple_of(c * (r // 2), 16), r // 2)
            return pl.ds(pl.multiple_of(c * (r // 2) + q * (r // 4), 16), r // 4)

        def own(w, j):
            piece = outs[w].at[me, rows(w)]
            return _remote(piece, piece, s_own.at[w, j], r_own.at[w, j], (nbr[j][0], nbr[j][1], c))

        def from_nbr(w, j):
            piece = outs[w].at[nbr_id[j], rows(w)]
            return _remote(piece, piece, s_own.at[w, j], r_own.at[w, j], (nbr[j][0], nbr[j][1], c))

        def fwd(w, j):
            piece = outs[w].at[nbr_id[j], rows(w, j)]
            return _remote(piece, piece, s_fwd.at[w, j], r_fwd.at[w, j], (nbr[1 - j][0], nbr[1 - j][1], c))

        def from_diag(w, j):
            piece = outs[w].at[diag_id, rows(w, j)]
            return _remote(piece, piece, s_fwd.at[w, j], r_fwd.at[w, j], (nbr[1 - j][0], nbr[1 - j][1], c))

        def to_sib(w, k):
            piece = (outs[w].at[nbr_id[k], rows(w)] if k < 2 else outs[w].at[diag_id, rows(w, k - 2)])
            return _remote(piece, piece, s_sib.at[w, k], r_sib.at[w, k], sib)

        def from_sib(w, k):
            r = outs[w].shape[1]
            if k < 2:
                piece = outs[w].at[nbr_id[k], pl.ds(pl.multiple_of((1 - c) * (r // 2), 16), r // 2)]
            else:
                piece = outs[w].at[diag_id, pl.ds(pl.multiple_of((1 - c) * (r // 2) + (k - 2) * (r // 4), 16), r // 4)]
            return _remote(piece, piece, s_sib.at[w, k], r_sib.at[w, k], sib)

        return own, from_nbr, fwd, from_diag, to_sib, from_sib

    def first(_, outs, sems):
        own = copies(outs, sems)[0]
        for w in range(n):
            own(w, 0).start()
            own(w, 1).start()

    def mid(_, outs, sems):
        _, from_nbr, fwd, _, to_sib, _ = copies(outs, sems)
        for w in range(n):
            for j in range(2):
                from_nbr(w, j).wait_recv()
                fwd(w, j).start()
                to_sib(w, j).start()

    def last(_, outs, sems):
        own, _, fwd, from_diag, to_sib, from_sib = copies(outs, sems)
        for w in range(n):
            for j in range(2):
                from_diag(w, j).wait_recv()
                to_sib(w, 2 + j).start()
        for w in range(n):
            for k in range(4):
                from_sib(w, k).wait_recv()
        for w in range(n):
            for j in range(2):
                own(w, j).wait_send()
                fwd(w, j).wait_send()
            for k in range(4):
                to_sib(w, k).wait_send()

    dma = pltpu.SemaphoreType.DMA
    return Comm(bufs, [jax.ShapeDtypeStruct(b.shape, b.dtype) for b in bufs], {w: w for w in range(n)},
                [dma((n, 2)), dma((n, 2)), dma((n, 2)), dma((n, 2)), dma((n, 4)), dma((n, 4))],
                {"first": first, "mid": mid, "last": last})


def _start_wait(make):
    def first(bufs, outs, sems):
        for cp in make(bufs, outs, sems):
            cp.start()

    def last(bufs, outs, sems):
        for cp in make(bufs, outs, sems):
            cp.wait()

    return {"first": first, "last": last}


def exchange_comm(grads):
    n = len(grads)

    def make(ins, outs, sems):
        x, y, c = _place()
        cps = []
        for w in range(n):
            half = ins[w].shape[1] // 2
            src = ins[w].at[:, pl.ds(pl.multiple_of((1 - c) * half, 8), half), :]
            cps.append(_remote(src, outs[w], sems[0].at[w], sems[1].at[w], (x, y, 1 - c)))
        return cps

    dma = pltpu.SemaphoreType.DMA
    return Comm(grads, [jax.ShapeDtypeStruct((N_SHARD, g.shape[1] // 2, g.shape[2]), g.dtype) for g in grads],
                {}, [dma((n,)), dma((n,))], _start_wait(make))


def scatter_comm(sums):
    n = len(sums)

    def make(ins, outs, sems):
        x, y, c = _place()
        chips = ((1 - x, y), (x, 1 - y), (1 - x, 1 - y))
        return [_remote(ins[w].at[2 * ch[0] + ch[1]], outs[w].at[j], sems[0].at[w, j], sems[1].at[w, j],
                        (ch[0], ch[1], c))
                for w in range(n) for j, ch in enumerate(chips)]

    dma = pltpu.SemaphoreType.DMA
    return Comm(sums, [jax.ShapeDtypeStruct((3,) + s.shape[1:], s.dtype) for s in sums],
                {}, [dma((n, 3)), dma((n, 3))], _start_wait(make))


def join_comm(shards):
    n = len(shards)

    def make(_, outs, sems):
        x, y, c = _place()
        cps = []
        for w in range(n):
            half = outs[w].shape[0] // 2
            mine = outs[w].at[pl.ds(pl.multiple_of(c * half, 8), half), :]
            cps.append(_remote(mine, mine, sems[0].at[w], sems[1].at[w], (x, y, 1 - c)))
        return cps

    dma = pltpu.SemaphoreType.DMA
    return Comm(shards, [jax.ShapeDtypeStruct(s.shape, s.dtype) for s in shards], {w: w for w in range(n)},
                [dma((n,)), dma((n,))], _start_wait(make))


def allreduce_small(vec):
    rows = vec.shape[0]

    def body(v_ref, o_ref, buf, ssem, rsem):
        x, y, c = _place()
        me = 4 * x + 2 * y + c
        cps = []
        for k in range(1, 8):
            fx, fy, fc = (k >> 2) & 1, (k >> 1) & 1, k & 1
            dev = (x ^ fx, y ^ fy, c ^ fc)
            cp = _remote(v_ref, buf.at[k], ssem.at[k], rsem.at[k], dev)
            cp.start()
            cps.append(cp)
        buf[0] = v_ref[...]
        for cp in cps:
            cp.wait()
        acc = buf[me]
        for a in range(1, 8):
            acc = acc + buf[a ^ me]
        o_ref[...] = acc

    return pl.pallas_call(
        body, name="allreduce_small",
        out_shape=jax.ShapeDtypeStruct(vec.shape, F32),
        in_specs=[pl.BlockSpec(memory_space=pltpu.VMEM)],
        out_specs=pl.BlockSpec(memory_space=pltpu.VMEM),
        scratch_shapes=[pltpu.VMEM((8, rows, 128), F32), pltpu.SemaphoreType.DMA((8,)), pltpu.SemaphoreType.DMA((8,))],
        compiler_params=pltpu.CompilerParams(vmem_limit_bytes=VMEM_LIMIT),
    )(vec)


ROW_TILE = 128


def add_own_half(core, g, recv, name):
    _, r, cc = g.shape
    half = r // 2
    nb = half // ROW_TILE

    def body(c_ref, g_ref, r_ref, o_ref, ob_ref):
        del c_ref
        s = g_ref[...] + r_ref[...]
        o_ref[...] = s
        ob_ref[...] = s.astype(BF16)

    blk = (None, ROW_TILE, cc)
    return pl.pallas_call(
        body, name=name,
        grid_spec=pltpu.PrefetchScalarGridSpec(
            num_scalar_prefetch=1, grid=(N_SHARD, nb),
            in_specs=[pl.BlockSpec(blk, lambda s, i, c_ref: (s, c_ref[0] * nb + i, 0)),
                      pl.BlockSpec(blk, lambda s, i, c_ref: (s, i, 0))],
            out_specs=(pl.BlockSpec(blk, lambda s, i, c_ref: (s, i, 0)),
                       pl.BlockSpec(blk, lambda s, i, c_ref: (s, i, 0)))),
        out_shape=(jax.ShapeDtypeStruct((N_SHARD, half, cc), F32),
                   jax.ShapeDtypeStruct((N_SHARD, half, cc), BF16)),
        compiler_params=_cparams(("arbitrary", "arbitrary")),
    )(core, g, recv)


def add_chip_sums(place, own, recv, name):
    _, half, cc = own.shape
    nb = half // ROW_TILE

    def body(s_ref, o_ref, r_ref, out_ref):
        del s_ref
        acc = o_ref[...] + r_ref[0].astype(F32)
        acc = acc + r_ref[1].astype(F32)
        out_ref[...] = acc + r_ref[2].astype(F32)

    return pl.pallas_call(
        body, name=name,
        grid_spec=pltpu.PrefetchScalarGridSpec(
            num_scalar_prefetch=1, grid=(nb,),
            in_specs=[pl.BlockSpec((None, ROW_TILE, cc), lambda i, s_ref: (s_ref[0], i, 0)),
                      pl.BlockSpec((3, ROW_TILE, cc), lambda i, s_ref: (0, i, 0))],
            out_specs=pl.BlockSpec((ROW_TILE, cc), lambda i, s_ref: (s_ref[1] * nb + i, 0))),
        out_shape=jax.ShapeDtypeStruct((2 * half, cc), F32),
        compiler_params=_cparams(("arbitrary",)),
    )(place, own, recv)


def adamw(w, m, v, g, name, copy_g=False):
    r, cc = w.shape
    tr = next(t for t in (256, 352, r) if r % t == 0)
    bc1 = 1.0 / (1.0 - ADAM_B1 ** ADAM_STEP)
    bc2 = 1.0 / (1.0 - ADAM_B2 ** ADAM_STEP)

    def body(w_ref, m_ref, v_ref, g_ref, d_ref, mo_ref, vo_ref, *go_ref):
        gv = g_ref[...]
        mn = ADAM_B1 * m_ref[...] + (1.0 - ADAM_B1) * gv
        vn = ADAM_B2 * v_ref[...] + (1.0 - ADAM_B2) * (gv * gv)
        mo_ref[...] = mn
        vo_ref[...] = vn
        d_ref[...] = -ADAM_LR * ((mn * bc1) / (jnp.sqrt(vn * bc2) + ADAM_EPS) + ADAM_WD * w_ref[...])
        if copy_g:
            go_ref[0][...] = gv

    blk = pl.BlockSpec((tr, cc), lambda i: (i, 0))
    shp = jax.ShapeDtypeStruct((r, cc), F32)
    nout = 4 if copy_g else 3
    return pl.pallas_call(
        body, name=name, grid=(r // tr,),
        out_shape=(shp,) * nout, in_specs=[blk] * 4, out_specs=(blk,) * nout,
        compiler_params=_cparams(("arbitrary",)),
    )(w, m, v, g)


GATHER_FIRST = ("ffn1_w_gate", "ffn1_w_up", "ffn1_w_down")
GATHER_ON = {"ffn1_fwd": ("w_in", "s5_glu_val", "s5_glu_gate", "hg_w_out", "w_merge_out"),
             "s5_fwd": ("ffn2_w_gate", "ffn2_w_up", "ffn2_w_down", "ple_w_gate", "ple_w_proj")}
REDUCE = ((("ple_w_gate", "ple_w_proj", "ffn2_w_gate", "ffn2_w_up", "ffn2_w_down"), "merge_bwd", "hgrn_bwd"),
          (("w_merge_out", "s5_glu_val", "s5_glu_gate", "hg_w_out"), "s5_bwd", "inproj_bwd"),
          (("w_in",), None, "ffn1_bwd"),
          (("ffn1_w_gate", "ffn1_w_up", "ffn1_w_down"), None, None))


class DistSchedule(Schedule):
    def __init__(self, bufs, chip, core):
        first = run_comm(gather_comm([bufs[k] for k in GATHER_FIRST]), "gather_ffn1")
        super().__init__(zip(GATHER_FIRST, first))
        self.bufs = bufs
        self.core = core.reshape(1)
        self.place = jnp.stack([chip, core])
        self.sums, self.halves = {}, {}

    def _exchange(self, names):
        return exchange_comm([self.grads[k] for k in names])

    def _scatter(self, names):
        return scatter_comm([self.sums[k][1] for k in names])

    def _pair_sums(self, names, recv):
        for k, r in zip(names, recv):
            self.sums[k] = add_own_half(self.core, self.grads[k], r, "pair_sum_" + k)

    def _chip_sums(self, names, recv):
        for k, r in zip(names, recv):
            self.halves[k] = add_chip_sums(self.place, self.sums[k][0], r, "chip_sum_" + k)

    def before(self, kernel_name):
        if kernel_name in GATHER_ON:
            return gather_comm([self.bufs[k] for k in GATHER_ON[kernel_name]])
        for names, exchange_on, scatter_on in REDUCE:
            if kernel_name == exchange_on:
                return self._exchange(names)
            if kernel_name == scatter_on:
                if exchange_on is None:
                    self._pair_sums(names, run_comm(self._exchange(names), "exchange_" + names[0]))
                return self._scatter(names)
        return None

    def after(self, kernel_name, results):
        if kernel_name in GATHER_ON:
            self.wts.update(zip(GATHER_ON[kernel_name], results))
        for names, exchange_on, scatter_on in REDUCE:
            if kernel_name == exchange_on:
                self._pair_sums(names, results)
            if kernel_name == scatter_on:
                self._chip_sums(names, results)

    def finish(self):
        for names, exchange_on, scatter_on in REDUCE:
            if scatter_on is None:
                self._pair_sums(names, run_comm(self._exchange(names), "exchange_" + names[0]))
                self._chip_sums(names, run_comm(self._scatter(names), "scatter_" + names[0]))
        return dict(zip(BIG, run_comm(join_comm([self.halves[k] for k in BIG]), "join_halves")))


SMALL = ("ffn1_norm", "mix_norm", "s5_lam_re", "s5_lam_im", "s5_log_dt", "s5_b_re", "s5_b_im", "s5_c_re",
         "s5_c_im", "s5_d", "hg_lower_bound", "hg_out_norm", "ffn2_norm", "ple_norm", "final_norm")
WEIGHTS = ("ffn1_norm", "ffn1_w_gate", "ffn1_w_up", "ffn1_w_down", "mix_norm", "w_in", "s5_lam_re", "s5_lam_im",
           "s5_log_dt", "s5_b_re", "s5_b_im", "s5_c_re", "s5_c_im", "s5_d", "s5_glu_val", "s5_glu_gate",
           "hg_lower_bound", "hg_out_norm", "hg_w_out", "w_merge_out", "ffn2_norm", "ffn2_w_gate", "ffn2_w_up",
           "ffn2_w_down", "ple_norm", "ple_w_gate", "ple_w_proj", "final_norm")


def _as_rows(name, w):
    return jnp.swapaxes(w[0], 0, 1) if name in FFN_T else w[0]


def _from_rows(name, w):
    return (jnp.swapaxes(w, 0, 1) if name in FFN_T else w)[None]


def _gather_buffer(name, w_rows, chip):
    r, c = BIG_SHARD[name]
    shard = jnp.pad(w_rows.astype(BF16), ((0, r - w_rows.shape[0]), (0, 0)))
    return lax.dynamic_update_slice(jnp.zeros((N_SHARD, r, c), BF16), shard[None], (chip, 0, 0))


def _pack(parts):
    flat = jnp.concatenate([jnp.zeros((128,), F32)] + [a.reshape(-1) for a in parts])
    rows = -(-flat.shape[0] // 1024) * 8
    return jnp.pad(flat, (0, rows * 128 - flat.shape[0])).reshape(rows, 128)


def _unpack(vec, likes):
    flat = vec.reshape(-1)
    out, off = [], 128
    for a in likes:
        out.append(flat[off:off + a.size].reshape(a.shape))
        off += a.size
    return out


def _small_view(name, w):
    if name.startswith("s5_") and name != "s5_d":
        return w[0]
    if name == "final_norm":
        return w.reshape(1, D_MODEL)
    return w


def kernel(x, p, ffn1_norm, ffn1_w_gate, ffn1_w_up, ffn1_w_down, mix_norm, w_in, s5_lam_re, s5_lam_im, s5_log_dt, s5_b_re, s5_b_im, s5_c_re, s5_c_im, s5_d, s5_glu_val, s5_glu_gate, hg_lower_bound, hg_out_norm, hg_w_out, w_merge_out, ffn2_norm, ffn2_w_gate, ffn2_w_up, ffn2_w_down, ple_norm, ple_w_gate, ple_w_proj, final_norm, loss_target, m_ffn1_norm, m_ffn1_w_gate, m_ffn1_w_up, m_ffn1_w_down, m_mix_norm, m_w_in, m_s5_lam_re, m_s5_lam_im, m_s5_log_dt, m_s5_b_re, m_s5_b_im, m_s5_c_re, m_s5_c_im, m_s5_d, m_s5_glu_val, m_s5_glu_gate, m_hg_lower_bound, m_hg_out_norm, m_hg_w_out, m_w_merge_out, m_ffn2_norm, m_ffn2_w_gate, m_ffn2_w_up, m_ffn2_w_down, m_ple_norm, m_ple_w_gate, m_ple_w_proj, m_final_norm, v_ffn1_norm, v_ffn1_w_gate, v_ffn1_w_up, v_ffn1_w_down, v_mix_norm, v_w_in, v_s5_lam_re, v_s5_lam_im, v_s5_log_dt, v_s5_b_re, v_s5_b_im, v_s5_c_re, v_s5_c_im, v_s5_d, v_s5_glu_val, v_s5_glu_gate, v_hg_lower_bound, v_hg_out_norm, v_hg_w_out, v_w_merge_out, v_ffn2_norm, v_ffn2_w_gate, v_ffn2_w_up, v_ffn2_w_down, v_ple_norm, v_ple_w_gate, v_ple_w_proj, v_final_norm):
    given = dict(locals())
    wv = {k: given[k] for k in WEIGHTS}
    mv = {k: given["m_" + k] for k in WEIGHTS}
    vv = {k: given["v_" + k] for k in WEIGHTS}

    core = lax.axis_index("c").astype(jnp.int32)
    chip = (2 * lax.axis_index("x") + lax.axis_index("y")).astype(jnp.int32)
    w_rows = {k: _as_rows(k, wv[k]) for k in BIG}
    sched = DistSchedule({k: _gather_buffer(k, w_rows[k], chip) for k in BIG}, chip, core)
    sm = {k: _small_view(k, wv[k]) for k in SMALL}

    loss_blk, dx, gsm = local_step(x[0], p[0, 0], loss_target[0], sched, sm)
    full = sched.finish()

    small_likes = [wv[k] for k in SMALL]
    packed = _pack([gsm[k] for k in SMALL])
    packed = packed.at[0, 0].set(loss_blk[0, 0])
    total = allreduce_small(packed)
    loss = total[0, 0]
    gsmall = dict(zip(SMALL, _unpack(total, small_likes)))

    grads, deltas, new_m, new_v = {}, {}, {}, {}
    for k in BIG:
        padded = full[k].shape != w_rows[k].shape
        res = adamw(w_rows[k], _as_rows(k, mv[k]), _as_rows(k, vv[k]), full[k], "adamw_" + k, copy_g=padded)
        grads[k] = _from_rows(k, res[3] if padded else full[k])
        deltas[k], new_m[k], new_v[k] = (_from_rows(k, a) for a in res[:3])
    sw = _pack([wv[k] for k in SMALL])
    smm = _pack([mv[k] for k in SMALL])
    svv = _pack([vv[k] for k in SMALL])
    sd, smn, svn = adamw(sw, smm, svv, total, "adamw_small")
    for k, d, mn, vn in zip(SMALL, _unpack(sd, small_likes), _unpack(smn, small_likes), _unpack(svn, small_likes)):
        grads[k], deltas[k], new_m[k], new_v[k] = gsmall[k], d, mn, vn

    return (loss, dx[None], *[grads[k] for k in WEIGHTS], *[deltas[k] for k in WEIGHTS],
            *[new_m[k] for k in WEIGHTS], *[new_v[k] for k in WEIGHTS])
```

```python
import math

import jax
import jax.numpy as jnp
from jax import lax
from jax.experimental import pallas as pl
from jax.experimental.pallas import tpu as pltpu

F32 = jnp.float32
BF16 = jnp.bfloat16

D_MODEL = 1024
D_FF = 2816
N_SHARD = 4
FF_SHARD = D_FF // N_SHARD
FF_PAD = 768
NORM_EPS = 1e-6
PLE_DIM = 256

S5_WIDTH = 512
S5_GROUPS = 32
S5_GROUP = 16
S5_STATE = 64
S5_N = S5_GROUPS * S5_STATE
S5_KT = 2

HG_HEADS = 8
HG_E = 128
HG_WIDTH = 1024
CHUNK = 64
IN_COLS = S5_WIDTH + 4 * HG_WIDTH + 2 * D_MODEL
IN_SPLITS = (0, 512, 1536, 2560, 3584, 4608, 5632, 6656)

ADAM_LR = 0.001
ADAM_B1 = 0.9
ADAM_B2 = 0.999
ADAM_EPS = 1e-08
ADAM_WD = 0.01
ADAM_STEP = 10

VMEM_LIMIT = 60 * 1024 * 1024
HIGHEST = lax.Precision.HIGHEST


def _cparams(sem=None, **kw):
    return pltpu.CompilerParams(dimension_semantics=sem, vmem_limit_bytes=VMEM_LIMIT, **kw)


def _const_spec(shape):
    nd = len(shape)
    return pl.BlockSpec(shape, lambda *_: (0,) * nd, pipeline_mode=pl.Buffered(1))


def _dot(a, b):
    return jnp.dot(a, b, preferred_element_type=F32)


def _dot_nt(a, b):
    return lax.dot_general(a, b, (((1,), (1,)), ((), ())), preferred_element_type=F32)


def _dot_tn(a, b):
    return lax.dot_general(a, b, (((0,), (0,)), ((), ())), preferred_element_type=F32)


def _sigmoid(x):
    return 1.0 / (1.0 + jnp.exp(-x))


def _rms_fwd(x, g):
    r = lax.rsqrt(jnp.mean(x * x, axis=-1, keepdims=True) + NORM_EPS)
    return x * r * g, r


def _rms_bwd(x, r, g, dy):
    xh = x * r
    dyg = dy * g
    m = jnp.mean(dyg * xh, axis=-1, keepdims=True)
    return r * (dyg - xh * m), jnp.sum(dy * xh, axis=0, keepdims=True)


def _accum(ref, val, first):
    @pl.when(first)
    def _():
        ref[...] = val

    @pl.when(jnp.logical_not(first))
    def _():
        ref[...] += val


def ffn_fwd(h, gain, wg, wu, wd, name, comm=None, tm=256):
    t = h.shape[0]

    def body(h_ref, g_ref, wg_ref, wu_ref, wd_ref, o_ref, a_ref, b_ref):
        hv = h_ref[...]
        n, _ = _rms_fwd(hv, g_ref[...])
        nb = n.astype(BF16)
        acc = jnp.zeros((tm, D_MODEL), F32)
        for s in range(N_SHARD):
            a = _dot_nt(nb, wg_ref[s])
            b = _dot_nt(nb, wu_ref[s])
            a_ref[s] = a.astype(BF16)
            b_ref[s] = b.astype(BF16)
            sv = (a * _sigmoid(a) * b).astype(BF16)
            acc = acc + _dot(sv, wd_ref[s])
        o_ref[...] = hv + 0.5 * acc

    return _carry(
        body, comm, name=name, steps=t // tm,
        out_shape=(jax.ShapeDtypeStruct((t, D_MODEL), F32),
                   jax.ShapeDtypeStruct((N_SHARD, t, FF_PAD), BF16),
                   jax.ShapeDtypeStruct((N_SHARD, t, FF_PAD), BF16)),
        in_specs=[pl.BlockSpec((tm, D_MODEL), lambda i: (i, 0)),
                  _const_spec((1, D_MODEL)),
                  _const_spec((N_SHARD, FF_PAD, D_MODEL)),
                  _const_spec((N_SHARD, FF_PAD, D_MODEL)),
                  _const_spec((N_SHARD, FF_PAD, D_MODEL))],
        out_specs=(pl.BlockSpec((tm, D_MODEL), lambda i: (i, 0)),
                   pl.BlockSpec((N_SHARD, tm, FF_PAD), lambda i: (0, i, 0)),
                   pl.BlockSpec((N_SHARD, tm, FF_PAD), lambda i: (0, i, 0))),
        args=(h, gain, wg, wu, wd),
    )


def ffn_bwd(dho, h, a, b, gain, wg, wu, wd, name, comm=None, tm=256):
    t = h.shape[0]

    def body(dho_ref, h_ref, a_ref, b_ref, g_ref, wg_ref, wu_ref, wd_ref,
             dh_ref, dg_ref, nb_ref, dhb_ref, da_ref, db_ref, s_ref):
        hv = h_ref[...]
        g = g_ref[...]
        n, r = _rms_fwd(hv, g)
        nb_ref[...] = n.astype(BF16)
        dhalf = (0.5 * dho_ref[...]).astype(BF16)
        dhb_ref[...] = dhalf
        dn = jnp.zeros((tm, D_MODEL), F32)
        for s in range(N_SHARD):
            av = a_ref[s].astype(F32)
            bv = b_ref[s].astype(F32)
            sg = _sigmoid(av)
            sil = av * sg
            s_ref[s] = (sil * bv).astype(BF16)
            ds = _dot_nt(dhalf, wd_ref[s])
            da = (ds * bv * (sg * (1.0 + av * (1.0 - sg)))).astype(BF16)
            db = (ds * sil).astype(BF16)
            da_ref[s] = da
            db_ref[s] = db
            dn = dn + _dot(da, wg_ref[s]) + _dot(db, wu_ref[s])
        dx, dg = _rms_bwd(hv, r, g, dn)
        dh_ref[...] = dho_ref[...] + dx
        _accum(dg_ref, dg, pl.program_id(0) == 0)

    tok = pl.BlockSpec((tm, D_MODEL), lambda i: (i, 0))
    hid = pl.BlockSpec((N_SHARD, tm, FF_PAD), lambda i: (0, i, 0))
    return _carry(
        body, comm, name=name, steps=t // tm,
        out_shape=(jax.ShapeDtypeStruct((t, D_MODEL), F32),
                   jax.ShapeDtypeStruct((1, D_MODEL), F32),
                   jax.ShapeDtypeStruct((t, D_MODEL), BF16),
                   jax.ShapeDtypeStruct((t, D_MODEL), BF16),
                   jax.ShapeDtypeStruct((N_SHARD, t, FF_PAD), BF16),
                   jax.ShapeDtypeStruct((N_SHARD, t, FF_PAD), BF16),
                   jax.ShapeDtypeStruct((N_SHARD, t, FF_PAD), BF16)),
        in_specs=[tok, tok, hid, hid, _const_spec((1, D_MODEL)),
                  _const_spec((N_SHARD, FF_PAD, D_MODEL)),
                  _const_spec((N_SHARD, FF_PAD, D_MODEL)),
                  _const_spec((N_SHARD, FF_PAD, D_MODEL))],
        out_specs=(tok, pl.BlockSpec((1, D_MODEL), lambda i: (0, 0)), tok, tok, hid, hid, hid),
        args=(dho, h, a, b, gain, wg, wu, wd),
    )


TN_VMEM_BUDGET = 44 * 1024 * 1024


def tn_matmul(x, y, name, shard):
    x3, y3 = x.ndim == 3, y.ndim == 3
    t = x.shape[-2]
    m = x.shape[-1] // (N_SHARD if (shard == "rows" and not x3) else 1)
    n = y.shape[-1] // (N_SHARD if (shard == "cols" and not y3) else 1)
    per_token = 2 * (m * x.dtype.itemsize + n * y.dtype.itemsize)
    tk = t
    while tk > 512 and tk * per_token + 2 * m * n * 4 > TN_VMEM_BUDGET:
        tk //= 2
    nk = t // tk

    def body(x_ref, y_ref, o_ref):
        _accum(o_ref, _dot_tn(x_ref[...].astype(BF16), y_ref[...].astype(BF16)), pl.program_id(1) == 0)

    if x3:
        x_spec = pl.BlockSpec((None, tk, m), lambda s, k: (s, k, 0))
    elif shard == "rows":
        x_spec = pl.BlockSpec((tk, m), lambda s, k: (k, s))
    else:
        x_spec = pl.BlockSpec((tk, m), lambda s, k: (k, 0))
    if y3:
        y_spec = pl.BlockSpec((None, tk, n), lambda s, k: (s, k, 0))
    elif shard == "cols":
        y_spec = pl.BlockSpec((tk, n), lambda s, k: (k, s))
    else:
        y_spec = pl.BlockSpec((tk, n), lambda s, k: (k, 0))
    return pl.pallas_call(
        body, name=name, grid=(N_SHARD, nk),
        out_shape=jax.ShapeDtypeStruct((N_SHARD, m, n), F32),
        in_specs=[x_spec, y_spec],
        out_specs=pl.BlockSpec((None, m, n), lambda s, k: (s, 0, 0)),
        compiler_params=_cparams(("arbitrary", "arbitrary")),
    )(x, y)


def inproj_fwd(h, gain, w_in, tm=256):
    t = h.shape[0]
    widths = [IN_SPLITS[j + 1] - IN_SPLITS[j] for j in range(7)]
    sh_cols = IN_COLS // N_SHARD

    def body(h_ref, g_ref, w_ref, *outs):
        n, _ = _rms_fwd(h_ref[...], g_ref[...])
        nb = n.astype(BF16)
        proj = jnp.concatenate([_dot(nb, w_ref[s]) for s in range(N_SHARD)], axis=1)
        for j, o_ref in enumerate(outs):
            o_ref[...] = proj[:, IN_SPLITS[j]:IN_SPLITS[j + 1]]

    return pl.pallas_call(
        body, name="inproj_fwd", grid=(t // tm,),
        out_shape=tuple(jax.ShapeDtypeStruct((t, w), F32) for w in widths),
        in_specs=[pl.BlockSpec((tm, D_MODEL), lambda i: (i, 0)),
                  _const_spec((1, D_MODEL)),
                  _const_spec((N_SHARD, D_MODEL, sh_cols))],
        out_specs=tuple(pl.BlockSpec((tm, w), lambda i: (i, 0)) for w in widths),
        compiler_params=_cparams(("arbitrary",)),
    )(h, gain, w_in)


def inproj_bwd(dres, h, gain, w_in, dparts, comm=None, tm=256):
    t = h.shape[0]
    widths = [IN_SPLITS[j + 1] - IN_SPLITS[j] for j in range(7)]
    sh_cols = IN_COLS // N_SHARD

    def body(dres_ref, h_ref, g_ref, w_ref, d0, d1, d2, d3, d4, d5, d6, dh_ref, dg_ref, nb_ref, dp_ref):
        hv = h_ref[...]
        g = g_ref[...]
        n, r = _rms_fwd(hv, g)
        nb_ref[...] = n.astype(BF16)
        dproj = jnp.concatenate([d[...] for d in (d0, d1, d2, d3, d4, d5, d6)], axis=1).astype(BF16)
        dp_ref[...] = dproj
        dn = jnp.zeros((tm, D_MODEL), F32)
        for s in range(N_SHARD):
            dn = dn + _dot_nt(dproj[:, s * sh_cols:(s + 1) * sh_cols], w_ref[s])
        dx, dg = _rms_bwd(hv, r, g, dn)
        dh_ref[...] = dres_ref[...] + dx
        _accum(dg_ref, dg, pl.program_id(0) == 0)

    tok = pl.BlockSpec((tm, D_MODEL), lambda i: (i, 0))
    return _carry(
        body, comm, name="inproj_bwd", steps=t // tm,
        out_shape=(jax.ShapeDtypeStruct((t, D_MODEL), F32),
                   jax.ShapeDtypeStruct((1, D_MODEL), F32),
                   jax.ShapeDtypeStruct((t, D_MODEL), BF16),
                   jax.ShapeDtypeStruct((t, IN_COLS), BF16)),
        in_specs=[tok, tok, _const_spec((1, D_MODEL)), _const_spec((N_SHARD, D_MODEL, sh_cols))]
                 + [pl.BlockSpec((tm, w), lambda i: (i, 0)) for w in widths],
        out_specs=(tok, pl.BlockSpec((1, D_MODEL), lambda i: (0, 0)), tok,
                   pl.BlockSpec((tm, IN_COLS), lambda i: (i, 0))),
        args=(dres, h, gain, w_in, *dparts),
    )


def s5_prep(lam_re, lam_im, log_dt, b_re, b_im, c_re, c_im):
    dt = jnp.exp(log_dt)[:, None]
    mag = jnp.exp(lam_re * dt)
    lbr = mag * jnp.cos(lam_im * dt)
    lbi = mag * jnp.sin(lam_im * dt)
    den = lam_re * lam_re + lam_im * lam_im
    nr, ni = lbr - 1.0, lbi
    kr = (nr * lam_re + ni * lam_im) / den
    ki = (ni * lam_re - nr * lam_im) / den
    bbr = kr[..., None] * b_re - ki[..., None] * b_im
    bbi = kr[..., None] * b_im + ki[..., None] * b_re
    eye = jnp.eye(16, dtype=F32)

    def bm(bp):
        return jnp.einsum('kgph,gG->kghGp', bp.reshape(S5_KT, 16, S5_STATE, S5_GROUP), eye).reshape(S5_KT, 256, 1024)

    def cm(cp):
        return jnp.einsum('kghp,gG->kgpGh', cp.reshape(S5_KT, 16, S5_GROUP, S5_STATE), eye).reshape(S5_KT, 1024, 256)

    lam_bar = jnp.stack([lbr.reshape(S5_N), lbi.reshape(S5_N)])
    bmat = jnp.stack([bm(bbr), bm(bbi)])
    cmat = jnp.stack([cm(c_re), -cm(c_im)])
    return lam_bar, bmat, cmat


def _lam_powers(lam_bar):
    lr, li = lam_bar[0], lam_bar[1]
    pr, pi = [lr], [li]
    for _ in range(7):
        pr, pi = pr + [pr[-1] * lr - pi[-1] * li], pi + [pr[-1] * li + pi[-1] * lr]
    return jnp.stack(pr), jnp.stack(pi)


SCAN_SHIFTS = ((1, 0), (2, 1), (4, 3))


def _scan_tables(pw_r, pw_i, reverse):
    rows = jnp.arange(8)[:, None]
    planes_r, planes_i = [], []
    for sh, idx in SCAN_SHIFTS:
        keep = (rows < 8 - sh) if reverse else (rows >= sh)
        planes_r.append(jnp.where(keep, pw_r[idx:idx + 1], 0.0))
        planes_i.append(jnp.where(keep, pw_i[idx:idx + 1], 0.0))
    carry = [pw_r[::-1], pw_i[::-1]] if reverse else [pw_r, pw_i]
    return jnp.stack(planes_r + planes_i + carry)


def s5_fwd(u, tab, bmat, cmat, dvec, comm=None, tm=256):
    t = u.shape[0]
    nch = tm // 8

    def body(u_ref, tab_ref, b_ref, c_ref, d_ref, y_ref, xp_ref, x_scr, carry):
        @pl.when(pl.program_id(0) == 0)
        def _():
            carry[...] = jnp.zeros_like(carry)

        uv = u_ref[...]
        ub = uv.astype(BF16)
        for part in range(2):
            for kt in range(S5_KT):
                x_scr[:, pl.ds(part * S5_N + kt * 1024, 1024)] = _dot(ub[:, kt * 256:(kt + 1) * 256], b_ref[part, kt])
        row = lax.broadcasted_iota(jnp.int32, (8, S5_N), 0)

        def chunk(i, c):
            cr, ci = c
            r0 = pl.multiple_of(i * 8, 8)
            xr = x_scr[pl.ds(r0, 8), pl.ds(0, S5_N)]
            xi = x_scr[pl.ds(r0, 8), pl.ds(S5_N, S5_N)]
            for lvl, (sh, _) in enumerate(SCAN_SHIFTS):
                sr = pltpu.roll(xr, sh, 0)
                si = pltpu.roll(xi, sh, 0)
                lr = tab_ref[lvl]
                li = tab_ref[3 + lvl]
                xr, xi = xr + lr * sr - li * si, xi + lr * si + li * sr
            pwr = tab_ref[6]
            pwi = tab_ref[7]
            xr, xi = xr + pwr * cr - pwi * ci, xi + pwr * ci + pwi * cr
            x_scr[pl.ds(r0, 8), pl.ds(0, S5_N)] = xr
            x_scr[pl.ds(r0, 8), pl.ds(S5_N, S5_N)] = xi
            xp_ref[pl.ds(r0, 8), pl.ds(0, S5_N)] = jnp.where(row == 0, cr, pltpu.roll(xr, 1, 0))
            xp_ref[pl.ds(r0, 8), pl.ds(S5_N, S5_N)] = jnp.where(row == 0, ci, pltpu.roll(xi, 1, 0))
            return xr[7:8, :], xi[7:8, :]

        cr, ci = lax.fori_loop(0, nch, chunk, (carry[0:1, :], carry[1:2, :]))
        carry[0:1, :] = cr
        carry[1:2, :] = ci
        for kt in range(S5_KT):
            acc = jnp.zeros((tm, 256), F32)
            for part in range(2):
                acc = acc + _dot(x_scr[:, pl.ds(part * S5_N + kt * 1024, 1024)].astype(BF16), c_ref[part, kt])
            y_ref[:, pl.ds(kt * 256, 256)] = acc + d_ref[:, pl.ds(kt * 256, 256)] * uv[:, kt * 256:(kt + 1) * 256]

    return _carry(
        body, comm, name="s5_fwd", steps=t // tm,
        out_shape=(jax.ShapeDtypeStruct((t, S5_WIDTH), F32),
                   jax.ShapeDtypeStruct((t, 2 * S5_N), F32)),
        in_specs=[pl.BlockSpec((tm, S5_WIDTH), lambda i: (i, 0)),
                  _const_spec((8, 8, S5_N)),
                  _const_spec((2, S5_KT, 256, 1024)), _const_spec((2, S5_KT, 1024, 256)),
                  _const_spec((1, S5_WIDTH))],
        out_specs=(pl.BlockSpec((tm, S5_WIDTH), lambda i: (i, 0)),
                   pl.BlockSpec((tm, 2 * S5_N), lambda i: (i, 0))),
        scratch_shapes=[pltpu.VMEM((tm, 2 * S5_N), F32), pltpu.VMEM((8, S5_N), F32)],
        args=(u, tab, bmat, cmat, dvec),
    )


def s5_bwd(dy, u, xp, tab, bmat, bmat_t, cmat_t, dvec, comm=None, tm=256):
    t = u.shape[0]
    nt = t // tm
    nch = tm // 8

    def body(dy_ref, u_ref, xp_ref, tab_ref, b_ref, bt_ref, ct_ref, d_ref,
             du_ref, db_ref, dc_ref, dl_ref, dd_ref, g_scr, x_scr, carry):
        first = pl.program_id(0) == 0

        @pl.when(first)
        def _():
            carry[...] = jnp.zeros_like(carry)
            dl_ref[...] = jnp.zeros_like(dl_ref)

        dyv = dy_ref[...]
        uv = u_ref[...]
        dyb = dyv.astype(BF16)
        ub = uv.astype(BF16)
        lr1 = tab_ref[6, 7:8, :]
        li1 = tab_ref[7, 7:8, :]
        for kt in range(S5_KT):
            cols = pl.ds(kt * 1024, 1024)
            colsi = pl.ds(S5_N + kt * 1024, 1024)
            g_scr[:, cols] = _dot(dyb[:, kt * 256:(kt + 1) * 256], ct_ref[0, kt])
            g_scr[:, colsi] = _dot(dyb[:, kt * 256:(kt + 1) * 256], ct_ref[1, kt])
            bur = _dot(ub[:, kt * 256:(kt + 1) * 256], b_ref[0, kt])
            bui = _dot(ub[:, kt * 256:(kt + 1) * 256], b_ref[1, kt])
            xpr = xp_ref[:, cols]
            xpi = xp_ref[:, colsi]
            lrk = lr1[:, kt * 1024:(kt + 1) * 1024]
            lik = li1[:, kt * 1024:(kt + 1) * 1024]
            x_scr[:, cols] = lrk * xpr - lik * xpi + bur
            x_scr[:, colsi] = lrk * xpi + lik * xpr + bui

        def chunk(j, c):
            cr, ci = c
            r0 = pl.multiple_of((nch - 1 - j) * 8, 8)
            gr = g_scr[pl.ds(r0, 8), pl.ds(0, S5_N)]
            gi = g_scr[pl.ds(r0, 8), pl.ds(S5_N, S5_N)]
            for lvl, (sh, _) in enumerate(SCAN_SHIFTS):
                sr = pltpu.roll(gr, 8 - sh, 0)
                si = pltpu.roll(gi, 8 - sh, 0)
                lr = tab_ref[lvl]
                li = tab_ref[3 + lvl]
                gr, gi = gr + lr * sr + li * si, gi + lr * si - li * sr
            pvr = tab_ref[6]
            pvi = tab_ref[7]
            gr, gi = gr + pvr * cr + pvi * ci, gi + pvr * ci - pvi * cr
            g_scr[pl.ds(r0, 8), pl.ds(0, S5_N)] = gr
            g_scr[pl.ds(r0, 8), pl.ds(S5_N, S5_N)] = gi
            xpr = xp_ref[pl.ds(r0, 8), pl.ds(0, S5_N)]
            xpi = xp_ref[pl.ds(r0, 8), pl.ds(S5_N, S5_N)]
            dl_ref[0] += gr * xpr + gi * xpi
            dl_ref[1] += gi * xpr - gr * xpi
            return gr[0:1, :], gi[0:1, :]

        cr, ci = lax.fori_loop(0, nch, chunk, (carry[0:1, :], carry[1:2, :]))
        carry[0:1, :] = cr
        carry[1:2, :] = ci

        for kt in range(S5_KT):
            du = jnp.zeros((tm, 256), F32)
            ukt = ub[:, kt * 256:(kt + 1) * 256]
            dykt = dyb[:, kt * 256:(kt + 1) * 256]
            for part in range(2):
                gb = g_scr[:, pl.ds(part * S5_N + kt * 1024, 1024)].astype(BF16)
                xb = x_scr[:, pl.ds(part * S5_N + kt * 1024, 1024)].astype(BF16)
                du = du + _dot(gb, bt_ref[part, kt])
                dbv = _dot_tn(ukt, gb)
                dcv = _dot_tn(xb, dykt)

                @pl.when(first)
                def _():
                    db_ref[part, kt] = dbv
                    dc_ref[part, kt] = dcv

                @pl.when(jnp.logical_not(first))
                def _():
                    db_ref[part, kt] += dbv
                    dc_ref[part, kt] += dcv
            du_ref[:, pl.ds(kt * 256, 256)] = du + d_ref[:, pl.ds(kt * 256, 256)] * dyv[:, kt * 256:(kt + 1) * 256]
        _accum(dd_ref, jnp.sum(dyv * uv, axis=0, keepdims=True), first)

    rev = lambda i: (nt - 1 - i, 0)
    return _carry(
        body, comm, name="s5_bwd", steps=nt,
        out_shape=(jax.ShapeDtypeStruct((t, S5_WIDTH), F32),
                   jax.ShapeDtypeStruct((2, S5_KT, 256, 1024), F32),
                   jax.ShapeDtypeStruct((2, S5_KT, 1024, 256), F32),
                   jax.ShapeDtypeStruct((2, 8, S5_N), F32),
                   jax.ShapeDtypeStruct((1, S5_WIDTH), F32)),
        in_specs=[pl.BlockSpec((tm, S5_WIDTH), rev), pl.BlockSpec((tm, S5_WIDTH), rev),
                  pl.BlockSpec((tm, 2 * S5_N), rev),
                  _const_spec((8, 8, S5_N)),
                  _const_spec((2, S5_KT, 256, 1024)), _const_spec((2, S5_KT, 1024, 256)),
                  _const_spec((2, S5_KT, 256, 1024)), _const_spec((1, S5_WIDTH))],
        out_specs=(pl.BlockSpec((tm, S5_WIDTH), rev),
                   pl.BlockSpec((2, S5_KT, 256, 1024), lambda i: (0, 0, 0, 0)),
                   pl.BlockSpec((2, S5_KT, 1024, 256), lambda i: (0, 0, 0, 0)),
                   pl.BlockSpec((2, 8, S5_N), lambda i: (0, 0, 0)),
                   pl.BlockSpec((1, S5_WIDTH), lambda i: (0, 0))),
        scratch_shapes=[pltpu.VMEM((tm, 2 * S5_N), F32), pltpu.VMEM((tm, 2 * S5_N), F32),
                        pltpu.VMEM((8, S5_N), F32)],
        args=(dy, u, xp, tab, bmat, bmat_t, cmat_t, dvec),
    )


def _hg_gates(z, lb):
    sg = _sigmoid(z)
    sgn = _sigmoid(-z)
    fg = lb + (1.0 - lb) * sg
    return sg, sgn, fg, jnp.log(fg), (1.0 - lb) * sgn


def _hg_decays(g, tril):
    gc = jnp.dot(tril, g, precision=HIGHEST, preferred_element_type=F32)
    mid = gc[CHUNK // 2 - 1:CHUNK // 2, :]
    last = gc[CHUNK - 1:CHUNK, :]
    return jnp.exp(gc), jnp.exp(gc - mid), jnp.exp(mid - gc), jnp.exp(last - gc), jnp.exp(last)


def _split_bf16(x):
    hi = x.astype(BF16)
    return hi, (x - hi.astype(F32)).astype(BF16)


def _hg_scores(qt, qlo, kt, klo, sl, causal):
    a = _dot_nt(qt[:, sl], kt[:, sl]) + _dot_nt(qt[:, sl], klo[:, sl]) + _dot_nt(qlo[:, sl], kt[:, sl])
    return jnp.where(causal, a, 0.0).astype(BF16)


def hgrn_fwd(q, f, v, lb):
    t = q.shape[0]
    nc = t // CHUNK
    scale = HG_E ** -0.5

    def body(q_ref, f_ref, v_ref, lb_ref, o_ref, st_ref, state):
        @pl.when(pl.program_id(0) == 0)
        def _():
            state[...] = jnp.zeros_like(state)

        ri = lax.broadcasted_iota(jnp.int32, (CHUNK, CHUNK), 0)
        ci = lax.broadcasted_iota(jnp.int32, (CHUNK, CHUNK), 1)
        causal = ri >= ci
        tril = causal.astype(F32)
        _, _, _, g, k = _hg_gates(f_ref[...], lb_ref[...])
        eg, eq, ek, ed, el = _hg_decays(g, tril)
        qs = q_ref[...] * scale
        qg = (qs * eg).astype(BF16)
        qt, qlo = _split_bf16(qs * eq)
        kt, klo = _split_bf16(k * ek)
        kd = (k * ed).astype(BF16)
        vb = v_ref[...].astype(BF16)
        for h in range(HG_HEADS):
            sl = slice(h * HG_E, (h + 1) * HG_E)
            st = state[h]
            a = _hg_scores(qt, qlo, kt, klo, sl, causal)
            o_ref[:, sl] = _dot(a, vb[:, sl]) + _dot_nt(qg[:, sl], st.astype(BF16))
            st_new = st * el[:, sl] + _dot_tn(vb[:, sl], kd[:, sl])
            state[h] = st_new
            st_ref[0, h] = st_new

    tok = pl.BlockSpec((CHUNK, HG_WIDTH), lambda i: (i, 0))
    return pl.pallas_call(
        body, name="hgrn_fwd", grid=(nc,),
        out_shape=(jax.ShapeDtypeStruct((t, HG_WIDTH), F32),
                   jax.ShapeDtypeStruct((nc, HG_HEADS, HG_E, HG_E), F32)),
        in_specs=[tok, tok, tok, _const_spec((1, HG_WIDTH))],
        out_specs=(tok, pl.BlockSpec((1, HG_HEADS, HG_E, HG_E), lambda i: (i, 0, 0, 0))),
        scratch_shapes=[pltpu.VMEM((HG_HEADS, HG_E, HG_E), F32)],
        compiler_params=_cparams(("arbitrary",)),
    )(q, f, v, lb)


def hgrn_bwd(do, q, f, v, lb, states, comm=None):
    t = q.shape[0]
    nc = t // CHUNK
    scale = HG_E ** -0.5

    def body(do_ref, q_ref, f_ref, v_ref, lb_ref, s0_ref, dq_ref, df_ref, dv_ref, dlb_ref, dstate):
        first = pl.program_id(0) == 0
        c_idx = nc - 1 - pl.program_id(0)

        @pl.when(first)
        def _():
            dstate[...] = jnp.zeros_like(dstate)

        ri = lax.broadcasted_iota(jnp.int32, (CHUNK, CHUNK), 0)
        ci = lax.broadcasted_iota(jnp.int32, (CHUNK, CHUNK), 1)
        causal = ri >= ci
        tril = causal.astype(F32)
        triu = (ri <= ci).astype(F32)
        lb = lb_ref[...]
        sg, sgn, fg, g, k = _hg_gates(f_ref[...], lb)
        eg, eq, ek, ed, el = _hg_decays(g, tril)
        qs = q_ref[...] * scale
        qg = (qs * eg).astype(BF16)
        qt, qlo = _split_bf16(qs * eq)
        kt, klo = _split_bf16(k * ek)
        kd = (k * ed).astype(BF16)
        vb = v_ref[...].astype(BF16)
        dob = do_ref[...].astype(BF16)
        has_prev = jnp.where(c_idx > 0, 1.0, 0.0)
        dqs_l, dk_l, dgc_l, dgl_l = [], [], [], []
        for h in range(HG_HEADS):
            sl = slice(h * HG_E, (h + 1) * HG_E)
            s0 = s0_ref[0, h] * has_prev
            ds1 = dstate[h]
            ds1b = ds1.astype(BF16)
            a = _hg_scores(qt, qlo, kt, klo, sl, causal)
            da = jnp.where(causal, _dot_nt(dob[:, sl], vb[:, sl]), 0.0).astype(BF16)
            dv_ref[:, sl] = _dot_tn(a, dob[:, sl]) + _dot_nt(kd[:, sl], ds1b)
            dkd = _dot(vb[:, sl], ds1b)
            dqt = _dot(da, kt[:, sl])
            dkt = _dot_tn(da, qt[:, sl])
            dqg = _dot(dob[:, sl], s0.astype(BF16))
            dqs_l.append(dqt * eq[:, sl] + dqg * eg[:, sl])
            dk_l.append(dkt * ek[:, sl] + dkd * ed[:, sl])
            kd_dkd = kd[:, sl].astype(F32) * dkd
            dgc_l.append(qt[:, sl].astype(F32) * dqt - kt[:, sl].astype(F32) * dkt
                         + qg[:, sl].astype(F32) * dqg - kd_dkd)
            dgl_l.append(el[:, sl] * jnp.sum(ds1 * s0, axis=0, keepdims=True)
                         + jnp.sum(kd_dkd, axis=0, keepdims=True))
            dstate[h] = ds1 * el[:, sl] + _dot_tn(dob[:, sl], qg[:, sl])
        dqs = jnp.concatenate(dqs_l, axis=1)
        dk = jnp.concatenate(dk_l, axis=1)
        dgl = jnp.concatenate(dgl_l, axis=1)
        dq_ref[...] = dqs * scale
        rowc = lax.broadcasted_iota(jnp.int32, (CHUNK, HG_WIDTH), 0)
        dgc = jnp.concatenate(dgc_l, axis=1) + jnp.where(rowc == CHUNK - 1, dgl, 0.0)
        dg = jnp.dot(triu, dgc, precision=HIGHEST, preferred_element_type=F32)
        w = dg / fg - dk
        df_ref[...] = w * (1.0 - lb) * sg * sgn
        _accum(dlb_ref, jnp.sum(w * sgn, axis=0, keepdims=True), first)

    rev = lambda i: (nc - 1 - i, 0)
    tok = pl.BlockSpec((CHUNK, HG_WIDTH), rev)
    return _carry(
        body, comm, name="hgrn_bwd", steps=nc,
        out_shape=(jax.ShapeDtypeStruct((t, HG_WIDTH), F32),
                   jax.ShapeDtypeStruct((t, HG_WIDTH), F32),
                   jax.ShapeDtypeStruct((t, HG_WIDTH), F32),
                   jax.ShapeDtypeStruct((1, HG_WIDTH), F32)),
        in_specs=[tok, tok, tok, tok, _const_spec((1, HG_WIDTH)),
                  pl.BlockSpec((1, HG_HEADS, HG_E, HG_E), lambda i: (jnp.maximum(nc - 2 - i, 0), 0, 0, 0))],
        out_specs=(tok, tok, tok, pl.BlockSpec((1, HG_WIDTH), lambda i: (0, 0))),
        scratch_shapes=[pltpu.VMEM((HG_HEADS, HG_E, HG_E), F32)],
        args=(do, q, f, v, lb, states),
    )


GELU_C = math.sqrt(2.0 / math.pi)


def _gelu(x):
    th = jnp.tanh(GELU_C * (x + 0.044715 * x * x * x))
    return 0.5 * x * (1.0 + th), th


def _merge_core(ys5, o, og, ga, gb, wv_ref, wt_ref, ghg, who_ref):
    ys, th = _gelu(ys5)
    ysb = ys.astype(BF16)
    va = jnp.concatenate([_dot(ysb, wv_ref[s]) for s in range(N_SHARD)], axis=1)
    vt = jnp.concatenate([_dot(ysb, wt_ref[s]) for s in range(N_SHARD)], axis=1)
    svt = _sigmoid(vt)
    ya = va * svt
    rs, ons = [], []
    for h in range(HG_HEADS):
        oh = o[:, h * HG_E:(h + 1) * HG_E]
        r = lax.rsqrt(jnp.mean(oh * oh, axis=-1, keepdims=True) + NORM_EPS)
        rs.append(r)
        ons.append(oh * r)
    on = jnp.concatenate(ons, axis=1)
    sgo = _sigmoid(og)
    o2 = on * ghg * (og * sgo)
    o2b = o2.astype(BF16)
    yb = _dot(o2b, who_ref[...])
    sa = _sigmoid(ga)
    sb = _sigmoid(gb)
    mixed = sa * ya + sb * yb
    return dict(ys=ys, th=th, ysb=ysb, va=va, svt=svt, ya=ya, rs=rs, on=on, sgo=sgo, o2b=o2b, yb=yb,
                sa=sa, sb=sb, mixed=mixed)


def merge_fwd(h, ys5, o, og, ga, gb, wv, wt, ghg, who, wmo, tm=256):
    t = h.shape[0]

    def body(h_ref, ys5_ref, o_ref, og_ref, ga_ref, gb_ref, wv_ref, wt_ref, ghg_ref, who_ref, wmo_ref, out_ref):
        c = _merge_core(ys5_ref[...], o_ref[...], og_ref[...], ga_ref[...], gb_ref[...],
                        wv_ref, wt_ref, ghg_ref[...], who_ref)
        out_ref[...] = h_ref[...] + _dot(c["mixed"].astype(BF16), wmo_ref[...])

    tok = pl.BlockSpec((tm, D_MODEL), lambda i: (i, 0))
    return pl.pallas_call(
        body, name="merge_fwd", grid=(t // tm,),
        out_shape=jax.ShapeDtypeStruct((t, D_MODEL), F32),
        in_specs=[tok, pl.BlockSpec((tm, S5_WIDTH), lambda i: (i, 0)), tok, tok, tok, tok,
                  _const_spec((N_SHARD, S5_WIDTH, 256)), _const_spec((N_SHARD, S5_WIDTH, 256)),
                  _const_spec((1, HG_WIDTH)), _const_spec((HG_WIDTH, D_MODEL)), _const_spec((D_MODEL, D_MODEL))],
        out_specs=tok,
        compiler_params=_cparams(("arbitrary",)),
    )(h, ys5, o, og, ga, gb, wv, wt, ghg, who, wmo)


def merge_bwd(dh, ys5, o, og, ga, gb, wv, wt, ghg, who, wmo, comm=None, tm=256):
    t = dh.shape[0]

    def body(dh_ref, ys5_ref, o_ref, og_ref, ga_ref, gb_ref, wv_ref, wt_ref, ghg_ref, who_ref, wmo_ref,
             dys5_ref, do_ref, dog_ref, dga_ref, dgb_ref, dghg_ref,
             mixb_ref, dhb_ref, ysb_ref, dvab_ref, dvtb_ref, o2b_ref, dybb_ref):
        ys5 = ys5_ref[...]
        o = o_ref[...]
        og = og_ref[...]
        ghg = ghg_ref[...]
        c = _merge_core(ys5, o, og, ga_ref[...], gb_ref[...], wv_ref, wt_ref, ghg, who_ref)
        dhb = dh_ref[...].astype(BF16)
        dhb_ref[...] = dhb
        mixb_ref[...] = c["mixed"].astype(BF16)
        ysb_ref[...] = c["ysb"]
        o2b_ref[...] = c["o2b"]
        dmix = _dot_nt(dhb, wmo_ref[...])
        sa, sb = c["sa"], c["sb"]
        dya = dmix * sa
        dyb = dmix * sb
        dga_ref[...] = dmix * c["ya"] * sa * (1.0 - sa)
        dgb_ref[...] = dmix * c["yb"] * sb * (1.0 - sb)
        svt = c["svt"]
        dva = (dya * svt).astype(BF16)
        dvt = (dya * c["va"] * svt * (1.0 - svt)).astype(BF16)
        dvab_ref[...] = dva
        dvtb_ref[...] = dvt
        dys = jnp.zeros((tm, S5_WIDTH), F32)
        for s in range(N_SHARD):
            dys = dys + _dot_nt(dva[:, s * 256:(s + 1) * 256], wv_ref[s]) + _dot_nt(dvt[:, s * 256:(s + 1) * 256], wt_ref[s])
        th = c["th"]
        dgelu = 0.5 * (1.0 + th) + 0.5 * ys5 * (1.0 - th * th) * GELU_C * (1.0 + 3.0 * 0.044715 * ys5 * ys5)
        dys5_ref[...] = dys * dgelu
        dybb = dyb.astype(BF16)
        dybb_ref[...] = dybb
        do2 = _dot_nt(dybb, who_ref[...])
        sgo = c["sgo"]
        sil = og * sgo
        on = c["on"]
        dog_ref[...] = do2 * on * ghg * (sgo * (1.0 + og * (1.0 - sgo)))
        _accum(dghg_ref, jnp.sum(do2 * on * sil, axis=0, keepdims=True), pl.program_id(0) == 0)
        don = do2 * ghg * sil
        dos = []
        for h in range(HG_HEADS):
            sl = slice(h * HG_E, (h + 1) * HG_E)
            m = jnp.mean(don[:, sl] * on[:, sl], axis=-1, keepdims=True)
            dos.append(c["rs"][h] * (don[:, sl] - on[:, sl] * m))
        do_ref[...] = jnp.concatenate(dos, axis=1)

    tok = pl.BlockSpec((tm, D_MODEL), lambda i: (i, 0))
    s5b = pl.BlockSpec((tm, S5_WIDTH), lambda i: (i, 0))
    f32t = jax.ShapeDtypeStruct((t, D_MODEL), F32)
    bft = jax.ShapeDtypeStruct((t, D_MODEL), BF16)
    return _carry(
        body, comm, name="merge_bwd", steps=t // tm,
        out_shape=(jax.ShapeDtypeStruct((t, S5_WIDTH), F32), f32t, f32t, f32t, f32t,
                   jax.ShapeDtypeStruct((1, HG_WIDTH), F32),
                   bft, bft, jax.ShapeDtypeStruct((t, S5_WIDTH), BF16), bft, bft, bft, bft),
        in_specs=[tok, s5b, tok, tok, tok, tok,
                  _const_spec((N_SHARD, S5_WIDTH, 256)), _const_spec((N_SHARD, S5_WIDTH, 256)),
                  _const_spec((1, HG_WIDTH)), _const_spec((HG_WIDTH, D_MODEL)), _const_spec((D_MODEL, D_MODEL))],
        out_specs=(s5b, tok, tok, tok, tok, pl.BlockSpec((1, HG_WIDTH), lambda i: (0, 0)),
                   tok, tok, s5b, tok, tok, tok, tok),
        args=(dh, ys5, o, og, ga, gb, wv, wt, ghg, who, wmo),
    )


def head_fwd_bwd(h, p, tgt, gple, wpg, wpp, gfin, tm=256):
    t = h.shape[0]

    def body(h_ref, p_ref, tgt_ref, gple_ref, wpg_ref, wpp_ref, gfin_ref,
             loss_ref, dh_ref, dgple_ref, dgfin_ref, nb_ref, dlb_ref, dppb_ref):
        first = pl.program_id(0) == 0
        hv = h_ref[...]
        gple = gple_ref[...]
        gfin = gfin_ref[...]
        n, r3 = _rms_fwd(hv, gple)
        nb = n.astype(BF16)
        nb_ref[...] = nb
        pg = _sigmoid(_dot(nb, wpg_ref[...]))
        pb = p_ref[...].astype(BF16)
        pp = jnp.concatenate([_dot(pb, wpp_ref[s]) for s in range(N_SHARD)], axis=1)
        h4 = hv + pg * pp
        y, r4 = _rms_fwd(h4, gfin)
        err = y - tgt_ref[...]
        lsum = 0.5 * jnp.sum(jnp.sum(err * err, axis=-1, keepdims=True), axis=0, keepdims=True) / D_MODEL
        _accum(loss_ref, jnp.broadcast_to(lsum, (8, 128)), first)
        dy = err * (1.0 / D_MODEL)
        dh4, dgf = _rms_bwd(h4, r4, gfin, dy)
        _accum(dgfin_ref, dgf, first)
        dpp = dh4 * pg
        dppb_ref[...] = dpp.astype(BF16)
        dl = (dh4 * pp * pg * (1.0 - pg)).astype(BF16)
        dlb_ref[...] = dl
        dn = _dot_nt(dl, wpg_ref[...])
        dx, dgp = _rms_bwd(hv, r3, gple, dn)
        _accum(dgple_ref, dgp, first)
        dh_ref[...] = dh4 + dx

    tok = pl.BlockSpec((tm, D_MODEL), lambda i: (i, 0))
    vec = pl.BlockSpec((1, D_MODEL), lambda i: (0, 0))
    bft = jax.ShapeDtypeStruct((t, D_MODEL), BF16)
    return pl.pallas_call(
        body, name="head_fwd_bwd", grid=(t // tm,),
        out_shape=(jax.ShapeDtypeStruct((8, 128), F32), jax.ShapeDtypeStruct((t, D_MODEL), F32),
                   jax.ShapeDtypeStruct((1, D_MODEL), F32), jax.ShapeDtypeStruct((1, D_MODEL), F32),
                   bft, bft, bft),
        in_specs=[tok, pl.BlockSpec((tm, PLE_DIM), lambda i: (i, 0)), tok,
                  _const_spec((1, D_MODEL)), _const_spec((D_MODEL, D_MODEL)),
                  _const_spec((N_SHARD, PLE_DIM, 256)), _const_spec((1, D_MODEL))],
        out_specs=(pl.BlockSpec((8, 128), lambda i: (0, 0)), tok, vec, vec, tok, tok, tok),
        compiler_params=_cparams(("arbitrary",)),
    )(h, p, tgt, gple, wpg, wpp, gfin)


BIG = ("ffn1_w_gate", "ffn1_w_up", "ffn1_w_down", "w_in", "s5_glu_val", "s5_glu_gate", "hg_w_out",
       "w_merge_out", "ffn2_w_gate", "ffn2_w_up", "ffn2_w_down", "ple_w_gate", "ple_w_proj")
FFN_T = ("ffn1_w_gate", "ffn1_w_up", "ffn2_w_gate", "ffn2_w_up")
BIG_SHARD = {
    "ffn1_w_gate": (FF_PAD, D_MODEL), "ffn1_w_up": (FF_PAD, D_MODEL), "ffn1_w_down": (FF_PAD, D_MODEL),
    "ffn2_w_gate": (FF_PAD, D_MODEL), "ffn2_w_up": (FF_PAD, D_MODEL), "ffn2_w_down": (FF_PAD, D_MODEL),
    "w_in": (D_MODEL, IN_COLS // N_SHARD), "s5_glu_val": (S5_WIDTH, 256), "s5_glu_gate": (S5_WIDTH, 256),
    "hg_w_out": (256, D_MODEL), "w_merge_out": (256, D_MODEL), "ple_w_gate": (256, D_MODEL),
    "ple_w_proj": (PLE_DIM, 256),
}


def _lower_bound(hb):
    return jax.nn.softmax(hb, axis=0)[0:1]


class Schedule:
    def __init__(self, wts):
        self.wts = dict(wts)
        self.grads = {}

    def before(self, kernel_name):
        return None

    def after(self, kernel_name, results):
        pass

    def grad(self, name, g):
        self.grads[name] = g


def local_step(x, p, tgt, sched, sm):
    wts = sched.wts
    rows_full = lambda w: w.reshape(N_SHARD * w.shape[1], w.shape[2])

    def carried(kernel_name, fn, *args):
        outs, results = fn(*args, comm=sched.before(kernel_name))
        sched.after(kernel_name, results)
        return outs

    lb, lb_vjp = jax.vjp(_lower_bound, sm["hg_lower_bound"])
    s5_names = ("s5_lam_re", "s5_lam_im", "s5_log_dt", "s5_b_re", "s5_b_im", "s5_c_re", "s5_c_im")
    (lam_bar, bmat, cmat), s5_vjp = jax.vjp(s5_prep, *[sm[k] for k in s5_names])
    pw_r, pw_i = _lam_powers(lam_bar)
    bmat_b = bmat.astype(BF16)
    cmat_b = cmat.astype(BF16)
    bmat_t = jnp.swapaxes(bmat, -1, -2).astype(BF16)
    cmat_t = jnp.swapaxes(cmat, -1, -2).astype(BF16)

    h1, a1, b1 = carried("ffn1_fwd", ffn_fwd, x, sm["ffn1_norm"], wts["ffn1_w_gate"], wts["ffn1_w_up"],
                         wts["ffn1_w_down"], "ffn1_fwd")
    s5in, q, f, v, og, ga, gb = inproj_fwd(h1, sm["mix_norm"], wts["w_in"])
    ys5, xp = carried("s5_fwd", s5_fwd, s5in, _scan_tables(pw_r, pw_i, False), bmat_b, cmat_b, sm["s5_d"])
    o, states = hgrn_fwd(q, f, v, lb)
    who = rows_full(wts["hg_w_out"])
    wmo = rows_full(wts["w_merge_out"])
    h2 = merge_fwd(h1, ys5, o, og, ga, gb, wts["s5_glu_val"], wts["s5_glu_gate"], sm["hg_out_norm"], who, wmo)
    (h3, a2, b2), _ = ffn_fwd(h2, sm["ffn2_norm"], wts["ffn2_w_gate"], wts["ffn2_w_up"], wts["ffn2_w_down"], "ffn2_fwd")
    loss, dh3, d_ple_norm, d_final_norm, npb, dlgb, dppb = head_fwd_bwd(
        h3, p, tgt, sm["ple_norm"], rows_full(wts["ple_w_gate"]), wts["ple_w_proj"], sm["final_norm"])

    gs = {"ple_norm": d_ple_norm, "final_norm": d_final_norm}
    sched.grad("ple_w_gate", tn_matmul(npb, dlgb, "g_ple_w_gate", "rows"))
    sched.grad("ple_w_proj", tn_matmul(p, dppb, "g_ple_w_proj", "cols"))

    (dh2, gs["ffn2_norm"], n2b, dhb2, da2, db2, s2), _ = ffn_bwd(
        dh3, h2, a2, b2, sm["ffn2_norm"], wts["ffn2_w_gate"], wts["ffn2_w_up"], wts["ffn2_w_down"], "ffn2_bwd")
    sched.grad("ffn2_w_gate", tn_matmul(da2, n2b, "g_ffn2_w_gate", "rows"))
    sched.grad("ffn2_w_up", tn_matmul(db2, n2b, "g_ffn2_w_up", "rows"))
    sched.grad("ffn2_w_down", tn_matmul(s2, dhb2, "g_ffn2_w_down", "rows"))

    dys5, do, dog, dga, dgb, gs["hg_out_norm"], mixb, dh2b, ysb, dvab, dvtb, o2b, dybb = carried(
        "merge_bwd", merge_bwd,
        dh2, ys5, o, og, ga, gb, wts["s5_glu_val"], wts["s5_glu_gate"], sm["hg_out_norm"], who, wmo)
    sched.grad("w_merge_out", tn_matmul(mixb, dh2b, "g_w_merge_out", "rows"))
    sched.grad("s5_glu_val", tn_matmul(ysb, dvab, "g_s5_glu_val", "cols"))
    sched.grad("s5_glu_gate", tn_matmul(ysb, dvtb, "g_s5_glu_gate", "cols"))
    sched.grad("hg_w_out", tn_matmul(o2b, dybb, "g_hg_w_out", "rows"))

    dq, df, dv, dlb = carried("hgrn_bwd", hgrn_bwd, do, q, f, v, lb, states)
    (gs["hg_lower_bound"],) = lb_vjp(dlb)
    du, dbmat, dcmat, dlam8, gs["s5_d"] = carried(
        "s5_bwd", s5_bwd,
        dys5, s5in, xp, _scan_tables(pw_r, pw_i, True), bmat_b, bmat_t, cmat_t, sm["s5_d"])
    for k, g in zip(s5_names, s5_vjp((jnp.sum(dlam8, axis=1), dbmat, dcmat))):
        gs[k] = g

    dh1, gs["mix_norm"], nmb, dprojb = carried(
        "inproj_bwd", inproj_bwd, dh2, h1, sm["mix_norm"], wts["w_in"], (du, dq, df, dv, dog, dga, dgb))
    sched.grad("w_in", tn_matmul(nmb, dprojb, "g_w_in", "cols"))

    dx, gs["ffn1_norm"], n1b, dhb1, da1, db1, s1 = carried(
        "ffn1_bwd", ffn_bwd,
        dh1, x, a1, b1, sm["ffn1_norm"], wts["ffn1_w_gate"], wts["ffn1_w_up"], wts["ffn1_w_down"], "ffn1_bwd")
    sched.grad("ffn1_w_gate", tn_matmul(da1, n1b, "g_ffn1_w_gate", "rows"))
    sched.grad("ffn1_w_up", tn_matmul(db1, n1b, "g_ffn1_w_up", "rows"))
    sched.grad("ffn1_w_down", tn_matmul(s1, dhb1, "g_ffn1_w_down", "rows"))
    return loss, dx, gs


MESH = pl.DeviceIdType.MESH
ANY = pl.BlockSpec(memory_space=pl.ANY)


def _place():
    x, y, c = lax.axis_index("x"), lax.axis_index("y"), lax.axis_index("c")
    return x, y, c


def _remote(src, dst, ssem, rsem, dev):
    return pltpu.make_async_remote_copy(src_ref=src, dst_ref=dst, send_sem=ssem, recv_sem=rsem,
                                        device_id=dev, device_id_type=MESH)


class Comm:
    def __init__(self, bufs, outs, alias, sems, hooks):
        self.bufs, self.outs, self.alias, self.sems, self.hooks = list(bufs), list(outs), alias, list(sems), hooks


def run_comm(comm, name):
    nb, no = len(comm.bufs), len(comm.outs)

    def body(*refs):
        for which in ("first", "mid", "last"):
            if which in comm.hooks:
                comm.hooks[which](refs[:nb], refs[nb:nb + no], refs[nb + no:])

    return pl.pallas_call(
        body, name=name, out_shape=tuple(comm.outs), in_specs=[ANY] * nb, out_specs=tuple([ANY] * no),
        input_output_aliases=dict(comm.alias), scratch_shapes=comm.sems,
    )(*comm.bufs)


def _carry(body, comm, *, name, steps, out_shape, in_specs, out_specs, args, scratch_shapes=()):
    out_shape, out_specs, scratch_shapes = tuple(out_shape), tuple(out_specs), list(scratch_shapes)
    if comm is None:
        res = pl.pallas_call(body, name=name, grid=(steps,), out_shape=out_shape, in_specs=list(in_specs),
                             out_specs=out_specs, scratch_shapes=scratch_shapes,
                             compiler_params=_cparams(("arbitrary",)))(*args)
        return tuple(res), ()
    n_in, n_out, n_scr = len(args), len(out_shape), len(scratch_shapes)
    nb, no = len(comm.bufs), len(comm.outs)

    def wrapped(*refs):
        ins, cb = refs[:n_in], refs[n_in:n_in + nb]
        o0 = n_in + nb
        outs, co = refs[o0:o0 + n_out], refs[o0 + n_out:o0 + n_out + no]
        s0 = o0 + n_out + no
        scr, cs = refs[s0:s0 + n_scr], refs[s0 + n_scr:]
        step = pl.program_id(0)

        def hook(which, at):
            if which in comm.hooks:
                pl.when(step == at)(lambda: comm.hooks[which](cb, co, cs))

        hook("first", 0)
        hook("mid", steps // 2)
        body(*ins, *outs, *scr)
        hook("last", steps - 1)

    res = pl.pallas_call(
        wrapped, name=name, grid=(steps,), out_shape=out_shape + tuple(comm.outs),
        in_specs=list(in_specs) + [ANY] * nb, out_specs=out_specs + (ANY,) * no,
        scratch_shapes=scratch_shapes + comm.sems,
        input_output_aliases={n_in + i: n_out + o for i, o in comm.alias.items()},
        compiler_params=_cparams(("arbitrary",)),
    )(*args, *comm.bufs)
    return tuple(res[:n_out]), tuple(res[n_out:])


def gather_comm(bufs):
    n = len(bufs)

    def copies(outs, sems):
        s_own, r_own, s_fwd, r_fwd, s_sib, r_sib = sems
        x, y, c = _place()
        me = 2 * x + y
        nbr = ((1 - x, y), (x, 1 - y))
        nbr_id = (2 * (1 - x) + y, 2 * x + (1 - y))
        diag_id = 2 * (1 - x) + (1 - y)
        sib = (x, y, 1 - c)

        def rows(w, q=None):
            r = outs[w].shape[1]
            if q is None:
                return pl.ds(pl.multiple_of(c * (r // 2), 16), r // 2)
            return pl.ds(pl.multiple_of(c * (r // 2) + q * (r // 4), 16), r // 4)

        def own(w, j):
            piece = outs[w].at[me, rows(w)]
            return _remote(piece, piece, s_own.at[w, j], r_own.at[w, j], (nbr[j][0], nbr[j][1], c))

        def from_nbr(w, j):
            piece = outs[w].at[nbr_id[j], rows(w)]
            return _remote(piece, piece, s_own.at[w, j], r_own.at[w, j], (nbr[j][0], nbr[j][1], c))

        def fwd(w, j):
            piece = outs[w].at[nbr_id[j], rows(w, j)]
            return _remote(piece, piece, s_fwd.at[w, j], r_fwd.at[w, j], (nbr[1 - j][0], nbr[1 - j][1], c))

        def from_diag(w, j):
            piece = outs[w].at[diag_id, rows(w, j)]
            return _remote(piece, piece, s_fwd.at[w, j], r_fwd.at[w, j], (nbr[1 - j][0], nbr[1 - j][1], c))

        def to_sib(w, k):
            piece = (outs[w].at[nbr_id[k], rows(w)] if k < 2 else outs[w].at[diag_id, rows(w, k - 2)])
            return _remote(piece, piece, s_sib.at[w, k], r_sib.at[w, k], sib)

        def from_sib(w, k):
            r = outs[w].shape[1]
            if k < 2:
                piece = outs[w].at[nbr_id[k], pl.ds(pl.multiple_of((1 - c) * (r // 2), 16), r // 2)]
            else:
                piece = outs[w].at[diag_id, pl.ds(pl.multiple_of((1 - c) * (r // 2) + (k - 2) * (r // 4), 16), r // 4)]
            return _remote(piece, piece, s_sib.at[w, k], r_sib.at[w, k], sib)

        return own, from_nbr, fwd, from_diag, to_sib, from_sib

    def first(_, outs, sems):
        own = copies(outs, sems)[0]
        for w in range(n):
            own(w, 0).start()
            own(w, 1).start()

    def mid(_, outs, sems):
        _, from_nbr, fwd, _, to_sib, _ = copies(outs, sems)
        for w in range(n):
            for j in range(2):
                from_nbr(w, j).wait_recv()
                fwd(w, j).start()
                to_sib(w, j).start()

    def last(_, outs, sems):
        own, _, fwd, from_diag, to_sib, from_sib = copies(outs, sems)
        for w in range(n):
            for j in range(2):
                from_diag(w, j).wait_recv()
                to_sib(w, 2 + j).start()
        for w in range(n):
            for k in range(4):
                from_sib(w, k).wait_recv()
        for w in range(n):
            for j in range(2):
                own(w, j).wait_send()
                fwd(w, j).wait_send()
            for k in range(4):
                to_sib(w, k).wait_send()

    dma = pltpu.SemaphoreType.DMA
    return Comm(bufs, [jax.ShapeDtypeStruct(b.shape, b.dtype) for b in bufs], {w: w for w in range(n)},
                [dma((n, 2)), dma((n, 2)), dma((n, 2)), dma((n, 2)), dma((n, 4)), dma((n, 4))],
                {"first": first, "mid": mid, "last": last})


def _start_wait(make):
    def first(bufs, outs, sems):
        for cp in make(bufs, outs, sems):
            cp.start()

    def last(bufs, outs, sems):
        for cp in make(bufs, outs, sems):
            cp.wait()

    return {"first": first, "last": last}


def exchange_comm(grads):
    n = len(grads)

    def make(ins, outs, sems):
        x, y, c = _place()
        cps = []
        for w in range(n):
            half = ins[w].shape[1] // 2
            src = ins[w].at[:, pl.ds(pl.multiple_of((1 - c) * half, 8), half), :]
            cps.append(_remote(src, outs[w], sems[0].at[w], sems[1].at[w], (x, y, 1 - c)))
        return cps

    dma = pltpu.SemaphoreType.DMA
    return Comm(grads, [jax.ShapeDtypeStruct((N_SHARD, g.shape[1] // 2, g.shape[2]), g.dtype) for g in grads],
                {}, [dma((n,)), dma((n,))], _start_wait(make))


def scatter_comm(sums):
    n = len(sums)

    def make(ins, outs, sems):
        x, y, c = _place()
        chips = ((1 - x, y), (x, 1 - y), (1 - x, 1 - y))
        return [_remote(ins[w].at[2 * ch[0] + ch[1]], outs[w].at[j], sems[0].at[w, j], sems[1].at[w, j],
                        (ch[0], ch[1], c))
                for w in range(n) for j, ch in enumerate(chips)]

    dma = pltpu.SemaphoreType.DMA
    return Comm(sums, [jax.ShapeDtypeStruct((3,) + s.shape[1:], s.dtype) for s in sums],
                {}, [dma((n, 3)), dma((n, 3))], _start_wait(make))


def join_comm(shards):
    n = len(shards)

    def make(_, outs, sems):
        x, y, c = _place()
        cps = []
        for w in range(n):
            half = outs[w].shape[0] // 2
            mine = outs[w].at[pl.ds(pl.multiple_of(c * half, 8), half), :]
            cps.append(_remote(mine, mine, sems[0].at[w], sems[1].at[w], (x, y, 1 - c)))
        return cps

    dma = pltpu.SemaphoreType.DMA
    return Comm(shards, [jax.ShapeDtypeStruct(s.shape, s.dtype) for s in shards], {w: w for w in range(n)},
                [dma((n,)), dma((n,))], _start_wait(make))


def allreduce_small(vec):
    half = vec.shape[0] // 2

    def body(v_ref, o_ref, pair, chips_buf, s1, r1, s2, r2, s3, r3):
        x, y, c = _place()
        chip = 2 * x + y
        sib = (x, y, 1 - c)
        mine = pl.ds(pl.multiple_of(c * half, 8), half)
        other = pl.ds(pl.multiple_of((1 - c) * half, 8), half)
        to_sib = _remote(v_ref.at[other], pair, s1, r1, sib)
        to_sib.start()
        to_sib.wait()
        chips_buf[chip] = v_ref[mine, :] + pair[...]
        sends = [_remote(chips_buf.at[chip], chips_buf.at[chip], s2.at[j], r2.at[j], (ch[0], ch[1], c))
                 for j, ch in enumerate(((1 - x, y), (x, 1 - y), (1 - x, 1 - y)))]
        for cp in sends:
            cp.start()
        for cp in sends:
            cp.wait()
        o_ref[mine, :] = (chips_buf[0] + chips_buf[1]) + (chips_buf[2] + chips_buf[3])
        back = _remote(o_ref.at[mine], o_ref.at[mine], s3, r3, sib)
        back.start()
        back.wait()

    dma = pltpu.SemaphoreType.DMA
    return pl.pallas_call(
        body, name="allreduce_small",
        out_shape=jax.ShapeDtypeStruct(vec.shape, F32),
        in_specs=[pl.BlockSpec(memory_space=pltpu.VMEM)],
        out_specs=pl.BlockSpec(memory_space=pltpu.VMEM),
        scratch_shapes=[pltpu.VMEM((half, 128), F32), pltpu.VMEM((N_SHARD, half, 128), F32),
                        dma, dma, dma((3,)), dma((3,)), dma, dma],
        compiler_params=pltpu.CompilerParams(vmem_limit_bytes=VMEM_LIMIT),
    )(vec)


ROW_TILE = 128


def add_own_half(core, g, recv, name):
    _, r, cc = g.shape
    half = r // 2
    nb = half // ROW_TILE

    def body(c_ref, g_ref, r_ref, o_ref, ob_ref):
        del c_ref
        s = g_ref[...] + r_ref[...]
        o_ref[...] = s
        ob_ref[...] = s.astype(BF16)

    blk = (None, ROW_TILE, cc)
    return pl.pallas_call(
        body, name=name,
        grid_spec=pltpu.PrefetchScalarGridSpec(
            num_scalar_prefetch=1, grid=(N_SHARD, nb),
            in_specs=[pl.BlockSpec(blk, lambda s, i, c_ref: (s, c_ref[0] * nb + i, 0)),
                      pl.BlockSpec(blk, lambda s, i, c_ref: (s, i, 0))],
            out_specs=(pl.BlockSpec(blk, lambda s, i, c_ref: (s, i, 0)),
                       pl.BlockSpec(blk, lambda s, i, c_ref: (s, i, 0)))),
        out_shape=(jax.ShapeDtypeStruct((N_SHARD, half, cc), F32),
                   jax.ShapeDtypeStruct((N_SHARD, half, cc), BF16)),
        compiler_params=_cparams(("arbitrary", "arbitrary")),
    )(core, g, recv)


def add_chip_sums(place, own, recv, name):
    _, half, cc = own.shape
    nb = half // ROW_TILE

    def body(s_ref, o_ref, r_ref, out_ref):
        del s_ref
        acc = o_ref[...] + r_ref[0].astype(F32)
        acc = acc + r_ref[1].astype(F32)
        out_ref[...] = acc + r_ref[2].astype(F32)

    return pl.pallas_call(
        body, name=name,
        grid_spec=pltpu.PrefetchScalarGridSpec(
            num_scalar_prefetch=1, grid=(nb,),
            in_specs=[pl.BlockSpec((None, ROW_TILE, cc), lambda i, s_ref: (s_ref[0], i, 0)),
                      pl.BlockSpec((3, ROW_TILE, cc), lambda i, s_ref: (0, i, 0))],
            out_specs=pl.BlockSpec((ROW_TILE, cc), lambda i, s_ref: (s_ref[1] * nb + i, 0))),
        out_shape=jax.ShapeDtypeStruct((2 * half, cc), F32),
        compiler_params=_cparams(("arbitrary",)),
    )(place, own, recv)


def adamw(w, m, v, g, name, copy_g=False):
    r, cc = w.shape
    tr = next(t for t in (256, 352, r) if r % t == 0)
    bc1 = 1.0 / (1.0 - ADAM_B1 ** ADAM_STEP)
    bc2 = 1.0 / (1.0 - ADAM_B2 ** ADAM_STEP)

    def body(w_ref, m_ref, v_ref, g_ref, d_ref, mo_ref, vo_ref, *go_ref):
        gv = g_ref[...]
        mn = ADAM_B1 * m_ref[...] + (1.0 - ADAM_B1) * gv
        vn = ADAM_B2 * v_ref[...] + (1.0 - ADAM_B2) * (gv * gv)
        mo_ref[...] = mn
        vo_ref[...] = vn
        d_ref[...] = -ADAM_LR * ((mn * bc1) / (jnp.sqrt(vn * bc2) + ADAM_EPS) + ADAM_WD * w_ref[...])
        if copy_g:
            go_ref[0][...] = gv

    blk = pl.BlockSpec((tr, cc), lambda i: (i, 0))
    shp = jax.ShapeDtypeStruct((r, cc), F32)
    nout = 4 if copy_g else 3
    return pl.pallas_call(
        body, name=name, grid=(r // tr,),
        out_shape=(shp,) * nout, in_specs=[blk] * 4, out_specs=(blk,) * nout,
        compiler_params=_cparams(("arbitrary",)),
    )(w, m, v, g)


GATHER_FIRST = ("ffn1_w_gate", "ffn1_w_up", "ffn1_w_down")
GATHER_ON = {"ffn1_fwd": ("w_in", "s5_glu_val", "s5_glu_gate", "hg_w_out", "w_merge_out"),
             "s5_fwd": ("ffn2_w_gate", "ffn2_w_up", "ffn2_w_down", "ple_w_gate", "ple_w_proj")}
REDUCE = ((("ple_w_gate", "ple_w_proj", "ffn2_w_gate", "ffn2_w_up", "ffn2_w_down"), "merge_bwd", "hgrn_bwd"),
          (("w_merge_out", "s5_glu_val", "s5_glu_gate", "hg_w_out"), "s5_bwd", "inproj_bwd"),
          (("w_in",), None, "ffn1_bwd"),
          (("ffn1_w_gate", "ffn1_w_up", "ffn1_w_down"), None, None))


class DistSchedule(Schedule):
    def __init__(self, bufs, chip, core):
        first = run_comm(gather_comm([bufs[k] for k in GATHER_FIRST]), "gather_ffn1")
        super().__init__(zip(GATHER_FIRST, first))
        self.bufs = bufs
        self.core = core.reshape(1)
        self.place = jnp.stack([chip, core])
        self.sums, self.halves = {}, {}

    def _exchange(self, names):
        return exchange_comm([self.grads[k] for k in names])

    def _scatter(self, names):
        return scatter_comm([self.sums[k][1] for k in names])

    def _pair_sums(self, names, recv):
        for k, r in zip(names, recv):
            self.sums[k] = add_own_half(self.core, self.grads[k], r, "pair_sum_" + k)

    def _chip_sums(self, names, recv):
        for k, r in zip(names, recv):
            self.halves[k] = add_chip_sums(self.place, self.sums[k][0], r, "chip_sum_" + k)

    def before(self, kernel_name):
        if kernel_name in GATHER_ON:
            return gather_comm([self.bufs[k] for k in GATHER_ON[kernel_name]])
        for names, exchange_on, scatter_on in REDUCE:
            if kernel_name == exchange_on:
                return self._exchange(names)
            if kernel_name == scatter_on:
                if exchange_on is None:
                    self._pair_sums(names, run_comm(self._exchange(names), "exchange_" + names[0]))
                return self._scatter(names)
        return None

    def after(self, kernel_name, results):
        if kernel_name in GATHER_ON:
            self.wts.update(zip(GATHER_ON[kernel_name], results))
        for names, exchange_on, scatter_on in REDUCE:
            if kernel_name == exchange_on:
                self._pair_sums(names, results)
            if kernel_name == scatter_on:
                self._chip_sums(names, results)

    def finish(self):
        for names, exchange_on, scatter_on in REDUCE:
            if scatter_on is None:
                self._pair_sums(names, run_comm(self._exchange(names), "exchange_" + names[0]))
                self._chip_sums(names, run_comm(self._scatter(names), "scatter_" + names[0]))
        return dict(zip(BIG, run_comm(join_comm([self.halves[k] for k in BIG]), "join_halves")))


SMALL = ("ffn1_norm", "mix_norm", "s5_lam_re", "s5_lam_im", "s5_log_dt", "s5_b_re", "s5_b_im", "s5_c_re",
         "s5_c_im", "s5_d", "hg_lower_bound", "hg_out_norm", "ffn2_norm", "ple_norm", "final_norm")
WEIGHTS = ("ffn1_norm", "ffn1_w_gate", "ffn1_w_up", "ffn1_w_down", "mix_norm", "w_in", "s5_lam_re", "s5_lam_im",
           "s5_log_dt", "s5_b_re", "s5_b_im", "s5_c_re", "s5_c_im", "s5_d", "s5_glu_val", "s5_glu_gate",
           "hg_lower_bound", "hg_out_norm", "hg_w_out", "w_merge_out", "ffn2_norm", "ffn2_w_gate", "ffn2_w_up",
           "ffn2_w_down", "ple_norm", "ple_w_gate", "ple_w_proj", "final_norm")


def _as_rows(name, w):
    return jnp.swapaxes(w[0], 0, 1) if name in FFN_T else w[0]


def _from_rows(name, w):
    return (jnp.swapaxes(w, 0, 1) if name in FFN_T else w)[None]


def _gather_buffer(name, w_rows, chip):
    r, c = BIG_SHARD[name]
    shard = jnp.pad(w_rows.astype(BF16), ((0, r - w_rows.shape[0]), (0, 0)))
    return lax.dynamic_update_slice(jnp.zeros((N_SHARD, r, c), BF16), shard[None], (chip, 0, 0))


def _pack(parts):
    flat = jnp.concatenate([jnp.zeros((128,), F32)] + [a.reshape(-1) for a in parts])
    rows = -(-flat.shape[0] // 2048) * 16
    return jnp.pad(flat, (0, rows * 128 - flat.shape[0])).reshape(rows, 128)


def _unpack(vec, likes):
    flat = vec.reshape(-1)
    out, off = [], 128
    for a in likes:
        out.append(flat[off:off + a.size].reshape(a.shape))
        off += a.size
    return out


def _small_view(name, w):
    if name.startswith("s5_") and name != "s5_d":
        return w[0]
    if name == "final_norm":
        return w.reshape(1, D_MODEL)
    return w


def kernel(x, p, ffn1_norm, ffn1_w_gate, ffn1_w_up, ffn1_w_down, mix_norm, w_in, s5_lam_re, s5_lam_im, s5_log_dt, s5_b_re, s5_b_im, s5_c_re, s5_c_im, s5_d, s5_glu_val, s5_glu_gate, hg_lower_bound, hg_out_norm, hg_w_out, w_merge_out, ffn2_norm, ffn2_w_gate, ffn2_w_up, ffn2_w_down, ple_norm, ple_w_gate, ple_w_proj, final_norm, loss_target, m_ffn1_norm, m_ffn1_w_gate, m_ffn1_w_up, m_ffn1_w_down, m_mix_norm, m_w_in, m_s5_lam_re, m_s5_lam_im, m_s5_log_dt, m_s5_b_re, m_s5_b_im, m_s5_c_re, m_s5_c_im, m_s5_d, m_s5_glu_val, m_s5_glu_gate, m_hg_lower_bound, m_hg_out_norm, m_hg_w_out, m_w_merge_out, m_ffn2_norm, m_ffn2_w_gate, m_ffn2_w_up, m_ffn2_w_down, m_ple_norm, m_ple_w_gate, m_ple_w_proj, m_final_norm, v_ffn1_norm, v_ffn1_w_gate, v_ffn1_w_up, v_ffn1_w_down, v_mix_norm, v_w_in, v_s5_lam_re, v_s5_lam_im, v_s5_log_dt, v_s5_b_re, v_s5_b_im, v_s5_c_re, v_s5_c_im, v_s5_d, v_s5_glu_val, v_s5_glu_gate, v_hg_lower_bound, v_hg_out_norm, v_hg_w_out, v_w_merge_out, v_ffn2_norm, v_ffn2_w_gate, v_ffn2_w_up, v_ffn2_w_down, v_ple_norm, v_ple_w_gate, v_ple_w_proj, v_final_norm):
    given = dict(locals())
    wv = {k: given[k] for k in WEIGHTS}
    mv = {k: given["m_" + k] for k in WEIGHTS}
    vv = {k: given["v_" + k] for k in WEIGHTS}

    core = lax.axis_index("c").astype(jnp.int32)
    chip = (2 * lax.axis_index("x") + lax.axis_index("y")).astype(jnp.int32)
    w_rows = {k: _as_rows(k, wv[k]) for k in BIG}
    sched = DistSchedule({k: _gather_buffer(k, w_rows[k], chip) for k in BIG}, chip, core)
    sm = {k: _small_view(k, wv[k]) for k in SMALL}

    loss_blk, dx, gsm = local_step(x[0], p[0, 0], loss_target[0], sched, sm)
    full = sched.finish()

    small_likes = [wv[k] for k in SMALL]
    packed = _pack([gsm[k] for k in SMALL])
    packed = packed.at[0, 0].set(loss_blk[0, 0])
    total = allreduce_small(packed)
    loss = total[0, 0]
    gsmall = dict(zip(SMALL, _unpack(total, small_likes)))

    grads, deltas, new_m, new_v = {}, {}, {}, {}
    for k in BIG:
        padded = full[k].shape != w_rows[k].shape
        res = adamw(w_rows[k], _as_rows(k, mv[k]), _as_rows(k, vv[k]), full[k], "adamw_" + k, copy_g=padded)
        grads[k] = _from_rows(k, res[3] if padded else full[k])
        deltas[k], new_m[k], new_v[k] = (_from_rows(k, a) for a in res[:3])
    sw = _pack([wv[k] for k in SMALL])
    smm = _pack([mv[k] for k in SMALL])
    svv = _pack([vv[k] for k in SMALL])
    sd, smn, svn = adamw(sw, smm, svv, total, "adamw_small")
    for k, d, mn, vn in zip(SMALL, _unpack(sd, small_likes), _unpack(smn, small_likes), _unpack(svn, small_likes)):
        grads[k], deltas[k], new_m[k], new_v[k] = gsmall[k], d, mn, vn

    return (loss, dx[None], *[grads[k] for k in WEIGHTS], *[deltas[k] for k in WEIGHTS],
            *[new_m[k] for k in WEIGHTS], *[new_v[k] for k in WEIGHTS])
```

```python
import math

import jax
import jax.numpy as jnp
from jax import lax
from jax.experimental import pallas as pl
from jax.experimental.pallas import tpu as pltpu

F32 = jnp.float32
BF16 = jnp.bfloat16

D_MODEL = 1024
D_FF = 2816
N_SHARD = 4
FF_SHARD = D_FF // N_SHARD
FF_PAD = 768
NORM_EPS = 1e-6
PLE_DIM = 256

S5_WIDTH = 512
S5_GROUPS = 32
S5_GROUP = 16
S5_STATE = 64
S5_N = S5_GROUPS * S5_STATE
S5_KT = 2

HG_HEADS = 8
HG_E = 128
HG_WIDTH = 1024
CHUNK = 64
IN_COLS = S5_WIDTH + 4 * HG_WIDTH + 2 * D_MODEL
IN_SPLITS = (0, 512, 1536, 2560, 3584, 4608, 5632, 6656)

ADAM_LR = 0.001
ADAM_B1 = 0.9
ADAM_B2 = 0.999
ADAM_EPS = 1e-08
ADAM_WD = 0.01
ADAM_STEP = 10

VMEM_LIMIT = 60 * 1024 * 1024
HIGHEST = lax.Precision.HIGHEST


def _cparams(sem=None, **kw):
    return pltpu.CompilerParams(dimension_semantics=sem, vmem_limit_bytes=VMEM_LIMIT, **kw)


def _const_spec(shape):
    nd = len(shape)
    return pl.BlockSpec(shape, lambda *_: (0,) * nd, pipeline_mode=pl.Buffered(1))


def _dot(a, b):
    return jnp.dot(a, b, preferred_element_type=F32)


def _dot_nt(a, b):
    return lax.dot_general(a, b, (((1,), (1,)), ((), ())), preferred_element_type=F32)


def _dot_tn(a, b):
    return lax.dot_general(a, b, (((0,), (0,)), ((), ())), preferred_element_type=F32)


def _sigmoid(x):
    return 1.0 / (1.0 + jnp.exp(-x))


def _rms_fwd(x, g):
    r = lax.rsqrt(jnp.mean(x * x, axis=-1, keepdims=True) + NORM_EPS)
    return x * r * g, r


def _rms_bwd(x, r, g, dy):
    xh = x * r
    dyg = dy * g
    m = jnp.mean(dyg * xh, axis=-1, keepdims=True)
    return r * (dyg - xh * m), jnp.sum(dy * xh, axis=0, keepdims=True)


def _accum(ref, val, first):
    @pl.when(first)
    def _():
        ref[...] = val

    @pl.when(jnp.logical_not(first))
    def _():
        ref[...] += val


def ffn_fwd(h, gain, wg, wu, wd, name, comm=None, tm=256):
    t = h.shape[0]

    def body(h_ref, g_ref, wg_ref, wu_ref, wd_ref, o_ref, a_ref, b_ref):
        hv = h_ref[...]
        n, _ = _rms_fwd(hv, g_ref[...])
        nb = n.astype(BF16)
        acc = jnp.zeros((tm, D_MODEL), F32)
        for s in range(N_SHARD):
            a = _dot_nt(nb, wg_ref[s])
            b = _dot_nt(nb, wu_ref[s])
            a_ref[s] = a.astype(BF16)
            b_ref[s] = b.astype(BF16)
            sv = (a * _sigmoid(a) * b).astype(BF16)
            acc = acc + _dot(sv, wd_ref[s])
        o_ref[...] = hv + 0.5 * acc

    return _carry(
        body, comm, name=name, steps=t // tm,
        out_shape=(jax.ShapeDtypeStruct((t, D_MODEL), F32),
                   jax.ShapeDtypeStruct((N_SHARD, t, FF_PAD), BF16),
                   jax.ShapeDtypeStruct((N_SHARD, t, FF_PAD), BF16)),
        in_specs=[pl.BlockSpec((tm, D_MODEL), lambda i: (i, 0)),
                  _const_spec((1, D_MODEL)),
                  _const_spec((N_SHARD, FF_PAD, D_MODEL)),
                  _const_spec((N_SHARD, FF_PAD, D_MODEL)),
                  _const_spec((N_SHARD, FF_PAD, D_MODEL))],
        out_specs=(pl.BlockSpec((tm, D_MODEL), lambda i: (i, 0)),
                   pl.BlockSpec((N_SHARD, tm, FF_PAD), lambda i: (0, i, 0)),
                   pl.BlockSpec((N_SHARD, tm, FF_PAD), lambda i: (0, i, 0))),
        args=(h, gain, wg, wu, wd),
    )


def ffn_bwd(dho, h, a, b, gain, wg, wu, wd, name, comm=None, tm=256):
    t = h.shape[0]

    def body(dho_ref, h_ref, a_ref, b_ref, g_ref, wg_ref, wu_ref, wd_ref,
             dh_ref, dg_ref, nb_ref, dhb_ref, da_ref, db_ref, s_ref):
        hv = h_ref[...]
        g = g_ref[...]
        n, r = _rms_fwd(hv, g)
        nb_ref[...] = n.astype(BF16)
        dhalf = (0.5 * dho_ref[...]).astype(BF16)
        dhb_ref[...] = dhalf
        dn = jnp.zeros((tm, D_MODEL), F32)
        for s in range(N_SHARD):
            av = a_ref[s].astype(F32)
            bv = b_ref[s].astype(F32)
            sg = _sigmoid(av)
            sil = av * sg
            s_ref[s] = (sil * bv).astype(BF16)
            ds = _dot_nt(dhalf, wd_ref[s])
            da = (ds * bv * (sg * (1.0 + av * (1.0 - sg)))).astype(BF16)
            db = (ds * sil).astype(BF16)
            da_ref[s] = da
            db_ref[s] = db
            dn = dn + _dot(da, wg_ref[s]) + _dot(db, wu_ref[s])
        dx, dg = _rms_bwd(hv, r, g, dn)
        dh_ref[...] = dho_ref[...] + dx
        _accum(dg_ref, dg, pl.program_id(0) == 0)

    tok = pl.BlockSpec((tm, D_MODEL), lambda i: (i, 0))
    hid = pl.BlockSpec((N_SHARD, tm, FF_PAD), lambda i: (0, i, 0))
    return _carry(
        body, comm, name=name, steps=t // tm,
        out_shape=(jax.ShapeDtypeStruct((t, D_MODEL), F32),
                   jax.ShapeDtypeStruct((1, D_MODEL), F32),
                   jax.ShapeDtypeStruct((t, D_MODEL), BF16),
                   jax.ShapeDtypeStruct((t, D_MODEL), BF16),
                   jax.ShapeDtypeStruct((N_SHARD, t, FF_PAD), BF16),
                   jax.ShapeDtypeStruct((N_SHARD, t, FF_PAD), BF16),
                   jax.ShapeDtypeStruct((N_SHARD, t, FF_PAD), BF16)),
        in_specs=[tok, tok, hid, hid, _const_spec((1, D_MODEL)),
                  _const_spec((N_SHARD, FF_PAD, D_MODEL)),
                  _const_spec((N_SHARD, FF_PAD, D_MODEL)),
                  _const_spec((N_SHARD, FF_PAD, D_MODEL))],
        out_specs=(tok, pl.BlockSpec((1, D_MODEL), lambda i: (0, 0)), tok, tok, hid, hid, hid),
        args=(dho, h, a, b, gain, wg, wu, wd),
    )


TN_VMEM_BUDGET = 44 * 1024 * 1024


def tn_matmul(x, y, name, shard):
    x3, y3 = x.ndim == 3, y.ndim == 3
    t = x.shape[-2]
    m = x.shape[-1] // (N_SHARD if (shard == "rows" and not x3) else 1)
    n = y.shape[-1] // (N_SHARD if (shard == "cols" and not y3) else 1)
    per_token = 2 * (m * x.dtype.itemsize + n * y.dtype.itemsize)
    tk = t
    while tk > 512 and tk * per_token + 2 * m * n * 4 > TN_VMEM_BUDGET:
        tk //= 2
    nk = t // tk

    def body(x_ref, y_ref, o_ref):
        _accum(o_ref, _dot_tn(x_ref[...].astype(BF16), y_ref[...].astype(BF16)), pl.program_id(1) == 0)

    if x3:
        x_spec = pl.BlockSpec((None, tk, m), lambda s, k: (s, k, 0))
    elif shard == "rows":
        x_spec = pl.BlockSpec((tk, m), lambda s, k: (k, s))
    else:
        x_spec = pl.BlockSpec((tk, m), lambda s, k: (k, 0))
    if y3:
        y_spec = pl.BlockSpec((None, tk, n), lambda s, k: (s, k, 0))
    elif shard == "cols":
        y_spec = pl.BlockSpec((tk, n), lambda s, k: (k, s))
    else:
        y_spec = pl.BlockSpec((tk, n), lambda s, k: (k, 0))
    return pl.pallas_call(
        body, name=name, grid=(N_SHARD, nk),
        out_shape=jax.ShapeDtypeStruct((N_SHARD, m, n), F32),
        in_specs=[x_spec, y_spec],
        out_specs=pl.BlockSpec((None, m, n), lambda s, k: (s, 0, 0)),
        compiler_params=_cparams(("arbitrary", "arbitrary")),
    )(x, y)


def inproj_fwd(h, gain, w_in, tm=256):
    t = h.shape[0]
    widths = [IN_SPLITS[j + 1] - IN_SPLITS[j] for j in range(7)]
    sh_cols = IN_COLS // N_SHARD

    def body(h_ref, g_ref, w_ref, *outs):
        n, _ = _rms_fwd(h_ref[...], g_ref[...])
        nb = n.astype(BF16)
        proj = jnp.concatenate([_dot(nb, w_ref[s]) for s in range(N_SHARD)], axis=1)
        for j, o_ref in enumerate(outs):
            o_ref[...] = proj[:, IN_SPLITS[j]:IN_SPLITS[j + 1]]

    return pl.pallas_call(
        body, name="inproj_fwd", grid=(t // tm,),
        out_shape=tuple(jax.ShapeDtypeStruct((t, w), F32) for w in widths),
        in_specs=[pl.BlockSpec((tm, D_MODEL), lambda i: (i, 0)),
                  _const_spec((1, D_MODEL)),
                  _const_spec((N_SHARD, D_MODEL, sh_cols))],
        out_specs=tuple(pl.BlockSpec((tm, w), lambda i: (i, 0)) for w in widths),
        compiler_params=_cparams(("arbitrary",)),
    )(h, gain, w_in)


def inproj_bwd(dres, h, gain, w_in, dparts, comm=None, tm=256):
    t = h.shape[0]
    widths = [IN_SPLITS[j + 1] - IN_SPLITS[j] for j in range(7)]
    sh_cols = IN_COLS // N_SHARD

    def body(dres_ref, h_ref, g_ref, w_ref, d0, d1, d2, d3, d4, d5, d6, dh_ref, dg_ref, nb_ref, dp_ref):
        hv = h_ref[...]
        g = g_ref[...]
        n, r = _rms_fwd(hv, g)
        nb_ref[...] = n.astype(BF16)
        dproj = jnp.concatenate([d[...] for d in (d0, d1, d2, d3, d4, d5, d6)], axis=1).astype(BF16)
        dp_ref[...] = dproj
        dn = jnp.zeros((tm, D_MODEL), F32)
        for s in range(N_SHARD):
            dn = dn + _dot_nt(dproj[:, s * sh_cols:(s + 1) * sh_cols], w_ref[s])
        dx, dg = _rms_bwd(hv, r, g, dn)
        dh_ref[...] = dres_ref[...] + dx
        _accum(dg_ref, dg, pl.program_id(0) == 0)

    tok = pl.BlockSpec((tm, D_MODEL), lambda i: (i, 0))
    return _carry(
        body, comm, name="inproj_bwd", steps=t // tm,
        out_shape=(jax.ShapeDtypeStruct((t, D_MODEL), F32),
                   jax.ShapeDtypeStruct((1, D_MODEL), F32),
                   jax.ShapeDtypeStruct((t, D_MODEL), BF16),
                   jax.ShapeDtypeStruct((t, IN_COLS), BF16)),
        in_specs=[tok, tok, _const_spec((1, D_MODEL)), _const_spec((N_SHARD, D_MODEL, sh_cols))]
                 + [pl.BlockSpec((tm, w), lambda i: (i, 0)) for w in widths],
        out_specs=(tok, pl.BlockSpec((1, D_MODEL), lambda i: (0, 0)), tok,
                   pl.BlockSpec((tm, IN_COLS), lambda i: (i, 0))),
        args=(dres, h, gain, w_in, *dparts),
    )


def s5_prep(lam_re, lam_im, log_dt, b_re, b_im, c_re, c_im):
    dt = jnp.exp(log_dt)[:, None]
    mag = jnp.exp(lam_re * dt)
    lbr = mag * jnp.cos(lam_im * dt)
    lbi = mag * jnp.sin(lam_im * dt)
    den = lam_re * lam_re + lam_im * lam_im
    nr, ni = lbr - 1.0, lbi
    kr = (nr * lam_re + ni * lam_im) / den
    ki = (ni * lam_re - nr * lam_im) / den
    bbr = kr[..., None] * b_re - ki[..., None] * b_im
    bbi = kr[..., None] * b_im + ki[..., None] * b_re
    eye = jnp.eye(16, dtype=F32)

    def bm(bp):
        return jnp.einsum('kgph,gG->kghGp', bp.reshape(S5_KT, 16, S5_STATE, S5_GROUP), eye).reshape(S5_KT, 256, 1024)

    def cm(cp):
        return jnp.einsum('kghp,gG->kgpGh', cp.reshape(S5_KT, 16, S5_GROUP, S5_STATE), eye).reshape(S5_KT, 1024, 256)

    lam_bar = jnp.stack([lbr.reshape(S5_N), lbi.reshape(S5_N)])
    bmat = jnp.stack([bm(bbr), bm(bbi)])
    cmat = jnp.stack([cm(c_re), -cm(c_im)])
    return lam_bar, bmat, cmat


def _lam_powers(lam_bar):
    lr, li = lam_bar[0], lam_bar[1]
    pr, pi = [lr], [li]
    for _ in range(7):
        pr, pi = pr + [pr[-1] * lr - pi[-1] * li], pi + [pr[-1] * li + pi[-1] * lr]
    return jnp.stack(pr), jnp.stack(pi)


SCAN_SHIFTS = ((1, 0), (2, 1), (4, 3))


def _scan_tables(pw_r, pw_i, reverse):
    rows = jnp.arange(8)[:, None]
    planes_r, planes_i = [], []
    for sh, idx in SCAN_SHIFTS:
        keep = (rows < 8 - sh) if reverse else (rows >= sh)
        planes_r.append(jnp.where(keep, pw_r[idx:idx + 1], 0.0))
        planes_i.append(jnp.where(keep, pw_i[idx:idx + 1], 0.0))
    carry = [pw_r[::-1], pw_i[::-1]] if reverse else [pw_r, pw_i]
    return jnp.stack(planes_r + planes_i + carry)


def s5_fwd(u, tab, bmat, cmat, dvec, comm=None, tm=256):
    t = u.shape[0]
    nch = tm // 8

    def body(u_ref, tab_ref, b_ref, c_ref, d_ref, y_ref, xp_ref, x_scr, xp_scr, carry):
        @pl.when(pl.program_id(0) == 0)
        def _():
            carry[...] = jnp.zeros_like(carry)

        uv = u_ref[...]
        ub = uv.astype(BF16)
        for part in range(2):
            for kt in range(S5_KT):
                x_scr[:, pl.ds(part * S5_N + kt * 1024, 1024)] = _dot(ub[:, kt * 256:(kt + 1) * 256], b_ref[part, kt])
        row = lax.broadcasted_iota(jnp.int32, (8, S5_N), 0)

        def chunk(i, c):
            cr, ci = c
            r0 = pl.multiple_of(i * 8, 8)
            xr = x_scr[pl.ds(r0, 8), pl.ds(0, S5_N)]
            xi = x_scr[pl.ds(r0, 8), pl.ds(S5_N, S5_N)]
            for lvl, (sh, _) in enumerate(SCAN_SHIFTS):
                sr = pltpu.roll(xr, sh, 0)
                si = pltpu.roll(xi, sh, 0)
                lr = tab_ref[lvl]
                li = tab_ref[3 + lvl]
                xr, xi = xr + lr * sr - li * si, xi + lr * si + li * sr
            pwr = tab_ref[6]
            pwi = tab_ref[7]
            xr, xi = xr + pwr * cr - pwi * ci, xi + pwr * ci + pwi * cr
            x_scr[pl.ds(r0, 8), pl.ds(0, S5_N)] = xr
            x_scr[pl.ds(r0, 8), pl.ds(S5_N, S5_N)] = xi
            xp_scr[pl.ds(r0, 8), pl.ds(0, S5_N)] = jnp.where(row == 0, cr, pltpu.roll(xr, 1, 0))
            xp_scr[pl.ds(r0, 8), pl.ds(S5_N, S5_N)] = jnp.where(row == 0, ci, pltpu.roll(xi, 1, 0))
            return xr[7:8, :], xi[7:8, :]

        cr, ci = lax.fori_loop(0, nch, chunk, (carry[0:1, :], carry[1:2, :]))
        carry[0:1, :] = cr
        carry[1:2, :] = ci
        xp_ref[...] = xp_scr[...].astype(BF16)
        for kt in range(S5_KT):
            acc = jnp.zeros((tm, 256), F32)
            for part in range(2):
                acc = acc + _dot(x_scr[:, pl.ds(part * S5_N + kt * 1024, 1024)].astype(BF16), c_ref[part, kt])
            y_ref[:, pl.ds(kt * 256, 256)] = acc + d_ref[:, pl.ds(kt * 256, 256)] * uv[:, kt * 256:(kt + 1) * 256]

    return _carry(
        body, comm, name="s5_fwd", steps=t // tm,
        out_shape=(jax.ShapeDtypeStruct((t, S5_WIDTH), F32),
                   jax.ShapeDtypeStruct((t, 2 * S5_N), BF16)),
        in_specs=[pl.BlockSpec((tm, S5_WIDTH), lambda i: (i, 0)),
                  _const_spec((8, 8, S5_N)),
                  _const_spec((2, S5_KT, 256, 1024)), _const_spec((2, S5_KT, 1024, 256)),
                  _const_spec((1, S5_WIDTH))],
        out_specs=(pl.BlockSpec((tm, S5_WIDTH), lambda i: (i, 0)),
                   pl.BlockSpec((tm, 2 * S5_N), lambda i: (i, 0))),
        scratch_shapes=[pltpu.VMEM((tm, 2 * S5_N), F32), pltpu.VMEM((tm, 2 * S5_N), F32),
                        pltpu.VMEM((8, S5_N), F32)],
        args=(u, tab, bmat, cmat, dvec),
    )


def s5_bwd(dy, u, xp, tab, bmat, bmat_t, cmat_t, dvec, comm=None, tm=256):
    t = u.shape[0]
    nt = t // tm
    nch = tm // 8

    def body(dy_ref, u_ref, xp_ref, tab_ref, b_ref, bt_ref, ct_ref, d_ref,
             du_ref, db_ref, dc_ref, dl_ref, dd_ref, g_scr, x_scr, carry):
        first = pl.program_id(0) == 0

        @pl.when(first)
        def _():
            carry[...] = jnp.zeros_like(carry)
            dl_ref[...] = jnp.zeros_like(dl_ref)

        dyv = dy_ref[...]
        uv = u_ref[...]
        dyb = dyv.astype(BF16)
        ub = uv.astype(BF16)
        lr1 = tab_ref[6, 7:8, :]
        li1 = tab_ref[7, 7:8, :]
        for kt in range(S5_KT):
            cols = pl.ds(kt * 1024, 1024)
            colsi = pl.ds(S5_N + kt * 1024, 1024)
            g_scr[:, cols] = _dot(dyb[:, kt * 256:(kt + 1) * 256], ct_ref[0, kt])
            g_scr[:, colsi] = _dot(dyb[:, kt * 256:(kt + 1) * 256], ct_ref[1, kt])
            bur = _dot(ub[:, kt * 256:(kt + 1) * 256], b_ref[0, kt])
            bui = _dot(ub[:, kt * 256:(kt + 1) * 256], b_ref[1, kt])
            xpr = xp_ref[:, cols].astype(F32)
            xpi = xp_ref[:, colsi].astype(F32)
            lrk = lr1[:, kt * 1024:(kt + 1) * 1024]
            lik = li1[:, kt * 1024:(kt + 1) * 1024]
            x_scr[:, cols] = lrk * xpr - lik * xpi + bur
            x_scr[:, colsi] = lrk * xpi + lik * xpr + bui

        def chunk(j, c):
            cr, ci = c
            r0 = pl.multiple_of((nch - 1 - j) * 8, 8)
            gr = g_scr[pl.ds(r0, 8), pl.ds(0, S5_N)]
            gi = g_scr[pl.ds(r0, 8), pl.ds(S5_N, S5_N)]
            for lvl, (sh, _) in enumerate(SCAN_SHIFTS):
                sr = pltpu.roll(gr, 8 - sh, 0)
                si = pltpu.roll(gi, 8 - sh, 0)
                lr = tab_ref[lvl]
                li = tab_ref[3 + lvl]
                gr, gi = gr + lr * sr + li * si, gi + lr * si - li * sr
            pvr = tab_ref[6]
            pvi = tab_ref[7]
            gr, gi = gr + pvr * cr + pvi * ci, gi + pvr * ci - pvi * cr
            g_scr[pl.ds(r0, 8), pl.ds(0, S5_N)] = gr
            g_scr[pl.ds(r0, 8), pl.ds(S5_N, S5_N)] = gi
            return gr[0:1, :], gi[0:1, :]

        cr, ci = lax.fori_loop(0, nch, chunk, (carry[0:1, :], carry[1:2, :]))
        carry[0:1, :] = cr
        carry[1:2, :] = ci

        for kt in range(S5_KT):
            cols = pl.ds(kt * 1024, 1024)
            colsi = pl.ds(S5_N + kt * 1024, 1024)
            gr, gi = g_scr[:, cols], g_scr[:, colsi]
            xpr, xpi = xp_ref[:, cols].astype(F32), xp_ref[:, colsi].astype(F32)
            dl_ref[0, :, cols] += jnp.sum((gr * xpr + gi * xpi).reshape(nch, 8, 1024), axis=0)
            dl_ref[1, :, cols] += jnp.sum((gi * xpr - gr * xpi).reshape(nch, 8, 1024), axis=0)
            du = jnp.zeros((tm, 256), F32)
            ukt = ub[:, kt * 256:(kt + 1) * 256]
            dykt = dyb[:, kt * 256:(kt + 1) * 256]
            for part in range(2):
                gb = g_scr[:, pl.ds(part * S5_N + kt * 1024, 1024)].astype(BF16)
                xb = x_scr[:, pl.ds(part * S5_N + kt * 1024, 1024)].astype(BF16)
                du = du + _dot(gb, bt_ref[part, kt])
                dbv = _dot_tn(ukt, gb)
                dcv = _dot_tn(xb, dykt)

                @pl.when(first)
                def _():
                    db_ref[part, kt] = dbv
                    dc_ref[part, kt] = dcv

                @pl.when(jnp.logical_not(first))
                def _():
                    db_ref[part, kt] += dbv
                    dc_ref[part, kt] += dcv
            du_ref[:, pl.ds(kt * 256, 256)] = du + d_ref[:, pl.ds(kt * 256, 256)] * dyv[:, kt * 256:(kt + 1) * 256]
        _accum(dd_ref, jnp.sum(dyv * uv, axis=0, keepdims=True), first)

    rev = lambda i: (nt - 1 - i, 0)
    return _carry(
        body, comm, name="s5_bwd", steps=nt,
        out_shape=(jax.ShapeDtypeStruct((t, S5_WIDTH), F32),
                   jax.ShapeDtypeStruct((2, S5_KT, 256, 1024), F32),
                   jax.ShapeDtypeStruct((2, S5_KT, 1024, 256), F32),
                   jax.ShapeDtypeStruct((2, 8, S5_N), F32),
                   jax.ShapeDtypeStruct((1, S5_WIDTH), F32)),
        in_specs=[pl.BlockSpec((tm, S5_WIDTH), rev), pl.BlockSpec((tm, S5_WIDTH), rev),
                  pl.BlockSpec((tm, 2 * S5_N), rev),
                  _const_spec((8, 8, S5_N)),
                  _const_spec((2, S5_KT, 256, 1024)), _const_spec((2, S5_KT, 1024, 256)),
                  _const_spec((2, S5_KT, 256, 1024)), _const_spec((1, S5_WIDTH))],
        out_specs=(pl.BlockSpec((tm, S5_WIDTH), rev),
                   pl.BlockSpec((2, S5_KT, 256, 1024), lambda i: (0, 0, 0, 0)),
                   pl.BlockSpec((2, S5_KT, 1024, 256), lambda i: (0, 0, 0, 0)),
                   pl.BlockSpec((2, 8, S5_N), lambda i: (0, 0, 0)),
                   pl.BlockSpec((1, S5_WIDTH), lambda i: (0, 0))),
        scratch_shapes=[pltpu.VMEM((tm, 2 * S5_N), F32), pltpu.VMEM((tm, 2 * S5_N), F32),
                        pltpu.VMEM((8, S5_N), F32)],
        args=(dy, u, xp, tab, bmat, bmat_t, cmat_t, dvec),
    )


def _hg_gates(z, lb):
    sg = _sigmoid(z)
    sgn = _sigmoid(-z)
    fg = lb + (1.0 - lb) * sg
    return sg, sgn, fg, jnp.log(fg), (1.0 - lb) * sgn


def _hg_decays(g, tril):
    gc = jnp.dot(tril, g, precision=HIGHEST, preferred_element_type=F32)
    mid = gc[CHUNK // 2 - 1:CHUNK // 2, :]
    last = gc[CHUNK - 1:CHUNK, :]
    return jnp.exp(gc), jnp.exp(gc - mid), jnp.exp(mid - gc), jnp.exp(last - gc), jnp.exp(last)


def _split_bf16(x):
    hi = x.astype(BF16)
    return hi, (x - hi.astype(F32)).astype(BF16)


def _hg_scores(qt, qlo, kt, klo, sl, causal, first_chunk):
    a = _dot_nt(qt[:, sl], kt[:, sl])
    a = a + lax.cond(first_chunk,
                     lambda: _dot_nt(qt[:, sl], klo[:, sl]) + _dot_nt(qlo[:, sl], kt[:, sl]),
                     lambda: jnp.zeros((CHUNK, CHUNK), F32))
    return jnp.where(causal, a, 0.0).astype(BF16)


def hgrn_fwd(q, f, v, lb):
    t = q.shape[0]
    nc = t // CHUNK
    scale = HG_E ** -0.5

    def body(q_ref, f_ref, v_ref, lb_ref, o_ref, st_ref, state):
        @pl.when(pl.program_id(0) == 0)
        def _():
            state[...] = jnp.zeros_like(state)

        ri = lax.broadcasted_iota(jnp.int32, (CHUNK, CHUNK), 0)
        ci = lax.broadcasted_iota(jnp.int32, (CHUNK, CHUNK), 1)
        causal = ri >= ci
        tril = causal.astype(F32)
        _, _, _, g, k = _hg_gates(f_ref[...], lb_ref[...])
        eg, eq, ek, ed, el = _hg_decays(g, tril)
        qs = q_ref[...] * scale
        qg = (qs * eg).astype(BF16)
        qt, qlo = _split_bf16(qs * eq)
        kt, klo = _split_bf16(k * ek)
        kd = (k * ed).astype(BF16)
        vb = v_ref[...].astype(BF16)
        for h in range(HG_HEADS):
            sl = slice(h * HG_E, (h + 1) * HG_E)
            st = state[h]
            a = _hg_scores(qt, qlo, kt, klo, sl, causal, pl.program_id(0) == 0)
            o_ref[:, sl] = _dot(a, vb[:, sl]) + _dot_nt(qg[:, sl], st.astype(BF16))
            st_new = st * el[:, sl] + _dot_tn(vb[:, sl], kd[:, sl])
            state[h] = st_new
            st_ref[0, h] = st_new

    tok = pl.BlockSpec((CHUNK, HG_WIDTH), lambda i: (i, 0))
    return pl.pallas_call(
        body, name="hgrn_fwd", grid=(nc,),
        out_shape=(jax.ShapeDtypeStruct((t, HG_WIDTH), F32),
                   jax.ShapeDtypeStruct((nc, HG_HEADS, HG_E, HG_E), F32)),
        in_specs=[tok, tok, tok, _const_spec((1, HG_WIDTH))],
        out_specs=(tok, pl.BlockSpec((1, HG_HEADS, HG_E, HG_E), lambda i: (i, 0, 0, 0))),
        scratch_shapes=[pltpu.VMEM((HG_HEADS, HG_E, HG_E), F32)],
        compiler_params=_cparams(("arbitrary",)),
    )(q, f, v, lb)


def hgrn_bwd(do, q, f, v, lb, states, comm=None):
    t = q.shape[0]
    nc = t // CHUNK
    scale = HG_E ** -0.5

    def body(do_ref, q_ref, f_ref, v_ref, lb_ref, s0_ref, dq_ref, df_ref, dv_ref, dlb_ref, dstate):
        first = pl.program_id(0) == 0
        c_idx = nc - 1 - pl.program_id(0)

        @pl.when(first)
        def _():
            dstate[...] = jnp.zeros_like(dstate)

        ri = lax.broadcasted_iota(jnp.int32, (CHUNK, CHUNK), 0)
        ci = lax.broadcasted_iota(jnp.int32, (CHUNK, CHUNK), 1)
        causal = ri >= ci
        tril = causal.astype(F32)
        triu = (ri <= ci).astype(F32)
        lb = lb_ref[...]
        sg, sgn, fg, g, k = _hg_gates(f_ref[...], lb)
        eg, eq, ek, ed, el = _hg_decays(g, tril)
        qs = q_ref[...] * scale
        qg = (qs * eg).astype(BF16)
        qt, qlo = _split_bf16(qs * eq)
        kt, klo = _split_bf16(k * ek)
        kd = (k * ed).astype(BF16)
        vb = v_ref[...].astype(BF16)
        dob = do_ref[...].astype(BF16)
        has_prev = jnp.where(c_idx > 0, 1.0, 0.0)
        dqs_l, dk_l, dgc_l, dgl_l = [], [], [], []
        for h in range(HG_HEADS):
            sl = slice(h * HG_E, (h + 1) * HG_E)
            s0 = s0_ref[0, h] * has_prev
            ds1 = dstate[h]
            ds1b = ds1.astype(BF16)
            a = _hg_scores(qt, qlo, kt, klo, sl, causal, c_idx == 0)
            da = jnp.where(causal, _dot_nt(dob[:, sl], vb[:, sl]), 0.0).astype(BF16)
            dv_ref[:, sl] = _dot_tn(a, dob[:, sl]) + _dot_nt(kd[:, sl], ds1b)
            dkd = _dot(vb[:, sl], ds1b)
            dqt = _dot(da, kt[:, sl])
            dkt = _dot_tn(da, qt[:, sl])
            dqg = _dot(dob[:, sl], s0.astype(BF16))
            dqs_l.append(dqt * eq[:, sl] + dqg * eg[:, sl])
            dk_l.append(dkt * ek[:, sl] + dkd * ed[:, sl])
            kd_dkd = kd[:, sl].astype(F32) * dkd
            dgc_l.append(qt[:, sl].astype(F32) * dqt - kt[:, sl].astype(F32) * dkt
                         + qg[:, sl].astype(F32) * dqg - kd_dkd)
            dgl_l.append(el[:, sl] * jnp.sum(ds1 * s0, axis=0, keepdims=True)
                         + jnp.sum(kd_dkd, axis=0, keepdims=True))
            dstate[h] = ds1 * el[:, sl] + _dot_tn(dob[:, sl], qg[:, sl])
        dqs = jnp.concatenate(dqs_l, axis=1)
        dk = jnp.concatenate(dk_l, axis=1)
        dgl = jnp.concatenate(dgl_l, axis=1)
        dq_ref[...] = dqs * scale
        rowc = lax.broadcasted_iota(jnp.int32, (CHUNK, HG_WIDTH), 0)
        dgc = jnp.concatenate(dgc_l, axis=1) + jnp.where(rowc == CHUNK - 1, dgl, 0.0)
        dg = jnp.dot(triu, dgc, precision=HIGHEST, preferred_element_type=F32)
        w = dg / fg - dk
        df_ref[...] = w * (1.0 - lb) * sg * sgn
        _accum(dlb_ref, jnp.sum(w * sgn, axis=0, keepdims=True), first)

    rev = lambda i: (nc - 1 - i, 0)
    tok = pl.BlockSpec((CHUNK, HG_WIDTH), rev)
    return _carry(
        body, comm, name="hgrn_bwd", steps=nc,
        out_shape=(jax.ShapeDtypeStruct((t, HG_WIDTH), F32),
                   jax.ShapeDtypeStruct((t, HG_WIDTH), F32),
                   jax.ShapeDtypeStruct((t, HG_WIDTH), F32),
                   jax.ShapeDtypeStruct((1, HG_WIDTH), F32)),
        in_specs=[tok, tok, tok, tok, _const_spec((1, HG_WIDTH)),
                  pl.BlockSpec((1, HG_HEADS, HG_E, HG_E), lambda i: (jnp.maximum(nc - 2 - i, 0), 0, 0, 0))],
        out_specs=(tok, tok, tok, pl.BlockSpec((1, HG_WIDTH), lambda i: (0, 0))),
        scratch_shapes=[pltpu.VMEM((HG_HEADS, HG_E, HG_E), F32)],
        args=(do, q, f, v, lb, states),
    )


GELU_C = math.sqrt(2.0 / math.pi)


def _gelu(x):
    th = jnp.tanh(GELU_C * (x + 0.044715 * x * x * x))
    return 0.5 * x * (1.0 + th), th


def _merge_core(ys5, o, og, ga, gb, wv_ref, wt_ref, ghg, who_ref):
    ys, th = _gelu(ys5)
    ysb = ys.astype(BF16)
    va = jnp.concatenate([_dot(ysb, wv_ref[s]) for s in range(N_SHARD)], axis=1)
    vt = jnp.concatenate([_dot(ysb, wt_ref[s]) for s in range(N_SHARD)], axis=1)
    svt = _sigmoid(vt)
    ya = va * svt
    rs, ons = [], []
    for h in range(HG_HEADS):
        oh = o[:, h * HG_E:(h + 1) * HG_E]
        r = lax.rsqrt(jnp.mean(oh * oh, axis=-1, keepdims=True) + NORM_EPS)
        rs.append(r)
        ons.append(oh * r)
    on = jnp.concatenate(ons, axis=1)
    sgo = _sigmoid(og)
    o2 = on * ghg * (og * sgo)
    o2b = o2.astype(BF16)
    yb = _dot(o2b, who_ref[...])
    sa = _sigmoid(ga)
    sb = _sigmoid(gb)
    mixed = sa * ya + sb * yb
    return dict(ys=ys, th=th, ysb=ysb, va=va, svt=svt, ya=ya, rs=rs, on=on, sgo=sgo, o2b=o2b, yb=yb,
                sa=sa, sb=sb, mixed=mixed)


def merge_fwd(h, ys5, o, og, ga, gb, wv, wt, ghg, who, wmo, tm=256):
    t = h.shape[0]

    def body(h_ref, ys5_ref, o_ref, og_ref, ga_ref, gb_ref, wv_ref, wt_ref, ghg_ref, who_ref, wmo_ref, out_ref):
        c = _merge_core(ys5_ref[...], o_ref[...], og_ref[...], ga_ref[...], gb_ref[...],
                        wv_ref, wt_ref, ghg_ref[...], who_ref)
        out_ref[...] = h_ref[...] + _dot(c["mixed"].astype(BF16), wmo_ref[...])

    tok = pl.BlockSpec((tm, D_MODEL), lambda i: (i, 0))
    return pl.pallas_call(
        body, name="merge_fwd", grid=(t // tm,),
        out_shape=jax.ShapeDtypeStruct((t, D_MODEL), F32),
        in_specs=[tok, pl.BlockSpec((tm, S5_WIDTH), lambda i: (i, 0)), tok, tok, tok, tok,
                  _const_spec((N_SHARD, S5_WIDTH, 256)), _const_spec((N_SHARD, S5_WIDTH, 256)),
                  _const_spec((1, HG_WIDTH)), _const_spec((HG_WIDTH, D_MODEL)), _const_spec((D_MODEL, D_MODEL))],
        out_specs=tok,
        compiler_params=_cparams(("arbitrary",)),
    )(h, ys5, o, og, ga, gb, wv, wt, ghg, who, wmo)


def merge_bwd(dh, ys5, o, og, ga, gb, wv, wt, ghg, who, wmo, comm=None, tm=256):
    t = dh.shape[0]

    def body(dh_ref, ys5_ref, o_ref, og_ref, ga_ref, gb_ref, wv_ref, wt_ref, ghg_ref, who_ref, wmo_ref,
             dys5_ref, do_ref, dog_ref, dga_ref, dgb_ref, dghg_ref,
             mixb_ref, dhb_ref, ysb_ref, dvab_ref, dvtb_ref, o2b_ref, dybb_ref):
        ys5 = ys5_ref[...]
        o = o_ref[...]
        og = og_ref[...]
        ghg = ghg_ref[...]
        c = _merge_core(ys5, o, og, ga_ref[...], gb_ref[...], wv_ref, wt_ref, ghg, who_ref)
        dhb = dh_ref[...].astype(BF16)
        dhb_ref[...] = dhb
        mixb_ref[...] = c["mixed"].astype(BF16)
        ysb_ref[...] = c["ysb"]
        o2b_ref[...] = c["o2b"]
        dmix = _dot_nt(dhb, wmo_ref[...])
        sa, sb = c["sa"], c["sb"]
        dya = dmix * sa
        dyb = dmix * sb
        dga_ref[...] = dmix * c["ya"] * sa * (1.0 - sa)
        dgb_ref[...] = dmix * c["yb"] * sb * (1.0 - sb)
        svt = c["svt"]
        dva = (dya * svt).astype(BF16)
        dvt = (dya * c["va"] * svt * (1.0 - svt)).astype(BF16)
        dvab_ref[...] = dva
        dvtb_ref[...] = dvt
        dys = jnp.zeros((tm, S5_WIDTH), F32)
        for s in range(N_SHARD):
            dys = dys + _dot_nt(dva[:, s * 256:(s + 1) * 256], wv_ref[s]) + _dot_nt(dvt[:, s * 256:(s + 1) * 256], wt_ref[s])
        th = c["th"]
        dgelu = 0.5 * (1.0 + th) + 0.5 * ys5 * (1.0 - th * th) * GELU_C * (1.0 + 3.0 * 0.044715 * ys5 * ys5)
        dys5_ref[...] = dys * dgelu
        dybb = dyb.astype(BF16)
        dybb_ref[...] = dybb
        do2 = _dot_nt(dybb, who_ref[...])
        sgo = c["sgo"]
        sil = og * sgo
        on = c["on"]
        dog_ref[...] = do2 * on * ghg * (sgo * (1.0 + og * (1.0 - sgo)))
        _accum(dghg_ref, jnp.sum(do2 * on * sil, axis=0, keepdims=True), pl.program_id(0) == 0)
        don = do2 * ghg * sil
        dos = []
        for h in range(HG_HEADS):
            sl = slice(h * HG_E, (h + 1) * HG_E)
            m = jnp.mean(don[:, sl] * on[:, sl], axis=-1, keepdims=True)
            dos.append(c["rs"][h] * (don[:, sl] - on[:, sl] * m))
        do_ref[...] = jnp.concatenate(dos, axis=1)

    tok = pl.BlockSpec((tm, D_MODEL), lambda i: (i, 0))
    s5b = pl.BlockSpec((tm, S5_WIDTH), lambda i: (i, 0))
    f32t = jax.ShapeDtypeStruct((t, D_MODEL), F32)
    bft = jax.ShapeDtypeStruct((t, D_MODEL), BF16)
    return _carry(
        body, comm, name="merge_bwd", steps=t // tm,
        out_shape=(jax.ShapeDtypeStruct((t, S5_WIDTH), F32), f32t, f32t, f32t, f32t,
                   jax.ShapeDtypeStruct((1, HG_WIDTH), F32),
                   bft, bft, jax.ShapeDtypeStruct((t, S5_WIDTH), BF16), bft, bft, bft, bft),
        in_specs=[tok, s5b, tok, tok, tok, tok,
                  _const_spec((N_SHARD, S5_WIDTH, 256)), _const_spec((N_SHARD, S5_WIDTH, 256)),
                  _const_spec((1, HG_WIDTH)), _const_spec((HG_WIDTH, D_MODEL)), _const_spec((D_MODEL, D_MODEL))],
        out_specs=(s5b, tok, tok, tok, tok, pl.BlockSpec((1, HG_WIDTH), lambda i: (0, 0)),
                   tok, tok, s5b, tok, tok, tok, tok),
        args=(dh, ys5, o, og, ga, gb, wv, wt, ghg, who, wmo),
    )


def head_fwd_bwd(h, p, tgt, gple, wpg, wpp, gfin, tm=256):
    t = h.shape[0]

    def body(h_ref, p_ref, tgt_ref, gple_ref, wpg_ref, wpp_ref, gfin_ref,
             loss_ref, dh_ref, dgple_ref, dgfin_ref, nb_ref, dlb_ref, dppb_ref):
        first = pl.program_id(0) == 0
        hv = h_ref[...]
        gple = gple_ref[...]
        gfin = gfin_ref[...]
        n, r3 = _rms_fwd(hv, gple)
        nb = n.astype(BF16)
        nb_ref[...] = nb
        pg = _sigmoid(_dot(nb, wpg_ref[...]))
        pb = p_ref[...].astype(BF16)
        pp = jnp.concatenate([_dot(pb, wpp_ref[s]) for s in range(N_SHARD)], axis=1)
        h4 = hv + pg * pp
        y, r4 = _rms_fwd(h4, gfin)
        err = y - tgt_ref[...]
        lsum = 0.5 * jnp.sum(jnp.sum(err * err, axis=-1, keepdims=True), axis=0, keepdims=True) / D_MODEL
        _accum(loss_ref, jnp.broadcast_to(lsum, (8, 128)), first)
        dy = err * (1.0 / D_MODEL)
        dh4, dgf = _rms_bwd(h4, r4, gfin, dy)
        _accum(dgfin_ref, dgf, first)
        dpp = dh4 * pg
        dppb_ref[...] = dpp.astype(BF16)
        dl = (dh4 * pp * pg * (1.0 - pg)).astype(BF16)
        dlb_ref[...] = dl
        dn = _dot_nt(dl, wpg_ref[...])
        dx, dgp = _rms_bwd(hv, r3, gple, dn)
        _accum(dgple_ref, dgp, first)
        dh_ref[...] = dh4 + dx

    tok = pl.BlockSpec((tm, D_MODEL), lambda i: (i, 0))
    vec = pl.BlockSpec((1, D_MODEL), lambda i: (0, 0))
    bft = jax.ShapeDtypeStruct((t, D_MODEL), BF16)
    return pl.pallas_call(
        body, name="head_fwd_bwd", grid=(t // tm,),
        out_shape=(jax.ShapeDtypeStruct((8, 128), F32), jax.ShapeDtypeStruct((t, D_MODEL), F32),
                   jax.ShapeDtypeStruct((1, D_MODEL), F32), jax.ShapeDtypeStruct((1, D_MODEL), F32),
                   bft, bft, bft),
        in_specs=[tok, pl.BlockSpec((tm, PLE_DIM), lambda i: (i, 0)), tok,
                  _const_spec((1, D_MODEL)), _const_spec((D_MODEL, D_MODEL)),
                  _const_spec((N_SHARD, PLE_DIM, 256)), _const_spec((1, D_MODEL))],
        out_specs=(pl.BlockSpec((8, 128), lambda i: (0, 0)), tok, vec, vec, tok, tok, tok),
        compiler_params=_cparams(("arbitrary",)),
    )(h, p, tgt, gple, wpg, wpp, gfin)


BIG = ("ffn1_w_gate", "ffn1_w_up", "ffn1_w_down", "w_in", "s5_glu_val", "s5_glu_gate", "hg_w_out",
       "w_merge_out", "ffn2_w_gate", "ffn2_w_up", "ffn2_w_down", "ple_w_gate", "ple_w_proj")
FFN_T = ("ffn1_w_gate", "ffn1_w_up", "ffn2_w_gate", "ffn2_w_up")
BIG_SHARD = {
    "ffn1_w_gate": (FF_PAD, D_MODEL), "ffn1_w_up": (FF_PAD, D_MODEL), "ffn1_w_down": (FF_PAD, D_MODEL),
    "ffn2_w_gate": (FF_PAD, D_MODEL), "ffn2_w_up": (FF_PAD, D_MODEL), "ffn2_w_down": (FF_PAD, D_MODEL),
    "w_in": (D_MODEL, IN_COLS // N_SHARD), "s5_glu_val": (S5_WIDTH, 256), "s5_glu_gate": (S5_WIDTH, 256),
    "hg_w_out": (256, D_MODEL), "w_merge_out": (256, D_MODEL), "ple_w_gate": (256, D_MODEL),
    "ple_w_proj": (PLE_DIM, 256),
}


def _lower_bound(hb):
    return jax.nn.softmax(hb, axis=0)[0:1]


class Schedule:
    def __init__(self, wts):
        self.wts = dict(wts)
        self.grads = {}

    def before(self, kernel_name):
        return None

    def after(self, kernel_name, results):
        pass

    def grad(self, name, g):
        self.grads[name] = g


def local_step(x, p, tgt, sched, sm):
    wts = sched.wts
    rows_full = lambda w: w.reshape(N_SHARD * w.shape[1], w.shape[2])

    def carried(kernel_name, fn, *args):
        outs, results = fn(*args, comm=sched.before(kernel_name))
        sched.after(kernel_name, results)
        return outs

    lb, lb_vjp = jax.vjp(_lower_bound, sm["hg_lower_bound"])
    s5_names = ("s5_lam_re", "s5_lam_im", "s5_log_dt", "s5_b_re", "s5_b_im", "s5_c_re", "s5_c_im")
    (lam_bar, bmat, cmat), s5_vjp = jax.vjp(s5_prep, *[sm[k] for k in s5_names])
    pw_r, pw_i = _lam_powers(lam_bar)
    bmat_b = bmat.astype(BF16)
    cmat_b = cmat.astype(BF16)
    bmat_t = jnp.swapaxes(bmat, -1, -2).astype(BF16)
    cmat_t = jnp.swapaxes(cmat, -1, -2).astype(BF16)

    h1, a1, b1 = carried("ffn1_fwd", ffn_fwd, x, sm["ffn1_norm"], wts["ffn1_w_gate"], wts["ffn1_w_up"],
                         wts["ffn1_w_down"], "ffn1_fwd")
    s5in, q, f, v, og, ga, gb = inproj_fwd(h1, sm["mix_norm"], wts["w_in"])
    ys5, xp = carried("s5_fwd", s5_fwd, s5in, _scan_tables(pw_r, pw_i, False), bmat_b, cmat_b, sm["s5_d"])
    o, states = hgrn_fwd(q, f, v, lb)
    who = rows_full(wts["hg_w_out"])
    wmo = rows_full(wts["w_merge_out"])
    h2 = merge_fwd(h1, ys5, o, og, ga, gb, wts["s5_glu_val"], wts["s5_glu_gate"], sm["hg_out_norm"], who, wmo)
    (h3, a2, b2), _ = ffn_fwd(h2, sm["ffn2_norm"], wts["ffn2_w_gate"], wts["ffn2_w_up"], wts["ffn2_w_down"], "ffn2_fwd")
    loss, dh3, d_ple_norm, d_final_norm, npb, dlgb, dppb = head_fwd_bwd(
        h3, p, tgt, sm["ple_norm"], rows_full(wts["ple_w_gate"]), wts["ple_w_proj"], sm["final_norm"])

    gs = {"ple_norm": d_ple_norm, "final_norm": d_final_norm}
    sched.grad("ple_w_gate", tn_matmul(npb, dlgb, "g_ple_w_gate", "rows"))
    sched.grad("ple_w_proj", tn_matmul(p, dppb, "g_ple_w_proj", "cols"))

    (dh2, gs["ffn2_norm"], n2b, dhb2, da2, db2, s2), _ = ffn_bwd(
        dh3, h2, a2, b2, sm["ffn2_norm"], wts["ffn2_w_gate"], wts["ffn2_w_up"], wts["ffn2_w_down"], "ffn2_bwd")
    sched.grad("ffn2_w_gate", tn_matmul(da2, n2b, "g_ffn2_w_gate", "rows"))
    sched.grad("ffn2_w_up", tn_matmul(db2, n2b, "g_ffn2_w_up", "rows"))
    sched.grad("ffn2_w_down", tn_matmul(s2, dhb2, "g_ffn2_w_down", "rows"))

    dys5, do, dog, dga, dgb, gs["hg_out_norm"], mixb, dh2b, ysb, dvab, dvtb, o2b, dybb = carried(
        "merge_bwd", merge_bwd,
        dh2, ys5, o, og, ga, gb, wts["s5_glu_val"], wts["s5_glu_gate"], sm["hg_out_norm"], who, wmo)
    sched.grad("w_merge_out", tn_matmul(mixb, dh2b, "g_w_merge_out", "rows"))
    sched.grad("s5_glu_val", tn_matmul(ysb, dvab, "g_s5_glu_val", "cols"))
    sched.grad("s5_glu_gate", tn_matmul(ysb, dvtb, "g_s5_glu_gate", "cols"))
    sched.grad("hg_w_out", tn_matmul(o2b, dybb, "g_hg_w_out", "rows"))

    dq, df, dv, dlb = carried("hgrn_bwd", hgrn_bwd, do, q, f, v, lb, states)
    (gs["hg_lower_bound"],) = lb_vjp(dlb)
    du, dbmat, dcmat, dlam8, gs["s5_d"] = carried(
        "s5_bwd", s5_bwd,
        dys5, s5in, xp, _scan_tables(pw_r, pw_i, True), bmat_b, bmat_t, cmat_t, sm["s5_d"])
    for k, g in zip(s5_names, s5_vjp((jnp.sum(dlam8, axis=1), dbmat, dcmat))):
        gs[k] = g

    dh1, gs["mix_norm"], nmb, dprojb = carried(
        "inproj_bwd", inproj_bwd, dh2, h1, sm["mix_norm"], wts["w_in"], (du, dq, df, dv, dog, dga, dgb))
    sched.grad("w_in", tn_matmul(nmb, dprojb, "g_w_in", "cols"))

    dx, gs["ffn1_norm"], n1b, dhb1, da1, db1, s1 = carried(
        "ffn1_bwd", ffn_bwd,
        dh1, x, a1, b1, sm["ffn1_norm"], wts["ffn1_w_gate"], wts["ffn1_w_up"], wts["ffn1_w_down"], "ffn1_bwd")
    sched.grad("ffn1_w_gate", tn_matmul(da1, n1b, "g_ffn1_w_gate", "rows"))
    sched.grad("ffn1_w_up", tn_matmul(db1, n1b, "g_ffn1_w_up", "rows"))
    sched.grad("ffn1_w_down", tn_matmul(s1, dhb1, "g_ffn1_w_down", "rows"))
    return loss, dx, gs


MESH = pl.DeviceIdType.MESH
ANY = pl.BlockSpec(memory_space=pl.ANY)


def _place():
    x, y, c = lax.axis_index("x"), lax.axis_index("y"), lax.axis_index("c")
    return x, y, c


def _remote(src, dst, ssem, rsem, dev):
    return pltpu.make_async_remote_copy(src_ref=src, dst_ref=dst, send_sem=ssem, recv_sem=rsem,
                                        device_id=dev, device_id_type=MESH)


class Comm:
    def __init__(self, bufs, outs, alias, sems, hooks):
        self.bufs, self.outs, self.alias, self.sems, self.hooks = list(bufs), list(outs), alias, list(sems), hooks


def run_comm(comm, name):
    nb, no = len(comm.bufs), len(comm.outs)

    def body(*refs):
        for which in ("first", "mid", "last"):
            if which in comm.hooks:
                comm.hooks[which](refs[:nb], refs[nb:nb + no], refs[nb + no:])

    return pl.pallas_call(
        body, name=name, out_shape=tuple(comm.outs), in_specs=[ANY] * nb, out_specs=tuple([ANY] * no),
        input_output_aliases=dict(comm.alias), scratch_shapes=comm.sems,
    )(*comm.bufs)


def _carry(body, comm, *, name, steps, out_shape, in_specs, out_specs, args, scratch_shapes=()):
    out_shape, out_specs, scratch_shapes = tuple(out_shape), tuple(out_specs), list(scratch_shapes)
    if comm is None:
        res = pl.pallas_call(body, name=name, grid=(steps,), out_shape=out_shape, in_specs=list(in_specs),
                             out_specs=out_specs, scratch_shapes=scratch_shapes,
                             compiler_params=_cparams(("arbitrary",)))(*args)
        return tuple(res), ()
    n_in, n_out, n_scr = len(args), len(out_shape), len(scratch_shapes)
    nb, no = len(comm.bufs), len(comm.outs)

    def wrapped(*refs):
        ins, cb = refs[:n_in], refs[n_in:n_in + nb]
        o0 = n_in + nb
        outs, co = refs[o0:o0 + n_out], refs[o0 + n_out:o0 + n_out + no]
        s0 = o0 + n_out + no
        scr, cs = refs[s0:s0 + n_scr], refs[s0 + n_scr:]
        step = pl.program_id(0)

        def hook(which, at):
            if which in comm.hooks:
                pl.when(step == at)(lambda: comm.hooks[which](cb, co, cs))

        hook("first", 0)
        hook("mid", steps // 2)
        body(*ins, *outs, *scr)
        hook("last", steps - 1)

    res = pl.pallas_call(
        wrapped, name=name, grid=(steps,), out_shape=out_shape + tuple(comm.outs),
        in_specs=list(in_specs) + [ANY] * nb, out_specs=out_specs + (ANY,) * no,
        scratch_shapes=scratch_shapes + comm.sems,
        input_output_aliases={n_in + i: n_out + o for i, o in comm.alias.items()},
        compiler_params=_cparams(("arbitrary",)),
    )(*args, *comm.bufs)
    return tuple(res[:n_out]), tuple(res[n_out:])


def gather_comm(bufs):
    n = len(bufs)

    def copies(outs, sems):
        s_own, r_own, s_fwd, r_fwd, s_sib, r_sib = sems
        x, y, c = _place()
        me = 2 * x + y
        nbr = ((1 - x, y), (x, 1 - y))
        nbr_id = (2 * (1 - x) + y, 2 * x + (1 - y))
        diag_id = 2 * (1 - x) + (1 - y)
        sib = (x, y, 1 - c)

        def rows(w, q=None):
            r = outs[w].shape[1]
            if q is None:
                return pl.ds(pl.multiple_of(c * (r // 2), 16), r // 2)
            return pl.ds(pl.multiple_of(c * (r // 2) + q * (r // 4), 16), r // 4)

        def own(w, j):
            piece = outs[w].at[me, rows(w)]
            return _remote(piece, piece, s_own.at[w, j], r_own.at[w, j], (nbr[j][0], nbr[j][1], c))

        def from_nbr(w, j):
            piece = outs[w].at[nbr_id[j], rows(w)]
            return _remote(piece, piece, s_own.at[w, j], r_own.at[w, j], (nbr[j][0], nbr[j][1], c))

        def fwd(w, j):
            piece = outs[w].at[nbr_id[j], rows(w, j)]
            return _remote(piece, piece, s_fwd.at[w, j], r_fwd.at[w, j], (nbr[1 - j][0], nbr[1 - j][1], c))

        def from_diag(w, j):
            piece = outs[w].at[diag_id, rows(w, j)]
            return _remote(piece, piece, s_fwd.at[w, j], r_fwd.at[w, j], (nbr[1 - j][0], nbr[1 - j][1], c))

        def to_sib(w, k):
            piece = (outs[w].at[nbr_id[k], rows(w)] if k < 2 else outs[w].at[diag_id, rows(w, k - 2)])
            return _remote(piece, piece, s_sib.at[w, k], r_sib.at[w, k], sib)

        def from_sib(w, k):
            r = outs[w].shape[1]
            if k < 2:
                piece = outs[w].at[nbr_id[k], pl.ds(pl.multiple_of((1 - c) * (r // 2), 16), r // 2)]
            else:
                piece = outs[w].at[diag_id, pl.ds(pl.multiple_of((1 - c) * (r // 2) + (k - 2) * (r // 4), 16), r // 4)]
            return _remote(piece, piece, s_sib.at[w, k], r_sib.at[w, k], sib)

        return own, from_nbr, fwd, from_diag, to_sib, from_sib

    def first(_, outs, sems):
        own = copies(outs, sems)[0]
        for w in range(n):
            own(w, 0).start()
            own(w, 1).start()

    def mid(_, outs, sems):
        _, from_nbr, fwd, _, to_sib, _ = copies(outs, sems)
        for w in range(n):
            for j in range(2):
                from_nbr(w, j).wait_recv()
                fwd(w, j).start()
                to_sib(w, j).start()

    def last(_, outs, sems):
        own, _, fwd, from_diag, to_sib, from_sib = copies(outs, sems)
        for w in range(n):
            for j in range(2):
                from_diag(w, j).wait_recv()
                to_sib(w, 2 + j).start()
        for w in range(n):
            for k in range(4):
                from_sib(w, k).wait_recv()
        for w in range(n):
            for j in range(2):
                own(w, j).wait_send()
                fwd(w, j).wait_send()
            for k in range(4):
                to_sib(w, k).wait_send()

    dma = pltpu.SemaphoreType.DMA
    return Comm(bufs, [jax.ShapeDtypeStruct(b.shape, b.dtype) for b in bufs], {w: w for w in range(n)},
                [dma((n, 2)), dma((n, 2)), dma((n, 2)), dma((n, 2)), dma((n, 4)), dma((n, 4))],
                {"first": first, "mid": mid, "last": last})


def _start_wait(make):
    def first(bufs, outs, sems):
        for cp in make(bufs, outs, sems):
            cp.start()

    def last(bufs, outs, sems):
        for cp in make(bufs, outs, sems):
            cp.wait()

    return {"first": first, "last": last}


def exchange_comm(grads):
    n = len(grads)

    def make(ins, outs, sems):
        x, y, c = _place()
        cps = []
        for w in range(n):
            half = ins[w].shape[1] // 2
            src = ins[w].at[:, pl.ds(pl.multiple_of((1 - c) * half, 8), half), :]
            cps.append(_remote(src, outs[w], sems[0].at[w], sems[1].at[w], (x, y, 1 - c)))
        return cps

    dma = pltpu.SemaphoreType.DMA
    return Comm(grads, [jax.ShapeDtypeStruct((N_SHARD, g.shape[1] // 2, g.shape[2]), g.dtype) for g in grads],
                {}, [dma((n,)), dma((n,))], _start_wait(make))


def scatter_comm(sums):
    n = len(sums)

    def make(ins, outs, sems):
        x, y, c = _place()
        chips = ((1 - x, y), (x, 1 - y), (1 - x, 1 - y))
        return [_remote(ins[w].at[2 * ch[0] + ch[1]], outs[w].at[j], sems[0].at[w, j], sems[1].at[w, j],
                        (ch[0], ch[1], c))
                for w in range(n) for j, ch in enumerate(chips)]

    dma = pltpu.SemaphoreType.DMA
    return Comm(sums, [jax.ShapeDtypeStruct((3,) + s.shape[1:], s.dtype) for s in sums],
                {}, [dma((n, 3)), dma((n, 3))], _start_wait(make))


def join_comm(shards):
    n = len(shards)

    def make(_, outs, sems):
        x, y, c = _place()
        cps = []
        for w in range(n):
            half = outs[w].shape[0] // 2
            mine = outs[w].at[pl.ds(pl.multiple_of(c * half, 8), half), :]
            cps.append(_remote(mine, mine, sems[0].at[w], sems[1].at[w], (x, y, 1 - c)))
        return cps

    dma = pltpu.SemaphoreType.DMA
    return Comm(shards, [jax.ShapeDtypeStruct(s.shape, s.dtype) for s in shards], {w: w for w in range(n)},
                [dma((n,)), dma((n,))], _start_wait(make))


def allreduce_small(vec):
    half = vec.shape[0] // 2

    def body(v_ref, o_ref, pair, chips_buf, s1, r1, s2, r2, s3, r3):
        x, y, c = _place()
        chip = 2 * x + y
        sib = (x, y, 1 - c)
        mine = pl.ds(pl.multiple_of(c * half, 8), half)
        other = pl.ds(pl.multiple_of((1 - c) * half, 8), half)
        to_sib = _remote(v_ref.at[other], pair, s1, r1, sib)
        to_sib.start()
        to_sib.wait()
        chips_buf[chip] = v_ref[mine, :] + pair[...]
        sends = [_remote(chips_buf.at[chip], chips_buf.at[chip], s2.at[j], r2.at[j], (ch[0], ch[1], c))
                 for j, ch in enumerate(((1 - x, y), (x, 1 - y), (1 - x, 1 - y)))]
        for cp in sends:
            cp.start()
        for cp in sends:
            cp.wait()
        o_ref[mine, :] = (chips_buf[0] + chips_buf[1]) + (chips_buf[2] + chips_buf[3])
        back = _remote(o_ref.at[mine], o_ref.at[mine], s3, r3, sib)
        back.start()
        back.wait()

    dma = pltpu.SemaphoreType.DMA
    return pl.pallas_call(
        body, name="allreduce_small",
        out_shape=jax.ShapeDtypeStruct(vec.shape, F32),
        in_specs=[pl.BlockSpec(memory_space=pltpu.VMEM)],
        out_specs=pl.BlockSpec(memory_space=pltpu.VMEM),
        scratch_shapes=[pltpu.VMEM((half, 128), F32), pltpu.VMEM((N_SHARD, half, 128), F32),
                        dma, dma, dma((3,)), dma((3,)), dma, dma],
        compiler_params=pltpu.CompilerParams(vmem_limit_bytes=VMEM_LIMIT),
    )(vec)


ROW_TILE = 128


def add_own_half(place, g, recv, name):
    _, r, cc = g.shape
    half = r // 2
    nb = half // ROW_TILE

    def body(p_ref, g_ref, r_ref, o_ref, ob_ref):
        s = g_ref[...] + r_ref[...]
        ob_ref[...] = s.astype(BF16)

        @pl.when(pl.program_id(1) == p_ref[0])
        def _():
            o_ref[...] = s

    blk = (None, ROW_TILE, cc)
    return pl.pallas_call(
        body, name=name,
        grid_spec=pltpu.PrefetchScalarGridSpec(
            num_scalar_prefetch=1, grid=(nb, N_SHARD),
            in_specs=[pl.BlockSpec(blk, lambda i, s, p_ref: (s, p_ref[1] * nb + i, 0)),
                      pl.BlockSpec(blk, lambda i, s, p_ref: (s, i, 0))],
            out_specs=(pl.BlockSpec((ROW_TILE, cc), lambda i, s, p_ref: (i, 0)),
                       pl.BlockSpec(blk, lambda i, s, p_ref: (s, i, 0)))),
        out_shape=(jax.ShapeDtypeStruct((half, cc), F32),
                   jax.ShapeDtypeStruct((N_SHARD, half, cc), BF16)),
        compiler_params=_cparams(("arbitrary", "arbitrary")),
    )(place, g, recv)


def add_chip_sums(place, own, recv, name):
    half, cc = own.shape
    nb = half // ROW_TILE

    def body(s_ref, o_ref, r_ref, out_ref):
        del s_ref
        acc = o_ref[...] + r_ref[0].astype(F32)
        acc = acc + r_ref[1].astype(F32)
        out_ref[...] = acc + r_ref[2].astype(F32)

    return pl.pallas_call(
        body, name=name,
        grid_spec=pltpu.PrefetchScalarGridSpec(
            num_scalar_prefetch=1, grid=(nb,),
            in_specs=[pl.BlockSpec((ROW_TILE, cc), lambda i, s_ref: (i, 0)),
                      pl.BlockSpec((3, ROW_TILE, cc), lambda i, s_ref: (0, i, 0))],
            out_specs=pl.BlockSpec((ROW_TILE, cc), lambda i, s_ref: (s_ref[1] * nb + i, 0))),
        out_shape=jax.ShapeDtypeStruct((2 * half, cc), F32),
        compiler_params=_cparams(("arbitrary",)),
    )(place, own, recv)


def adamw(w, m, v, g, name, copy_g=False):
    r, cc = w.shape
    tr = next(t for t in (256, 352, r) if r % t == 0)
    bc1 = 1.0 / (1.0 - ADAM_B1 ** ADAM_STEP)
    bc2 = 1.0 / (1.0 - ADAM_B2 ** ADAM_STEP)

    def body(w_ref, m_ref, v_ref, g_ref, d_ref, mo_ref, vo_ref, *go_ref):
        gv = g_ref[...]
        mn = ADAM_B1 * m_ref[...] + (1.0 - ADAM_B1) * gv
        vn = ADAM_B2 * v_ref[...] + (1.0 - ADAM_B2) * (gv * gv)
        mo_ref[...] = mn
        vo_ref[...] = vn
        d_ref[...] = -ADAM_LR * ((mn * bc1) / (jnp.sqrt(vn * bc2) + ADAM_EPS) + ADAM_WD * w_ref[...])
        if copy_g:
            go_ref[0][...] = gv

    blk = pl.BlockSpec((tr, cc), lambda i: (i, 0))
    shp = jax.ShapeDtypeStruct((r, cc), F32)
    nout = 4 if copy_g else 3
    return pl.pallas_call(
        body, name=name, grid=(r // tr,),
        out_shape=(shp,) * nout, in_specs=[blk] * 4, out_specs=(blk,) * nout,
        compiler_params=_cparams(("arbitrary",)),
    )(w, m, v, g)


GATHER_FIRST = ("ffn1_w_gate", "ffn1_w_up", "ffn1_w_down")
GATHER_ON = {"ffn1_fwd": ("w_in", "s5_glu_val", "s5_glu_gate", "hg_w_out", "w_merge_out"),
             "s5_fwd": ("ffn2_w_gate", "ffn2_w_up", "ffn2_w_down", "ple_w_gate", "ple_w_proj")}
REDUCE = ((("ple_w_gate", "ple_w_proj", "ffn2_w_gate", "ffn2_w_up", "ffn2_w_down"), "merge_bwd", "hgrn_bwd"),
          (("w_merge_out", "s5_glu_val", "s5_glu_gate", "hg_w_out"), "s5_bwd", "inproj_bwd"),
          (("w_in",), None, "ffn1_bwd"),
          (("ffn1_w_gate", "ffn1_w_up", "ffn1_w_down"), None, None))


class DistSchedule(Schedule):
    def __init__(self, bufs, chip, core):
        first = run_comm(gather_comm([bufs[k] for k in GATHER_FIRST]), "gather_ffn1")
        super().__init__(zip(GATHER_FIRST, first))
        self.bufs = bufs
        self.place = jnp.stack([chip, core])
        self.sums, self.halves = {}, {}

    def _exchange(self, names):
        return exchange_comm([self.grads[k] for k in names])

    def _scatter(self, names):
        return scatter_comm([self.sums[k][1] for k in names])

    def _pair_sums(self, names, recv):
        for k, r in zip(names, recv):
            self.sums[k] = add_own_half(self.place, self.grads[k], r, "pair_sum_" + k)

    def _chip_sums(self, names, recv):
        for k, r in zip(names, recv):
            self.halves[k] = add_chip_sums(self.place, self.sums[k][0], r, "chip_sum_" + k)

    def before(self, kernel_name):
        if kernel_name in GATHER_ON:
            return gather_comm([self.bufs[k] for k in GATHER_ON[kernel_name]])
        for names, exchange_on, scatter_on in REDUCE:
            if kernel_name == exchange_on:
                return self._exchange(names)
            if kernel_name == scatter_on:
                if exchange_on is None:
                    self._pair_sums(names, run_comm(self._exchange(names), "exchange_" + names[0]))
                return self._scatter(names)
        return None

    def after(self, kernel_name, results):
        if kernel_name in GATHER_ON:
            self.wts.update(zip(GATHER_ON[kernel_name], results))
        for names, exchange_on, scatter_on in REDUCE:
            if kernel_name == exchange_on:
                self._pair_sums(names, results)
            if kernel_name == scatter_on:
                self._chip_sums(names, results)

    def finish(self):
        for names, exchange_on, scatter_on in REDUCE:
            if scatter_on is None:
                self._pair_sums(names, run_comm(self._exchange(names), "exchange_" + names[0]))
                self._chip_sums(names, run_comm(self._scatter(names), "scatter_" + names[0]))
        return dict(zip(BIG, run_comm(join_comm([self.halves[k] for k in BIG]), "join_halves")))


SMALL = ("ffn1_norm", "mix_norm", "s5_lam_re", "s5_lam_im", "s5_log_dt", "s5_b_re", "s5_b_im", "s5_c_re",
         "s5_c_im", "s5_d", "hg_lower_bound", "hg_out_norm", "ffn2_norm", "ple_norm", "final_norm")
WEIGHTS = ("ffn1_norm", "ffn1_w_gate", "ffn1_w_up", "ffn1_w_down", "mix_norm", "w_in", "s5_lam_re", "s5_lam_im",
           "s5_log_dt", "s5_b_re", "s5_b_im", "s5_c_re", "s5_c_im", "s5_d", "s5_glu_val", "s5_glu_gate",
           "hg_lower_bound", "hg_out_norm", "hg_w_out", "w_merge_out", "ffn2_norm", "ffn2_w_gate", "ffn2_w_up",
           "ffn2_w_down", "ple_norm", "ple_w_gate", "ple_w_proj", "final_norm")


def _as_rows(name, w):
    return jnp.swapaxes(w[0], 0, 1) if name in FFN_T else w[0]


def _from_rows(name, w):
    return (jnp.swapaxes(w, 0, 1) if name in FFN_T else w)[None]


def _gather_buffer(name, w_rows, chip):
    r, c = BIG_SHARD[name]
    shard = jnp.pad(w_rows.astype(BF16), ((0, r - w_rows.shape[0]), (0, 0)))
    return lax.dynamic_update_slice(jnp.zeros((N_SHARD, r, c), BF16), shard[None], (chip, 0, 0))


def _pack(parts):
    flat = jnp.concatenate([jnp.zeros((128,), F32)] + [a.reshape(-1) for a in parts])
    rows = -(-flat.shape[0] // 2048) * 16
    return jnp.pad(flat, (0, rows * 128 - flat.shape[0])).reshape(rows, 128)


def _unpack(vec, likes):
    flat = vec.reshape(-1)
    out, off = [], 128
    for a in likes:
        out.append(flat[off:off + a.size].reshape(a.shape))
        off += a.size
    return out


def _small_view(name, w):
    if name.startswith("s5_") and name != "s5_d":
        return w[0]
    if name == "final_norm":
        return w.reshape(1, D_MODEL)
    return w


def kernel(x, p, ffn1_norm, ffn1_w_gate, ffn1_w_up, ffn1_w_down, mix_norm, w_in, s5_lam_re, s5_lam_im, s5_log_dt, s5_b_re, s5_b_im, s5_c_re, s5_c_im, s5_d, s5_glu_val, s5_glu_gate, hg_lower_bound, hg_out_norm, hg_w_out, w_merge_out, ffn2_norm, ffn2_w_gate, ffn2_w_up, ffn2_w_down, ple_norm, ple_w_gate, ple_w_proj, final_norm, loss_target, m_ffn1_norm, m_ffn1_w_gate, m_ffn1_w_up, m_ffn1_w_down, m_mix_norm, m_w_in, m_s5_lam_re, m_s5_lam_im, m_s5_log_dt, m_s5_b_re, m_s5_b_im, m_s5_c_re, m_s5_c_im, m_s5_d, m_s5_glu_val, m_s5_glu_gate, m_hg_lower_bound, m_hg_out_norm, m_hg_w_out, m_w_merge_out, m_ffn2_norm, m_ffn2_w_gate, m_ffn2_w_up, m_ffn2_w_down, m_ple_norm, m_ple_w_gate, m_ple_w_proj, m_final_norm, v_ffn1_norm, v_ffn1_w_gate, v_ffn1_w_up, v_ffn1_w_down, v_mix_norm, v_w_in, v_s5_lam_re, v_s5_lam_im, v_s5_log_dt, v_s5_b_re, v_s5_b_im, v_s5_c_re, v_s5_c_im, v_s5_d, v_s5_glu_val, v_s5_glu_gate, v_hg_lower_bound, v_hg_out_norm, v_hg_w_out, v_w_merge_out, v_ffn2_norm, v_ffn2_w_gate, v_ffn2_w_up, v_ffn2_w_down, v_ple_norm, v_ple_w_gate, v_ple_w_proj, v_final_norm):
    given = dict(locals())
    wv = {k: given[k] for k in WEIGHTS}
    mv = {k: given["m_" + k] for k in WEIGHTS}
    vv = {k: given["v_" + k] for k in WEIGHTS}

    core = lax.axis_index("c").astype(jnp.int32)
    chip = (2 * lax.axis_index("x") + lax.axis_index("y")).astype(jnp.int32)
    w_rows = {k: _as_rows(k, wv[k]) for k in BIG}
    sched = DistSchedule({k: _gather_buffer(k, w_rows[k], chip) for k in BIG}, chip, core)
    sm = {k: _small_view(k, wv[k]) for k in SMALL}

    loss_blk, dx, gsm = local_step(x[0], p[0, 0], loss_target[0], sched, sm)
    full = sched.finish()

    small_likes = [wv[k] for k in SMALL]
    packed = _pack([gsm[k] for k in SMALL])
    packed = packed.at[0, 0].set(loss_blk[0, 0])
    total = allreduce_small(packed)
    loss = total[0, 0]
    gsmall = dict(zip(SMALL, _unpack(total, small_likes)))

    grads, deltas, new_m, new_v = {}, {}, {}, {}
    for k in BIG:
        padded = full[k].shape != w_rows[k].shape
        res = adamw(w_rows[k], _as_rows(k, mv[k]), _as_rows(k, vv[k]), full[k], "adamw_" + k, copy_g=padded)
        grads[k] = _from_rows(k, res[3] if padded else full[k])
        deltas[k], new_m[k], new_v[k] = (_from_rows(k, a) for a in res[:3])
    sw = _pack([wv[k] for k in SMALL])
    smm = _pack([mv[k] for k in SMALL])
    svv = _pack([vv[k] for k in SMALL])
    sd, smn, svn = adamw(sw, smm, svv, total, "adamw_small")
    for k, d, mn, vn in zip(SMALL, _unpack(sd, small_likes), _unpack(smn, small_likes), _unpack(svn, small_likes)):
        grads[k], deltas[k], new_m[k], new_v[k] = gsmall[k], d, mn, vn

    return (loss, dx[None], *[grads[k] for k in WEIGHTS], *[deltas[k] for k in WEIGHTS],
            *[new_m[k] for k in WEIGHTS], *[new_v[k] for k in WEIGHTS])
```

```python
import math

import jax
import jax.numpy as jnp
from jax import lax
from jax.experimental import pallas as pl
from jax.experimental.pallas import tpu as pltpu

F32 = jnp.float32
BF16 = jnp.bfloat16

D_MODEL = 1024
D_FF = 2816
N_SHARD = 4
FF_SHARD = D_FF // N_SHARD
FF_PAD = 768
NORM_EPS = 1e-6
PLE_DIM = 256

S5_WIDTH = 512
S5_GROUPS = 32
S5_GROUP = 16
S5_STATE = 64
S5_N = S5_GROUPS * S5_STATE
S5_KT = 2

HG_HEADS = 8
HG_E = 128
HG_WIDTH = 1024
CHUNK = 64
IN_COLS = S5_WIDTH + 4 * HG_WIDTH + 2 * D_MODEL
IN_SPLITS = (0, 512, 1536, 2560, 3584, 4608, 5632, 6656)

ADAM_LR = 0.001
ADAM_B1 = 0.9
ADAM_B2 = 0.999
ADAM_EPS = 1e-08
ADAM_WD = 0.01
ADAM_STEP = 10

VMEM_LIMIT = 60 * 1024 * 1024
HIGHEST = lax.Precision.HIGHEST


def _cparams(sem=None, **kw):
    return pltpu.CompilerParams(dimension_semantics=sem, vmem_limit_bytes=VMEM_LIMIT, **kw)


def _const_spec(shape):
    nd = len(shape)
    return pl.BlockSpec(shape, lambda *_: (0,) * nd, pipeline_mode=pl.Buffered(1))


def _dot(a, b):
    return jnp.dot(a, b, preferred_element_type=F32)


def _dot_nt(a, b):
    return lax.dot_general(a, b, (((1,), (1,)), ((), ())), preferred_element_type=F32)


def _dot_tn(a, b):
    return lax.dot_general(a, b, (((0,), (0,)), ((), ())), preferred_element_type=F32)


def _sigmoid(x):
    return 1.0 / (1.0 + jnp.exp(-x))


def _rms_fwd(x, g):
    r = lax.rsqrt(jnp.mean(x * x, axis=-1, keepdims=True) + NORM_EPS)
    return x * r * g, r


def _rms_bwd(x, r, g, dy):
    xh = x * r
    dyg = dy * g
    m = jnp.mean(dyg * xh, axis=-1, keepdims=True)
    return r * (dyg - xh * m), jnp.sum(dy * xh, axis=0, keepdims=True)


def _accum(ref, val, first):
    @pl.when(first)
    def _():
        ref[...] = val

    @pl.when(jnp.logical_not(first))
    def _():
        ref[...] += val


def ffn_fwd(h, gain, wg, wu, wd, name, comm=None, tm=256):
    t = h.shape[0]

    def body(h_ref, g_ref, wg_ref, wu_ref, wd_ref, o_ref, a_ref, b_ref):
        hv = h_ref[...]
        n, _ = _rms_fwd(hv, g_ref[...])
        nb = n.astype(BF16)
        acc = jnp.zeros((tm, D_MODEL), F32)
        for s in range(N_SHARD):
            a = _dot_nt(nb, wg_ref[s])
            b = _dot_nt(nb, wu_ref[s])
            a_ref[s] = a.astype(BF16)
            b_ref[s] = b.astype(BF16)
            sv = (a * _sigmoid(a) * b).astype(BF16)
            acc = acc + _dot(sv, wd_ref[s])
        o_ref[...] = hv + 0.5 * acc

    return _carry(
        body, comm, name=name, steps=t // tm,
        out_shape=(jax.ShapeDtypeStruct((t, D_MODEL), F32),
                   jax.ShapeDtypeStruct((N_SHARD, t, FF_PAD), BF16),
                   jax.ShapeDtypeStruct((N_SHARD, t, FF_PAD), BF16)),
        in_specs=[pl.BlockSpec((tm, D_MODEL), lambda i: (i, 0)),
                  _const_spec((1, D_MODEL)),
                  _const_spec((N_SHARD, FF_PAD, D_MODEL)),
                  _const_spec((N_SHARD, FF_PAD, D_MODEL)),
                  _const_spec((N_SHARD, FF_PAD, D_MODEL))],
        out_specs=(pl.BlockSpec((tm, D_MODEL), lambda i: (i, 0)),
                   pl.BlockSpec((N_SHARD, tm, FF_PAD), lambda i: (0, i, 0)),
                   pl.BlockSpec((N_SHARD, tm, FF_PAD), lambda i: (0, i, 0))),
        args=(h, gain, wg, wu, wd),
    )


def ffn_bwd(dho, h, a, b, gain, wg, wu, wd, name, comm=None, tm=256):
    t = h.shape[0]

    def body(dho_ref, h_ref, a_ref, b_ref, g_ref, wg_ref, wu_ref, wd_ref,
             dh_ref, dg_ref, nb_ref, dhb_ref, da_ref, db_ref, s_ref):
        hv = h_ref[...]
        g = g_ref[...]
        n, r = _rms_fwd(hv, g)
        nb_ref[...] = n.astype(BF16)
        dhalf = (0.5 * dho_ref[...]).astype(BF16)
        dhb_ref[...] = dhalf
        dn = jnp.zeros((tm, D_MODEL), F32)
        for s in range(N_SHARD):
            av = a_ref[s].astype(F32)
            bv = b_ref[s].astype(F32)
            sg = _sigmoid(av)
            sil = av * sg
            s_ref[s] = (sil * bv).astype(BF16)
            ds = _dot_nt(dhalf, wd_ref[s])
            da = (ds * bv * (sg * (1.0 + av * (1.0 - sg)))).astype(BF16)
            db = (ds * sil).astype(BF16)
            da_ref[s] = da
            db_ref[s] = db
            dn = dn + _dot(da, wg_ref[s]) + _dot(db, wu_ref[s])
        dx, dg = _rms_bwd(hv, r, g, dn)
        dh_ref[...] = dho_ref[...] + dx
        _accum(dg_ref, dg, pl.program_id(0) == 0)

    tok = pl.BlockSpec((tm, D_MODEL), lambda i: (i, 0))
    hid = pl.BlockSpec((N_SHARD, tm, FF_PAD), lambda i: (0, i, 0))
    return _carry(
        body, comm, name=name, steps=t // tm,
        out_shape=(jax.ShapeDtypeStruct((t, D_MODEL), F32),
                   jax.ShapeDtypeStruct((1, D_MODEL), F32),
                   jax.ShapeDtypeStruct((t, D_MODEL), BF16),
                   jax.ShapeDtypeStruct((t, D_MODEL), BF16),
                   jax.ShapeDtypeStruct((N_SHARD, t, FF_PAD), BF16),
                   jax.ShapeDtypeStruct((N_SHARD, t, FF_PAD), BF16),
                   jax.ShapeDtypeStruct((N_SHARD, t, FF_PAD), BF16)),
        in_specs=[tok, tok, hid, hid, _const_spec((1, D_MODEL)),
                  _const_spec((N_SHARD, FF_PAD, D_MODEL)),
                  _const_spec((N_SHARD, FF_PAD, D_MODEL)),
                  _const_spec((N_SHARD, FF_PAD, D_MODEL))],
        out_specs=(tok, pl.BlockSpec((1, D_MODEL), lambda i: (0, 0)), tok, tok, hid, hid, hid),
        args=(dho, h, a, b, gain, wg, wu, wd),
    )


TN_VMEM_BUDGET = 44 * 1024 * 1024


def tn_matmul(x, y, name, shard, comm=None):
    x3, y3 = x.ndim == 3, y.ndim == 3
    t = x.shape[-2]
    m = x.shape[-1] // (N_SHARD if (shard == "rows" and not x3) else 1)
    n = y.shape[-1] // (N_SHARD if (shard == "cols" and not y3) else 1)
    per_token = 2 * (m * x.dtype.itemsize + n * y.dtype.itemsize)
    tk = t
    while tk > 512 and tk * per_token + 2 * m * n * 4 > TN_VMEM_BUDGET:
        tk //= 2
    nk = t // tk

    out_shape = jax.ShapeDtypeStruct((N_SHARD, m, n), F32)
    if nk == 1:
        def whole(x_ref, y_ref, o_ref):
            o_ref[...] = _dot_tn(x_ref[...].astype(BF16), y_ref[...].astype(BF16))

        x_one = (pl.BlockSpec((None, t, m), lambda s: (s, 0, 0)) if x3 else
                 pl.BlockSpec((t, m), (lambda s: (0, s)) if shard == "rows" else (lambda s: (0, 0))))
        y_one = (pl.BlockSpec((None, t, n), lambda s: (s, 0, 0)) if y3 else
                 pl.BlockSpec((t, n), (lambda s: (0, s)) if shard == "cols" else (lambda s: (0, 0))))
        return _carry(whole, comm, name=name, steps=N_SHARD, out_shape=(out_shape,), in_specs=[x_one, y_one],
                      out_specs=(pl.BlockSpec((None, m, n), lambda s: (s, 0, 0)),), args=(x, y))
    assert comm is None

    def body(x_ref, y_ref, o_ref):
        _accum(o_ref, _dot_tn(x_ref[...].astype(BF16), y_ref[...].astype(BF16)), pl.program_id(1) == 0)

    if x3:
        x_spec = pl.BlockSpec((None, tk, m), lambda s, k: (s, k, 0))
    elif shard == "rows":
        x_spec = pl.BlockSpec((tk, m), lambda s, k: (k, s))
    else:
        x_spec = pl.BlockSpec((tk, m), lambda s, k: (k, 0))
    if y3:
        y_spec = pl.BlockSpec((None, tk, n), lambda s, k: (s, k, 0))
    elif shard == "cols":
        y_spec = pl.BlockSpec((tk, n), lambda s, k: (k, s))
    else:
        y_spec = pl.BlockSpec((tk, n), lambda s, k: (k, 0))
    res = pl.pallas_call(
        body, name=name, grid=(N_SHARD, nk),
        out_shape=out_shape,
        in_specs=[x_spec, y_spec],
        out_specs=pl.BlockSpec((None, m, n), lambda s, k: (s, 0, 0)),
        compiler_params=_cparams(("arbitrary", "arbitrary")),
    )(x, y)
    return (res,), ()


def inproj_fwd(h, gain, w_in, tm=256):
    t = h.shape[0]
    widths = [IN_SPLITS[j + 1] - IN_SPLITS[j] for j in range(7)]
    sh_cols = IN_COLS // N_SHARD

    def body(h_ref, g_ref, w_ref, *outs):
        n, _ = _rms_fwd(h_ref[...], g_ref[...])
        nb = n.astype(BF16)
        proj = jnp.concatenate([_dot(nb, w_ref[s]) for s in range(N_SHARD)], axis=1)
        for j, o_ref in enumerate(outs):
            o_ref[...] = proj[:, IN_SPLITS[j]:IN_SPLITS[j + 1]]

    return pl.pallas_call(
        body, name="inproj_fwd", grid=(t // tm,),
        out_shape=tuple(jax.ShapeDtypeStruct((t, w), F32) for w in widths),
        in_specs=[pl.BlockSpec((tm, D_MODEL), lambda i: (i, 0)),
                  _const_spec((1, D_MODEL)),
                  _const_spec((N_SHARD, D_MODEL, sh_cols))],
        out_specs=tuple(pl.BlockSpec((tm, w), lambda i: (i, 0)) for w in widths),
        compiler_params=_cparams(("arbitrary",)),
    )(h, gain, w_in)


def inproj_bwd(dres, h, gain, w_in, dparts, comm=None, tm=256):
    t = h.shape[0]
    widths = [IN_SPLITS[j + 1] - IN_SPLITS[j] for j in range(7)]
    sh_cols = IN_COLS // N_SHARD

    def body(dres_ref, h_ref, g_ref, w_ref, d0, d1, d2, d3, d4, d5, d6, dh_ref, dg_ref, nb_ref, dp_ref):
        hv = h_ref[...]
        g = g_ref[...]
        n, r = _rms_fwd(hv, g)
        nb_ref[...] = n.astype(BF16)
        dproj = jnp.concatenate([d[...] for d in (d0, d1, d2, d3, d4, d5, d6)], axis=1).astype(BF16)
        dp_ref[...] = dproj
        dn = jnp.zeros((tm, D_MODEL), F32)
        for s in range(N_SHARD):
            dn = dn + _dot_nt(dproj[:, s * sh_cols:(s + 1) * sh_cols], w_ref[s])
        dx, dg = _rms_bwd(hv, r, g, dn)
        dh_ref[...] = dres_ref[...] + dx
        _accum(dg_ref, dg, pl.program_id(0) == 0)

    tok = pl.BlockSpec((tm, D_MODEL), lambda i: (i, 0))
    return _carry(
        body, comm, name="inproj_bwd", steps=t // tm,
        out_shape=(jax.ShapeDtypeStruct((t, D_MODEL), F32),
                   jax.ShapeDtypeStruct((1, D_MODEL), F32),
                   jax.ShapeDtypeStruct((t, D_MODEL), BF16),
                   jax.ShapeDtypeStruct((t, IN_COLS), BF16)),
        in_specs=[tok, tok, _const_spec((1, D_MODEL)), _const_spec((N_SHARD, D_MODEL, sh_cols))]
                 + [pl.BlockSpec((tm, w), lambda i: (i, 0)) for w in widths],
        out_specs=(tok, pl.BlockSpec((1, D_MODEL), lambda i: (0, 0)), tok,
                   pl.BlockSpec((tm, IN_COLS), lambda i: (i, 0))),
        args=(dres, h, gain, w_in, *dparts),
    )


def s5_prep(lam_re, lam_im, log_dt, b_re, b_im, c_re, c_im):
    dt = jnp.exp(log_dt)[:, None]
    mag = jnp.exp(lam_re * dt)
    lbr = mag * jnp.cos(lam_im * dt)
    lbi = mag * jnp.sin(lam_im * dt)
    den = lam_re * lam_re + lam_im * lam_im
    nr, ni = lbr - 1.0, lbi
    kr = (nr * lam_re + ni * lam_im) / den
    ki = (ni * lam_re - nr * lam_im) / den
    bbr = kr[..., None] * b_re - ki[..., None] * b_im
    bbi = kr[..., None] * b_im + ki[..., None] * b_re
    eye = jnp.eye(16, dtype=F32)

    def bm(bp):
        return jnp.einsum('kgph,gG->kghGp', bp.reshape(S5_KT, 16, S5_STATE, S5_GROUP), eye).reshape(S5_KT, 256, 1024)

    def cm(cp):
        return jnp.einsum('kghp,gG->kgpGh', cp.reshape(S5_KT, 16, S5_GROUP, S5_STATE), eye).reshape(S5_KT, 1024, 256)

    lam_bar = jnp.stack([lbr.reshape(S5_N), lbi.reshape(S5_N)])
    bmat = jnp.stack([bm(bbr), bm(bbi)])
    cmat = jnp.stack([cm(c_re), -cm(c_im)])
    return lam_bar, bmat, cmat


def _lam_powers(lam_bar):
    lr, li = lam_bar[0], lam_bar[1]
    pr, pi = [lr], [li]
    for _ in range(7):
        pr, pi = pr + [pr[-1] * lr - pi[-1] * li], pi + [pr[-1] * li + pi[-1] * lr]
    return jnp.stack(pr), jnp.stack(pi)


SCAN_SHIFTS = ((1, 0), (2, 1), (4, 3))


def _scan_tables(pw_r, pw_i, reverse):
    rows = jnp.arange(8)[:, None]
    planes_r, planes_i = [], []
    for sh, idx in SCAN_SHIFTS:
        keep = (rows < 8 - sh) if reverse else (rows >= sh)
        planes_r.append(jnp.where(keep, pw_r[idx:idx + 1], 0.0))
        planes_i.append(jnp.where(keep, pw_i[idx:idx + 1], 0.0))
    carry = [pw_r[::-1], pw_i[::-1]] if reverse else [pw_r, pw_i]
    return jnp.stack(planes_r + planes_i + carry)


def s5_fwd(u, tab, bmat, cmat, dvec, comm=None, tm=256):
    t = u.shape[0]
    nch = tm // 8

    def body(u_ref, tab_ref, b_ref, c_ref, d_ref, y_ref, xp_ref, x_scr, carry):
        @pl.when(pl.program_id(0) == 0)
        def _():
            carry[...] = jnp.zeros_like(carry)

        uv = u_ref[...]
        ub = uv.astype(BF16)
        for part in range(2):
            for kt in range(S5_KT):
                x_scr[:, pl.ds(part * S5_N + kt * 1024, 1024)] = _dot(ub[:, kt * 256:(kt + 1) * 256], b_ref[part, kt])
        row = lax.broadcasted_iota(jnp.int32, (8, S5_N), 0)

        def chunk(i, c):
            cr, ci = c
            r0 = pl.multiple_of(i * 8, 8)
            xr = x_scr[pl.ds(r0, 8), pl.ds(0, S5_N)]
            xi = x_scr[pl.ds(r0, 8), pl.ds(S5_N, S5_N)]
            for lvl, (sh, _) in enumerate(SCAN_SHIFTS):
                sr = pltpu.roll(xr, sh, 0)
                si = pltpu.roll(xi, sh, 0)
                lr = tab_ref[lvl]
                li = tab_ref[3 + lvl]
                xr, xi = xr + lr * sr - li * si, xi + lr * si + li * sr
            pwr = tab_ref[6]
            pwi = tab_ref[7]
            xr, xi = xr + pwr * cr - pwi * ci, xi + pwr * ci + pwi * cr
            x_scr[pl.ds(r0, 8), pl.ds(0, S5_N)] = xr
            x_scr[pl.ds(r0, 8), pl.ds(S5_N, S5_N)] = xi
            xp_ref[pl.ds(r0, 8), pl.ds(0, S5_N)] = jnp.where(row == 0, cr, pltpu.roll(xr, 1, 0))
            xp_ref[pl.ds(r0, 8), pl.ds(S5_N, S5_N)] = jnp.where(row == 0, ci, pltpu.roll(xi, 1, 0))
            return xr[7:8, :], xi[7:8, :]

        cr, ci = lax.fori_loop(0, nch, chunk, (carry[0:1, :], carry[1:2, :]))
        carry[0:1, :] = cr
        carry[1:2, :] = ci
        for kt in range(S5_KT):
            acc = jnp.zeros((tm, 256), F32)
            for part in range(2):
                acc = acc + _dot(x_scr[:, pl.ds(part * S5_N + kt * 1024, 1024)].astype(BF16), c_ref[part, kt])
            y_ref[:, pl.ds(kt * 256, 256)] = acc + d_ref[:, pl.ds(kt * 256, 256)] * uv[:, kt * 256:(kt + 1) * 256]

    return _carry(
        body, comm, name="s5_fwd", steps=t // tm,
        out_shape=(jax.ShapeDtypeStruct((t, S5_WIDTH), F32),
                   jax.ShapeDtypeStruct((t, 2 * S5_N), F32)),
        in_specs=[pl.BlockSpec((tm, S5_WIDTH), lambda i: (i, 0)),
                  _const_spec((8, 8, S5_N)),
                  _const_spec((2, S5_KT, 256, 1024)), _const_spec((2, S5_KT, 1024, 256)),
                  _const_spec((1, S5_WIDTH))],
        out_specs=(pl.BlockSpec((tm, S5_WIDTH), lambda i: (i, 0)),
                   pl.BlockSpec((tm, 2 * S5_N), lambda i: (i, 0))),
        scratch_shapes=[pltpu.VMEM((tm, 2 * S5_N), F32), pltpu.VMEM((8, S5_N), F32)],
        args=(u, tab, bmat, cmat, dvec),
    )


def s5_bwd(dy, u, xp, tab, bmat, bmat_t, cmat_t, dvec, comm=None, tm=256):
    t = u.shape[0]
    nt = t // tm
    nch = tm // 8

    def body(dy_ref, u_ref, xp_ref, tab_ref, b_ref, bt_ref, ct_ref, d_ref,
             du_ref, db_ref, dc_ref, dl_ref, dd_ref, g_scr, x_scr, carry):
        first = pl.program_id(0) == 0

        @pl.when(first)
        def _():
            carry[...] = jnp.zeros_like(carry)
            dl_ref[...] = jnp.zeros_like(dl_ref)

        dyv = dy_ref[...]
        uv = u_ref[...]
        dyb = dyv.astype(BF16)
        ub = uv.astype(BF16)
        lr1 = tab_ref[6, 7:8, :]
        li1 = tab_ref[7, 7:8, :]
        for kt in range(S5_KT):
            cols = pl.ds(kt * 1024, 1024)
            colsi = pl.ds(S5_N + kt * 1024, 1024)
            g_scr[:, cols] = _dot(dyb[:, kt * 256:(kt + 1) * 256], ct_ref[0, kt])
            g_scr[:, colsi] = _dot(dyb[:, kt * 256:(kt + 1) * 256], ct_ref[1, kt])
            bur = _dot(ub[:, kt * 256:(kt + 1) * 256], b_ref[0, kt])
            bui = _dot(ub[:, kt * 256:(kt + 1) * 256], b_ref[1, kt])
            xpr = xp_ref[:, cols]
            xpi = xp_ref[:, colsi]
            lrk = lr1[:, kt * 1024:(kt + 1) * 1024]
            lik = li1[:, kt * 1024:(kt + 1) * 1024]
            x_scr[:, cols] = lrk * xpr - lik * xpi + bur
            x_scr[:, colsi] = lrk * xpi + lik * xpr + bui

        def chunk(j, c):
            cr, ci = c
            r0 = pl.multiple_of((nch - 1 - j) * 8, 8)
            gr = g_scr[pl.ds(r0, 8), pl.ds(0, S5_N)]
            gi = g_scr[pl.ds(r0, 8), pl.ds(S5_N, S5_N)]
            for lvl, (sh, _) in enumerate(SCAN_SHIFTS):
                sr = pltpu.roll(gr, 8 - sh, 0)
                si = pltpu.roll(gi, 8 - sh, 0)
                lr = tab_ref[lvl]
                li = tab_ref[3 + lvl]
                gr, gi = gr + lr * sr + li * si, gi + lr * si - li * sr
            pvr = tab_ref[6]
            pvi = tab_ref[7]
            gr, gi = gr + pvr * cr + pvi * ci, gi + pvr * ci - pvi * cr
            g_scr[pl.ds(r0, 8), pl.ds(0, S5_N)] = gr
            g_scr[pl.ds(r0, 8), pl.ds(S5_N, S5_N)] = gi
            xpr = xp_ref[pl.ds(r0, 8), pl.ds(0, S5_N)]
            xpi = xp_ref[pl.ds(r0, 8), pl.ds(S5_N, S5_N)]
            dl_ref[0] += gr * xpr + gi * xpi
            dl_ref[1] += gi * xpr - gr * xpi
            return gr[0:1, :], gi[0:1, :]

        cr, ci = lax.fori_loop(0, nch, chunk, (carry[0:1, :], carry[1:2, :]))
        carry[0:1, :] = cr
        carry[1:2, :] = ci

        for kt in range(S5_KT):
            du = jnp.zeros((tm, 256), F32)
            ukt = ub[:, kt * 256:(kt + 1) * 256]
            dykt = dyb[:, kt * 256:(kt + 1) * 256]
            for part in range(2):
                gb = g_scr[:, pl.ds(part * S5_N + kt * 1024, 1024)].astype(BF16)
                xb = x_scr[:, pl.ds(part * S5_N + kt * 1024, 1024)].astype(BF16)
                du = du + _dot(gb, bt_ref[part, kt])
                dbv = _dot_tn(ukt, gb)
                dcv = _dot_tn(xb, dykt)

                @pl.when(first)
                def _():
                    db_ref[part, kt] = dbv
                    dc_ref[part, kt] = dcv

                @pl.when(jnp.logical_not(first))
                def _():
                    db_ref[part, kt] += dbv
                    dc_ref[part, kt] += dcv
            du_ref[:, pl.ds(kt * 256, 256)] = du + d_ref[:, pl.ds(kt * 256, 256)] * dyv[:, kt * 256:(kt + 1) * 256]
        _accum(dd_ref, jnp.sum(dyv * uv, axis=0, keepdims=True), first)

    rev = lambda i: (nt - 1 - i, 0)
    return _carry(
        body, comm, name="s5_bwd", steps=nt,
        out_shape=(jax.ShapeDtypeStruct((t, S5_WIDTH), F32),
                   jax.ShapeDtypeStruct((2, S5_KT, 256, 1024), F32),
                   jax.ShapeDtypeStruct((2, S5_KT, 1024, 256), F32),
                   jax.ShapeDtypeStruct((2, 8, S5_N), F32),
                   jax.ShapeDtypeStruct((1, S5_WIDTH), F32)),
        in_specs=[pl.BlockSpec((tm, S5_WIDTH), rev), pl.BlockSpec((tm, S5_WIDTH), rev),
                  pl.BlockSpec((tm, 2 * S5_N), rev),
                  _const_spec((8, 8, S5_N)),
                  _const_spec((2, S5_KT, 256, 1024)), _const_spec((2, S5_KT, 1024, 256)),
                  _const_spec((2, S5_KT, 256, 1024)), _const_spec((1, S5_WIDTH))],
        out_specs=(pl.BlockSpec((tm, S5_WIDTH), rev),
                   pl.BlockSpec((2, S5_KT, 256, 1024), lambda i: (0, 0, 0, 0)),
                   pl.BlockSpec((2, S5_KT, 1024, 256), lambda i: (0, 0, 0, 0)),
                   pl.BlockSpec((2, 8, S5_N), lambda i: (0, 0, 0)),
                   pl.BlockSpec((1, S5_WIDTH), lambda i: (0, 0))),
        scratch_shapes=[pltpu.VMEM((tm, 2 * S5_N), F32), pltpu.VMEM((tm, 2 * S5_N), F32),
                        pltpu.VMEM((8, S5_N), F32)],
        args=(dy, u, xp, tab, bmat, bmat_t, cmat_t, dvec),
    )


def _hg_gates(z, lb):
    sg = _sigmoid(z)
    sgn = _sigmoid(-z)
    fg = lb + (1.0 - lb) * sg
    return sg, sgn, fg, jnp.log(fg), (1.0 - lb) * sgn


def _hg_decays(g, tril):
    gc = jnp.dot(tril, g, precision=HIGHEST, preferred_element_type=F32)
    mid = gc[CHUNK // 2 - 1:CHUNK // 2, :]
    last = gc[CHUNK - 1:CHUNK, :]
    return jnp.exp(gc), jnp.exp(gc - mid), jnp.exp(mid - gc), jnp.exp(last - gc), jnp.exp(last)


def _split_bf16(x):
    hi = x.astype(BF16)
    return hi, (x - hi.astype(F32)).astype(BF16)


def _hg_scores(qt, qlo, kt, klo, sl, causal):
    a = _dot_nt(qt[:, sl], kt[:, sl]) + _dot_nt(qt[:, sl], klo[:, sl]) + _dot_nt(qlo[:, sl], kt[:, sl])
    return jnp.where(causal, a, 0.0).astype(BF16)


def hgrn_fwd(q, f, v, lb):
    t = q.shape[0]
    nc = t // CHUNK
    scale = HG_E ** -0.5

    def body(q_ref, f_ref, v_ref, lb_ref, o_ref, st_ref, state):
        @pl.when(pl.program_id(0) == 0)
        def _():
            state[...] = jnp.zeros_like(state)

        ri = lax.broadcasted_iota(jnp.int32, (CHUNK, CHUNK), 0)
        ci = lax.broadcasted_iota(jnp.int32, (CHUNK, CHUNK), 1)
        causal = ri >= ci
        tril = causal.astype(F32)
        _, _, _, g, k = _hg_gates(f_ref[...], lb_ref[...])
        eg, eq, ek, ed, el = _hg_decays(g, tril)
        qs = q_ref[...] * scale
        qg = (qs * eg).astype(BF16)
        qt, qlo = _split_bf16(qs * eq)
        kt, klo = _split_bf16(k * ek)
        kd = (k * ed).astype(BF16)
        vb = v_ref[...].astype(BF16)
        for h in range(HG_HEADS):
            sl = slice(h * HG_E, (h + 1) * HG_E)
            st = state[h]
            a = _hg_scores(qt, qlo, kt, klo, sl, causal)
            o_ref[:, sl] = _dot(a, vb[:, sl]) + _dot_nt(qg[:, sl], st.astype(BF16))
            st_new = st * el[:, sl] + _dot_tn(vb[:, sl], kd[:, sl])
            state[h] = st_new
            st_ref[0, h] = st_new

    tok = pl.BlockSpec((CHUNK, HG_WIDTH), lambda i: (i, 0))
    return pl.pallas_call(
        body, name="hgrn_fwd", grid=(nc,),
        out_shape=(jax.ShapeDtypeStruct((t, HG_WIDTH), F32),
                   jax.ShapeDtypeStruct((nc, HG_HEADS, HG_E, HG_E), F32)),
        in_specs=[tok, tok, tok, _const_spec((1, HG_WIDTH))],
        out_specs=(tok, pl.BlockSpec((1, HG_HEADS, HG_E, HG_E), lambda i: (i, 0, 0, 0))),
        scratch_shapes=[pltpu.VMEM((HG_HEADS, HG_E, HG_E), F32)],
        compiler_params=_cparams(("arbitrary",)),
    )(q, f, v, lb)


def hgrn_bwd(do, q, f, v, lb, states, comm=None):
    t = q.shape[0]
    nc = t // CHUNK
    scale = HG_E ** -0.5

    def body(do_ref, q_ref, f_ref, v_ref, lb_ref, s0_ref, dq_ref, df_ref, dv_ref, dlb_ref, dstate):
        first = pl.program_id(0) == 0
        c_idx = nc - 1 - pl.program_id(0)

        @pl.when(first)
        def _():
            dstate[...] = jnp.zeros_like(dstate)

        ri = lax.broadcasted_iota(jnp.int32, (CHUNK, CHUNK), 0)
        ci = lax.broadcasted_iota(jnp.int32, (CHUNK, CHUNK), 1)
        causal = ri >= ci
        tril = causal.astype(F32)
        triu = (ri <= ci).astype(F32)
        lb = lb_ref[...]
        sg, sgn, fg, g, k = _hg_gates(f_ref[...], lb)
        eg, eq, ek, ed, el = _hg_decays(g, tril)
        qs = q_ref[...] * scale
        qg = (qs * eg).astype(BF16)
        qt, qlo = _split_bf16(qs * eq)
        kt, klo = _split_bf16(k * ek)
        kd = (k * ed).astype(BF16)
        vb = v_ref[...].astype(BF16)
        dob = do_ref[...].astype(BF16)
        has_prev = jnp.where(c_idx > 0, 1.0, 0.0)
        dqs_l, dk_l, dgc_l, dgl_l = [], [], [], []
        for h in range(HG_HEADS):
            sl = slice(h * HG_E, (h + 1) * HG_E)
            s0 = s0_ref[0, h] * has_prev
            ds1 = dstate[h]
            ds1b = ds1.astype(BF16)
            a = _hg_scores(qt, qlo, kt, klo, sl, causal)
            da = jnp.where(causal, _dot_nt(dob[:, sl], vb[:, sl]), 0.0).astype(BF16)
            dv_ref[:, sl] = _dot_tn(a, dob[:, sl]) + _dot_nt(kd[:, sl], ds1b)
            dkd = _dot(vb[:, sl], ds1b)
            dqt = _dot(da, kt[:, sl])
            dkt = _dot_tn(da, qt[:, sl])
            dqg = _dot(dob[:, sl], s0.astype(BF16))
            dqs_l.append(dqt * eq[:, sl] + dqg * eg[:, sl])
            dk_l.append(dkt * ek[:, sl] + dkd * ed[:, sl])
            kd_dkd = kd[:, sl].astype(F32) * dkd
            dgc_l.append(qt[:, sl].astype(F32) * dqt - kt[:, sl].astype(F32) * dkt
                         + qg[:, sl].astype(F32) * dqg - kd_dkd)
            dgl_l.append(el[:, sl] * jnp.sum(ds1 * s0, axis=0, keepdims=True)
                         + jnp.sum(kd_dkd, axis=0, keepdims=True))
            dstate[h] = ds1 * el[:, sl] + _dot_tn(dob[:, sl], qg[:, sl])
        dqs = jnp.concatenate(dqs_l, axis=1)
        dk = jnp.concatenate(dk_l, axis=1)
        dgl = jnp.concatenate(dgl_l, axis=1)
        dq_ref[...] = dqs * scale
        rowc = lax.broadcasted_iota(jnp.int32, (CHUNK, HG_WIDTH), 0)
        dgc = jnp.concatenate(dgc_l, axis=1) + jnp.where(rowc == CHUNK - 1, dgl, 0.0)
        dg = jnp.dot(triu, dgc, precision=HIGHEST, preferred_element_type=F32)
        w = dg / fg - dk
        df_ref[...] = w * (1.0 - lb) * sg * sgn
        _accum(dlb_ref, jnp.sum(w * sgn, axis=0, keepdims=True), first)

    rev = lambda i: (nc - 1 - i, 0)
    tok = pl.BlockSpec((CHUNK, HG_WIDTH), rev)
    return _carry(
        body, comm, name="hgrn_bwd", steps=nc,
        out_shape=(jax.ShapeDtypeStruct((t, HG_WIDTH), F32),
                   jax.ShapeDtypeStruct((t, HG_WIDTH), F32),
                   jax.ShapeDtypeStruct((t, HG_WIDTH), F32),
                   jax.ShapeDtypeStruct((1, HG_WIDTH), F32)),
        in_specs=[tok, tok, tok, tok, _const_spec((1, HG_WIDTH)),
                  pl.BlockSpec((1, HG_HEADS, HG_E, HG_E), lambda i: (jnp.maximum(nc - 2 - i, 0), 0, 0, 0))],
        out_specs=(tok, tok, tok, pl.BlockSpec((1, HG_WIDTH), lambda i: (0, 0))),
        scratch_shapes=[pltpu.VMEM((HG_HEADS, HG_E, HG_E), F32)],
        args=(do, q, f, v, lb, states),
    )


GELU_C = math.sqrt(2.0 / math.pi)


def _gelu(x):
    th = jnp.tanh(GELU_C * (x + 0.044715 * x * x * x))
    return 0.5 * x * (1.0 + th), th


def _merge_core(ys5, o, og, ga, gb, wv_ref, wt_ref, ghg, who_ref):
    ys, th = _gelu(ys5)
    ysb = ys.astype(BF16)
    va = jnp.concatenate([_dot(ysb, wv_ref[s]) for s in range(N_SHARD)], axis=1)
    vt = jnp.concatenate([_dot(ysb, wt_ref[s]) for s in range(N_SHARD)], axis=1)
    svt = _sigmoid(vt)
    ya = va * svt
    rs, ons = [], []
    for h in range(HG_HEADS):
        oh = o[:, h * HG_E:(h + 1) * HG_E]
        r = lax.rsqrt(jnp.mean(oh * oh, axis=-1, keepdims=True) + NORM_EPS)
        rs.append(r)
        ons.append(oh * r)
    on = jnp.concatenate(ons, axis=1)
    sgo = _sigmoid(og)
    o2 = on * ghg * (og * sgo)
    o2b = o2.astype(BF16)
    yb = _dot(o2b, who_ref[...])
    sa = _sigmoid(ga)
    sb = _sigmoid(gb)
    mixed = sa * ya + sb * yb
    return dict(ys=ys, th=th, ysb=ysb, va=va, svt=svt, ya=ya, rs=rs, on=on, sgo=sgo, o2b=o2b, yb=yb,
                sa=sa, sb=sb, mixed=mixed)


def merge_fwd(h, ys5, o, og, ga, gb, wv, wt, ghg, who, wmo, tm=256):
    t = h.shape[0]

    def body(h_ref, ys5_ref, o_ref, og_ref, ga_ref, gb_ref, wv_ref, wt_ref, ghg_ref, who_ref, wmo_ref, out_ref):
        c = _merge_core(ys5_ref[...], o_ref[...], og_ref[...], ga_ref[...], gb_ref[...],
                        wv_ref, wt_ref, ghg_ref[...], who_ref)
        out_ref[...] = h_ref[...] + _dot(c["mixed"].astype(BF16), wmo_ref[...])

    tok = pl.BlockSpec((tm, D_MODEL), lambda i: (i, 0))
    return pl.pallas_call(
        body, name="merge_fwd", grid=(t // tm,),
        out_shape=jax.ShapeDtypeStruct((t, D_MODEL), F32),
        in_specs=[tok, pl.BlockSpec((tm, S5_WIDTH), lambda i: (i, 0)), tok, tok, tok, tok,
                  _const_spec((N_SHARD, S5_WIDTH, 256)), _const_spec((N_SHARD, S5_WIDTH, 256)),
                  _const_spec((1, HG_WIDTH)), _const_spec((HG_WIDTH, D_MODEL)), _const_spec((D_MODEL, D_MODEL))],
        out_specs=tok,
        compiler_params=_cparams(("arbitrary",)),
    )(h, ys5, o, og, ga, gb, wv, wt, ghg, who, wmo)


def merge_bwd(dh, ys5, o, og, ga, gb, wv, wt, ghg, who, wmo, comm=None, tm=256):
    t = dh.shape[0]

    def body(dh_ref, ys5_ref, o_ref, og_ref, ga_ref, gb_ref, wv_ref, wt_ref, ghg_ref, who_ref, wmo_ref,
             dys5_ref, do_ref, dog_ref, dga_ref, dgb_ref, dghg_ref,
             mixb_ref, dhb_ref, ysb_ref, dvab_ref, dvtb_ref, o2b_ref, dybb_ref):
        ys5 = ys5_ref[...]
        o = o_ref[...]
        og = og_ref[...]
        ghg = ghg_ref[...]
        c = _merge_core(ys5, o, og, ga_ref[...], gb_ref[...], wv_ref, wt_ref, ghg, who_ref)
        dhb = dh_ref[...].astype(BF16)
        dhb_ref[...] = dhb
        mixb_ref[...] = c["mixed"].astype(BF16)
        ysb_ref[...] = c["ysb"]
        o2b_ref[...] = c["o2b"]
        dmix = _dot_nt(dhb, wmo_ref[...])
        sa, sb = c["sa"], c["sb"]
        dya = dmix * sa
        dyb = dmix * sb
        dga_ref[...] = dmix * c["ya"] * sa * (1.0 - sa)
        dgb_ref[...] = dmix * c["yb"] * sb * (1.0 - sb)
        svt = c["svt"]
        dva = (dya * svt).astype(BF16)
        dvt = (dya * c["va"] * svt * (1.0 - svt)).astype(BF16)
        dvab_ref[...] = dva
        dvtb_ref[...] = dvt
        dys = jnp.zeros((tm, S5_WIDTH), F32)
        for s in range(N_SHARD):
            dys = dys + _dot_nt(dva[:, s * 256:(s + 1) * 256], wv_ref[s]) + _dot_nt(dvt[:, s * 256:(s + 1) * 256], wt_ref[s])
        th = c["th"]
        dgelu = 0.5 * (1.0 + th) + 0.5 * ys5 * (1.0 - th * th) * GELU_C * (1.0 + 3.0 * 0.044715 * ys5 * ys5)
        dys5_ref[...] = dys * dgelu
        dybb = dyb.astype(BF16)
        dybb_ref[...] = dybb
        do2 = _dot_nt(dybb, who_ref[...])
        sgo = c["sgo"]
        sil = og * sgo
        on = c["on"]
        dog_ref[...] = do2 * on * ghg * (sgo * (1.0 + og * (1.0 - sgo)))
        _accum(dghg_ref, jnp.sum(do2 * on * sil, axis=0, keepdims=True), pl.program_id(0) == 0)
        don = do2 * ghg * sil
        dos = []
        for h in range(HG_HEADS):
            sl = slice(h * HG_E, (h + 1) * HG_E)
            m = jnp.mean(don[:, sl] * on[:, sl], axis=-1, keepdims=True)
            dos.append(c["rs"][h] * (don[:, sl] - on[:, sl] * m))
        do_ref[...] = jnp.concatenate(dos, axis=1)

    tok = pl.BlockSpec((tm, D_MODEL), lambda i: (i, 0))
    s5b = pl.BlockSpec((tm, S5_WIDTH), lambda i: (i, 0))
    f32t = jax.ShapeDtypeStruct((t, D_MODEL), F32)
    bft = jax.ShapeDtypeStruct((t, D_MODEL), BF16)
    return _carry(
        body, comm, name="merge_bwd", steps=t // tm,
        out_shape=(jax.ShapeDtypeStruct((t, S5_WIDTH), F32), f32t, f32t, f32t, f32t,
                   jax.ShapeDtypeStruct((1, HG_WIDTH), F32),
                   bft, bft, jax.ShapeDtypeStruct((t, S5_WIDTH), BF16), bft, bft, bft, bft),
        in_specs=[tok, s5b, tok, tok, tok, tok,
                  _const_spec((N_SHARD, S5_WIDTH, 256)), _const_spec((N_SHARD, S5_WIDTH, 256)),
                  _const_spec((1, HG_WIDTH)), _const_spec((HG_WIDTH, D_MODEL)), _const_spec((D_MODEL, D_MODEL))],
        out_specs=(s5b, tok, tok, tok, tok, pl.BlockSpec((1, HG_WIDTH), lambda i: (0, 0)),
                   tok, tok, s5b, tok, tok, tok, tok),
        args=(dh, ys5, o, og, ga, gb, wv, wt, ghg, who, wmo),
    )


def head_fwd_bwd(h, p, tgt, gple, wpg, wpp, gfin, tm=256):
    t = h.shape[0]

    def body(h_ref, p_ref, tgt_ref, gple_ref, wpg_ref, wpp_ref, gfin_ref,
             loss_ref, dh_ref, dgple_ref, dgfin_ref, nb_ref, dlb_ref, dppb_ref):
        first = pl.program_id(0) == 0
        hv = h_ref[...]
        gple = gple_ref[...]
        gfin = gfin_ref[...]
        n, r3 = _rms_fwd(hv, gple)
        nb = n.astype(BF16)
        nb_ref[...] = nb
        pg = _sigmoid(_dot(nb, wpg_ref[...]))
        pb = p_ref[...].astype(BF16)
        pp = jnp.concatenate([_dot(pb, wpp_ref[s]) for s in range(N_SHARD)], axis=1)
        h4 = hv + pg * pp
        y, r4 = _rms_fwd(h4, gfin)
        err = y - tgt_ref[...]
        lsum = 0.5 * jnp.sum(jnp.sum(err * err, axis=-1, keepdims=True), axis=0, keepdims=True) / D_MODEL
        _accum(loss_ref, jnp.broadcast_to(lsum, (8, 128)), first)
        dy = err * (1.0 / D_MODEL)
        dh4, dgf = _rms_bwd(h4, r4, gfin, dy)
        _accum(dgfin_ref, dgf, first)
        dpp = dh4 * pg
        dppb_ref[...] = dpp.astype(BF16)
        dl = (dh4 * pp * pg * (1.0 - pg)).astype(BF16)
        dlb_ref[...] = dl
        dn = _dot_nt(dl, wpg_ref[...])
        dx, dgp = _rms_bwd(hv, r3, gple, dn)
        _accum(dgple_ref, dgp, first)
        dh_ref[...] = dh4 + dx

    tok = pl.BlockSpec((tm, D_MODEL), lambda i: (i, 0))
    vec = pl.BlockSpec((1, D_MODEL), lambda i: (0, 0))
    bft = jax.ShapeDtypeStruct((t, D_MODEL), BF16)
    return pl.pallas_call(
        body, name="head_fwd_bwd", grid=(t // tm,),
        out_shape=(jax.ShapeDtypeStruct((8, 128), F32), jax.ShapeDtypeStruct((t, D_MODEL), F32),
                   jax.ShapeDtypeStruct((1, D_MODEL), F32), jax.ShapeDtypeStruct((1, D_MODEL), F32),
                   bft, bft, bft),
        in_specs=[tok, pl.BlockSpec((tm, PLE_DIM), lambda i: (i, 0)), tok,
                  _const_spec((1, D_MODEL)), _const_spec((D_MODEL, D_MODEL)),
                  _const_spec((N_SHARD, PLE_DIM, 256)), _const_spec((1, D_MODEL))],
        out_specs=(pl.BlockSpec((8, 128), lambda i: (0, 0)), tok, vec, vec, tok, tok, tok),
        compiler_params=_cparams(("arbitrary",)),
    )(h, p, tgt, gple, wpg, wpp, gfin)


BIG = ("ffn1_w_gate", "ffn1_w_up", "ffn1_w_down", "w_in", "s5_glu_val", "s5_glu_gate", "hg_w_out",
       "w_merge_out", "ffn2_w_gate", "ffn2_w_up", "ffn2_w_down", "ple_w_gate", "ple_w_proj")
FFN_T = ("ffn1_w_gate", "ffn1_w_up", "ffn2_w_gate", "ffn2_w_up")
BIG_SHARD = {
    "ffn1_w_gate": (FF_PAD, D_MODEL), "ffn1_w_up": (FF_PAD, D_MODEL), "ffn1_w_down": (FF_PAD, D_MODEL),
    "ffn2_w_gate": (FF_PAD, D_MODEL), "ffn2_w_up": (FF_PAD, D_MODEL), "ffn2_w_down": (FF_PAD, D_MODEL),
    "w_in": (D_MODEL, IN_COLS // N_SHARD), "s5_glu_val": (S5_WIDTH, 256), "s5_glu_gate": (S5_WIDTH, 256),
    "hg_w_out": (256, D_MODEL), "w_merge_out": (256, D_MODEL), "ple_w_gate": (256, D_MODEL),
    "ple_w_proj": (PLE_DIM, 256),
}


def _lower_bound(hb):
    return jax.nn.softmax(hb, axis=0)[0:1]


class Schedule:
    def __init__(self, wts):
        self.wts = dict(wts)
        self.grads = {}

    def before(self, kernel_name):
        return None

    def after(self, kernel_name, results):
        pass

    def grad(self, name, g):
        self.grads[name] = g


def local_step(x, p, tgt, sched, sm):
    wts = sched.wts
    rows_full = lambda w: w.reshape(N_SHARD * w.shape[1], w.shape[2])

    def carried(kernel_name, fn, *args):
        outs, results = fn(*args, comm=sched.before(kernel_name))
        sched.after(kernel_name, results)
        return outs

    def weight_grad(name, xs, ys, shard):
        kernel_name = "g_" + name
        (g,), results = tn_matmul(xs, ys, kernel_name, shard, comm=sched.before(kernel_name))
        sched.grad(name, g)
        sched.after(kernel_name, results)

    lb, lb_vjp = jax.vjp(_lower_bound, sm["hg_lower_bound"])
    s5_names = ("s5_lam_re", "s5_lam_im", "s5_log_dt", "s5_b_re", "s5_b_im", "s5_c_re", "s5_c_im")
    (lam_bar, bmat, cmat), s5_vjp = jax.vjp(s5_prep, *[sm[k] for k in s5_names])
    pw_r, pw_i = _lam_powers(lam_bar)
    bmat_b = bmat.astype(BF16)
    cmat_b = cmat.astype(BF16)
    bmat_t = jnp.swapaxes(bmat, -1, -2).astype(BF16)
    cmat_t = jnp.swapaxes(cmat, -1, -2).astype(BF16)

    h1, a1, b1 = carried("ffn1_fwd", ffn_fwd, x, sm["ffn1_norm"], wts["ffn1_w_gate"], wts["ffn1_w_up"],
                         wts["ffn1_w_down"], "ffn1_fwd")
    s5in, q, f, v, og, ga, gb = inproj_fwd(h1, sm["mix_norm"], wts["w_in"])
    ys5, xp = carried("s5_fwd", s5_fwd, s5in, _scan_tables(pw_r, pw_i, False), bmat_b, cmat_b, sm["s5_d"])
    o, states = hgrn_fwd(q, f, v, lb)
    who = rows_full(wts["hg_w_out"])
    wmo = rows_full(wts["w_merge_out"])
    h2 = merge_fwd(h1, ys5, o, og, ga, gb, wts["s5_glu_val"], wts["s5_glu_gate"], sm["hg_out_norm"], who, wmo)
    (h3, a2, b2), _ = ffn_fwd(h2, sm["ffn2_norm"], wts["ffn2_w_gate"], wts["ffn2_w_up"], wts["ffn2_w_down"], "ffn2_fwd")
    loss, dh3, d_ple_norm, d_final_norm, npb, dlgb, dppb = head_fwd_bwd(
        h3, p, tgt, sm["ple_norm"], rows_full(wts["ple_w_gate"]), wts["ple_w_proj"], sm["final_norm"])

    gs = {"ple_norm": d_ple_norm, "final_norm": d_final_norm}
    weight_grad("ple_w_gate", npb, dlgb, "rows")
    weight_grad("ple_w_proj", p, dppb, "cols")

    (dh2, gs["ffn2_norm"], n2b, dhb2, da2, db2, s2), _ = ffn_bwd(
        dh3, h2, a2, b2, sm["ffn2_norm"], wts["ffn2_w_gate"], wts["ffn2_w_up"], wts["ffn2_w_down"], "ffn2_bwd")
    weight_grad("ffn2_w_gate", da2, n2b, "rows")
    weight_grad("ffn2_w_up", db2, n2b, "rows")
    weight_grad("ffn2_w_down", s2, dhb2, "rows")

    dys5, do, dog, dga, dgb, gs["hg_out_norm"], mixb, dh2b, ysb, dvab, dvtb, o2b, dybb = carried(
        "merge_bwd", merge_bwd,
        dh2, ys5, o, og, ga, gb, wts["s5_glu_val"], wts["s5_glu_gate"], sm["hg_out_norm"], who, wmo)
    weight_grad("w_merge_out", mixb, dh2b, "rows")
    weight_grad("s5_glu_val", ysb, dvab, "cols")
    weight_grad("s5_glu_gate", ysb, dvtb, "cols")
    weight_grad("hg_w_out", o2b, dybb, "rows")

    dq, df, dv, dlb = carried("hgrn_bwd", hgrn_bwd, do, q, f, v, lb, states)
    (gs["hg_lower_bound"],) = lb_vjp(dlb)
    du, dbmat, dcmat, dlam8, gs["s5_d"] = carried(
        "s5_bwd", s5_bwd,
        dys5, s5in, xp, _scan_tables(pw_r, pw_i, True), bmat_b, bmat_t, cmat_t, sm["s5_d"])
    for k, g in zip(s5_names, s5_vjp((jnp.sum(dlam8, axis=1), dbmat, dcmat))):
        gs[k] = g

    dh1, gs["mix_norm"], nmb, dprojb = carried(
        "inproj_bwd", inproj_bwd, dh2, h1, sm["mix_norm"], wts["w_in"], (du, dq, df, dv, dog, dga, dgb))
    weight_grad("w_in", nmb, dprojb, "cols")

    dx, gs["ffn1_norm"], n1b, dhb1, da1, db1, s1 = carried(
        "ffn1_bwd", ffn_bwd,
        dh1, x, a1, b1, sm["ffn1_norm"], wts["ffn1_w_gate"], wts["ffn1_w_up"], wts["ffn1_w_down"], "ffn1_bwd")
    weight_grad("ffn1_w_gate", da1, n1b, "rows")
    weight_grad("ffn1_w_up", db1, n1b, "rows")
    weight_grad("ffn1_w_down", s1, dhb1, "rows")
    return loss, dx, gs


MESH = pl.DeviceIdType.MESH
ANY = pl.BlockSpec(memory_space=pl.ANY)


def _place():
    x, y, c = lax.axis_index("x"), lax.axis_index("y"), lax.axis_index("c")
    return x, y, c


def _remote(src, dst, ssem, rsem, dev):
    return pltpu.make_async_remote_copy(src_ref=src, dst_ref=dst, send_sem=ssem, recv_sem=rsem,
                                        device_id=dev, device_id_type=MESH)


class Comm:
    def __init__(self, bufs, outs, alias, sems, hooks):
        self.bufs, self.outs, self.alias, self.sems, self.hooks = list(bufs), list(outs), alias, list(sems), hooks


def run_comm(comm, name):
    nb, no = len(comm.bufs), len(comm.outs)

    def body(*refs):
        for which in ("first", "mid", "last"):
            if which in comm.hooks:
                comm.hooks[which](refs[:nb], refs[nb:nb + no], refs[nb + no:])

    return pl.pallas_call(
        body, name=name, out_shape=tuple(comm.outs), in_specs=[ANY] * nb, out_specs=tuple([ANY] * no),
        input_output_aliases=dict(comm.alias), scratch_shapes=comm.sems,
    )(*comm.bufs)


def _carry(body, comm, *, name, steps, out_shape, in_specs, out_specs, args, scratch_shapes=()):
    out_shape, out_specs, scratch_shapes = tuple(out_shape), tuple(out_specs), list(scratch_shapes)
    if comm is None:
        res = pl.pallas_call(body, name=name, grid=(steps,), out_shape=out_shape, in_specs=list(in_specs),
                             out_specs=out_specs, scratch_shapes=scratch_shapes,
                             compiler_params=_cparams(("arbitrary",)))(*args)
        return tuple(res), ()
    n_in, n_out, n_scr = len(args), len(out_shape), len(scratch_shapes)
    nb, no = len(comm.bufs), len(comm.outs)

    def wrapped(*refs):
        ins, cb = refs[:n_in], refs[n_in:n_in + nb]
        o0 = n_in + nb
        outs, co = refs[o0:o0 + n_out], refs[o0 + n_out:o0 + n_out + no]
        s0 = o0 + n_out + no
        scr, cs = refs[s0:s0 + n_scr], refs[s0 + n_scr:]
        step = pl.program_id(0)

        def hook(which, at):
            if which in comm.hooks:
                pl.when(step == at)(lambda: comm.hooks[which](cb, co, cs))

        hook("first", 0)
        hook("mid", steps // 2)
        body(*ins, *outs, *scr)
        hook("last", steps - 1)

    res = pl.pallas_call(
        wrapped, name=name, grid=(steps,), out_shape=out_shape + tuple(comm.outs),
        in_specs=list(in_specs) + [ANY] * nb, out_specs=out_specs + (ANY,) * no,
        scratch_shapes=scratch_shapes + comm.sems,
        input_output_aliases={n_in + i: n_out + o for i, o in comm.alias.items()},
        compiler_params=_cparams(("arbitrary",)),
    )(*args, *comm.bufs)
    return tuple(res[:n_out]), tuple(res[n_out:])


def gather_comm(bufs):
    n = len(bufs)

    def copies(outs, sems):
        s_own, r_own, s_fwd, r_fwd, s_sib, r_sib = sems
        x, y, c = _place()
        me = 2 * x + y
        nbr = ((1 - x, y), (x, 1 - y))
        nbr_id = (2 * (1 - x) + y, 2 * x + (1 - y))
        diag_id = 2 * (1 - x) + (1 - y)
        sib = (x, y, 1 - c)

        def rows(w, q=None):
            r = outs[w].shape[1]
            if q is None:
                return pl.ds(pl.multiple_of(c * (r // 2), 16), r // 2)
            return pl.ds(pl.multiple_of(c * (r // 2) + q * (r // 4), 16), r // 4)

        def own(w, j):
            piece = outs[w].at[me, rows(w)]
            return _remote(piece, piece, s_own.at[w, j], r_own.at[w, j], (nbr[j][0], nbr[j][1], c))

        def from_nbr(w, j):
            piece = outs[w].at[nbr_id[j], rows(w)]
            return _remote(piece, piece, s_own.at[w, j], r_own.at[w, j], (nbr[j][0], nbr[j][1], c))

        def fwd(w, j):
            piece = outs[w].at[nbr_id[j], rows(w, j)]
            return _remote(piece, piece, s_fwd.at[w, j], r_fwd.at[w, j], (nbr[1 - j][0], nbr[1 - j][1], c))

        def from_diag(w, j):
            piece = outs[w].at[diag_id, rows(w, j)]
            return _remote(piece, piece, s_fwd.at[w, j], r_fwd.at[w, j], (nbr[1 - j][0], nbr[1 - j][1], c))

        def to_sib(w, k):
            piece = (outs[w].at[nbr_id[k], rows(w)] if k < 2 else outs[w].at[diag_id, rows(w, k - 2)])
            return _remote(piece, piece, s_sib.at[w, k], r_sib.at[w, k], sib)

        def from_sib(w, k):
            r = outs[w].shape[1]
            if k < 2:
                piece = outs[w].at[nbr_id[k], pl.ds(pl.multiple_of((1 - c) * (r // 2), 16), r // 2)]
            else:
                piece = outs[w].at[diag_id, pl.ds(pl.multiple_of((1 - c) * (r // 2) + (k - 2) * (r // 4), 16), r // 4)]
            return _remote(piece, piece, s_sib.at[w, k], r_sib.at[w, k], sib)

        return own, from_nbr, fwd, from_diag, to_sib, from_sib

    def first(_, outs, sems):
        own = copies(outs, sems)[0]
        for w in range(n):
            own(w, 0).start()
            own(w, 1).start()

    def mid(_, outs, sems):
        _, from_nbr, fwd, _, to_sib, _ = copies(outs, sems)
        for w in range(n):
            for j in range(2):
                from_nbr(w, j).wait_recv()
                fwd(w, j).start()
                to_sib(w, j).start()

    def last(_, outs, sems):
        own, _, fwd, from_diag, to_sib, from_sib = copies(outs, sems)
        for w in range(n):
            for j in range(2):
                from_diag(w, j).wait_recv()
                to_sib(w, 2 + j).start()
        for w in range(n):
            for k in range(4):
                from_sib(w, k).wait_recv()
        for w in range(n):
            for j in range(2):
                own(w, j).wait_send()
                fwd(w, j).wait_send()
            for k in range(4):
                to_sib(w, k).wait_send()

    dma = pltpu.SemaphoreType.DMA
    return Comm(bufs, [jax.ShapeDtypeStruct(b.shape, b.dtype) for b in bufs], {w: w for w in range(n)},
                [dma((n, 2)), dma((n, 2)), dma((n, 2)), dma((n, 2)), dma((n, 4)), dma((n, 4))],
                {"first": first, "mid": mid, "last": last})


def _start_wait(make):
    def first(bufs, outs, sems):
        for cp in make(bufs, outs, sems):
            cp.start()

    def last(bufs, outs, sems):
        for cp in make(bufs, outs, sems):
            cp.wait()

    return {"first": first, "last": last}


def exchange_comm(grads):
    n = len(grads)

    def make(ins, outs, sems):
        x, y, c = _place()
        cps = []
        for w in range(n):
            half = ins[w].shape[1] // 2
            src = ins[w].at[:, pl.ds(pl.multiple_of((1 - c) * half, 8), half), :]
            cps.append(_remote(src, outs[w], sems[0].at[w], sems[1].at[w], (x, y, 1 - c)))
        return cps

    dma = pltpu.SemaphoreType.DMA
    return Comm(grads, [jax.ShapeDtypeStruct((N_SHARD, g.shape[1] // 2, g.shape[2]), g.dtype) for g in grads],
                {}, [dma((n,)), dma((n,))], _start_wait(make))


def scatter_comm(sums):
    n = len(sums)

    def make(ins, outs, sems):
        x, y, c = _place()
        chips = ((1 - x, y), (x, 1 - y), (1 - x, 1 - y))
        return [_remote(ins[w].at[2 * ch[0] + ch[1]], outs[w].at[j], sems[0].at[w, j], sems[1].at[w, j],
                        (ch[0], ch[1], c))
                for w in range(n) for j, ch in enumerate(chips)]

    dma = pltpu.SemaphoreType.DMA
    return Comm(sums, [jax.ShapeDtypeStruct((3,) + s.shape[1:], s.dtype) for s in sums],
                {}, [dma((n, 3)), dma((n, 3))], _start_wait(make))


def join_comm(shards):
    n = len(shards)

    def make(_, outs, sems):
        x, y, c = _place()
        cps = []
        for w in range(n):
            half = outs[w].shape[0] // 2
            mine = outs[w].at[pl.ds(pl.multiple_of(c * half, 8), half), :]
            cps.append(_remote(mine, mine, sems[0].at[w], sems[1].at[w], (x, y, 1 - c)))
        return cps

    dma = pltpu.SemaphoreType.DMA
    return Comm(shards, [jax.ShapeDtypeStruct(s.shape, s.dtype) for s in shards], {w: w for w in range(n)},
                [dma((n,)), dma((n,))], _start_wait(make))


def allreduce_small(vec):
    half = vec.shape[0] // 2

    def body(v_ref, o_ref, pair, chips_buf, s1, r1, s2, r2, s3, r3):
        x, y, c = _place()
        chip = 2 * x + y
        sib = (x, y, 1 - c)
        mine = pl.ds(pl.multiple_of(c * half, 8), half)
        other = pl.ds(pl.multiple_of((1 - c) * half, 8), half)
        to_sib = _remote(v_ref.at[other], pair, s1, r1, sib)
        to_sib.start()
        to_sib.wait()
        chips_buf[chip] = v_ref[mine, :] + pair[...]
        sends = [_remote(chips_buf.at[chip], chips_buf.at[chip], s2.at[j], r2.at[j], (ch[0], ch[1], c))
                 for j, ch in enumerate(((1 - x, y), (x, 1 - y), (1 - x, 1 - y)))]
        for cp in sends:
            cp.start()
        for cp in sends:
            cp.wait()
        o_ref[mine, :] = (chips_buf[0] + chips_buf[1]) + (chips_buf[2] + chips_buf[3])
        back = _remote(o_ref.at[mine], o_ref.at[mine], s3, r3, sib)
        back.start()
        back.wait()

    dma = pltpu.SemaphoreType.DMA
    return pl.pallas_call(
        body, name="allreduce_small",
        out_shape=jax.ShapeDtypeStruct(vec.shape, F32),
        in_specs=[pl.BlockSpec(memory_space=pltpu.VMEM)],
        out_specs=pl.BlockSpec(memory_space=pltpu.VMEM),
        scratch_shapes=[pltpu.VMEM((half, 128), F32), pltpu.VMEM((N_SHARD, half, 128), F32),
                        dma, dma, dma((3,)), dma((3,)), dma, dma],
        compiler_params=pltpu.CompilerParams(vmem_limit_bytes=VMEM_LIMIT),
    )(vec)


ROW_TILE = 128


def add_own_half(place, g, recv, name):
    _, r, cc = g.shape
    half = r // 2
    nb = half // ROW_TILE

    def body(p_ref, g_ref, r_ref, o_ref, ob_ref):
        s = g_ref[...] + r_ref[...]
        ob_ref[...] = s.astype(BF16)

        @pl.when(pl.program_id(1) == p_ref[0])
        def _():
            o_ref[...] = s

    blk = (None, ROW_TILE, cc)
    return pl.pallas_call(
        body, name=name,
        grid_spec=pltpu.PrefetchScalarGridSpec(
            num_scalar_prefetch=1, grid=(nb, N_SHARD),
            in_specs=[pl.BlockSpec(blk, lambda i, s, p_ref: (s, p_ref[1] * nb + i, 0)),
                      pl.BlockSpec(blk, lambda i, s, p_ref: (s, i, 0))],
            out_specs=(pl.BlockSpec((ROW_TILE, cc), lambda i, s, p_ref: (i, 0)),
                       pl.BlockSpec(blk, lambda i, s, p_ref: (s, i, 0)))),
        out_shape=(jax.ShapeDtypeStruct((half, cc), F32),
                   jax.ShapeDtypeStruct((N_SHARD, half, cc), BF16)),
        compiler_params=_cparams(("arbitrary", "arbitrary")),
    )(place, g, recv)


def add_chip_sums(place, own, recv, name):
    half, cc = own.shape
    nb = half // ROW_TILE

    def body(s_ref, o_ref, r_ref, out_ref):
        del s_ref
        acc = o_ref[...] + r_ref[0].astype(F32)
        acc = acc + r_ref[1].astype(F32)
        out_ref[...] = acc + r_ref[2].astype(F32)

    return pl.pallas_call(
        body, name=name,
        grid_spec=pltpu.PrefetchScalarGridSpec(
            num_scalar_prefetch=1, grid=(nb,),
            in_specs=[pl.BlockSpec((ROW_TILE, cc), lambda i, s_ref: (i, 0)),
                      pl.BlockSpec((3, ROW_TILE, cc), lambda i, s_ref: (0, i, 0))],
            out_specs=pl.BlockSpec((ROW_TILE, cc), lambda i, s_ref: (s_ref[1] * nb + i, 0))),
        out_shape=jax.ShapeDtypeStruct((2 * half, cc), F32),
        compiler_params=_cparams(("arbitrary",)),
    )(place, own, recv)


def adamw(w, m, v, g, name, copy_g=False):
    r, cc = w.shape
    tr = next(t for t in (256, 352, r) if r % t == 0)
    bc1 = 1.0 / (1.0 - ADAM_B1 ** ADAM_STEP)
    bc2 = 1.0 / (1.0 - ADAM_B2 ** ADAM_STEP)

    def body(w_ref, m_ref, v_ref, g_ref, d_ref, mo_ref, vo_ref, *go_ref):
        gv = g_ref[...]
        mn = ADAM_B1 * m_ref[...] + (1.0 - ADAM_B1) * gv
        vn = ADAM_B2 * v_ref[...] + (1.0 - ADAM_B2) * (gv * gv)
        mo_ref[...] = mn
        vo_ref[...] = vn
        d_ref[...] = -ADAM_LR * ((mn * bc1) / (jnp.sqrt(vn * bc2) + ADAM_EPS) + ADAM_WD * w_ref[...])
        if copy_g:
            go_ref[0][...] = gv

    blk = pl.BlockSpec((tr, cc), lambda i: (i, 0))
    shp = jax.ShapeDtypeStruct((r, cc), F32)
    nout = 4 if copy_g else 3
    return pl.pallas_call(
        body, name=name, grid=(r // tr,),
        out_shape=(shp,) * nout, in_specs=[blk] * 4, out_specs=(blk,) * nout,
        compiler_params=_cparams(("arbitrary",)),
    )(w, m, v, g)


GATHER_FIRST = ("ffn1_w_gate", "ffn1_w_up", "ffn1_w_down")
GATHER_ON = {"ffn1_fwd": ("w_in", "s5_glu_val", "s5_glu_gate", "hg_w_out", "w_merge_out"),
             "s5_fwd": ("ffn2_w_gate", "ffn2_w_up", "ffn2_w_down", "ple_w_gate", "ple_w_proj")}
REDUCE = ((("ple_w_gate", "ple_w_proj", "ffn2_w_gate", "ffn2_w_up", "ffn2_w_down"), "merge_bwd", "hgrn_bwd"),
          (("w_merge_out", "s5_glu_val", "s5_glu_gate", "hg_w_out"), "s5_bwd", "inproj_bwd"),
          (("w_in",), None, "ffn1_bwd"),
          (("ffn1_w_gate",), "g_ffn1_w_up", "g_ffn1_w_down"),
          (("ffn1_w_up",), "g_ffn1_w_down", None),
          (("ffn1_w_down",), None, None))


def merge_comms(comms):
    if len(comms) == 1:
        return comms[0], [len(comms[0].outs)]
    bufs, outs, sems, alias, spans = [], [], [], {}, []
    for c in comms:
        spans.append((len(bufs), len(bufs) + len(c.bufs), len(outs), len(outs) + len(c.outs),
                      len(sems), len(sems) + len(c.sems)))
        alias.update({len(bufs) + i: len(outs) + o for i, o in c.alias.items()})
        bufs, outs, sems = bufs + c.bufs, outs + c.outs, sems + c.sems

    def hook(which):
        def run(b, o, s):
            for c, (b0, b1, o0, o1, s0, s1) in zip(comms, spans):
                if which in c.hooks:
                    c.hooks[which](b[b0:b1], o[o0:o1], s[s0:s1])
        return run

    hooks = {w: hook(w) for w in ("first", "mid", "last") if any(w in c.hooks for c in comms)}
    return Comm(bufs, outs, alias, sems, hooks), [len(c.outs) for c in comms]


class DistSchedule(Schedule):
    def __init__(self, bufs, chip, core):
        first = run_comm(gather_comm([bufs[k] for k in GATHER_FIRST]), "gather_ffn1")
        super().__init__(zip(GATHER_FIRST, first))
        self.bufs = bufs
        self.place = jnp.stack([chip, core])
        self.sums, self.halves = {}, {}

    def _exchange(self, names):
        return exchange_comm([self.grads[k] for k in names])

    def _scatter(self, names):
        return scatter_comm([self.sums[k][1] for k in names])

    def _pair_sums(self, names, recv):
        for k, r in zip(names, recv):
            self.sums[k] = add_own_half(self.place, self.grads[k], r, "pair_sum_" + k)

    def _chip_sums(self, names, recv):
        for k, r in zip(names, recv):
            self.halves[k] = add_chip_sums(self.place, self.sums[k][0], r, "chip_sum_" + k)

    def before(self, kernel_name):
        comms, takers = [], []
        if kernel_name in GATHER_ON:
            names = GATHER_ON[kernel_name]
            comms.append(gather_comm([self.bufs[k] for k in names]))
            takers.append(lambda res, names=names: self.wts.update(zip(names, res)))
        for names, exchange_on, scatter_on in REDUCE:
            if kernel_name == exchange_on:
                comms.append(self._exchange(names))
                takers.append(lambda res, names=names: self._pair_sums(names, res))
            if kernel_name == scatter_on:
                if exchange_on is None:
                    self._pair_sums(names, run_comm(self._exchange(names), "exchange_" + names[0]))
                comms.append(self._scatter(names))
                takers.append(lambda res, names=names: self._chip_sums(names, res))
        if not comms:
            return None
        merged, counts = merge_comms(comms)
        self.pending = (takers, counts)
        return merged

    def after(self, kernel_name, results):
        if not results:
            return
        takers, counts = self.pending
        start = 0
        for take, count in zip(takers, counts):
            take(results[start:start + count])
            start += count

    def finish(self):
        tail = [names for names, _, scatter_on in REDUCE if scatter_on is None]
        alone = [k for names, exchange_on, scatter_on in REDUCE if scatter_on is None and exchange_on is None
                 for k in names]
        self._pair_sums(alone, run_comm(self._exchange(alone), "exchange_tail"))
        tail = [k for names in tail for k in names]
        self._chip_sums(tail, run_comm(self._scatter(tail), "scatter_tail"))
        return dict(zip(BIG, run_comm(join_comm([self.halves[k] for k in BIG]), "join_halves")))


SMALL = ("ffn1_norm", "mix_norm", "s5_lam_re", "s5_lam_im", "s5_log_dt", "s5_b_re", "s5_b_im", "s5_c_re",
         "s5_c_im", "s5_d", "hg_lower_bound", "hg_out_norm", "ffn2_norm", "ple_norm", "final_norm")
WEIGHTS = ("ffn1_norm", "ffn1_w_gate", "ffn1_w_up", "ffn1_w_down", "mix_norm", "w_in", "s5_lam_re", "s5_lam_im",
           "s5_log_dt", "s5_b_re", "s5_b_im", "s5_c_re", "s5_c_im", "s5_d", "s5_glu_val", "s5_glu_gate",
           "hg_lower_bound", "hg_out_norm", "hg_w_out", "w_merge_out", "ffn2_norm", "ffn2_w_gate", "ffn2_w_up",
           "ffn2_w_down", "ple_norm", "ple_w_gate", "ple_w_proj", "final_norm")


def _as_rows(name, w):
    return jnp.swapaxes(w[0], 0, 1) if name in FFN_T else w[0]


def _from_rows(name, w):
    return (jnp.swapaxes(w, 0, 1) if name in FFN_T else w)[None]


def _gather_buffer(name, w_rows, chip):
    r, c = BIG_SHARD[name]
    shard = jnp.pad(w_rows.astype(BF16), ((0, r - w_rows.shape[0]), (0, 0)))
    return lax.dynamic_update_slice(jnp.zeros((N_SHARD, r, c), BF16), shard[None], (chip, 0, 0))


def _pack(parts):
    flat = jnp.concatenate([jnp.zeros((128,), F32)] + [a.reshape(-1) for a in parts])
    rows = -(-flat.shape[0] // 2048) * 16
    return jnp.pad(flat, (0, rows * 128 - flat.shape[0])).reshape(rows, 128)


def _unpack(vec, likes):
    flat = vec.reshape(-1)
    out, off = [], 128
    for a in likes:
        out.append(flat[off:off + a.size].reshape(a.shape))
        off += a.size
    return out


def _small_view(name, w):
    if name.startswith("s5_") and name != "s5_d":
        return w[0]
    if name == "final_norm":
        return w.reshape(1, D_MODEL)
    return w


def kernel(x, p, ffn1_norm, ffn1_w_gate, ffn1_w_up, ffn1_w_down, mix_norm, w_in, s5_lam_re, s5_lam_im, s5_log_dt, s5_b_re, s5_b_im, s5_c_re, s5_c_im, s5_d, s5_glu_val, s5_glu_gate, hg_lower_bound, hg_out_norm, hg_w_out, w_merge_out, ffn2_norm, ffn2_w_gate, ffn2_w_up, ffn2_w_down, ple_norm, ple_w_gate, ple_w_proj, final_norm, loss_target, m_ffn1_norm, m_ffn1_w_gate, m_ffn1_w_up, m_ffn1_w_down, m_mix_norm, m_w_in, m_s5_lam_re, m_s5_lam_im, m_s5_log_dt, m_s5_b_re, m_s5_b_im, m_s5_c_re, m_s5_c_im, m_s5_d, m_s5_glu_val, m_s5_glu_gate, m_hg_lower_bound, m_hg_out_norm, m_hg_w_out, m_w_merge_out, m_ffn2_norm, m_ffn2_w_gate, m_ffn2_w_up, m_ffn2_w_down, m_ple_norm, m_ple_w_gate, m_ple_w_proj, m_final_norm, v_ffn1_norm, v_ffn1_w_gate, v_ffn1_w_up, v_ffn1_w_down, v_mix_norm, v_w_in, v_s5_lam_re, v_s5_lam_im, v_s5_log_dt, v_s5_b_re, v_s5_b_im, v_s5_c_re, v_s5_c_im, v_s5_d, v_s5_glu_val, v_s5_glu_gate, v_hg_lower_bound, v_hg_out_norm, v_hg_w_out, v_w_merge_out, v_ffn2_norm, v_ffn2_w_gate, v_ffn2_w_up, v_ffn2_w_down, v_ple_norm, v_ple_w_gate, v_ple_w_proj, v_final_norm):
    given = dict(locals())
    wv = {k: given[k] for k in WEIGHTS}
    mv = {k: given["m_" + k] for k in WEIGHTS}
    vv = {k: given["v_" + k] for k in WEIGHTS}

    core = lax.axis_index("c").astype(jnp.int32)
    chip = (2 * lax.axis_index("x") + lax.axis_index("y")).astype(jnp.int32)
    w_rows = {k: _as_rows(k, wv[k]) for k in BIG}
    sched = DistSchedule({k: _gather_buffer(k, w_rows[k], chip) for k in BIG}, chip, core)
    sm = {k: _small_view(k, wv[k]) for k in SMALL}

    loss_blk, dx, gsm = local_step(x[0], p[0, 0], loss_target[0], sched, sm)
    full = sched.finish()

    small_likes = [wv[k] for k in SMALL]
    packed = _pack([gsm[k] for k in SMALL])
    packed = packed.at[0, 0].set(loss_blk[0, 0])
    total = allreduce_small(packed)
    loss = total[0, 0]
    gsmall = dict(zip(SMALL, _unpack(total, small_likes)))

    grads, deltas, new_m, new_v = {}, {}, {}, {}
    for k in BIG:
        padded = full[k].shape != w_rows[k].shape
        res = adamw(w_rows[k], _as_rows(k, mv[k]), _as_rows(k, vv[k]), full[k], "adamw_" + k, copy_g=padded)
        grads[k] = _from_rows(k, res[3] if padded else full[k])
        deltas[k], new_m[k], new_v[k] = (_from_rows(k, a) for a in res[:3])
    sw = _pack([wv[k] for k in SMALL])
    smm = _pack([mv[k] for k in SMALL])
    svv = _pack([vv[k] for k in SMALL])
    sd, smn, svn = adamw(sw, smm, svv, total, "adamw_small")
    for k, d, mn, vn in zip(SMALL, _unpack(sd, small_likes), _unpack(smn, small_likes), _unpack(svn, small_likes)):
        grads[k], deltas[k], new_m[k], new_v[k] = gsmall[k], d, mn, vn

    return (loss, dx[None], *[grads[k] for k in WEIGHTS], *[deltas[k] for k in WEIGHTS],
            *[new_m[k] for k in WEIGHTS], *[new_v[k] for k in WEIGHTS])
```

```python
import math

import jax
import jax.numpy as jnp
from jax import lax
from jax.experimental import pallas as pl
from jax.experimental.pallas import tpu as pltpu

F32 = jnp.float32
BF16 = jnp.bfloat16

D_MODEL = 1024
D_FF = 2816
N_SHARD = 4
FF_SHARD = D_FF // N_SHARD
FF_PAD = 768
NORM_EPS = 1e-6
PLE_DIM = 256

S5_WIDTH = 512
S5_GROUPS = 32
S5_GROUP = 16
S5_STATE = 64
S5_N = S5_GROUPS * S5_STATE
S5_KT = 2

HG_HEADS = 8
HG_E = 128
HG_WIDTH = 1024
CHUNK = 64
HG_SUB = 4
IN_COLS = S5_WIDTH + 4 * HG_WIDTH + 2 * D_MODEL
IN_SPLITS = (0, 512, 1536, 2560, 3584, 4608, 5632, 6656)

ADAM_LR = 0.001
ADAM_B1 = 0.9
ADAM_B2 = 0.999
ADAM_EPS = 1e-08
ADAM_WD = 0.01
ADAM_STEP = 10

VMEM_LIMIT = 60 * 1024 * 1024
HIGHEST = lax.Precision.HIGHEST


def _cparams(sem=None, **kw):
    return pltpu.CompilerParams(dimension_semantics=sem, vmem_limit_bytes=VMEM_LIMIT, **kw)


def _const_spec(shape):
    nd = len(shape)
    return pl.BlockSpec(shape, lambda *_: (0,) * nd, pipeline_mode=pl.Buffered(1))


def _dot(a, b):
    return jnp.dot(a, b, preferred_element_type=F32)


def _dot_nt(a, b):
    return lax.dot_general(a, b, (((1,), (1,)), ((), ())), preferred_element_type=F32)


def _dot_tn(a, b):
    return lax.dot_general(a, b, (((0,), (0,)), ((), ())), preferred_element_type=F32)


def _sigmoid(x):
    return 1.0 / (1.0 + jnp.exp(-x))


def _rms_fwd(x, g):
    r = lax.rsqrt(jnp.mean(x * x, axis=-1, keepdims=True) + NORM_EPS)
    return x * r * g, r


def _rms_bwd(x, r, g, dy):
    xh = x * r
    dyg = dy * g
    m = jnp.mean(dyg * xh, axis=-1, keepdims=True)
    return r * (dyg - xh * m), jnp.sum(dy * xh, axis=0, keepdims=True)


def _accum(ref, val, first):
    @pl.when(first)
    def _():
        ref[...] = val

    @pl.when(jnp.logical_not(first))
    def _():
        ref[...] += val


def ffn_fwd(h, gain, wg, wu, wd, name, comm=None, tm=256):
    t = h.shape[0]

    def body(h_ref, g_ref, wg_ref, wu_ref, wd_ref, o_ref, a_ref, b_ref):
        hv = h_ref[...]
        n, _ = _rms_fwd(hv, g_ref[...])
        nb = n.astype(BF16)
        acc = jnp.zeros((tm, D_MODEL), F32)
        for s in range(N_SHARD):
            a = _dot_nt(nb, wg_ref[s])
            b = _dot_nt(nb, wu_ref[s])
            a_ref[s] = a.astype(BF16)
            b_ref[s] = b.astype(BF16)
            sv = (a * _sigmoid(a) * b).astype(BF16)
            acc = acc + _dot(sv, wd_ref[s])
        o_ref[...] = hv + 0.5 * acc

    return _carry(
        body, comm, name=name, steps=t // tm,
        out_shape=(jax.ShapeDtypeStruct((t, D_MODEL), F32),
                   jax.ShapeDtypeStruct((N_SHARD, t, FF_PAD), BF16),
                   jax.ShapeDtypeStruct((N_SHARD, t, FF_PAD), BF16)),
        in_specs=[pl.BlockSpec((tm, D_MODEL), lambda i: (i, 0)),
                  _const_spec((1, D_MODEL)),
                  _const_spec((N_SHARD, FF_PAD, D_MODEL)),
                  _const_spec((N_SHARD, FF_PAD, D_MODEL)),
                  _const_spec((N_SHARD, FF_PAD, D_MODEL))],
        out_specs=(pl.BlockSpec((tm, D_MODEL), lambda i: (i, 0)),
                   pl.BlockSpec((N_SHARD, tm, FF_PAD), lambda i: (0, i, 0)),
                   pl.BlockSpec((N_SHARD, tm, FF_PAD), lambda i: (0, i, 0))),
        args=(h, gain, wg, wu, wd),
    )


def ffn_bwd(dho, h, a, b, gain, wg, wu, wd, name, comm=None, tm=256):
    t = h.shape[0]

    def body(dho_ref, h_ref, a_ref, b_ref, g_ref, wg_ref, wu_ref, wd_ref,
             dh_ref, dg_ref, nb_ref, dhb_ref, da_ref, db_ref, s_ref):
        hv = h_ref[...]
        g = g_ref[...]
        n, r = _rms_fwd(hv, g)
        nb_ref[...] = n.astype(BF16)
        dhalf = (0.5 * dho_ref[...]).astype(BF16)
        dhb_ref[...] = dhalf
        dn = jnp.zeros((tm, D_MODEL), F32)
        for s in range(N_SHARD):
            av = a_ref[s].astype(F32)
            bv = b_ref[s].astype(F32)
            sg = _sigmoid(av)
            sil = av * sg
            s_ref[s] = (sil * bv).astype(BF16)
            ds = _dot_nt(dhalf, wd_ref[s])
            da = (ds * bv * (sg * (1.0 + av * (1.0 - sg)))).astype(BF16)
            db = (ds * sil).astype(BF16)
            da_ref[s] = da
            db_ref[s] = db
            dn = dn + _dot(da, wg_ref[s]) + _dot(db, wu_ref[s])
        dx, dg = _rms_bwd(hv, r, g, dn)
        dh_ref[...] = dho_ref[...] + dx
        _accum(dg_ref, dg, pl.program_id(0) == 0)

    tok = pl.BlockSpec((tm, D_MODEL), lambda i: (i, 0))
    hid = pl.BlockSpec((N_SHARD, tm, FF_PAD), lambda i: (0, i, 0))
    return _carry(
        body, comm, name=name, steps=t // tm,
        out_shape=(jax.ShapeDtypeStruct((t, D_MODEL), F32),
                   jax.ShapeDtypeStruct((1, D_MODEL), F32),
                   jax.ShapeDtypeStruct((t, D_MODEL), BF16),
                   jax.ShapeDtypeStruct((t, D_MODEL), BF16),
                   jax.ShapeDtypeStruct((N_SHARD, t, FF_PAD), BF16),
                   jax.ShapeDtypeStruct((N_SHARD, t, FF_PAD), BF16),
                   jax.ShapeDtypeStruct((N_SHARD, t, FF_PAD), BF16)),
        in_specs=[tok, tok, hid, hid, _const_spec((1, D_MODEL)),
                  _const_spec((N_SHARD, FF_PAD, D_MODEL)),
                  _const_spec((N_SHARD, FF_PAD, D_MODEL)),
                  _const_spec((N_SHARD, FF_PAD, D_MODEL))],
        out_specs=(tok, pl.BlockSpec((1, D_MODEL), lambda i: (0, 0)), tok, tok, hid, hid, hid),
        args=(dho, h, a, b, gain, wg, wu, wd),
    )


TN_VMEM_BUDGET = 44 * 1024 * 1024


def tn_matmul(x, y, name, shard, comm=None):
    x3, y3 = x.ndim == 3, y.ndim == 3
    t = x.shape[-2]
    m = x.shape[-1] // (N_SHARD if (shard == "rows" and not x3) else 1)
    n = y.shape[-1] // (N_SHARD if (shard == "cols" and not y3) else 1)
    per_token = 2 * (m * x.dtype.itemsize + n * y.dtype.itemsize)
    tk = t
    while tk > 512 and tk * per_token + 2 * m * n * 4 > TN_VMEM_BUDGET:
        tk //= 2
    nk = t // tk

    out_shape = jax.ShapeDtypeStruct((N_SHARD, m, n), F32)
    if nk == 1:
        def whole(x_ref, y_ref, o_ref):
            o_ref[...] = _dot_tn(x_ref[...].astype(BF16), y_ref[...].astype(BF16))

        x_one = (pl.BlockSpec((None, t, m), lambda s: (s, 0, 0)) if x3 else
                 pl.BlockSpec((t, m), (lambda s: (0, s)) if shard == "rows" else (lambda s: (0, 0))))
        y_one = (pl.BlockSpec((None, t, n), lambda s: (s, 0, 0)) if y3 else
                 pl.BlockSpec((t, n), (lambda s: (0, s)) if shard == "cols" else (lambda s: (0, 0))))
        return _carry(whole, comm, name=name, steps=N_SHARD, out_shape=(out_shape,), in_specs=[x_one, y_one],
                      out_specs=(pl.BlockSpec((None, m, n), lambda s: (s, 0, 0)),), args=(x, y))
    assert comm is None

    def body(x_ref, y_ref, o_ref):
        _accum(o_ref, _dot_tn(x_ref[...].astype(BF16), y_ref[...].astype(BF16)), pl.program_id(1) == 0)

    if x3:
        x_spec = pl.BlockSpec((None, tk, m), lambda s, k: (s, k, 0))
    elif shard == "rows":
        x_spec = pl.BlockSpec((tk, m), lambda s, k: (k, s))
    else:
        x_spec = pl.BlockSpec((tk, m), lambda s, k: (k, 0))
    if y3:
        y_spec = pl.BlockSpec((None, tk, n), lambda s, k: (s, k, 0))
    elif shard == "cols":
        y_spec = pl.BlockSpec((tk, n), lambda s, k: (k, s))
    else:
        y_spec = pl.BlockSpec((tk, n), lambda s, k: (k, 0))
    res = pl.pallas_call(
        body, name=name, grid=(N_SHARD, nk),
        out_shape=out_shape,
        in_specs=[x_spec, y_spec],
        out_specs=pl.BlockSpec((None, m, n), lambda s, k: (s, 0, 0)),
        compiler_params=_cparams(("arbitrary", "arbitrary")),
    )(x, y)
    return (res,), ()


def inproj_fwd(h, gain, w_in, tm=256):
    t = h.shape[0]
    widths = [IN_SPLITS[j + 1] - IN_SPLITS[j] for j in range(7)]
    sh_cols = IN_COLS // N_SHARD

    def body(h_ref, g_ref, w_ref, *outs):
        n, _ = _rms_fwd(h_ref[...], g_ref[...])
        nb = n.astype(BF16)
        proj = jnp.concatenate([_dot(nb, w_ref[s]) for s in range(N_SHARD)], axis=1)
        for j, o_ref in enumerate(outs):
            o_ref[...] = proj[:, IN_SPLITS[j]:IN_SPLITS[j + 1]]

    return pl.pallas_call(
        body, name="inproj_fwd", grid=(t // tm,),
        out_shape=tuple(jax.ShapeDtypeStruct((t, w), F32) for w in widths),
        in_specs=[pl.BlockSpec((tm, D_MODEL), lambda i: (i, 0)),
                  _const_spec((1, D_MODEL)),
                  _const_spec((N_SHARD, D_MODEL, sh_cols))],
        out_specs=tuple(pl.BlockSpec((tm, w), lambda i: (i, 0)) for w in widths),
        compiler_params=_cparams(("arbitrary",)),
    )(h, gain, w_in)


def inproj_bwd(dres, h, gain, w_in, dparts, comm=None, tm=256):
    t = h.shape[0]
    widths = [IN_SPLITS[j + 1] - IN_SPLITS[j] for j in range(7)]
    sh_cols = IN_COLS // N_SHARD

    def body(dres_ref, h_ref, g_ref, w_ref, d0, d1, d2, d3, d4, d5, d6, dh_ref, dg_ref, nb_ref, dp_ref):
        hv = h_ref[...]
        g = g_ref[...]
        n, r = _rms_fwd(hv, g)
        nb_ref[...] = n.astype(BF16)
        dproj = jnp.concatenate([d[...] for d in (d0, d1, d2, d3, d4, d5, d6)], axis=1).astype(BF16)
        dp_ref[...] = dproj
        dn = jnp.zeros((tm, D_MODEL), F32)
        for s in range(N_SHARD):
            dn = dn + _dot_nt(dproj[:, s * sh_cols:(s + 1) * sh_cols], w_ref[s])
        dx, dg = _rms_bwd(hv, r, g, dn)
        dh_ref[...] = dres_ref[...] + dx
        _accum(dg_ref, dg, pl.program_id(0) == 0)

    tok = pl.BlockSpec((tm, D_MODEL), lambda i: (i, 0))
    return _carry(
        body, comm, name="inproj_bwd", steps=t // tm,
        out_shape=(jax.ShapeDtypeStruct((t, D_MODEL), F32),
                   jax.ShapeDtypeStruct((1, D_MODEL), F32),
                   jax.ShapeDtypeStruct((t, D_MODEL), BF16),
                   jax.ShapeDtypeStruct((t, IN_COLS), BF16)),
        in_specs=[tok, tok, _const_spec((1, D_MODEL)), _const_spec((N_SHARD, D_MODEL, sh_cols))]
                 + [pl.BlockSpec((tm, w), lambda i: (i, 0)) for w in widths],
        out_specs=(tok, pl.BlockSpec((1, D_MODEL), lambda i: (0, 0)), tok,
                   pl.BlockSpec((tm, IN_COLS), lambda i: (i, 0))),
        args=(dres, h, gain, w_in, *dparts),
    )


def s5_prep(lam_re, lam_im, log_dt, b_re, b_im, c_re, c_im):
    dt = jnp.exp(log_dt)[:, None]
    mag = jnp.exp(lam_re * dt)
    lbr = mag * jnp.cos(lam_im * dt)
    lbi = mag * jnp.sin(lam_im * dt)
    den = lam_re * lam_re + lam_im * lam_im
    nr, ni = lbr - 1.0, lbi
    kr = (nr * lam_re + ni * lam_im) / den
    ki = (ni * lam_re - nr * lam_im) / den
    bbr = kr[..., None] * b_re - ki[..., None] * b_im
    bbi = kr[..., None] * b_im + ki[..., None] * b_re
    eye = jnp.eye(16, dtype=F32)

    def bm(bp):
        return jnp.einsum('kgph,gG->kghGp', bp.reshape(S5_KT, 16, S5_STATE, S5_GROUP), eye).reshape(S5_KT, 256, 1024)

    def cm(cp):
        return jnp.einsum('kghp,gG->kgpGh', cp.reshape(S5_KT, 16, S5_GROUP, S5_STATE), eye).reshape(S5_KT, 1024, 256)

    lam_bar = jnp.stack([lbr.reshape(S5_N), lbi.reshape(S5_N)])
    bmat = jnp.stack([bm(bbr), bm(bbi)])
    cmat = jnp.stack([cm(c_re), -cm(c_im)])
    return lam_bar, bmat, cmat


def _lam_powers(lam_bar):
    lr, li = lam_bar[0], lam_bar[1]
    pr, pi = [lr], [li]
    for _ in range(7):
        pr, pi = pr + [pr[-1] * lr - pi[-1] * li], pi + [pr[-1] * li + pi[-1] * lr]
    return jnp.stack(pr), jnp.stack(pi)


SCAN_SHIFTS = ((1, 0), (2, 1), (4, 3))


def _scan_tables(pw_r, pw_i, reverse):
    rows = jnp.arange(8)[:, None]
    planes_r, planes_i = [], []
    for sh, idx in SCAN_SHIFTS:
        keep = (rows < 8 - sh) if reverse else (rows >= sh)
        planes_r.append(jnp.where(keep, pw_r[idx:idx + 1], 0.0))
        planes_i.append(jnp.where(keep, pw_i[idx:idx + 1], 0.0))
    carry = [pw_r[::-1], pw_i[::-1]] if reverse else [pw_r, pw_i]
    return jnp.stack(planes_r + planes_i + carry)


def s5_fwd(u, tab, bmat, cmat, dvec, comm=None, tm=256):
    t = u.shape[0]
    nch = tm // 8

    def body(u_ref, tab_ref, b_ref, c_ref, d_ref, y_ref, xp_ref, x_scr, carry):
        @pl.when(pl.program_id(0) == 0)
        def _():
            carry[...] = jnp.zeros_like(carry)

        uv = u_ref[...]
        ub = uv.astype(BF16)
        for part in range(2):
            for kt in range(S5_KT):
                x_scr[:, pl.ds(part * S5_N + kt * 1024, 1024)] = _dot(ub[:, kt * 256:(kt + 1) * 256], b_ref[part, kt])
        row = lax.broadcasted_iota(jnp.int32, (8, S5_N), 0)

        def chunk(i, c):
            cr, ci = c
            r0 = pl.multiple_of(i * 8, 8)
            xr = x_scr[pl.ds(r0, 8), pl.ds(0, S5_N)]
            xi = x_scr[pl.ds(r0, 8), pl.ds(S5_N, S5_N)]
            for lvl, (sh, _) in enumerate(SCAN_SHIFTS):
                sr = pltpu.roll(xr, sh, 0)
                si = pltpu.roll(xi, sh, 0)
                lr = tab_ref[lvl]
                li = tab_ref[3 + lvl]
                xr, xi = xr + lr * sr - li * si, xi + lr * si + li * sr
            pwr = tab_ref[6]
            pwi = tab_ref[7]
            xr, xi = xr + pwr * cr - pwi * ci, xi + pwr * ci + pwi * cr
            x_scr[pl.ds(r0, 8), pl.ds(0, S5_N)] = xr
            x_scr[pl.ds(r0, 8), pl.ds(S5_N, S5_N)] = xi
            xp_ref[pl.ds(r0, 8), pl.ds(0, S5_N)] = jnp.where(row == 0, cr, pltpu.roll(xr, 1, 0))
            xp_ref[pl.ds(r0, 8), pl.ds(S5_N, S5_N)] = jnp.where(row == 0, ci, pltpu.roll(xi, 1, 0))
            return xr[7:8, :], xi[7:8, :]

        cr, ci = lax.fori_loop(0, nch, chunk, (carry[0:1, :], carry[1:2, :]))
        carry[0:1, :] = cr
        carry[1:2, :] = ci
        for kt in range(S5_KT):
            acc = jnp.zeros((tm, 256), F32)
            for part in range(2):
                acc = acc + _dot(x_scr[:, pl.ds(part * S5_N + kt * 1024, 1024)].astype(BF16), c_ref[part, kt])
            y_ref[:, pl.ds(kt * 256, 256)] = acc + d_ref[:, pl.ds(kt * 256, 256)] * uv[:, kt * 256:(kt + 1) * 256]

    return _carry(
        body, comm, name="s5_fwd", steps=t // tm,
        out_shape=(jax.ShapeDtypeStruct((t, S5_WIDTH), F32),
                   jax.ShapeDtypeStruct((t, 2 * S5_N), F32)),
        in_specs=[pl.BlockSpec((tm, S5_WIDTH), lambda i: (i, 0)),
                  _const_spec((8, 8, S5_N)),
                  _const_spec((2, S5_KT, 256, 1024)), _const_spec((2, S5_KT, 1024, 256)),
                  _const_spec((1, S5_WIDTH))],
        out_specs=(pl.BlockSpec((tm, S5_WIDTH), lambda i: (i, 0)),
                   pl.BlockSpec((tm, 2 * S5_N), lambda i: (i, 0))),
        scratch_shapes=[pltpu.VMEM((tm, 2 * S5_N), F32), pltpu.VMEM((8, S5_N), F32)],
        args=(u, tab, bmat, cmat, dvec),
    )


def s5_bwd(dy, u, xp, tab, bmat, bmat_t, cmat_t, dvec, comm=None, tm=256):
    t = u.shape[0]
    nt = t // tm
    nch = tm // 8

    def body(dy_ref, u_ref, xp_ref, tab_ref, b_ref, bt_ref, ct_ref, d_ref,
             du_ref, db_ref, dc_ref, dl_ref, dd_ref, g_scr, x_scr, carry):
        first = pl.program_id(0) == 0

        @pl.when(first)
        def _():
            carry[...] = jnp.zeros_like(carry)
            dl_ref[...] = jnp.zeros_like(dl_ref)

        dyv = dy_ref[...]
        uv = u_ref[...]
        dyb = dyv.astype(BF16)
        ub = uv.astype(BF16)
        lr1 = tab_ref[6, 7:8, :]
        li1 = tab_ref[7, 7:8, :]
        for kt in range(S5_KT):
            cols = pl.ds(kt * 1024, 1024)
            colsi = pl.ds(S5_N + kt * 1024, 1024)
            g_scr[:, cols] = _dot(dyb[:, kt * 256:(kt + 1) * 256], ct_ref[0, kt])
            g_scr[:, colsi] = _dot(dyb[:, kt * 256:(kt + 1) * 256], ct_ref[1, kt])
            bur = _dot(ub[:, kt * 256:(kt + 1) * 256], b_ref[0, kt])
            bui = _dot(ub[:, kt * 256:(kt + 1) * 256], b_ref[1, kt])
            xpr = xp_ref[:, cols]
            xpi = xp_ref[:, colsi]
            lrk = lr1[:, kt * 1024:(kt + 1) * 1024]
            lik = li1[:, kt * 1024:(kt + 1) * 1024]
            x_scr[:, cols] = lrk * xpr - lik * xpi + bur
            x_scr[:, colsi] = lrk * xpi + lik * xpr + bui

        def chunk(j, c):
            cr, ci = c
            r0 = pl.multiple_of((nch - 1 - j) * 8, 8)
            gr = g_scr[pl.ds(r0, 8), pl.ds(0, S5_N)]
            gi = g_scr[pl.ds(r0, 8), pl.ds(S5_N, S5_N)]
            for lvl, (sh, _) in enumerate(SCAN_SHIFTS):
                sr = pltpu.roll(gr, 8 - sh, 0)
                si = pltpu.roll(gi, 8 - sh, 0)
                lr = tab_ref[lvl]
                li = tab_ref[3 + lvl]
                gr, gi = gr + lr * sr + li * si, gi + lr * si - li * sr
            pvr = tab_ref[6]
            pvi = tab_ref[7]
            gr, gi = gr + pvr * cr + pvi * ci, gi + pvr * ci - pvi * cr
            g_scr[pl.ds(r0, 8), pl.ds(0, S5_N)] = gr
            g_scr[pl.ds(r0, 8), pl.ds(S5_N, S5_N)] = gi
            xpr = xp_ref[pl.ds(r0, 8), pl.ds(0, S5_N)]
            xpi = xp_ref[pl.ds(r0, 8), pl.ds(S5_N, S5_N)]
            dl_ref[0] += gr * xpr + gi * xpi
            dl_ref[1] += gi * xpr - gr * xpi
            return gr[0:1, :], gi[0:1, :]

        cr, ci = lax.fori_loop(0, nch, chunk, (carry[0:1, :], carry[1:2, :]))
        carry[0:1, :] = cr
        carry[1:2, :] = ci

        for kt in range(S5_KT):
            du = jnp.zeros((tm, 256), F32)
            ukt = ub[:, kt * 256:(kt + 1) * 256]
            dykt = dyb[:, kt * 256:(kt + 1) * 256]
            for part in range(2):
                gb = g_scr[:, pl.ds(part * S5_N + kt * 1024, 1024)].astype(BF16)
                xb = x_scr[:, pl.ds(part * S5_N + kt * 1024, 1024)].astype(BF16)
                du = du + _dot(gb, bt_ref[part, kt])
                dbv = _dot_tn(ukt, gb)
                dcv = _dot_tn(xb, dykt)

                @pl.when(first)
                def _():
                    db_ref[part, kt] = dbv
                    dc_ref[part, kt] = dcv

                @pl.when(jnp.logical_not(first))
                def _():
                    db_ref[part, kt] += dbv
                    dc_ref[part, kt] += dcv
            du_ref[:, pl.ds(kt * 256, 256)] = du + d_ref[:, pl.ds(kt * 256, 256)] * dyv[:, kt * 256:(kt + 1) * 256]
        _accum(dd_ref, jnp.sum(dyv * uv, axis=0, keepdims=True), first)

    rev = lambda i: (nt - 1 - i, 0)
    return _carry(
        body, comm, name="s5_bwd", steps=nt,
        out_shape=(jax.ShapeDtypeStruct((t, S5_WIDTH), F32),
                   jax.ShapeDtypeStruct((2, S5_KT, 256, 1024), F32),
                   jax.ShapeDtypeStruct((2, S5_KT, 1024, 256), F32),
                   jax.ShapeDtypeStruct((2, 8, S5_N), F32),
                   jax.ShapeDtypeStruct((1, S5_WIDTH), F32)),
        in_specs=[pl.BlockSpec((tm, S5_WIDTH), rev), pl.BlockSpec((tm, S5_WIDTH), rev),
                  pl.BlockSpec((tm, 2 * S5_N), rev),
                  _const_spec((8, 8, S5_N)),
                  _const_spec((2, S5_KT, 256, 1024)), _const_spec((2, S5_KT, 1024, 256)),
                  _const_spec((2, S5_KT, 256, 1024)), _const_spec((1, S5_WIDTH))],
        out_specs=(pl.BlockSpec((tm, S5_WIDTH), rev),
                   pl.BlockSpec((2, S5_KT, 256, 1024), lambda i: (0, 0, 0, 0)),
                   pl.BlockSpec((2, S5_KT, 1024, 256), lambda i: (0, 0, 0, 0)),
                   pl.BlockSpec((2, 8, S5_N), lambda i: (0, 0, 0)),
                   pl.BlockSpec((1, S5_WIDTH), lambda i: (0, 0))),
        scratch_shapes=[pltpu.VMEM((tm, 2 * S5_N), F32), pltpu.VMEM((tm, 2 * S5_N), F32),
                        pltpu.VMEM((8, S5_N), F32)],
        args=(dy, u, xp, tab, bmat, bmat_t, cmat_t, dvec),
    )


def _hg_gates(z, lb):
    sg = _sigmoid(z)
    sgn = _sigmoid(-z)
    fg = lb + (1.0 - lb) * sg
    return sg, sgn, fg, jnp.log(fg), (1.0 - lb) * sgn


def _hg_decays(g, tril):
    gc = jnp.dot(tril, g, precision=HIGHEST, preferred_element_type=F32)
    mid = gc[CHUNK // 2 - 1:CHUNK // 2, :]
    last = gc[CHUNK - 1:CHUNK, :]
    return jnp.exp(gc), jnp.exp(gc - mid), jnp.exp(mid - gc), jnp.exp(last - gc), jnp.exp(last)


def _split_bf16(x):
    hi = x.astype(BF16)
    return hi, (x - hi.astype(F32)).astype(BF16)


def _hg_scores(qt, qlo, kt, klo, sl, causal):
    a = _dot_nt(qt[:, sl], kt[:, sl]) + _dot_nt(qt[:, sl], klo[:, sl]) + _dot_nt(qlo[:, sl], kt[:, sl])
    return jnp.where(causal, a, 0.0).astype(BF16)


def hgrn_fwd(q, f, v, lb):
    t = q.shape[0]
    nc = t // CHUNK
    scale = HG_E ** -0.5

    def body(q_ref, f_ref, v_ref, lb_ref, o_ref, st_ref, state):
        @pl.when(pl.program_id(0) == 0)
        def _():
            state[...] = jnp.zeros_like(state)

        ri = lax.broadcasted_iota(jnp.int32, (CHUNK, CHUNK), 0)
        ci = lax.broadcasted_iota(jnp.int32, (CHUNK, CHUNK), 1)
        causal = ri >= ci
        tril = causal.astype(F32)
        for sub in range(HG_SUB):
            rows = pl.ds(sub * CHUNK, CHUNK)
            _, _, _, g, k = _hg_gates(f_ref[rows, :], lb_ref[...])
            eg, eq, ek, ed, el = _hg_decays(g, tril)
            qs = q_ref[rows, :] * scale
            qg = (qs * eg).astype(BF16)
            qt, qlo = _split_bf16(qs * eq)
            kt, klo = _split_bf16(k * ek)
            kd = (k * ed).astype(BF16)
            vb = v_ref[rows, :].astype(BF16)
            for h in range(HG_HEADS):
                sl = slice(h * HG_E, (h + 1) * HG_E)
                st = state[h]
                a = _hg_scores(qt, qlo, kt, klo, sl, causal)
                o_ref[rows, sl] = _dot(a, vb[:, sl]) + _dot_nt(qg[:, sl], st.astype(BF16))
                st_new = st * el[:, sl] + _dot_tn(vb[:, sl], kd[:, sl])
                state[h] = st_new
                st_ref[sub, h] = st_new

    tok = pl.BlockSpec((HG_SUB * CHUNK, HG_WIDTH), lambda i: (i, 0))
    return pl.pallas_call(
        body, name="hgrn_fwd", grid=(nc // HG_SUB,),
        out_shape=(jax.ShapeDtypeStruct((t, HG_WIDTH), F32),
                   jax.ShapeDtypeStruct((nc, HG_HEADS, HG_E, HG_E), F32)),
        in_specs=[tok, tok, tok, _const_spec((1, HG_WIDTH))],
        out_specs=(tok, pl.BlockSpec((HG_SUB, HG_HEADS, HG_E, HG_E), lambda i: (i, 0, 0, 0))),
        scratch_shapes=[pltpu.VMEM((HG_HEADS, HG_E, HG_E), F32)],
        compiler_params=_cparams(("arbitrary",)),
    )(q, f, v, lb)


def hgrn_bwd(do, q, f, v, lb, states, comm=None):
    t = q.shape[0]
    nc = t // CHUNK
    scale = HG_E ** -0.5

    ns = nc // HG_SUB

    def body(do_ref, q_ref, f_ref, v_ref, lb_ref, scur_ref, sprev_ref, dq_ref, df_ref, dv_ref, dlb_ref, dstate):
        first = pl.program_id(0) == 0
        has_prev = jnp.where(pl.program_id(0) < ns - 1, 1.0, 0.0)

        @pl.when(first)
        def _():
            dstate[...] = jnp.zeros_like(dstate)

        ri = lax.broadcasted_iota(jnp.int32, (CHUNK, CHUNK), 0)
        ci = lax.broadcasted_iota(jnp.int32, (CHUNK, CHUNK), 1)
        causal = ri >= ci
        tril = causal.astype(F32)
        triu = (ri <= ci).astype(F32)
        rowc = lax.broadcasted_iota(jnp.int32, (CHUNK, HG_WIDTH), 0)
        lb = lb_ref[...]
        dlb = jnp.zeros((1, HG_WIDTH), F32)
        for sub in reversed(range(HG_SUB)):
            rows = pl.ds(sub * CHUNK, CHUNK)
            sg, sgn, fg, g, k = _hg_gates(f_ref[rows, :], lb)
            eg, eq, ek, ed, el = _hg_decays(g, tril)
            qs = q_ref[rows, :] * scale
            qg = (qs * eg).astype(BF16)
            qt, qlo = _split_bf16(qs * eq)
            kt, klo = _split_bf16(k * ek)
            kd = (k * ed).astype(BF16)
            vb = v_ref[rows, :].astype(BF16)
            dob = do_ref[rows, :].astype(BF16)
            dqs_l, dk_l, dgc_l, dgl_l = [], [], [], []
            for h in range(HG_HEADS):
                sl = slice(h * HG_E, (h + 1) * HG_E)
                s0 = scur_ref[sub - 1, h] if sub > 0 else sprev_ref[HG_SUB - 1, h] * has_prev
                ds1 = dstate[h]
                ds1b = ds1.astype(BF16)
                a = _hg_scores(qt, qlo, kt, klo, sl, causal)
                da = jnp.where(causal, _dot_nt(dob[:, sl], vb[:, sl]), 0.0).astype(BF16)
                dv_ref[rows, sl] = _dot_tn(a, dob[:, sl]) + _dot_nt(kd[:, sl], ds1b)
                dkd = _dot(vb[:, sl], ds1b)
                dqt = _dot(da, kt[:, sl])
                dkt = _dot_tn(da, qt[:, sl])
                dqg = _dot(dob[:, sl], s0.astype(BF16))
                dqs_l.append(dqt * eq[:, sl] + dqg * eg[:, sl])
                dk_l.append(dkt * ek[:, sl] + dkd * ed[:, sl])
                kd_dkd = kd[:, sl].astype(F32) * dkd
                dgc_l.append(qt[:, sl].astype(F32) * dqt - kt[:, sl].astype(F32) * dkt
                             + qg[:, sl].astype(F32) * dqg - kd_dkd)
                dgl_l.append(el[:, sl] * jnp.sum(ds1 * s0, axis=0, keepdims=True)
                             + jnp.sum(kd_dkd, axis=0, keepdims=True))
                dstate[h] = ds1 * el[:, sl] + _dot_tn(dob[:, sl], qg[:, sl])
            dqs = jnp.concatenate(dqs_l, axis=1)
            dk = jnp.concatenate(dk_l, axis=1)
            dgl = jnp.concatenate(dgl_l, axis=1)
            dq_ref[rows, :] = dqs * scale
            dgc = jnp.concatenate(dgc_l, axis=1) + jnp.where(rowc == CHUNK - 1, dgl, 0.0)
            dg = jnp.dot(triu, dgc, precision=HIGHEST, preferred_element_type=F32)
            w = dg / fg - dk
            df_ref[rows, :] = w * (1.0 - lb) * sg * sgn
            dlb = dlb + jnp.sum(w * sgn, axis=0, keepdims=True)
        _accum(dlb_ref, dlb, first)

    rev = lambda i: (ns - 1 - i, 0)
    tok = pl.BlockSpec((HG_SUB * CHUNK, HG_WIDTH), rev)
    st_blk = (HG_SUB, HG_HEADS, HG_E, HG_E)
    return _carry(
        body, comm, name="hgrn_bwd", steps=ns,
        out_shape=(jax.ShapeDtypeStruct((t, HG_WIDTH), F32),
                   jax.ShapeDtypeStruct((t, HG_WIDTH), F32),
                   jax.ShapeDtypeStruct((t, HG_WIDTH), F32),
                   jax.ShapeDtypeStruct((1, HG_WIDTH), F32)),
        in_specs=[tok, tok, tok, tok, _const_spec((1, HG_WIDTH)),
                  pl.BlockSpec(st_blk, lambda i: (ns - 1 - i, 0, 0, 0)),
                  pl.BlockSpec(st_blk, lambda i: (jnp.maximum(ns - 2 - i, 0), 0, 0, 0))],
        out_specs=(tok, tok, tok, pl.BlockSpec((1, HG_WIDTH), lambda i: (0, 0))),
        scratch_shapes=[pltpu.VMEM((HG_HEADS, HG_E, HG_E), F32)],
        args=(do, q, f, v, lb, states, states),
    )


GELU_C = math.sqrt(2.0 / math.pi)


def _gelu(x):
    th = jnp.tanh(GELU_C * (x + 0.044715 * x * x * x))
    return 0.5 * x * (1.0 + th), th


def _merge_core(ys5, o, og, ga, gb, wv_ref, wt_ref, ghg, who_ref):
    ys, th = _gelu(ys5)
    ysb = ys.astype(BF16)
    va = jnp.concatenate([_dot(ysb, wv_ref[s]) for s in range(N_SHARD)], axis=1)
    vt = jnp.concatenate([_dot(ysb, wt_ref[s]) for s in range(N_SHARD)], axis=1)
    svt = _sigmoid(vt)
    ya = va * svt
    rs, ons = [], []
    for h in range(HG_HEADS):
        oh = o[:, h * HG_E:(h + 1) * HG_E]
        r = lax.rsqrt(jnp.mean(oh * oh, axis=-1, keepdims=True) + NORM_EPS)
        rs.append(r)
        ons.append(oh * r)
    on = jnp.concatenate(ons, axis=1)
    sgo = _sigmoid(og)
    o2 = on * ghg * (og * sgo)
    o2b = o2.astype(BF16)
    yb = _dot(o2b, who_ref[...])
    sa = _sigmoid(ga)
    sb = _sigmoid(gb)
    mixed = sa * ya + sb * yb
    return dict(ys=ys, th=th, ysb=ysb, va=va, svt=svt, ya=ya, rs=rs, on=on, sgo=sgo, o2b=o2b, yb=yb,
                sa=sa, sb=sb, mixed=mixed)


def merge_fwd(h, ys5, o, og, ga, gb, wv, wt, ghg, who, wmo, tm=256):
    t = h.shape[0]

    def body(h_ref, ys5_ref, o_ref, og_ref, ga_ref, gb_ref, wv_ref, wt_ref, ghg_ref, who_ref, wmo_ref, out_ref):
        c = _merge_core(ys5_ref[...], o_ref[...], og_ref[...], ga_ref[...], gb_ref[...],
                        wv_ref, wt_ref, ghg_ref[...], who_ref)
        out_ref[...] = h_ref[...] + _dot(c["mixed"].astype(BF16), wmo_ref[...])

    tok = pl.BlockSpec((tm, D_MODEL), lambda i: (i, 0))
    return pl.pallas_call(
        body, name="merge_fwd", grid=(t // tm,),
        out_shape=jax.ShapeDtypeStruct((t, D_MODEL), F32),
        in_specs=[tok, pl.BlockSpec((tm, S5_WIDTH), lambda i: (i, 0)), tok, tok, tok, tok,
                  _const_spec((N_SHARD, S5_WIDTH, 256)), _const_spec((N_SHARD, S5_WIDTH, 256)),
                  _const_spec((1, HG_WIDTH)), _const_spec((HG_WIDTH, D_MODEL)), _const_spec((D_MODEL, D_MODEL))],
        out_specs=tok,
        compiler_params=_cparams(("arbitrary",)),
    )(h, ys5, o, og, ga, gb, wv, wt, ghg, who, wmo)


def merge_bwd(dh, ys5, o, og, ga, gb, wv, wt, ghg, who, wmo, comm=None, tm=256):
    t = dh.shape[0]

    def body(dh_ref, ys5_ref, o_ref, og_ref, ga_ref, gb_ref, wv_ref, wt_ref, ghg_ref, who_ref, wmo_ref,
             dys5_ref, do_ref, dog_ref, dga_ref, dgb_ref, dghg_ref,
             mixb_ref, dhb_ref, ysb_ref, dvab_ref, dvtb_ref, o2b_ref, dybb_ref):
        ys5 = ys5_ref[...]
        o = o_ref[...]
        og = og_ref[...]
        ghg = ghg_ref[...]
        c = _merge_core(ys5, o, og, ga_ref[...], gb_ref[...], wv_ref, wt_ref, ghg, who_ref)
        dhb = dh_ref[...].astype(BF16)
        dhb_ref[...] = dhb
        mixb_ref[...] = c["mixed"].astype(BF16)
        ysb_ref[...] = c["ysb"]
        o2b_ref[...] = c["o2b"]
        dmix = _dot_nt(dhb, wmo_ref[...])
        sa, sb = c["sa"], c["sb"]
        dya = dmix * sa
        dyb = dmix * sb
        dga_ref[...] = dmix * c["ya"] * sa * (1.0 - sa)
        dgb_ref[...] = dmix * c["yb"] * sb * (1.0 - sb)
        svt = c["svt"]
        dva = (dya * svt).astype(BF16)
        dvt = (dya * c["va"] * svt * (1.0 - svt)).astype(BF16)
        dvab_ref[...] = dva
        dvtb_ref[...] = dvt
        dys = jnp.zeros((tm, S5_WIDTH), F32)
        for s in range(N_SHARD):
            dys = dys + _dot_nt(dva[:, s * 256:(s + 1) * 256], wv_ref[s]) + _dot_nt(dvt[:, s * 256:(s + 1) * 256], wt_ref[s])
        th = c["th"]
        dgelu = 0.5 * (1.0 + th) + 0.5 * ys5 * (1.0 - th * th) * GELU_C * (1.0 + 3.0 * 0.044715 * ys5 * ys5)
        dys5_ref[...] = dys * dgelu
        dybb = dyb.astype(BF16)
        dybb_ref[...] = dybb
        do2 = _dot_nt(dybb, who_ref[...])
        sgo = c["sgo"]
        sil = og * sgo
        on = c["on"]
        dog_ref[...] = do2 * on * ghg * (sgo * (1.0 + og * (1.0 - sgo)))
        _accum(dghg_ref, jnp.sum(do2 * on * sil, axis=0, keepdims=True), pl.program_id(0) == 0)
        don = do2 * ghg * sil
        dos = []
        for h in range(HG_HEADS):
            sl = slice(h * HG_E, (h + 1) * HG_E)
            m = jnp.mean(don[:, sl] * on[:, sl], axis=-1, keepdims=True)
            dos.append(c["rs"][h] * (don[:, sl] - on[:, sl] * m))
        do_ref[...] = jnp.concatenate(dos, axis=1)

    tok = pl.BlockSpec((tm, D_MODEL), lambda i: (i, 0))
    s5b = pl.BlockSpec((tm, S5_WIDTH), lambda i: (i, 0))
    f32t = jax.ShapeDtypeStruct((t, D_MODEL), F32)
    bft = jax.ShapeDtypeStruct((t, D_MODEL), BF16)
    return _carry(
        body, comm, name="merge_bwd", steps=t // tm,
        out_shape=(jax.ShapeDtypeStruct((t, S5_WIDTH), F32), f32t, f32t, f32t, f32t,
                   jax.ShapeDtypeStruct((1, HG_WIDTH), F32),
                   bft, bft, jax.ShapeDtypeStruct((t, S5_WIDTH), BF16), bft, bft, bft, bft),
        in_specs=[tok, s5b, tok, tok, tok, tok,
                  _const_spec((N_SHARD, S5_WIDTH, 256)), _const_spec((N_SHARD, S5_WIDTH, 256)),
                  _const_spec((1, HG_WIDTH)), _const_spec((HG_WIDTH, D_MODEL)), _const_spec((D_MODEL, D_MODEL))],
        out_specs=(s5b, tok, tok, tok, tok, pl.BlockSpec((1, HG_WIDTH), lambda i: (0, 0)),
                   tok, tok, s5b, tok, tok, tok, tok),
        args=(dh, ys5, o, og, ga, gb, wv, wt, ghg, who, wmo),
    )


def head_fwd_bwd(h, p, tgt, gple, wpg, wpp, gfin, tm=256):
    t = h.shape[0]

    def body(h_ref, p_ref, tgt_ref, gple_ref, wpg_ref, wpp_ref, gfin_ref,
             loss_ref, dh_ref, dgple_ref, dgfin_ref, nb_ref, dlb_ref, dppb_ref):
        first = pl.program_id(0) == 0
        hv = h_ref[...]
        gple = gple_ref[...]
        gfin = gfin_ref[...]
        n, r3 = _rms_fwd(hv, gple)
        nb = n.astype(BF16)
        nb_ref[...] = nb
        pg = _sigmoid(_dot(nb, wpg_ref[...]))
        pb = p_ref[...].astype(BF16)
        pp = jnp.concatenate([_dot(pb, wpp_ref[s]) for s in range(N_SHARD)], axis=1)
        h4 = hv + pg * pp
        y, r4 = _rms_fwd(h4, gfin)
        err = y - tgt_ref[...]
        lsum = 0.5 * jnp.sum(jnp.sum(err * err, axis=-1, keepdims=True), axis=0, keepdims=True) / D_MODEL
        _accum(loss_ref, jnp.broadcast_to(lsum, (8, 128)), first)
        dy = err * (1.0 / D_MODEL)
        dh4, dgf = _rms_bwd(h4, r4, gfin, dy)
        _accum(dgfin_ref, dgf, first)
        dpp = dh4 * pg
        dppb_ref[...] = dpp.astype(BF16)
        dl = (dh4 * pp * pg * (1.0 - pg)).astype(BF16)
        dlb_ref[...] = dl
        dn = _dot_nt(dl, wpg_ref[...])
        dx, dgp = _rms_bwd(hv, r3, gple, dn)
        _accum(dgple_ref, dgp, first)
        dh_ref[...] = dh4 + dx

    tok = pl.BlockSpec((tm, D_MODEL), lambda i: (i, 0))
    vec = pl.BlockSpec((1, D_MODEL), lambda i: (0, 0))
    bft = jax.ShapeDtypeStruct((t, D_MODEL), BF16)
    return pl.pallas_call(
        body, name="head_fwd_bwd", grid=(t // tm,),
        out_shape=(jax.ShapeDtypeStruct((8, 128), F32), jax.ShapeDtypeStruct((t, D_MODEL), F32),
                   jax.ShapeDtypeStruct((1, D_MODEL), F32), jax.ShapeDtypeStruct((1, D_MODEL), F32),
                   bft, bft, bft),
        in_specs=[tok, pl.BlockSpec((tm, PLE_DIM), lambda i: (i, 0)), tok,
                  _const_spec((1, D_MODEL)), _const_spec((D_MODEL, D_MODEL)),
                  _const_spec((N_SHARD, PLE_DIM, 256)), _const_spec((1, D_MODEL))],
        out_specs=(pl.BlockSpec((8, 128), lambda i: (0, 0)), tok, vec, vec, tok, tok, tok),
        compiler_params=_cparams(("arbitrary",)),
    )(h, p, tgt, gple, wpg, wpp, gfin)


BIG = ("ffn1_w_gate", "ffn1_w_up", "ffn1_w_down", "w_in", "s5_glu_val", "s5_glu_gate", "hg_w_out",
       "w_merge_out", "ffn2_w_gate", "ffn2_w_up", "ffn2_w_down", "ple_w_gate", "ple_w_proj")
FFN_T = ("ffn1_w_gate", "ffn1_w_up", "ffn2_w_gate", "ffn2_w_up")
BIG_SHARD = {
    "ffn1_w_gate": (FF_PAD, D_MODEL), "ffn1_w_up": (FF_PAD, D_MODEL), "ffn1_w_down": (FF_PAD, D_MODEL),
    "ffn2_w_gate": (FF_PAD, D_MODEL), "ffn2_w_up": (FF_PAD, D_MODEL), "ffn2_w_down": (FF_PAD, D_MODEL),
    "w_in": (D_MODEL, IN_COLS // N_SHARD), "s5_glu_val": (S5_WIDTH, 256), "s5_glu_gate": (S5_WIDTH, 256),
    "hg_w_out": (256, D_MODEL), "w_merge_out": (256, D_MODEL), "ple_w_gate": (256, D_MODEL),
    "ple_w_proj": (PLE_DIM, 256),
}


def _lower_bound(hb):
    return jax.nn.softmax(hb, axis=0)[0:1]


class Schedule:
    def __init__(self, wts):
        self.wts = dict(wts)
        self.grads = {}

    def before(self, kernel_name):
        return None

    def after(self, kernel_name, results):
        pass

    def grad(self, name, g):
        self.grads[name] = g


def local_step(x, p, tgt, sched, sm):
    wts = sched.wts
    rows_full = lambda w: w.reshape(N_SHARD * w.shape[1], w.shape[2])

    def carried(kernel_name, fn, *args):
        outs, results = fn(*args, comm=sched.before(kernel_name))
        sched.after(kernel_name, results)
        return outs

    def weight_grad(name, xs, ys, shard):
        kernel_name = "g_" + name
        (g,), results = tn_matmul(xs, ys, kernel_name, shard, comm=sched.before(kernel_name))
        sched.grad(name, g)
        sched.after(kernel_name, results)

    lb, lb_vjp = jax.vjp(_lower_bound, sm["hg_lower_bound"])
    s5_names = ("s5_lam_re", "s5_lam_im", "s5_log_dt", "s5_b_re", "s5_b_im", "s5_c_re", "s5_c_im")
    (lam_bar, bmat, cmat), s5_vjp = jax.vjp(s5_prep, *[sm[k] for k in s5_names])
    pw_r, pw_i = _lam_powers(lam_bar)
    bmat_b = bmat.astype(BF16)
    cmat_b = cmat.astype(BF16)
    bmat_t = jnp.swapaxes(bmat, -1, -2).astype(BF16)
    cmat_t = jnp.swapaxes(cmat, -1, -2).astype(BF16)

    h1, a1, b1 = carried("ffn1_fwd", ffn_fwd, x, sm["ffn1_norm"], wts["ffn1_w_gate"], wts["ffn1_w_up"],
                         wts["ffn1_w_down"], "ffn1_fwd")
    s5in, q, f, v, og, ga, gb = inproj_fwd(h1, sm["mix_norm"], wts["w_in"])
    ys5, xp = carried("s5_fwd", s5_fwd, s5in, _scan_tables(pw_r, pw_i, False), bmat_b, cmat_b, sm["s5_d"])
    o, states = hgrn_fwd(q, f, v, lb)
    who = rows_full(wts["hg_w_out"])
    wmo = rows_full(wts["w_merge_out"])
    h2 = merge_fwd(h1, ys5, o, og, ga, gb, wts["s5_glu_val"], wts["s5_glu_gate"], sm["hg_out_norm"], who, wmo)
    (h3, a2, b2), _ = ffn_fwd(h2, sm["ffn2_norm"], wts["ffn2_w_gate"], wts["ffn2_w_up"], wts["ffn2_w_down"], "ffn2_fwd")
    loss, dh3, d_ple_norm, d_final_norm, npb, dlgb, dppb = head_fwd_bwd(
        h3, p, tgt, sm["ple_norm"], rows_full(wts["ple_w_gate"]), wts["ple_w_proj"], sm["final_norm"])

    gs = {"ple_norm": d_ple_norm, "final_norm": d_final_norm}
    weight_grad("ple_w_gate", npb, dlgb, "rows")
    weight_grad("ple_w_proj", p, dppb, "cols")

    (dh2, gs["ffn2_norm"], n2b, dhb2, da2, db2, s2), _ = ffn_bwd(
        dh3, h2, a2, b2, sm["ffn2_norm"], wts["ffn2_w_gate"], wts["ffn2_w_up"], wts["ffn2_w_down"], "ffn2_bwd")
    weight_grad("ffn2_w_gate", da2, n2b, "rows")
    weight_grad("ffn2_w_up", db2, n2b, "rows")
    weight_grad("ffn2_w_down", s2, dhb2, "rows")

    dys5, do, dog, dga, dgb, gs["hg_out_norm"], mixb, dh2b, ysb, dvab, dvtb, o2b, dybb = carried(
        "merge_bwd", merge_bwd,
        dh2, ys5, o, og, ga, gb, wts["s5_glu_val"], wts["s5_glu_gate"], sm["hg_out_norm"], who, wmo)
    weight_grad("w_merge_out", mixb, dh2b, "rows")
    weight_grad("s5_glu_val", ysb, dvab, "cols")
    weight_grad("s5_glu_gate", ysb, dvtb, "cols")
    weight_grad("hg_w_out", o2b, dybb, "rows")

    dq, df, dv, dlb = carried("hgrn_bwd", hgrn_bwd, do, q, f, v, lb, states)
    (gs["hg_lower_bound"],) = lb_vjp(dlb)
    du, dbmat, dcmat, dlam8, gs["s5_d"] = carried(
        "s5_bwd", s5_bwd,
        dys5, s5in, xp, _scan_tables(pw_r, pw_i, True), bmat_b, bmat_t, cmat_t, sm["s5_d"])
    for k, g in zip(s5_names, s5_vjp((jnp.sum(dlam8, axis=1), dbmat, dcmat))):
        gs[k] = g

    dh1, gs["mix_norm"], nmb, dprojb = carried(
        "inproj_bwd", inproj_bwd, dh2, h1, sm["mix_norm"], wts["w_in"], (du, dq, df, dv, dog, dga, dgb))
    weight_grad("w_in", nmb, dprojb, "cols")

    dx, gs["ffn1_norm"], n1b, dhb1, da1, db1, s1 = carried(
        "ffn1_bwd", ffn_bwd,
        dh1, x, a1, b1, sm["ffn1_norm"], wts["ffn1_w_gate"], wts["ffn1_w_up"], wts["ffn1_w_down"], "ffn1_bwd")
    weight_grad("ffn1_w_gate", da1, n1b, "rows")
    weight_grad("ffn1_w_up", db1, n1b, "rows")
    weight_grad("ffn1_w_down", s1, dhb1, "rows")
    return loss, dx, gs


MESH = pl.DeviceIdType.MESH
ANY = pl.BlockSpec(memory_space=pl.ANY)


def _place():
    x, y, c = lax.axis_index("x"), lax.axis_index("y"), lax.axis_index("c")
    return x, y, c


def _remote(src, dst, ssem, rsem, dev):
    return pltpu.make_async_remote_copy(src_ref=src, dst_ref=dst, send_sem=ssem, recv_sem=rsem,
                                        device_id=dev, device_id_type=MESH)


class Comm:
    def __init__(self, bufs, outs, alias, sems, hooks):
        self.bufs, self.outs, self.alias, self.sems, self.hooks = list(bufs), list(outs), alias, list(sems), hooks


def run_comm(comm, name):
    nb, no = len(comm.bufs), len(comm.outs)

    def body(*refs):
        for which in ("first", "mid", "last"):
            if which in comm.hooks:
                comm.hooks[which](refs[:nb], refs[nb:nb + no], refs[nb + no:])

    return pl.pallas_call(
        body, name=name, out_shape=tuple(comm.outs), in_specs=[ANY] * nb, out_specs=tuple([ANY] * no),
        input_output_aliases=dict(comm.alias), scratch_shapes=comm.sems,
    )(*comm.bufs)


def _carry(body, comm, *, name, steps, out_shape, in_specs, out_specs, args, scratch_shapes=()):
    out_shape, out_specs, scratch_shapes = tuple(out_shape), tuple(out_specs), list(scratch_shapes)
    if comm is None:
        res = pl.pallas_call(body, name=name, grid=(steps,), out_shape=out_shape, in_specs=list(in_specs),
                             out_specs=out_specs, scratch_shapes=scratch_shapes,
                             compiler_params=_cparams(("arbitrary",)))(*args)
        return tuple(res), ()
    n_in, n_out, n_scr = len(args), len(out_shape), len(scratch_shapes)
    nb, no = len(comm.bufs), len(comm.outs)

    def wrapped(*refs):
        ins, cb = refs[:n_in], refs[n_in:n_in + nb]
        o0 = n_in + nb
        outs, co = refs[o0:o0 + n_out], refs[o0 + n_out:o0 + n_out + no]
        s0 = o0 + n_out + no
        scr, cs = refs[s0:s0 + n_scr], refs[s0 + n_scr:]
        step = pl.program_id(0)

        def hook(which, at):
            if which in comm.hooks:
                pl.when(step == at)(lambda: comm.hooks[which](cb, co, cs))

        hook("first", 0)
        hook("mid", steps // 2)
        body(*ins, *outs, *scr)
        hook("last", steps - 1)

    res = pl.pallas_call(
        wrapped, name=name, grid=(steps,), out_shape=out_shape + tuple(comm.outs),
        in_specs=list(in_specs) + [ANY] * nb, out_specs=out_specs + (ANY,) * no,
        scratch_shapes=scratch_shapes + comm.sems,
        input_output_aliases={n_in + i: n_out + o for i, o in comm.alias.items()},
        compiler_params=_cparams(("arbitrary",)),
    )(*args, *comm.bufs)
    return tuple(res[:n_out]), tuple(res[n_out:])


def gather_comm(bufs):
    n = len(bufs)

    def copies(outs, sems):
        s_own, r_own, s_fwd, r_fwd, s_sib, r_sib = sems
        x, y, c = _place()
        me = 2 * x + y
        nbr = ((1 - x, y), (x, 1 - y))
        nbr_id = (2 * (1 - x) + y, 2 * x + (1 - y))
        diag_id = 2 * (1 - x) + (1 - y)
        sib = (x, y, 1 - c)

        def rows(w, q=None):
            r = outs[w].shape[1]
            if q is None:
                return pl.ds(pl.multiple_of(c * (r // 2), 16), r // 2)
            return pl.ds(pl.multiple_of(c * (r // 2) + q * (r // 4), 16), r // 4)

        def own(w, j):
            piece = outs[w].at[me, rows(w)]
            return _remote(piece, piece, s_own.at[w, j], r_own.at[w, j], (nbr[j][0], nbr[j][1], c))

        def from_nbr(w, j):
            piece = outs[w].at[nbr_id[j], rows(w)]
            return _remote(piece, piece, s_own.at[w, j], r_own.at[w, j], (nbr[j][0], nbr[j][1], c))

        def fwd(w, j):
            piece = outs[w].at[nbr_id[j], rows(w, j)]
            return _remote(piece, piece, s_fwd.at[w, j], r_fwd.at[w, j], (nbr[1 - j][0], nbr[1 - j][1], c))

        def from_diag(w, j):
            piece = outs[w].at[diag_id, rows(w, j)]
            return _remote(piece, piece, s_fwd.at[w, j], r_fwd.at[w, j], (nbr[1 - j][0], nbr[1 - j][1], c))

        def to_sib(w, k):
            piece = (outs[w].at[nbr_id[k], rows(w)] if k < 2 else outs[w].at[diag_id, rows(w, k - 2)])
            return _remote(piece, piece, s_sib.at[w, k], r_sib.at[w, k], sib)

        def from_sib(w, k):
            r = outs[w].shape[1]
            if k < 2:
                piece = outs[w].at[nbr_id[k], pl.ds(pl.multiple_of((1 - c) * (r // 2), 16), r // 2)]
            else:
                piece = outs[w].at[diag_id, pl.ds(pl.multiple_of((1 - c) * (r // 2) + (k - 2) * (r // 4), 16), r // 4)]
            return _remote(piece, piece, s_sib.at[w, k], r_sib.at[w, k], sib)

        return own, from_nbr, fwd, from_diag, to_sib, from_sib

    def first(_, outs, sems):
        own = copies(outs, sems)[0]
        for w in range(n):
            own(w, 0).start()
            own(w, 1).start()

    def mid(_, outs, sems):
        _, from_nbr, fwd, _, to_sib, _ = copies(outs, sems)
        for w in range(n):
            for j in range(2):
                from_nbr(w, j).wait_recv()
                fwd(w, j).start()
                to_sib(w, j).start()

    def last(_, outs, sems):
        own, _, fwd, from_diag, to_sib, from_sib = copies(outs, sems)
        for w in range(n):
            for j in range(2):
                from_diag(w, j).wait_recv()
                to_sib(w, 2 + j).start()
        for w in range(n):
            for k in range(4):
                from_sib(w, k).wait_recv()
        for w in range(n):
            for j in range(2):
                own(w, j).wait_send()
                fwd(w, j).wait_send()
            for k in range(4):
                to_sib(w, k).wait_send()

    dma = pltpu.SemaphoreType.DMA
    return Comm(bufs, [jax.ShapeDtypeStruct(b.shape, b.dtype) for b in bufs], {w: w for w in range(n)},
                [dma((n, 2)), dma((n, 2)), dma((n, 2)), dma((n, 2)), dma((n, 4)), dma((n, 4))],
                {"first": first, "mid": mid, "last": last})


def _start_wait(make):
    def first(bufs, outs, sems):
        for cp in make(bufs, outs, sems):
            cp.start()

    def last(bufs, outs, sems):
        for cp in make(bufs, outs, sems):
            cp.wait()

    return {"first": first, "last": last}


def exchange_comm(grads):
    n = len(grads)

    def make(ins, outs, sems):
        x, y, c = _place()
        cps = []
        for w in range(n):
            half = ins[w].shape[1] // 2
            src = ins[w].at[:, pl.ds(pl.multiple_of((1 - c) * half, 8), half), :]
            cps.append(_remote(src, outs[w], sems[0].at[w], sems[1].at[w], (x, y, 1 - c)))
        return cps

    dma = pltpu.SemaphoreType.DMA
    return Comm(grads, [jax.ShapeDtypeStruct((N_SHARD, g.shape[1] // 2, g.shape[2]), g.dtype) for g in grads],
                {}, [dma((n,)), dma((n,))], _start_wait(make))


def scatter_comm(sums):
    n = len(sums)

    def make(ins, outs, sems):
        x, y, c = _place()
        chips = ((1 - x, y), (x, 1 - y), (1 - x, 1 - y))
        return [_remote(ins[w].at[2 * ch[0] + ch[1]], outs[w].at[j], sems[0].at[w, j], sems[1].at[w, j],
                        (ch[0], ch[1], c))
                for w in range(n) for j, ch in enumerate(chips)]

    dma = pltpu.SemaphoreType.DMA
    return Comm(sums, [jax.ShapeDtypeStruct((3,) + s.shape[1:], s.dtype) for s in sums],
                {}, [dma((n, 3)), dma((n, 3))], _start_wait(make))


def join_comm(shards):
    n = len(shards)

    def make(_, outs, sems):
        x, y, c = _place()
        cps = []
        for w in range(n):
            half = outs[w].shape[0] // 2
            mine = outs[w].at[pl.ds(pl.multiple_of(c * half, 8), half), :]
            cps.append(_remote(mine, mine, sems[0].at[w], sems[1].at[w], (x, y, 1 - c)))
        return cps

    dma = pltpu.SemaphoreType.DMA
    return Comm(shards, [jax.ShapeDtypeStruct(s.shape, s.dtype) for s in shards], {w: w for w in range(n)},
                [dma((n,)), dma((n,))], _start_wait(make))


def allreduce_small(vec):
    half = vec.shape[0] // 2

    def body(v_ref, o_ref, pair, chips_buf, s1, r1, s2, r2, s3, r3):
        x, y, c = _place()
        chip = 2 * x + y
        sib = (x, y, 1 - c)
        mine = pl.ds(pl.multiple_of(c * half, 8), half)
        other = pl.ds(pl.multiple_of((1 - c) * half, 8), half)
        to_sib = _remote(v_ref.at[other], pair, s1, r1, sib)
        to_sib.start()
        to_sib.wait()
        chips_buf[chip] = v_ref[mine, :] + pair[...]
        sends = [_remote(chips_buf.at[chip], chips_buf.at[chip], s2.at[j], r2.at[j], (ch[0], ch[1], c))
                 for j, ch in enumerate(((1 - x, y), (x, 1 - y), (1 - x, 1 - y)))]
        for cp in sends:
            cp.start()
        for cp in sends:
            cp.wait()
        o_ref[mine, :] = (chips_buf[0] + chips_buf[1]) + (chips_buf[2] + chips_buf[3])
        back = _remote(o_ref.at[mine], o_ref.at[mine], s3, r3, sib)
        back.start()
        back.wait()

    dma = pltpu.SemaphoreType.DMA
    return pl.pallas_call(
        body, name="allreduce_small",
        out_shape=jax.ShapeDtypeStruct(vec.shape, F32),
        in_specs=[pl.BlockSpec(memory_space=pltpu.VMEM)],
        out_specs=pl.BlockSpec(memory_space=pltpu.VMEM),
        scratch_shapes=[pltpu.VMEM((half, 128), F32), pltpu.VMEM((N_SHARD, half, 128), F32),
                        dma, dma, dma((3,)), dma((3,)), dma, dma],
        compiler_params=pltpu.CompilerParams(vmem_limit_bytes=VMEM_LIMIT),
    )(vec)


ROW_TILE = 128


def add_own_half(place, g, recv, name):
    _, r, cc = g.shape
    half = r // 2
    nb = half // ROW_TILE

    def body(p_ref, g_ref, r_ref, o_ref, ob_ref):
        s = g_ref[...] + r_ref[...]
        ob_ref[...] = s.astype(BF16)

        @pl.when(pl.program_id(1) == p_ref[0])
        def _():
            o_ref[...] = s

    blk = (None, ROW_TILE, cc)
    return pl.pallas_call(
        body, name=name,
        grid_spec=pltpu.PrefetchScalarGridSpec(
            num_scalar_prefetch=1, grid=(nb, N_SHARD),
            in_specs=[pl.BlockSpec(blk, lambda i, s, p_ref: (s, p_ref[1] * nb + i, 0)),
                      pl.BlockSpec(blk, lambda i, s, p_ref: (s, i, 0))],
            out_specs=(pl.BlockSpec((ROW_TILE, cc), lambda i, s, p_ref: (i, 0)),
                       pl.BlockSpec(blk, lambda i, s, p_ref: (s, i, 0)))),
        out_shape=(jax.ShapeDtypeStruct((half, cc), F32),
                   jax.ShapeDtypeStruct((N_SHARD, half, cc), BF16)),
        compiler_params=_cparams(("arbitrary", "arbitrary")),
    )(place, g, recv)


def add_chip_sums(place, own, recv, name):
    half, cc = own.shape
    nb = half // ROW_TILE

    def body(s_ref, o_ref, r_ref, out_ref):
        del s_ref
        acc = o_ref[...] + r_ref[0].astype(F32)
        acc = acc + r_ref[1].astype(F32)
        out_ref[...] = acc + r_ref[2].astype(F32)

    return pl.pallas_call(
        body, name=name,
        grid_spec=pltpu.PrefetchScalarGridSpec(
            num_scalar_prefetch=1, grid=(nb,),
            in_specs=[pl.BlockSpec((ROW_TILE, cc), lambda i, s_ref: (i, 0)),
                      pl.BlockSpec((3, ROW_TILE, cc), lambda i, s_ref: (0, i, 0))],
            out_specs=pl.BlockSpec((ROW_TILE, cc), lambda i, s_ref: (s_ref[1] * nb + i, 0))),
        out_shape=jax.ShapeDtypeStruct((2 * half, cc), F32),
        compiler_params=_cparams(("arbitrary",)),
    )(place, own, recv)


def adamw(w, m, v, g, name, copy_g=False):
    r, cc = w.shape
    tr = next(t for t in (256, 352, r) if r % t == 0)
    bc1 = 1.0 / (1.0 - ADAM_B1 ** ADAM_STEP)
    bc2 = 1.0 / (1.0 - ADAM_B2 ** ADAM_STEP)

    def body(w_ref, m_ref, v_ref, g_ref, d_ref, mo_ref, vo_ref, *go_ref):
        gv = g_ref[...]
        mn = ADAM_B1 * m_ref[...] + (1.0 - ADAM_B1) * gv
        vn = ADAM_B2 * v_ref[...] + (1.0 - ADAM_B2) * (gv * gv)
        mo_ref[...] = mn
        vo_ref[...] = vn
        d_ref[...] = -ADAM_LR * ((mn * bc1) / (jnp.sqrt(vn * bc2) + ADAM_EPS) + ADAM_WD * w_ref[...])
        if copy_g:
            go_ref[0][...] = gv

    blk = pl.BlockSpec((tr, cc), lambda i: (i, 0))
    shp = jax.ShapeDtypeStruct((r, cc), F32)
    nout = 4 if copy_g else 3
    return pl.pallas_call(
        body, name=name, grid=(r // tr,),
        out_shape=(shp,) * nout, in_specs=[blk] * 4, out_specs=(blk,) * nout,
        compiler_params=_cparams(("arbitrary",)),
    )(w, m, v, g)


GATHER_FIRST = ("ffn1_w_gate", "ffn1_w_up", "ffn1_w_down")
GATHER_ON = {"ffn1_fwd": ("w_in", "s5_glu_val", "s5_glu_gate", "hg_w_out", "w_merge_out"),
             "s5_fwd": ("ffn2_w_gate", "ffn2_w_up", "ffn2_w_down", "ple_w_gate", "ple_w_proj")}
REDUCE = ((("ple_w_gate", "ple_w_proj", "ffn2_w_gate", "ffn2_w_up", "ffn2_w_down"), "merge_bwd", "hgrn_bwd"),
          (("w_merge_out", "s5_glu_val", "s5_glu_gate", "hg_w_out"), "s5_bwd", "inproj_bwd"),
          (("w_in",), None, "ffn1_bwd"),
          (("ffn1_w_gate",), "g_ffn1_w_up", "g_ffn1_w_down"),
          (("ffn1_w_up",), "g_ffn1_w_down", None),
          (("ffn1_w_down",), None, None))


def merge_comms(comms):
    if len(comms) == 1:
        return comms[0], [len(comms[0].outs)]
    bufs, outs, sems, alias, spans = [], [], [], {}, []
    for c in comms:
        spans.append((len(bufs), len(bufs) + len(c.bufs), len(outs), len(outs) + len(c.outs),
                      len(sems), len(sems) + len(c.sems)))
        alias.update({len(bufs) + i: len(outs) + o for i, o in c.alias.items()})
        bufs, outs, sems = bufs + c.bufs, outs + c.outs, sems + c.sems

    def hook(which):
        def run(b, o, s):
            for c, (b0, b1, o0, o1, s0, s1) in zip(comms, spans):
                if which in c.hooks:
                    c.hooks[which](b[b0:b1], o[o0:o1], s[s0:s1])
        return run

    hooks = {w: hook(w) for w in ("first", "mid", "last") if any(w in c.hooks for c in comms)}
    return Comm(bufs, outs, alias, sems, hooks), [len(c.outs) for c in comms]


class DistSchedule(Schedule):
    def __init__(self, bufs, chip, core):
        first = run_comm(gather_comm([bufs[k] for k in GATHER_FIRST]), "gather_ffn1")
        super().__init__(zip(GATHER_FIRST, first))
        self.bufs = bufs
        self.place = jnp.stack([chip, core])
        self.sums, self.halves = {}, {}

    def _exchange(self, names):
        return exchange_comm([self.grads[k] for k in names])

    def _scatter(self, names):
        return scatter_comm([self.sums[k][1] for k in names])

    def _pair_sums(self, names, recv):
        for k, r in zip(names, recv):
            self.sums[k] = add_own_half(self.place, self.grads[k], r, "pair_sum_" + k)

    def _chip_sums(self, names, recv):
        for k, r in zip(names, recv):
            self.halves[k] = add_chip_sums(self.place, self.sums[k][0], r, "chip_sum_" + k)

    def before(self, kernel_name):
        comms, takers = [], []
        if kernel_name in GATHER_ON:
            names = GATHER_ON[kernel_name]
            comms.append(gather_comm([self.bufs[k] for k in names]))
            takers.append(lambda res, names=names: self.wts.update(zip(names, res)))
        for names, exchange_on, scatter_on in REDUCE:
            if kernel_name == exchange_on:
                comms.append(self._exchange(names))
                takers.append(lambda res, names=names: self._pair_sums(names, res))
            if kernel_name == scatter_on:
                if exchange_on is None:
                    self._pair_sums(names, run_comm(self._exchange(names), "exchange_" + names[0]))
                comms.append(self._scatter(names))
                takers.append(lambda res, names=names: self._chip_sums(names, res))
        if not comms:
            return None
        merged, counts = merge_comms(comms)
        self.pending = (takers, counts)
        return merged

    def after(self, kernel_name, results):
        if not results:
            return
        takers, counts = self.pending
        start = 0
        for take, count in zip(takers, counts):
            take(results[start:start + count])
            start += count

    def finish(self):
        tail = [names for names, _, scatter_on in REDUCE if scatter_on is None]
        alone = [k for names, exchange_on, scatter_on in REDUCE if scatter_on is None and exchange_on is None
                 for k in names]
        self._pair_sums(alone, run_comm(self._exchange(alone), "exchange_tail"))
        tail = [k for names in tail for k in names]
        self._chip_sums(tail, run_comm(self._scatter(tail), "scatter_tail"))
        return dict(zip(BIG, run_comm(join_comm([self.halves[k] for k in BIG]), "join_halves")))


SMALL = ("ffn1_norm", "mix_norm", "s5_lam_re", "s5_lam_im", "s5_log_dt", "s5_b_re", "s5_b_im", "s5_c_re",
         "s5_c_im", "s5_d", "hg_lower_bound", "hg_out_norm", "ffn2_norm", "ple_norm", "final_norm")
WEIGHTS = ("ffn1_norm", "ffn1_w_gate", "ffn1_w_up", "ffn1_w_down", "mix_norm", "w_in", "s5_lam_re", "s5_lam_im",
           "s5_log_dt", "s5_b_re", "s5_b_im", "s5_c_re", "s5_c_im", "s5_d", "s5_glu_val", "s5_glu_gate",
           "hg_lower_bound", "hg_out_norm", "hg_w_out", "w_merge_out", "ffn2_norm", "ffn2_w_gate", "ffn2_w_up",
           "ffn2_w_down", "ple_norm", "ple_w_gate", "ple_w_proj", "final_norm")


def _as_rows(name, w):
    return jnp.swapaxes(w[0], 0, 1) if name in FFN_T else w[0]


def _from_rows(name, w):
    return (jnp.swapaxes(w, 0, 1) if name in FFN_T else w)[None]


def _gather_buffer(name, w_rows, chip):
    r, c = BIG_SHARD[name]
    shard = jnp.pad(w_rows.astype(BF16), ((0, r - w_rows.shape[0]), (0, 0)))
    return lax.dynamic_update_slice(jnp.zeros((N_SHARD, r, c), BF16), shard[None], (chip, 0, 0))


def _pack(parts):
    flat = jnp.concatenate([jnp.zeros((128,), F32)] + [a.reshape(-1) for a in parts])
    rows = -(-flat.shape[0] // 2048) * 16
    return jnp.pad(flat, (0, rows * 128 - flat.shape[0])).reshape(rows, 128)


def _unpack(vec, likes):
    flat = vec.reshape(-1)
    out, off = [], 128
    for a in likes:
        out.append(flat[off:off + a.size].reshape(a.shape))
        off += a.size
    return out


def _small_view(name, w):
    if name.startswith("s5_") and name != "s5_d":
        return w[0]
    if name == "final_norm":
        return w.reshape(1, D_MODEL)
    return w


def kernel(x, p, ffn1_norm, ffn1_w_gate, ffn1_w_up, ffn1_w_down, mix_norm, w_in, s5_lam_re, s5_lam_im, s5_log_dt, s5_b_re, s5_b_im, s5_c_re, s5_c_im, s5_d, s5_glu_val, s5_glu_gate, hg_lower_bound, hg_out_norm, hg_w_out, w_merge_out, ffn2_norm, ffn2_w_gate, ffn2_w_up, ffn2_w_down, ple_norm, ple_w_gate, ple_w_proj, final_norm, loss_target, m_ffn1_norm, m_ffn1_w_gate, m_ffn1_w_up, m_ffn1_w_down, m_mix_norm, m_w_in, m_s5_lam_re, m_s5_lam_im, m_s5_log_dt, m_s5_b_re, m_s5_b_im, m_s5_c_re, m_s5_c_im, m_s5_d, m_s5_glu_val, m_s5_glu_gate, m_hg_lower_bound, m_hg_out_norm, m_hg_w_out, m_w_merge_out, m_ffn2_norm, m_ffn2_w_gate, m_ffn2_w_up, m_ffn2_w_down, m_ple_norm, m_ple_w_gate, m_ple_w_proj, m_final_norm, v_ffn1_norm, v_ffn1_w_gate, v_ffn1_w_up, v_ffn1_w_down, v_mix_norm, v_w_in, v_s5_lam_re, v_s5_lam_im, v_s5_log_dt, v_s5_b_re, v_s5_b_im, v_s5_c_re, v_s5_c_im, v_s5_d, v_s5_glu_val, v_s5_glu_gate, v_hg_lower_bound, v_hg_out_norm, v_hg_w_out, v_w_merge_out, v_ffn2_norm, v_ffn2_w_gate, v_ffn2_w_up, v_ffn2_w_down, v_ple_norm, v_ple_w_gate, v_ple_w_proj, v_final_norm):
    given = dict(locals())
    wv = {k: given[k] for k in WEIGHTS}
    mv = {k: given["m_" + k] for k in WEIGHTS}
    vv = {k: given["v_" + k] for k in WEIGHTS}

    core = lax.axis_index("c").astype(jnp.int32)
    chip = (2 * lax.axis_index("x") + lax.axis_index("y")).astype(jnp.int32)
    w_rows = {k: _as_rows(k, wv[k]) for k in BIG}
    sched = DistSchedule({k: _gather_buffer(k, w_rows[k], chip) for k in BIG}, chip, core)
    sm = {k: _small_view(k, wv[k]) for k in SMALL}

    loss_blk, dx, gsm = local_step(x[0], p[0, 0], loss_target[0], sched, sm)
    full = sched.finish()

    small_likes = [wv[k] for k in SMALL]
    packed = _pack([gsm[k] for k in SMALL])
    packed = packed.at[0, 0].set(loss_blk[0, 0])
    total = allreduce_small(packed)
    loss = total[0, 0]
    gsmall = dict(zip(SMALL, _unpack(total, small_likes)))

    grads, deltas, new_m, new_v = {}, {}, {}, {}
    for k in BIG:
        padded = full[k].shape != w_rows[k].shape
        res = adamw(w_rows[k], _as_rows(k, mv[k]), _as_rows(k, vv[k]), full[k], "adamw_" + k, copy_g=padded)
        grads[k] = _from_rows(k, res[3] if padded else full[k])
        deltas[k], new_m[k], new_v[k] = (_from_rows(k, a) for a in res[:3])
    sw = _pack([wv[k] for k in SMALL])
    smm = _pack([mv[k] for k in SMALL])
    svv = _pack([vv[k] for k in SMALL])
    sd, smn, svn = adamw(sw, smm, svv, total, "adamw_small")
    for k, d, mn, vn in zip(SMALL, _unpack(sd, small_likes), _unpack(smn, small_likes), _unpack(svn, small_likes)):
        grads[k], deltas[k], new_m[k], new_v[k] = gsmall[k], d, mn, vn

    return (loss, dx[None], *[grads[k] for k in WEIGHTS], *[deltas[k] for k in WEIGHTS],
            *[new_m[k] for k in WEIGHTS], *[new_v[k] for k in WEIGHTS])
```

```python
import math

import jax
import jax.numpy as jnp
from jax import lax
from jax.experimental import pallas as pl
from jax.experimental.pallas import tpu as pltpu

F32 = jnp.float32
BF16 = jnp.bfloat16

D_MODEL = 1024
D_FF = 2816
N_SHARD = 4
FF_SHARD = D_FF // N_SHARD
FF_PAD = 768
NORM_EPS = 1e-6
PLE_DIM = 256

S5_WIDTH = 512
S5_GROUPS = 32
S5_GROUP = 16
S5_STATE = 64
S5_N = S5_GROUPS * S5_STATE
S5_KT = 2

HG_HEADS = 8
HG_E = 128
HG_WIDTH = 1024
CHUNK = 64
HG_SUB = 4
IN_COLS = S5_WIDTH + 4 * HG_WIDTH + 2 * D_MODEL
IN_SPLITS = (0, 512, 1536, 2560, 3584, 4608, 5632, 6656)

ADAM_LR = 0.001
ADAM_B1 = 0.9
ADAM_B2 = 0.999
ADAM_EPS = 1e-08
ADAM_WD = 0.01
ADAM_STEP = 10

VMEM_LIMIT = 60 * 1024 * 1024
HIGHEST = lax.Precision.HIGHEST


def _cparams(sem=None, **kw):
    return pltpu.CompilerParams(dimension_semantics=sem, vmem_limit_bytes=VMEM_LIMIT, **kw)


def _const_spec(shape):
    nd = len(shape)
    return pl.BlockSpec(shape, lambda *_: (0,) * nd, pipeline_mode=pl.Buffered(1))


def _dot(a, b):
    return jnp.dot(a, b, preferred_element_type=F32)


def _dot_nt(a, b):
    return lax.dot_general(a, b, (((1,), (1,)), ((), ())), preferred_element_type=F32)


def _dot_tn(a, b):
    return lax.dot_general(a, b, (((0,), (0,)), ((), ())), preferred_element_type=F32)


def _sigmoid(x):
    return 1.0 / (1.0 + jnp.exp(-x))


def _rms_fwd(x, g):
    r = lax.rsqrt(jnp.mean(x * x, axis=-1, keepdims=True) + NORM_EPS)
    return x * r * g, r


def _rms_bwd(x, r, g, dy):
    xh = x * r
    dyg = dy * g
    m = jnp.mean(dyg * xh, axis=-1, keepdims=True)
    return r * (dyg - xh * m), jnp.sum(dy * xh, axis=0, keepdims=True)


def _accum(ref, val, first):
    @pl.when(first)
    def _():
        ref[...] = val

    @pl.when(jnp.logical_not(first))
    def _():
        ref[...] += val


def ffn_fwd(h, gain, wg, wu, wd, name, comm=None, tm=256):
    t = h.shape[0]

    def body(h_ref, g_ref, wg_ref, wu_ref, wd_ref, o_ref, a_ref, b_ref):
        hv = h_ref[...]
        n, _ = _rms_fwd(hv, g_ref[...])
        nb = n.astype(BF16)
        acc = jnp.zeros((tm, D_MODEL), F32)
        for s in range(N_SHARD):
            a = _dot_nt(nb, wg_ref[s])
            b = _dot_nt(nb, wu_ref[s])
            a_ref[s] = a.astype(BF16)
            b_ref[s] = b.astype(BF16)
            sv = (a * _sigmoid(a) * b).astype(BF16)
            acc = acc + _dot(sv, wd_ref[s])
        o_ref[...] = hv + 0.5 * acc

    return _carry(
        body, comm, name=name, steps=t // tm,
        out_shape=(jax.ShapeDtypeStruct((t, D_MODEL), F32),
                   jax.ShapeDtypeStruct((N_SHARD, t, FF_PAD), BF16),
                   jax.ShapeDtypeStruct((N_SHARD, t, FF_PAD), BF16)),
        in_specs=[pl.BlockSpec((tm, D_MODEL), lambda i: (i, 0)),
                  _const_spec((1, D_MODEL)),
                  _const_spec((N_SHARD, FF_PAD, D_MODEL)),
                  _const_spec((N_SHARD, FF_PAD, D_MODEL)),
                  _const_spec((N_SHARD, FF_PAD, D_MODEL))],
        out_specs=(pl.BlockSpec((tm, D_MODEL), lambda i: (i, 0)),
                   pl.BlockSpec((N_SHARD, tm, FF_PAD), lambda i: (0, i, 0)),
                   pl.BlockSpec((N_SHARD, tm, FF_PAD), lambda i: (0, i, 0))),
        args=(h, gain, wg, wu, wd),
    )


def ffn_bwd(dho, h, a, b, gain, wg, wu, wd, name, comm=None, tm=256):
    t = h.shape[0]

    def body(dho_ref, h_ref, a_ref, b_ref, g_ref, wg_ref, wu_ref, wd_ref,
             dh_ref, dg_ref, nb_ref, dhb_ref, da_ref, db_ref, s_ref):
        hv = h_ref[...]
        g = g_ref[...]
        n, r = _rms_fwd(hv, g)
        nb_ref[...] = n.astype(BF16)
        dhalf = (0.5 * dho_ref[...]).astype(BF16)
        dhb_ref[...] = dhalf
        dn = jnp.zeros((tm, D_MODEL), F32)
        for s in range(N_SHARD):
            av = a_ref[s].astype(F32)
            bv = b_ref[s].astype(F32)
            sg = _sigmoid(av)
            sil = av * sg
            s_ref[s] = (sil * bv).astype(BF16)
            ds = _dot_nt(dhalf, wd_ref[s])
            da = (ds * bv * (sg * (1.0 + av * (1.0 - sg)))).astype(BF16)
            db = (ds * sil).astype(BF16)
            da_ref[s] = da
            db_ref[s] = db
            dn = dn + _dot(da, wg_ref[s]) + _dot(db, wu_ref[s])
        dx, dg = _rms_bwd(hv, r, g, dn)
        dh_ref[...] = dho_ref[...] + dx
        _accum(dg_ref, dg, pl.program_id(0) == 0)

    tok = pl.BlockSpec((tm, D_MODEL), lambda i: (i, 0))
    hid = pl.BlockSpec((N_SHARD, tm, FF_PAD), lambda i: (0, i, 0))
    return _carry(
        body, comm, name=name, steps=t // tm,
        out_shape=(jax.ShapeDtypeStruct((t, D_MODEL), F32),
                   jax.ShapeDtypeStruct((1, D_MODEL), F32),
                   jax.ShapeDtypeStruct((t, D_MODEL), BF16),
                   jax.ShapeDtypeStruct((t, D_MODEL), BF16),
                   jax.ShapeDtypeStruct((N_SHARD, t, FF_PAD), BF16),
                   jax.ShapeDtypeStruct((N_SHARD, t, FF_PAD), BF16),
                   jax.ShapeDtypeStruct((N_SHARD, t, FF_PAD), BF16)),
        in_specs=[tok, tok, hid, hid, _const_spec((1, D_MODEL)),
                  _const_spec((N_SHARD, FF_PAD, D_MODEL)),
                  _const_spec((N_SHARD, FF_PAD, D_MODEL)),
                  _const_spec((N_SHARD, FF_PAD, D_MODEL))],
        out_specs=(tok, pl.BlockSpec((1, D_MODEL), lambda i: (0, 0)), tok, tok, hid, hid, hid),
        args=(dho, h, a, b, gain, wg, wu, wd),
    )


TN_VMEM_BUDGET = 44 * 1024 * 1024


def tn_matmul(x, y, name, shard, comm=None):
    x3, y3 = x.ndim == 3, y.ndim == 3
    t = x.shape[-2]
    m = x.shape[-1] // (N_SHARD if (shard == "rows" and not x3) else 1)
    n = y.shape[-1] // (N_SHARD if (shard == "cols" and not y3) else 1)
    per_token = 2 * (m * x.dtype.itemsize + n * y.dtype.itemsize)
    tk = t
    while tk > 512 and tk * per_token + 2 * m * n * 4 > TN_VMEM_BUDGET:
        tk //= 2
    nk = t // tk

    out_shape = jax.ShapeDtypeStruct((N_SHARD, m, n), F32)
    if nk == 1:
        def whole(x_ref, y_ref, o_ref):
            o_ref[...] = _dot_tn(x_ref[...].astype(BF16), y_ref[...].astype(BF16))

        x_one = (pl.BlockSpec((None, t, m), lambda s: (s, 0, 0)) if x3 else
                 pl.BlockSpec((t, m), (lambda s: (0, s)) if shard == "rows" else (lambda s: (0, 0))))
        y_one = (pl.BlockSpec((None, t, n), lambda s: (s, 0, 0)) if y3 else
                 pl.BlockSpec((t, n), (lambda s: (0, s)) if shard == "cols" else (lambda s: (0, 0))))
        return _carry(whole, comm, name=name, steps=N_SHARD, out_shape=(out_shape,), in_specs=[x_one, y_one],
                      out_specs=(pl.BlockSpec((None, m, n), lambda s: (s, 0, 0)),), args=(x, y))
    assert comm is None

    def body(x_ref, y_ref, o_ref):
        _accum(o_ref, _dot_tn(x_ref[...].astype(BF16), y_ref[...].astype(BF16)), pl.program_id(1) == 0)

    if x3:
        x_spec = pl.BlockSpec((None, tk, m), lambda s, k: (s, k, 0))
    elif shard == "rows":
        x_spec = pl.BlockSpec((tk, m), lambda s, k: (k, s))
    else:
        x_spec = pl.BlockSpec((tk, m), lambda s, k: (k, 0))
    if y3:
        y_spec = pl.BlockSpec((None, tk, n), lambda s, k: (s, k, 0))
    elif shard == "cols":
        y_spec = pl.BlockSpec((tk, n), lambda s, k: (k, s))
    else:
        y_spec = pl.BlockSpec((tk, n), lambda s, k: (k, 0))
    res = pl.pallas_call(
        body, name=name, grid=(N_SHARD, nk),
        out_shape=out_shape,
        in_specs=[x_spec, y_spec],
        out_specs=pl.BlockSpec((None, m, n), lambda s, k: (s, 0, 0)),
        compiler_params=_cparams(("arbitrary", "arbitrary")),
    )(x, y)
    return (res,), ()


def inproj_fwd(h, gain, w_in, tm=256):
    t = h.shape[0]
    widths = [IN_SPLITS[j + 1] - IN_SPLITS[j] for j in range(7)]
    sh_cols = IN_COLS // N_SHARD

    def body(h_ref, g_ref, w_ref, *outs):
        n, _ = _rms_fwd(h_ref[...], g_ref[...])
        nb = n.astype(BF16)
        proj = jnp.concatenate([_dot(nb, w_ref[s]) for s in range(N_SHARD)], axis=1)
        for j, o_ref in enumerate(outs):
            o_ref[...] = proj[:, IN_SPLITS[j]:IN_SPLITS[j + 1]]

    return pl.pallas_call(
        body, name="inproj_fwd", grid=(t // tm,),
        out_shape=tuple(jax.ShapeDtypeStruct((t, w), F32) for w in widths),
        in_specs=[pl.BlockSpec((tm, D_MODEL), lambda i: (i, 0)),
                  _const_spec((1, D_MODEL)),
                  _const_spec((N_SHARD, D_MODEL, sh_cols))],
        out_specs=tuple(pl.BlockSpec((tm, w), lambda i: (i, 0)) for w in widths),
        compiler_params=_cparams(("arbitrary",)),
    )(h, gain, w_in)


def inproj_bwd(dres, h, gain, w_in, dparts, comm=None, tm=256):
    t = h.shape[0]
    widths = [IN_SPLITS[j + 1] - IN_SPLITS[j] for j in range(7)]
    sh_cols = IN_COLS // N_SHARD

    def body(dres_ref, h_ref, g_ref, w_ref, d0, d1, d2, d3, d4, d5, d6, dh_ref, dg_ref, nb_ref, dp_ref):
        hv = h_ref[...]
        g = g_ref[...]
        n, r = _rms_fwd(hv, g)
        nb_ref[...] = n.astype(BF16)
        dproj = jnp.concatenate([d[...] for d in (d0, d1, d2, d3, d4, d5, d6)], axis=1).astype(BF16)
        dp_ref[...] = dproj
        dn = jnp.zeros((tm, D_MODEL), F32)
        for s in range(N_SHARD):
            dn = dn + _dot_nt(dproj[:, s * sh_cols:(s + 1) * sh_cols], w_ref[s])
        dx, dg = _rms_bwd(hv, r, g, dn)
        dh_ref[...] = dres_ref[...] + dx
        _accum(dg_ref, dg, pl.program_id(0) == 0)

    tok = pl.BlockSpec((tm, D_MODEL), lambda i: (i, 0))
    return _carry(
        body, comm, name="inproj_bwd", steps=t // tm,
        out_shape=(jax.ShapeDtypeStruct((t, D_MODEL), F32),
                   jax.ShapeDtypeStruct((1, D_MODEL), F32),
                   jax.ShapeDtypeStruct((t, D_MODEL), BF16),
                   jax.ShapeDtypeStruct((t, IN_COLS), BF16)),
        in_specs=[tok, tok, _const_spec((1, D_MODEL)), _const_spec((N_SHARD, D_MODEL, sh_cols))]
                 + [pl.BlockSpec((tm, w), lambda i: (i, 0)) for w in widths],
        out_specs=(tok, pl.BlockSpec((1, D_MODEL), lambda i: (0, 0)), tok,
                   pl.BlockSpec((tm, IN_COLS), lambda i: (i, 0))),
        args=(dres, h, gain, w_in, *dparts),
    )


def s5_prep(lam_re, lam_im, log_dt, b_re, b_im, c_re, c_im):
    dt = jnp.exp(log_dt)[:, None]
    mag = jnp.exp(lam_re * dt)
    lbr = mag * jnp.cos(lam_im * dt)
    lbi = mag * jnp.sin(lam_im * dt)
    den = lam_re * lam_re + lam_im * lam_im
    nr, ni = lbr - 1.0, lbi
    kr = (nr * lam_re + ni * lam_im) / den
    ki = (ni * lam_re - nr * lam_im) / den
    bbr = kr[..., None] * b_re - ki[..., None] * b_im
    bbi = kr[..., None] * b_im + ki[..., None] * b_re
    eye = jnp.eye(16, dtype=F32)

    def bm(bp):
        return jnp.einsum('kgph,gG->kghGp', bp.reshape(S5_KT, 16, S5_STATE, S5_GROUP), eye).reshape(S5_KT, 256, 1024)

    def cm(cp):
        return jnp.einsum('kghp,gG->kgpGh', cp.reshape(S5_KT, 16, S5_GROUP, S5_STATE), eye).reshape(S5_KT, 1024, 256)

    lam_bar = jnp.stack([lbr.reshape(S5_N), lbi.reshape(S5_N)])
    bmat = jnp.stack([bm(bbr), bm(bbi)])
    cmat = jnp.stack([cm(c_re), -cm(c_im)])
    return lam_bar, bmat, cmat


def _lam_powers(lam_bar):
    lr, li = lam_bar[0], lam_bar[1]
    pr, pi = [lr], [li]
    for _ in range(7):
        pr, pi = pr + [pr[-1] * lr - pi[-1] * li], pi + [pr[-1] * li + pi[-1] * lr]
    return jnp.stack(pr), jnp.stack(pi)


SCAN_SHIFTS = ((1, 0), (2, 1), (4, 3))


def _scan_tables(pw_r, pw_i, reverse):
    rows = jnp.arange(8)[:, None]
    planes_r, planes_i = [], []
    for sh, idx in SCAN_SHIFTS:
        keep = (rows < 8 - sh) if reverse else (rows >= sh)
        planes_r.append(jnp.where(keep, pw_r[idx:idx + 1], 0.0))
        planes_i.append(jnp.where(keep, pw_i[idx:idx + 1], 0.0))
    carry = [pw_r[::-1], pw_i[::-1]] if reverse else [pw_r, pw_i]
    return jnp.stack(planes_r + planes_i + carry)


def s5_fwd(u, tab, bmat, cmat, dvec, comm=None, tm=256):
    t = u.shape[0]
    nch = tm // 8

    def body(u_ref, tab_ref, b_ref, c_ref, d_ref, y_ref, xp_ref, x_scr, carry):
        @pl.when(pl.program_id(0) == 0)
        def _():
            carry[...] = jnp.zeros_like(carry)

        uv = u_ref[...]
        ub = uv.astype(BF16)
        for part in range(2):
            for kt in range(S5_KT):
                x_scr[:, pl.ds(part * S5_N + kt * 1024, 1024)] = _dot(ub[:, kt * 256:(kt + 1) * 256], b_ref[part, kt])
        row = lax.broadcasted_iota(jnp.int32, (8, S5_N), 0)

        def chunk(i, c):
            cr, ci = c
            r0 = pl.multiple_of(i * 8, 8)
            xr = x_scr[pl.ds(r0, 8), pl.ds(0, S5_N)]
            xi = x_scr[pl.ds(r0, 8), pl.ds(S5_N, S5_N)]
            for lvl, (sh, _) in enumerate(SCAN_SHIFTS):
                sr = pltpu.roll(xr, sh, 0)
                si = pltpu.roll(xi, sh, 0)
                lr = tab_ref[lvl]
                li = tab_ref[3 + lvl]
                xr, xi = xr + lr * sr - li * si, xi + lr * si + li * sr
            pwr = tab_ref[6]
            pwi = tab_ref[7]
            xr, xi = xr + pwr * cr - pwi * ci, xi + pwr * ci + pwi * cr
            x_scr[pl.ds(r0, 8), pl.ds(0, S5_N)] = xr
            x_scr[pl.ds(r0, 8), pl.ds(S5_N, S5_N)] = xi
            xp_ref[pl.ds(r0, 8), pl.ds(0, S5_N)] = jnp.where(row == 0, cr, pltpu.roll(xr, 1, 0))
            xp_ref[pl.ds(r0, 8), pl.ds(S5_N, S5_N)] = jnp.where(row == 0, ci, pltpu.roll(xi, 1, 0))
            return xr[7:8, :], xi[7:8, :]

        cr, ci = lax.fori_loop(0, nch, chunk, (carry[0:1, :], carry[1:2, :]))
        carry[0:1, :] = cr
        carry[1:2, :] = ci
        for kt in range(S5_KT):
            acc = jnp.zeros((tm, 256), F32)
            for part in range(2):
                acc = acc + _dot(x_scr[:, pl.ds(part * S5_N + kt * 1024, 1024)].astype(BF16), c_ref[part, kt])
            y_ref[:, pl.ds(kt * 256, 256)] = acc + d_ref[:, pl.ds(kt * 256, 256)] * uv[:, kt * 256:(kt + 1) * 256]

    return _carry(
        body, comm, name="s5_fwd", steps=t // tm,
        out_shape=(jax.ShapeDtypeStruct((t, S5_WIDTH), F32),
                   jax.ShapeDtypeStruct((t, 2 * S5_N), F32)),
        in_specs=[pl.BlockSpec((tm, S5_WIDTH), lambda i: (i, 0)),
                  _const_spec((8, 8, S5_N)),
                  _const_spec((2, S5_KT, 256, 1024)), _const_spec((2, S5_KT, 1024, 256)),
                  _const_spec((1, S5_WIDTH))],
        out_specs=(pl.BlockSpec((tm, S5_WIDTH), lambda i: (i, 0)),
                   pl.BlockSpec((tm, 2 * S5_N), lambda i: (i, 0))),
        scratch_shapes=[pltpu.VMEM((tm, 2 * S5_N), F32), pltpu.VMEM((8, S5_N), F32)],
        args=(u, tab, bmat, cmat, dvec),
    )


def s5_bwd(dy, u, xp, tab, bmat, bmat_t, cmat_t, dvec, comm=None, tm=256):
    t = u.shape[0]
    nt = t // tm
    nch = tm // 8

    def body(dy_ref, u_ref, xp_ref, tab_ref, b_ref, bt_ref, ct_ref, d_ref,
             du_ref, db_ref, dc_ref, dl_ref, dd_ref, g_scr, x_scr, carry):
        first = pl.program_id(0) == 0

        @pl.when(first)
        def _():
            carry[...] = jnp.zeros_like(carry)
            dl_ref[...] = jnp.zeros_like(dl_ref)

        dyv = dy_ref[...]
        uv = u_ref[...]
        dyb = dyv.astype(BF16)
        ub = uv.astype(BF16)
        lr1 = tab_ref[6, 7:8, :]
        li1 = tab_ref[7, 7:8, :]
        for kt in range(S5_KT):
            cols = pl.ds(kt * 1024, 1024)
            colsi = pl.ds(S5_N + kt * 1024, 1024)
            g_scr[:, cols] = _dot(dyb[:, kt * 256:(kt + 1) * 256], ct_ref[0, kt])
            g_scr[:, colsi] = _dot(dyb[:, kt * 256:(kt + 1) * 256], ct_ref[1, kt])
            bur = _dot(ub[:, kt * 256:(kt + 1) * 256], b_ref[0, kt])
            bui = _dot(ub[:, kt * 256:(kt + 1) * 256], b_ref[1, kt])
            xpr = xp_ref[:, cols]
            xpi = xp_ref[:, colsi]
            lrk = lr1[:, kt * 1024:(kt + 1) * 1024]
            lik = li1[:, kt * 1024:(kt + 1) * 1024]
            x_scr[:, cols] = lrk * xpr - lik * xpi + bur
            x_scr[:, colsi] = lrk * xpi + lik * xpr + bui

        def chunk(j, c):
            cr, ci = c
            r0 = pl.multiple_of((nch - 1 - j) * 8, 8)
            gr = g_scr[pl.ds(r0, 8), pl.ds(0, S5_N)]
            gi = g_scr[pl.ds(r0, 8), pl.ds(S5_N, S5_N)]
            for lvl, (sh, _) in enumerate(SCAN_SHIFTS):
                sr = pltpu.roll(gr, 8 - sh, 0)
                si = pltpu.roll(gi, 8 - sh, 0)
                lr = tab_ref[lvl]
                li = tab_ref[3 + lvl]
                gr, gi = gr + lr * sr + li * si, gi + lr * si - li * sr
            pvr = tab_ref[6]
            pvi = tab_ref[7]
            gr, gi = gr + pvr * cr + pvi * ci, gi + pvr * ci - pvi * cr
            g_scr[pl.ds(r0, 8), pl.ds(0, S5_N)] = gr
            g_scr[pl.ds(r0, 8), pl.ds(S5_N, S5_N)] = gi
            xpr = xp_ref[pl.ds(r0, 8), pl.ds(0, S5_N)]
            xpi = xp_ref[pl.ds(r0, 8), pl.ds(S5_N, S5_N)]
            dl_ref[0] += gr * xpr + gi * xpi
            dl_ref[1] += gi * xpr - gr * xpi
            return gr[0:1, :], gi[0:1, :]

        cr, ci = lax.fori_loop(0, nch, chunk, (carry[0:1, :], carry[1:2, :]))
        carry[0:1, :] = cr
        carry[1:2, :] = ci

        for kt in range(S5_KT):
            du = jnp.zeros((tm, 256), F32)
            ukt = ub[:, kt * 256:(kt + 1) * 256]
            dykt = dyb[:, kt * 256:(kt + 1) * 256]
            for part in range(2):
                gb = g_scr[:, pl.ds(part * S5_N + kt * 1024, 1024)].astype(BF16)
                xb = x_scr[:, pl.ds(part * S5_N + kt * 1024, 1024)].astype(BF16)
                du = du + _dot(gb, bt_ref[part, kt])
                dbv = _dot_tn(ukt, gb)
                dcv = _dot_tn(xb, dykt)

                @pl.when(first)
                def _():
                    db_ref[part, kt] = dbv
                    dc_ref[part, kt] = dcv

                @pl.when(jnp.logical_not(first))
                def _():
                    db_ref[part, kt] += dbv
                    dc_ref[part, kt] += dcv
            du_ref[:, pl.ds(kt * 256, 256)] = du + d_ref[:, pl.ds(kt * 256, 256)] * dyv[:, kt * 256:(kt + 1) * 256]
        _accum(dd_ref, jnp.sum(dyv * uv, axis=0, keepdims=True), first)

    rev = lambda i: (nt - 1 - i, 0)
    return _carry(
        body, comm, name="s5_bwd", steps=nt,
        out_shape=(jax.ShapeDtypeStruct((t, S5_WIDTH), F32),
                   jax.ShapeDtypeStruct((2, S5_KT, 256, 1024), F32),
                   jax.ShapeDtypeStruct((2, S5_KT, 1024, 256), F32),
                   jax.ShapeDtypeStruct((2, 8, S5_N), F32),
                   jax.ShapeDtypeStruct((1, S5_WIDTH), F32)),
        in_specs=[pl.BlockSpec((tm, S5_WIDTH), rev), pl.BlockSpec((tm, S5_WIDTH), rev),
                  pl.BlockSpec((tm, 2 * S5_N), rev),
                  _const_spec((8, 8, S5_N)),
                  _const_spec((2, S5_KT, 256, 1024)), _const_spec((2, S5_KT, 1024, 256)),
                  _const_spec((2, S5_KT, 256, 1024)), _const_spec((1, S5_WIDTH))],
        out_specs=(pl.BlockSpec((tm, S5_WIDTH), rev),
                   pl.BlockSpec((2, S5_KT, 256, 1024), lambda i: (0, 0, 0, 0)),
                   pl.BlockSpec((2, S5_KT, 1024, 256), lambda i: (0, 0, 0, 0)),
                   pl.BlockSpec((2, 8, S5_N), lambda i: (0, 0, 0)),
                   pl.BlockSpec((1, S5_WIDTH), lambda i: (0, 0))),
        scratch_shapes=[pltpu.VMEM((tm, 2 * S5_N), F32), pltpu.VMEM((tm, 2 * S5_N), F32),
                        pltpu.VMEM((8, S5_N), F32)],
        args=(dy, u, xp, tab, bmat, bmat_t, cmat_t, dvec),
    )


def _hg_gates(z, lb):
    sg = _sigmoid(z)
    sgn = _sigmoid(-z)
    fg = lb + (1.0 - lb) * sg
    return sg, sgn, fg, jnp.log(fg), (1.0 - lb) * sgn


def _hg_decays(g, tril):
    gc = jnp.dot(tril, g, precision=HIGHEST, preferred_element_type=F32)
    mid = gc[CHUNK // 2 - 1:CHUNK // 2, :]
    last = gc[CHUNK - 1:CHUNK, :]
    return jnp.exp(gc), jnp.exp(gc - mid), jnp.exp(mid - gc), jnp.exp(last - gc), jnp.exp(last)


def _split_bf16(x):
    hi = x.astype(BF16)
    return hi, (x - hi.astype(F32)).astype(BF16)


def _hg_scores(qt, qlo, kt, klo, sl, causal):
    a = _dot_nt(qt[:, sl], kt[:, sl]) + _dot_nt(qt[:, sl], klo[:, sl]) + _dot_nt(qlo[:, sl], kt[:, sl])
    return jnp.where(causal, a, 0.0).astype(BF16)


def hgrn_fwd(q, f, v, lb):
    t = q.shape[0]
    nc = t // CHUNK
    scale = HG_E ** -0.5

    def body(q_ref, f_ref, v_ref, lb_ref, o_ref, st_ref, state):
        @pl.when(pl.program_id(0) == 0)
        def _():
            state[...] = jnp.zeros_like(state)

        ri = lax.broadcasted_iota(jnp.int32, (CHUNK, CHUNK), 0)
        ci = lax.broadcasted_iota(jnp.int32, (CHUNK, CHUNK), 1)
        causal = ri >= ci
        tril = causal.astype(F32)
        for sub in range(HG_SUB):
            rows = pl.ds(sub * CHUNK, CHUNK)
            _, _, _, g, k = _hg_gates(f_ref[rows, :], lb_ref[...])
            eg, eq, ek, ed, el = _hg_decays(g, tril)
            qs = q_ref[rows, :] * scale
            qg = (qs * eg).astype(BF16)
            qt, qlo = _split_bf16(qs * eq)
            kt, klo = _split_bf16(k * ek)
            kd = (k * ed).astype(BF16)
            vb = v_ref[rows, :].astype(BF16)
            for h in range(HG_HEADS):
                sl = slice(h * HG_E, (h + 1) * HG_E)
                st = state[h]
                a = _hg_scores(qt, qlo, kt, klo, sl, causal)
                o_ref[rows, sl] = _dot(a, vb[:, sl]) + _dot_nt(qg[:, sl], st.astype(BF16))
                st_new = st * el[:, sl] + _dot_tn(vb[:, sl], kd[:, sl])
                state[h] = st_new
                st_ref[sub, h] = st_new

    tok = pl.BlockSpec((HG_SUB * CHUNK, HG_WIDTH), lambda i: (i, 0))
    return pl.pallas_call(
        body, name="hgrn_fwd", grid=(nc // HG_SUB,),
        out_shape=(jax.ShapeDtypeStruct((t, HG_WIDTH), F32),
                   jax.ShapeDtypeStruct((nc, HG_HEADS, HG_E, HG_E), F32)),
        in_specs=[tok, tok, tok, _const_spec((1, HG_WIDTH))],
        out_specs=(tok, pl.BlockSpec((HG_SUB, HG_HEADS, HG_E, HG_E), lambda i: (i, 0, 0, 0))),
        scratch_shapes=[pltpu.VMEM((HG_HEADS, HG_E, HG_E), F32)],
        compiler_params=_cparams(("arbitrary",)),
    )(q, f, v, lb)


def hgrn_bwd(do, q, f, v, lb, states, comm=None):
    t = q.shape[0]
    nc = t // CHUNK
    scale = HG_E ** -0.5

    ns = nc // HG_SUB

    def body(do_ref, q_ref, f_ref, v_ref, lb_ref, scur_ref, sprev_ref, dq_ref, df_ref, dv_ref, dlb_ref, dstate):
        first = pl.program_id(0) == 0
        has_prev = jnp.where(pl.program_id(0) < ns - 1, 1.0, 0.0)

        @pl.when(first)
        def _():
            dstate[...] = jnp.zeros_like(dstate)

        ri = lax.broadcasted_iota(jnp.int32, (CHUNK, CHUNK), 0)
        ci = lax.broadcasted_iota(jnp.int32, (CHUNK, CHUNK), 1)
        causal = ri >= ci
        tril = causal.astype(F32)
        triu = (ri <= ci).astype(F32)
        rowc = lax.broadcasted_iota(jnp.int32, (CHUNK, HG_WIDTH), 0)
        lb = lb_ref[...]
        dlb = jnp.zeros((1, HG_WIDTH), F32)
        for sub in reversed(range(HG_SUB)):
            rows = pl.ds(sub * CHUNK, CHUNK)
            sg, sgn, fg, g, k = _hg_gates(f_ref[rows, :], lb)
            eg, eq, ek, ed, el = _hg_decays(g, tril)
            qs = q_ref[rows, :] * scale
            qg = (qs * eg).astype(BF16)
            qt, qlo = _split_bf16(qs * eq)
            kt, klo = _split_bf16(k * ek)
            kd = (k * ed).astype(BF16)
            vb = v_ref[rows, :].astype(BF16)
            dob = do_ref[rows, :].astype(BF16)
            dqs_l, dk_l, dgc_l, dgl_l = [], [], [], []
            for h in range(HG_HEADS):
                sl = slice(h * HG_E, (h + 1) * HG_E)
                s0 = scur_ref[sub - 1, h] if sub > 0 else sprev_ref[HG_SUB - 1, h] * has_prev
                ds1 = dstate[h]
                ds1b = ds1.astype(BF16)
                a = _hg_scores(qt, qlo, kt, klo, sl, causal)
                da = jnp.where(causal, _dot_nt(dob[:, sl], vb[:, sl]), 0.0).astype(BF16)
                dv_ref[rows, sl] = _dot_tn(a, dob[:, sl]) + _dot_nt(kd[:, sl], ds1b)
                dkd = _dot(vb[:, sl], ds1b)
                dqt = _dot(da, kt[:, sl])
                dkt = _dot_tn(da, qt[:, sl])
                dqg = _dot(dob[:, sl], s0.astype(BF16))
                dqs_l.append(dqt * eq[:, sl] + dqg * eg[:, sl])
                dk_l.append(dkt * ek[:, sl] + dkd * ed[:, sl])
                kd_dkd = kd[:, sl].astype(F32) * dkd
                dgc_l.append(qt[:, sl].astype(F32) * dqt - kt[:, sl].astype(F32) * dkt
                             + qg[:, sl].astype(F32) * dqg - kd_dkd)
                dgl_l.append(el[:, sl] * jnp.sum(ds1 * s0, axis=0, keepdims=True)
                             + jnp.sum(kd_dkd, axis=0, keepdims=True))
                dstate[h] = ds1 * el[:, sl] + _dot_tn(dob[:, sl], qg[:, sl])
            dqs = jnp.concatenate(dqs_l, axis=1)
            dk = jnp.concatenate(dk_l, axis=1)
            dgl = jnp.concatenate(dgl_l, axis=1)
            dq_ref[rows, :] = dqs * scale
            dgc = jnp.concatenate(dgc_l, axis=1) + jnp.where(rowc == CHUNK - 1, dgl, 0.0)
            dg = jnp.dot(triu, dgc, precision=HIGHEST, preferred_element_type=F32)
            w = dg / fg - dk
            df_ref[rows, :] = w * (1.0 - lb) * sg * sgn
            dlb = dlb + jnp.sum(w * sgn, axis=0, keepdims=True)
        _accum(dlb_ref, dlb, first)

    rev = lambda i: (ns - 1 - i, 0)
    tok = pl.BlockSpec((HG_SUB * CHUNK, HG_WIDTH), rev)
    st_blk = (HG_SUB, HG_HEADS, HG_E, HG_E)
    return _carry(
        body, comm, name="hgrn_bwd", steps=ns,
        out_shape=(jax.ShapeDtypeStruct((t, HG_WIDTH), F32),
                   jax.ShapeDtypeStruct((t, HG_WIDTH), F32),
                   jax.ShapeDtypeStruct((t, HG_WIDTH), F32),
                   jax.ShapeDtypeStruct((1, HG_WIDTH), F32)),
        in_specs=[tok, tok, tok, tok, _const_spec((1, HG_WIDTH)),
                  pl.BlockSpec(st_blk, lambda i: (ns - 1 - i, 0, 0, 0)),
                  pl.BlockSpec(st_blk, lambda i: (jnp.maximum(ns - 2 - i, 0), 0, 0, 0))],
        out_specs=(tok, tok, tok, pl.BlockSpec((1, HG_WIDTH), lambda i: (0, 0))),
        scratch_shapes=[pltpu.VMEM((HG_HEADS, HG_E, HG_E), F32)],
        args=(do, q, f, v, lb, states, states),
    )


GELU_C = math.sqrt(2.0 / math.pi)


def _gelu(x):
    th = jnp.tanh(GELU_C * (x + 0.044715 * x * x * x))
    return 0.5 * x * (1.0 + th), th


def _merge_core(ys5, o, og, ga, gb, wv_ref, wt_ref, ghg, who_ref):
    ys, th = _gelu(ys5)
    ysb = ys.astype(BF16)
    va = jnp.concatenate([_dot(ysb, wv_ref[s]) for s in range(N_SHARD)], axis=1)
    vt = jnp.concatenate([_dot(ysb, wt_ref[s]) for s in range(N_SHARD)], axis=1)
    svt = _sigmoid(vt)
    ya = va * svt
    rs, ons = [], []
    for h in range(HG_HEADS):
        oh = o[:, h * HG_E:(h + 1) * HG_E]
        r = lax.rsqrt(jnp.mean(oh * oh, axis=-1, keepdims=True) + NORM_EPS)
        rs.append(r)
        ons.append(oh * r)
    on = jnp.concatenate(ons, axis=1)
    sgo = _sigmoid(og)
    o2 = on * ghg * (og * sgo)
    o2b = o2.astype(BF16)
    yb = _dot(o2b, who_ref[...])
    sa = _sigmoid(ga)
    sb = _sigmoid(gb)
    mixed = sa * ya + sb * yb
    return dict(ys=ys, th=th, ysb=ysb, va=va, svt=svt, ya=ya, rs=rs, on=on, sgo=sgo, o2b=o2b, yb=yb,
                sa=sa, sb=sb, mixed=mixed)


def merge_fwd(h, ys5, o, og, ga, gb, wv, wt, ghg, who, wmo, tm=256):
    t = h.shape[0]

    def body(h_ref, ys5_ref, o_ref, og_ref, ga_ref, gb_ref, wv_ref, wt_ref, ghg_ref, who_ref, wmo_ref, out_ref):
        c = _merge_core(ys5_ref[...], o_ref[...], og_ref[...], ga_ref[...], gb_ref[...],
                        wv_ref, wt_ref, ghg_ref[...], who_ref)
        out_ref[...] = h_ref[...] + _dot(c["mixed"].astype(BF16), wmo_ref[...])

    tok = pl.BlockSpec((tm, D_MODEL), lambda i: (i, 0))
    return pl.pallas_call(
        body, name="merge_fwd", grid=(t // tm,),
        out_shape=jax.ShapeDtypeStruct((t, D_MODEL), F32),
        in_specs=[tok, pl.BlockSpec((tm, S5_WIDTH), lambda i: (i, 0)), tok, tok, tok, tok,
                  _const_spec((N_SHARD, S5_WIDTH, 256)), _const_spec((N_SHARD, S5_WIDTH, 256)),
                  _const_spec((1, HG_WIDTH)), _const_spec((HG_WIDTH, D_MODEL)), _const_spec((D_MODEL, D_MODEL))],
        out_specs=tok,
        compiler_params=_cparams(("arbitrary",)),
    )(h, ys5, o, og, ga, gb, wv, wt, ghg, who, wmo)


def merge_bwd(dh, ys5, o, og, ga, gb, wv, wt, ghg, who, wmo, comm=None, tm=256):
    t = dh.shape[0]

    def body(dh_ref, ys5_ref, o_ref, og_ref, ga_ref, gb_ref, wv_ref, wt_ref, ghg_ref, who_ref, wmo_ref,
             dys5_ref, do_ref, dog_ref, dga_ref, dgb_ref, dghg_ref,
             mixb_ref, dhb_ref, ysb_ref, dvab_ref, dvtb_ref, o2b_ref, dybb_ref):
        ys5 = ys5_ref[...]
        o = o_ref[...]
        og = og_ref[...]
        ghg = ghg_ref[...]
        c = _merge_core(ys5, o, og, ga_ref[...], gb_ref[...], wv_ref, wt_ref, ghg, who_ref)
        dhb = dh_ref[...].astype(BF16)
        dhb_ref[...] = dhb
        mixb_ref[...] = c["mixed"].astype(BF16)
        ysb_ref[...] = c["ysb"]
        o2b_ref[...] = c["o2b"]
        dmix = _dot_nt(dhb, wmo_ref[...])
        sa, sb = c["sa"], c["sb"]
        dya = dmix * sa
        dyb = dmix * sb
        dga_ref[...] = dmix * c["ya"] * sa * (1.0 - sa)
        dgb_ref[...] = dmix * c["yb"] * sb * (1.0 - sb)
        svt = c["svt"]
        dva = (dya * svt).astype(BF16)
        dvt = (dya * c["va"] * svt * (1.0 - svt)).astype(BF16)
        dvab_ref[...] = dva
        dvtb_ref[...] = dvt
        dys = jnp.zeros((tm, S5_WIDTH), F32)
        for s in range(N_SHARD):
            dys = dys + _dot_nt(dva[:, s * 256:(s + 1) * 256], wv_ref[s]) + _dot_nt(dvt[:, s * 256:(s + 1) * 256], wt_ref[s])
        th = c["th"]
        dgelu = 0.5 * (1.0 + th) + 0.5 * ys5 * (1.0 - th * th) * GELU_C * (1.0 + 3.0 * 0.044715 * ys5 * ys5)
        dys5_ref[...] = dys * dgelu
        dybb = dyb.astype(BF16)
        dybb_ref[...] = dybb
        do2 = _dot_nt(dybb, who_ref[...])
        sgo = c["sgo"]
        sil = og * sgo
        on = c["on"]
        dog_ref[...] = do2 * on * ghg * (sgo * (1.0 + og * (1.0 - sgo)))
        _accum(dghg_ref, jnp.sum(do2 * on * sil, axis=0, keepdims=True), pl.program_id(0) == 0)
        don = do2 * ghg * sil
        dos = []
        for h in range(HG_HEADS):
            sl = slice(h * HG_E, (h + 1) * HG_E)
            m = jnp.mean(don[:, sl] * on[:, sl], axis=-1, keepdims=True)
            dos.append(c["rs"][h] * (don[:, sl] - on[:, sl] * m))
        do_ref[...] = jnp.concatenate(dos, axis=1)

    tok = pl.BlockSpec((tm, D_MODEL), lambda i: (i, 0))
    s5b = pl.BlockSpec((tm, S5_WIDTH), lambda i: (i, 0))
    f32t = jax.ShapeDtypeStruct((t, D_MODEL), F32)
    bft = jax.ShapeDtypeStruct((t, D_MODEL), BF16)
    return _carry(
        body, comm, name="merge_bwd", steps=t // tm,
        out_shape=(jax.ShapeDtypeStruct((t, S5_WIDTH), F32), f32t, f32t, f32t, f32t,
                   jax.ShapeDtypeStruct((1, HG_WIDTH), F32),
                   bft, bft, jax.ShapeDtypeStruct((t, S5_WIDTH), BF16), bft, bft, bft, bft),
        in_specs=[tok, s5b, tok, tok, tok, tok,
                  _const_spec((N_SHARD, S5_WIDTH, 256)), _const_spec((N_SHARD, S5_WIDTH, 256)),
                  _const_spec((1, HG_WIDTH)), _const_spec((HG_WIDTH, D_MODEL)), _const_spec((D_MODEL, D_MODEL))],
        out_specs=(s5b, tok, tok, tok, tok, pl.BlockSpec((1, HG_WIDTH), lambda i: (0, 0)),
                   tok, tok, s5b, tok, tok, tok, tok),
        args=(dh, ys5, o, og, ga, gb, wv, wt, ghg, who, wmo),
    )


def head_fwd_bwd(h, p, tgt, gple, wpg, wpp, gfin, tm=256):
    t = h.shape[0]

    def body(h_ref, p_ref, tgt_ref, gple_ref, wpg_ref, wpp_ref, gfin_ref,
             loss_ref, dh_ref, dgple_ref, dgfin_ref, nb_ref, dlb_ref, dppb_ref):
        first = pl.program_id(0) == 0
        hv = h_ref[...]
        gple = gple_ref[...]
        gfin = gfin_ref[...]
        n, r3 = _rms_fwd(hv, gple)
        nb = n.astype(BF16)
        nb_ref[...] = nb
        pg = _sigmoid(_dot(nb, wpg_ref[...]))
        pb = p_ref[...].astype(BF16)
        pp = jnp.concatenate([_dot(pb, wpp_ref[s]) for s in range(N_SHARD)], axis=1)
        h4 = hv + pg * pp
        y, r4 = _rms_fwd(h4, gfin)
        err = y - tgt_ref[...]
        lsum = 0.5 * jnp.sum(jnp.sum(err * err, axis=-1, keepdims=True), axis=0, keepdims=True) / D_MODEL
        _accum(loss_ref, jnp.broadcast_to(lsum, (8, 128)), first)
        dy = err * (1.0 / D_MODEL)
        dh4, dgf = _rms_bwd(h4, r4, gfin, dy)
        _accum(dgfin_ref, dgf, first)
        dpp = dh4 * pg
        dppb_ref[...] = dpp.astype(BF16)
        dl = (dh4 * pp * pg * (1.0 - pg)).astype(BF16)
        dlb_ref[...] = dl
        dn = _dot_nt(dl, wpg_ref[...])
        dx, dgp = _rms_bwd(hv, r3, gple, dn)
        _accum(dgple_ref, dgp, first)
        dh_ref[...] = dh4 + dx

    tok = pl.BlockSpec((tm, D_MODEL), lambda i: (i, 0))
    vec = pl.BlockSpec((1, D_MODEL), lambda i: (0, 0))
    bft = jax.ShapeDtypeStruct((t, D_MODEL), BF16)
    return pl.pallas_call(
        body, name="head_fwd_bwd", grid=(t // tm,),
        out_shape=(jax.ShapeDtypeStruct((8, 128), F32), jax.ShapeDtypeStruct((t, D_MODEL), F32),
                   jax.ShapeDtypeStruct((1, D_MODEL), F32), jax.ShapeDtypeStruct((1, D_MODEL), F32),
                   bft, bft, bft),
        in_specs=[tok, pl.BlockSpec((tm, PLE_DIM), lambda i: (i, 0)), tok,
                  _const_spec((1, D_MODEL)), _const_spec((D_MODEL, D_MODEL)),
                  _const_spec((N_SHARD, PLE_DIM, 256)), _const_spec((1, D_MODEL))],
        out_specs=(pl.BlockSpec((8, 128), lambda i: (0, 0)), tok, vec, vec, tok, tok, tok),
        compiler_params=_cparams(("arbitrary",)),
    )(h, p, tgt, gple, wpg, wpp, gfin)


BIG = ("ffn1_w_gate", "ffn1_w_up", "ffn1_w_down", "w_in", "s5_glu_val", "s5_glu_gate", "hg_w_out",
       "w_merge_out", "ffn2_w_gate", "ffn2_w_up", "ffn2_w_down", "ple_w_gate", "ple_w_proj")
FFN_T = ("ffn1_w_gate", "ffn1_w_up", "ffn2_w_gate", "ffn2_w_up")
BIG_SHARD = {
    "ffn1_w_gate": (FF_PAD, D_MODEL), "ffn1_w_up": (FF_PAD, D_MODEL), "ffn1_w_down": (FF_PAD, D_MODEL),
    "ffn2_w_gate": (FF_PAD, D_MODEL), "ffn2_w_up": (FF_PAD, D_MODEL), "ffn2_w_down": (FF_PAD, D_MODEL),
    "w_in": (D_MODEL, IN_COLS // N_SHARD), "s5_glu_val": (S5_WIDTH, 256), "s5_glu_gate": (S5_WIDTH, 256),
    "hg_w_out": (256, D_MODEL), "w_merge_out": (256, D_MODEL), "ple_w_gate": (256, D_MODEL),
    "ple_w_proj": (PLE_DIM, 256),
}


def _lower_bound(hb):
    return jax.nn.softmax(hb, axis=0)[0:1]


class Schedule:
    def __init__(self, wts):
        self.wts = dict(wts)
        self.grads = {}

    def before(self, kernel_name):
        return None

    def after(self, kernel_name, results):
        pass

    def grad(self, name, g):
        self.grads[name] = g


def local_step(x, p, tgt, sched, sm):
    wts = sched.wts
    rows_full = lambda w: w.reshape(N_SHARD * w.shape[1], w.shape[2])

    def carried(kernel_name, fn, *args):
        outs, results = fn(*args, comm=sched.before(kernel_name))
        sched.after(kernel_name, results)
        return outs

    def weight_grad(name, xs, ys, shard):
        kernel_name = "g_" + name
        (g,), results = tn_matmul(xs, ys, kernel_name, shard, comm=sched.before(kernel_name))
        sched.grad(name, g)
        sched.after(kernel_name, results)

    lb, lb_vjp = jax.vjp(_lower_bound, sm["hg_lower_bound"])
    s5_names = ("s5_lam_re", "s5_lam_im", "s5_log_dt", "s5_b_re", "s5_b_im", "s5_c_re", "s5_c_im")
    (lam_bar, bmat, cmat), s5_vjp = jax.vjp(s5_prep, *[sm[k] for k in s5_names])
    pw_r, pw_i = _lam_powers(lam_bar)
    bmat_b = bmat.astype(BF16)
    cmat_b = cmat.astype(BF16)
    bmat_t = jnp.swapaxes(bmat, -1, -2).astype(BF16)
    cmat_t = jnp.swapaxes(cmat, -1, -2).astype(BF16)

    h1, a1, b1 = carried("ffn1_fwd", ffn_fwd, x, sm["ffn1_norm"], wts["ffn1_w_gate"], wts["ffn1_w_up"],
                         wts["ffn1_w_down"], "ffn1_fwd")
    s5in, q, f, v, og, ga, gb = inproj_fwd(h1, sm["mix_norm"], wts["w_in"])
    ys5, xp = carried("s5_fwd", s5_fwd, s5in, _scan_tables(pw_r, pw_i, False), bmat_b, cmat_b, sm["s5_d"])
    o, states = hgrn_fwd(q, f, v, lb)
    who = rows_full(wts["hg_w_out"])
    wmo = rows_full(wts["w_merge_out"])
    h2 = merge_fwd(h1, ys5, o, og, ga, gb, wts["s5_glu_val"], wts["s5_glu_gate"], sm["hg_out_norm"], who, wmo)
    (h3, a2, b2), _ = ffn_fwd(h2, sm["ffn2_norm"], wts["ffn2_w_gate"], wts["ffn2_w_up"], wts["ffn2_w_down"], "ffn2_fwd")
    loss, dh3, d_ple_norm, d_final_norm, npb, dlgb, dppb = head_fwd_bwd(
        h3, p, tgt, sm["ple_norm"], rows_full(wts["ple_w_gate"]), wts["ple_w_proj"], sm["final_norm"])

    gs = {"ple_norm": d_ple_norm, "final_norm": d_final_norm}
    weight_grad("ple_w_gate", npb, dlgb, "rows")
    weight_grad("ple_w_proj", p, dppb, "cols")

    (dh2, gs["ffn2_norm"], n2b, dhb2, da2, db2, s2), _ = ffn_bwd(
        dh3, h2, a2, b2, sm["ffn2_norm"], wts["ffn2_w_gate"], wts["ffn2_w_up"], wts["ffn2_w_down"], "ffn2_bwd")
    weight_grad("ffn2_w_gate", da2, n2b, "rows")
    weight_grad("ffn2_w_up", db2, n2b, "rows")
    weight_grad("ffn2_w_down", s2, dhb2, "rows")

    dys5, do, dog, dga, dgb, gs["hg_out_norm"], mixb, dh2b, ysb, dvab, dvtb, o2b, dybb = carried(
        "merge_bwd", merge_bwd,
        dh2, ys5, o, og, ga, gb, wts["s5_glu_val"], wts["s5_glu_gate"], sm["hg_out_norm"], who, wmo)
    weight_grad("w_merge_out", mixb, dh2b, "rows")
    weight_grad("s5_glu_val", ysb, dvab, "cols")
    weight_grad("s5_glu_gate", ysb, dvtb, "cols")
    weight_grad("hg_w_out", o2b, dybb, "rows")

    dq, df, dv, dlb = carried("hgrn_bwd", hgrn_bwd, do, q, f, v, lb, states)
    (gs["hg_lower_bound"],) = lb_vjp(dlb)
    du, dbmat, dcmat, dlam8, gs["s5_d"] = carried(
        "s5_bwd", s5_bwd,
        dys5, s5in, xp, _scan_tables(pw_r, pw_i, True), bmat_b, bmat_t, cmat_t, sm["s5_d"])
    for k, g in zip(s5_names, s5_vjp((jnp.sum(dlam8, axis=1), dbmat, dcmat))):
        gs[k] = g

    dh1, gs["mix_norm"], nmb, dprojb = carried(
        "inproj_bwd", inproj_bwd, dh2, h1, sm["mix_norm"], wts["w_in"], (du, dq, df, dv, dog, dga, dgb))
    weight_grad("w_in", nmb, dprojb, "cols")

    dx, gs["ffn1_norm"], n1b, dhb1, da1, db1, s1 = carried(
        "ffn1_bwd", ffn_bwd,
        dh1, x, a1, b1, sm["ffn1_norm"], wts["ffn1_w_gate"], wts["ffn1_w_up"], wts["ffn1_w_down"], "ffn1_bwd")
    weight_grad("ffn1_w_gate", da1, n1b, "rows")
    weight_grad("ffn1_w_up", db1, n1b, "rows")
    weight_grad("ffn1_w_down", s1, dhb1, "rows")
    return loss, dx, gs


MESH = pl.DeviceIdType.MESH
ANY = pl.BlockSpec(memory_space=pl.ANY)


def _place():
    x, y, c = lax.axis_index("x"), lax.axis_index("y"), lax.axis_index("c")
    return x, y, c


def _remote(src, dst, ssem, rsem, dev):
    return pltpu.make_async_remote_copy(src_ref=src, dst_ref=dst, send_sem=ssem, recv_sem=rsem,
                                        device_id=dev, device_id_type=MESH)


class Comm:
    def __init__(self, bufs, outs, alias, sems, hooks):
        self.bufs, self.outs, self.alias, self.sems, self.hooks = list(bufs), list(outs), alias, list(sems), hooks


def run_comm(comm, name):
    nb, no = len(comm.bufs), len(comm.outs)

    def body(*refs):
        for which in ("first", "mid", "last"):
            if which in comm.hooks:
                comm.hooks[which](refs[:nb], refs[nb:nb + no], refs[nb + no:])

    return pl.pallas_call(
        body, name=name, out_shape=tuple(comm.outs), in_specs=[ANY] * nb, out_specs=tuple([ANY] * no),
        input_output_aliases=dict(comm.alias), scratch_shapes=comm.sems,
    )(*comm.bufs)


PLACE_ROWS = {1024: 256, 704: 352, 512: 256, 256: 256}


def place_shards(shards, padded_rows, comm, name):
    n, nb, no = len(shards), len(comm.bufs), len(comm.outs)
    stage_rows = max(PLACE_ROWS.values())
    stage_cols = max(s.shape[1] for s in shards)

    def body(*refs):
        ins, cb = refs[:n], refs[n:n + nb]
        outs, co = refs[n + nb:2 * n + nb], refs[2 * n + nb:2 * n + nb + no]
        stage_f32, stage_bf16, zeros, sem = refs[2 * n + nb + no:2 * n + nb + no + 4]
        cs = refs[2 * n + nb + no + 4:]
        chip = 2 * lax.axis_index("x") + lax.axis_index("y")
        zeros[...] = jnp.zeros_like(zeros)
        comm.hooks["first"](cb, co, cs)
        for w in range(n):
            if w == n // 2:
                comm.hooks["mid"](cb, co, cs)
            r0, cols = ins[w].shape
            step = PLACE_ROWS[r0]
            src32 = stage_f32.at[pl.ds(0, step), pl.ds(0, cols)]
            dst16 = stage_bf16.at[pl.ds(0, step), pl.ds(0, cols)]
            for row in range(0, r0, step):
                pltpu.sync_copy(ins[w].at[pl.ds(row, step), :], src32)
                dst16[...] = src32[...].astype(BF16)
                pltpu.sync_copy(dst16, outs[w].at[chip, pl.ds(row, step), :])
            pad = outs[w].shape[1] - r0
            if pad:
                cp = pltpu.make_async_copy(zeros.at[pl.ds(0, pad), pl.ds(0, cols)],
                                           outs[w].at[chip, pl.ds(r0, pad), :], sem)
                cp.start()
                cp.wait()
        comm.hooks["last"](cb, co, cs)

    res = pl.pallas_call(
        body, name=name,
        out_shape=tuple(jax.ShapeDtypeStruct((N_SHARD, r, s.shape[1]), BF16) for s, r in zip(shards, padded_rows))
        + tuple(comm.outs),
        in_specs=[ANY] * (n + nb), out_specs=tuple([ANY] * (n + no)),
        input_output_aliases={n + i: n + o for i, o in comm.alias.items()},
        scratch_shapes=[pltpu.VMEM((stage_rows, stage_cols), F32), pltpu.VMEM((stage_rows, stage_cols), BF16),
                        pltpu.VMEM((FF_PAD - FF_SHARD, D_MODEL), BF16), pltpu.SemaphoreType.DMA] + comm.sems,
        compiler_params=pltpu.CompilerParams(vmem_limit_bytes=VMEM_LIMIT),
    )(*shards, *comm.bufs)
    return res[:n], res[n:]


def _carry(body, comm, *, name, steps, out_shape, in_specs, out_specs, args, scratch_shapes=()):
    out_shape, out_specs, scratch_shapes = tuple(out_shape), tuple(out_specs), list(scratch_shapes)
    if comm is None:
        res = pl.pallas_call(body, name=name, grid=(steps,), out_shape=out_shape, in_specs=list(in_specs),
                             out_specs=out_specs, scratch_shapes=scratch_shapes,
                             compiler_params=_cparams(("arbitrary",)))(*args)
        return tuple(res), ()
    n_in, n_out, n_scr = len(args), len(out_shape), len(scratch_shapes)
    nb, no = len(comm.bufs), len(comm.outs)

    def wrapped(*refs):
        ins, cb = refs[:n_in], refs[n_in:n_in + nb]
        o0 = n_in + nb
        outs, co = refs[o0:o0 + n_out], refs[o0 + n_out:o0 + n_out + no]
        s0 = o0 + n_out + no
        scr, cs = refs[s0:s0 + n_scr], refs[s0 + n_scr:]
        step = pl.program_id(0)

        def hook(which, at):
            if which in comm.hooks:
                pl.when(step == at)(lambda: comm.hooks[which](cb, co, cs))

        hook("first", 0)
        hook("mid", steps // 2)
        body(*ins, *outs, *scr)
        hook("last", steps - 1)

    res = pl.pallas_call(
        wrapped, name=name, grid=(steps,), out_shape=out_shape + tuple(comm.outs),
        in_specs=list(in_specs) + [ANY] * nb, out_specs=out_specs + (ANY,) * no,
        scratch_shapes=scratch_shapes + comm.sems,
        input_output_aliases={n_in + i: n_out + o for i, o in comm.alias.items()},
        compiler_params=_cparams(("arbitrary",)),
    )(*args, *comm.bufs)
    return tuple(res[:n_out]), tuple(res[n_out:])


def gather_comm(bufs):
    n = len(bufs)

    def copies(outs, sems):
        s_own, r_own, s_fwd, r_fwd, s_sib, r_sib = sems
        x, y, c = _place()
        me = 2 * x + y
        nbr = ((1 - x, y), (x, 1 - y))
        nbr_id = (2 * (1 - x) + y, 2 * x + (1 - y))
        diag_id = 2 * (1 - x) + (1 - y)
        sib = (x, y, 1 - c)

        def rows(w, q=None):
            r = outs[w].shape[1]
            if q is None:
                return pl.ds(pl.multiple_of(c * (r // 2), 16), r // 2)
            return pl.ds(pl.multiple_of(c * (r // 2) + q * (r // 4), 16), r // 4)

        def own(w, j):
            piece = outs[w].at[me, rows(w)]
            return _remote(piece, piece, s_own.at[w, j], r_own.at[w, j], (nbr[j][0], nbr[j][1], c))

        def from_nbr(w, j):
            piece = outs[w].at[nbr_id[j], rows(w)]
            return _remote(piece, piece, s_own.at[w, j], r_own.at[w, j], (nbr[j][0], nbr[j][1], c))

        def fwd(w, j):
            piece = outs[w].at[nbr_id[j], rows(w, j)]
            return _remote(piece, piece, s_fwd.at[w, j], r_fwd.at[w, j], (nbr[1 - j][0], nbr[1 - j][1], c))

        def from_diag(w, j):
            piece = outs[w].at[diag_id, rows(w, j)]
            return _remote(piece, piece, s_fwd.at[w, j], r_fwd.at[w, j], (nbr[1 - j][0], nbr[1 - j][1], c))

        def to_sib(w, k):
            piece = (outs[w].at[nbr_id[k], rows(w)] if k < 2 else outs[w].at[diag_id, rows(w, k - 2)])
            return _remote(piece, piece, s_sib.at[w, k], r_sib.at[w, k], sib)

        def from_sib(w, k):
            r = outs[w].shape[1]
            if k < 2:
                piece = outs[w].at[nbr_id[k], pl.ds(pl.multiple_of((1 - c) * (r // 2), 16), r // 2)]
            else:
                piece = outs[w].at[diag_id, pl.ds(pl.multiple_of((1 - c) * (r // 2) + (k - 2) * (r // 4), 16), r // 4)]
            return _remote(piece, piece, s_sib.at[w, k], r_sib.at[w, k], sib)

        return own, from_nbr, fwd, from_diag, to_sib, from_sib

    def first(_, outs, sems):
        own = copies(outs, sems)[0]
        for w in range(n):
            own(w, 0).start()
            own(w, 1).start()

    def mid(_, outs, sems):
        _, from_nbr, fwd, _, to_sib, _ = copies(outs, sems)
        for w in range(n):
            for j in range(2):
                from_nbr(w, j).wait_recv()
                fwd(w, j).start()
                to_sib(w, j).start()

    def last(_, outs, sems):
        own, _, fwd, from_diag, to_sib, from_sib = copies(outs, sems)
        for w in range(n):
            for j in range(2):
                from_diag(w, j).wait_recv()
                to_sib(w, 2 + j).start()
        for w in range(n):
            for k in range(4):
                from_sib(w, k).wait_recv()
        for w in range(n):
            for j in range(2):
                own(w, j).wait_send()
                fwd(w, j).wait_send()
            for k in range(4):
                to_sib(w, k).wait_send()

    dma = pltpu.SemaphoreType.DMA
    return Comm(bufs, [jax.ShapeDtypeStruct(b.shape, b.dtype) for b in bufs], {w: w for w in range(n)},
                [dma((n, 2)), dma((n, 2)), dma((n, 2)), dma((n, 2)), dma((n, 4)), dma((n, 4))],
                {"first": first, "mid": mid, "last": last})


def _start_wait(make):
    def first(bufs, outs, sems):
        for cp in make(bufs, outs, sems):
            cp.start()

    def last(bufs, outs, sems):
        for cp in make(bufs, outs, sems):
            cp.wait()

    return {"first": first, "last": last}


def exchange_comm(grads):
    n = len(grads)

    def make(ins, outs, sems):
        x, y, c = _place()
        cps = []
        for w in range(n):
            half = ins[w].shape[1] // 2
            src = ins[w].at[:, pl.ds(pl.multiple_of((1 - c) * half, 8), half), :]
            cps.append(_remote(src, outs[w], sems[0].at[w], sems[1].at[w], (x, y, 1 - c)))
        return cps

    dma = pltpu.SemaphoreType.DMA
    return Comm(grads, [jax.ShapeDtypeStruct((N_SHARD, g.shape[1] // 2, g.shape[2]), g.dtype) for g in grads],
                {}, [dma((n,)), dma((n,))], _start_wait(make))


def scatter_comm(sums):
    n = len(sums)

    def make(ins, outs, sems):
        x, y, c = _place()
        chips = ((1 - x, y), (x, 1 - y), (1 - x, 1 - y))
        return [_remote(ins[w].at[2 * ch[0] + ch[1]], outs[w].at[j], sems[0].at[w, j], sems[1].at[w, j],
                        (ch[0], ch[1], c))
                for w in range(n) for j, ch in enumerate(chips)]

    dma = pltpu.SemaphoreType.DMA
    return Comm(sums, [jax.ShapeDtypeStruct((3,) + s.shape[1:], s.dtype) for s in sums],
                {}, [dma((n, 3)), dma((n, 3))], _start_wait(make))


def join_comm(shards):
    n = len(shards)

    def make(_, outs, sems):
        x, y, c = _place()
        cps = []
        for w in range(n):
            half = outs[w].shape[0] // 2
            mine = outs[w].at[pl.ds(pl.multiple_of(c * half, 8), half), :]
            cps.append(_remote(mine, mine, sems[0].at[w], sems[1].at[w], (x, y, 1 - c)))
        return cps

    dma = pltpu.SemaphoreType.DMA
    return Comm(shards, [jax.ShapeDtypeStruct(s.shape, s.dtype) for s in shards], {w: w for w in range(n)},
                [dma((n,)), dma((n,))], _start_wait(make))


def allreduce_small(vec):
    half = vec.shape[0] // 2

    def body(v_ref, o_ref, pair, chips_buf, s1, r1, s2, r2, s3, r3):
        x, y, c = _place()
        chip = 2 * x + y
        sib = (x, y, 1 - c)
        mine = pl.ds(pl.multiple_of(c * half, 8), half)
        other = pl.ds(pl.multiple_of((1 - c) * half, 8), half)
        to_sib = _remote(v_ref.at[other], pair, s1, r1, sib)
        to_sib.start()
        to_sib.wait()
        chips_buf[chip] = v_ref[mine, :] + pair[...]
        sends = [_remote(chips_buf.at[chip], chips_buf.at[chip], s2.at[j], r2.at[j], (ch[0], ch[1], c))
                 for j, ch in enumerate(((1 - x, y), (x, 1 - y), (1 - x, 1 - y)))]
        for cp in sends:
            cp.start()
        for cp in sends:
            cp.wait()
        o_ref[mine, :] = (chips_buf[0] + chips_buf[1]) + (chips_buf[2] + chips_buf[3])
        back = _remote(o_ref.at[mine], o_ref.at[mine], s3, r3, sib)
        back.start()
        back.wait()

    dma = pltpu.SemaphoreType.DMA
    return pl.pallas_call(
        body, name="allreduce_small",
        out_shape=jax.ShapeDtypeStruct(vec.shape, F32),
        in_specs=[pl.BlockSpec(memory_space=pltpu.VMEM)],
        out_specs=pl.BlockSpec(memory_space=pltpu.VMEM),
        scratch_shapes=[pltpu.VMEM((half, 128), F32), pltpu.VMEM((N_SHARD, half, 128), F32),
                        dma, dma, dma((3,)), dma((3,)), dma, dma],
        compiler_params=pltpu.CompilerParams(vmem_limit_bytes=VMEM_LIMIT),
    )(vec)


ROW_TILE = 128


def add_own_half(place, g, recv, name):
    _, r, cc = g.shape
    half = r // 2
    nb = half // ROW_TILE

    def body(p_ref, g_ref, r_ref, o_ref, ob_ref):
        s = g_ref[...] + r_ref[...]
        ob_ref[...] = s.astype(BF16)

        @pl.when(pl.program_id(1) == p_ref[0])
        def _():
            o_ref[...] = s

    blk = (None, ROW_TILE, cc)
    return pl.pallas_call(
        body, name=name,
        grid_spec=pltpu.PrefetchScalarGridSpec(
            num_scalar_prefetch=1, grid=(nb, N_SHARD),
            in_specs=[pl.BlockSpec(blk, lambda i, s, p_ref: (s, p_ref[1] * nb + i, 0)),
                      pl.BlockSpec(blk, lambda i, s, p_ref: (s, i, 0))],
            out_specs=(pl.BlockSpec((ROW_TILE, cc), lambda i, s, p_ref: (i, 0)),
                       pl.BlockSpec(blk, lambda i, s, p_ref: (s, i, 0)))),
        out_shape=(jax.ShapeDtypeStruct((half, cc), F32),
                   jax.ShapeDtypeStruct((N_SHARD, half, cc), BF16)),
        compiler_params=_cparams(("arbitrary", "arbitrary")),
    )(place, g, recv)


def add_chip_sums(place, own, recv, name):
    half, cc = own.shape
    nb = half // ROW_TILE

    def body(s_ref, o_ref, r_ref, out_ref):
        del s_ref
        acc = o_ref[...] + r_ref[0].astype(F32)
        acc = acc + r_ref[1].astype(F32)
        out_ref[...] = acc + r_ref[2].astype(F32)

    return pl.pallas_call(
        body, name=name,
        grid_spec=pltpu.PrefetchScalarGridSpec(
            num_scalar_prefetch=1, grid=(nb,),
            in_specs=[pl.BlockSpec((ROW_TILE, cc), lambda i, s_ref: (i, 0)),
                      pl.BlockSpec((3, ROW_TILE, cc), lambda i, s_ref: (0, i, 0))],
            out_specs=pl.BlockSpec((ROW_TILE, cc), lambda i, s_ref: (s_ref[1] * nb + i, 0))),
        out_shape=jax.ShapeDtypeStruct((2 * half, cc), F32),
        compiler_params=_cparams(("arbitrary",)),
    )(place, own, recv)


def adamw(w, m, v, g, name, copy_g=False):
    r, cc = w.shape
    tr = next(t for t in (256, 352, r) if r % t == 0)
    bc1 = 1.0 / (1.0 - ADAM_B1 ** ADAM_STEP)
    bc2 = 1.0 / (1.0 - ADAM_B2 ** ADAM_STEP)

    def body(w_ref, m_ref, v_ref, g_ref, d_ref, mo_ref, vo_ref, *go_ref):
        gv = g_ref[...]
        mn = ADAM_B1 * m_ref[...] + (1.0 - ADAM_B1) * gv
        vn = ADAM_B2 * v_ref[...] + (1.0 - ADAM_B2) * (gv * gv)
        mo_ref[...] = mn
        vo_ref[...] = vn
        d_ref[...] = -ADAM_LR * ((mn * bc1) / (jnp.sqrt(vn * bc2) + ADAM_EPS) + ADAM_WD * w_ref[...])
        if copy_g:
            go_ref[0][...] = gv

    blk = pl.BlockSpec((tr, cc), lambda i: (i, 0))
    shp = jax.ShapeDtypeStruct((r, cc), F32)
    nout = 4 if copy_g else 3
    return pl.pallas_call(
        body, name=name, grid=(r // tr,),
        out_shape=(shp,) * nout, in_specs=[blk] * 4, out_specs=(blk,) * nout,
        compiler_params=_cparams(("arbitrary",)),
    )(w, m, v, g)


GATHER_FIRST = ("ffn1_w_gate", "ffn1_w_up", "ffn1_w_down")
GATHER_ON = {"ffn1_fwd": ("w_in", "s5_glu_val", "s5_glu_gate", "hg_w_out", "w_merge_out"),
             "s5_fwd": ("ffn2_w_gate", "ffn2_w_up", "ffn2_w_down", "ple_w_gate", "ple_w_proj")}
REDUCE = ((("ple_w_gate", "ple_w_proj", "ffn2_w_gate", "ffn2_w_up", "ffn2_w_down"), "merge_bwd", "hgrn_bwd"),
          (("w_merge_out", "s5_glu_val", "s5_glu_gate", "hg_w_out"), "s5_bwd", "inproj_bwd"),
          (("w_in",), None, "ffn1_bwd"),
          (("ffn1_w_gate",), "g_ffn1_w_up", "g_ffn1_w_down"),
          (("ffn1_w_up",), "g_ffn1_w_down", None),
          (("ffn1_w_down",), None, None))


def merge_comms(comms):
    if len(comms) == 1:
        return comms[0], [len(comms[0].outs)]
    bufs, outs, sems, alias, spans = [], [], [], {}, []
    for c in comms:
        spans.append((len(bufs), len(bufs) + len(c.bufs), len(outs), len(outs) + len(c.outs),
                      len(sems), len(sems) + len(c.sems)))
        alias.update({len(bufs) + i: len(outs) + o for i, o in c.alias.items()})
        bufs, outs, sems = bufs + c.bufs, outs + c.outs, sems + c.sems

    def hook(which):
        def run(b, o, s):
            for c, (b0, b1, o0, o1, s0, s1) in zip(comms, spans):
                if which in c.hooks:
                    c.hooks[which](b[b0:b1], o[o0:o1], s[s0:s1])
        return run

    hooks = {w: hook(w) for w in ("first", "mid", "last") if any(w in c.hooks for c in comms)}
    return Comm(bufs, outs, alias, sems, hooks), [len(c.outs) for c in comms]


class DistSchedule(Schedule):
    def __init__(self, w_rows, chip, core):
        first = gather_comm([_gather_buffer(k, w_rows[k], chip) for k in GATHER_FIRST])
        later = [k for k in BIG if k not in GATHER_FIRST]
        placed, gathered = place_shards([w_rows[k] for k in later], [BIG_SHARD[k][0] for k in later], first,
                                        "place_shards_gather_ffn1")
        super().__init__(zip(GATHER_FIRST, gathered))
        self.bufs = dict(zip(later, placed))
        self.place = jnp.stack([chip, core])
        self.sums, self.halves = {}, {}

    def _exchange(self, names):
        return exchange_comm([self.grads[k] for k in names])

    def _scatter(self, names):
        return scatter_comm([self.sums[k][1] for k in names])

    def _pair_sums(self, names, recv):
        for k, r in zip(names, recv):
            self.sums[k] = add_own_half(self.place, self.grads[k], r, "pair_sum_" + k)

    def _chip_sums(self, names, recv):
        for k, r in zip(names, recv):
            self.halves[k] = add_chip_sums(self.place, self.sums[k][0], r, "chip_sum_" + k)

    def before(self, kernel_name):
        comms, takers = [], []
        if kernel_name in GATHER_ON:
            names = GATHER_ON[kernel_name]
            comms.append(gather_comm([self.bufs[k] for k in names]))
            takers.append(lambda res, names=names: self.wts.update(zip(names, res)))
        for names, exchange_on, scatter_on in REDUCE:
            if kernel_name == exchange_on:
                comms.append(self._exchange(names))
                takers.append(lambda res, names=names: self._pair_sums(names, res))
            if kernel_name == scatter_on:
                if exchange_on is None:
                    self._pair_sums(names, run_comm(self._exchange(names), "exchange_" + names[0]))
                comms.append(self._scatter(names))
                takers.append(lambda res, names=names: self._chip_sums(names, res))
        if not comms:
            return None
        merged, counts = merge_comms(comms)
        self.pending = (takers, counts)
        return merged

    def after(self, kernel_name, results):
        if not results:
            return
        takers, counts = self.pending
        start = 0
        for take, count in zip(takers, counts):
            take(results[start:start + count])
            start += count

    def finish(self):
        tail = [names for names, _, scatter_on in REDUCE if scatter_on is None]
        alone = [k for names, exchange_on, scatter_on in REDUCE if scatter_on is None and exchange_on is None
                 for k in names]
        self._pair_sums(alone, run_comm(self._exchange(alone), "exchange_tail"))
        tail = [k for names in tail for k in names]
        self._chip_sums(tail, run_comm(self._scatter(tail), "scatter_tail"))
        return dict(zip(BIG, run_comm(join_comm([self.halves[k] for k in BIG]), "join_halves")))


SMALL = ("ffn1_norm", "mix_norm", "s5_lam_re", "s5_lam_im", "s5_log_dt", "s5_b_re", "s5_b_im", "s5_c_re",
         "s5_c_im", "s5_d", "hg_lower_bound", "hg_out_norm", "ffn2_norm", "ple_norm", "final_norm")
WEIGHTS = ("ffn1_norm", "ffn1_w_gate", "ffn1_w_up", "ffn1_w_down", "mix_norm", "w_in", "s5_lam_re", "s5_lam_im",
           "s5_log_dt", "s5_b_re", "s5_b_im", "s5_c_re", "s5_c_im", "s5_d", "s5_glu_val", "s5_glu_gate",
           "hg_lower_bound", "hg_out_norm", "hg_w_out", "w_merge_out", "ffn2_norm", "ffn2_w_gate", "ffn2_w_up",
           "ffn2_w_down", "ple_norm", "ple_w_gate", "ple_w_proj", "final_norm")


def _as_rows(name, w):
    return jnp.swapaxes(w[0], 0, 1) if name in FFN_T else w[0]


def _from_rows(name, w):
    return (jnp.swapaxes(w, 0, 1) if name in FFN_T else w)[None]


def _gather_buffer(name, w_rows, chip):
    r, c = BIG_SHARD[name]
    shard = jnp.pad(w_rows.astype(BF16), ((0, r - w_rows.shape[0]), (0, 0)))
    return lax.dynamic_update_slice(jnp.zeros((N_SHARD, r, c), BF16), shard[None], (chip, 0, 0))


def _pack(parts):
    flat = jnp.concatenate([jnp.zeros((128,), F32)] + [a.reshape(-1) for a in parts])
    rows = -(-flat.shape[0] // 2048) * 16
    return jnp.pad(flat, (0, rows * 128 - flat.shape[0])).reshape(rows, 128)


def _unpack(vec, likes):
    flat = vec.reshape(-1)
    out, off = [], 128
    for a in likes:
        out.append(flat[off:off + a.size].reshape(a.shape))
        off += a.size
    return out


def _small_view(name, w):
    if name.startswith("s5_") and name != "s5_d":
        return w[0]
    if name == "final_norm":
        return w.reshape(1, D_MODEL)
    return w


def kernel(x, p, ffn1_norm, ffn1_w_gate, ffn1_w_up, ffn1_w_down, mix_norm, w_in, s5_lam_re, s5_lam_im, s5_log_dt, s5_b_re, s5_b_im, s5_c_re, s5_c_im, s5_d, s5_glu_val, s5_glu_gate, hg_lower_bound, hg_out_norm, hg_w_out, w_merge_out, ffn2_norm, ffn2_w_gate, ffn2_w_up, ffn2_w_down, ple_norm, ple_w_gate, ple_w_proj, final_norm, loss_target, m_ffn1_norm, m_ffn1_w_gate, m_ffn1_w_up, m_ffn1_w_down, m_mix_norm, m_w_in, m_s5_lam_re, m_s5_lam_im, m_s5_log_dt, m_s5_b_re, m_s5_b_im, m_s5_c_re, m_s5_c_im, m_s5_d, m_s5_glu_val, m_s5_glu_gate, m_hg_lower_bound, m_hg_out_norm, m_hg_w_out, m_w_merge_out, m_ffn2_norm, m_ffn2_w_gate, m_ffn2_w_up, m_ffn2_w_down, m_ple_norm, m_ple_w_gate, m_ple_w_proj, m_final_norm, v_ffn1_norm, v_ffn1_w_gate, v_ffn1_w_up, v_ffn1_w_down, v_mix_norm, v_w_in, v_s5_lam_re, v_s5_lam_im, v_s5_log_dt, v_s5_b_re, v_s5_b_im, v_s5_c_re, v_s5_c_im, v_s5_d, v_s5_glu_val, v_s5_glu_gate, v_hg_lower_bound, v_hg_out_norm, v_hg_w_out, v_w_merge_out, v_ffn2_norm, v_ffn2_w_gate, v_ffn2_w_up, v_ffn2_w_down, v_ple_norm, v_ple_w_gate, v_ple_w_proj, v_final_norm):
    given = dict(locals())
    wv = {k: given[k] for k in WEIGHTS}
    mv = {k: given["m_" + k] for k in WEIGHTS}
    vv = {k: given["v_" + k] for k in WEIGHTS}

    core = lax.axis_index("c").astype(jnp.int32)
    chip = (2 * lax.axis_index("x") + lax.axis_index("y")).astype(jnp.int32)
    w_rows = {k: _as_rows(k, wv[k]) for k in BIG}
    sched = DistSchedule(w_rows, chip, core)
    sm = {k: _small_view(k, wv[k]) for k in SMALL}

    loss_blk, dx, gsm = local_step(x[0], p[0, 0], loss_target[0], sched, sm)
    full = sched.finish()

    small_likes = [wv[k] for k in SMALL]
    packed = _pack([gsm[k] for k in SMALL])
    packed = packed.at[0, 0].set(loss_blk[0, 0])
    total = allreduce_small(packed)
    loss = total[0, 0]
    gsmall = dict(zip(SMALL, _unpack(total, small_likes)))

    grads, deltas, new_m, new_v = {}, {}, {}, {}
    for k in BIG:
        padded = full[k].shape != w_rows[k].shape
        res = adamw(w_rows[k], _as_rows(k, mv[k]), _as_rows(k, vv[k]), full[k], "adamw_" + k, copy_g=padded)
        grads[k] = _from_rows(k, res[3] if padded else full[k])
        deltas[k], new_m[k], new_v[k] = (_from_rows(k, a) for a in res[:3])
    sw = _pack([wv[k] for k in SMALL])
    smm = _pack([mv[k] for k in SMALL])
    svv = _pack([vv[k] for k in SMALL])
    sd, smn, svn = adamw(sw, smm, svv, total, "adamw_small")
    for k, d, mn, vn in zip(SMALL, _unpack(sd, small_likes), _unpack(smn, small_likes), _unpack(svn, small_likes)):
        grads[k], deltas[k], new_m[k], new_v[k] = gsmall[k], d, mn, vn

    return (loss, dx[None], *[grads[k] for k in WEIGHTS], *[deltas[k] for k in WEIGHTS],
            *[new_m[k] for k in WEIGHTS], *[new_v[k] for k in WEIGHTS])
```

```python
import math

import jax
import jax.numpy as jnp
from jax import lax
from jax.experimental import pallas as pl
from jax.experimental.pallas import tpu as pltpu

F32 = jnp.float32
BF16 = jnp.bfloat16

D_MODEL = 1024
D_FF = 2816
N_SHARD = 4
FF_SHARD = D_FF // N_SHARD
FF_PAD = 768
NORM_EPS = 1e-6
PLE_DIM = 256

S5_WIDTH = 512
S5_GROUPS = 32
S5_GROUP = 16
S5_STATE = 64
S5_N = S5_GROUPS * S5_STATE
S5_KT = 2

HG_HEADS = 8
HG_E = 128
HG_WIDTH = 1024
CHUNK = 64
HG_SUB = 4
IN_COLS = S5_WIDTH + 4 * HG_WIDTH + 2 * D_MODEL
IN_SPLITS = (0, 512, 1536, 2560, 3584, 4608, 5632, 6656)

ADAM_LR = 0.001
ADAM_B1 = 0.9
ADAM_B2 = 0.999
ADAM_EPS = 1e-08
ADAM_WD = 0.01
ADAM_STEP = 10

VMEM_LIMIT = 60 * 1024 * 1024
HIGHEST = lax.Precision.HIGHEST


def _cparams(sem=None, **kw):
    return pltpu.CompilerParams(dimension_semantics=sem, vmem_limit_bytes=VMEM_LIMIT, **kw)


def _const_spec(shape):
    nd = len(shape)
    return pl.BlockSpec(shape, lambda *_: (0,) * nd, pipeline_mode=pl.Buffered(1))


def _dot(a, b):
    return jnp.dot(a, b, preferred_element_type=F32)


def _dot_nt(a, b):
    return lax.dot_general(a, b, (((1,), (1,)), ((), ())), preferred_element_type=F32)


def _dot_tn(a, b):
    return lax.dot_general(a, b, (((0,), (0,)), ((), ())), preferred_element_type=F32)


def _sigmoid(x):
    return 1.0 / (1.0 + jnp.exp(-x))


def _rms_fwd(x, g):
    r = lax.rsqrt(jnp.mean(x * x, axis=-1, keepdims=True) + NORM_EPS)
    return x * r * g, r


def _rms_bwd(x, r, g, dy):
    xh = x * r
    dyg = dy * g
    m = jnp.mean(dyg * xh, axis=-1, keepdims=True)
    return r * (dyg - xh * m), jnp.sum(dy * xh, axis=0, keepdims=True)


def _accum(ref, val, first):
    @pl.when(first)
    def _():
        ref[...] = val

    @pl.when(jnp.logical_not(first))
    def _():
        ref[...] += val


def ffn_fwd(h, gain, wg, wu, wd, name, comm=None, tm=256):
    t = h.shape[0]

    def body(h_ref, g_ref, wg_ref, wu_ref, wd_ref, o_ref, a_ref, b_ref):
        hv = h_ref[...]
        n, _ = _rms_fwd(hv, g_ref[...])
        nb = n.astype(BF16)
        acc = jnp.zeros((tm, D_MODEL), F32)
        for s in range(N_SHARD):
            a = _dot_nt(nb, wg_ref[s])
            b = _dot_nt(nb, wu_ref[s])
            a_ref[s] = a.astype(BF16)
            b_ref[s] = b.astype(BF16)
            sv = (a * _sigmoid(a) * b).astype(BF16)
            acc = acc + _dot(sv, wd_ref[s])
        o_ref[...] = hv + 0.5 * acc

    return _carry(
        body, comm, name=name, steps=t // tm,
        out_shape=(jax.ShapeDtypeStruct((t, D_MODEL), F32),
                   jax.ShapeDtypeStruct((N_SHARD, t, FF_PAD), BF16),
                   jax.ShapeDtypeStruct((N_SHARD, t, FF_PAD), BF16)),
        in_specs=[pl.BlockSpec((tm, D_MODEL), lambda i: (i, 0)),
                  _const_spec((1, D_MODEL)),
                  _const_spec((N_SHARD, FF_PAD, D_MODEL)),
                  _const_spec((N_SHARD, FF_PAD, D_MODEL)),
                  _const_spec((N_SHARD, FF_PAD, D_MODEL))],
        out_specs=(pl.BlockSpec((tm, D_MODEL), lambda i: (i, 0)),
                   pl.BlockSpec((N_SHARD, tm, FF_PAD), lambda i: (0, i, 0)),
                   pl.BlockSpec((N_SHARD, tm, FF_PAD), lambda i: (0, i, 0))),
        args=(h, gain, wg, wu, wd),
    )


def ffn_bwd(dho, h, a, b, gain, wg, wu, wd, name, comm=None, tm=256):
    t = h.shape[0]

    def body(dho_ref, h_ref, a_ref, b_ref, g_ref, wg_ref, wu_ref, wd_ref,
             dh_ref, dg_ref, nb_ref, dhb_ref, da_ref, db_ref, s_ref):
        hv = h_ref[...]
        g = g_ref[...]
        n, r = _rms_fwd(hv, g)
        nb_ref[...] = n.astype(BF16)
        dhalf = (0.5 * dho_ref[...]).astype(BF16)
        dhb_ref[...] = dhalf
        dn = jnp.zeros((tm, D_MODEL), F32)
        for s in range(N_SHARD):
            av = a_ref[s].astype(F32)
            bv = b_ref[s].astype(F32)
            sg = _sigmoid(av)
            sil = av * sg
            s_ref[s] = (sil * bv).astype(BF16)
            ds = _dot_nt(dhalf, wd_ref[s])
            da = (ds * bv * (sg * (1.0 + av * (1.0 - sg)))).astype(BF16)
            db = (ds * sil).astype(BF16)
            da_ref[s] = da
            db_ref[s] = db
            dn = dn + _dot(da, wg_ref[s]) + _dot(db, wu_ref[s])
        dx, dg = _rms_bwd(hv, r, g, dn)
        dh_ref[...] = dho_ref[...] + dx
        _accum(dg_ref, dg, pl.program_id(0) == 0)

    tok = pl.BlockSpec((tm, D_MODEL), lambda i: (i, 0))
    hid = pl.BlockSpec((N_SHARD, tm, FF_PAD), lambda i: (0, i, 0))
    return _carry(
        body, comm, name=name, steps=t // tm,
        out_shape=(jax.ShapeDtypeStruct((t, D_MODEL), F32),
                   jax.ShapeDtypeStruct((1, D_MODEL), F32),
                   jax.ShapeDtypeStruct((t, D_MODEL), BF16),
                   jax.ShapeDtypeStruct((t, D_MODEL), BF16),
                   jax.ShapeDtypeStruct((N_SHARD, t, FF_PAD), BF16),
                   jax.ShapeDtypeStruct((N_SHARD, t, FF_PAD), BF16),
                   jax.ShapeDtypeStruct((N_SHARD, t, FF_PAD), BF16)),
        in_specs=[tok, tok, hid, hid, _const_spec((1, D_MODEL)),
                  _const_spec((N_SHARD, FF_PAD, D_MODEL)),
                  _const_spec((N_SHARD, FF_PAD, D_MODEL)),
                  _const_spec((N_SHARD, FF_PAD, D_MODEL))],
        out_specs=(tok, pl.BlockSpec((1, D_MODEL), lambda i: (0, 0)), tok, tok, hid, hid, hid),
        args=(dho, h, a, b, gain, wg, wu, wd),
    )


TN_VMEM_BUDGET = 44 * 1024 * 1024


def tn_matmul(x, y, name, shard, comm=None):
    x3, y3 = x.ndim == 3, y.ndim == 3
    t = x.shape[-2]
    m = x.shape[-1] // (N_SHARD if (shard == "rows" and not x3) else 1)
    n = y.shape[-1] // (N_SHARD if (shard == "cols" and not y3) else 1)
    per_token = 2 * (m * x.dtype.itemsize + n * y.dtype.itemsize)
    tk = t
    while tk > 512 and tk * per_token + 2 * m * n * 4 > TN_VMEM_BUDGET:
        tk //= 2
    nk = t // tk

    out_shape = jax.ShapeDtypeStruct((N_SHARD, m, n), F32)
    if nk == 1:
        def whole(x_ref, y_ref, o_ref):
            o_ref[...] = _dot_tn(x_ref[...].astype(BF16), y_ref[...].astype(BF16))

        x_one = (pl.BlockSpec((None, t, m), lambda s: (s, 0, 0)) if x3 else
                 pl.BlockSpec((t, m), (lambda s: (0, s)) if shard == "rows" else (lambda s: (0, 0))))
        y_one = (pl.BlockSpec((None, t, n), lambda s: (s, 0, 0)) if y3 else
                 pl.BlockSpec((t, n), (lambda s: (0, s)) if shard == "cols" else (lambda s: (0, 0))))
        return _carry(whole, comm, name=name, steps=N_SHARD, out_shape=(out_shape,), in_specs=[x_one, y_one],
                      out_specs=(pl.BlockSpec((None, m, n), lambda s: (s, 0, 0)),), args=(x, y))
    assert comm is None

    def body(x_ref, y_ref, o_ref):
        _accum(o_ref, _dot_tn(x_ref[...].astype(BF16), y_ref[...].astype(BF16)), pl.program_id(1) == 0)

    if x3:
        x_spec = pl.BlockSpec((None, tk, m), lambda s, k: (s, k, 0))
    elif shard == "rows":
        x_spec = pl.BlockSpec((tk, m), lambda s, k: (k, s))
    else:
        x_spec = pl.BlockSpec((tk, m), lambda s, k: (k, 0))
    if y3:
        y_spec = pl.BlockSpec((None, tk, n), lambda s, k: (s, k, 0))
    elif shard == "cols":
        y_spec = pl.BlockSpec((tk, n), lambda s, k: (k, s))
    else:
        y_spec = pl.BlockSpec((tk, n), lambda s, k: (k, 0))
    res = pl.pallas_call(
        body, name=name, grid=(N_SHARD, nk),
        out_shape=out_shape,
        in_specs=[x_spec, y_spec],
        out_specs=pl.BlockSpec((None, m, n), lambda s, k: (s, 0, 0)),
        compiler_params=_cparams(("arbitrary", "arbitrary")),
    )(x, y)
    return (res,), ()


def inproj_fwd(h, gain, w_in, tm=256):
    t = h.shape[0]
    widths = [IN_SPLITS[j + 1] - IN_SPLITS[j] for j in range(7)]
    sh_cols = IN_COLS // N_SHARD

    def body(h_ref, g_ref, w_ref, *outs):
        n, _ = _rms_fwd(h_ref[...], g_ref[...])
        nb = n.astype(BF16)
        proj = jnp.concatenate([_dot(nb, w_ref[s]) for s in range(N_SHARD)], axis=1)
        for j, o_ref in enumerate(outs):
            o_ref[...] = proj[:, IN_SPLITS[j]:IN_SPLITS[j + 1]]

    return pl.pallas_call(
        body, name="inproj_fwd", grid=(t // tm,),
        out_shape=tuple(jax.ShapeDtypeStruct((t, w), F32) for w in widths),
        in_specs=[pl.BlockSpec((tm, D_MODEL), lambda i: (i, 0)),
                  _const_spec((1, D_MODEL)),
                  _const_spec((N_SHARD, D_MODEL, sh_cols))],
        out_specs=tuple(pl.BlockSpec((tm, w), lambda i: (i, 0)) for w in widths),
        compiler_params=_cparams(("arbitrary",)),
    )(h, gain, w_in)


def inproj_bwd(dres, h, gain, w_in, dparts, comm=None, tm=256):
    t = h.shape[0]
    widths = [IN_SPLITS[j + 1] - IN_SPLITS[j] for j in range(7)]
    sh_cols = IN_COLS // N_SHARD

    def body(dres_ref, h_ref, g_ref, w_ref, d0, d1, d2, d3, d4, d5, d6, dh_ref, dg_ref, nb_ref, dp_ref):
        hv = h_ref[...]
        g = g_ref[...]
        n, r = _rms_fwd(hv, g)
        nb_ref[...] = n.astype(BF16)
        dproj = jnp.concatenate([d[...] for d in (d0, d1, d2, d3, d4, d5, d6)], axis=1).astype(BF16)
        dp_ref[...] = dproj
        dn = jnp.zeros((tm, D_MODEL), F32)
        for s in range(N_SHARD):
            dn = dn + _dot_nt(dproj[:, s * sh_cols:(s + 1) * sh_cols], w_ref[s])
        dx, dg = _rms_bwd(hv, r, g, dn)
        dh_ref[...] = dres_ref[...] + dx
        _accum(dg_ref, dg, pl.program_id(0) == 0)

    tok = pl.BlockSpec((tm, D_MODEL), lambda i: (i, 0))
    return _carry(
        body, comm, name="inproj_bwd", steps=t // tm,
        out_shape=(jax.ShapeDtypeStruct((t, D_MODEL), F32),
                   jax.ShapeDtypeStruct((1, D_MODEL), F32),
                   jax.ShapeDtypeStruct((t, D_MODEL), BF16),
                   jax.ShapeDtypeStruct((t, IN_COLS), BF16)),
        in_specs=[tok, tok, _const_spec((1, D_MODEL)), _const_spec((N_SHARD, D_MODEL, sh_cols))]
                 + [pl.BlockSpec((tm, w), lambda i: (i, 0)) for w in widths],
        out_specs=(tok, pl.BlockSpec((1, D_MODEL), lambda i: (0, 0)), tok,
                   pl.BlockSpec((tm, IN_COLS), lambda i: (i, 0))),
        args=(dres, h, gain, w_in, *dparts),
    )


def s5_prep(lam_re, lam_im, log_dt, b_re, b_im, c_re, c_im):
    dt = jnp.exp(log_dt)[:, None]
    mag = jnp.exp(lam_re * dt)
    lbr = mag * jnp.cos(lam_im * dt)
    lbi = mag * jnp.sin(lam_im * dt)
    den = lam_re * lam_re + lam_im * lam_im
    nr, ni = lbr - 1.0, lbi
    kr = (nr * lam_re + ni * lam_im) / den
    ki = (ni * lam_re - nr * lam_im) / den
    bbr = kr[..., None] * b_re - ki[..., None] * b_im
    bbi = kr[..., None] * b_im + ki[..., None] * b_re
    eye = jnp.eye(16, dtype=F32)

    def bm(bp):
        return jnp.einsum('kgph,gG->kghGp', bp.reshape(S5_KT, 16, S5_STATE, S5_GROUP), eye).reshape(S5_KT, 256, 1024)

    def cm(cp):
        return jnp.einsum('kghp,gG->kgpGh', cp.reshape(S5_KT, 16, S5_GROUP, S5_STATE), eye).reshape(S5_KT, 1024, 256)

    lam_bar = jnp.stack([lbr.reshape(S5_N), lbi.reshape(S5_N)])
    bmat = jnp.stack([bm(bbr), bm(bbi)])
    cmat = jnp.stack([cm(c_re), -cm(c_im)])
    return lam_bar, bmat, cmat


def _lam_powers(lam_bar):
    lr, li = lam_bar[0], lam_bar[1]
    pr, pi = [lr], [li]
    for _ in range(7):
        pr, pi = pr + [pr[-1] * lr - pi[-1] * li], pi + [pr[-1] * li + pi[-1] * lr]
    return jnp.stack(pr), jnp.stack(pi)


SCAN_SHIFTS = ((1, 0), (2, 1), (4, 3))


def _scan_tables(pw_r, pw_i, reverse):
    rows = jnp.arange(8)[:, None]
    planes_r, planes_i = [], []
    for sh, idx in SCAN_SHIFTS:
        keep = (rows < 8 - sh) if reverse else (rows >= sh)
        planes_r.append(jnp.where(keep, pw_r[idx:idx + 1], 0.0))
        planes_i.append(jnp.where(keep, pw_i[idx:idx + 1], 0.0))
    carry = [pw_r[::-1], pw_i[::-1]] if reverse else [pw_r, pw_i]
    return jnp.stack(planes_r + planes_i + carry)


def s5_fwd(u, tab, bmat, cmat, dvec, comm=None, tm=256):
    t = u.shape[0]
    nch = tm // 8

    def body(u_ref, tab_ref, b_ref, c_ref, d_ref, y_ref, xp_ref, x_scr, carry):
        @pl.when(pl.program_id(0) == 0)
        def _():
            carry[...] = jnp.zeros_like(carry)

        uv = u_ref[...]
        ub = uv.astype(BF16)
        for part in range(2):
            for kt in range(S5_KT):
                x_scr[:, pl.ds(part * S5_N + kt * 1024, 1024)] = _dot(ub[:, kt * 256:(kt + 1) * 256], b_ref[part, kt])
        row = lax.broadcasted_iota(jnp.int32, (8, S5_N), 0)

        def chunk(i, c):
            cr, ci = c
            r0 = pl.multiple_of(i * 8, 8)
            xr = x_scr[pl.ds(r0, 8), pl.ds(0, S5_N)]
            xi = x_scr[pl.ds(r0, 8), pl.ds(S5_N, S5_N)]
            for lvl, (sh, _) in enumerate(SCAN_SHIFTS):
                sr = pltpu.roll(xr, sh, 0)
                si = pltpu.roll(xi, sh, 0)
                lr = tab_ref[lvl]
                li = tab_ref[3 + lvl]
                xr, xi = xr + lr * sr - li * si, xi + lr * si + li * sr
            pwr = tab_ref[6]
            pwi = tab_ref[7]
            xr, xi = xr + pwr * cr - pwi * ci, xi + pwr * ci + pwi * cr
            x_scr[pl.ds(r0, 8), pl.ds(0, S5_N)] = xr
            x_scr[pl.ds(r0, 8), pl.ds(S5_N, S5_N)] = xi
            xp_ref[pl.ds(r0, 8), pl.ds(0, S5_N)] = jnp.where(row == 0, cr, pltpu.roll(xr, 1, 0))
            xp_ref[pl.ds(r0, 8), pl.ds(S5_N, S5_N)] = jnp.where(row == 0, ci, pltpu.roll(xi, 1, 0))
            return xr[7:8, :], xi[7:8, :]

        cr, ci = lax.fori_loop(0, nch, chunk, (carry[0:1, :], carry[1:2, :]))
        carry[0:1, :] = cr
        carry[1:2, :] = ci
        for kt in range(S5_KT):
            acc = jnp.zeros((tm, 256), F32)
            for part in range(2):
                acc = acc + _dot(x_scr[:, pl.ds(part * S5_N + kt * 1024, 1024)].astype(BF16), c_ref[part, kt])
            y_ref[:, pl.ds(kt * 256, 256)] = acc + d_ref[:, pl.ds(kt * 256, 256)] * uv[:, kt * 256:(kt + 1) * 256]

    return _carry(
        body, comm, name="s5_fwd", steps=t // tm,
        out_shape=(jax.ShapeDtypeStruct((t, S5_WIDTH), F32),
                   jax.ShapeDtypeStruct((t, 2 * S5_N), F32)),
        in_specs=[pl.BlockSpec((tm, S5_WIDTH), lambda i: (i, 0)),
                  _const_spec((8, 8, S5_N)),
                  _const_spec((2, S5_KT, 256, 1024)), _const_spec((2, S5_KT, 1024, 256)),
                  _const_spec((1, S5_WIDTH))],
        out_specs=(pl.BlockSpec((tm, S5_WIDTH), lambda i: (i, 0)),
                   pl.BlockSpec((tm, 2 * S5_N), lambda i: (i, 0))),
        scratch_shapes=[pltpu.VMEM((tm, 2 * S5_N), F32), pltpu.VMEM((8, S5_N), F32)],
        args=(u, tab, bmat, cmat, dvec),
    )


def s5_bwd(dy, u, xp, tab, bmat, bmat_t, cmat_t, dvec, comm=None, tm=256):
    t = u.shape[0]
    nt = t // tm
    nch = tm // 8

    def body(dy_ref, u_ref, xp_ref, tab_ref, b_ref, bt_ref, ct_ref, d_ref,
             du_ref, db_ref, dc_ref, dl_ref, dd_ref, g_scr, x_scr, carry):
        first = pl.program_id(0) == 0

        @pl.when(first)
        def _():
            carry[...] = jnp.zeros_like(carry)
            dl_ref[...] = jnp.zeros_like(dl_ref)

        dyv = dy_ref[...]
        uv = u_ref[...]
        dyb = dyv.astype(BF16)
        ub = uv.astype(BF16)
        lr1 = tab_ref[6, 7:8, :]
        li1 = tab_ref[7, 7:8, :]
        for kt in range(S5_KT):
            cols = pl.ds(kt * 1024, 1024)
            colsi = pl.ds(S5_N + kt * 1024, 1024)
            g_scr[:, cols] = _dot(dyb[:, kt * 256:(kt + 1) * 256], ct_ref[0, kt])
            g_scr[:, colsi] = _dot(dyb[:, kt * 256:(kt + 1) * 256], ct_ref[1, kt])
            bur = _dot(ub[:, kt * 256:(kt + 1) * 256], b_ref[0, kt])
            bui = _dot(ub[:, kt * 256:(kt + 1) * 256], b_ref[1, kt])
            xpr = xp_ref[:, cols]
            xpi = xp_ref[:, colsi]
            lrk = lr1[:, kt * 1024:(kt + 1) * 1024]
            lik = li1[:, kt * 1024:(kt + 1) * 1024]
            x_scr[:, cols] = lrk * xpr - lik * xpi + bur
            x_scr[:, colsi] = lrk * xpi + lik * xpr + bui

        def chunk(j, c):
            cr, ci = c
            r0 = pl.multiple_of((nch - 1 - j) * 8, 8)
            gr = g_scr[pl.ds(r0, 8), pl.ds(0, S5_N)]
            gi = g_scr[pl.ds(r0, 8), pl.ds(S5_N, S5_N)]
            for lvl, (sh, _) in enumerate(SCAN_SHIFTS):
                sr = pltpu.roll(gr, 8 - sh, 0)
                si = pltpu.roll(gi, 8 - sh, 0)
                lr = tab_ref[lvl]
                li = tab_ref[3 + lvl]
                gr, gi = gr + lr * sr + li * si, gi + lr * si - li * sr
            pvr = tab_ref[6]
            pvi = tab_ref[7]
            gr, gi = gr + pvr * cr + pvi * ci, gi + pvr * ci - pvi * cr
            g_scr[pl.ds(r0, 8), pl.ds(0, S5_N)] = gr
            g_scr[pl.ds(r0, 8), pl.ds(S5_N, S5_N)] = gi
            xpr = xp_ref[pl.ds(r0, 8), pl.ds(0, S5_N)]
            xpi = xp_ref[pl.ds(r0, 8), pl.ds(S5_N, S5_N)]
            dl_ref[0] += gr * xpr + gi * xpi
            dl_ref[1] += gi * xpr - gr * xpi
            return gr[0:1, :], gi[0:1, :]

        cr, ci = lax.fori_loop(0, nch, chunk, (carry[0:1, :], carry[1:2, :]))
        carry[0:1, :] = cr
        carry[1:2, :] = ci

        for kt in range(S5_KT):
            du = jnp.zeros((tm, 256), F32)
            ukt = ub[:, kt * 256:(kt + 1) * 256]
            dykt = dyb[:, kt * 256:(kt + 1) * 256]
            for part in range(2):
                gb = g_scr[:, pl.ds(part * S5_N + kt * 1024, 1024)].astype(BF16)
                xb = x_scr[:, pl.ds(part * S5_N + kt * 1024, 1024)].astype(BF16)
                du = du + _dot(gb, bt_ref[part, kt])
                dbv = _dot_tn(ukt, gb)
                dcv = _dot_tn(xb, dykt)

                @pl.when(first)
                def _():
                    db_ref[part, kt] = dbv
                    dc_ref[part, kt] = dcv

                @pl.when(jnp.logical_not(first))
                def _():
                    db_ref[part, kt] += dbv
                    dc_ref[part, kt] += dcv
            du_ref[:, pl.ds(kt * 256, 256)] = du + d_ref[:, pl.ds(kt * 256, 256)] * dyv[:, kt * 256:(kt + 1) * 256]
        _accum(dd_ref, jnp.sum(dyv * uv, axis=0, keepdims=True), first)

    rev = lambda i: (nt - 1 - i, 0)
    return _carry(
        body, comm, name="s5_bwd", steps=nt,
        out_shape=(jax.ShapeDtypeStruct((t, S5_WIDTH), F32),
                   jax.ShapeDtypeStruct((2, S5_KT, 256, 1024), F32),
                   jax.ShapeDtypeStruct((2, S5_KT, 1024, 256), F32),
                   jax.ShapeDtypeStruct((2, 8, S5_N), F32),
                   jax.ShapeDtypeStruct((1, S5_WIDTH), F32)),
        in_specs=[pl.BlockSpec((tm, S5_WIDTH), rev), pl.BlockSpec((tm, S5_WIDTH), rev),
                  pl.BlockSpec((tm, 2 * S5_N), rev),
                  _const_spec((8, 8, S5_N)),
                  _const_spec((2, S5_KT, 256, 1024)), _const_spec((2, S5_KT, 1024, 256)),
                  _const_spec((2, S5_KT, 256, 1024)), _const_spec((1, S5_WIDTH))],
        out_specs=(pl.BlockSpec((tm, S5_WIDTH), rev),
                   pl.BlockSpec((2, S5_KT, 256, 1024), lambda i: (0, 0, 0, 0)),
                   pl.BlockSpec((2, S5_KT, 1024, 256), lambda i: (0, 0, 0, 0)),
                   pl.BlockSpec((2, 8, S5_N), lambda i: (0, 0, 0)),
                   pl.BlockSpec((1, S5_WIDTH), lambda i: (0, 0))),
        scratch_shapes=[pltpu.VMEM((tm, 2 * S5_N), F32), pltpu.VMEM((tm, 2 * S5_N), F32),
                        pltpu.VMEM((8, S5_N), F32)],
        args=(dy, u, xp, tab, bmat, bmat_t, cmat_t, dvec),
    )


def _hg_gates(z, lb):
    sg = _sigmoid(z)
    sgn = _sigmoid(-z)
    fg = lb + (1.0 - lb) * sg
    return sg, sgn, fg, jnp.log(fg), (1.0 - lb) * sgn


def _hg_decays(g, tril):
    gc = jnp.dot(tril, g, precision=HIGHEST, preferred_element_type=F32)
    mid = gc[CHUNK // 2 - 1:CHUNK // 2, :]
    last = gc[CHUNK - 1:CHUNK, :]
    return jnp.exp(gc), jnp.exp(gc - mid), jnp.exp(mid - gc), jnp.exp(last - gc), jnp.exp(last)


def _split_bf16(x):
    hi = x.astype(BF16)
    return hi, (x - hi.astype(F32)).astype(BF16)


def _hg_scores(qt, qlo, kt, klo, sl, causal):
    a = _dot_nt(qt[:, sl], kt[:, sl]) + _dot_nt(qt[:, sl], klo[:, sl]) + _dot_nt(qlo[:, sl], kt[:, sl])
    return jnp.where(causal, a, 0.0).astype(BF16)


def hgrn_fwd(q, f, v, lb):
    t = q.shape[0]
    nc = t // CHUNK
    scale = HG_E ** -0.5

    def body(q_ref, f_ref, v_ref, lb_ref, o_ref, st_ref, state):
        @pl.when(pl.program_id(0) == 0)
        def _():
            state[...] = jnp.zeros_like(state)

        ri = lax.broadcasted_iota(jnp.int32, (CHUNK, CHUNK), 0)
        ci = lax.broadcasted_iota(jnp.int32, (CHUNK, CHUNK), 1)
        causal = ri >= ci
        tril = causal.astype(F32)
        for sub in range(HG_SUB):
            rows = pl.ds(sub * CHUNK, CHUNK)
            _, _, _, g, k = _hg_gates(f_ref[rows, :], lb_ref[...])
            eg, eq, ek, ed, el = _hg_decays(g, tril)
            qs = q_ref[rows, :] * scale
            qg = (qs * eg).astype(BF16)
            qt, qlo = _split_bf16(qs * eq)
            kt, klo = _split_bf16(k * ek)
            kd = (k * ed).astype(BF16)
            vb = v_ref[rows, :].astype(BF16)
            for h in range(HG_HEADS):
                sl = slice(h * HG_E, (h + 1) * HG_E)
                st = state[h]
                a = _hg_scores(qt, qlo, kt, klo, sl, causal)
                o_ref[rows, sl] = _dot(a, vb[:, sl]) + _dot_nt(qg[:, sl], st.astype(BF16))
                st_new = st * el[:, sl] + _dot_tn(vb[:, sl], kd[:, sl])
                state[h] = st_new
                st_ref[sub, h] = st_new

    tok = pl.BlockSpec((HG_SUB * CHUNK, HG_WIDTH), lambda i: (i, 0))
    return pl.pallas_call(
        body, name="hgrn_fwd", grid=(nc // HG_SUB,),
        out_shape=(jax.ShapeDtypeStruct((t, HG_WIDTH), F32),
                   jax.ShapeDtypeStruct((nc, HG_HEADS, HG_E, HG_E), F32)),
        in_specs=[tok, tok, tok, _const_spec((1, HG_WIDTH))],
        out_specs=(tok, pl.BlockSpec((HG_SUB, HG_HEADS, HG_E, HG_E), lambda i: (i, 0, 0, 0))),
        scratch_shapes=[pltpu.VMEM((HG_HEADS, HG_E, HG_E), F32)],
        compiler_params=_cparams(("arbitrary",)),
    )(q, f, v, lb)


def hgrn_bwd(do, q, f, v, lb, states, comm=None):
    t = q.shape[0]
    nc = t // CHUNK
    scale = HG_E ** -0.5

    ns = nc // HG_SUB

    def body(do_ref, q_ref, f_ref, v_ref, lb_ref, scur_ref, sprev_ref, dq_ref, df_ref, dv_ref, dlb_ref, dstate):
        first = pl.program_id(0) == 0
        has_prev = jnp.where(pl.program_id(0) < ns - 1, 1.0, 0.0)

        @pl.when(first)
        def _():
            dstate[...] = jnp.zeros_like(dstate)

        ri = lax.broadcasted_iota(jnp.int32, (CHUNK, CHUNK), 0)
        ci = lax.broadcasted_iota(jnp.int32, (CHUNK, CHUNK), 1)
        causal = ri >= ci
        tril = causal.astype(F32)
        triu = (ri <= ci).astype(F32)
        rowc = lax.broadcasted_iota(jnp.int32, (CHUNK, HG_WIDTH), 0)
        lb = lb_ref[...]
        dlb = jnp.zeros((1, HG_WIDTH), F32)
        for sub in reversed(range(HG_SUB)):
            rows = pl.ds(sub * CHUNK, CHUNK)
            sg, sgn, fg, g, k = _hg_gates(f_ref[rows, :], lb)
            eg, eq, ek, ed, el = _hg_decays(g, tril)
            qs = q_ref[rows, :] * scale
            qg = (qs * eg).astype(BF16)
            qt, qlo = _split_bf16(qs * eq)
            kt, klo = _split_bf16(k * ek)
            kd = (k * ed).astype(BF16)
            vb = v_ref[rows, :].astype(BF16)
            dob = do_ref[rows, :].astype(BF16)
            dqs_l, dk_l, dgc_l, dgl_l = [], [], [], []
            for h in range(HG_HEADS):
                sl = slice(h * HG_E, (h + 1) * HG_E)
                s0 = scur_ref[sub - 1, h] if sub > 0 else sprev_ref[HG_SUB - 1, h] * has_prev
                ds1 = dstate[h]
                ds1b = ds1.astype(BF16)
                a = _hg_scores(qt, qlo, kt, klo, sl, causal)
                da = jnp.where(causal, _dot_nt(dob[:, sl], vb[:, sl]), 0.0).astype(BF16)
                dv_ref[rows, sl] = _dot_tn(a, dob[:, sl]) + _dot_nt(kd[:, sl], ds1b)
                dkd = _dot(vb[:, sl], ds1b)
                dqt = _dot(da, kt[:, sl])
                dkt = _dot_tn(da, qt[:, sl])
                dqg = _dot(dob[:, sl], s0.astype(BF16))
                dqs_l.append(dqt * eq[:, sl] + dqg * eg[:, sl])
                dk_l.append(dkt * ek[:, sl] + dkd * ed[:, sl])
                kd_dkd = kd[:, sl].astype(F32) * dkd
                dgc_l.append(qt[:, sl].astype(F32) * dqt - kt[:, sl].astype(F32) * dkt
                             + qg[:, sl].astype(F32) * dqg - kd_dkd)
                dgl_l.append(el[:, sl] * jnp.sum(ds1 * s0, axis=0, keepdims=True)
                             + jnp.sum(kd_dkd, axis=0, keepdims=True))
                dstate[h] = ds1 * el[:, sl] + _dot_tn(dob[:, sl], qg[:, sl])
            dqs = jnp.concatenate(dqs_l, axis=1)
            dk = jnp.concatenate(dk_l, axis=1)
            dgl = jnp.concatenate(dgl_l, axis=1)
            dq_ref[rows, :] = dqs * scale
            dgc = jnp.concatenate(dgc_l, axis=1) + jnp.where(rowc == CHUNK - 1, dgl, 0.0)
            dg = jnp.dot(triu, dgc, precision=HIGHEST, preferred_element_type=F32)
            w = dg / fg - dk
            df_ref[rows, :] = w * (1.0 - lb) * sg * sgn
            dlb = dlb + jnp.sum(w * sgn, axis=0, keepdims=True)
        _accum(dlb_ref, dlb, first)

    rev = lambda i: (ns - 1 - i, 0)
    tok = pl.BlockSpec((HG_SUB * CHUNK, HG_WIDTH), rev)
    st_blk = (HG_SUB, HG_HEADS, HG_E, HG_E)
    return _carry(
        body, comm, name="hgrn_bwd", steps=ns,
        out_shape=(jax.ShapeDtypeStruct((t, HG_WIDTH), F32),
                   jax.ShapeDtypeStruct((t, HG_WIDTH), F32),
                   jax.ShapeDtypeStruct((t, HG_WIDTH), F32),
                   jax.ShapeDtypeStruct((1, HG_WIDTH), F32)),
        in_specs=[tok, tok, tok, tok, _const_spec((1, HG_WIDTH)),
                  pl.BlockSpec(st_blk, lambda i: (ns - 1 - i, 0, 0, 0)),
                  pl.BlockSpec(st_blk, lambda i: (jnp.maximum(ns - 2 - i, 0), 0, 0, 0))],
        out_specs=(tok, tok, tok, pl.BlockSpec((1, HG_WIDTH), lambda i: (0, 0))),
        scratch_shapes=[pltpu.VMEM((HG_HEADS, HG_E, HG_E), F32)],
        args=(do, q, f, v, lb, states, states),
    )


GELU_C = math.sqrt(2.0 / math.pi)


def _gelu(x):
    th = jnp.tanh(GELU_C * (x + 0.044715 * x * x * x))
    return 0.5 * x * (1.0 + th), th


def _merge_core(ys5, o, og, ga, gb, wv_ref, wt_ref, ghg, who_ref):
    ys, th = _gelu(ys5)
    ysb = ys.astype(BF16)
    va = jnp.concatenate([_dot(ysb, wv_ref[s]) for s in range(N_SHARD)], axis=1)
    vt = jnp.concatenate([_dot(ysb, wt_ref[s]) for s in range(N_SHARD)], axis=1)
    svt = _sigmoid(vt)
    ya = va * svt
    rs, ons = [], []
    for h in range(HG_HEADS):
        oh = o[:, h * HG_E:(h + 1) * HG_E]
        r = lax.rsqrt(jnp.mean(oh * oh, axis=-1, keepdims=True) + NORM_EPS)
        rs.append(r)
        ons.append(oh * r)
    on = jnp.concatenate(ons, axis=1)
    sgo = _sigmoid(og)
    o2 = on * ghg * (og * sgo)
    o2b = o2.astype(BF16)
    yb = _dot(o2b, who_ref[...])
    sa = _sigmoid(ga)
    sb = _sigmoid(gb)
    mixed = sa * ya + sb * yb
    return dict(ys=ys, th=th, ysb=ysb, va=va, svt=svt, ya=ya, rs=rs, on=on, sgo=sgo, o2b=o2b, yb=yb,
                sa=sa, sb=sb, mixed=mixed)


def merge_fwd(h, ys5, o, og, ga, gb, wv, wt, ghg, who, wmo, tm=256):
    t = h.shape[0]

    def body(h_ref, ys5_ref, o_ref, og_ref, ga_ref, gb_ref, wv_ref, wt_ref, ghg_ref, who_ref, wmo_ref, out_ref):
        c = _merge_core(ys5_ref[...], o_ref[...], og_ref[...], ga_ref[...], gb_ref[...],
                        wv_ref, wt_ref, ghg_ref[...], who_ref)
        out_ref[...] = h_ref[...] + _dot(c["mixed"].astype(BF16), wmo_ref[...])

    tok = pl.BlockSpec((tm, D_MODEL), lambda i: (i, 0))
    return pl.pallas_call(
        body, name="merge_fwd", grid=(t // tm,),
        out_shape=jax.ShapeDtypeStruct((t, D_MODEL), F32),
        in_specs=[tok, pl.BlockSpec((tm, S5_WIDTH), lambda i: (i, 0)), tok, tok, tok, tok,
                  _const_spec((N_SHARD, S5_WIDTH, 256)), _const_spec((N_SHARD, S5_WIDTH, 256)),
                  _const_spec((1, HG_WIDTH)), _const_spec((HG_WIDTH, D_MODEL)), _const_spec((D_MODEL, D_MODEL))],
        out_specs=tok,
        compiler_params=_cparams(("arbitrary",)),
    )(h, ys5, o, og, ga, gb, wv, wt, ghg, who, wmo)


def merge_bwd(dh, ys5, o, og, ga, gb, wv, wt, ghg, who, wmo, comm=None, tm=256):
    t = dh.shape[0]

    def body(dh_ref, ys5_ref, o_ref, og_ref, ga_ref, gb_ref, wv_ref, wt_ref, ghg_ref, who_ref, wmo_ref,
             dys5_ref, do_ref, dog_ref, dga_ref, dgb_ref, dghg_ref,
             mixb_ref, dhb_ref, ysb_ref, dvab_ref, dvtb_ref, o2b_ref, dybb_ref):
        ys5 = ys5_ref[...]
        o = o_ref[...]
        og = og_ref[...]
        ghg = ghg_ref[...]
        c = _merge_core(ys5, o, og, ga_ref[...], gb_ref[...], wv_ref, wt_ref, ghg, who_ref)
        dhb = dh_ref[...].astype(BF16)
        dhb_ref[...] = dhb
        mixb_ref[...] = c["mixed"].astype(BF16)
        ysb_ref[...] = c["ysb"]
        o2b_ref[...] = c["o2b"]
        dmix = _dot_nt(dhb, wmo_ref[...])
        sa, sb = c["sa"], c["sb"]
        dya = dmix * sa
        dyb = dmix * sb
        dga_ref[...] = dmix * c["ya"] * sa * (1.0 - sa)
        dgb_ref[...] = dmix * c["yb"] * sb * (1.0 - sb)
        svt = c["svt"]
        dva = (dya * svt).astype(BF16)
        dvt = (dya * c["va"] * svt * (1.0 - svt)).astype(BF16)
        dvab_ref[...] = dva
        dvtb_ref[...] = dvt
        dys = jnp.zeros((tm, S5_WIDTH), F32)
        for s in range(N_SHARD):
            dys = dys + _dot_nt(dva[:, s * 256:(s + 1) * 256], wv_ref[s]) + _dot_nt(dvt[:, s * 256:(s + 1) * 256], wt_ref[s])
        th = c["th"]
        dgelu = 0.5 * (1.0 + th) + 0.5 * ys5 * (1.0 - th * th) * GELU_C * (1.0 + 3.0 * 0.044715 * ys5 * ys5)
        dys5_ref[...] = dys * dgelu
        dybb = dyb.astype(BF16)
        dybb_ref[...] = dybb
        do2 = _dot_nt(dybb, who_ref[...])
        sgo = c["sgo"]
        sil = og * sgo
        on = c["on"]
        dog_ref[...] = do2 * on * ghg * (sgo * (1.0 + og * (1.0 - sgo)))
        _accum(dghg_ref, jnp.sum(do2 * on * sil, axis=0, keepdims=True), pl.program_id(0) == 0)
        don = do2 * ghg * sil
        dos = []
        for h in range(HG_HEADS):
            sl = slice(h * HG_E, (h + 1) * HG_E)
            m = jnp.mean(don[:, sl] * on[:, sl], axis=-1, keepdims=True)
            dos.append(c["rs"][h] * (don[:, sl] - on[:, sl] * m))
        do_ref[...] = jnp.concatenate(dos, axis=1)

    tok = pl.BlockSpec((tm, D_MODEL), lambda i: (i, 0))
    s5b = pl.BlockSpec((tm, S5_WIDTH), lambda i: (i, 0))
    f32t = jax.ShapeDtypeStruct((t, D_MODEL), F32)
    bft = jax.ShapeDtypeStruct((t, D_MODEL), BF16)
    return _carry(
        body, comm, name="merge_bwd", steps=t // tm,
        out_shape=(jax.ShapeDtypeStruct((t, S5_WIDTH), F32), f32t, f32t, f32t, f32t,
                   jax.ShapeDtypeStruct((1, HG_WIDTH), F32),
                   bft, bft, jax.ShapeDtypeStruct((t, S5_WIDTH), BF16), bft, bft, bft, bft),
        in_specs=[tok, s5b, tok, tok, tok, tok,
                  _const_spec((N_SHARD, S5_WIDTH, 256)), _const_spec((N_SHARD, S5_WIDTH, 256)),
                  _const_spec((1, HG_WIDTH)), _const_spec((HG_WIDTH, D_MODEL)), _const_spec((D_MODEL, D_MODEL))],
        out_specs=(s5b, tok, tok, tok, tok, pl.BlockSpec((1, HG_WIDTH), lambda i: (0, 0)),
                   tok, tok, s5b, tok, tok, tok, tok),
        args=(dh, ys5, o, og, ga, gb, wv, wt, ghg, who, wmo),
    )


def head_fwd_bwd(h, p, tgt, gple, wpg, wpp, gfin, tm=256):
    t = h.shape[0]

    def body(h_ref, p_ref, tgt_ref, gple_ref, wpg_ref, wpp_ref, gfin_ref,
             loss_ref, dh_ref, dgple_ref, dgfin_ref, nb_ref, dlb_ref, dppb_ref):
        first = pl.program_id(0) == 0
        hv = h_ref[...]
        gple = gple_ref[...]
        gfin = gfin_ref[...]
        n, r3 = _rms_fwd(hv, gple)
        nb = n.astype(BF16)
        nb_ref[...] = nb
        pg = _sigmoid(_dot(nb, wpg_ref[...]))
        pb = p_ref[...].astype(BF16)
        pp = jnp.concatenate([_dot(pb, wpp_ref[s]) for s in range(N_SHARD)], axis=1)
        h4 = hv + pg * pp
        y, r4 = _rms_fwd(h4, gfin)
        err = y - tgt_ref[...]
        lsum = 0.5 * jnp.sum(jnp.sum(err * err, axis=-1, keepdims=True), axis=0, keepdims=True) / D_MODEL
        _accum(loss_ref, jnp.broadcast_to(lsum, (8, 128)), first)
        dy = err * (1.0 / D_MODEL)
        dh4, dgf = _rms_bwd(h4, r4, gfin, dy)
        _accum(dgfin_ref, dgf, first)
        dpp = dh4 * pg
        dppb_ref[...] = dpp.astype(BF16)
        dl = (dh4 * pp * pg * (1.0 - pg)).astype(BF16)
        dlb_ref[...] = dl
        dn = _dot_nt(dl, wpg_ref[...])
        dx, dgp = _rms_bwd(hv, r3, gple, dn)
        _accum(dgple_ref, dgp, first)
        dh_ref[...] = dh4 + dx

    tok = pl.BlockSpec((tm, D_MODEL), lambda i: (i, 0))
    vec = pl.BlockSpec((1, D_MODEL), lambda i: (0, 0))
    bft = jax.ShapeDtypeStruct((t, D_MODEL), BF16)
    return pl.pallas_call(
        body, name="head_fwd_bwd", grid=(t // tm,),
        out_shape=(jax.ShapeDtypeStruct((8, 128), F32), jax.ShapeDtypeStruct((t, D_MODEL), F32),
                   jax.ShapeDtypeStruct((1, D_MODEL), F32), jax.ShapeDtypeStruct((1, D_MODEL), F32),
                   bft, bft, bft),
        in_specs=[tok, pl.BlockSpec((tm, PLE_DIM), lambda i: (i, 0)), tok,
                  _const_spec((1, D_MODEL)), _const_spec((D_MODEL, D_MODEL)),
                  _const_spec((N_SHARD, PLE_DIM, 256)), _const_spec((1, D_MODEL))],
        out_specs=(pl.BlockSpec((8, 128), lambda i: (0, 0)), tok, vec, vec, tok, tok, tok),
        compiler_params=_cparams(("arbitrary",)),
    )(h, p, tgt, gple, wpg, wpp, gfin)


BIG = ("ffn1_w_gate", "ffn1_w_up", "ffn1_w_down", "w_in", "s5_glu_val", "s5_glu_gate", "hg_w_out",
       "w_merge_out", "ffn2_w_gate", "ffn2_w_up", "ffn2_w_down", "ple_w_gate", "ple_w_proj")
FFN_T = ("ffn1_w_gate", "ffn1_w_up", "ffn2_w_gate", "ffn2_w_up")
BIG_SHARD = {
    "ffn1_w_gate": (FF_PAD, D_MODEL), "ffn1_w_up": (FF_PAD, D_MODEL), "ffn1_w_down": (FF_PAD, D_MODEL),
    "ffn2_w_gate": (FF_PAD, D_MODEL), "ffn2_w_up": (FF_PAD, D_MODEL), "ffn2_w_down": (FF_PAD, D_MODEL),
    "w_in": (D_MODEL, IN_COLS // N_SHARD), "s5_glu_val": (S5_WIDTH, 256), "s5_glu_gate": (S5_WIDTH, 256),
    "hg_w_out": (256, D_MODEL), "w_merge_out": (256, D_MODEL), "ple_w_gate": (256, D_MODEL),
    "ple_w_proj": (PLE_DIM, 256),
}


def _lower_bound(hb):
    return jax.nn.softmax(hb, axis=0)[0:1]


class Schedule:
    def __init__(self, wts):
        self.wts = dict(wts)
        self.grads = {}

    def before(self, kernel_name):
        return None

    def after(self, kernel_name, results):
        pass

    def grad(self, name, g):
        self.grads[name] = g


def local_step(x, p, tgt, sched, sm):
    wts = sched.wts
    rows_full = lambda w: w.reshape(N_SHARD * w.shape[1], w.shape[2])

    def carried(kernel_name, fn, *args):
        outs, results = fn(*args, comm=sched.before(kernel_name))
        sched.after(kernel_name, results)
        return outs

    def weight_grad(name, xs, ys, shard):
        kernel_name = "g_" + name
        (g,), results = tn_matmul(xs, ys, kernel_name, shard, comm=sched.before(kernel_name))
        sched.grad(name, g)
        sched.after(kernel_name, results)

    lb, lb_vjp = jax.vjp(_lower_bound, sm["hg_lower_bound"])
    s5_names = ("s5_lam_re", "s5_lam_im", "s5_log_dt", "s5_b_re", "s5_b_im", "s5_c_re", "s5_c_im")
    (lam_bar, bmat, cmat), s5_vjp = jax.vjp(s5_prep, *[sm[k] for k in s5_names])
    pw_r, pw_i = _lam_powers(lam_bar)
    bmat_b = bmat.astype(BF16)
    cmat_b = cmat.astype(BF16)
    bmat_t = jnp.swapaxes(bmat, -1, -2).astype(BF16)
    cmat_t = jnp.swapaxes(cmat, -1, -2).astype(BF16)

    h1, a1, b1 = carried("ffn1_fwd", ffn_fwd, x, sm["ffn1_norm"], wts["ffn1_w_gate"], wts["ffn1_w_up"],
                         wts["ffn1_w_down"], "ffn1_fwd")
    s5in, q, f, v, og, ga, gb = inproj_fwd(h1, sm["mix_norm"], wts["w_in"])
    ys5, xp = carried("s5_fwd", s5_fwd, s5in, _scan_tables(pw_r, pw_i, False), bmat_b, cmat_b, sm["s5_d"])
    o, states = hgrn_fwd(q, f, v, lb)
    who = rows_full(wts["hg_w_out"])
    wmo = rows_full(wts["w_merge_out"])
    h2 = merge_fwd(h1, ys5, o, og, ga, gb, wts["s5_glu_val"], wts["s5_glu_gate"], sm["hg_out_norm"], who, wmo)
    (h3, a2, b2), _ = ffn_fwd(h2, sm["ffn2_norm"], wts["ffn2_w_gate"], wts["ffn2_w_up"], wts["ffn2_w_down"], "ffn2_fwd")
    loss, dh3, d_ple_norm, d_final_norm, npb, dlgb, dppb = head_fwd_bwd(
        h3, p, tgt, sm["ple_norm"], rows_full(wts["ple_w_gate"]), wts["ple_w_proj"], sm["final_norm"])

    gs = {"ple_norm": d_ple_norm, "final_norm": d_final_norm}
    weight_grad("ple_w_gate", npb, dlgb, "rows")
    weight_grad("ple_w_proj", p, dppb, "cols")

    (dh2, gs["ffn2_norm"], n2b, dhb2, da2, db2, s2), _ = ffn_bwd(
        dh3, h2, a2, b2, sm["ffn2_norm"], wts["ffn2_w_gate"], wts["ffn2_w_up"], wts["ffn2_w_down"], "ffn2_bwd")
    weight_grad("ffn2_w_gate", da2, n2b, "rows")
    weight_grad("ffn2_w_up", db2, n2b, "rows")
    weight_grad("ffn2_w_down", s2, dhb2, "rows")

    dys5, do, dog, dga, dgb, gs["hg_out_norm"], mixb, dh2b, ysb, dvab, dvtb, o2b, dybb = carried(
        "merge_bwd", merge_bwd,
        dh2, ys5, o, og, ga, gb, wts["s5_glu_val"], wts["s5_glu_gate"], sm["hg_out_norm"], who, wmo)
    weight_grad("w_merge_out", mixb, dh2b, "rows")
    weight_grad("s5_glu_val", ysb, dvab, "cols")
    weight_grad("s5_glu_gate", ysb, dvtb, "cols")
    weight_grad("hg_w_out", o2b, dybb, "rows")

    dq, df, dv, dlb = carried("hgrn_bwd", hgrn_bwd, do, q, f, v, lb, states)
    (gs["hg_lower_bound"],) = lb_vjp(dlb)
    du, dbmat, dcmat, dlam8, gs["s5_d"] = carried(
        "s5_bwd", s5_bwd,
        dys5, s5in, xp, _scan_tables(pw_r, pw_i, True), bmat_b, bmat_t, cmat_t, sm["s5_d"])
    for k, g in zip(s5_names, s5_vjp((jnp.sum(dlam8, axis=1), dbmat, dcmat))):
        gs[k] = g

    dh1, gs["mix_norm"], nmb, dprojb = carried(
        "inproj_bwd", inproj_bwd, dh2, h1, sm["mix_norm"], wts["w_in"], (du, dq, df, dv, dog, dga, dgb))
    weight_grad("w_in", nmb, dprojb, "cols")

    dx, gs["ffn1_norm"], n1b, dhb1, da1, db1, s1 = carried(
        "ffn1_bwd", ffn_bwd,
        dh1, x, a1, b1, sm["ffn1_norm"], wts["ffn1_w_gate"], wts["ffn1_w_up"], wts["ffn1_w_down"], "ffn1_bwd")
    weight_grad("ffn1_w_gate", da1, n1b, "rows")
    weight_grad("ffn1_w_up", db1, n1b, "rows")
    weight_grad("ffn1_w_down", s1, dhb1, "rows")
    return loss, dx, gs


MESH = pl.DeviceIdType.MESH
ANY = pl.BlockSpec(memory_space=pl.ANY)


def _place():
    x, y, c = lax.axis_index("x"), lax.axis_index("y"), lax.axis_index("c")
    return x, y, c


def _remote(src, dst, ssem, rsem, dev):
    return pltpu.make_async_remote_copy(src_ref=src, dst_ref=dst, send_sem=ssem, recv_sem=rsem,
                                        device_id=dev, device_id_type=MESH)


class Comm:
    def __init__(self, bufs, outs, alias, sems, hooks):
        self.bufs, self.outs, self.alias, self.sems, self.hooks = list(bufs), list(outs), alias, list(sems), hooks


def run_comm(comm, name):
    nb, no = len(comm.bufs), len(comm.outs)

    def body(*refs):
        for which in ("first", "mid", "last"):
            if which in comm.hooks:
                comm.hooks[which](refs[:nb], refs[nb:nb + no], refs[nb + no:])

    return pl.pallas_call(
        body, name=name, out_shape=tuple(comm.outs), in_specs=[ANY] * nb, out_specs=tuple([ANY] * no),
        input_output_aliases=dict(comm.alias), scratch_shapes=comm.sems,
    )(*comm.bufs)


PLACE_ROWS = {1024: 256, 704: 352, 512: 256, 256: 256}


def place_shards(shards, padded_rows, comm, name):
    n, nb, no = len(shards), len(comm.bufs), len(comm.outs)
    stage_rows = max(PLACE_ROWS.values())
    stage_cols = max(s.shape[1] for s in shards)

    def body(*refs):
        ins, cb = refs[:n], refs[n:n + nb]
        outs, co = refs[n + nb:2 * n + nb], refs[2 * n + nb:2 * n + nb + no]
        stage_f32, stage_bf16, zeros, sem = refs[2 * n + nb + no:2 * n + nb + no + 4]
        cs = refs[2 * n + nb + no + 4:]
        chip = 2 * lax.axis_index("x") + lax.axis_index("y")
        zeros[...] = jnp.zeros_like(zeros)
        comm.hooks["first"](cb, co, cs)
        for w in range(n):
            if w == n // 2:
                comm.hooks["mid"](cb, co, cs)
            r0, cols = ins[w].shape
            step = PLACE_ROWS[r0]
            src32 = stage_f32.at[pl.ds(0, step), pl.ds(0, cols)]
            dst16 = stage_bf16.at[pl.ds(0, step), pl.ds(0, cols)]
            for row in range(0, r0, step):
                pltpu.sync_copy(ins[w].at[pl.ds(row, step), :], src32)
                dst16[...] = src32[...].astype(BF16)
                pltpu.sync_copy(dst16, outs[w].at[chip, pl.ds(row, step), :])
            pad = outs[w].shape[1] - r0
            if pad:
                cp = pltpu.make_async_copy(zeros.at[pl.ds(0, pad), pl.ds(0, cols)],
                                           outs[w].at[chip, pl.ds(r0, pad), :], sem)
                cp.start()
                cp.wait()
        comm.hooks["last"](cb, co, cs)

    res = pl.pallas_call(
        body, name=name,
        out_shape=tuple(jax.ShapeDtypeStruct((N_SHARD, r, s.shape[1]), BF16) for s, r in zip(shards, padded_rows))
        + tuple(comm.outs),
        in_specs=[ANY] * (n + nb), out_specs=tuple([ANY] * (n + no)),
        input_output_aliases={n + i: n + o for i, o in comm.alias.items()},
        scratch_shapes=[pltpu.VMEM((stage_rows, stage_cols), F32), pltpu.VMEM((stage_rows, stage_cols), BF16),
                        pltpu.VMEM((FF_PAD - FF_SHARD, D_MODEL), BF16), pltpu.SemaphoreType.DMA] + comm.sems,
        compiler_params=pltpu.CompilerParams(vmem_limit_bytes=VMEM_LIMIT),
    )(*shards, *comm.bufs)
    return res[:n], res[n:]


def _carry(body, comm, *, name, steps, out_shape, in_specs, out_specs, args, scratch_shapes=()):
    out_shape, out_specs, scratch_shapes = tuple(out_shape), tuple(out_specs), list(scratch_shapes)
    if comm is None:
        res = pl.pallas_call(body, name=name, grid=(steps,), out_shape=out_shape, in_specs=list(in_specs),
                             out_specs=out_specs, scratch_shapes=scratch_shapes,
                             compiler_params=_cparams(("arbitrary",)))(*args)
        return tuple(res), ()
    n_in, n_out, n_scr = len(args), len(out_shape), len(scratch_shapes)
    nb, no = len(comm.bufs), len(comm.outs)

    def wrapped(*refs):
        ins, cb = refs[:n_in], refs[n_in:n_in + nb]
        o0 = n_in + nb
        outs, co = refs[o0:o0 + n_out], refs[o0 + n_out:o0 + n_out + no]
        s0 = o0 + n_out + no
        scr, cs = refs[s0:s0 + n_scr], refs[s0 + n_scr:]
        step = pl.program_id(0)

        def hook(which, at):
            if which in comm.hooks:
                pl.when(step == at)(lambda: comm.hooks[which](cb, co, cs))

        hook("first", 0)
        hook("mid", steps // 2)
        body(*ins, *outs, *scr)
        hook("last", steps - 1)

    res = pl.pallas_call(
        wrapped, name=name, grid=(steps,), out_shape=out_shape + tuple(comm.outs),
        in_specs=list(in_specs) + [ANY] * nb, out_specs=out_specs + (ANY,) * no,
        scratch_shapes=scratch_shapes + comm.sems,
        input_output_aliases={n_in + i: n_out + o for i, o in comm.alias.items()},
        compiler_params=_cparams(("arbitrary",)),
    )(*args, *comm.bufs)
    return tuple(res[:n_out]), tuple(res[n_out:])


def gather_comm(bufs):
    n = len(bufs)

    def copies(outs, sems):
        s_own, r_own, s_fwd, r_fwd, s_sib, r_sib = sems
        x, y, c = _place()
        me = 2 * x + y
        nbr = ((1 - x, y), (x, 1 - y))
        nbr_id = (2 * (1 - x) + y, 2 * x + (1 - y))
        diag_id = 2 * (1 - x) + (1 - y)
        sib = (x, y, 1 - c)

        def rows(w, q=None):
            r = outs[w].shape[1]
            if q is None:
                return pl.ds(pl.multiple_of(c * (r // 2), 16), r // 2)
            return pl.ds(pl.multiple_of(c * (r // 2) + q * (r // 4), 16), r // 4)

        def own(w, j):
            piece = outs[w].at[me, rows(w)]
            return _remote(piece, piece, s_own.at[w, j], r_own.at[w, j], (nbr[j][0], nbr[j][1], c))

        def from_nbr(w, j):
            piece = outs[w].at[nbr_id[j], rows(w)]
            return _remote(piece, piece, s_own.at[w, j], r_own.at[w, j], (nbr[j][0], nbr[j][1], c))

        def fwd(w, j):
            piece = outs[w].at[nbr_id[j], rows(w, j)]
            return _remote(piece, piece, s_fwd.at[w, j], r_fwd.at[w, j], (nbr[1 - j][0], nbr[1 - j][1], c))

        def from_diag(w, j):
            piece = outs[w].at[diag_id, rows(w, j)]
            return _remote(piece, piece, s_fwd.at[w, j], r_fwd.at[w, j], (nbr[1 - j][0], nbr[1 - j][1], c))

        def to_sib(w, k):
            piece = (outs[w].at[nbr_id[k], rows(w)] if k < 2 else outs[w].at[diag_id, rows(w, k - 2)])
            return _remote(piece, piece, s_sib.at[w, k], r_sib.at[w, k], sib)

        def from_sib(w, k):
            r = outs[w].shape[1]
            if k < 2:
                piece = outs[w].at[nbr_id[k], pl.ds(pl.multiple_of((1 - c) * (r // 2), 16), r // 2)]
            else:
                piece = outs[w].at[diag_id, pl.ds(pl.multiple_of((1 - c) * (r // 2) + (k - 2) * (r // 4), 16), r // 4)]
            return _remote(piece, piece, s_sib.at[w, k], r_sib.at[w, k], sib)

        return own, from_nbr, fwd, from_diag, to_sib, from_sib

    def first(_, outs, sems):
        own = copies(outs, sems)[0]
        for w in range(n):
            own(w, 0).start()
            own(w, 1).start()

    def mid(_, outs, sems):
        _, from_nbr, fwd, _, to_sib, _ = copies(outs, sems)
        for w in range(n):
            for j in range(2):
                from_nbr(w, j).wait_recv()
                fwd(w, j).start()
                to_sib(w, j).start()

    def last(_, outs, sems):
        own, _, fwd, from_diag, to_sib, from_sib = copies(outs, sems)
        for w in range(n):
            for j in range(2):
                from_diag(w, j).wait_recv()
                to_sib(w, 2 + j).start()
        for w in range(n):
            for k in range(4):
                from_sib(w, k).wait_recv()
        for w in range(n):
            for j in range(2):
                own(w, j).wait_send()
                fwd(w, j).wait_send()
            for k in range(4):
                to_sib(w, k).wait_send()

    dma = pltpu.SemaphoreType.DMA
    return Comm(bufs, [jax.ShapeDtypeStruct(b.shape, b.dtype) for b in bufs], {w: w for w in range(n)},
                [dma((n, 2)), dma((n, 2)), dma((n, 2)), dma((n, 2)), dma((n, 4)), dma((n, 4))],
                {"first": first, "mid": mid, "last": last})


def _start_wait(make):
    def first(bufs, outs, sems):
        for cp in make(bufs, outs, sems):
            cp.start()

    def last(bufs, outs, sems):
        for cp in make(bufs, outs, sems):
            cp.wait()

    return {"first": first, "last": last}


def exchange_comm(grads):
    n = len(grads)

    def make(ins, outs, sems):
        x, y, c = _place()
        cps = []
        for w in range(n):
            half = ins[w].shape[1] // 2
            src = ins[w].at[:, pl.ds(pl.multiple_of((1 - c) * half, 8), half), :]
            cps.append(_remote(src, outs[w], sems[0].at[w], sems[1].at[w], (x, y, 1 - c)))
        return cps

    dma = pltpu.SemaphoreType.DMA
    return Comm(grads, [jax.ShapeDtypeStruct((N_SHARD, g.shape[1] // 2, g.shape[2]), g.dtype) for g in grads],
                {}, [dma((n,)), dma((n,))], _start_wait(make))


def scatter_comm(sums):
    n = len(sums)

    def make(ins, outs, sems):
        x, y, c = _place()
        chips = ((1 - x, y), (x, 1 - y), (1 - x, 1 - y))
        return [_remote(ins[w].at[2 * ch[0] + ch[1]], outs[w].at[j], sems[0].at[w, j], sems[1].at[w, j],
                        (ch[0], ch[1], c))
                for w in range(n) for j, ch in enumerate(chips)]

    dma = pltpu.SemaphoreType.DMA
    return Comm(sums, [jax.ShapeDtypeStruct((3,) + s.shape[1:], s.dtype) for s in sums],
                {}, [dma((n, 3)), dma((n, 3))], _start_wait(make))


def join_comm(shards):
    n = len(shards)

    def make(_, outs, sems):
        x, y, c = _place()
        cps = []
        for w in range(n):
            half = outs[w].shape[0] // 2
            mine = outs[w].at[pl.ds(pl.multiple_of(c * half, 8), half), :]
            cps.append(_remote(mine, mine, sems[0].at[w], sems[1].at[w], (x, y, 1 - c)))
        return cps

    dma = pltpu.SemaphoreType.DMA
    return Comm(shards, [jax.ShapeDtypeStruct(s.shape, s.dtype) for s in shards], {w: w for w in range(n)},
                [dma((n,)), dma((n,))], _start_wait(make))


def allreduce_small(vec):
    half = vec.shape[0] // 2

    def body(v_ref, o_ref, pair, chips_buf, s1, r1, s2, r2, s3, r3):
        x, y, c = _place()
        chip = 2 * x + y
        sib = (x, y, 1 - c)
        mine = pl.ds(pl.multiple_of(c * half, 8), half)
        other = pl.ds(pl.multiple_of((1 - c) * half, 8), half)
        to_sib = _remote(v_ref.at[other], pair, s1, r1, sib)
        to_sib.start()
        to_sib.wait()
        chips_buf[chip] = v_ref[mine, :] + pair[...]
        sends = [_remote(chips_buf.at[chip], chips_buf.at[chip], s2.at[j], r2.at[j], (ch[0], ch[1], c))
                 for j, ch in enumerate(((1 - x, y), (x, 1 - y), (1 - x, 1 - y)))]
        for cp in sends:
            cp.start()
        for cp in sends:
            cp.wait()
        o_ref[mine, :] = (chips_buf[0] + chips_buf[1]) + (chips_buf[2] + chips_buf[3])
        back = _remote(o_ref.at[mine], o_ref.at[mine], s3, r3, sib)
        back.start()
        back.wait()

    dma = pltpu.SemaphoreType.DMA
    return pl.pallas_call(
        body, name="allreduce_small",
        out_shape=jax.ShapeDtypeStruct(vec.shape, F32),
        in_specs=[pl.BlockSpec(memory_space=pltpu.VMEM)],
        out_specs=pl.BlockSpec(memory_space=pltpu.VMEM),
        scratch_shapes=[pltpu.VMEM((half, 128), F32), pltpu.VMEM((N_SHARD, half, 128), F32),
                        dma, dma, dma((3,)), dma((3,)), dma, dma],
        compiler_params=pltpu.CompilerParams(vmem_limit_bytes=VMEM_LIMIT),
    )(vec)


ROW_TILE = 128


def add_own_half(place, g, recv, name):
    _, r, cc = g.shape
    half = r // 2
    nb = half // ROW_TILE

    def body(p_ref, g_ref, r_ref, o_ref, ob_ref):
        s = g_ref[...] + r_ref[...]
        ob_ref[...] = s.astype(BF16)

        @pl.when(pl.program_id(1) == p_ref[0])
        def _():
            o_ref[...] = s

    blk = (None, ROW_TILE, cc)
    return pl.pallas_call(
        body, name=name,
        grid_spec=pltpu.PrefetchScalarGridSpec(
            num_scalar_prefetch=1, grid=(nb, N_SHARD),
            in_specs=[pl.BlockSpec(blk, lambda i, s, p_ref: (s, p_ref[1] * nb + i, 0)),
                      pl.BlockSpec(blk, lambda i, s, p_ref: (s, i, 0))],
            out_specs=(pl.BlockSpec((ROW_TILE, cc), lambda i, s, p_ref: (i, 0)),
                       pl.BlockSpec(blk, lambda i, s, p_ref: (s, i, 0)))),
        out_shape=(jax.ShapeDtypeStruct((half, cc), F32),
                   jax.ShapeDtypeStruct((N_SHARD, half, cc), BF16)),
        compiler_params=_cparams(("arbitrary", "arbitrary")),
    )(place, g, recv)


def add_chip_sums(place, own, recv, name):
    half, cc = own.shape
    nb = half // ROW_TILE

    def body(s_ref, o_ref, r_ref, out_ref):
        del s_ref
        acc = o_ref[...] + r_ref[0].astype(F32)
        acc = acc + r_ref[1].astype(F32)
        out_ref[...] = acc + r_ref[2].astype(F32)

    return pl.pallas_call(
        body, name=name,
        grid_spec=pltpu.PrefetchScalarGridSpec(
            num_scalar_prefetch=1, grid=(nb,),
            in_specs=[pl.BlockSpec((ROW_TILE, cc), lambda i, s_ref: (i, 0)),
                      pl.BlockSpec((3, ROW_TILE, cc), lambda i, s_ref: (0, i, 0))],
            out_specs=pl.BlockSpec((ROW_TILE, cc), lambda i, s_ref: (s_ref[1] * nb + i, 0))),
        out_shape=jax.ShapeDtypeStruct((2 * half, cc), F32),
        compiler_params=_cparams(("arbitrary",)),
    )(place, own, recv)


def adamw(w, m, v, g, name, copy_g=False):
    r, cc = w.shape
    tr = next(t for t in (256, 352, r) if r % t == 0)
    bc1 = 1.0 / (1.0 - ADAM_B1 ** ADAM_STEP)
    bc2 = 1.0 / (1.0 - ADAM_B2 ** ADAM_STEP)

    def body(w_ref, m_ref, v_ref, g_ref, d_ref, mo_ref, vo_ref, *go_ref):
        gv = g_ref[...]
        mn = ADAM_B1 * m_ref[...] + (1.0 - ADAM_B1) * gv
        vn = ADAM_B2 * v_ref[...] + (1.0 - ADAM_B2) * (gv * gv)
        mo_ref[...] = mn
        vo_ref[...] = vn
        d_ref[...] = -ADAM_LR * ((mn * bc1) / (jnp.sqrt(vn * bc2) + ADAM_EPS) + ADAM_WD * w_ref[...])
        if copy_g:
            go_ref[0][...] = gv

    blk = pl.BlockSpec((tr, cc), lambda i: (i, 0))
    shp = jax.ShapeDtypeStruct((r, cc), F32)
    nout = 4 if copy_g else 3
    return pl.pallas_call(
        body, name=name, grid=(r // tr,),
        out_shape=(shp,) * nout, in_specs=[blk] * 4, out_specs=(blk,) * nout,
        compiler_params=_cparams(("arbitrary",)),
    )(w, m, v, g)


GATHER_FIRST = ("ffn1_w_gate", "ffn1_w_up", "ffn1_w_down")
GATHER_ON = {"ffn1_fwd": ("w_in", "s5_glu_val", "s5_glu_gate", "hg_w_out", "w_merge_out"),
             "s5_fwd": ("ffn2_w_gate", "ffn2_w_up", "ffn2_w_down", "ple_w_gate", "ple_w_proj")}
REDUCE = ((("ple_w_gate", "ple_w_proj", "ffn2_w_gate", "ffn2_w_up", "ffn2_w_down"), "merge_bwd", "hgrn_bwd"),
          (("w_merge_out", "s5_glu_val", "s5_glu_gate", "hg_w_out"), "s5_bwd", "inproj_bwd"),
          (("w_in",), None, "ffn1_bwd"),
          (("ffn1_w_gate",), "g_ffn1_w_up", "g_ffn1_w_down"),
          (("ffn1_w_up",), "g_ffn1_w_down", None),
          (("ffn1_w_down",), None, None))


def merge_comms(comms):
    if len(comms) == 1:
        return comms[0], [len(comms[0].outs)]
    bufs, outs, sems, alias, spans = [], [], [], {}, []
    for c in comms:
        spans.append((len(bufs), len(bufs) + len(c.bufs), len(outs), len(outs) + len(c.outs),
                      len(sems), len(sems) + len(c.sems)))
        alias.update({len(bufs) + i: len(outs) + o for i, o in c.alias.items()})
        bufs, outs, sems = bufs + c.bufs, outs + c.outs, sems + c.sems

    def hook(which):
        def run(b, o, s):
            for c, (b0, b1, o0, o1, s0, s1) in zip(comms, spans):
                if which in c.hooks:
                    c.hooks[which](b[b0:b1], o[o0:o1], s[s0:s1])
        return run

    hooks = {w: hook(w) for w in ("first", "mid", "last") if any(w in c.hooks for c in comms)}
    return Comm(bufs, outs, alias, sems, hooks), [len(c.outs) for c in comms]


class DistSchedule(Schedule):
    def __init__(self, w_rows, chip, core):
        first = gather_comm([_gather_buffer(k, w_rows[k], chip) for k in GATHER_FIRST])
        later = [k for k in BIG if k not in GATHER_FIRST]
        placed, gathered = place_shards([w_rows[k] for k in later], [BIG_SHARD[k][0] for k in later], first,
                                        "place_shards_gather_ffn1")
        super().__init__(zip(GATHER_FIRST, gathered))
        self.bufs = dict(zip(later, placed))
        self.place = jnp.stack([chip, core])
        self.sums, self.halves = {}, {}

    def _exchange(self, names):
        return exchange_comm([self.grads[k] for k in names])

    def _scatter(self, names):
        return scatter_comm([self.sums[k][1] for k in names])

    def _pair_sums(self, names, recv):
        for k, r in zip(names, recv):
            self.sums[k] = add_own_half(self.place, self.grads[k], r, "pair_sum_" + k)

    def _chip_sums(self, names, recv):
        for k, r in zip(names, recv):
            self.halves[k] = add_chip_sums(self.place, self.sums[k][0], r, "chip_sum_" + k)

    def before(self, kernel_name):
        comms, takers = [], []
        if kernel_name in GATHER_ON:
            names = GATHER_ON[kernel_name]
            comms.append(gather_comm([self.bufs[k] for k in names]))
            takers.append(lambda res, names=names: self.wts.update(zip(names, res)))
        for names, exchange_on, scatter_on in REDUCE:
            if kernel_name == exchange_on:
                comms.append(self._exchange(names))
                takers.append(lambda res, names=names: self._pair_sums(names, res))
            if kernel_name == scatter_on:
                if exchange_on is None:
                    self._pair_sums(names, run_comm(self._exchange(names), "exchange_" + names[0]))
                comms.append(self._scatter(names))
                takers.append(lambda res, names=names: self._chip_sums(names, res))
        if not comms:
            return None
        merged, counts = merge_comms(comms)
        self.pending = (takers, counts)
        return merged

    def after(self, kernel_name, results):
        if not results:
            return
        takers, counts = self.pending
        start = 0
        for take, count in zip(takers, counts):
            take(results[start:start + count])
            start += count

    def finish(self):
        tail = [names for names, _, scatter_on in REDUCE if scatter_on is None]
        alone = [k for names, exchange_on, scatter_on in REDUCE if scatter_on is None and exchange_on is None
                 for k in names]
        self._pair_sums(alone, run_comm(self._exchange(alone), "exchange_tail"))
        tail = [k for names in tail for k in names]
        early = [k for k in BIG if k not in tail]
        both, counts = merge_comms([self._scatter(tail), join_comm([self.halves[k] for k in early])])
        res = run_comm(both, "scatter_tail_join_early")
        self._chip_sums(tail, res[:counts[0]])
        full = dict(zip(early, res[counts[0]:]))
        full.update(zip(tail, run_comm(join_comm([self.halves[k] for k in tail]), "join_tail")))
        return full


SMALL = ("ffn1_norm", "mix_norm", "s5_lam_re", "s5_lam_im", "s5_log_dt", "s5_b_re", "s5_b_im", "s5_c_re",
         "s5_c_im", "s5_d", "hg_lower_bound", "hg_out_norm", "ffn2_norm", "ple_norm", "final_norm")
WEIGHTS = ("ffn1_norm", "ffn1_w_gate", "ffn1_w_up", "ffn1_w_down", "mix_norm", "w_in", "s5_lam_re", "s5_lam_im",
           "s5_log_dt", "s5_b_re", "s5_b_im", "s5_c_re", "s5_c_im", "s5_d", "s5_glu_val", "s5_glu_gate",
           "hg_lower_bound", "hg_out_norm", "hg_w_out", "w_merge_out", "ffn2_norm", "ffn2_w_gate", "ffn2_w_up",
           "ffn2_w_down", "ple_norm", "ple_w_gate", "ple_w_proj", "final_norm")


def _as_rows(name, w):
    return jnp.swapaxes(w[0], 0, 1) if name in FFN_T else w[0]


def _from_rows(name, w):
    return (jnp.swapaxes(w, 0, 1) if name in FFN_T else w)[None]


def _gather_buffer(name, w_rows, chip):
    r, c = BIG_SHARD[name]
    shard = jnp.pad(w_rows.astype(BF16), ((0, r - w_rows.shape[0]), (0, 0)))
    return lax.dynamic_update_slice(jnp.zeros((N_SHARD, r, c), BF16), shard[None], (chip, 0, 0))


def _pack(parts):
    flat = jnp.concatenate([jnp.zeros((128,), F32)] + [a.reshape(-1) for a in parts])
    rows = -(-flat.shape[0] // 2048) * 16
    return jnp.pad(flat, (0, rows * 128 - flat.shape[0])).reshape(rows, 128)


def _unpack(vec, likes):
    flat = vec.reshape(-1)
    out, off = [], 128
    for a in likes:
        out.append(flat[off:off + a.size].reshape(a.shape))
        off += a.size
    return out


def _small_view(name, w):
    if name.startswith("s5_") and name != "s5_d":
        return w[0]
    if name == "final_norm":
        return w.reshape(1, D_MODEL)
    return w


def kernel(x, p, ffn1_norm, ffn1_w_gate, ffn1_w_up, ffn1_w_down, mix_norm, w_in, s5_lam_re, s5_lam_im, s5_log_dt, s5_b_re, s5_b_im, s5_c_re, s5_c_im, s5_d, s5_glu_val, s5_glu_gate, hg_lower_bound, hg_out_norm, hg_w_out, w_merge_out, ffn2_norm, ffn2_w_gate, ffn2_w_up, ffn2_w_down, ple_norm, ple_w_gate, ple_w_proj, final_norm, loss_target, m_ffn1_norm, m_ffn1_w_gate, m_ffn1_w_up, m_ffn1_w_down, m_mix_norm, m_w_in, m_s5_lam_re, m_s5_lam_im, m_s5_log_dt, m_s5_b_re, m_s5_b_im, m_s5_c_re, m_s5_c_im, m_s5_d, m_s5_glu_val, m_s5_glu_gate, m_hg_lower_bound, m_hg_out_norm, m_hg_w_out, m_w_merge_out, m_ffn2_norm, m_ffn2_w_gate, m_ffn2_w_up, m_ffn2_w_down, m_ple_norm, m_ple_w_gate, m_ple_w_proj, m_final_norm, v_ffn1_norm, v_ffn1_w_gate, v_ffn1_w_up, v_ffn1_w_down, v_mix_norm, v_w_in, v_s5_lam_re, v_s5_lam_im, v_s5_log_dt, v_s5_b_re, v_s5_b_im, v_s5_c_re, v_s5_c_im, v_s5_d, v_s5_glu_val, v_s5_glu_gate, v_hg_lower_bound, v_hg_out_norm, v_hg_w_out, v_w_merge_out, v_ffn2_norm, v_ffn2_w_gate, v_ffn2_w_up, v_ffn2_w_down, v_ple_norm, v_ple_w_gate, v_ple_w_proj, v_final_norm):
    given = dict(locals())
    wv = {k: given[k] for k in WEIGHTS}
    mv = {k: given["m_" + k] for k in WEIGHTS}
    vv = {k: given["v_" + k] for k in WEIGHTS}

    core = lax.axis_index("c").astype(jnp.int32)
    chip = (2 * lax.axis_index("x") + lax.axis_index("y")).astype(jnp.int32)
    w_rows = {k: _as_rows(k, wv[k]) for k in BIG}
    sched = DistSchedule(w_rows, chip, core)
    sm = {k: _small_view(k, wv[k]) for k in SMALL}

    loss_blk, dx, gsm = local_step(x[0], p[0, 0], loss_target[0], sched, sm)
    full = sched.finish()

    small_likes = [wv[k] for k in SMALL]
    packed = _pack([gsm[k] for k in SMALL])
    packed = packed.at[0, 0].set(loss_blk[0, 0])
    total = allreduce_small(packed)
    loss = total[0, 0]
    gsmall = dict(zip(SMALL, _unpack(total, small_likes)))

    grads, deltas, new_m, new_v = {}, {}, {}, {}
    for k in BIG:
        padded = full[k].shape != w_rows[k].shape
        res = adamw(w_rows[k], _as_rows(k, mv[k]), _as_rows(k, vv[k]), full[k], "adamw_" + k, copy_g=padded)
        grads[k] = _from_rows(k, res[3] if padded else full[k])
        deltas[k], new_m[k], new_v[k] = (_from_rows(k, a) for a in res[:3])
    sw = _pack([wv[k] for k in SMALL])
    smm = _pack([mv[k] for k in SMALL])
    svv = _pack([vv[k] for k in SMALL])
    sd, smn, svn = adamw(sw, smm, svv, total, "adamw_small")
    for k, d, mn, vn in zip(SMALL, _unpack(sd, small_likes), _unpack(smn, small_likes), _unpack(svn, small_likes)):
        grads[k], deltas[k], new_m[k], new_v[k] = gsmall[k], d, mn, vn

    return (loss, dx[None], *[grads[k] for k in WEIGHTS], *[deltas[k] for k in WEIGHTS],
            *[new_m[k] for k in WEIGHTS], *[new_v[k] for k in WEIGHTS])
```

```python
import math

import jax
import jax.numpy as jnp
from jax import lax
from jax.experimental import pallas as pl
from jax.experimental.pallas import tpu as pltpu

F32 = jnp.float32
BF16 = jnp.bfloat16

D_MODEL = 1024
D_FF = 2816
N_SHARD = 4
FF_SHARD = D_FF // N_SHARD
FF_PAD = 768
NORM_EPS = 1e-6
PLE_DIM = 256

S5_WIDTH = 512
S5_GROUPS = 32
S5_GROUP = 16
S5_STATE = 64
S5_N = S5_GROUPS * S5_STATE
S5_KT = 2

HG_HEADS = 8
HG_E = 128
HG_WIDTH = 1024
CHUNK = 64
HG_SUB = 4
IN_COLS = S5_WIDTH + 4 * HG_WIDTH + 2 * D_MODEL
IN_SPLITS = (0, 512, 1536, 2560, 3584, 4608, 5632, 6656)

ADAM_LR = 0.001
ADAM_B1 = 0.9
ADAM_B2 = 0.999
ADAM_EPS = 1e-08
ADAM_WD = 0.01
ADAM_STEP = 10

VMEM_LIMIT = 60 * 1024 * 1024
HIGHEST = lax.Precision.HIGHEST


def _cparams(sem=None, **kw):
    return pltpu.CompilerParams(dimension_semantics=sem, vmem_limit_bytes=VMEM_LIMIT, **kw)


def _const_spec(shape):
    nd = len(shape)
    return pl.BlockSpec(shape, lambda *_: (0,) * nd, pipeline_mode=pl.Buffered(1))


def _dot(a, b):
    return jnp.dot(a, b, preferred_element_type=F32)


def _dot_nt(a, b):
    return lax.dot_general(a, b, (((1,), (1,)), ((), ())), preferred_element_type=F32)


def _dot_tn(a, b):
    return lax.dot_general(a, b, (((0,), (0,)), ((), ())), preferred_element_type=F32)


def _sigmoid(x):
    return 1.0 / (1.0 + jnp.exp(-x))


def _rms_fwd(x, g):
    r = lax.rsqrt(jnp.mean(x * x, axis=-1, keepdims=True) + NORM_EPS)
    return x * r * g, r


def _rms_bwd(x, r, g, dy):
    xh = x * r
    dyg = dy * g
    m = jnp.mean(dyg * xh, axis=-1, keepdims=True)
    return r * (dyg - xh * m), jnp.sum(dy * xh, axis=0, keepdims=True)


def _accum(ref, val, first):
    @pl.when(first)
    def _():
        ref[...] = val

    @pl.when(jnp.logical_not(first))
    def _():
        ref[...] += val


def ffn_fwd(h, gain, wg, wu, wd, name, comm=None, tm=512):
    t = h.shape[0]

    def body(h_ref, g_ref, wg_ref, wu_ref, wd_ref, o_ref, a_ref, b_ref):
        hv = h_ref[...]
        n, _ = _rms_fwd(hv, g_ref[...])
        nb = n.astype(BF16)
        acc = jnp.zeros((tm, D_MODEL), F32)
        for s in range(N_SHARD):
            a = _dot_nt(nb, wg_ref[s])
            b = _dot_nt(nb, wu_ref[s])
            a_ref[s] = a.astype(BF16)
            b_ref[s] = b.astype(BF16)
            sv = (a * _sigmoid(a) * b).astype(BF16)
            acc = acc + _dot(sv, wd_ref[s])
        o_ref[...] = hv + 0.5 * acc

    return _carry(
        body, comm, name=name, steps=t // tm,
        out_shape=(jax.ShapeDtypeStruct((t, D_MODEL), F32),
                   jax.ShapeDtypeStruct((N_SHARD, t, FF_PAD), BF16),
                   jax.ShapeDtypeStruct((N_SHARD, t, FF_PAD), BF16)),
        in_specs=[pl.BlockSpec((tm, D_MODEL), lambda i: (i, 0)),
                  _const_spec((1, D_MODEL)),
                  _const_spec((N_SHARD, FF_PAD, D_MODEL)),
                  _const_spec((N_SHARD, FF_PAD, D_MODEL)),
                  _const_spec((N_SHARD, FF_PAD, D_MODEL))],
        out_specs=(pl.BlockSpec((tm, D_MODEL), lambda i: (i, 0)),
                   pl.BlockSpec((N_SHARD, tm, FF_PAD), lambda i: (0, i, 0)),
                   pl.BlockSpec((N_SHARD, tm, FF_PAD), lambda i: (0, i, 0))),
        args=(h, gain, wg, wu, wd),
    )


def ffn_bwd(dho, h, a, b, gain, wg, wu, wd, name, comm=None, tm=256):
    t = h.shape[0]

    def body(dho_ref, h_ref, a_ref, b_ref, g_ref, wg_ref, wu_ref, wd_ref,
             dh_ref, dg_ref, nb_ref, dhb_ref, da_ref, db_ref, s_ref):
        hv = h_ref[...]
        g = g_ref[...]
        n, r = _rms_fwd(hv, g)
        nb_ref[...] = n.astype(BF16)
        dhalf = (0.5 * dho_ref[...]).astype(BF16)
        dhb_ref[...] = dhalf
        dn = jnp.zeros((tm, D_MODEL), F32)
        for s in range(N_SHARD):
            av = a_ref[s].astype(F32)
            bv = b_ref[s].astype(F32)
            sg = _sigmoid(av)
            sil = av * sg
            s_ref[s] = (sil * bv).astype(BF16)
            ds = _dot_nt(dhalf, wd_ref[s])
            da = (ds * bv * (sg * (1.0 + av * (1.0 - sg)))).astype(BF16)
            db = (ds * sil).astype(BF16)
            da_ref[s] = da
            db_ref[s] = db
            dn = dn + _dot(da, wg_ref[s]) + _dot(db, wu_ref[s])
        dx, dg = _rms_bwd(hv, r, g, dn)
        dh_ref[...] = dho_ref[...] + dx
        _accum(dg_ref, dg, pl.program_id(0) == 0)

    tok = pl.BlockSpec((tm, D_MODEL), lambda i: (i, 0))
    hid = pl.BlockSpec((N_SHARD, tm, FF_PAD), lambda i: (0, i, 0))
    return _carry(
        body, comm, name=name, steps=t // tm,
        out_shape=(jax.ShapeDtypeStruct((t, D_MODEL), F32),
                   jax.ShapeDtypeStruct((1, D_MODEL), F32),
                   jax.ShapeDtypeStruct((t, D_MODEL), BF16),
                   jax.ShapeDtypeStruct((t, D_MODEL), BF16),
                   jax.ShapeDtypeStruct((N_SHARD, t, FF_PAD), BF16),
                   jax.ShapeDtypeStruct((N_SHARD, t, FF_PAD), BF16),
                   jax.ShapeDtypeStruct((N_SHARD, t, FF_PAD), BF16)),
        in_specs=[tok, tok, hid, hid, _const_spec((1, D_MODEL)),
                  _const_spec((N_SHARD, FF_PAD, D_MODEL)),
                  _const_spec((N_SHARD, FF_PAD, D_MODEL)),
                  _const_spec((N_SHARD, FF_PAD, D_MODEL))],
        out_specs=(tok, pl.BlockSpec((1, D_MODEL), lambda i: (0, 0)), tok, tok, hid, hid, hid),
        args=(dho, h, a, b, gain, wg, wu, wd),
    )


TN_VMEM_BUDGET = 44 * 1024 * 1024


def tn_matmul(x, y, name, shard, comm=None):
    x3, y3 = x.ndim == 3, y.ndim == 3
    t = x.shape[-2]
    m = x.shape[-1] // (N_SHARD if (shard == "rows" and not x3) else 1)
    n = y.shape[-1] // (N_SHARD if (shard == "cols" and not y3) else 1)
    per_token = 2 * (m * x.dtype.itemsize + n * y.dtype.itemsize)
    tk = t
    while tk > 512 and tk * per_token + 2 * m * n * 4 > TN_VMEM_BUDGET:
        tk //= 2
    nk = t // tk

    out_shape = jax.ShapeDtypeStruct((N_SHARD, m, n), F32)
    if nk == 1:
        def whole(x_ref, y_ref, o_ref):
            o_ref[...] = _dot_tn(x_ref[...].astype(BF16), y_ref[...].astype(BF16))

        x_one = (pl.BlockSpec((None, t, m), lambda s: (s, 0, 0)) if x3 else
                 pl.BlockSpec((t, m), (lambda s: (0, s)) if shard == "rows" else (lambda s: (0, 0))))
        y_one = (pl.BlockSpec((None, t, n), lambda s: (s, 0, 0)) if y3 else
                 pl.BlockSpec((t, n), (lambda s: (0, s)) if shard == "cols" else (lambda s: (0, 0))))
        return _carry(whole, comm, name=name, steps=N_SHARD, out_shape=(out_shape,), in_specs=[x_one, y_one],
                      out_specs=(pl.BlockSpec((None, m, n), lambda s: (s, 0, 0)),), args=(x, y))
    assert comm is None

    def body(x_ref, y_ref, o_ref):
        _accum(o_ref, _dot_tn(x_ref[...].astype(BF16), y_ref[...].astype(BF16)), pl.program_id(1) == 0)

    if x3:
        x_spec = pl.BlockSpec((None, tk, m), lambda s, k: (s, k, 0))
    elif shard == "rows":
        x_spec = pl.BlockSpec((tk, m), lambda s, k: (k, s))
    else:
        x_spec = pl.BlockSpec((tk, m), lambda s, k: (k, 0))
    if y3:
        y_spec = pl.BlockSpec((None, tk, n), lambda s, k: (s, k, 0))
    elif shard == "cols":
        y_spec = pl.BlockSpec((tk, n), lambda s, k: (k, s))
    else:
        y_spec = pl.BlockSpec((tk, n), lambda s, k: (k, 0))
    res = pl.pallas_call(
        body, name=name, grid=(N_SHARD, nk),
        out_shape=out_shape,
        in_specs=[x_spec, y_spec],
        out_specs=pl.BlockSpec((None, m, n), lambda s, k: (s, 0, 0)),
        compiler_params=_cparams(("arbitrary", "arbitrary")),
    )(x, y)
    return (res,), ()


def inproj_fwd(h, gain, w_in, tm=256):
    t = h.shape[0]
    widths = [IN_SPLITS[j + 1] - IN_SPLITS[j] for j in range(7)]
    sh_cols = IN_COLS // N_SHARD

    def body(h_ref, g_ref, w_ref, *outs):
        n, _ = _rms_fwd(h_ref[...], g_ref[...])
        nb = n.astype(BF16)
        proj = jnp.concatenate([_dot(nb, w_ref[s]) for s in range(N_SHARD)], axis=1)
        for j, o_ref in enumerate(outs):
            o_ref[...] = proj[:, IN_SPLITS[j]:IN_SPLITS[j + 1]]

    return pl.pallas_call(
        body, name="inproj_fwd", grid=(t // tm,),
        out_shape=tuple(jax.ShapeDtypeStruct((t, w), F32) for w in widths),
        in_specs=[pl.BlockSpec((tm, D_MODEL), lambda i: (i, 0)),
                  _const_spec((1, D_MODEL)),
                  _const_spec((N_SHARD, D_MODEL, sh_cols))],
        out_specs=tuple(pl.BlockSpec((tm, w), lambda i: (i, 0)) for w in widths),
        compiler_params=_cparams(("arbitrary",)),
    )(h, gain, w_in)


def inproj_bwd(dres, h, gain, w_in, dparts, comm=None, tm=256):
    t = h.shape[0]
    widths = [IN_SPLITS[j + 1] - IN_SPLITS[j] for j in range(7)]
    sh_cols = IN_COLS // N_SHARD

    def body(dres_ref, h_ref, g_ref, w_ref, d0, d1, d2, d3, d4, d5, d6, dh_ref, dg_ref, nb_ref, dp_ref):
        hv = h_ref[...]
        g = g_ref[...]
        n, r = _rms_fwd(hv, g)
        nb_ref[...] = n.astype(BF16)
        dproj = jnp.concatenate([d[...] for d in (d0, d1, d2, d3, d4, d5, d6)], axis=1).astype(BF16)
        dp_ref[...] = dproj
        dn = jnp.zeros((tm, D_MODEL), F32)
        for s in range(N_SHARD):
            dn = dn + _dot_nt(dproj[:, s * sh_cols:(s + 1) * sh_cols], w_ref[s])
        dx, dg = _rms_bwd(hv, r, g, dn)
        dh_ref[...] = dres_ref[...] + dx
        _accum(dg_ref, dg, pl.program_id(0) == 0)

    tok = pl.BlockSpec((tm, D_MODEL), lambda i: (i, 0))
    return _carry(
        body, comm, name="inproj_bwd", steps=t // tm,
        out_shape=(jax.ShapeDtypeStruct((t, D_MODEL), F32),
                   jax.ShapeDtypeStruct((1, D_MODEL), F32),
                   jax.ShapeDtypeStruct((t, D_MODEL), BF16),
                   jax.ShapeDtypeStruct((t, IN_COLS), BF16)),
        in_specs=[tok, tok, _const_spec((1, D_MODEL)), _const_spec((N_SHARD, D_MODEL, sh_cols))]
                 + [pl.BlockSpec((tm, w), lambda i: (i, 0)) for w in widths],
        out_specs=(tok, pl.BlockSpec((1, D_MODEL), lambda i: (0, 0)), tok,
                   pl.BlockSpec((tm, IN_COLS), lambda i: (i, 0))),
        args=(dres, h, gain, w_in, *dparts),
    )


def s5_prep(lam_re, lam_im, log_dt, b_re, b_im, c_re, c_im):
    dt = jnp.exp(log_dt)[:, None]
    mag = jnp.exp(lam_re * dt)
    lbr = mag * jnp.cos(lam_im * dt)
    lbi = mag * jnp.sin(lam_im * dt)
    den = lam_re * lam_re + lam_im * lam_im
    nr, ni = lbr - 1.0, lbi
    kr = (nr * lam_re + ni * lam_im) / den
    ki = (ni * lam_re - nr * lam_im) / den
    bbr = kr[..., None] * b_re - ki[..., None] * b_im
    bbi = kr[..., None] * b_im + ki[..., None] * b_re
    eye = jnp.eye(16, dtype=F32)

    def bm(bp):
        return jnp.einsum('kgph,gG->kghGp', bp.reshape(S5_KT, 16, S5_STATE, S5_GROUP), eye).reshape(S5_KT, 256, 1024)

    def cm(cp):
        return jnp.einsum('kghp,gG->kgpGh', cp.reshape(S5_KT, 16, S5_GROUP, S5_STATE), eye).reshape(S5_KT, 1024, 256)

    lam_bar = jnp.stack([lbr.reshape(S5_N), lbi.reshape(S5_N)])
    bmat = jnp.stack([bm(bbr), bm(bbi)])
    cmat = jnp.stack([cm(c_re), -cm(c_im)])
    return lam_bar, bmat, cmat


def _lam_powers(lam_bar):
    lr, li = lam_bar[0], lam_bar[1]
    pr, pi = [lr], [li]
    for _ in range(7):
        pr, pi = pr + [pr[-1] * lr - pi[-1] * li], pi + [pr[-1] * li + pi[-1] * lr]
    return jnp.stack(pr), jnp.stack(pi)


SCAN_SHIFTS = ((1, 0), (2, 1), (4, 3))


def _scan_tables(pw_r, pw_i, reverse):
    rows = jnp.arange(8)[:, None]
    planes_r, planes_i = [], []
    for sh, idx in SCAN_SHIFTS:
        keep = (rows < 8 - sh) if reverse else (rows >= sh)
        planes_r.append(jnp.where(keep, pw_r[idx:idx + 1], 0.0))
        planes_i.append(jnp.where(keep, pw_i[idx:idx + 1], 0.0))
    carry = [pw_r[::-1], pw_i[::-1]] if reverse else [pw_r, pw_i]
    return jnp.stack(planes_r + planes_i + carry)


def s5_fwd(u, tab, bmat, cmat, dvec, comm=None, tm=256):
    t = u.shape[0]
    nch = tm // 8

    def body(u_ref, tab_ref, b_ref, c_ref, d_ref, y_ref, xp_ref, x_scr, carry):
        @pl.when(pl.program_id(0) == 0)
        def _():
            carry[...] = jnp.zeros_like(carry)

        uv = u_ref[...]
        ub = uv.astype(BF16)
        for part in range(2):
            for kt in range(S5_KT):
                x_scr[:, pl.ds(part * S5_N + kt * 1024, 1024)] = _dot(ub[:, kt * 256:(kt + 1) * 256], b_ref[part, kt])
        row = lax.broadcasted_iota(jnp.int32, (8, S5_N), 0)

        def chunk(i, c):
            cr, ci = c
            r0 = pl.multiple_of(i * 8, 8)
            xr = x_scr[pl.ds(r0, 8), pl.ds(0, S5_N)]
            xi = x_scr[pl.ds(r0, 8), pl.ds(S5_N, S5_N)]
            for lvl, (sh, _) in enumerate(SCAN_SHIFTS):
                sr = pltpu.roll(xr, sh, 0)
                si = pltpu.roll(xi, sh, 0)
                lr = tab_ref[lvl]
                li = tab_ref[3 + lvl]
                xr, xi = xr + lr * sr - li * si, xi + lr * si + li * sr
            pwr = tab_ref[6]
            pwi = tab_ref[7]
            xr, xi = xr + pwr * cr - pwi * ci, xi + pwr * ci + pwi * cr
            x_scr[pl.ds(r0, 8), pl.ds(0, S5_N)] = xr
            x_scr[pl.ds(r0, 8), pl.ds(S5_N, S5_N)] = xi
            xp_ref[pl.ds(r0, 8), pl.ds(0, S5_N)] = jnp.where(row == 0, cr, pltpu.roll(xr, 1, 0))
            xp_ref[pl.ds(r0, 8), pl.ds(S5_N, S5_N)] = jnp.where(row == 0, ci, pltpu.roll(xi, 1, 0))
            return xr[7:8, :], xi[7:8, :]

        cr, ci = lax.fori_loop(0, nch, chunk, (carry[0:1, :], carry[1:2, :]))
        carry[0:1, :] = cr
        carry[1:2, :] = ci
        for kt in range(S5_KT):
            acc = jnp.zeros((tm, 256), F32)
            for part in range(2):
                acc = acc + _dot(x_scr[:, pl.ds(part * S5_N + kt * 1024, 1024)].astype(BF16), c_ref[part, kt])
            y_ref[:, pl.ds(kt * 256, 256)] = acc + d_ref[:, pl.ds(kt * 256, 256)] * uv[:, kt * 256:(kt + 1) * 256]

    return _carry(
        body, comm, name="s5_fwd", steps=t // tm,
        out_shape=(jax.ShapeDtypeStruct((t, S5_WIDTH), F32),
                   jax.ShapeDtypeStruct((t, 2 * S5_N), F32)),
        in_specs=[pl.BlockSpec((tm, S5_WIDTH), lambda i: (i, 0)),
                  _const_spec((8, 8, S5_N)),
                  _const_spec((2, S5_KT, 256, 1024)), _const_spec((2, S5_KT, 1024, 256)),
                  _const_spec((1, S5_WIDTH))],
        out_specs=(pl.BlockSpec((tm, S5_WIDTH), lambda i: (i, 0)),
                   pl.BlockSpec((tm, 2 * S5_N), lambda i: (i, 0))),
        scratch_shapes=[pltpu.VMEM((tm, 2 * S5_N), F32), pltpu.VMEM((8, S5_N), F32)],
        args=(u, tab, bmat, cmat, dvec),
    )


def s5_bwd(dy, u, xp, tab, bmat, bmat_t, cmat_t, dvec, comm=None, tm=256):
    t = u.shape[0]
    nt = t // tm
    nch = tm // 8

    def body(dy_ref, u_ref, xp_ref, tab_ref, b_ref, bt_ref, ct_ref, d_ref,
             du_ref, db_ref, dc_ref, dl_ref, dd_ref, g_scr, x_scr, carry):
        first = pl.program_id(0) == 0

        @pl.when(first)
        def _():
            carry[...] = jnp.zeros_like(carry)
            dl_ref[...] = jnp.zeros_like(dl_ref)

        dyv = dy_ref[...]
        uv = u_ref[...]
        dyb = dyv.astype(BF16)
        ub = uv.astype(BF16)
        lr1 = tab_ref[6, 7:8, :]
        li1 = tab_ref[7, 7:8, :]
        for kt in range(S5_KT):
            cols = pl.ds(kt * 1024, 1024)
            colsi = pl.ds(S5_N + kt * 1024, 1024)
            g_scr[:, cols] = _dot(dyb[:, kt * 256:(kt + 1) * 256], ct_ref[0, kt])
            g_scr[:, colsi] = _dot(dyb[:, kt * 256:(kt + 1) * 256], ct_ref[1, kt])
            bur = _dot(ub[:, kt * 256:(kt + 1) * 256], b_ref[0, kt])
            bui = _dot(ub[:, kt * 256:(kt + 1) * 256], b_ref[1, kt])
            xpr = xp_ref[:, cols]
            xpi = xp_ref[:, colsi]
            lrk = lr1[:, kt * 1024:(kt + 1) * 1024]
            lik = li1[:, kt * 1024:(kt + 1) * 1024]
            x_scr[:, cols] = lrk * xpr - lik * xpi + bur
            x_scr[:, colsi] = lrk * xpi + lik * xpr + bui

        def chunk(j, c):
            cr, ci = c
            r0 = pl.multiple_of((nch - 1 - j) * 8, 8)
            gr = g_scr[pl.ds(r0, 8), pl.ds(0, S5_N)]
            gi = g_scr[pl.ds(r0, 8), pl.ds(S5_N, S5_N)]
            for lvl, (sh, _) in enumerate(SCAN_SHIFTS):
                sr = pltpu.roll(gr, 8 - sh, 0)
                si = pltpu.roll(gi, 8 - sh, 0)
                lr = tab_ref[lvl]
                li = tab_ref[3 + lvl]
                gr, gi = gr + lr * sr + li * si, gi + lr * si - li * sr
            pvr = tab_ref[6]
            pvi = tab_ref[7]
            gr, gi = gr + pvr * cr + pvi * ci, gi + pvr * ci - pvi * cr
            g_scr[pl.ds(r0, 8), pl.ds(0, S5_N)] = gr
            g_scr[pl.ds(r0, 8), pl.ds(S5_N, S5_N)] = gi
            xpr = xp_ref[pl.ds(r0, 8), pl.ds(0, S5_N)]
            xpi = xp_ref[pl.ds(r0, 8), pl.ds(S5_N, S5_N)]
            dl_ref[0] += gr * xpr + gi * xpi
            dl_ref[1] += gi * xpr - gr * xpi
            return gr[0:1, :], gi[0:1, :]

        cr, ci = lax.fori_loop(0, nch, chunk, (carry[0:1, :], carry[1:2, :]))
        carry[0:1, :] = cr
        carry[1:2, :] = ci

        for kt in range(S5_KT):
            du = jnp.zeros((tm, 256), F32)
            ukt = ub[:, kt * 256:(kt + 1) * 256]
            dykt = dyb[:, kt * 256:(kt + 1) * 256]
            for part in range(2):
                gb = g_scr[:, pl.ds(part * S5_N + kt * 1024, 1024)].astype(BF16)
                xb = x_scr[:, pl.ds(part * S5_N + kt * 1024, 1024)].astype(BF16)
                du = du + _dot(gb, bt_ref[part, kt])
                dbv = _dot_tn(ukt, gb)
                dcv = _dot_tn(xb, dykt)

                @pl.when(first)
                def _():
                    db_ref[part, kt] = dbv
                    dc_ref[part, kt] = dcv

                @pl.when(jnp.logical_not(first))
                def _():
                    db_ref[part, kt] += dbv
                    dc_ref[part, kt] += dcv
            du_ref[:, pl.ds(kt * 256, 256)] = du + d_ref[:, pl.ds(kt * 256, 256)] * dyv[:, kt * 256:(kt + 1) * 256]
        _accum(dd_ref, jnp.sum(dyv * uv, axis=0, keepdims=True), first)

    rev = lambda i: (nt - 1 - i, 0)
    return _carry(
        body, comm, name="s5_bwd", steps=nt,
        out_shape=(jax.ShapeDtypeStruct((t, S5_WIDTH), F32),
                   jax.ShapeDtypeStruct((2, S5_KT, 256, 1024), F32),
                   jax.ShapeDtypeStruct((2, S5_KT, 1024, 256), F32),
                   jax.ShapeDtypeStruct((2, 8, S5_N), F32),
                   jax.ShapeDtypeStruct((1, S5_WIDTH), F32)),
        in_specs=[pl.BlockSpec((tm, S5_WIDTH), rev), pl.BlockSpec((tm, S5_WIDTH), rev),
                  pl.BlockSpec((tm, 2 * S5_N), rev),
                  _const_spec((8, 8, S5_N)),
                  _const_spec((2, S5_KT, 256, 1024)), _const_spec((2, S5_KT, 1024, 256)),
                  _const_spec((2, S5_KT, 256, 1024)), _const_spec((1, S5_WIDTH))],
        out_specs=(pl.BlockSpec((tm, S5_WIDTH), rev),
                   pl.BlockSpec((2, S5_KT, 256, 1024), lambda i: (0, 0, 0, 0)),
                   pl.BlockSpec((2, S5_KT, 1024, 256), lambda i: (0, 0, 0, 0)),
                   pl.BlockSpec((2, 8, S5_N), lambda i: (0, 0, 0)),
                   pl.BlockSpec((1, S5_WIDTH), lambda i: (0, 0))),
        scratch_shapes=[pltpu.VMEM((tm, 2 * S5_N), F32), pltpu.VMEM((tm, 2 * S5_N), F32),
                        pltpu.VMEM((8, S5_N), F32)],
        args=(dy, u, xp, tab, bmat, bmat_t, cmat_t, dvec),
    )


def _hg_gates(z, lb):
    sg = _sigmoid(z)
    sgn = _sigmoid(-z)
    fg = lb + (1.0 - lb) * sg
    return sg, sgn, fg, jnp.log(fg), (1.0 - lb) * sgn


def _hg_decays(g, tril):
    gc = jnp.dot(tril, g, precision=HIGHEST, preferred_element_type=F32)
    mid = gc[CHUNK // 2 - 1:CHUNK // 2, :]
    last = gc[CHUNK - 1:CHUNK, :]
    return jnp.exp(gc), jnp.exp(gc - mid), jnp.exp(mid - gc), jnp.exp(last - gc), jnp.exp(last)


def _split_bf16(x):
    hi = x.astype(BF16)
    return hi, (x - hi.astype(F32)).astype(BF16)


def _hg_scores(qt, qlo, kt, klo, sl, causal):
    a = _dot_nt(qt[:, sl], kt[:, sl]) + _dot_nt(qt[:, sl], klo[:, sl]) + _dot_nt(qlo[:, sl], kt[:, sl])
    return jnp.where(causal, a, 0.0).astype(BF16)


def hgrn_fwd(q, f, v, lb):
    t = q.shape[0]
    nc = t // CHUNK
    scale = HG_E ** -0.5

    def body(q_ref, f_ref, v_ref, lb_ref, o_ref, st_ref, state):
        @pl.when(pl.program_id(0) == 0)
        def _():
            state[...] = jnp.zeros_like(state)

        ri = lax.broadcasted_iota(jnp.int32, (CHUNK, CHUNK), 0)
        ci = lax.broadcasted_iota(jnp.int32, (CHUNK, CHUNK), 1)
        causal = ri >= ci
        tril = causal.astype(F32)
        for sub in range(HG_SUB):
            rows = pl.ds(sub * CHUNK, CHUNK)
            _, _, _, g, k = _hg_gates(f_ref[rows, :], lb_ref[...])
            eg, eq, ek, ed, el = _hg_decays(g, tril)
            qs = q_ref[rows, :] * scale
            qg = (qs * eg).astype(BF16)
            qt, qlo = _split_bf16(qs * eq)
            kt, klo = _split_bf16(k * ek)
            kd = (k * ed).astype(BF16)
            vb = v_ref[rows, :].astype(BF16)
            for h in range(HG_HEADS):
                sl = slice(h * HG_E, (h + 1) * HG_E)
                st = state[h]
                a = _hg_scores(qt, qlo, kt, klo, sl, causal)
                o_ref[rows, sl] = _dot(a, vb[:, sl]) + _dot_nt(qg[:, sl], st.astype(BF16))
                st_new = st * el[:, sl] + _dot_tn(vb[:, sl], kd[:, sl])
                state[h] = st_new
                st_ref[sub, h] = st_new

    tok = pl.BlockSpec((HG_SUB * CHUNK, HG_WIDTH), lambda i: (i, 0))
    return pl.pallas_call(
        body, name="hgrn_fwd", grid=(nc // HG_SUB,),
        out_shape=(jax.ShapeDtypeStruct((t, HG_WIDTH), F32),
                   jax.ShapeDtypeStruct((nc, HG_HEADS, HG_E, HG_E), F32)),
        in_specs=[tok, tok, tok, _const_spec((1, HG_WIDTH))],
        out_specs=(tok, pl.BlockSpec((HG_SUB, HG_HEADS, HG_E, HG_E), lambda i: (i, 0, 0, 0))),
        scratch_shapes=[pltpu.VMEM((HG_HEADS, HG_E, HG_E), F32)],
        compiler_params=_cparams(("arbitrary",)),
    )(q, f, v, lb)


def hgrn_bwd(do, q, f, v, lb, states, comm=None):
    t = q.shape[0]
    nc = t // CHUNK
    scale = HG_E ** -0.5

    ns = nc // HG_SUB

    def body(do_ref, q_ref, f_ref, v_ref, lb_ref, scur_ref, sprev_ref, dq_ref, df_ref, dv_ref, dlb_ref, dstate):
        first = pl.program_id(0) == 0
        has_prev = jnp.where(pl.program_id(0) < ns - 1, 1.0, 0.0)

        @pl.when(first)
        def _():
            dstate[...] = jnp.zeros_like(dstate)

        ri = lax.broadcasted_iota(jnp.int32, (CHUNK, CHUNK), 0)
        ci = lax.broadcasted_iota(jnp.int32, (CHUNK, CHUNK), 1)
        causal = ri >= ci
        tril = causal.astype(F32)
        triu = (ri <= ci).astype(F32)
        rowc = lax.broadcasted_iota(jnp.int32, (CHUNK, HG_WIDTH), 0)
        lb = lb_ref[...]
        dlb = jnp.zeros((1, HG_WIDTH), F32)
        for sub in reversed(range(HG_SUB)):
            rows = pl.ds(sub * CHUNK, CHUNK)
            sg, sgn, fg, g, k = _hg_gates(f_ref[rows, :], lb)
            eg, eq, ek, ed, el = _hg_decays(g, tril)
            qs = q_ref[rows, :] * scale
            qg = (qs * eg).astype(BF16)
            qt, qlo = _split_bf16(qs * eq)
            kt, klo = _split_bf16(k * ek)
            kd = (k * ed).astype(BF16)
            vb = v_ref[rows, :].astype(BF16)
            dob = do_ref[rows, :].astype(BF16)
            dqs_l, dk_l, dgc_l, dgl_l = [], [], [], []
            for h in range(HG_HEADS):
                sl = slice(h * HG_E, (h + 1) * HG_E)
                s0 = scur_ref[sub - 1, h] if sub > 0 else sprev_ref[HG_SUB - 1, h] * has_prev
                ds1 = dstate[h]
                ds1b = ds1.astype(BF16)
                a = _hg_scores(qt, qlo, kt, klo, sl, causal)
                da = jnp.where(causal, _dot_nt(dob[:, sl], vb[:, sl]), 0.0).astype(BF16)
                dv_ref[rows, sl] = _dot_tn(a, dob[:, sl]) + _dot_nt(kd[:, sl], ds1b)
                dkd = _dot(vb[:, sl], ds1b)
                dqt = _dot(da, kt[:, sl])
                dkt = _dot_tn(da, qt[:, sl])
                dqg = _dot(dob[:, sl], s0.astype(BF16))
                dqs_l.append(dqt * eq[:, sl] + dqg * eg[:, sl])
                dk_l.append(dkt * ek[:, sl] + dkd * ed[:, sl])
                kd_dkd = kd[:, sl].astype(F32) * dkd
                dgc_l.append(qt[:, sl].astype(F32) * dqt - kt[:, sl].astype(F32) * dkt
                             + qg[:, sl].astype(F32) * dqg - kd_dkd)
                dgl_l.append(el[:, sl] * jnp.sum(ds1 * s0, axis=0, keepdims=True)
                             + jnp.sum(kd_dkd, axis=0, keepdims=True))
                dstate[h] = ds1 * el[:, sl] + _dot_tn(dob[:, sl], qg[:, sl])
            dqs = jnp.concatenate(dqs_l, axis=1)
            dk = jnp.concatenate(dk_l, axis=1)
            dgl = jnp.concatenate(dgl_l, axis=1)
            dq_ref[rows, :] = dqs * scale
            dgc = jnp.concatenate(dgc_l, axis=1) + jnp.where(rowc == CHUNK - 1, dgl, 0.0)
            dg = jnp.dot(triu, dgc, precision=HIGHEST, preferred_element_type=F32)
            w = dg / fg - dk
            df_ref[rows, :] = w * (1.0 - lb) * sg * sgn
            dlb = dlb + jnp.sum(w * sgn, axis=0, keepdims=True)
        _accum(dlb_ref, dlb, first)

    rev = lambda i: (ns - 1 - i, 0)
    tok = pl.BlockSpec((HG_SUB * CHUNK, HG_WIDTH), rev)
    st_blk = (HG_SUB, HG_HEADS, HG_E, HG_E)
    return _carry(
        body, comm, name="hgrn_bwd", steps=ns,
        out_shape=(jax.ShapeDtypeStruct((t, HG_WIDTH), F32),
                   jax.ShapeDtypeStruct((t, HG_WIDTH), F32),
                   jax.ShapeDtypeStruct((t, HG_WIDTH), F32),
                   jax.ShapeDtypeStruct((1, HG_WIDTH), F32)),
        in_specs=[tok, tok, tok, tok, _const_spec((1, HG_WIDTH)),
                  pl.BlockSpec(st_blk, lambda i: (ns - 1 - i, 0, 0, 0)),
                  pl.BlockSpec(st_blk, lambda i: (jnp.maximum(ns - 2 - i, 0), 0, 0, 0))],
        out_specs=(tok, tok, tok, pl.BlockSpec((1, HG_WIDTH), lambda i: (0, 0))),
        scratch_shapes=[pltpu.VMEM((HG_HEADS, HG_E, HG_E), F32)],
        args=(do, q, f, v, lb, states, states),
    )


GELU_C = math.sqrt(2.0 / math.pi)


def _gelu(x):
    th = jnp.tanh(GELU_C * (x + 0.044715 * x * x * x))
    return 0.5 * x * (1.0 + th), th


def _merge_core(ys5, o, og, ga, gb, wv_ref, wt_ref, ghg, who_ref):
    ys, th = _gelu(ys5)
    ysb = ys.astype(BF16)
    va = jnp.concatenate([_dot(ysb, wv_ref[s]) for s in range(N_SHARD)], axis=1)
    vt = jnp.concatenate([_dot(ysb, wt_ref[s]) for s in range(N_SHARD)], axis=1)
    svt = _sigmoid(vt)
    ya = va * svt
    rs, ons = [], []
    for h in range(HG_HEADS):
        oh = o[:, h * HG_E:(h + 1) * HG_E]
        r = lax.rsqrt(jnp.mean(oh * oh, axis=-1, keepdims=True) + NORM_EPS)
        rs.append(r)
        ons.append(oh * r)
    on = jnp.concatenate(ons, axis=1)
    sgo = _sigmoid(og)
    o2 = on * ghg * (og * sgo)
    o2b = o2.astype(BF16)
    yb = _dot(o2b, who_ref[...])
    sa = _sigmoid(ga)
    sb = _sigmoid(gb)
    mixed = sa * ya + sb * yb
    return dict(ys=ys, th=th, ysb=ysb, va=va, svt=svt, ya=ya, rs=rs, on=on, sgo=sgo, o2b=o2b, yb=yb,
                sa=sa, sb=sb, mixed=mixed)


def merge_fwd(h, ys5, o, og, ga, gb, wv, wt, ghg, who, wmo, tm=256):
    t = h.shape[0]

    def body(h_ref, ys5_ref, o_ref, og_ref, ga_ref, gb_ref, wv_ref, wt_ref, ghg_ref, who_ref, wmo_ref, out_ref):
        c = _merge_core(ys5_ref[...], o_ref[...], og_ref[...], ga_ref[...], gb_ref[...],
                        wv_ref, wt_ref, ghg_ref[...], who_ref)
        out_ref[...] = h_ref[...] + _dot(c["mixed"].astype(BF16), wmo_ref[...])

    tok = pl.BlockSpec((tm, D_MODEL), lambda i: (i, 0))
    return pl.pallas_call(
        body, name="merge_fwd", grid=(t // tm,),
        out_shape=jax.ShapeDtypeStruct((t, D_MODEL), F32),
        in_specs=[tok, pl.BlockSpec((tm, S5_WIDTH), lambda i: (i, 0)), tok, tok, tok, tok,
                  _const_spec((N_SHARD, S5_WIDTH, 256)), _const_spec((N_SHARD, S5_WIDTH, 256)),
                  _const_spec((1, HG_WIDTH)), _const_spec((HG_WIDTH, D_MODEL)), _const_spec((D_MODEL, D_MODEL))],
        out_specs=tok,
        compiler_params=_cparams(("arbitrary",)),
    )(h, ys5, o, og, ga, gb, wv, wt, ghg, who, wmo)


def merge_bwd(dh, ys5, o, og, ga, gb, wv, wt, ghg, who, wmo, comm=None, tm=256):
    t = dh.shape[0]

    def body(dh_ref, ys5_ref, o_ref, og_ref, ga_ref, gb_ref, wv_ref, wt_ref, ghg_ref, who_ref, wmo_ref,
             dys5_ref, do_ref, dog_ref, dga_ref, dgb_ref, dghg_ref,
             mixb_ref, dhb_ref, ysb_ref, dvab_ref, dvtb_ref, o2b_ref, dybb_ref):
        ys5 = ys5_ref[...]
        o = o_ref[...]
        og = og_ref[...]
        ghg = ghg_ref[...]
        c = _merge_core(ys5, o, og, ga_ref[...], gb_ref[...], wv_ref, wt_ref, ghg, who_ref)
        dhb = dh_ref[...].astype(BF16)
        dhb_ref[...] = dhb
        mixb_ref[...] = c["mixed"].astype(BF16)
        ysb_ref[...] = c["ysb"]
        o2b_ref[...] = c["o2b"]
        dmix = _dot_nt(dhb, wmo_ref[...])
        sa, sb = c["sa"], c["sb"]
        dya = dmix * sa
        dyb = dmix * sb
        dga_ref[...] = dmix * c["ya"] * sa * (1.0 - sa)
        dgb_ref[...] = dmix * c["yb"] * sb * (1.0 - sb)
        svt = c["svt"]
        dva = (dya * svt).astype(BF16)
        dvt = (dya * c["va"] * svt * (1.0 - svt)).astype(BF16)
        dvab_ref[...] = dva
        dvtb_ref[...] = dvt
        dys = jnp.zeros((tm, S5_WIDTH), F32)
        for s in range(N_SHARD):
            dys = dys + _dot_nt(dva[:, s * 256:(s + 1) * 256], wv_ref[s]) + _dot_nt(dvt[:, s * 256:(s + 1) * 256], wt_ref[s])
        th = c["th"]
        dgelu = 0.5 * (1.0 + th) + 0.5 * ys5 * (1.0 - th * th) * GELU_C * (1.0 + 3.0 * 0.044715 * ys5 * ys5)
        dys5_ref[...] = dys * dgelu
        dybb = dyb.astype(BF16)
        dybb_ref[...] = dybb
        do2 = _dot_nt(dybb, who_ref[...])
        sgo = c["sgo"]
        sil = og * sgo
        on = c["on"]
        dog_ref[...] = do2 * on * ghg * (sgo * (1.0 + og * (1.0 - sgo)))
        _accum(dghg_ref, jnp.sum(do2 * on * sil, axis=0, keepdims=True), pl.program_id(0) == 0)
        don = do2 * ghg * sil
        dos = []
        for h in range(HG_HEADS):
            sl = slice(h * HG_E, (h + 1) * HG_E)
            m = jnp.mean(don[:, sl] * on[:, sl], axis=-1, keepdims=True)
            dos.append(c["rs"][h] * (don[:, sl] - on[:, sl] * m))
        do_ref[...] = jnp.concatenate(dos, axis=1)

    tok = pl.BlockSpec((tm, D_MODEL), lambda i: (i, 0))
    s5b = pl.BlockSpec((tm, S5_WIDTH), lambda i: (i, 0))
    f32t = jax.ShapeDtypeStruct((t, D_MODEL), F32)
    bft = jax.ShapeDtypeStruct((t, D_MODEL), BF16)
    return _carry(
        body, comm, name="merge_bwd", steps=t // tm,
        out_shape=(jax.ShapeDtypeStruct((t, S5_WIDTH), F32), f32t, f32t, f32t, f32t,
                   jax.ShapeDtypeStruct((1, HG_WIDTH), F32),
                   bft, bft, jax.ShapeDtypeStruct((t, S5_WIDTH), BF16), bft, bft, bft, bft),
        in_specs=[tok, s5b, tok, tok, tok, tok,
                  _const_spec((N_SHARD, S5_WIDTH, 256)), _const_spec((N_SHARD, S5_WIDTH, 256)),
                  _const_spec((1, HG_WIDTH)), _const_spec((HG_WIDTH, D_MODEL)), _const_spec((D_MODEL, D_MODEL))],
        out_specs=(s5b, tok, tok, tok, tok, pl.BlockSpec((1, HG_WIDTH), lambda i: (0, 0)),
                   tok, tok, s5b, tok, tok, tok, tok),
        args=(dh, ys5, o, og, ga, gb, wv, wt, ghg, who, wmo),
    )


def head_fwd_bwd(h, p, tgt, gple, wpg, wpp, gfin, tm=256):
    t = h.shape[0]

    def body(h_ref, p_ref, tgt_ref, gple_ref, wpg_ref, wpp_ref, gfin_ref,
             loss_ref, dh_ref, dgple_ref, dgfin_ref, nb_ref, dlb_ref, dppb_ref):
        first = pl.program_id(0) == 0
        hv = h_ref[...]
        gple = gple_ref[...]
        gfin = gfin_ref[...]
        n, r3 = _rms_fwd(hv, gple)
        nb = n.astype(BF16)
        nb_ref[...] = nb
        pg = _sigmoid(_dot(nb, wpg_ref[...]))
        pb = p_ref[...].astype(BF16)
        pp = jnp.concatenate([_dot(pb, wpp_ref[s]) for s in range(N_SHARD)], axis=1)
        h4 = hv + pg * pp
        y, r4 = _rms_fwd(h4, gfin)
        err = y - tgt_ref[...]
        lsum = 0.5 * jnp.sum(jnp.sum(err * err, axis=-1, keepdims=True), axis=0, keepdims=True) / D_MODEL
        _accum(loss_ref, jnp.broadcast_to(lsum, (8, 128)), first)
        dy = err * (1.0 / D_MODEL)
        dh4, dgf = _rms_bwd(h4, r4, gfin, dy)
        _accum(dgfin_ref, dgf, first)
        dpp = dh4 * pg
        dppb_ref[...] = dpp.astype(BF16)
        dl = (dh4 * pp * pg * (1.0 - pg)).astype(BF16)
        dlb_ref[...] = dl
        dn = _dot_nt(dl, wpg_ref[...])
        dx, dgp = _rms_bwd(hv, r3, gple, dn)
        _accum(dgple_ref, dgp, first)
        dh_ref[...] = dh4 + dx

    tok = pl.BlockSpec((tm, D_MODEL), lambda i: (i, 0))
    vec = pl.BlockSpec((1, D_MODEL), lambda i: (0, 0))
    bft = jax.ShapeDtypeStruct((t, D_MODEL), BF16)
    return pl.pallas_call(
        body, name="head_fwd_bwd", grid=(t // tm,),
        out_shape=(jax.ShapeDtypeStruct((8, 128), F32), jax.ShapeDtypeStruct((t, D_MODEL), F32),
                   jax.ShapeDtypeStruct((1, D_MODEL), F32), jax.ShapeDtypeStruct((1, D_MODEL), F32),
                   bft, bft, bft),
        in_specs=[tok, pl.BlockSpec((tm, PLE_DIM), lambda i: (i, 0)), tok,
                  _const_spec((1, D_MODEL)), _const_spec((D_MODEL, D_MODEL)),
                  _const_spec((N_SHARD, PLE_DIM, 256)), _const_spec((1, D_MODEL))],
        out_specs=(pl.BlockSpec((8, 128), lambda i: (0, 0)), tok, vec, vec, tok, tok, tok),
        compiler_params=_cparams(("arbitrary",)),
    )(h, p, tgt, gple, wpg, wpp, gfin)


BIG = ("ffn1_w_gate", "ffn1_w_up", "ffn1_w_down", "w_in", "s5_glu_val", "s5_glu_gate", "hg_w_out",
       "w_merge_out", "ffn2_w_gate", "ffn2_w_up", "ffn2_w_down", "ple_w_gate", "ple_w_proj")
FFN_T = ("ffn1_w_gate", "ffn1_w_up", "ffn2_w_gate", "ffn2_w_up")
BIG_SHARD = {
    "ffn1_w_gate": (FF_PAD, D_MODEL), "ffn1_w_up": (FF_PAD, D_MODEL), "ffn1_w_down": (FF_PAD, D_MODEL),
    "ffn2_w_gate": (FF_PAD, D_MODEL), "ffn2_w_up": (FF_PAD, D_MODEL), "ffn2_w_down": (FF_PAD, D_MODEL),
    "w_in": (D_MODEL, IN_COLS // N_SHARD), "s5_glu_val": (S5_WIDTH, 256), "s5_glu_gate": (S5_WIDTH, 256),
    "hg_w_out": (256, D_MODEL), "w_merge_out": (256, D_MODEL), "ple_w_gate": (256, D_MODEL),
    "ple_w_proj": (PLE_DIM, 256),
}


def _lower_bound(hb):
    return jax.nn.softmax(hb, axis=0)[0:1]


class Schedule:
    def __init__(self, wts):
        self.wts = dict(wts)
        self.grads = {}

    def before(self, kernel_name):
        return None

    def after(self, kernel_name, results):
        pass

    def grad(self, name, g):
        self.grads[name] = g


def local_step(x, p, tgt, sched, sm):
    wts = sched.wts
    rows_full = lambda w: w.reshape(N_SHARD * w.shape[1], w.shape[2])

    def carried(kernel_name, fn, *args):
        outs, results = fn(*args, comm=sched.before(kernel_name))
        sched.after(kernel_name, results)
        return outs

    def weight_grad(name, xs, ys, shard):
        kernel_name = "g_" + name
        (g,), results = tn_matmul(xs, ys, kernel_name, shard, comm=sched.before(kernel_name))
        sched.grad(name, g)
        sched.after(kernel_name, results)

    lb, lb_vjp = jax.vjp(_lower_bound, sm["hg_lower_bound"])
    s5_names = ("s5_lam_re", "s5_lam_im", "s5_log_dt", "s5_b_re", "s5_b_im", "s5_c_re", "s5_c_im")
    (lam_bar, bmat, cmat), s5_vjp = jax.vjp(s5_prep, *[sm[k] for k in s5_names])
    pw_r, pw_i = _lam_powers(lam_bar)
    bmat_b = bmat.astype(BF16)
    cmat_b = cmat.astype(BF16)
    bmat_t = jnp.swapaxes(bmat, -1, -2).astype(BF16)
    cmat_t = jnp.swapaxes(cmat, -1, -2).astype(BF16)

    h1, a1, b1 = carried("ffn1_fwd", ffn_fwd, x, sm["ffn1_norm"], wts["ffn1_w_gate"], wts["ffn1_w_up"],
                         wts["ffn1_w_down"], "ffn1_fwd")
    s5in, q, f, v, og, ga, gb = inproj_fwd(h1, sm["mix_norm"], wts["w_in"])
    ys5, xp = carried("s5_fwd", s5_fwd, s5in, _scan_tables(pw_r, pw_i, False), bmat_b, cmat_b, sm["s5_d"])
    o, states = hgrn_fwd(q, f, v, lb)
    who = rows_full(wts["hg_w_out"])
    wmo = rows_full(wts["w_merge_out"])
    h2 = merge_fwd(h1, ys5, o, og, ga, gb, wts["s5_glu_val"], wts["s5_glu_gate"], sm["hg_out_norm"], who, wmo)
    (h3, a2, b2), _ = ffn_fwd(h2, sm["ffn2_norm"], wts["ffn2_w_gate"], wts["ffn2_w_up"], wts["ffn2_w_down"], "ffn2_fwd")
    loss, dh3, d_ple_norm, d_final_norm, npb, dlgb, dppb = head_fwd_bwd(
        h3, p, tgt, sm["ple_norm"], rows_full(wts["ple_w_gate"]), wts["ple_w_proj"], sm["final_norm"])

    gs = {"ple_norm": d_ple_norm, "final_norm": d_final_norm}
    weight_grad("ple_w_gate", npb, dlgb, "rows")
    weight_grad("ple_w_proj", p, dppb, "cols")

    (dh2, gs["ffn2_norm"], n2b, dhb2, da2, db2, s2), _ = ffn_bwd(
        dh3, h2, a2, b2, sm["ffn2_norm"], wts["ffn2_w_gate"], wts["ffn2_w_up"], wts["ffn2_w_down"], "ffn2_bwd")
    weight_grad("ffn2_w_gate", da2, n2b, "rows")
    weight_grad("ffn2_w_up", db2, n2b, "rows")
    weight_grad("ffn2_w_down", s2, dhb2, "rows")

    dys5, do, dog, dga, dgb, gs["hg_out_norm"], mixb, dh2b, ysb, dvab, dvtb, o2b, dybb = carried(
        "merge_bwd", merge_bwd,
        dh2, ys5, o, og, ga, gb, wts["s5_glu_val"], wts["s5_glu_gate"], sm["hg_out_norm"], who, wmo)
    weight_grad("w_merge_out", mixb, dh2b, "rows")
    weight_grad("s5_glu_val", ysb, dvab, "cols")
    weight_grad("s5_glu_gate", ysb, dvtb, "cols")
    weight_grad("hg_w_out", o2b, dybb, "rows")

    dq, df, dv, dlb = carried("hgrn_bwd", hgrn_bwd, do, q, f, v, lb, states)
    (gs["hg_lower_bound"],) = lb_vjp(dlb)
    du, dbmat, dcmat, dlam8, gs["s5_d"] = carried(
        "s5_bwd", s5_bwd,
        dys5, s5in, xp, _scan_tables(pw_r, pw_i, True), bmat_b, bmat_t, cmat_t, sm["s5_d"])
    for k, g in zip(s5_names, s5_vjp((jnp.sum(dlam8, axis=1), dbmat, dcmat))):
        gs[k] = g

    dh1, gs["mix_norm"], nmb, dprojb = carried(
        "inproj_bwd", inproj_bwd, dh2, h1, sm["mix_norm"], wts["w_in"], (du, dq, df, dv, dog, dga, dgb))
    weight_grad("w_in", nmb, dprojb, "cols")

    dx, gs["ffn1_norm"], n1b, dhb1, da1, db1, s1 = carried(
        "ffn1_bwd", ffn_bwd,
        dh1, x, a1, b1, sm["ffn1_norm"], wts["ffn1_w_gate"], wts["ffn1_w_up"], wts["ffn1_w_down"], "ffn1_bwd")
    weight_grad("ffn1_w_gate", da1, n1b, "rows")
    weight_grad("ffn1_w_up", db1, n1b, "rows")
    weight_grad("ffn1_w_down", s1, dhb1, "rows")
    return loss, dx, gs


MESH = pl.DeviceIdType.MESH
ANY = pl.BlockSpec(memory_space=pl.ANY)


def _place():
    x, y, c = lax.axis_index("x"), lax.axis_index("y"), lax.axis_index("c")
    return x, y, c


def _remote(src, dst, ssem, rsem, dev):
    return pltpu.make_async_remote_copy(src_ref=src, dst_ref=dst, send_sem=ssem, recv_sem=rsem,
                                        device_id=dev, device_id_type=MESH)


class Comm:
    def __init__(self, bufs, outs, alias, sems, hooks):
        self.bufs, self.outs, self.alias, self.sems, self.hooks = list(bufs), list(outs), alias, list(sems), hooks


def run_comm(comm, name):
    nb, no = len(comm.bufs), len(comm.outs)

    def body(*refs):
        for which in ("first", "mid", "last"):
            if which in comm.hooks:
                comm.hooks[which](refs[:nb], refs[nb:nb + no], refs[nb + no:])

    return pl.pallas_call(
        body, name=name, out_shape=tuple(comm.outs), in_specs=[ANY] * nb, out_specs=tuple([ANY] * no),
        input_output_aliases=dict(comm.alias), scratch_shapes=comm.sems,
    )(*comm.bufs)


PLACE_ROWS = {1024: 256, 704: 352, 512: 256, 256: 256}


def place_shards(shards, padded_rows, comm, name):
    n, nb, no = len(shards), len(comm.bufs), len(comm.outs)
    stage_rows = max(PLACE_ROWS.values())
    stage_cols = max(s.shape[1] for s in shards)

    def body(*refs):
        ins, cb = refs[:n], refs[n:n + nb]
        outs, co = refs[n + nb:2 * n + nb], refs[2 * n + nb:2 * n + nb + no]
        stage_f32, stage_bf16, zeros, sem = refs[2 * n + nb + no:2 * n + nb + no + 4]
        cs = refs[2 * n + nb + no + 4:]
        chip = 2 * lax.axis_index("x") + lax.axis_index("y")
        zeros[...] = jnp.zeros_like(zeros)
        comm.hooks["first"](cb, co, cs)
        for w in range(n):
            if w == n // 2:
                comm.hooks["mid"](cb, co, cs)
            r0, cols = ins[w].shape
            step = PLACE_ROWS[r0]
            src32 = stage_f32.at[pl.ds(0, step), pl.ds(0, cols)]
            dst16 = stage_bf16.at[pl.ds(0, step), pl.ds(0, cols)]
            for row in range(0, r0, step):
                pltpu.sync_copy(ins[w].at[pl.ds(row, step), :], src32)
                dst16[...] = src32[...].astype(BF16)
                pltpu.sync_copy(dst16, outs[w].at[chip, pl.ds(row, step), :])
            pad = outs[w].shape[1] - r0
            if pad:
                cp = pltpu.make_async_copy(zeros.at[pl.ds(0, pad), pl.ds(0, cols)],
                                           outs[w].at[chip, pl.ds(r0, pad), :], sem)
                cp.start()
                cp.wait()
        comm.hooks["last"](cb, co, cs)

    res = pl.pallas_call(
        body, name=name,
        out_shape=tuple(jax.ShapeDtypeStruct((N_SHARD, r, s.shape[1]), BF16) for s, r in zip(shards, padded_rows))
        + tuple(comm.outs),
        in_specs=[ANY] * (n + nb), out_specs=tuple([ANY] * (n + no)),
        input_output_aliases={n + i: n + o for i, o in comm.alias.items()},
        scratch_shapes=[pltpu.VMEM((stage_rows, stage_cols), F32), pltpu.VMEM((stage_rows, stage_cols), BF16),
                        pltpu.VMEM((FF_PAD - FF_SHARD, D_MODEL), BF16), pltpu.SemaphoreType.DMA] + comm.sems,
        compiler_params=pltpu.CompilerParams(vmem_limit_bytes=VMEM_LIMIT),
    )(*shards, *comm.bufs)
    return res[:n], res[n:]


def _carry(body, comm, *, name, steps, out_shape, in_specs, out_specs, args, scratch_shapes=()):
    out_shape, out_specs, scratch_shapes = tuple(out_shape), tuple(out_specs), list(scratch_shapes)
    if comm is None:
        res = pl.pallas_call(body, name=name, grid=(steps,), out_shape=out_shape, in_specs=list(in_specs),
                             out_specs=out_specs, scratch_shapes=scratch_shapes,
                             compiler_params=_cparams(("arbitrary",)))(*args)
        return tuple(res), ()
    n_in, n_out, n_scr = len(args), len(out_shape), len(scratch_shapes)
    nb, no = len(comm.bufs), len(comm.outs)

    def wrapped(*refs):
        ins, cb = refs[:n_in], refs[n_in:n_in + nb]
        o0 = n_in + nb
        outs, co = refs[o0:o0 + n_out], refs[o0 + n_out:o0 + n_out + no]
        s0 = o0 + n_out + no
        scr, cs = refs[s0:s0 + n_scr], refs[s0 + n_scr:]
        step = pl.program_id(0)

        def hook(which, at):
            if which in comm.hooks:
                pl.when(step == at)(lambda: comm.hooks[which](cb, co, cs))

        hook("first", 0)
        hook("mid", steps // 2)
        body(*ins, *outs, *scr)
        hook("last", steps - 1)

    res = pl.pallas_call(
        wrapped, name=name, grid=(steps,), out_shape=out_shape + tuple(comm.outs),
        in_specs=list(in_specs) + [ANY] * nb, out_specs=out_specs + (ANY,) * no,
        scratch_shapes=scratch_shapes + comm.sems,
        input_output_aliases={n_in + i: n_out + o for i, o in comm.alias.items()},
        compiler_params=_cparams(("arbitrary",)),
    )(*args, *comm.bufs)
    return tuple(res[:n_out]), tuple(res[n_out:])


def gather_comm(bufs):
    n = len(bufs)

    def copies(outs, sems):
        s_own, r_own, s_fwd, r_fwd, s_sib, r_sib = sems
        x, y, c = _place()
        me = 2 * x + y
        nbr = ((1 - x, y), (x, 1 - y))
        nbr_id = (2 * (1 - x) + y, 2 * x + (1 - y))
        diag_id = 2 * (1 - x) + (1 - y)
        sib = (x, y, 1 - c)

        def rows(w, q=None):
            r = outs[w].shape[1]
            if q is None:
                return pl.ds(pl.multiple_of(c * (r // 2), 16), r // 2)
            return pl.ds(pl.multiple_of(c * (r // 2) + q * (r // 4), 16), r // 4)

        def own(w, j):
            piece = outs[w].at[me, rows(w)]
            return _remote(piece, piece, s_own.at[w, j], r_own.at[w, j], (nbr[j][0], nbr[j][1], c))

        def from_nbr(w, j):
            piece = outs[w].at[nbr_id[j], rows(w)]
            return _remote(piece, piece, s_own.at[w, j], r_own.at[w, j], (nbr[j][0], nbr[j][1], c))

        def fwd(w, j):
            piece = outs[w].at[nbr_id[j], rows(w, j)]
            return _remote(piece, piece, s_fwd.at[w, j], r_fwd.at[w, j], (nbr[1 - j][0], nbr[1 - j][1], c))

        def from_diag(w, j):
            piece = outs[w].at[diag_id, rows(w, j)]
            return _remote(piece, piece, s_fwd.at[w, j], r_fwd.at[w, j], (nbr[1 - j][0], nbr[1 - j][1], c))

        def to_sib(w, k):
            piece = (outs[w].at[nbr_id[k], rows(w)] if k < 2 else outs[w].at[diag_id, rows(w, k - 2)])
            return _remote(piece, piece, s_sib.at[w, k], r_sib.at[w, k], sib)

        def from_sib(w, k):
            r = outs[w].shape[1]
            if k < 2:
                piece = outs[w].at[nbr_id[k], pl.ds(pl.multiple_of((1 - c) * (r // 2), 16), r // 2)]
            else:
                piece = outs[w].at[diag_id, pl.ds(pl.multiple_of((1 - c) * (r // 2) + (k - 2) * (r // 4), 16), r // 4)]
            return _remote(piece, piece, s_sib.at[w, k], r_sib.at[w, k], sib)

        return own, from_nbr, fwd, from_diag, to_sib, from_sib

    def first(_, outs, sems):
        own = copies(outs, sems)[0]
        for w in range(n):
            own(w, 0).start()
            own(w, 1).start()

    def mid(_, outs, sems):
        _, from_nbr, fwd, _, to_sib, _ = copies(outs, sems)
        for w in range(n):
            for j in range(2):
                from_nbr(w, j).wait_recv()
                fwd(w, j).start()
                to_sib(w, j).start()

    def last(_, outs, sems):
        own, _, fwd, from_diag, to_sib, from_sib = copies(outs, sems)
        for w in range(n):
            for j in range(2):
                from_diag(w, j).wait_recv()
                to_sib(w, 2 + j).start()
        for w in range(n):
            for k in range(4):
                from_sib(w, k).wait_recv()
        for w in range(n):
            for j in range(2):
                own(w, j).wait_send()
                fwd(w, j).wait_send()
            for k in range(4):
                to_sib(w, k).wait_send()

    dma = pltpu.SemaphoreType.DMA
    return Comm(bufs, [jax.ShapeDtypeStruct(b.shape, b.dtype) for b in bufs], {w: w for w in range(n)},
                [dma((n, 2)), dma((n, 2)), dma((n, 2)), dma((n, 2)), dma((n, 4)), dma((n, 4))],
                {"first": first, "mid": mid, "last": last})


def _start_wait(make):
    def first(bufs, outs, sems):
        for cp in make(bufs, outs, sems):
            cp.start()

    def last(bufs, outs, sems):
        for cp in make(bufs, outs, sems):
            cp.wait()

    return {"first": first, "last": last}


def exchange_comm(grads):
    n = len(grads)

    def make(ins, outs, sems):
        x, y, c = _place()
        cps = []
        for w in range(n):
            half = ins[w].shape[1] // 2
            src = ins[w].at[:, pl.ds(pl.multiple_of((1 - c) * half, 8), half), :]
            cps.append(_remote(src, outs[w], sems[0].at[w], sems[1].at[w], (x, y, 1 - c)))
        return cps

    dma = pltpu.SemaphoreType.DMA
    return Comm(grads, [jax.ShapeDtypeStruct((N_SHARD, g.shape[1] // 2, g.shape[2]), g.dtype) for g in grads],
                {}, [dma((n,)), dma((n,))], _start_wait(make))


def scatter_comm(sums):
    n = len(sums)

    def make(ins, outs, sems):
        x, y, c = _place()
        chips = ((1 - x, y), (x, 1 - y), (1 - x, 1 - y))
        return [_remote(ins[w].at[2 * ch[0] + ch[1]], outs[w].at[j], sems[0].at[w, j], sems[1].at[w, j],
                        (ch[0], ch[1], c))
                for w in range(n) for j, ch in enumerate(chips)]

    dma = pltpu.SemaphoreType.DMA
    return Comm(sums, [jax.ShapeDtypeStruct((3,) + s.shape[1:], s.dtype) for s in sums],
                {}, [dma((n, 3)), dma((n, 3))], _start_wait(make))


def join_comm(shards):
    n = len(shards)

    def make(_, outs, sems):
        x, y, c = _place()
        cps = []
        for w in range(n):
            half = outs[w].shape[0] // 2
            mine = outs[w].at[pl.ds(pl.multiple_of(c * half, 8), half), :]
            cps.append(_remote(mine, mine, sems[0].at[w], sems[1].at[w], (x, y, 1 - c)))
        return cps

    dma = pltpu.SemaphoreType.DMA
    return Comm(shards, [jax.ShapeDtypeStruct(s.shape, s.dtype) for s in shards], {w: w for w in range(n)},
                [dma((n,)), dma((n,))], _start_wait(make))


def allreduce_small(vec):
    half = vec.shape[0] // 2

    def body(v_ref, o_ref, pair, chips_buf, s1, r1, s2, r2, s3, r3):
        x, y, c = _place()
        chip = 2 * x + y
        sib = (x, y, 1 - c)
        mine = pl.ds(pl.multiple_of(c * half, 8), half)
        other = pl.ds(pl.multiple_of((1 - c) * half, 8), half)
        to_sib = _remote(v_ref.at[other], pair, s1, r1, sib)
        to_sib.start()
        to_sib.wait()
        chips_buf[chip] = v_ref[mine, :] + pair[...]
        sends = [_remote(chips_buf.at[chip], chips_buf.at[chip], s2.at[j], r2.at[j], (ch[0], ch[1], c))
                 for j, ch in enumerate(((1 - x, y), (x, 1 - y), (1 - x, 1 - y)))]
        for cp in sends:
            cp.start()
        for cp in sends:
            cp.wait()
        o_ref[mine, :] = (chips_buf[0] + chips_buf[1]) + (chips_buf[2] + chips_buf[3])
        back = _remote(o_ref.at[mine], o_ref.at[mine], s3, r3, sib)
        back.start()
        back.wait()

    dma = pltpu.SemaphoreType.DMA
    return pl.pallas_call(
        body, name="allreduce_small",
        out_shape=jax.ShapeDtypeStruct(vec.shape, F32),
        in_specs=[pl.BlockSpec(memory_space=pltpu.VMEM)],
        out_specs=pl.BlockSpec(memory_space=pltpu.VMEM),
        scratch_shapes=[pltpu.VMEM((half, 128), F32), pltpu.VMEM((N_SHARD, half, 128), F32),
                        dma, dma, dma((3,)), dma((3,)), dma, dma],
        compiler_params=pltpu.CompilerParams(vmem_limit_bytes=VMEM_LIMIT),
    )(vec)


ROW_TILE = 128


def add_own_half(place, g, recv, name):
    _, r, cc = g.shape
    half = r // 2
    nb = half // ROW_TILE

    def body(p_ref, g_ref, r_ref, o_ref, ob_ref):
        s = g_ref[...] + r_ref[...]
        ob_ref[...] = s.astype(BF16)

        @pl.when(pl.program_id(1) == p_ref[0])
        def _():
            o_ref[...] = s

    blk = (None, ROW_TILE, cc)
    return pl.pallas_call(
        body, name=name,
        grid_spec=pltpu.PrefetchScalarGridSpec(
            num_scalar_prefetch=1, grid=(nb, N_SHARD),
            in_specs=[pl.BlockSpec(blk, lambda i, s, p_ref: (s, p_ref[1] * nb + i, 0)),
                      pl.BlockSpec(blk, lambda i, s, p_ref: (s, i, 0))],
            out_specs=(pl.BlockSpec((ROW_TILE, cc), lambda i, s, p_ref: (i, 0)),
                       pl.BlockSpec(blk, lambda i, s, p_ref: (s, i, 0)))),
        out_shape=(jax.ShapeDtypeStruct((half, cc), F32),
                   jax.ShapeDtypeStruct((N_SHARD, half, cc), BF16)),
        compiler_params=_cparams(("arbitrary", "arbitrary")),
    )(place, g, recv)


def add_chip_sums(place, own, recv, name):
    half, cc = own.shape
    nb = half // ROW_TILE

    def body(s_ref, o_ref, r_ref, out_ref):
        del s_ref
        acc = o_ref[...] + r_ref[0].astype(F32)
        acc = acc + r_ref[1].astype(F32)
        out_ref[...] = acc + r_ref[2].astype(F32)

    return pl.pallas_call(
        body, name=name,
        grid_spec=pltpu.PrefetchScalarGridSpec(
            num_scalar_prefetch=1, grid=(nb,),
            in_specs=[pl.BlockSpec((ROW_TILE, cc), lambda i, s_ref: (i, 0)),
                      pl.BlockSpec((3, ROW_TILE, cc), lambda i, s_ref: (0, i, 0))],
            out_specs=pl.BlockSpec((ROW_TILE, cc), lambda i, s_ref: (s_ref[1] * nb + i, 0))),
        out_shape=jax.ShapeDtypeStruct((2 * half, cc), F32),
        compiler_params=_cparams(("arbitrary",)),
    )(place, own, recv)


def adamw(w, m, v, g, name, copy_g=False):
    r, cc = w.shape
    tr = next(t for t in (256, 352, r) if r % t == 0)
    bc1 = 1.0 / (1.0 - ADAM_B1 ** ADAM_STEP)
    bc2 = 1.0 / (1.0 - ADAM_B2 ** ADAM_STEP)

    def body(w_ref, m_ref, v_ref, g_ref, d_ref, mo_ref, vo_ref, *go_ref):
        gv = g_ref[...]
        mn = ADAM_B1 * m_ref[...] + (1.0 - ADAM_B1) * gv
        vn = ADAM_B2 * v_ref[...] + (1.0 - ADAM_B2) * (gv * gv)
        mo_ref[...] = mn
        vo_ref[...] = vn
        d_ref[...] = -ADAM_LR * ((mn * bc1) / (jnp.sqrt(vn * bc2) + ADAM_EPS) + ADAM_WD * w_ref[...])
        if copy_g:
            go_ref[0][...] = gv

    blk = pl.BlockSpec((tr, cc), lambda i: (i, 0))
    shp = jax.ShapeDtypeStruct((r, cc), F32)
    nout = 4 if copy_g else 3
    return pl.pallas_call(
        body, name=name, grid=(r // tr,),
        out_shape=(shp,) * nout, in_specs=[blk] * 4, out_specs=(blk,) * nout,
        compiler_params=_cparams(("arbitrary",)),
    )(w, m, v, g)


GATHER_FIRST = ("ffn1_w_gate", "ffn1_w_up", "ffn1_w_down")
GATHER_ON = {"ffn1_fwd": ("w_in", "s5_glu_val", "s5_glu_gate", "hg_w_out", "w_merge_out"),
             "s5_fwd": ("ffn2_w_gate", "ffn2_w_up", "ffn2_w_down", "ple_w_gate", "ple_w_proj")}
REDUCE = ((("ple_w_gate", "ple_w_proj", "ffn2_w_gate", "ffn2_w_up", "ffn2_w_down"), "merge_bwd", "hgrn_bwd"),
          (("w_merge_out", "s5_glu_val", "s5_glu_gate", "hg_w_out"), "s5_bwd", "inproj_bwd"),
          (("w_in",), None, "ffn1_bwd"),
          (("ffn1_w_gate",), "g_ffn1_w_up", "g_ffn1_w_down"),
          (("ffn1_w_up",), "g_ffn1_w_down", None),
          (("ffn1_w_down",), None, None))


def merge_comms(comms):
    if len(comms) == 1:
        return comms[0], [len(comms[0].outs)]
    bufs, outs, sems, alias, spans = [], [], [], {}, []
    for c in comms:
        spans.append((len(bufs), len(bufs) + len(c.bufs), len(outs), len(outs) + len(c.outs),
                      len(sems), len(sems) + len(c.sems)))
        alias.update({len(bufs) + i: len(outs) + o for i, o in c.alias.items()})
        bufs, outs, sems = bufs + c.bufs, outs + c.outs, sems + c.sems

    def hook(which):
        def run(b, o, s):
            for c, (b0, b1, o0, o1, s0, s1) in zip(comms, spans):
                if which in c.hooks:
                    c.hooks[which](b[b0:b1], o[o0:o1], s[s0:s1])
        return run

    hooks = {w: hook(w) for w in ("first", "mid", "last") if any(w in c.hooks for c in comms)}
    return Comm(bufs, outs, alias, sems, hooks), [len(c.outs) for c in comms]


class DistSchedule(Schedule):
    def __init__(self, w_rows, chip, core):
        first = gather_comm([_gather_buffer(k, w_rows[k], chip) for k in GATHER_FIRST])
        later = [k for k in BIG if k not in GATHER_FIRST]
        placed, gathered = place_shards([w_rows[k] for k in later], [BIG_SHARD[k][0] for k in later], first,
                                        "place_shards_gather_ffn1")
        super().__init__(zip(GATHER_FIRST, gathered))
        self.bufs = dict(zip(later, placed))
        self.place = jnp.stack([chip, core])
        self.sums, self.halves = {}, {}

    def _exchange(self, names):
        return exchange_comm([self.grads[k] for k in names])

    def _scatter(self, names):
        return scatter_comm([self.sums[k][1] for k in names])

    def _pair_sums(self, names, recv):
        for k, r in zip(names, recv):
            self.sums[k] = add_own_half(self.place, self.grads[k], r, "pair_sum_" + k)

    def _chip_sums(self, names, recv):
        for k, r in zip(names, recv):
            self.halves[k] = add_chip_sums(self.place, self.sums[k][0], r, "chip_sum_" + k)

    def before(self, kernel_name):
        comms, takers = [], []
        if kernel_name in GATHER_ON:
            names = GATHER_ON[kernel_name]
            comms.append(gather_comm([self.bufs[k] for k in names]))
            takers.append(lambda res, names=names: self.wts.update(zip(names, res)))
        for names, exchange_on, scatter_on in REDUCE:
            if kernel_name == exchange_on:
                comms.append(self._exchange(names))
                takers.append(lambda res, names=names: self._pair_sums(names, res))
            if kernel_name == scatter_on:
                if exchange_on is None:
                    self._pair_sums(names, run_comm(self._exchange(names), "exchange_" + names[0]))
                comms.append(self._scatter(names))
                takers.append(lambda res, names=names: self._chip_sums(names, res))
        if not comms:
            return None
        merged, counts = merge_comms(comms)
        self.pending = (takers, counts)
        return merged

    def after(self, kernel_name, results):
        if not results:
            return
        takers, counts = self.pending
        start = 0
        for take, count in zip(takers, counts):
            take(results[start:start + count])
            start += count

    def finish(self):
        tail = [names for names, _, scatter_on in REDUCE if scatter_on is None]
        alone = [k for names, exchange_on, scatter_on in REDUCE if scatter_on is None and exchange_on is None
                 for k in names]
        self._pair_sums(alone, run_comm(self._exchange(alone), "exchange_tail"))
        tail = [k for names in tail for k in names]
        early = [k for k in BIG if k not in tail]
        both, counts = merge_comms([self._scatter(tail), join_comm([self.halves[k] for k in early])])
        res = run_comm(both, "scatter_tail_join_early")
        self._chip_sums(tail, res[:counts[0]])
        full = dict(zip(early, res[counts[0]:]))
        full.update(zip(tail, run_comm(join_comm([self.halves[k] for k in tail]), "join_tail")))
        return full


SMALL = ("ffn1_norm", "mix_norm", "s5_lam_re", "s5_lam_im", "s5_log_dt", "s5_b_re", "s5_b_im", "s5_c_re",
         "s5_c_im", "s5_d", "hg_lower_bound", "hg_out_norm", "ffn2_norm", "ple_norm", "final_norm")
WEIGHTS = ("ffn1_norm", "ffn1_w_gate", "ffn1_w_up", "ffn1_w_down", "mix_norm", "w_in", "s5_lam_re", "s5_lam_im",
           "s5_log_dt", "s5_b_re", "s5_b_im", "s5_c_re", "s5_c_im", "s5_d", "s5_glu_val", "s5_glu_gate",
           "hg_lower_bound", "hg_out_norm", "hg_w_out", "w_merge_out", "ffn2_norm", "ffn2_w_gate", "ffn2_w_up",
           "ffn2_w_down", "ple_norm", "ple_w_gate", "ple_w_proj", "final_norm")


def _as_rows(name, w):
    return jnp.swapaxes(w[0], 0, 1) if name in FFN_T else w[0]


def _from_rows(name, w):
    return (jnp.swapaxes(w, 0, 1) if name in FFN_T else w)[None]


def _gather_buffer(name, w_rows, chip):
    r, c = BIG_SHARD[name]
    shard = jnp.pad(w_rows.astype(BF16), ((0, r - w_rows.shape[0]), (0, 0)))
    return lax.dynamic_update_slice(jnp.zeros((N_SHARD, r, c), BF16), shard[None], (chip, 0, 0))


def _pack(parts):
    flat = jnp.concatenate([jnp.zeros((128,), F32)] + [a.reshape(-1) for a in parts])
    rows = -(-flat.shape[0] // 2048) * 16
    return jnp.pad(flat, (0, rows * 128 - flat.shape[0])).reshape(rows, 128)


def _unpack(vec, likes):
    flat = vec.reshape(-1)
    out, off = [], 128
    for a in likes:
        out.append(flat[off:off + a.size].reshape(a.shape))
        off += a.size
    return out


def _small_view(name, w):
    if name.startswith("s5_") and name != "s5_d":
        return w[0]
    if name == "final_norm":
        return w.reshape(1, D_MODEL)
    return w


def kernel(x, p, ffn1_norm, ffn1_w_gate, ffn1_w_up, ffn1_w_down, mix_norm, w_in, s5_lam_re, s5_lam_im, s5_log_dt, s5_b_re, s5_b_im, s5_c_re, s5_c_im, s5_d, s5_glu_val, s5_glu_gate, hg_lower_bound, hg_out_norm, hg_w_out, w_merge_out, ffn2_norm, ffn2_w_gate, ffn2_w_up, ffn2_w_down, ple_norm, ple_w_gate, ple_w_proj, final_norm, loss_target, m_ffn1_norm, m_ffn1_w_gate, m_ffn1_w_up, m_ffn1_w_down, m_mix_norm, m_w_in, m_s5_lam_re, m_s5_lam_im, m_s5_log_dt, m_s5_b_re, m_s5_b_im, m_s5_c_re, m_s5_c_im, m_s5_d, m_s5_glu_val, m_s5_glu_gate, m_hg_lower_bound, m_hg_out_norm, m_hg_w_out, m_w_merge_out, m_ffn2_norm, m_ffn2_w_gate, m_ffn2_w_up, m_ffn2_w_down, m_ple_norm, m_ple_w_gate, m_ple_w_proj, m_final_norm, v_ffn1_norm, v_ffn1_w_gate, v_ffn1_w_up, v_ffn1_w_down, v_mix_norm, v_w_in, v_s5_lam_re, v_s5_lam_im, v_s5_log_dt, v_s5_b_re, v_s5_b_im, v_s5_c_re, v_s5_c_im, v_s5_d, v_s5_glu_val, v_s5_glu_gate, v_hg_lower_bound, v_hg_out_norm, v_hg_w_out, v_w_merge_out, v_ffn2_norm, v_ffn2_w_gate, v_ffn2_w_up, v_ffn2_w_down, v_ple_norm, v_ple_w_gate, v_ple_w_proj, v_final_norm):
    given = dict(locals())
    wv = {k: given[k] for k in WEIGHTS}
    mv = {k: given["m_" + k] for k in WEIGHTS}
    vv = {k: given["v_" + k] for k in WEIGHTS}

    core = lax.axis_index("c").astype(jnp.int32)
    chip = (2 * lax.axis_index("x") + lax.axis_index("y")).astype(jnp.int32)
    w_rows = {k: _as_rows(k, wv[k]) for k in BIG}
    sched = DistSchedule(w_rows, chip, core)
    sm = {k: _small_view(k, wv[k]) for k in SMALL}

    loss_blk, dx, gsm = local_step(x[0], p[0, 0], loss_target[0], sched, sm)
    full = sched.finish()

    small_likes = [wv[k] for k in SMALL]
    packed = _pack([gsm[k] for k in SMALL])
    packed = packed.at[0, 0].set(loss_blk[0, 0])
    total = allreduce_small(packed)
    loss = total[0, 0]
    gsmall = dict(zip(SMALL, _unpack(total, small_likes)))

    grads, deltas, new_m, new_v = {}, {}, {}, {}
    for k in BIG:
        padded = full[k].shape != w_rows[k].shape
        res = adamw(w_rows[k], _as_rows(k, mv[k]), _as_rows(k, vv[k]), full[k], "adamw_" + k, copy_g=padded)
        grads[k] = _from_rows(k, res[3] if padded else full[k])
        deltas[k], new_m[k], new_v[k] = (_from_rows(k, a) for a in res[:3])
    sw = _pack([wv[k] for k in SMALL])
    smm = _pack([mv[k] for k in SMALL])
    svv = _pack([vv[k] for k in SMALL])
    sd, smn, svn = adamw(sw, smm, svv, total, "adamw_small")
    for k, d, mn, vn in zip(SMALL, _unpack(sd, small_likes), _unpack(smn, small_likes), _unpack(svn, small_likes)):
        grads[k], deltas[k], new_m[k], new_v[k] = gsmall[k], d, mn, vn

    return (loss, dx[None], *[grads[k] for k in WEIGHTS], *[deltas[k] for k in WEIGHTS],
            *[new_m[k] for k in WEIGHTS], *[new_v[k] for k in WEIGHTS])
```

```python
import math

import jax
import jax.numpy as jnp
from jax import lax
from jax.experimental import pallas as pl
from jax.experimental.pallas import tpu as pltpu

F32 = jnp.float32
BF16 = jnp.bfloat16

D_MODEL = 1024
D_FF = 2816
N_SHARD = 4
FF_SHARD = D_FF // N_SHARD
FF_PAD = 768
NORM_EPS = 1e-6
PLE_DIM = 256

S5_WIDTH = 512
S5_GROUPS = 32
S5_GROUP = 16
S5_STATE = 64
S5_N = S5_GROUPS * S5_STATE
S5_KT = 2

HG_HEADS = 8
HG_E = 128
HG_WIDTH = 1024
CHUNK = 64
HG_SUB = 4
IN_COLS = S5_WIDTH + 4 * HG_WIDTH + 2 * D_MODEL
IN_SPLITS = (0, 512, 1536, 2560, 3584, 4608, 5632, 6656)

ADAM_LR = 0.001
ADAM_B1 = 0.9
ADAM_B2 = 0.999
ADAM_EPS = 1e-08
ADAM_WD = 0.01
ADAM_STEP = 10

VMEM_LIMIT = 60 * 1024 * 1024
HIGHEST = lax.Precision.HIGHEST


def _cparams(sem=None, **kw):
    return pltpu.CompilerParams(dimension_semantics=sem, vmem_limit_bytes=VMEM_LIMIT, **kw)


def _const_spec(shape):
    nd = len(shape)
    return pl.BlockSpec(shape, lambda *_: (0,) * nd, pipeline_mode=pl.Buffered(1))


def _dot(a, b):
    return jnp.dot(a, b, preferred_element_type=F32)


def _dot_nt(a, b):
    return lax.dot_general(a, b, (((1,), (1,)), ((), ())), preferred_element_type=F32)


def _dot_tn(a, b):
    return lax.dot_general(a, b, (((0,), (0,)), ((), ())), preferred_element_type=F32)


def _sigmoid(x):
    return 1.0 / (1.0 + jnp.exp(-x))


def _rms_fwd(x, g):
    r = lax.rsqrt(jnp.mean(x * x, axis=-1, keepdims=True) + NORM_EPS)
    return x * r * g, r


def _rms_bwd(x, r, g, dy):
    xh = x * r
    dyg = dy * g
    m = jnp.mean(dyg * xh, axis=-1, keepdims=True)
    return r * (dyg - xh * m), jnp.sum(dy * xh, axis=0, keepdims=True)


def _accum(ref, val, first):
    @pl.when(first)
    def _():
        ref[...] = val

    @pl.when(jnp.logical_not(first))
    def _():
        ref[...] += val


def ffn_fwd(h, gain, wg, wu, wd, name, comm=None, tm=512):
    t = h.shape[0]

    def body(h_ref, g_ref, wg_ref, wu_ref, wd_ref, o_ref, a_ref, b_ref):
        hv = h_ref[...]
        n, _ = _rms_fwd(hv, g_ref[...])
        nb = n.astype(BF16)
        acc = jnp.zeros((tm, D_MODEL), F32)
        for s in range(N_SHARD):
            a = _dot_nt(nb, wg_ref[s])
            b = _dot_nt(nb, wu_ref[s])
            a_ref[s] = a.astype(BF16)
            b_ref[s] = b.astype(BF16)
            sv = (a * _sigmoid(a) * b).astype(BF16)
            acc = acc + _dot(sv, wd_ref[s])
        o_ref[...] = hv + 0.5 * acc

    return _carry(
        body, comm, name=name, steps=t // tm,
        out_shape=(jax.ShapeDtypeStruct((t, D_MODEL), F32),
                   jax.ShapeDtypeStruct((N_SHARD, t, FF_PAD), BF16),
                   jax.ShapeDtypeStruct((N_SHARD, t, FF_PAD), BF16)),
        in_specs=[pl.BlockSpec((tm, D_MODEL), lambda i: (i, 0)),
                  _const_spec((1, D_MODEL)),
                  _const_spec((N_SHARD, FF_PAD, D_MODEL)),
                  _const_spec((N_SHARD, FF_PAD, D_MODEL)),
                  _const_spec((N_SHARD, FF_PAD, D_MODEL))],
        out_specs=(pl.BlockSpec((tm, D_MODEL), lambda i: (i, 0)),
                   pl.BlockSpec((N_SHARD, tm, FF_PAD), lambda i: (0, i, 0)),
                   pl.BlockSpec((N_SHARD, tm, FF_PAD), lambda i: (0, i, 0))),
        args=(h, gain, wg, wu, wd),
    )


def ffn_bwd(dho, h, a, b, gain, wg, wu, wd, name, comm=None, tm=256):
    t = h.shape[0]

    def body(dho_ref, h_ref, a_ref, b_ref, g_ref, wg_ref, wu_ref, wd_ref,
             dh_ref, dg_ref, nb_ref, dhb_ref, da_ref, db_ref, s_ref):
        hv = h_ref[...]
        g = g_ref[...]
        n, r = _rms_fwd(hv, g)
        nb_ref[...] = n.astype(BF16)
        dhalf = (0.5 * dho_ref[...]).astype(BF16)
        dhb_ref[...] = dhalf
        dn = jnp.zeros((tm, D_MODEL), F32)
        for s in range(N_SHARD):
            av = a_ref[s].astype(F32)
            bv = b_ref[s].astype(F32)
            sg = _sigmoid(av)
            sil = av * sg
            s_ref[s] = (sil * bv).astype(BF16)
            ds = _dot_nt(dhalf, wd_ref[s])
            da = (ds * bv * (sg * (1.0 + av * (1.0 - sg)))).astype(BF16)
            db = (ds * sil).astype(BF16)
            da_ref[s] = da
            db_ref[s] = db
            dn = dn + _dot(da, wg_ref[s]) + _dot(db, wu_ref[s])
        dx, dg = _rms_bwd(hv, r, g, dn)
        dh_ref[...] = dho_ref[...] + dx
        _accum(dg_ref, dg, pl.program_id(0) == 0)

    tok = pl.BlockSpec((tm, D_MODEL), lambda i: (i, 0))
    hid = pl.BlockSpec((N_SHARD, tm, FF_PAD), lambda i: (0, i, 0))
    return _carry(
        body, comm, name=name, steps=t // tm,
        out_shape=(jax.ShapeDtypeStruct((t, D_MODEL), F32),
                   jax.ShapeDtypeStruct((1, D_MODEL), F32),
                   jax.ShapeDtypeStruct((t, D_MODEL), BF16),
                   jax.ShapeDtypeStruct((t, D_MODEL), BF16),
                   jax.ShapeDtypeStruct((N_SHARD, t, FF_PAD), BF16),
                   jax.ShapeDtypeStruct((N_SHARD, t, FF_PAD), BF16),
                   jax.ShapeDtypeStruct((N_SHARD, t, FF_PAD), BF16)),
        in_specs=[tok, tok, hid, hid, _const_spec((1, D_MODEL)),
                  _const_spec((N_SHARD, FF_PAD, D_MODEL)),
                  _const_spec((N_SHARD, FF_PAD, D_MODEL)),
                  _const_spec((N_SHARD, FF_PAD, D_MODEL))],
        out_specs=(tok, pl.BlockSpec((1, D_MODEL), lambda i: (0, 0)), tok, tok, hid, hid, hid),
        args=(dho, h, a, b, gain, wg, wu, wd),
    )


TN_VMEM_BUDGET = 44 * 1024 * 1024


def tn_matmul(x, y, name, shard, comm=None):
    x3, y3 = x.ndim == 3, y.ndim == 3
    t = x.shape[-2]
    m = x.shape[-1] // (N_SHARD if (shard == "rows" and not x3) else 1)
    n = y.shape[-1] // (N_SHARD if (shard == "cols" and not y3) else 1)
    per_token = 2 * (m * x.dtype.itemsize + n * y.dtype.itemsize)
    tk = t
    while tk > 512 and tk * per_token + 2 * m * n * 4 > TN_VMEM_BUDGET:
        tk //= 2
    nk = t // tk

    out_shape = jax.ShapeDtypeStruct((N_SHARD, m, n), F32)
    if nk == 1:
        def whole(x_ref, y_ref, o_ref):
            o_ref[...] = _dot_tn(x_ref[...].astype(BF16), y_ref[...].astype(BF16))

        x_one = (pl.BlockSpec((None, t, m), lambda s: (s, 0, 0)) if x3 else
                 pl.BlockSpec((t, m), (lambda s: (0, s)) if shard == "rows" else (lambda s: (0, 0))))
        y_one = (pl.BlockSpec((None, t, n), lambda s: (s, 0, 0)) if y3 else
                 pl.BlockSpec((t, n), (lambda s: (0, s)) if shard == "cols" else (lambda s: (0, 0))))
        return _carry(whole, comm, name=name, steps=N_SHARD, out_shape=(out_shape,), in_specs=[x_one, y_one],
                      out_specs=(pl.BlockSpec((None, m, n), lambda s: (s, 0, 0)),), args=(x, y))
    assert comm is None

    def body(x_ref, y_ref, o_ref):
        _accum(o_ref, _dot_tn(x_ref[...].astype(BF16), y_ref[...].astype(BF16)), pl.program_id(1) == 0)

    if x3:
        x_spec = pl.BlockSpec((None, tk, m), lambda s, k: (s, k, 0))
    elif shard == "rows":
        x_spec = pl.BlockSpec((tk, m), lambda s, k: (k, s))
    else:
        x_spec = pl.BlockSpec((tk, m), lambda s, k: (k, 0))
    if y3:
        y_spec = pl.BlockSpec((None, tk, n), lambda s, k: (s, k, 0))
    elif shard == "cols":
        y_spec = pl.BlockSpec((tk, n), lambda s, k: (k, s))
    else:
        y_spec = pl.BlockSpec((tk, n), lambda s, k: (k, 0))
    res = pl.pallas_call(
        body, name=name, grid=(N_SHARD, nk),
        out_shape=out_shape,
        in_specs=[x_spec, y_spec],
        out_specs=pl.BlockSpec((None, m, n), lambda s, k: (s, 0, 0)),
        compiler_params=_cparams(("arbitrary", "arbitrary")),
    )(x, y)
    return (res,), ()


def inproj_fwd(h, gain, w_in, comm=None, tm=256):
    t = h.shape[0]
    widths = [IN_SPLITS[j + 1] - IN_SPLITS[j] for j in range(7)]
    sh_cols = IN_COLS // N_SHARD

    def body(h_ref, g_ref, w_ref, *outs):
        n, _ = _rms_fwd(h_ref[...], g_ref[...])
        nb = n.astype(BF16)
        proj = jnp.concatenate([_dot(nb, w_ref[s]) for s in range(N_SHARD)], axis=1)
        for j, o_ref in enumerate(outs):
            o_ref[...] = proj[:, IN_SPLITS[j]:IN_SPLITS[j + 1]]

    return _carry(
        body, comm, name="inproj_fwd", steps=t // tm,
        out_shape=tuple(jax.ShapeDtypeStruct((t, w), F32) for w in widths),
        in_specs=[pl.BlockSpec((tm, D_MODEL), lambda i: (i, 0)),
                  _const_spec((1, D_MODEL)),
                  _const_spec((N_SHARD, D_MODEL, sh_cols))],
        out_specs=tuple(pl.BlockSpec((tm, w), lambda i: (i, 0)) for w in widths),
        args=(h, gain, w_in),
    )


def inproj_bwd(dres, h, gain, w_in, dparts, comm=None, tm=256):
    t = h.shape[0]
    widths = [IN_SPLITS[j + 1] - IN_SPLITS[j] for j in range(7)]
    sh_cols = IN_COLS // N_SHARD

    def body(dres_ref, h_ref, g_ref, w_ref, d0, d1, d2, d3, d4, d5, d6, dh_ref, dg_ref, nb_ref, dp_ref):
        hv = h_ref[...]
        g = g_ref[...]
        n, r = _rms_fwd(hv, g)
        nb_ref[...] = n.astype(BF16)
        dproj = jnp.concatenate([d[...] for d in (d0, d1, d2, d3, d4, d5, d6)], axis=1).astype(BF16)
        dp_ref[...] = dproj
        dn = jnp.zeros((tm, D_MODEL), F32)
        for s in range(N_SHARD):
            dn = dn + _dot_nt(dproj[:, s * sh_cols:(s + 1) * sh_cols], w_ref[s])
        dx, dg = _rms_bwd(hv, r, g, dn)
        dh_ref[...] = dres_ref[...] + dx
        _accum(dg_ref, dg, pl.program_id(0) == 0)

    tok = pl.BlockSpec((tm, D_MODEL), lambda i: (i, 0))
    return _carry(
        body, comm, name="inproj_bwd", steps=t // tm,
        out_shape=(jax.ShapeDtypeStruct((t, D_MODEL), F32),
                   jax.ShapeDtypeStruct((1, D_MODEL), F32),
                   jax.ShapeDtypeStruct((t, D_MODEL), BF16),
                   jax.ShapeDtypeStruct((t, IN_COLS), BF16)),
        in_specs=[tok, tok, _const_spec((1, D_MODEL)), _const_spec((N_SHARD, D_MODEL, sh_cols))]
                 + [pl.BlockSpec((tm, w), lambda i: (i, 0)) for w in widths],
        out_specs=(tok, pl.BlockSpec((1, D_MODEL), lambda i: (0, 0)), tok,
                   pl.BlockSpec((tm, IN_COLS), lambda i: (i, 0))),
        args=(dres, h, gain, w_in, *dparts),
    )


def s5_prep(lam_re, lam_im, log_dt, b_re, b_im, c_re, c_im):
    dt = jnp.exp(log_dt)[:, None]
    mag = jnp.exp(lam_re * dt)
    lbr = mag * jnp.cos(lam_im * dt)
    lbi = mag * jnp.sin(lam_im * dt)
    den = lam_re * lam_re + lam_im * lam_im
    nr, ni = lbr - 1.0, lbi
    kr = (nr * lam_re + ni * lam_im) / den
    ki = (ni * lam_re - nr * lam_im) / den
    bbr = kr[..., None] * b_re - ki[..., None] * b_im
    bbi = kr[..., None] * b_im + ki[..., None] * b_re
    eye = jnp.eye(16, dtype=F32)

    def bm(bp):
        return jnp.einsum('kgph,gG->kghGp', bp.reshape(S5_KT, 16, S5_STATE, S5_GROUP), eye).reshape(S5_KT, 256, 1024)

    def cm(cp):
        return jnp.einsum('kghp,gG->kgpGh', cp.reshape(S5_KT, 16, S5_GROUP, S5_STATE), eye).reshape(S5_KT, 1024, 256)

    lam_bar = jnp.stack([lbr.reshape(S5_N), lbi.reshape(S5_N)])
    bmat = jnp.stack([bm(bbr), bm(bbi)])
    cmat = jnp.stack([cm(c_re), -cm(c_im)])
    return lam_bar, bmat, cmat


def _lam_powers(lam_bar):
    lr, li = lam_bar[0], lam_bar[1]
    pr, pi = [lr], [li]
    for _ in range(7):
        pr, pi = pr + [pr[-1] * lr - pi[-1] * li], pi + [pr[-1] * li + pi[-1] * lr]
    return jnp.stack(pr), jnp.stack(pi)


SCAN_SHIFTS = ((1, 0), (2, 1), (4, 3))


def _scan_tables(pw_r, pw_i, reverse):
    rows = jnp.arange(8)[:, None]
    planes_r, planes_i = [], []
    for sh, idx in SCAN_SHIFTS:
        keep = (rows < 8 - sh) if reverse else (rows >= sh)
        planes_r.append(jnp.where(keep, pw_r[idx:idx + 1], 0.0))
        planes_i.append(jnp.where(keep, pw_i[idx:idx + 1], 0.0))
    carry = [pw_r[::-1], pw_i[::-1]] if reverse else [pw_r, pw_i]
    return jnp.stack(planes_r + planes_i + carry)


def s5_fwd(u, tab, bmat, cmat, dvec, comm=None, tm=256):
    t = u.shape[0]
    nch = tm // 8

    def body(u_ref, tab_ref, b_ref, c_ref, d_ref, y_ref, xp_ref, x_scr, carry):
        @pl.when(pl.program_id(0) == 0)
        def _():
            carry[...] = jnp.zeros_like(carry)

        uv = u_ref[...]
        ub = uv.astype(BF16)
        for part in range(2):
            for kt in range(S5_KT):
                x_scr[:, pl.ds(part * S5_N + kt * 1024, 1024)] = _dot(ub[:, kt * 256:(kt + 1) * 256], b_ref[part, kt])
        row = lax.broadcasted_iota(jnp.int32, (8, S5_N), 0)

        def chunk(i, c):
            cr, ci = c
            r0 = pl.multiple_of(i * 8, 8)
            xr = x_scr[pl.ds(r0, 8), pl.ds(0, S5_N)]
            xi = x_scr[pl.ds(r0, 8), pl.ds(S5_N, S5_N)]
            for lvl, (sh, _) in enumerate(SCAN_SHIFTS):
                sr = pltpu.roll(xr, sh, 0)
                si = pltpu.roll(xi, sh, 0)
                lr = tab_ref[lvl]
                li = tab_ref[3 + lvl]
                xr, xi = xr + lr * sr - li * si, xi + lr * si + li * sr
            pwr = tab_ref[6]
            pwi = tab_ref[7]
            xr, xi = xr + pwr * cr - pwi * ci, xi + pwr * ci + pwi * cr
            x_scr[pl.ds(r0, 8), pl.ds(0, S5_N)] = xr
            x_scr[pl.ds(r0, 8), pl.ds(S5_N, S5_N)] = xi
            xp_ref[pl.ds(r0, 8), pl.ds(0, S5_N)] = jnp.where(row == 0, cr, pltpu.roll(xr, 1, 0))
            xp_ref[pl.ds(r0, 8), pl.ds(S5_N, S5_N)] = jnp.where(row == 0, ci, pltpu.roll(xi, 1, 0))
            return xr[7:8, :], xi[7:8, :]

        cr, ci = lax.fori_loop(0, nch, chunk, (carry[0:1, :], carry[1:2, :]))
        carry[0:1, :] = cr
        carry[1:2, :] = ci
        for kt in range(S5_KT):
            acc = jnp.zeros((tm, 256), F32)
            for part in range(2):
                acc = acc + _dot(x_scr[:, pl.ds(part * S5_N + kt * 1024, 1024)].astype(BF16), c_ref[part, kt])
            y_ref[:, pl.ds(kt * 256, 256)] = acc + d_ref[:, pl.ds(kt * 256, 256)] * uv[:, kt * 256:(kt + 1) * 256]

    return _carry(
        body, comm, name="s5_fwd", steps=t // tm,
        out_shape=(jax.ShapeDtypeStruct((t, S5_WIDTH), F32),
                   jax.ShapeDtypeStruct((t, 2 * S5_N), F32)),
        in_specs=[pl.BlockSpec((tm, S5_WIDTH), lambda i: (i, 0)),
                  _const_spec((8, 8, S5_N)),
                  _const_spec((2, S5_KT, 256, 1024)), _const_spec((2, S5_KT, 1024, 256)),
                  _const_spec((1, S5_WIDTH))],
        out_specs=(pl.BlockSpec((tm, S5_WIDTH), lambda i: (i, 0)),
                   pl.BlockSpec((tm, 2 * S5_N), lambda i: (i, 0))),
        scratch_shapes=[pltpu.VMEM((tm, 2 * S5_N), F32), pltpu.VMEM((8, S5_N), F32)],
        args=(u, tab, bmat, cmat, dvec),
    )


def s5_bwd(dy, u, xp, tab, bmat, bmat_t, cmat_t, dvec, comm=None, tm=256):
    t = u.shape[0]
    nt = t // tm
    nch = tm // 8

    def body(dy_ref, u_ref, xp_ref, tab_ref, b_ref, bt_ref, ct_ref, d_ref,
             du_ref, db_ref, dc_ref, dl_ref, dd_ref, g_scr, x_scr, carry):
        first = pl.program_id(0) == 0

        @pl.when(first)
        def _():
            carry[...] = jnp.zeros_like(carry)
            dl_ref[...] = jnp.zeros_like(dl_ref)

        dyv = dy_ref[...]
        uv = u_ref[...]
        dyb = dyv.astype(BF16)
        ub = uv.astype(BF16)
        lr1 = tab_ref[6, 7:8, :]
        li1 = tab_ref[7, 7:8, :]
        for kt in range(S5_KT):
            cols = pl.ds(kt * 1024, 1024)
            colsi = pl.ds(S5_N + kt * 1024, 1024)
            g_scr[:, cols] = _dot(dyb[:, kt * 256:(kt + 1) * 256], ct_ref[0, kt])
            g_scr[:, colsi] = _dot(dyb[:, kt * 256:(kt + 1) * 256], ct_ref[1, kt])
            bur = _dot(ub[:, kt * 256:(kt + 1) * 256], b_ref[0, kt])
            bui = _dot(ub[:, kt * 256:(kt + 1) * 256], b_ref[1, kt])
            xpr = xp_ref[:, cols]
            xpi = xp_ref[:, colsi]
            lrk = lr1[:, kt * 1024:(kt + 1) * 1024]
            lik = li1[:, kt * 1024:(kt + 1) * 1024]
            x_scr[:, cols] = lrk * xpr - lik * xpi + bur
            x_scr[:, colsi] = lrk * xpi + lik * xpr + bui

        def chunk(j, c):
            cr, ci = c
            r0 = pl.multiple_of((nch - 1 - j) * 8, 8)
            gr = g_scr[pl.ds(r0, 8), pl.ds(0, S5_N)]
            gi = g_scr[pl.ds(r0, 8), pl.ds(S5_N, S5_N)]
            for lvl, (sh, _) in enumerate(SCAN_SHIFTS):
                sr = pltpu.roll(gr, 8 - sh, 0)
                si = pltpu.roll(gi, 8 - sh, 0)
                lr = tab_ref[lvl]
                li = tab_ref[3 + lvl]
                gr, gi = gr + lr * sr + li * si, gi + lr * si - li * sr
            pvr = tab_ref[6]
            pvi = tab_ref[7]
            gr, gi = gr + pvr * cr + pvi * ci, gi + pvr * ci - pvi * cr
            g_scr[pl.ds(r0, 8), pl.ds(0, S5_N)] = gr
            g_scr[pl.ds(r0, 8), pl.ds(S5_N, S5_N)] = gi
            xpr = xp_ref[pl.ds(r0, 8), pl.ds(0, S5_N)]
            xpi = xp_ref[pl.ds(r0, 8), pl.ds(S5_N, S5_N)]
            dl_ref[0] += gr * xpr + gi * xpi
            dl_ref[1] += gi * xpr - gr * xpi
            return gr[0:1, :], gi[0:1, :]

        cr, ci = lax.fori_loop(0, nch, chunk, (carry[0:1, :], carry[1:2, :]))
        carry[0:1, :] = cr
        carry[1:2, :] = ci

        for kt in range(S5_KT):
            du = jnp.zeros((tm, 256), F32)
            ukt = ub[:, kt * 256:(kt + 1) * 256]
            dykt = dyb[:, kt * 256:(kt + 1) * 256]
            for part in range(2):
                gb = g_scr[:, pl.ds(part * S5_N + kt * 1024, 1024)].astype(BF16)
                xb = x_scr[:, pl.ds(part * S5_N + kt * 1024, 1024)].astype(BF16)
                du = du + _dot(gb, bt_ref[part, kt])
                dbv = _dot_tn(ukt, gb)
                dcv = _dot_tn(xb, dykt)

                @pl.when(first)
                def _():
                    db_ref[part, kt] = dbv
                    dc_ref[part, kt] = dcv

                @pl.when(jnp.logical_not(first))
                def _():
                    db_ref[part, kt] += dbv
                    dc_ref[part, kt] += dcv
            du_ref[:, pl.ds(kt * 256, 256)] = du + d_ref[:, pl.ds(kt * 256, 256)] * dyv[:, kt * 256:(kt + 1) * 256]
        _accum(dd_ref, jnp.sum(dyv * uv, axis=0, keepdims=True), first)

    rev = lambda i: (nt - 1 - i, 0)
    return _carry(
        body, comm, name="s5_bwd", steps=nt,
        out_shape=(jax.ShapeDtypeStruct((t, S5_WIDTH), F32),
                   jax.ShapeDtypeStruct((2, S5_KT, 256, 1024), F32),
                   jax.ShapeDtypeStruct((2, S5_KT, 1024, 256), F32),
                   jax.ShapeDtypeStruct((2, 8, S5_N), F32),
                   jax.ShapeDtypeStruct((1, S5_WIDTH), F32)),
        in_specs=[pl.BlockSpec((tm, S5_WIDTH), rev), pl.BlockSpec((tm, S5_WIDTH), rev),
                  pl.BlockSpec((tm, 2 * S5_N), rev),
                  _const_spec((8, 8, S5_N)),
                  _const_spec((2, S5_KT, 256, 1024)), _const_spec((2, S5_KT, 1024, 256)),
                  _const_spec((2, S5_KT, 256, 1024)), _const_spec((1, S5_WIDTH))],
        out_specs=(pl.BlockSpec((tm, S5_WIDTH), rev),
                   pl.BlockSpec((2, S5_KT, 256, 1024), lambda i: (0, 0, 0, 0)),
                   pl.BlockSpec((2, S5_KT, 1024, 256), lambda i: (0, 0, 0, 0)),
                   pl.BlockSpec((2, 8, S5_N), lambda i: (0, 0, 0)),
                   pl.BlockSpec((1, S5_WIDTH), lambda i: (0, 0))),
        scratch_shapes=[pltpu.VMEM((tm, 2 * S5_N), F32), pltpu.VMEM((tm, 2 * S5_N), F32),
                        pltpu.VMEM((8, S5_N), F32)],
        args=(dy, u, xp, tab, bmat, bmat_t, cmat_t, dvec),
    )


def _hg_gates(z, lb):
    sg = _sigmoid(z)
    sgn = _sigmoid(-z)
    fg = lb + (1.0 - lb) * sg
    return sg, sgn, fg, jnp.log(fg), (1.0 - lb) * sgn


def _hg_decays(g, tril):
    gc = jnp.dot(tril, g, precision=HIGHEST, preferred_element_type=F32)
    mid = gc[CHUNK // 2 - 1:CHUNK // 2, :]
    last = gc[CHUNK - 1:CHUNK, :]
    return jnp.exp(gc), jnp.exp(gc - mid), jnp.exp(mid - gc), jnp.exp(last - gc), jnp.exp(last)


def _split_bf16(x):
    hi = x.astype(BF16)
    return hi, (x - hi.astype(F32)).astype(BF16)


def _hg_scores(qt, qlo, kt, klo, sl, causal):
    a = _dot_nt(qt[:, sl], kt[:, sl]) + _dot_nt(qt[:, sl], klo[:, sl]) + _dot_nt(qlo[:, sl], kt[:, sl])
    return jnp.where(causal, a, 0.0).astype(BF16)


def hgrn_fwd(q, f, v, lb, comm=None):
    t = q.shape[0]
    nc = t // CHUNK
    scale = HG_E ** -0.5

    def body(q_ref, f_ref, v_ref, lb_ref, o_ref, st_ref, state):
        @pl.when(pl.program_id(0) == 0)
        def _():
            state[...] = jnp.zeros_like(state)

        ri = lax.broadcasted_iota(jnp.int32, (CHUNK, CHUNK), 0)
        ci = lax.broadcasted_iota(jnp.int32, (CHUNK, CHUNK), 1)
        causal = ri >= ci
        tril = causal.astype(F32)
        for sub in range(HG_SUB):
            rows = pl.ds(sub * CHUNK, CHUNK)
            _, _, _, g, k = _hg_gates(f_ref[rows, :], lb_ref[...])
            eg, eq, ek, ed, el = _hg_decays(g, tril)
            qs = q_ref[rows, :] * scale
            qg = (qs * eg).astype(BF16)
            qt, qlo = _split_bf16(qs * eq)
            kt, klo = _split_bf16(k * ek)
            kd = (k * ed).astype(BF16)
            vb = v_ref[rows, :].astype(BF16)
            for h in range(HG_HEADS):
                sl = slice(h * HG_E, (h + 1) * HG_E)
                st = state[h]
                a = _hg_scores(qt, qlo, kt, klo, sl, causal)
                o_ref[rows, sl] = _dot(a, vb[:, sl]) + _dot_nt(qg[:, sl], st.astype(BF16))
                st_new = st * el[:, sl] + _dot_tn(vb[:, sl], kd[:, sl])
                state[h] = st_new
                st_ref[sub, h] = st_new

    tok = pl.BlockSpec((HG_SUB * CHUNK, HG_WIDTH), lambda i: (i, 0))
    return _carry(
        body, comm, name="hgrn_fwd", steps=nc // HG_SUB,
        out_shape=(jax.ShapeDtypeStruct((t, HG_WIDTH), F32),
                   jax.ShapeDtypeStruct((nc, HG_HEADS, HG_E, HG_E), F32)),
        in_specs=[tok, tok, tok, _const_spec((1, HG_WIDTH))],
        out_specs=(tok, pl.BlockSpec((HG_SUB, HG_HEADS, HG_E, HG_E), lambda i: (i, 0, 0, 0))),
        scratch_shapes=[pltpu.VMEM((HG_HEADS, HG_E, HG_E), F32)],
        args=(q, f, v, lb),
    )


def hgrn_bwd(do, q, f, v, lb, states, comm=None):
    t = q.shape[0]
    nc = t // CHUNK
    scale = HG_E ** -0.5

    ns = nc // HG_SUB

    def body(do_ref, q_ref, f_ref, v_ref, lb_ref, scur_ref, sprev_ref, dq_ref, df_ref, dv_ref, dlb_ref, dstate):
        first = pl.program_id(0) == 0
        has_prev = jnp.where(pl.program_id(0) < ns - 1, 1.0, 0.0)

        @pl.when(first)
        def _():
            dstate[...] = jnp.zeros_like(dstate)

        ri = lax.broadcasted_iota(jnp.int32, (CHUNK, CHUNK), 0)
        ci = lax.broadcasted_iota(jnp.int32, (CHUNK, CHUNK), 1)
        causal = ri >= ci
        tril = causal.astype(F32)
        triu = (ri <= ci).astype(F32)
        rowc = lax.broadcasted_iota(jnp.int32, (CHUNK, HG_WIDTH), 0)
        lb = lb_ref[...]
        dlb = jnp.zeros((1, HG_WIDTH), F32)
        for sub in reversed(range(HG_SUB)):
            rows = pl.ds(sub * CHUNK, CHUNK)
            sg, sgn, fg, g, k = _hg_gates(f_ref[rows, :], lb)
            eg, eq, ek, ed, el = _hg_decays(g, tril)
            qs = q_ref[rows, :] * scale
            qg = (qs * eg).astype(BF16)
            qt, qlo = _split_bf16(qs * eq)
            kt, klo = _split_bf16(k * ek)
            kd = (k * ed).astype(BF16)
            vb = v_ref[rows, :].astype(BF16)
            dob = do_ref[rows, :].astype(BF16)
            dqs_l, dk_l, dgc_l, dgl_l = [], [], [], []
            for h in range(HG_HEADS):
                sl = slice(h * HG_E, (h + 1) * HG_E)
                s0 = scur_ref[sub - 1, h] if sub > 0 else sprev_ref[HG_SUB - 1, h] * has_prev
                ds1 = dstate[h]
                ds1b = ds1.astype(BF16)
                a = _hg_scores(qt, qlo, kt, klo, sl, causal)
                da = jnp.where(causal, _dot_nt(dob[:, sl], vb[:, sl]), 0.0).astype(BF16)
                dv_ref[rows, sl] = _dot_tn(a, dob[:, sl]) + _dot_nt(kd[:, sl], ds1b)
                dkd = _dot(vb[:, sl], ds1b)
                dqt = _dot(da, kt[:, sl])
                dkt = _dot_tn(da, qt[:, sl])
                dqg = _dot(dob[:, sl], s0.astype(BF16))
                dqs_l.append(dqt * eq[:, sl] + dqg * eg[:, sl])
                dk_l.append(dkt * ek[:, sl] + dkd * ed[:, sl])
                kd_dkd = kd[:, sl].astype(F32) * dkd
                dgc_l.append(qt[:, sl].astype(F32) * dqt - kt[:, sl].astype(F32) * dkt
                             + qg[:, sl].astype(F32) * dqg - kd_dkd)
                dgl_l.append(el[:, sl] * jnp.sum(ds1 * s0, axis=0, keepdims=True)
                             + jnp.sum(kd_dkd, axis=0, keepdims=True))
                dstate[h] = ds1 * el[:, sl] + _dot_tn(dob[:, sl], qg[:, sl])
            dqs = jnp.concatenate(dqs_l, axis=1)
            dk = jnp.concatenate(dk_l, axis=1)
            dgl = jnp.concatenate(dgl_l, axis=1)
            dq_ref[rows, :] = dqs * scale
            dgc = jnp.concatenate(dgc_l, axis=1) + jnp.where(rowc == CHUNK - 1, dgl, 0.0)
            dg = jnp.dot(triu, dgc, precision=HIGHEST, preferred_element_type=F32)
            w = dg / fg - dk
            df_ref[rows, :] = w * (1.0 - lb) * sg * sgn
            dlb = dlb + jnp.sum(w * sgn, axis=0, keepdims=True)
        _accum(dlb_ref, dlb, first)

    rev = lambda i: (ns - 1 - i, 0)
    tok = pl.BlockSpec((HG_SUB * CHUNK, HG_WIDTH), rev)
    st_blk = (HG_SUB, HG_HEADS, HG_E, HG_E)
    return _carry(
        body, comm, name="hgrn_bwd", steps=ns,
        out_shape=(jax.ShapeDtypeStruct((t, HG_WIDTH), F32),
                   jax.ShapeDtypeStruct((t, HG_WIDTH), F32),
                   jax.ShapeDtypeStruct((t, HG_WIDTH), F32),
                   jax.ShapeDtypeStruct((1, HG_WIDTH), F32)),
        in_specs=[tok, tok, tok, tok, _const_spec((1, HG_WIDTH)),
                  pl.BlockSpec(st_blk, lambda i: (ns - 1 - i, 0, 0, 0)),
                  pl.BlockSpec(st_blk, lambda i: (jnp.maximum(ns - 2 - i, 0), 0, 0, 0))],
        out_specs=(tok, tok, tok, pl.BlockSpec((1, HG_WIDTH), lambda i: (0, 0))),
        scratch_shapes=[pltpu.VMEM((HG_HEADS, HG_E, HG_E), F32)],
        args=(do, q, f, v, lb, states, states),
    )


GELU_C = math.sqrt(2.0 / math.pi)


def _gelu(x):
    th = jnp.tanh(GELU_C * (x + 0.044715 * x * x * x))
    return 0.5 * x * (1.0 + th), th


def _merge_core(ys5, o, og, ga, gb, wv_ref, wt_ref, ghg, who_ref):
    ys, th = _gelu(ys5)
    ysb = ys.astype(BF16)
    va = jnp.concatenate([_dot(ysb, wv_ref[s]) for s in range(N_SHARD)], axis=1)
    vt = jnp.concatenate([_dot(ysb, wt_ref[s]) for s in range(N_SHARD)], axis=1)
    svt = _sigmoid(vt)
    ya = va * svt
    rs, ons = [], []
    for h in range(HG_HEADS):
        oh = o[:, h * HG_E:(h + 1) * HG_E]
        r = lax.rsqrt(jnp.mean(oh * oh, axis=-1, keepdims=True) + NORM_EPS)
        rs.append(r)
        ons.append(oh * r)
    on = jnp.concatenate(ons, axis=1)
    sgo = _sigmoid(og)
    o2 = on * ghg * (og * sgo)
    o2b = o2.astype(BF16)
    yb = _dot(o2b, who_ref[...])
    sa = _sigmoid(ga)
    sb = _sigmoid(gb)
    mixed = sa * ya + sb * yb
    return dict(ys=ys, th=th, ysb=ysb, va=va, svt=svt, ya=ya, rs=rs, on=on, sgo=sgo, o2b=o2b, yb=yb,
                sa=sa, sb=sb, mixed=mixed)


def merge_fwd(h, ys5, o, og, ga, gb, wv, wt, ghg, who, wmo, tm=256):
    t = h.shape[0]

    def body(h_ref, ys5_ref, o_ref, og_ref, ga_ref, gb_ref, wv_ref, wt_ref, ghg_ref, who_ref, wmo_ref, out_ref):
        c = _merge_core(ys5_ref[...], o_ref[...], og_ref[...], ga_ref[...], gb_ref[...],
                        wv_ref, wt_ref, ghg_ref[...], who_ref)
        out_ref[...] = h_ref[...] + _dot(c["mixed"].astype(BF16), wmo_ref[...])

    tok = pl.BlockSpec((tm, D_MODEL), lambda i: (i, 0))
    return pl.pallas_call(
        body, name="merge_fwd", grid=(t // tm,),
        out_shape=jax.ShapeDtypeStruct((t, D_MODEL), F32),
        in_specs=[tok, pl.BlockSpec((tm, S5_WIDTH), lambda i: (i, 0)), tok, tok, tok, tok,
                  _const_spec((N_SHARD, S5_WIDTH, 256)), _const_spec((N_SHARD, S5_WIDTH, 256)),
                  _const_spec((1, HG_WIDTH)), _const_spec((HG_WIDTH, D_MODEL)), _const_spec((D_MODEL, D_MODEL))],
        out_specs=tok,
        compiler_params=_cparams(("arbitrary",)),
    )(h, ys5, o, og, ga, gb, wv, wt, ghg, who, wmo)


def merge_bwd(dh, ys5, o, og, ga, gb, wv, wt, ghg, who, wmo, comm=None, tm=256):
    t = dh.shape[0]

    def body(dh_ref, ys5_ref, o_ref, og_ref, ga_ref, gb_ref, wv_ref, wt_ref, ghg_ref, who_ref, wmo_ref,
             dys5_ref, do_ref, dog_ref, dga_ref, dgb_ref, dghg_ref,
             mixb_ref, dhb_ref, ysb_ref, dvab_ref, dvtb_ref, o2b_ref, dybb_ref):
        ys5 = ys5_ref[...]
        o = o_ref[...]
        og = og_ref[...]
        ghg = ghg_ref[...]
        c = _merge_core(ys5, o, og, ga_ref[...], gb_ref[...], wv_ref, wt_ref, ghg, who_ref)
        dhb = dh_ref[...].astype(BF16)
        dhb_ref[...] = dhb
        mixb_ref[...] = c["mixed"].astype(BF16)
        ysb_ref[...] = c["ysb"]
        o2b_ref[...] = c["o2b"]
        dmix = _dot_nt(dhb, wmo_ref[...])
        sa, sb = c["sa"], c["sb"]
        dya = dmix * sa
        dyb = dmix * sb
        dga_ref[...] = dmix * c["ya"] * sa * (1.0 - sa)
        dgb_ref[...] = dmix * c["yb"] * sb * (1.0 - sb)
        svt = c["svt"]
        dva = (dya * svt).astype(BF16)
        dvt = (dya * c["va"] * svt * (1.0 - svt)).astype(BF16)
        dvab_ref[...] = dva
        dvtb_ref[...] = dvt
        dys = jnp.zeros((tm, S5_WIDTH), F32)
        for s in range(N_SHARD):
            dys = dys + _dot_nt(dva[:, s * 256:(s + 1) * 256], wv_ref[s]) + _dot_nt(dvt[:, s * 256:(s + 1) * 256], wt_ref[s])
        th = c["th"]
        dgelu = 0.5 * (1.0 + th) + 0.5 * ys5 * (1.0 - th * th) * GELU_C * (1.0 + 3.0 * 0.044715 * ys5 * ys5)
        dys5_ref[...] = dys * dgelu
        dybb = dyb.astype(BF16)
        dybb_ref[...] = dybb
        do2 = _dot_nt(dybb, who_ref[...])
        sgo = c["sgo"]
        sil = og * sgo
        on = c["on"]
        dog_ref[...] = do2 * on * ghg * (sgo * (1.0 + og * (1.0 - sgo)))
        _accum(dghg_ref, jnp.sum(do2 * on * sil, axis=0, keepdims=True), pl.program_id(0) == 0)
        don = do2 * ghg * sil
        dos = []
        for h in range(HG_HEADS):
            sl = slice(h * HG_E, (h + 1) * HG_E)
            m = jnp.mean(don[:, sl] * on[:, sl], axis=-1, keepdims=True)
            dos.append(c["rs"][h] * (don[:, sl] - on[:, sl] * m))
        do_ref[...] = jnp.concatenate(dos, axis=1)

    tok = pl.BlockSpec((tm, D_MODEL), lambda i: (i, 0))
    s5b = pl.BlockSpec((tm, S5_WIDTH), lambda i: (i, 0))
    f32t = jax.ShapeDtypeStruct((t, D_MODEL), F32)
    bft = jax.ShapeDtypeStruct((t, D_MODEL), BF16)
    return _carry(
        body, comm, name="merge_bwd", steps=t // tm,
        out_shape=(jax.ShapeDtypeStruct((t, S5_WIDTH), F32), f32t, f32t, f32t, f32t,
                   jax.ShapeDtypeStruct((1, HG_WIDTH), F32),
                   bft, bft, jax.ShapeDtypeStruct((t, S5_WIDTH), BF16), bft, bft, bft, bft),
        in_specs=[tok, s5b, tok, tok, tok, tok,
                  _const_spec((N_SHARD, S5_WIDTH, 256)), _const_spec((N_SHARD, S5_WIDTH, 256)),
                  _const_spec((1, HG_WIDTH)), _const_spec((HG_WIDTH, D_MODEL)), _const_spec((D_MODEL, D_MODEL))],
        out_specs=(s5b, tok, tok, tok, tok, pl.BlockSpec((1, HG_WIDTH), lambda i: (0, 0)),
                   tok, tok, s5b, tok, tok, tok, tok),
        args=(dh, ys5, o, og, ga, gb, wv, wt, ghg, who, wmo),
    )


def head_fwd_bwd(h, p, tgt, gple, wpg, wpp, gfin, tm=256):
    t = h.shape[0]

    def body(h_ref, p_ref, tgt_ref, gple_ref, wpg_ref, wpp_ref, gfin_ref,
             loss_ref, dh_ref, dgple_ref, dgfin_ref, nb_ref, dlb_ref, dppb_ref):
        first = pl.program_id(0) == 0
        hv = h_ref[...]
        gple = gple_ref[...]
        gfin = gfin_ref[...]
        n, r3 = _rms_fwd(hv, gple)
        nb = n.astype(BF16)
        nb_ref[...] = nb
        pg = _sigmoid(_dot(nb, wpg_ref[...]))
        pb = p_ref[...].astype(BF16)
        pp = jnp.concatenate([_dot(pb, wpp_ref[s]) for s in range(N_SHARD)], axis=1)
        h4 = hv + pg * pp
        y, r4 = _rms_fwd(h4, gfin)
        err = y - tgt_ref[...]
        lsum = 0.5 * jnp.sum(jnp.sum(err * err, axis=-1, keepdims=True), axis=0, keepdims=True) / D_MODEL
        _accum(loss_ref, jnp.broadcast_to(lsum, (8, 128)), first)
        dy = err * (1.0 / D_MODEL)
        dh4, dgf = _rms_bwd(h4, r4, gfin, dy)
        _accum(dgfin_ref, dgf, first)
        dpp = dh4 * pg
        dppb_ref[...] = dpp.astype(BF16)
        dl = (dh4 * pp * pg * (1.0 - pg)).astype(BF16)
        dlb_ref[...] = dl
        dn = _dot_nt(dl, wpg_ref[...])
        dx, dgp = _rms_bwd(hv, r3, gple, dn)
        _accum(dgple_ref, dgp, first)
        dh_ref[...] = dh4 + dx

    tok = pl.BlockSpec((tm, D_MODEL), lambda i: (i, 0))
    vec = pl.BlockSpec((1, D_MODEL), lambda i: (0, 0))
    bft = jax.ShapeDtypeStruct((t, D_MODEL), BF16)
    return pl.pallas_call(
        body, name="head_fwd_bwd", grid=(t // tm,),
        out_shape=(jax.ShapeDtypeStruct((8, 128), F32), jax.ShapeDtypeStruct((t, D_MODEL), F32),
                   jax.ShapeDtypeStruct((1, D_MODEL), F32), jax.ShapeDtypeStruct((1, D_MODEL), F32),
                   bft, bft, bft),
        in_specs=[tok, pl.BlockSpec((tm, PLE_DIM), lambda i: (i, 0)), tok,
                  _const_spec((1, D_MODEL)), _const_spec((D_MODEL, D_MODEL)),
                  _const_spec((N_SHARD, PLE_DIM, 256)), _const_spec((1, D_MODEL))],
        out_specs=(pl.BlockSpec((8, 128), lambda i: (0, 0)), tok, vec, vec, tok, tok, tok),
        compiler_params=_cparams(("arbitrary",)),
    )(h, p, tgt, gple, wpg, wpp, gfin)


BIG = ("ffn1_w_gate", "ffn1_w_up", "ffn1_w_down", "w_in", "s5_glu_val", "s5_glu_gate", "hg_w_out",
       "w_merge_out", "ffn2_w_gate", "ffn2_w_up", "ffn2_w_down", "ple_w_gate", "ple_w_proj")
FFN_T = ("ffn1_w_gate", "ffn1_w_up", "ffn2_w_gate", "ffn2_w_up")
BIG_SHARD = {
    "ffn1_w_gate": (FF_PAD, D_MODEL), "ffn1_w_up": (FF_PAD, D_MODEL), "ffn1_w_down": (FF_PAD, D_MODEL),
    "ffn2_w_gate": (FF_PAD, D_MODEL), "ffn2_w_up": (FF_PAD, D_MODEL), "ffn2_w_down": (FF_PAD, D_MODEL),
    "w_in": (D_MODEL, IN_COLS // N_SHARD), "s5_glu_val": (S5_WIDTH, 256), "s5_glu_gate": (S5_WIDTH, 256),
    "hg_w_out": (256, D_MODEL), "w_merge_out": (256, D_MODEL), "ple_w_gate": (256, D_MODEL),
    "ple_w_proj": (PLE_DIM, 256),
}


def _lower_bound(hb):
    return jax.nn.softmax(hb, axis=0)[0:1]


class Schedule:
    def __init__(self, wts):
        self.wts = dict(wts)
        self.grads = {}

    def before(self, kernel_name):
        return None

    def after(self, kernel_name, results):
        pass

    def grad(self, name, g):
        self.grads[name] = g


def local_step(x, p, tgt, sched, sm):
    wts = sched.wts
    rows_full = lambda w: w.reshape(N_SHARD * w.shape[1], w.shape[2])

    def carried(kernel_name, fn, *args):
        outs, results = fn(*args, comm=sched.before(kernel_name))
        sched.after(kernel_name, results)
        return outs

    def weight_grad(name, xs, ys, shard):
        kernel_name = "g_" + name
        (g,), results = tn_matmul(xs, ys, kernel_name, shard, comm=sched.before(kernel_name))
        sched.grad(name, g)
        sched.after(kernel_name, results)

    lb, lb_vjp = jax.vjp(_lower_bound, sm["hg_lower_bound"])
    s5_names = ("s5_lam_re", "s5_lam_im", "s5_log_dt", "s5_b_re", "s5_b_im", "s5_c_re", "s5_c_im")
    (lam_bar, bmat, cmat), s5_vjp = jax.vjp(s5_prep, *[sm[k] for k in s5_names])
    pw_r, pw_i = _lam_powers(lam_bar)
    bmat_b = bmat.astype(BF16)
    cmat_b = cmat.astype(BF16)
    bmat_t = jnp.swapaxes(bmat, -1, -2).astype(BF16)
    cmat_t = jnp.swapaxes(cmat, -1, -2).astype(BF16)

    h1, a1, b1 = carried("ffn1_fwd", ffn_fwd, x, sm["ffn1_norm"], wts["ffn1_w_gate"], wts["ffn1_w_up"],
                         wts["ffn1_w_down"], "ffn1_fwd")
    s5in, q, f, v, og, ga, gb = carried("inproj_fwd", inproj_fwd, h1, sm["mix_norm"], wts["w_in"])
    ys5, xp = carried("s5_fwd", s5_fwd, s5in, _scan_tables(pw_r, pw_i, False), bmat_b, cmat_b, sm["s5_d"])
    o, states = carried("hgrn_fwd", hgrn_fwd, q, f, v, lb)
    who = rows_full(wts["hg_w_out"])
    wmo = rows_full(wts["w_merge_out"])
    h2 = merge_fwd(h1, ys5, o, og, ga, gb, wts["s5_glu_val"], wts["s5_glu_gate"], sm["hg_out_norm"], who, wmo)
    (h3, a2, b2), _ = ffn_fwd(h2, sm["ffn2_norm"], wts["ffn2_w_gate"], wts["ffn2_w_up"], wts["ffn2_w_down"], "ffn2_fwd")
    loss, dh3, d_ple_norm, d_final_norm, npb, dlgb, dppb = head_fwd_bwd(
        h3, p, tgt, sm["ple_norm"], rows_full(wts["ple_w_gate"]), wts["ple_w_proj"], sm["final_norm"])

    gs = {"ple_norm": d_ple_norm, "final_norm": d_final_norm}
    weight_grad("ple_w_gate", npb, dlgb, "rows")
    weight_grad("ple_w_proj", p, dppb, "cols")

    (dh2, gs["ffn2_norm"], n2b, dhb2, da2, db2, s2), _ = ffn_bwd(
        dh3, h2, a2, b2, sm["ffn2_norm"], wts["ffn2_w_gate"], wts["ffn2_w_up"], wts["ffn2_w_down"], "ffn2_bwd")
    weight_grad("ffn2_w_gate", da2, n2b, "rows")
    weight_grad("ffn2_w_up", db2, n2b, "rows")
    weight_grad("ffn2_w_down", s2, dhb2, "rows")

    dys5, do, dog, dga, dgb, gs["hg_out_norm"], mixb, dh2b, ysb, dvab, dvtb, o2b, dybb = carried(
        "merge_bwd", merge_bwd,
        dh2, ys5, o, og, ga, gb, wts["s5_glu_val"], wts["s5_glu_gate"], sm["hg_out_norm"], who, wmo)
    weight_grad("w_merge_out", mixb, dh2b, "rows")
    weight_grad("s5_glu_val", ysb, dvab, "cols")
    weight_grad("s5_glu_gate", ysb, dvtb, "cols")
    weight_grad("hg_w_out", o2b, dybb, "rows")

    dq, df, dv, dlb = carried("hgrn_bwd", hgrn_bwd, do, q, f, v, lb, states)
    (gs["hg_lower_bound"],) = lb_vjp(dlb)
    du, dbmat, dcmat, dlam8, gs["s5_d"] = carried(
        "s5_bwd", s5_bwd,
        dys5, s5in, xp, _scan_tables(pw_r, pw_i, True), bmat_b, bmat_t, cmat_t, sm["s5_d"])
    for k, g in zip(s5_names, s5_vjp((jnp.sum(dlam8, axis=1), dbmat, dcmat))):
        gs[k] = g

    dh1, gs["mix_norm"], nmb, dprojb = carried(
        "inproj_bwd", inproj_bwd, dh2, h1, sm["mix_norm"], wts["w_in"], (du, dq, df, dv, dog, dga, dgb))
    weight_grad("w_in", nmb, dprojb, "cols")

    dx, gs["ffn1_norm"], n1b, dhb1, da1, db1, s1 = carried(
        "ffn1_bwd", ffn_bwd,
        dh1, x, a1, b1, sm["ffn1_norm"], wts["ffn1_w_gate"], wts["ffn1_w_up"], wts["ffn1_w_down"], "ffn1_bwd")
    weight_grad("ffn1_w_gate", da1, n1b, "rows")
    weight_grad("ffn1_w_up", db1, n1b, "rows")
    weight_grad("ffn1_w_down", s1, dhb1, "rows")
    return loss, dx, gs


MESH = pl.DeviceIdType.MESH
ANY = pl.BlockSpec(memory_space=pl.ANY)


def _place():
    x, y, c = lax.axis_index("x"), lax.axis_index("y"), lax.axis_index("c")
    return x, y, c


def _remote(src, dst, ssem, rsem, dev):
    return pltpu.make_async_remote_copy(src_ref=src, dst_ref=dst, send_sem=ssem, recv_sem=rsem,
                                        device_id=dev, device_id_type=MESH)


class Comm:
    def __init__(self, bufs, outs, alias, sems, hooks):
        self.bufs, self.outs, self.alias, self.sems, self.hooks = list(bufs), list(outs), alias, list(sems), hooks


def run_comm(comm, name):
    nb, no = len(comm.bufs), len(comm.outs)

    def body(*refs):
        for which in ("first", "mid", "last"):
            if which in comm.hooks:
                comm.hooks[which](refs[:nb], refs[nb:nb + no], refs[nb + no:])

    return pl.pallas_call(
        body, name=name, out_shape=tuple(comm.outs), in_specs=[ANY] * nb, out_specs=tuple([ANY] * no),
        input_output_aliases=dict(comm.alias), scratch_shapes=comm.sems,
    )(*comm.bufs)


PLACE_ROWS = {1024: 256, 704: 352, 512: 256, 256: 256}


def place_shards(shards, padded_rows, comm, name):
    n, nb, no = len(shards), len(comm.bufs), len(comm.outs)
    stage_rows = max(PLACE_ROWS.values())
    stage_cols = max(s.shape[1] for s in shards)

    def body(*refs):
        ins, cb = refs[:n], refs[n:n + nb]
        outs, co = refs[n + nb:2 * n + nb], refs[2 * n + nb:2 * n + nb + no]
        stage_f32, stage_bf16, zeros, sem = refs[2 * n + nb + no:2 * n + nb + no + 4]
        cs = refs[2 * n + nb + no + 4:]
        chip = 2 * lax.axis_index("x") + lax.axis_index("y")
        zeros[...] = jnp.zeros_like(zeros)
        comm.hooks["first"](cb, co, cs)
        for w in range(n):
            if w == n // 2:
                comm.hooks["mid"](cb, co, cs)
            r0, cols = ins[w].shape
            step = PLACE_ROWS[r0]
            src32 = stage_f32.at[pl.ds(0, step), pl.ds(0, cols)]
            dst16 = stage_bf16.at[pl.ds(0, step), pl.ds(0, cols)]
            for row in range(0, r0, step):
                pltpu.sync_copy(ins[w].at[pl.ds(row, step), :], src32)
                dst16[...] = src32[...].astype(BF16)
                pltpu.sync_copy(dst16, outs[w].at[chip, pl.ds(row, step), :])
            pad = outs[w].shape[1] - r0
            if pad:
                cp = pltpu.make_async_copy(zeros.at[pl.ds(0, pad), pl.ds(0, cols)],
                                           outs[w].at[chip, pl.ds(r0, pad), :], sem)
                cp.start()
                cp.wait()
        comm.hooks["last"](cb, co, cs)

    res = pl.pallas_call(
        body, name=name,
        out_shape=tuple(jax.ShapeDtypeStruct((N_SHARD, r, s.shape[1]), BF16) for s, r in zip(shards, padded_rows))
        + tuple(comm.outs),
        in_specs=[ANY] * (n + nb), out_specs=tuple([ANY] * (n + no)),
        input_output_aliases={n + i: n + o for i, o in comm.alias.items()},
        scratch_shapes=[pltpu.VMEM((stage_rows, stage_cols), F32), pltpu.VMEM((stage_rows, stage_cols), BF16),
                        pltpu.VMEM((FF_PAD - FF_SHARD, D_MODEL), BF16), pltpu.SemaphoreType.DMA] + comm.sems,
        compiler_params=pltpu.CompilerParams(vmem_limit_bytes=VMEM_LIMIT),
    )(*shards, *comm.bufs)
    return res[:n], res[n:]


def _carry(body, comm, *, name, steps, out_shape, in_specs, out_specs, args, scratch_shapes=()):
    out_shape, out_specs, scratch_shapes = tuple(out_shape), tuple(out_specs), list(scratch_shapes)
    if comm is None:
        res = pl.pallas_call(body, name=name, grid=(steps,), out_shape=out_shape, in_specs=list(in_specs),
                             out_specs=out_specs, scratch_shapes=scratch_shapes,
                             compiler_params=_cparams(("arbitrary",)))(*args)
        return tuple(res), ()
    n_in, n_out, n_scr = len(args), len(out_shape), len(scratch_shapes)
    nb, no = len(comm.bufs), len(comm.outs)

    def wrapped(*refs):
        ins, cb = refs[:n_in], refs[n_in:n_in + nb]
        o0 = n_in + nb
        outs, co = refs[o0:o0 + n_out], refs[o0 + n_out:o0 + n_out + no]
        s0 = o0 + n_out + no
        scr, cs = refs[s0:s0 + n_scr], refs[s0 + n_scr:]
        step = pl.program_id(0)

        def hook(which, at):
            if which in comm.hooks:
                pl.when(step == at)(lambda: comm.hooks[which](cb, co, cs))

        hook("first", 0)
        hook("mid", steps // 2)
        body(*ins, *outs, *scr)
        hook("last", steps - 1)

    res = pl.pallas_call(
        wrapped, name=name, grid=(steps,), out_shape=out_shape + tuple(comm.outs),
        in_specs=list(in_specs) + [ANY] * nb, out_specs=out_specs + (ANY,) * no,
        scratch_shapes=scratch_shapes + comm.sems,
        input_output_aliases={n_in + i: n_out + o for i, o in comm.alias.items()},
        compiler_params=_cparams(("arbitrary",)),
    )(*args, *comm.bufs)
    return tuple(res[:n_out]), tuple(res[n_out:])


def gather_comm(bufs):
    n = len(bufs)

    def copies(outs, sems):
        s_own, r_own, s_fwd, r_fwd, s_sib, r_sib = sems
        x, y, c = _place()
        me = 2 * x + y
        nbr = ((1 - x, y), (x, 1 - y))
        nbr_id = (2 * (1 - x) + y, 2 * x + (1 - y))
        diag_id = 2 * (1 - x) + (1 - y)
        sib = (x, y, 1 - c)

        def rows(w, q=None):
            r = outs[w].shape[1]
            if q is None:
                return pl.ds(pl.multiple_of(c * (r // 2), 16), r // 2)
            return pl.ds(pl.multiple_of(c * (r // 2) + q * (r // 4), 16), r // 4)

        def own(w, j):
            piece = outs[w].at[me, rows(w)]
            return _remote(piece, piece, s_own.at[w, j], r_own.at[w, j], (nbr[j][0], nbr[j][1], c))

        def from_nbr(w, j):
            piece = outs[w].at[nbr_id[j], rows(w)]
            return _remote(piece, piece, s_own.at[w, j], r_own.at[w, j], (nbr[j][0], nbr[j][1], c))

        def fwd(w, j):
            piece = outs[w].at[nbr_id[j], rows(w, j)]
            return _remote(piece, piece, s_fwd.at[w, j], r_fwd.at[w, j], (nbr[1 - j][0], nbr[1 - j][1], c))

        def from_diag(w, j):
            piece = outs[w].at[diag_id, rows(w, j)]
            return _remote(piece, piece, s_fwd.at[w, j], r_fwd.at[w, j], (nbr[1 - j][0], nbr[1 - j][1], c))

        def to_sib(w, k):
            piece = (outs[w].at[nbr_id[k], rows(w)] if k < 2 else outs[w].at[diag_id, rows(w, k - 2)])
            return _remote(piece, piece, s_sib.at[w, k], r_sib.at[w, k], sib)

        def from_sib(w, k):
            r = outs[w].shape[1]
            if k < 2:
                piece = outs[w].at[nbr_id[k], pl.ds(pl.multiple_of((1 - c) * (r // 2), 16), r // 2)]
            else:
                piece = outs[w].at[diag_id, pl.ds(pl.multiple_of((1 - c) * (r // 2) + (k - 2) * (r // 4), 16), r // 4)]
            return _remote(piece, piece, s_sib.at[w, k], r_sib.at[w, k], sib)

        return own, from_nbr, fwd, from_diag, to_sib, from_sib

    def first(_, outs, sems):
        own = copies(outs, sems)[0]
        for w in range(n):
            own(w, 0).start()
            own(w, 1).start()

    def mid(_, outs, sems):
        _, from_nbr, fwd, _, to_sib, _ = copies(outs, sems)
        for w in range(n):
            for j in range(2):
                from_nbr(w, j).wait_recv()
                fwd(w, j).start()
                to_sib(w, j).start()

    def last(_, outs, sems):
        own, _, fwd, from_diag, to_sib, from_sib = copies(outs, sems)
        for w in range(n):
            for j in range(2):
                from_diag(w, j).wait_recv()
                to_sib(w, 2 + j).start()
        for w in range(n):
            for k in range(4):
                from_sib(w, k).wait_recv()
        for w in range(n):
            for j in range(2):
                own(w, j).wait_send()
                fwd(w, j).wait_send()
            for k in range(4):
                to_sib(w, k).wait_send()

    dma = pltpu.SemaphoreType.DMA
    return Comm(bufs, [jax.ShapeDtypeStruct(b.shape, b.dtype) for b in bufs], {w: w for w in range(n)},
                [dma((n, 2)), dma((n, 2)), dma((n, 2)), dma((n, 2)), dma((n, 4)), dma((n, 4))],
                {"first": first, "mid": mid, "last": last})


def _start_wait(make):
    def first(bufs, outs, sems):
        for cp in make(bufs, outs, sems):
            cp.start()

    def last(bufs, outs, sems):
        for cp in make(bufs, outs, sems):
            cp.wait()

    return {"first": first, "last": last}


def exchange_comm(grads):
    n = len(grads)

    def make(ins, outs, sems):
        x, y, c = _place()
        cps = []
        for w in range(n):
            half = ins[w].shape[1] // 2
            src = ins[w].at[:, pl.ds(pl.multiple_of((1 - c) * half, 8), half), :]
            cps.append(_remote(src, outs[w], sems[0].at[w], sems[1].at[w], (x, y, 1 - c)))
        return cps

    dma = pltpu.SemaphoreType.DMA
    return Comm(grads, [jax.ShapeDtypeStruct((N_SHARD, g.shape[1] // 2, g.shape[2]), g.dtype) for g in grads],
                {}, [dma((n,)), dma((n,))], _start_wait(make))


def scatter_comm(sums):
    n = len(sums)

    def make(ins, outs, sems):
        x, y, c = _place()
        chips = ((1 - x, y), (x, 1 - y), (1 - x, 1 - y))
        return [_remote(ins[w].at[2 * ch[0] + ch[1]], outs[w].at[j], sems[0].at[w, j], sems[1].at[w, j],
                        (ch[0], ch[1], c))
                for w in range(n) for j, ch in enumerate(chips)]

    dma = pltpu.SemaphoreType.DMA
    return Comm(sums, [jax.ShapeDtypeStruct((3,) + s.shape[1:], s.dtype) for s in sums],
                {}, [dma((n, 3)), dma((n, 3))], _start_wait(make))


def join_comm(shards):
    n = len(shards)

    def make(_, outs, sems):
        x, y, c = _place()
        cps = []
        for w in range(n):
            half = outs[w].shape[0] // 2
            mine = outs[w].at[pl.ds(pl.multiple_of(c * half, 8), half), :]
            cps.append(_remote(mine, mine, sems[0].at[w], sems[1].at[w], (x, y, 1 - c)))
        return cps

    dma = pltpu.SemaphoreType.DMA
    return Comm(shards, [jax.ShapeDtypeStruct(s.shape, s.dtype) for s in shards], {w: w for w in range(n)},
                [dma((n,)), dma((n,))], _start_wait(make))


def allreduce_small(vec):
    half = vec.shape[0] // 2

    def body(v_ref, o_ref, pair, chips_buf, s1, r1, s2, r2, s3, r3):
        x, y, c = _place()
        chip = 2 * x + y
        sib = (x, y, 1 - c)
        mine = pl.ds(pl.multiple_of(c * half, 8), half)
        other = pl.ds(pl.multiple_of((1 - c) * half, 8), half)
        to_sib = _remote(v_ref.at[other], pair, s1, r1, sib)
        to_sib.start()
        to_sib.wait()
        chips_buf[chip] = v_ref[mine, :] + pair[...]
        sends = [_remote(chips_buf.at[chip], chips_buf.at[chip], s2.at[j], r2.at[j], (ch[0], ch[1], c))
                 for j, ch in enumerate(((1 - x, y), (x, 1 - y), (1 - x, 1 - y)))]
        for cp in sends:
            cp.start()
        for cp in sends:
            cp.wait()
        o_ref[mine, :] = (chips_buf[0] + chips_buf[1]) + (chips_buf[2] + chips_buf[3])
        back = _remote(o_ref.at[mine], o_ref.at[mine], s3, r3, sib)
        back.start()
        back.wait()

    dma = pltpu.SemaphoreType.DMA
    return pl.pallas_call(
        body, name="allreduce_small",
        out_shape=jax.ShapeDtypeStruct(vec.shape, F32),
        in_specs=[pl.BlockSpec(memory_space=pltpu.VMEM)],
        out_specs=pl.BlockSpec(memory_space=pltpu.VMEM),
        scratch_shapes=[pltpu.VMEM((half, 128), F32), pltpu.VMEM((N_SHARD, half, 128), F32),
                        dma, dma, dma((3,)), dma((3,)), dma, dma],
        compiler_params=pltpu.CompilerParams(vmem_limit_bytes=VMEM_LIMIT),
    )(vec)


ROW_TILE = 128


def add_own_half(place, g, recv, name):
    _, r, cc = g.shape
    half = r // 2
    nb = half // ROW_TILE

    def body(p_ref, g_ref, r_ref, o_ref, ob_ref):
        s = g_ref[...] + r_ref[...]
        ob_ref[...] = s.astype(BF16)

        @pl.when(pl.program_id(1) == p_ref[0])
        def _():
            o_ref[...] = s

    blk = (None, ROW_TILE, cc)
    return pl.pallas_call(
        body, name=name,
        grid_spec=pltpu.PrefetchScalarGridSpec(
            num_scalar_prefetch=1, grid=(nb, N_SHARD),
            in_specs=[pl.BlockSpec(blk, lambda i, s, p_ref: (s, p_ref[1] * nb + i, 0)),
                      pl.BlockSpec(blk, lambda i, s, p_ref: (s, i, 0))],
            out_specs=(pl.BlockSpec((ROW_TILE, cc), lambda i, s, p_ref: (i, 0)),
                       pl.BlockSpec(blk, lambda i, s, p_ref: (s, i, 0)))),
        out_shape=(jax.ShapeDtypeStruct((half, cc), F32),
                   jax.ShapeDtypeStruct((N_SHARD, half, cc), BF16)),
        compiler_params=_cparams(("arbitrary", "arbitrary")),
    )(place, g, recv)


def add_chip_sums(place, own, recv, name):
    half, cc = own.shape
    nb = half // ROW_TILE

    def body(s_ref, o_ref, r_ref, out_ref):
        del s_ref
        acc = o_ref[...] + r_ref[0].astype(F32)
        acc = acc + r_ref[1].astype(F32)
        out_ref[...] = acc + r_ref[2].astype(F32)

    return pl.pallas_call(
        body, name=name,
        grid_spec=pltpu.PrefetchScalarGridSpec(
            num_scalar_prefetch=1, grid=(nb,),
            in_specs=[pl.BlockSpec((ROW_TILE, cc), lambda i, s_ref: (i, 0)),
                      pl.BlockSpec((3, ROW_TILE, cc), lambda i, s_ref: (0, i, 0))],
            out_specs=pl.BlockSpec((ROW_TILE, cc), lambda i, s_ref: (s_ref[1] * nb + i, 0))),
        out_shape=jax.ShapeDtypeStruct((2 * half, cc), F32),
        compiler_params=_cparams(("arbitrary",)),
    )(place, own, recv)


def adamw(w, m, v, g, name, copy_g=False):
    r, cc = w.shape
    tr = next(t for t in (256, 352, r) if r % t == 0)
    bc1 = 1.0 / (1.0 - ADAM_B1 ** ADAM_STEP)
    bc2 = 1.0 / (1.0 - ADAM_B2 ** ADAM_STEP)

    def body(w_ref, m_ref, v_ref, g_ref, d_ref, mo_ref, vo_ref, *go_ref):
        gv = g_ref[...]
        mn = ADAM_B1 * m_ref[...] + (1.0 - ADAM_B1) * gv
        vn = ADAM_B2 * v_ref[...] + (1.0 - ADAM_B2) * (gv * gv)
        mo_ref[...] = mn
        vo_ref[...] = vn
        d_ref[...] = -ADAM_LR * ((mn * bc1) / (jnp.sqrt(vn * bc2) + ADAM_EPS) + ADAM_WD * w_ref[...])
        if copy_g:
            go_ref[0][...] = gv

    blk = pl.BlockSpec((tr, cc), lambda i: (i, 0))
    shp = jax.ShapeDtypeStruct((r, cc), F32)
    nout = 4 if copy_g else 3
    return pl.pallas_call(
        body, name=name, grid=(r // tr,),
        out_shape=(shp,) * nout, in_specs=[blk] * 4, out_specs=(blk,) * nout,
        compiler_params=_cparams(("arbitrary",)),
    )(w, m, v, g)


GATHER_FIRST = ("ffn1_w_gate", "ffn1_w_up", "ffn1_w_down")
GATHER_ON = {"ffn1_fwd": ("w_in",),
             "inproj_fwd": ("s5_glu_val", "s5_glu_gate", "hg_w_out", "w_merge_out"),
             "s5_fwd": ("ffn2_w_gate", "ffn2_w_up"),
             "hgrn_fwd": ("ffn2_w_down", "ple_w_gate", "ple_w_proj")}
REDUCE = ((("ple_w_gate", "ple_w_proj", "ffn2_w_gate", "ffn2_w_up", "ffn2_w_down"), "merge_bwd", "hgrn_bwd"),
          (("w_merge_out", "s5_glu_val", "s5_glu_gate", "hg_w_out"), "s5_bwd", "inproj_bwd"),
          (("w_in",), None, "ffn1_bwd"),
          (("ffn1_w_gate",), "g_ffn1_w_up", "g_ffn1_w_down"),
          (("ffn1_w_up",), "g_ffn1_w_down", None),
          (("ffn1_w_down",), None, None))


def merge_comms(comms):
    if len(comms) == 1:
        return comms[0], [len(comms[0].outs)]
    bufs, outs, sems, alias, spans = [], [], [], {}, []
    for c in comms:
        spans.append((len(bufs), len(bufs) + len(c.bufs), len(outs), len(outs) + len(c.outs),
                      len(sems), len(sems) + len(c.sems)))
        alias.update({len(bufs) + i: len(outs) + o for i, o in c.alias.items()})
        bufs, outs, sems = bufs + c.bufs, outs + c.outs, sems + c.sems

    def hook(which):
        def run(b, o, s):
            for c, (b0, b1, o0, o1, s0, s1) in zip(comms, spans):
                if which in c.hooks:
                    c.hooks[which](b[b0:b1], o[o0:o1], s[s0:s1])
        return run

    hooks = {w: hook(w) for w in ("first", "mid", "last") if any(w in c.hooks for c in comms)}
    return Comm(bufs, outs, alias, sems, hooks), [len(c.outs) for c in comms]


class DistSchedule(Schedule):
    def __init__(self, w_rows, chip, core):
        first = gather_comm([_gather_buffer(k, w_rows[k], chip) for k in GATHER_FIRST])
        later = [k for k in BIG if k not in GATHER_FIRST]
        placed, gathered = place_shards([w_rows[k] for k in later], [BIG_SHARD[k][0] for k in later], first,
                                        "place_shards_gather_ffn1")
        super().__init__(zip(GATHER_FIRST, gathered))
        self.bufs = dict(zip(later, placed))
        self.place = jnp.stack([chip, core])
        self.sums, self.halves = {}, {}

    def _exchange(self, names):
        return exchange_comm([self.grads[k] for k in names])

    def _scatter(self, names):
        return scatter_comm([self.sums[k][1] for k in names])

    def _pair_sums(self, names, recv):
        for k, r in zip(names, recv):
            self.sums[k] = add_own_half(self.place, self.grads[k], r, "pair_sum_" + k)

    def _chip_sums(self, names, recv):
        for k, r in zip(names, recv):
            self.halves[k] = add_chip_sums(self.place, self.sums[k][0], r, "chip_sum_" + k)

    def before(self, kernel_name):
        comms, takers = [], []
        if kernel_name in GATHER_ON:
            names = GATHER_ON[kernel_name]
            comms.append(gather_comm([self.bufs[k] for k in names]))
            takers.append(lambda res, names=names: self.wts.update(zip(names, res)))
        for names, exchange_on, scatter_on in REDUCE:
            if kernel_name == exchange_on:
                comms.append(self._exchange(names))
                takers.append(lambda res, names=names: self._pair_sums(names, res))
            if kernel_name == scatter_on:
                if exchange_on is None:
                    self._pair_sums(names, run_comm(self._exchange(names), "exchange_" + names[0]))
                comms.append(self._scatter(names))
                takers.append(lambda res, names=names: self._chip_sums(names, res))
        if not comms:
            return None
        merged, counts = merge_comms(comms)
        self.pending = (takers, counts)
        return merged

    def after(self, kernel_name, results):
        if not results:
            return
        takers, counts = self.pending
        start = 0
        for take, count in zip(takers, counts):
            take(results[start:start + count])
            start += count

    def finish(self):
        tail = [names for names, _, scatter_on in REDUCE if scatter_on is None]
        alone = [k for names, exchange_on, scatter_on in REDUCE if scatter_on is None and exchange_on is None
                 for k in names]
        self._pair_sums(alone, run_comm(self._exchange(alone), "exchange_tail"))
        tail = [k for names in tail for k in names]
        early = [k for k in BIG if k not in tail]
        both, counts = merge_comms([self._scatter(tail), join_comm([self.halves[k] for k in early])])
        res = run_comm(both, "scatter_tail_join_early")
        self._chip_sums(tail, res[:counts[0]])
        full = dict(zip(early, res[counts[0]:]))
        full.update(zip(tail, run_comm(join_comm([self.halves[k] for k in tail]), "join_tail")))
        return full


SMALL = ("ffn1_norm", "mix_norm", "s5_lam_re", "s5_lam_im", "s5_log_dt", "s5_b_re", "s5_b_im", "s5_c_re",
         "s5_c_im", "s5_d", "hg_lower_bound", "hg_out_norm", "ffn2_norm", "ple_norm", "final_norm")
WEIGHTS = ("ffn1_norm", "ffn1_w_gate", "ffn1_w_up", "ffn1_w_down", "mix_norm", "w_in", "s5_lam_re", "s5_lam_im",
           "s5_log_dt", "s5_b_re", "s5_b_im", "s5_c_re", "s5_c_im", "s5_d", "s5_glu_val", "s5_glu_gate",
           "hg_lower_bound", "hg_out_norm", "hg_w_out", "w_merge_out", "ffn2_norm", "ffn2_w_gate", "ffn2_w_up",
           "ffn2_w_down", "ple_norm", "ple_w_gate", "ple_w_proj", "final_norm")


def _as_rows(name, w):
    return jnp.swapaxes(w[0], 0, 1) if name in FFN_T else w[0]


def _from_rows(name, w):
    return (jnp.swapaxes(w, 0, 1) if name in FFN_T else w)[None]


def _gather_buffer(name, w_rows, chip):
    r, c = BIG_SHARD[name]
    shard = jnp.pad(w_rows.astype(BF16), ((0, r - w_rows.shape[0]), (0, 0)))
    return lax.dynamic_update_slice(jnp.zeros((N_SHARD, r, c), BF16), shard[None], (chip, 0, 0))


def _pack(parts):
    flat = jnp.concatenate([jnp.zeros((128,), F32)] + [a.reshape(-1) for a in parts])
    rows = -(-flat.shape[0] // 2048) * 16
    return jnp.pad(flat, (0, rows * 128 - flat.shape[0])).reshape(rows, 128)


def _unpack(vec, likes):
    flat = vec.reshape(-1)
    out, off = [], 128
    for a in likes:
        out.append(flat[off:off + a.size].reshape(a.shape))
        off += a.size
    return out


def _small_view(name, w):
    if name.startswith("s5_") and name != "s5_d":
        return w[0]
    if name == "final_norm":
        return w.reshape(1, D_MODEL)
    return w


def kernel(x, p, ffn1_norm, ffn1_w_gate, ffn1_w_up, ffn1_w_down, mix_norm, w_in, s5_lam_re, s5_lam_im, s5_log_dt, s5_b_re, s5_b_im, s5_c_re, s5_c_im, s5_d, s5_glu_val, s5_glu_gate, hg_lower_bound, hg_out_norm, hg_w_out, w_merge_out, ffn2_norm, ffn2_w_gate, ffn2_w_up, ffn2_w_down, ple_norm, ple_w_gate, ple_w_proj, final_norm, loss_target, m_ffn1_norm, m_ffn1_w_gate, m_ffn1_w_up, m_ffn1_w_down, m_mix_norm, m_w_in, m_s5_lam_re, m_s5_lam_im, m_s5_log_dt, m_s5_b_re, m_s5_b_im, m_s5_c_re, m_s5_c_im, m_s5_d, m_s5_glu_val, m_s5_glu_gate, m_hg_lower_bound, m_hg_out_norm, m_hg_w_out, m_w_merge_out, m_ffn2_norm, m_ffn2_w_gate, m_ffn2_w_up, m_ffn2_w_down, m_ple_norm, m_ple_w_gate, m_ple_w_proj, m_final_norm, v_ffn1_norm, v_ffn1_w_gate, v_ffn1_w_up, v_ffn1_w_down, v_mix_norm, v_w_in, v_s5_lam_re, v_s5_lam_im, v_s5_log_dt, v_s5_b_re, v_s5_b_im, v_s5_c_re, v_s5_c_im, v_s5_d, v_s5_glu_val, v_s5_glu_gate, v_hg_lower_bound, v_hg_out_norm, v_hg_w_out, v_w_merge_out, v_ffn2_norm, v_ffn2_w_gate, v_ffn2_w_up, v_ffn2_w_down, v_ple_norm, v_ple_w_gate, v_ple_w_proj, v_final_norm):
    given = dict(locals())
    wv = {k: given[k] for k in WEIGHTS}
    mv = {k: given["m_" + k] for k in WEIGHTS}
    vv = {k: given["v_" + k] for k in WEIGHTS}

    core = lax.axis_index("c").astype(jnp.int32)
    chip = (2 * lax.axis_index("x") + lax.axis_index("y")).astype(jnp.int32)
    w_rows = {k: _as_rows(k, wv[k]) for k in BIG}
    sched = DistSchedule(w_rows, chip, core)
    sm = {k: _small_view(k, wv[k]) for k in SMALL}

    loss_blk, dx, gsm = local_step(x[0], p[0, 0], loss_target[0], sched, sm)
    full = sched.finish()

    small_likes = [wv[k] for k in SMALL]
    packed = _pack([gsm[k] for k in SMALL])
    packed = packed.at[0, 0].set(loss_blk[0, 0])
    total = allreduce_small(packed)
    loss = total[0, 0]
    gsmall = dict(zip(SMALL, _unpack(total, small_likes)))

    grads, deltas, new_m, new_v = {}, {}, {}, {}
    for k in BIG:
        padded = full[k].shape != w_rows[k].shape
        res = adamw(w_rows[k], _as_rows(k, mv[k]), _as_rows(k, vv[k]), full[k], "adamw_" + k, copy_g=padded)
        grads[k] = _from_rows(k, res[3] if padded else full[k])
        deltas[k], new_m[k], new_v[k] = (_from_rows(k, a) for a in res[:3])
    sw = _pack([wv[k] for k in SMALL])
    smm = _pack([mv[k] for k in SMALL])
    svv = _pack([vv[k] for k in SMALL])
    sd, smn, svn = adamw(sw, smm, svv, total, "adamw_small")
    for k, d, mn, vn in zip(SMALL, _unpack(sd, small_likes), _unpack(smn, small_likes), _unpack(svn, small_likes)):
        grads[k], deltas[k], new_m[k], new_v[k] = gsmall[k], d, mn, vn

    return (loss, dx[None], *[grads[k] for k in WEIGHTS], *[deltas[k] for k in WEIGHTS],
            *[new_m[k] for k in WEIGHTS], *[new_v[k] for k in WEIGHTS])
```

```python
import math

import jax
import jax.numpy as jnp
from jax import lax
from jax.experimental import pallas as pl
from jax.experimental.pallas import tpu as pltpu

F32 = jnp.float32
BF16 = jnp.bfloat16

D_MODEL = 1024
D_FF = 2816
N_SHARD = 4
FF_SHARD = D_FF // N_SHARD
FF_PAD = 768
NORM_EPS = 1e-6
PLE_DIM = 256

S5_WIDTH = 512
S5_GROUPS = 32
S5_GROUP = 16
S5_STATE = 64
S5_N = S5_GROUPS * S5_STATE
S5_KT = 2

HG_HEADS = 8
HG_E = 128
HG_WIDTH = 1024
CHUNK = 64
HG_SUB = 4
IN_COLS = S5_WIDTH + 4 * HG_WIDTH + 2 * D_MODEL
IN_SPLITS = (0, 512, 1536, 2560, 3584, 4608, 5632, 6656)

ADAM_LR = 0.001
ADAM_B1 = 0.9
ADAM_B2 = 0.999
ADAM_EPS = 1e-08
ADAM_WD = 0.01
ADAM_STEP = 10

VMEM_LIMIT = 60 * 1024 * 1024
HIGHEST = lax.Precision.HIGHEST


def _cparams(sem=None, **kw):
    return pltpu.CompilerParams(dimension_semantics=sem, vmem_limit_bytes=VMEM_LIMIT, **kw)


def _const_spec(shape):
    nd = len(shape)
    return pl.BlockSpec(shape, lambda *_: (0,) * nd, pipeline_mode=pl.Buffered(1))


def _dot(a, b):
    return jnp.dot(a, b, preferred_element_type=F32)


def _dot_nt(a, b):
    return lax.dot_general(a, b, (((1,), (1,)), ((), ())), preferred_element_type=F32)


def _dot_tn(a, b):
    return lax.dot_general(a, b, (((0,), (0,)), ((), ())), preferred_element_type=F32)


def _sigmoid(x):
    return 1.0 / (1.0 + jnp.exp(-x))


def _rms_fwd(x, g):
    r = lax.rsqrt(jnp.mean(x * x, axis=-1, keepdims=True) + NORM_EPS)
    return x * r * g, r


def _rms_bwd(x, r, g, dy):
    xh = x * r
    dyg = dy * g
    m = jnp.mean(dyg * xh, axis=-1, keepdims=True)
    return r * (dyg - xh * m), jnp.sum(dy * xh, axis=0, keepdims=True)


def _accum(ref, val, first):
    @pl.when(first)
    def _():
        ref[...] = val

    @pl.when(jnp.logical_not(first))
    def _():
        ref[...] += val


def ffn_fwd(h, gain, wg, wu, wd, name, comm=None, tm=512):
    t = h.shape[0]

    def body(h_ref, g_ref, wg_ref, wu_ref, wd_ref, o_ref, a_ref, b_ref):
        hv = h_ref[...]
        n, _ = _rms_fwd(hv, g_ref[...])
        nb = n.astype(BF16)
        acc = jnp.zeros((tm, D_MODEL), F32)
        for s in range(N_SHARD):
            a = _dot_nt(nb, wg_ref[s])
            b = _dot_nt(nb, wu_ref[s])
            a_ref[s] = a.astype(BF16)
            b_ref[s] = b.astype(BF16)
            sv = (a * _sigmoid(a) * b).astype(BF16)
            acc = acc + _dot(sv, wd_ref[s])
        o_ref[...] = hv + 0.5 * acc

    return _carry(
        body, comm, name=name, steps=t // tm,
        out_shape=(jax.ShapeDtypeStruct((t, D_MODEL), F32),
                   jax.ShapeDtypeStruct((N_SHARD, t, FF_PAD), BF16),
                   jax.ShapeDtypeStruct((N_SHARD, t, FF_PAD), BF16)),
        in_specs=[pl.BlockSpec((tm, D_MODEL), lambda i: (i, 0)),
                  _const_spec((1, D_MODEL)),
                  _const_spec((N_SHARD, FF_PAD, D_MODEL)),
                  _const_spec((N_SHARD, FF_PAD, D_MODEL)),
                  _const_spec((N_SHARD, FF_PAD, D_MODEL))],
        out_specs=(pl.BlockSpec((tm, D_MODEL), lambda i: (i, 0)),
                   pl.BlockSpec((N_SHARD, tm, FF_PAD), lambda i: (0, i, 0)),
                   pl.BlockSpec((N_SHARD, tm, FF_PAD), lambda i: (0, i, 0))),
        args=(h, gain, wg, wu, wd),
    )


def ffn_bwd(dho, h, a, b, gain, wg, wu, wd, name, comm=None, tm=256):
    t = h.shape[0]

    def body(dho_ref, h_ref, a_ref, b_ref, g_ref, wg_ref, wu_ref, wd_ref,
             dh_ref, dg_ref, nb_ref, dhb_ref, da_ref, db_ref, s_ref):
        hv = h_ref[...]
        g = g_ref[...]
        n, r = _rms_fwd(hv, g)
        nb_ref[...] = n.astype(BF16)
        dhalf = (0.5 * dho_ref[...]).astype(BF16)
        dhb_ref[...] = dhalf
        dn = jnp.zeros((tm, D_MODEL), F32)
        for s in range(N_SHARD):
            av = a_ref[s].astype(F32)
            bv = b_ref[s].astype(F32)
            sg = _sigmoid(av)
            sil = av * sg
            s_ref[s] = (sil * bv).astype(BF16)
            ds = _dot_nt(dhalf, wd_ref[s])
            da = (ds * bv * (sg * (1.0 + av * (1.0 - sg)))).astype(BF16)
            db = (ds * sil).astype(BF16)
            da_ref[s] = da
            db_ref[s] = db
            dn = dn + _dot(da, wg_ref[s]) + _dot(db, wu_ref[s])
        dx, dg = _rms_bwd(hv, r, g, dn)
        dh_ref[...] = dho_ref[...] + dx
        _accum(dg_ref, dg, pl.program_id(0) == 0)

    tok = pl.BlockSpec((tm, D_MODEL), lambda i: (i, 0))
    hid = pl.BlockSpec((N_SHARD, tm, FF_PAD), lambda i: (0, i, 0))
    return _carry(
        body, comm, name=name, steps=t // tm,
        out_shape=(jax.ShapeDtypeStruct((t, D_MODEL), F32),
                   jax.ShapeDtypeStruct((1, D_MODEL), F32),
                   jax.ShapeDtypeStruct((t, D_MODEL), BF16),
                   jax.ShapeDtypeStruct((t, D_MODEL), BF16),
                   jax.ShapeDtypeStruct((N_SHARD, t, FF_PAD), BF16),
                   jax.ShapeDtypeStruct((N_SHARD, t, FF_PAD), BF16),
                   jax.ShapeDtypeStruct((N_SHARD, t, FF_PAD), BF16)),
        in_specs=[tok, tok, hid, hid, _const_spec((1, D_MODEL)),
                  _const_spec((N_SHARD, FF_PAD, D_MODEL)),
                  _const_spec((N_SHARD, FF_PAD, D_MODEL)),
                  _const_spec((N_SHARD, FF_PAD, D_MODEL))],
        out_specs=(tok, pl.BlockSpec((1, D_MODEL), lambda i: (0, 0)), tok, tok, hid, hid, hid),
        args=(dho, h, a, b, gain, wg, wu, wd),
    )


TN_VMEM_BUDGET = 44 * 1024 * 1024


def tn_matmul(x, y, name, shard, comm=None):
    x3, y3 = x.ndim == 3, y.ndim == 3
    t = x.shape[-2]
    m = x.shape[-1] // (N_SHARD if (shard == "rows" and not x3) else 1)
    n = y.shape[-1] // (N_SHARD if (shard == "cols" and not y3) else 1)
    per_token = 2 * (m * x.dtype.itemsize + n * y.dtype.itemsize)
    tk = t
    while tk > 512 and tk * per_token + 2 * m * n * 4 > TN_VMEM_BUDGET:
        tk //= 2
    nk = t // tk

    out_shape = jax.ShapeDtypeStruct((N_SHARD, m, n), F32)
    if nk == 1:
        def whole(x_ref, y_ref, o_ref):
            o_ref[...] = _dot_tn(x_ref[...].astype(BF16), y_ref[...].astype(BF16))

        x_one = (pl.BlockSpec((None, t, m), lambda s: (s, 0, 0)) if x3 else
                 pl.BlockSpec((t, m), (lambda s: (0, s)) if shard == "rows" else (lambda s: (0, 0))))
        y_one = (pl.BlockSpec((None, t, n), lambda s: (s, 0, 0)) if y3 else
                 pl.BlockSpec((t, n), (lambda s: (0, s)) if shard == "cols" else (lambda s: (0, 0))))
        return _carry(whole, comm, name=name, steps=N_SHARD, out_shape=(out_shape,), in_specs=[x_one, y_one],
                      out_specs=(pl.BlockSpec((None, m, n), lambda s: (s, 0, 0)),), args=(x, y))
    assert comm is None

    def body(x_ref, y_ref, o_ref):
        _accum(o_ref, _dot_tn(x_ref[...].astype(BF16), y_ref[...].astype(BF16)), pl.program_id(1) == 0)

    if x3:
        x_spec = pl.BlockSpec((None, tk, m), lambda s, k: (s, k, 0))
    elif shard == "rows":
        x_spec = pl.BlockSpec((tk, m), lambda s, k: (k, s))
    else:
        x_spec = pl.BlockSpec((tk, m), lambda s, k: (k, 0))
    if y3:
        y_spec = pl.BlockSpec((None, tk, n), lambda s, k: (s, k, 0))
    elif shard == "cols":
        y_spec = pl.BlockSpec((tk, n), lambda s, k: (k, s))
    else:
        y_spec = pl.BlockSpec((tk, n), lambda s, k: (k, 0))
    res = pl.pallas_call(
        body, name=name, grid=(N_SHARD, nk),
        out_shape=out_shape,
        in_specs=[x_spec, y_spec],
        out_specs=pl.BlockSpec((None, m, n), lambda s, k: (s, 0, 0)),
        compiler_params=_cparams(("arbitrary", "arbitrary")),
    )(x, y)
    return (res,), ()


def inproj_fwd(h, gain, w_in, comm=None, tm=256):
    t = h.shape[0]
    widths = [IN_SPLITS[j + 1] - IN_SPLITS[j] for j in range(7)]
    sh_cols = IN_COLS // N_SHARD

    def body(h_ref, g_ref, w_ref, *outs):
        n, _ = _rms_fwd(h_ref[...], g_ref[...])
        nb = n.astype(BF16)
        proj = jnp.concatenate([_dot(nb, w_ref[s]) for s in range(N_SHARD)], axis=1)
        for j, o_ref in enumerate(outs):
            o_ref[...] = proj[:, IN_SPLITS[j]:IN_SPLITS[j + 1]]

    return _carry(
        body, comm, name="inproj_fwd", steps=t // tm,
        out_shape=tuple(jax.ShapeDtypeStruct((t, w), F32) for w in widths),
        in_specs=[pl.BlockSpec((tm, D_MODEL), lambda i: (i, 0)),
                  _const_spec((1, D_MODEL)),
                  _const_spec((N_SHARD, D_MODEL, sh_cols))],
        out_specs=tuple(pl.BlockSpec((tm, w), lambda i: (i, 0)) for w in widths),
        args=(h, gain, w_in),
    )


def inproj_bwd(dres, h, gain, w_in, dparts, comm=None, tm=256):
    t = h.shape[0]
    widths = [IN_SPLITS[j + 1] - IN_SPLITS[j] for j in range(7)]
    sh_cols = IN_COLS // N_SHARD

    def body(dres_ref, h_ref, g_ref, w_ref, d0, d1, d2, d3, d4, d5, d6, dh_ref, dg_ref, nb_ref, dp_ref):
        hv = h_ref[...]
        g = g_ref[...]
        n, r = _rms_fwd(hv, g)
        nb_ref[...] = n.astype(BF16)
        dproj = jnp.concatenate([d[...] for d in (d0, d1, d2, d3, d4, d5, d6)], axis=1).astype(BF16)
        dp_ref[...] = dproj
        dn = jnp.zeros((tm, D_MODEL), F32)
        for s in range(N_SHARD):
            dn = dn + _dot_nt(dproj[:, s * sh_cols:(s + 1) * sh_cols], w_ref[s])
        dx, dg = _rms_bwd(hv, r, g, dn)
        dh_ref[...] = dres_ref[...] + dx
        _accum(dg_ref, dg, pl.program_id(0) == 0)

    tok = pl.BlockSpec((tm, D_MODEL), lambda i: (i, 0))
    return _carry(
        body, comm, name="inproj_bwd", steps=t // tm,
        out_shape=(jax.ShapeDtypeStruct((t, D_MODEL), F32),
                   jax.ShapeDtypeStruct((1, D_MODEL), F32),
                   jax.ShapeDtypeStruct((t, D_MODEL), BF16),
                   jax.ShapeDtypeStruct((t, IN_COLS), BF16)),
        in_specs=[tok, tok, _const_spec((1, D_MODEL)), _const_spec((N_SHARD, D_MODEL, sh_cols))]
                 + [pl.BlockSpec((tm, w), lambda i: (i, 0)) for w in widths],
        out_specs=(tok, pl.BlockSpec((1, D_MODEL), lambda i: (0, 0)), tok,
                   pl.BlockSpec((tm, IN_COLS), lambda i: (i, 0))),
        args=(dres, h, gain, w_in, *dparts),
    )


def s5_prep(lam_re, lam_im, log_dt, b_re, b_im, c_re, c_im):
    dt = jnp.exp(log_dt)[:, None]
    mag = jnp.exp(lam_re * dt)
    lbr = mag * jnp.cos(lam_im * dt)
    lbi = mag * jnp.sin(lam_im * dt)
    den = lam_re * lam_re + lam_im * lam_im
    nr, ni = lbr - 1.0, lbi
    kr = (nr * lam_re + ni * lam_im) / den
    ki = (ni * lam_re - nr * lam_im) / den
    bbr = kr[..., None] * b_re - ki[..., None] * b_im
    bbi = kr[..., None] * b_im + ki[..., None] * b_re
    eye = jnp.eye(16, dtype=F32)

    def bm(bp):
        return jnp.einsum('kgph,gG->kghGp', bp.reshape(S5_KT, 16, S5_STATE, S5_GROUP), eye).reshape(S5_KT, 256, 1024)

    def cm(cp):
        return jnp.einsum('kghp,gG->kgpGh', cp.reshape(S5_KT, 16, S5_GROUP, S5_STATE), eye).reshape(S5_KT, 1024, 256)

    lam_bar = jnp.stack([lbr.reshape(S5_N), lbi.reshape(S5_N)])
    bmat = jnp.stack([bm(bbr), bm(bbi)])
    cmat = jnp.stack([cm(c_re), -cm(c_im)])
    return lam_bar, bmat, cmat


def _lam_powers(lam_bar):
    lr, li = lam_bar[0], lam_bar[1]
    pr, pi = [lr], [li]
    for _ in range(7):
        pr, pi = pr + [pr[-1] * lr - pi[-1] * li], pi + [pr[-1] * li + pi[-1] * lr]
    return jnp.stack(pr), jnp.stack(pi)


SCAN_SHIFTS = ((1, 0), (2, 1), (4, 3))


def _scan_tables(pw_r, pw_i, reverse):
    rows = jnp.arange(8)[:, None]
    planes_r, planes_i = [], []
    for sh, idx in SCAN_SHIFTS:
        keep = (rows < 8 - sh) if reverse else (rows >= sh)
        planes_r.append(jnp.where(keep, pw_r[idx:idx + 1], 0.0))
        planes_i.append(jnp.where(keep, pw_i[idx:idx + 1], 0.0))
    carry = [pw_r[::-1], pw_i[::-1]] if reverse else [pw_r, pw_i]
    return jnp.stack(planes_r + planes_i + carry)


def s5_fwd(u, tab, bmat, cmat, dvec, comm=None, tm=256):
    t = u.shape[0]
    nch = tm // 8

    def body(u_ref, tab_ref, b_ref, c_ref, d_ref, y_ref, xp_ref, x_scr, carry):
        @pl.when(pl.program_id(0) == 0)
        def _():
            carry[...] = jnp.zeros_like(carry)

        uv = u_ref[...]
        ub = uv.astype(BF16)
        for part in range(2):
            for kt in range(S5_KT):
                x_scr[:, pl.ds(part * S5_N + kt * 1024, 1024)] = _dot(ub[:, kt * 256:(kt + 1) * 256], b_ref[part, kt])
        row = lax.broadcasted_iota(jnp.int32, (8, S5_N), 0)

        def chunk(i, c):
            cr, ci = c
            r0 = pl.multiple_of(i * 8, 8)
            xr = x_scr[pl.ds(r0, 8), pl.ds(0, S5_N)]
            xi = x_scr[pl.ds(r0, 8), pl.ds(S5_N, S5_N)]
            for lvl, (sh, _) in enumerate(SCAN_SHIFTS):
                sr = pltpu.roll(xr, sh, 0)
                si = pltpu.roll(xi, sh, 0)
                lr = tab_ref[lvl]
                li = tab_ref[3 + lvl]
                xr, xi = xr + lr * sr - li * si, xi + lr * si + li * sr
            pwr = tab_ref[6]
            pwi = tab_ref[7]
            xr, xi = xr + pwr * cr - pwi * ci, xi + pwr * ci + pwi * cr
            x_scr[pl.ds(r0, 8), pl.ds(0, S5_N)] = xr
            x_scr[pl.ds(r0, 8), pl.ds(S5_N, S5_N)] = xi
            xp_ref[pl.ds(r0, 8), pl.ds(0, S5_N)] = jnp.where(row == 0, cr, pltpu.roll(xr, 1, 0))
            xp_ref[pl.ds(r0, 8), pl.ds(S5_N, S5_N)] = jnp.where(row == 0, ci, pltpu.roll(xi, 1, 0))
            return xr[7:8, :], xi[7:8, :]

        cr, ci = lax.fori_loop(0, nch, chunk, (carry[0:1, :], carry[1:2, :]))
        carry[0:1, :] = cr
        carry[1:2, :] = ci
        for kt in range(S5_KT):
            acc = jnp.zeros((tm, 256), F32)
            for part in range(2):
                acc = acc + _dot(x_scr[:, pl.ds(part * S5_N + kt * 1024, 1024)].astype(BF16), c_ref[part, kt])
            y_ref[:, pl.ds(kt * 256, 256)] = acc + d_ref[:, pl.ds(kt * 256, 256)] * uv[:, kt * 256:(kt + 1) * 256]

    return _carry(
        body, comm, name="s5_fwd", steps=t // tm,
        out_shape=(jax.ShapeDtypeStruct((t, S5_WIDTH), F32),
                   jax.ShapeDtypeStruct((t, 2 * S5_N), F32)),
        in_specs=[pl.BlockSpec((tm, S5_WIDTH), lambda i: (i, 0)),
                  _const_spec((8, 8, S5_N)),
                  _const_spec((2, S5_KT, 256, 1024)), _const_spec((2, S5_KT, 1024, 256)),
                  _const_spec((1, S5_WIDTH))],
        out_specs=(pl.BlockSpec((tm, S5_WIDTH), lambda i: (i, 0)),
                   pl.BlockSpec((tm, 2 * S5_N), lambda i: (i, 0))),
        scratch_shapes=[pltpu.VMEM((tm, 2 * S5_N), F32), pltpu.VMEM((8, S5_N), F32)],
        args=(u, tab, bmat, cmat, dvec),
    )


def s5_bwd(dy, u, xp, tab, bmat, cmat, dvec, comm=None, tm=256):
    t = u.shape[0]
    nt = t // tm
    nch = tm // 8

    def body(dy_ref, u_ref, xp_ref, tab_ref, b_ref, c_ref, d_ref,
             du_ref, db_ref, dc_ref, dl_ref, dd_ref, g_scr, x_scr, carry):
        first = pl.program_id(0) == 0

        @pl.when(first)
        def _():
            carry[...] = jnp.zeros_like(carry)
            dl_ref[...] = jnp.zeros_like(dl_ref)

        dyv = dy_ref[...]
        uv = u_ref[...]
        dyb = dyv.astype(BF16)
        ub = uv.astype(BF16)
        lr1 = tab_ref[6, 7:8, :]
        li1 = tab_ref[7, 7:8, :]
        for kt in range(S5_KT):
            cols = pl.ds(kt * 1024, 1024)
            colsi = pl.ds(S5_N + kt * 1024, 1024)
            g_scr[:, cols] = _dot_nt(dyb[:, kt * 256:(kt + 1) * 256], c_ref[0, kt])
            g_scr[:, colsi] = _dot_nt(dyb[:, kt * 256:(kt + 1) * 256], c_ref[1, kt])
            bur = _dot(ub[:, kt * 256:(kt + 1) * 256], b_ref[0, kt])
            bui = _dot(ub[:, kt * 256:(kt + 1) * 256], b_ref[1, kt])
            xpr = xp_ref[:, cols]
            xpi = xp_ref[:, colsi]
            lrk = lr1[:, kt * 1024:(kt + 1) * 1024]
            lik = li1[:, kt * 1024:(kt + 1) * 1024]
            x_scr[:, cols] = lrk * xpr - lik * xpi + bur
            x_scr[:, colsi] = lrk * xpi + lik * xpr + bui

        def chunk(j, c):
            cr, ci = c
            r0 = pl.multiple_of((nch - 1 - j) * 8, 8)
            gr = g_scr[pl.ds(r0, 8), pl.ds(0, S5_N)]
            gi = g_scr[pl.ds(r0, 8), pl.ds(S5_N, S5_N)]
            for lvl, (sh, _) in enumerate(SCAN_SHIFTS):
                sr = pltpu.roll(gr, 8 - sh, 0)
                si = pltpu.roll(gi, 8 - sh, 0)
                lr = tab_ref[lvl]
                li = tab_ref[3 + lvl]
                gr, gi = gr + lr * sr + li * si, gi + lr * si - li * sr
            pvr = tab_ref[6]
            pvi = tab_ref[7]
            gr, gi = gr + pvr * cr + pvi * ci, gi + pvr * ci - pvi * cr
            g_scr[pl.ds(r0, 8), pl.ds(0, S5_N)] = gr
            g_scr[pl.ds(r0, 8), pl.ds(S5_N, S5_N)] = gi
            xpr = xp_ref[pl.ds(r0, 8), pl.ds(0, S5_N)]
            xpi = xp_ref[pl.ds(r0, 8), pl.ds(S5_N, S5_N)]
            dl_ref[0] += gr * xpr + gi * xpi
            dl_ref[1] += gi * xpr - gr * xpi
            return gr[0:1, :], gi[0:1, :]

        cr, ci = lax.fori_loop(0, nch, chunk, (carry[0:1, :], carry[1:2, :]))
        carry[0:1, :] = cr
        carry[1:2, :] = ci

        for kt in range(S5_KT):
            du = jnp.zeros((tm, 256), F32)
            ukt = ub[:, kt * 256:(kt + 1) * 256]
            dykt = dyb[:, kt * 256:(kt + 1) * 256]
            for part in range(2):
                gb = g_scr[:, pl.ds(part * S5_N + kt * 1024, 1024)].astype(BF16)
                xb = x_scr[:, pl.ds(part * S5_N + kt * 1024, 1024)].astype(BF16)
                du = du + _dot_nt(gb, b_ref[part, kt])
                dbv = _dot_tn(ukt, gb)
                dcv = _dot_tn(xb, dykt)

                @pl.when(first)
                def _():
                    db_ref[part, kt] = dbv
                    dc_ref[part, kt] = dcv

                @pl.when(jnp.logical_not(first))
                def _():
                    db_ref[part, kt] += dbv
                    dc_ref[part, kt] += dcv
            du_ref[:, pl.ds(kt * 256, 256)] = du + d_ref[:, pl.ds(kt * 256, 256)] * dyv[:, kt * 256:(kt + 1) * 256]
        _accum(dd_ref, jnp.sum(dyv * uv, axis=0, keepdims=True), first)

    rev = lambda i: (nt - 1 - i, 0)
    return _carry(
        body, comm, name="s5_bwd", steps=nt,
        out_shape=(jax.ShapeDtypeStruct((t, S5_WIDTH), F32),
                   jax.ShapeDtypeStruct((2, S5_KT, 256, 1024), F32),
                   jax.ShapeDtypeStruct((2, S5_KT, 1024, 256), F32),
                   jax.ShapeDtypeStruct((2, 8, S5_N), F32),
                   jax.ShapeDtypeStruct((1, S5_WIDTH), F32)),
        in_specs=[pl.BlockSpec((tm, S5_WIDTH), rev), pl.BlockSpec((tm, S5_WIDTH), rev),
                  pl.BlockSpec((tm, 2 * S5_N), rev),
                  _const_spec((8, 8, S5_N)),
                  _const_spec((2, S5_KT, 256, 1024)), _const_spec((2, S5_KT, 1024, 256)),
                  _const_spec((1, S5_WIDTH))],
        out_specs=(pl.BlockSpec((tm, S5_WIDTH), rev),
                   pl.BlockSpec((2, S5_KT, 256, 1024), lambda i: (0, 0, 0, 0)),
                   pl.BlockSpec((2, S5_KT, 1024, 256), lambda i: (0, 0, 0, 0)),
                   pl.BlockSpec((2, 8, S5_N), lambda i: (0, 0, 0)),
                   pl.BlockSpec((1, S5_WIDTH), lambda i: (0, 0))),
        scratch_shapes=[pltpu.VMEM((tm, 2 * S5_N), F32), pltpu.VMEM((tm, 2 * S5_N), F32),
                        pltpu.VMEM((8, S5_N), F32)],
        args=(dy, u, xp, tab, bmat, cmat, dvec),
    )


def _hg_gates(z, lb):
    sg = _sigmoid(z)
    sgn = _sigmoid(-z)
    fg = lb + (1.0 - lb) * sg
    return sg, sgn, fg, jnp.log(fg), (1.0 - lb) * sgn


def _hg_decays(g, tril):
    gc = jnp.dot(tril, g, precision=HIGHEST, preferred_element_type=F32)
    mid = gc[CHUNK // 2 - 1:CHUNK // 2, :]
    last = gc[CHUNK - 1:CHUNK, :]
    return jnp.exp(gc), jnp.exp(gc - mid), jnp.exp(mid - gc), jnp.exp(last - gc), jnp.exp(last)


def _split_bf16(x):
    hi = x.astype(BF16)
    return hi, (x - hi.astype(F32)).astype(BF16)


def _hg_scores(qt, qlo, kt, klo, sl, causal):
    a = _dot_nt(qt[:, sl], kt[:, sl]) + _dot_nt(qt[:, sl], klo[:, sl]) + _dot_nt(qlo[:, sl], kt[:, sl])
    return jnp.where(causal, a, 0.0).astype(BF16)


def hgrn_fwd(q, f, v, lb, comm=None):
    t = q.shape[0]
    nc = t // CHUNK
    scale = HG_E ** -0.5

    def body(q_ref, f_ref, v_ref, lb_ref, o_ref, st_ref, state):
        @pl.when(pl.program_id(0) == 0)
        def _():
            state[...] = jnp.zeros_like(state)

        ri = lax.broadcasted_iota(jnp.int32, (CHUNK, CHUNK), 0)
        ci = lax.broadcasted_iota(jnp.int32, (CHUNK, CHUNK), 1)
        causal = ri >= ci
        tril = causal.astype(F32)
        for sub in range(HG_SUB):
            rows = pl.ds(sub * CHUNK, CHUNK)
            _, _, _, g, k = _hg_gates(f_ref[rows, :], lb_ref[...])
            eg, eq, ek, ed, el = _hg_decays(g, tril)
            qs = q_ref[rows, :] * scale
            qg = (qs * eg).astype(BF16)
            qt, qlo = _split_bf16(qs * eq)
            kt, klo = _split_bf16(k * ek)
            kd = (k * ed).astype(BF16)
            vb = v_ref[rows, :].astype(BF16)
            for h in range(HG_HEADS):
                sl = slice(h * HG_E, (h + 1) * HG_E)
                st = state[h]
                a = _hg_scores(qt, qlo, kt, klo, sl, causal)
                o_ref[rows, sl] = _dot(a, vb[:, sl]) + _dot_nt(qg[:, sl], st.astype(BF16))
                st_new = st * el[:, sl] + _dot_tn(vb[:, sl], kd[:, sl])
                state[h] = st_new
                st_ref[sub, h] = st_new

    tok = pl.BlockSpec((HG_SUB * CHUNK, HG_WIDTH), lambda i: (i, 0))
    return _carry(
        body, comm, name="hgrn_fwd", steps=nc // HG_SUB,
        out_shape=(jax.ShapeDtypeStruct((t, HG_WIDTH), F32),
                   jax.ShapeDtypeStruct((nc, HG_HEADS, HG_E, HG_E), F32)),
        in_specs=[tok, tok, tok, _const_spec((1, HG_WIDTH))],
        out_specs=(tok, pl.BlockSpec((HG_SUB, HG_HEADS, HG_E, HG_E), lambda i: (i, 0, 0, 0))),
        scratch_shapes=[pltpu.VMEM((HG_HEADS, HG_E, HG_E), F32)],
        args=(q, f, v, lb),
    )


def hgrn_bwd(do, q, f, v, lb, states, comm=None):
    t = q.shape[0]
    nc = t // CHUNK
    scale = HG_E ** -0.5

    ns = nc // HG_SUB

    def body(do_ref, q_ref, f_ref, v_ref, lb_ref, scur_ref, sprev_ref, dq_ref, df_ref, dv_ref, dlb_ref, dstate):
        first = pl.program_id(0) == 0
        has_prev = jnp.where(pl.program_id(0) < ns - 1, 1.0, 0.0)

        @pl.when(first)
        def _():
            dstate[...] = jnp.zeros_like(dstate)

        ri = lax.broadcasted_iota(jnp.int32, (CHUNK, CHUNK), 0)
        ci = lax.broadcasted_iota(jnp.int32, (CHUNK, CHUNK), 1)
        causal = ri >= ci
        tril = causal.astype(F32)
        triu = (ri <= ci).astype(F32)
        rowc = lax.broadcasted_iota(jnp.int32, (CHUNK, HG_WIDTH), 0)
        lb = lb_ref[...]
        dlb = jnp.zeros((1, HG_WIDTH), F32)
        for sub in reversed(range(HG_SUB)):
            rows = pl.ds(sub * CHUNK, CHUNK)
            sg, sgn, fg, g, k = _hg_gates(f_ref[rows, :], lb)
            eg, eq, ek, ed, el = _hg_decays(g, tril)
            qs = q_ref[rows, :] * scale
            qg = (qs * eg).astype(BF16)
            qt, qlo = _split_bf16(qs * eq)
            kt, klo = _split_bf16(k * ek)
            kd = (k * ed).astype(BF16)
            vb = v_ref[rows, :].astype(BF16)
            dob = do_ref[rows, :].astype(BF16)
            dqs_l, dk_l, dgc_l, dgl_l = [], [], [], []
            for h in range(HG_HEADS):
                sl = slice(h * HG_E, (h + 1) * HG_E)
                s0 = scur_ref[sub - 1, h] if sub > 0 else sprev_ref[HG_SUB - 1, h] * has_prev
                ds1 = dstate[h]
                ds1b = ds1.astype(BF16)
                a = _hg_scores(qt, qlo, kt, klo, sl, causal)
                da = jnp.where(causal, _dot_nt(dob[:, sl], vb[:, sl]), 0.0).astype(BF16)
                dv_ref[rows, sl] = _dot_tn(a, dob[:, sl]) + _dot_nt(kd[:, sl], ds1b)
                dkd = _dot(vb[:, sl], ds1b)
                dqt = _dot(da, kt[:, sl])
                dkt = _dot_tn(da, qt[:, sl])
                dqg = _dot(dob[:, sl], s0.astype(BF16))
                dqs_l.append(dqt * eq[:, sl] + dqg * eg[:, sl])
                dk_l.append(dkt * ek[:, sl] + dkd * ed[:, sl])
                kd_dkd = kd[:, sl].astype(F32) * dkd
                dgc_l.append(qt[:, sl].astype(F32) * dqt - kt[:, sl].astype(F32) * dkt
                             + qg[:, sl].astype(F32) * dqg - kd_dkd)
                dgl_l.append(el[:, sl] * jnp.sum(ds1 * s0, axis=0, keepdims=True)
                             + jnp.sum(kd_dkd, axis=0, keepdims=True))
                dstate[h] = ds1 * el[:, sl] + _dot_tn(dob[:, sl], qg[:, sl])
            dqs = jnp.concatenate(dqs_l, axis=1)
            dk = jnp.concatenate(dk_l, axis=1)
            dgl = jnp.concatenate(dgl_l, axis=1)
            dq_ref[rows, :] = dqs * scale
            dgc = jnp.concatenate(dgc_l, axis=1) + jnp.where(rowc == CHUNK - 1, dgl, 0.0)
            dg = jnp.dot(triu, dgc, precision=HIGHEST, preferred_element_type=F32)
            w = dg / fg - dk
            df_ref[rows, :] = w * (1.0 - lb) * sg * sgn
            dlb = dlb + jnp.sum(w * sgn, axis=0, keepdims=True)
        _accum(dlb_ref, dlb, first)

    rev = lambda i: (ns - 1 - i, 0)
    tok = pl.BlockSpec((HG_SUB * CHUNK, HG_WIDTH), rev)
    st_blk = (HG_SUB, HG_HEADS, HG_E, HG_E)
    return _carry(
        body, comm, name="hgrn_bwd", steps=ns,
        out_shape=(jax.ShapeDtypeStruct((t, HG_WIDTH), F32),
                   jax.ShapeDtypeStruct((t, HG_WIDTH), F32),
                   jax.ShapeDtypeStruct((t, HG_WIDTH), F32),
                   jax.ShapeDtypeStruct((1, HG_WIDTH), F32)),
        in_specs=[tok, tok, tok, tok, _const_spec((1, HG_WIDTH)),
                  pl.BlockSpec(st_blk, lambda i: (ns - 1 - i, 0, 0, 0)),
                  pl.BlockSpec(st_blk, lambda i: (jnp.maximum(ns - 2 - i, 0), 0, 0, 0))],
        out_specs=(tok, tok, tok, pl.BlockSpec((1, HG_WIDTH), lambda i: (0, 0))),
        scratch_shapes=[pltpu.VMEM((HG_HEADS, HG_E, HG_E), F32)],
        args=(do, q, f, v, lb, states, states),
    )


GELU_C = math.sqrt(2.0 / math.pi)


def _gelu(x):
    th = jnp.tanh(GELU_C * (x + 0.044715 * x * x * x))
    return 0.5 * x * (1.0 + th), th


def _merge_core(ys5, o, og, ga, gb, wv_ref, wt_ref, ghg, who_ref):
    ys, th = _gelu(ys5)
    ysb = ys.astype(BF16)
    va = jnp.concatenate([_dot(ysb, wv_ref[s]) for s in range(N_SHARD)], axis=1)
    vt = jnp.concatenate([_dot(ysb, wt_ref[s]) for s in range(N_SHARD)], axis=1)
    svt = _sigmoid(vt)
    ya = va * svt
    rs, ons = [], []
    for h in range(HG_HEADS):
        oh = o[:, h * HG_E:(h + 1) * HG_E]
        r = lax.rsqrt(jnp.mean(oh * oh, axis=-1, keepdims=True) + NORM_EPS)
        rs.append(r)
        ons.append(oh * r)
    on = jnp.concatenate(ons, axis=1)
    sgo = _sigmoid(og)
    o2 = on * ghg * (og * sgo)
    o2b = o2.astype(BF16)
    yb = _dot(o2b, who_ref[...])
    sa = _sigmoid(ga)
    sb = _sigmoid(gb)
    mixed = sa * ya + sb * yb
    return dict(ys=ys, th=th, ysb=ysb, va=va, svt=svt, ya=ya, rs=rs, on=on, sgo=sgo, o2b=o2b, yb=yb,
                sa=sa, sb=sb, mixed=mixed)


def merge_fwd(h, ys5, o, og, ga, gb, wv, wt, ghg, who, wmo, tm=256):
    t = h.shape[0]

    def body(h_ref, ys5_ref, o_ref, og_ref, ga_ref, gb_ref, wv_ref, wt_ref, ghg_ref, who_ref, wmo_ref, out_ref):
        c = _merge_core(ys5_ref[...], o_ref[...], og_ref[...], ga_ref[...], gb_ref[...],
                        wv_ref, wt_ref, ghg_ref[...], who_ref)
        out_ref[...] = h_ref[...] + _dot(c["mixed"].astype(BF16), wmo_ref[...])

    tok = pl.BlockSpec((tm, D_MODEL), lambda i: (i, 0))
    return pl.pallas_call(
        body, name="merge_fwd", grid=(t // tm,),
        out_shape=jax.ShapeDtypeStruct((t, D_MODEL), F32),
        in_specs=[tok, pl.BlockSpec((tm, S5_WIDTH), lambda i: (i, 0)), tok, tok, tok, tok,
                  _const_spec((N_SHARD, S5_WIDTH, 256)), _const_spec((N_SHARD, S5_WIDTH, 256)),
                  _const_spec((1, HG_WIDTH)), _const_spec((HG_WIDTH, D_MODEL)), _const_spec((D_MODEL, D_MODEL))],
        out_specs=tok,
        compiler_params=_cparams(("arbitrary",)),
    )(h, ys5, o, og, ga, gb, wv, wt, ghg, who, wmo)


def merge_bwd(dh, ys5, o, og, ga, gb, wv, wt, ghg, who, wmo, comm=None, tm=256):
    t = dh.shape[0]

    def body(dh_ref, ys5_ref, o_ref, og_ref, ga_ref, gb_ref, wv_ref, wt_ref, ghg_ref, who_ref, wmo_ref,
             dys5_ref, do_ref, dog_ref, dga_ref, dgb_ref, dghg_ref,
             mixb_ref, dhb_ref, ysb_ref, dvab_ref, dvtb_ref, o2b_ref, dybb_ref):
        ys5 = ys5_ref[...]
        o = o_ref[...]
        og = og_ref[...]
        ghg = ghg_ref[...]
        c = _merge_core(ys5, o, og, ga_ref[...], gb_ref[...], wv_ref, wt_ref, ghg, who_ref)
        dhb = dh_ref[...].astype(BF16)
        dhb_ref[...] = dhb
        mixb_ref[...] = c["mixed"].astype(BF16)
        ysb_ref[...] = c["ysb"]
        o2b_ref[...] = c["o2b"]
        dmix = _dot_nt(dhb, wmo_ref[...])
        sa, sb = c["sa"], c["sb"]
        dya = dmix * sa
        dyb = dmix * sb
        dga_ref[...] = dmix * c["ya"] * sa * (1.0 - sa)
        dgb_ref[...] = dmix * c["yb"] * sb * (1.0 - sb)
        svt = c["svt"]
        dva = (dya * svt).astype(BF16)
        dvt = (dya * c["va"] * svt * (1.0 - svt)).astype(BF16)
        dvab_ref[...] = dva
        dvtb_ref[...] = dvt
        dys = jnp.zeros((tm, S5_WIDTH), F32)
        for s in range(N_SHARD):
            dys = dys + _dot_nt(dva[:, s * 256:(s + 1) * 256], wv_ref[s]) + _dot_nt(dvt[:, s * 256:(s + 1) * 256], wt_ref[s])
        th = c["th"]
        dgelu = 0.5 * (1.0 + th) + 0.5 * ys5 * (1.0 - th * th) * GELU_C * (1.0 + 3.0 * 0.044715 * ys5 * ys5)
        dys5_ref[...] = dys * dgelu
        dybb = dyb.astype(BF16)
        dybb_ref[...] = dybb
        do2 = _dot_nt(dybb, who_ref[...])
        sgo = c["sgo"]
        sil = og * sgo
        on = c["on"]
        dog_ref[...] = do2 * on * ghg * (sgo * (1.0 + og * (1.0 - sgo)))
        _accum(dghg_ref, jnp.sum(do2 * on * sil, axis=0, keepdims=True), pl.program_id(0) == 0)
        don = do2 * ghg * sil
        dos = []
        for h in range(HG_HEADS):
            sl = slice(h * HG_E, (h + 1) * HG_E)
            m = jnp.mean(don[:, sl] * on[:, sl], axis=-1, keepdims=True)
            dos.append(c["rs"][h] * (don[:, sl] - on[:, sl] * m))
        do_ref[...] = jnp.concatenate(dos, axis=1)

    tok = pl.BlockSpec((tm, D_MODEL), lambda i: (i, 0))
    s5b = pl.BlockSpec((tm, S5_WIDTH), lambda i: (i, 0))
    f32t = jax.ShapeDtypeStruct((t, D_MODEL), F32)
    bft = jax.ShapeDtypeStruct((t, D_MODEL), BF16)
    return _carry(
        body, comm, name="merge_bwd", steps=t // tm,
        out_shape=(jax.ShapeDtypeStruct((t, S5_WIDTH), F32), f32t, f32t, f32t, f32t,
                   jax.ShapeDtypeStruct((1, HG_WIDTH), F32),
                   bft, bft, jax.ShapeDtypeStruct((t, S5_WIDTH), BF16), bft, bft, bft, bft),
        in_specs=[tok, s5b, tok, tok, tok, tok,
                  _const_spec((N_SHARD, S5_WIDTH, 256)), _const_spec((N_SHARD, S5_WIDTH, 256)),
                  _const_spec((1, HG_WIDTH)), _const_spec((HG_WIDTH, D_MODEL)), _const_spec((D_MODEL, D_MODEL))],
        out_specs=(s5b, tok, tok, tok, tok, pl.BlockSpec((1, HG_WIDTH), lambda i: (0, 0)),
                   tok, tok, s5b, tok, tok, tok, tok),
        args=(dh, ys5, o, og, ga, gb, wv, wt, ghg, who, wmo),
    )


def head_fwd_bwd(h, p, tgt, gple, wpg, wpp, gfin, tm=256):
    t = h.shape[0]

    def body(h_ref, p_ref, tgt_ref, gple_ref, wpg_ref, wpp_ref, gfin_ref,
             loss_ref, dh_ref, dgple_ref, dgfin_ref, nb_ref, dlb_ref, dppb_ref):
        first = pl.program_id(0) == 0
        hv = h_ref[...]
        gple = gple_ref[...]
        gfin = gfin_ref[...]
        n, r3 = _rms_fwd(hv, gple)
        nb = n.astype(BF16)
        nb_ref[...] = nb
        pg = _sigmoid(_dot(nb, wpg_ref[...]))
        pb = p_ref[...].astype(BF16)
        pp = jnp.concatenate([_dot(pb, wpp_ref[s]) for s in range(N_SHARD)], axis=1)
        h4 = hv + pg * pp
        y, r4 = _rms_fwd(h4, gfin)
        err = y - tgt_ref[...]
        lsum = 0.5 * jnp.sum(jnp.sum(err * err, axis=-1, keepdims=True), axis=0, keepdims=True) / D_MODEL
        _accum(loss_ref, jnp.broadcast_to(lsum, (8, 128)), first)
        dy = err * (1.0 / D_MODEL)
        dh4, dgf = _rms_bwd(h4, r4, gfin, dy)
        _accum(dgfin_ref, dgf, first)
        dpp = dh4 * pg
        dppb_ref[...] = dpp.astype(BF16)
        dl = (dh4 * pp * pg * (1.0 - pg)).astype(BF16)
        dlb_ref[...] = dl
        dn = _dot_nt(dl, wpg_ref[...])
        dx, dgp = _rms_bwd(hv, r3, gple, dn)
        _accum(dgple_ref, dgp, first)
        dh_ref[...] = dh4 + dx

    tok = pl.BlockSpec((tm, D_MODEL), lambda i: (i, 0))
    vec = pl.BlockSpec((1, D_MODEL), lambda i: (0, 0))
    bft = jax.ShapeDtypeStruct((t, D_MODEL), BF16)
    return pl.pallas_call(
        body, name="head_fwd_bwd", grid=(t // tm,),
        out_shape=(jax.ShapeDtypeStruct((8, 128), F32), jax.ShapeDtypeStruct((t, D_MODEL), F32),
                   jax.ShapeDtypeStruct((1, D_MODEL), F32), jax.ShapeDtypeStruct((1, D_MODEL), F32),
                   bft, bft, bft),
        in_specs=[tok, pl.BlockSpec((tm, PLE_DIM), lambda i: (i, 0)), tok,
                  _const_spec((1, D_MODEL)), _const_spec((D_MODEL, D_MODEL)),
                  _const_spec((N_SHARD, PLE_DIM, 256)), _const_spec((1, D_MODEL))],
        out_specs=(pl.BlockSpec((8, 128), lambda i: (0, 0)), tok, vec, vec, tok, tok, tok),
        compiler_params=_cparams(("arbitrary",)),
    )(h, p, tgt, gple, wpg, wpp, gfin)


BIG = ("ffn1_w_gate", "ffn1_w_up", "ffn1_w_down", "w_in", "s5_glu_val", "s5_glu_gate", "hg_w_out",
       "w_merge_out", "ffn2_w_gate", "ffn2_w_up", "ffn2_w_down", "ple_w_gate", "ple_w_proj")
FFN_T = ("ffn1_w_gate", "ffn1_w_up", "ffn2_w_gate", "ffn2_w_up")
BIG_SHARD = {
    "ffn1_w_gate": (FF_PAD, D_MODEL), "ffn1_w_up": (FF_PAD, D_MODEL), "ffn1_w_down": (FF_PAD, D_MODEL),
    "ffn2_w_gate": (FF_PAD, D_MODEL), "ffn2_w_up": (FF_PAD, D_MODEL), "ffn2_w_down": (FF_PAD, D_MODEL),
    "w_in": (D_MODEL, IN_COLS // N_SHARD), "s5_glu_val": (S5_WIDTH, 256), "s5_glu_gate": (S5_WIDTH, 256),
    "hg_w_out": (256, D_MODEL), "w_merge_out": (256, D_MODEL), "ple_w_gate": (256, D_MODEL),
    "ple_w_proj": (PLE_DIM, 256),
}


def _lower_bound(hb):
    return jax.nn.softmax(hb, axis=0)[0:1]


class Schedule:
    def __init__(self, wts):
        self.wts = dict(wts)
        self.grads = {}

    def before(self, kernel_name):
        return None

    def after(self, kernel_name, results):
        pass

    def grad(self, name, g):
        self.grads[name] = g


def local_step(x, p, tgt, sched, sm):
    wts = sched.wts
    rows_full = lambda w: w.reshape(N_SHARD * w.shape[1], w.shape[2])

    def carried(kernel_name, fn, *args):
        outs, results = fn(*args, comm=sched.before(kernel_name))
        sched.after(kernel_name, results)
        return outs

    def weight_grad(name, xs, ys, shard):
        kernel_name = "g_" + name
        (g,), results = tn_matmul(xs, ys, kernel_name, shard, comm=sched.before(kernel_name))
        sched.grad(name, g)
        sched.after(kernel_name, results)

    lb, lb_vjp = jax.vjp(_lower_bound, sm["hg_lower_bound"])
    s5_names = ("s5_lam_re", "s5_lam_im", "s5_log_dt", "s5_b_re", "s5_b_im", "s5_c_re", "s5_c_im")
    (lam_bar, bmat, cmat), s5_vjp = jax.vjp(s5_prep, *[sm[k] for k in s5_names])
    pw_r, pw_i = _lam_powers(lam_bar)
    bmat_b = bmat.astype(BF16)
    cmat_b = cmat.astype(BF16)

    h1, a1, b1 = carried("ffn1_fwd", ffn_fwd, x, sm["ffn1_norm"], wts["ffn1_w_gate"], wts["ffn1_w_up"],
                         wts["ffn1_w_down"], "ffn1_fwd")
    s5in, q, f, v, og, ga, gb = carried("inproj_fwd", inproj_fwd, h1, sm["mix_norm"], wts["w_in"])
    ys5, xp = carried("s5_fwd", s5_fwd, s5in, _scan_tables(pw_r, pw_i, False), bmat_b, cmat_b, sm["s5_d"])
    o, states = carried("hgrn_fwd", hgrn_fwd, q, f, v, lb)
    who = rows_full(wts["hg_w_out"])
    wmo = rows_full(wts["w_merge_out"])
    h2 = merge_fwd(h1, ys5, o, og, ga, gb, wts["s5_glu_val"], wts["s5_glu_gate"], sm["hg_out_norm"], who, wmo)
    (h3, a2, b2), _ = ffn_fwd(h2, sm["ffn2_norm"], wts["ffn2_w_gate"], wts["ffn2_w_up"], wts["ffn2_w_down"], "ffn2_fwd")
    loss, dh3, d_ple_norm, d_final_norm, npb, dlgb, dppb = head_fwd_bwd(
        h3, p, tgt, sm["ple_norm"], rows_full(wts["ple_w_gate"]), wts["ple_w_proj"], sm["final_norm"])

    gs = {"ple_norm": d_ple_norm, "final_norm": d_final_norm}
    weight_grad("ple_w_gate", npb, dlgb, "rows")
    weight_grad("ple_w_proj", p, dppb, "cols")

    (dh2, gs["ffn2_norm"], n2b, dhb2, da2, db2, s2), _ = ffn_bwd(
        dh3, h2, a2, b2, sm["ffn2_norm"], wts["ffn2_w_gate"], wts["ffn2_w_up"], wts["ffn2_w_down"], "ffn2_bwd")
    weight_grad("ffn2_w_gate", da2, n2b, "rows")
    weight_grad("ffn2_w_up", db2, n2b, "rows")
    weight_grad("ffn2_w_down", s2, dhb2, "rows")

    dys5, do, dog, dga, dgb, gs["hg_out_norm"], mixb, dh2b, ysb, dvab, dvtb, o2b, dybb = carried(
        "merge_bwd", merge_bwd,
        dh2, ys5, o, og, ga, gb, wts["s5_glu_val"], wts["s5_glu_gate"], sm["hg_out_norm"], who, wmo)
    weight_grad("w_merge_out", mixb, dh2b, "rows")
    weight_grad("s5_glu_val", ysb, dvab, "cols")
    weight_grad("s5_glu_gate", ysb, dvtb, "cols")
    weight_grad("hg_w_out", o2b, dybb, "rows")

    dq, df, dv, dlb = carried("hgrn_bwd", hgrn_bwd, do, q, f, v, lb, states)
    (gs["hg_lower_bound"],) = lb_vjp(dlb)
    du, dbmat, dcmat, dlam8, gs["s5_d"] = carried(
        "s5_bwd", s5_bwd,
        dys5, s5in, xp, _scan_tables(pw_r, pw_i, True), bmat_b, cmat_b, sm["s5_d"])
    for k, g in zip(s5_names, s5_vjp((jnp.sum(dlam8, axis=1), dbmat, dcmat))):
        gs[k] = g

    dh1, gs["mix_norm"], nmb, dprojb = carried(
        "inproj_bwd", inproj_bwd, dh2, h1, sm["mix_norm"], wts["w_in"], (du, dq, df, dv, dog, dga, dgb))
    weight_grad("w_in", nmb, dprojb, "cols")

    dx, gs["ffn1_norm"], n1b, dhb1, da1, db1, s1 = carried(
        "ffn1_bwd", ffn_bwd,
        dh1, x, a1, b1, sm["ffn1_norm"], wts["ffn1_w_gate"], wts["ffn1_w_up"], wts["ffn1_w_down"], "ffn1_bwd")
    weight_grad("ffn1_w_gate", da1, n1b, "rows")
    weight_grad("ffn1_w_up", db1, n1b, "rows")
    weight_grad("ffn1_w_down", s1, dhb1, "rows")
    return loss, dx, gs


MESH = pl.DeviceIdType.MESH
ANY = pl.BlockSpec(memory_space=pl.ANY)


def _place():
    x, y, c = lax.axis_index("x"), lax.axis_index("y"), lax.axis_index("c")
    return x, y, c


def _remote(src, dst, ssem, rsem, dev):
    return pltpu.make_async_remote_copy(src_ref=src, dst_ref=dst, send_sem=ssem, recv_sem=rsem,
                                        device_id=dev, device_id_type=MESH)


class Comm:
    def __init__(self, bufs, outs, alias, sems, hooks):
        self.bufs, self.outs, self.alias, self.sems, self.hooks = list(bufs), list(outs), alias, list(sems), hooks


def run_comm(comm, name):
    nb, no = len(comm.bufs), len(comm.outs)

    def body(*refs):
        for which in ("first", "mid", "last"):
            if which in comm.hooks:
                comm.hooks[which](refs[:nb], refs[nb:nb + no], refs[nb + no:])

    return pl.pallas_call(
        body, name=name, out_shape=tuple(comm.outs), in_specs=[ANY] * nb, out_specs=tuple([ANY] * no),
        input_output_aliases=dict(comm.alias), scratch_shapes=comm.sems,
    )(*comm.bufs)


PLACE_ROWS = {1024: 256, 704: 352, 512: 256, 256: 256}


def place_shards(shards, padded_rows, comm, name):
    n, nb, no = len(shards), len(comm.bufs), len(comm.outs)
    stage_rows = max(PLACE_ROWS.values())
    stage_cols = max(s.shape[1] for s in shards)

    def body(*refs):
        ins, cb = refs[:n], refs[n:n + nb]
        outs, co = refs[n + nb:2 * n + nb], refs[2 * n + nb:2 * n + nb + no]
        stage_f32, stage_bf16, zeros, sem = refs[2 * n + nb + no:2 * n + nb + no + 4]
        cs = refs[2 * n + nb + no + 4:]
        chip = 2 * lax.axis_index("x") + lax.axis_index("y")
        zeros[...] = jnp.zeros_like(zeros)
        comm.hooks["first"](cb, co, cs)
        for w in range(n):
            if w == n // 2:
                comm.hooks["mid"](cb, co, cs)
            r0, cols = ins[w].shape
            step = PLACE_ROWS[r0]
            src32 = stage_f32.at[pl.ds(0, step), pl.ds(0, cols)]
            dst16 = stage_bf16.at[pl.ds(0, step), pl.ds(0, cols)]
            for row in range(0, r0, step):
                pltpu.sync_copy(ins[w].at[pl.ds(row, step), :], src32)
                dst16[...] = src32[...].astype(BF16)
                pltpu.sync_copy(dst16, outs[w].at[chip, pl.ds(row, step), :])
            pad = outs[w].shape[1] - r0
            if pad:
                cp = pltpu.make_async_copy(zeros.at[pl.ds(0, pad), pl.ds(0, cols)],
                                           outs[w].at[chip, pl.ds(r0, pad), :], sem)
                cp.start()
                cp.wait()
        comm.hooks["last"](cb, co, cs)

    res = pl.pallas_call(
        body, name=name,
        out_shape=tuple(jax.ShapeDtypeStruct((N_SHARD, r, s.shape[1]), BF16) for s, r in zip(shards, padded_rows))
        + tuple(comm.outs),
        in_specs=[ANY] * (n + nb), out_specs=tuple([ANY] * (n + no)),
        input_output_aliases={n + i: n + o for i, o in comm.alias.items()},
        scratch_shapes=[pltpu.VMEM((stage_rows, stage_cols), F32), pltpu.VMEM((stage_rows, stage_cols), BF16),
                        pltpu.VMEM((FF_PAD - FF_SHARD, D_MODEL), BF16), pltpu.SemaphoreType.DMA] + comm.sems,
        compiler_params=pltpu.CompilerParams(vmem_limit_bytes=VMEM_LIMIT),
    )(*shards, *comm.bufs)
    return res[:n], res[n:]


def _carry(body, comm, *, name, steps, out_shape, in_specs, out_specs, args, scratch_shapes=()):
    out_shape, out_specs, scratch_shapes = tuple(out_shape), tuple(out_specs), list(scratch_shapes)
    if comm is None:
        res = pl.pallas_call(body, name=name, grid=(steps,), out_shape=out_shape, in_specs=list(in_specs),
                             out_specs=out_specs, scratch_shapes=scratch_shapes,
                             compiler_params=_cparams(("arbitrary",)))(*args)
        return tuple(res), ()
    n_in, n_out, n_scr = len(args), len(out_shape), len(scratch_shapes)
    nb, no = len(comm.bufs), len(comm.outs)

    def wrapped(*refs):
        ins, cb = refs[:n_in], refs[n_in:n_in + nb]
        o0 = n_in + nb
        outs, co = refs[o0:o0 + n_out], refs[o0 + n_out:o0 + n_out + no]
        s0 = o0 + n_out + no
        scr, cs = refs[s0:s0 + n_scr], refs[s0 + n_scr:]
        step = pl.program_id(0)

        def hook(which, at):
            if which in comm.hooks:
                pl.when(step == at)(lambda: comm.hooks[which](cb, co, cs))

        hook("first", 0)
        hook("mid", steps // 2)
        body(*ins, *outs, *scr)
        hook("last", steps - 1)

    res = pl.pallas_call(
        wrapped, name=name, grid=(steps,), out_shape=out_shape + tuple(comm.outs),
        in_specs=list(in_specs) + [ANY] * nb, out_specs=out_specs + (ANY,) * no,
        scratch_shapes=scratch_shapes + comm.sems,
        input_output_aliases={n_in + i: n_out + o for i, o in comm.alias.items()},
        compiler_params=_cparams(("arbitrary",)),
    )(*args, *comm.bufs)
    return tuple(res[:n_out]), tuple(res[n_out:])


def gather_comm(bufs):
    n = len(bufs)

    def copies(outs, sems):
        s_own, r_own, s_fwd, r_fwd, s_sib, r_sib = sems
        x, y, c = _place()
        me = 2 * x + y
        nbr = ((1 - x, y), (x, 1 - y))
        nbr_id = (2 * (1 - x) + y, 2 * x + (1 - y))
        diag_id = 2 * (1 - x) + (1 - y)
        sib = (x, y, 1 - c)

        def rows(w, q=None):
            r = outs[w].shape[1]
            if q is None:
                return pl.ds(pl.multiple_of(c * (r // 2), 16), r // 2)
            return pl.ds(pl.multiple_of(c * (r // 2) + q * (r // 4), 16), r // 4)

        def own(w, j):
            piece = outs[w].at[me, rows(w)]
            return _remote(piece, piece, s_own.at[w, j], r_own.at[w, j], (nbr[j][0], nbr[j][1], c))

        def from_nbr(w, j):
            piece = outs[w].at[nbr_id[j], rows(w)]
            return _remote(piece, piece, s_own.at[w, j], r_own.at[w, j], (nbr[j][0], nbr[j][1], c))

        def fwd(w, j):
            piece = outs[w].at[nbr_id[j], rows(w, j)]
            return _remote(piece, piece, s_fwd.at[w, j], r_fwd.at[w, j], (nbr[1 - j][0], nbr[1 - j][1], c))

        def from_diag(w, j):
            piece = outs[w].at[diag_id, rows(w, j)]
            return _remote(piece, piece, s_fwd.at[w, j], r_fwd.at[w, j], (nbr[1 - j][0], nbr[1 - j][1], c))

        def to_sib(w, k):
            piece = (outs[w].at[nbr_id[k], rows(w)] if k < 2 else outs[w].at[diag_id, rows(w, k - 2)])
            return _remote(piece, piece, s_sib.at[w, k], r_sib.at[w, k], sib)

        def from_sib(w, k):
            r = outs[w].shape[1]
            if k < 2:
                piece = outs[w].at[nbr_id[k], pl.ds(pl.multiple_of((1 - c) * (r // 2), 16), r // 2)]
            else:
                piece = outs[w].at[diag_id, pl.ds(pl.multiple_of((1 - c) * (r // 2) + (k - 2) * (r // 4), 16), r // 4)]
            return _remote(piece, piece, s_sib.at[w, k], r_sib.at[w, k], sib)

        return own, from_nbr, fwd, from_diag, to_sib, from_sib

    def first(_, outs, sems):
        own = copies(outs, sems)[0]
        for w in range(n):
            own(w, 0).start()
            own(w, 1).start()

    def mid(_, outs, sems):
        _, from_nbr, fwd, _, to_sib, _ = copies(outs, sems)
        for w in range(n):
            for j in range(2):
                from_nbr(w, j).wait_recv()
                fwd(w, j).start()
                to_sib(w, j).start()

    def last(_, outs, sems):
        own, _, fwd, from_diag, to_sib, from_sib = copies(outs, sems)
        for w in range(n):
            for j in range(2):
                from_diag(w, j).wait_recv()
                to_sib(w, 2 + j).start()
        for w in range(n):
            for k in range(4):
                from_sib(w, k).wait_recv()
        for w in range(n):
            for j in range(2):
                own(w, j).wait_send()
                fwd(w, j).wait_send()
            for k in range(4):
                to_sib(w, k).wait_send()

    dma = pltpu.SemaphoreType.DMA
    return Comm(bufs, [jax.ShapeDtypeStruct(b.shape, b.dtype) for b in bufs], {w: w for w in range(n)},
                [dma((n, 2)), dma((n, 2)), dma((n, 2)), dma((n, 2)), dma((n, 4)), dma((n, 4))],
                {"first": first, "mid": mid, "last": last})


def _start_wait(make):
    def first(bufs, outs, sems):
        for cp in make(bufs, outs, sems):
            cp.start()

    def last(bufs, outs, sems):
        for cp in make(bufs, outs, sems):
            cp.wait()

    return {"first": first, "last": last}


def exchange_comm(grads):
    n = len(grads)

    def make(ins, outs, sems):
        x, y, c = _place()
        cps = []
        for w in range(n):
            half = ins[w].shape[1] // 2
            src = ins[w].at[:, pl.ds(pl.multiple_of((1 - c) * half, 8), half), :]
            cps.append(_remote(src, outs[w], sems[0].at[w], sems[1].at[w], (x, y, 1 - c)))
        return cps

    dma = pltpu.SemaphoreType.DMA
    return Comm(grads, [jax.ShapeDtypeStruct((N_SHARD, g.shape[1] // 2, g.shape[2]), g.dtype) for g in grads],
                {}, [dma((n,)), dma((n,))], _start_wait(make))


def scatter_comm(sums):
    n = len(sums)

    def make(ins, outs, sems):
        x, y, c = _place()
        chips = ((1 - x, y), (x, 1 - y), (1 - x, 1 - y))
        return [_remote(ins[w].at[2 * ch[0] + ch[1]], outs[w].at[j], sems[0].at[w, j], sems[1].at[w, j],
                        (ch[0], ch[1], c))
                for w in range(n) for j, ch in enumerate(chips)]

    dma = pltpu.SemaphoreType.DMA
    return Comm(sums, [jax.ShapeDtypeStruct((3,) + s.shape[1:], s.dtype) for s in sums],
                {}, [dma((n, 3)), dma((n, 3))], _start_wait(make))


def join_comm(shards):
    n = len(shards)

    def make(_, outs, sems):
        x, y, c = _place()
        cps = []
        for w in range(n):
            half = outs[w].shape[0] // 2
            mine = outs[w].at[pl.ds(pl.multiple_of(c * half, 8), half), :]
            cps.append(_remote(mine, mine, sems[0].at[w], sems[1].at[w], (x, y, 1 - c)))
        return cps

    dma = pltpu.SemaphoreType.DMA
    return Comm(shards, [jax.ShapeDtypeStruct(s.shape, s.dtype) for s in shards], {w: w for w in range(n)},
                [dma((n,)), dma((n,))], _start_wait(make))


def allreduce_small(vec):
    half = vec.shape[0] // 2

    def body(v_ref, o_ref, pair, chips_buf, s1, r1, s2, r2, s3, r3):
        x, y, c = _place()
        chip = 2 * x + y
        sib = (x, y, 1 - c)
        mine = pl.ds(pl.multiple_of(c * half, 8), half)
        other = pl.ds(pl.multiple_of((1 - c) * half, 8), half)
        to_sib = _remote(v_ref.at[other], pair, s1, r1, sib)
        to_sib.start()
        to_sib.wait()
        chips_buf[chip] = v_ref[mine, :] + pair[...]
        sends = [_remote(chips_buf.at[chip], chips_buf.at[chip], s2.at[j], r2.at[j], (ch[0], ch[1], c))
                 for j, ch in enumerate(((1 - x, y), (x, 1 - y), (1 - x, 1 - y)))]
        for cp in sends:
            cp.start()
        for cp in sends:
            cp.wait()
        o_ref[mine, :] = (chips_buf[0] + chips_buf[1]) + (chips_buf[2] + chips_buf[3])
        back = _remote(o_ref.at[mine], o_ref.at[mine], s3, r3, sib)
        back.start()
        back.wait()

    dma = pltpu.SemaphoreType.DMA
    return pl.pallas_call(
        body, name="allreduce_small",
        out_shape=jax.ShapeDtypeStruct(vec.shape, F32),
        in_specs=[pl.BlockSpec(memory_space=pltpu.VMEM)],
        out_specs=pl.BlockSpec(memory_space=pltpu.VMEM),
        scratch_shapes=[pltpu.VMEM((half, 128), F32), pltpu.VMEM((N_SHARD, half, 128), F32),
                        dma, dma, dma((3,)), dma((3,)), dma, dma],
        compiler_params=pltpu.CompilerParams(vmem_limit_bytes=VMEM_LIMIT),
    )(vec)


ROW_TILE = 128


def add_own_half(place, g, recv, name):
    _, r, cc = g.shape
    half = r // 2
    nb = half // ROW_TILE

    def body(p_ref, g_ref, r_ref, o_ref, ob_ref):
        s = g_ref[...] + r_ref[...]
        ob_ref[...] = s.astype(BF16)

        @pl.when(pl.program_id(1) == p_ref[0])
        def _():
            o_ref[...] = s

    blk = (None, ROW_TILE, cc)
    return pl.pallas_call(
        body, name=name,
        grid_spec=pltpu.PrefetchScalarGridSpec(
            num_scalar_prefetch=1, grid=(nb, N_SHARD),
            in_specs=[pl.BlockSpec(blk, lambda i, s, p_ref: (s, p_ref[1] * nb + i, 0)),
                      pl.BlockSpec(blk, lambda i, s, p_ref: (s, i, 0))],
            out_specs=(pl.BlockSpec((ROW_TILE, cc), lambda i, s, p_ref: (i, 0)),
                       pl.BlockSpec(blk, lambda i, s, p_ref: (s, i, 0)))),
        out_shape=(jax.ShapeDtypeStruct((half, cc), F32),
                   jax.ShapeDtypeStruct((N_SHARD, half, cc), BF16)),
        compiler_params=_cparams(("arbitrary", "arbitrary")),
    )(place, g, recv)


def add_chip_sums(place, own, recv, name):
    half, cc = own.shape
    nb = half // ROW_TILE

    def body(s_ref, o_ref, r_ref, out_ref):
        del s_ref
        acc = o_ref[...] + r_ref[0].astype(F32)
        acc = acc + r_ref[1].astype(F32)
        out_ref[...] = acc + r_ref[2].astype(F32)

    return pl.pallas_call(
        body, name=name,
        grid_spec=pltpu.PrefetchScalarGridSpec(
            num_scalar_prefetch=1, grid=(nb,),
            in_specs=[pl.BlockSpec((ROW_TILE, cc), lambda i, s_ref: (i, 0)),
                      pl.BlockSpec((3, ROW_TILE, cc), lambda i, s_ref: (0, i, 0))],
            out_specs=pl.BlockSpec((ROW_TILE, cc), lambda i, s_ref: (s_ref[1] * nb + i, 0))),
        out_shape=jax.ShapeDtypeStruct((2 * half, cc), F32),
        compiler_params=_cparams(("arbitrary",)),
    )(place, own, recv)


def adamw(w, m, v, g, name, copy_g=False):
    r, cc = w.shape
    tr = next(t for t in (256, 352, r) if r % t == 0)
    bc1 = 1.0 / (1.0 - ADAM_B1 ** ADAM_STEP)
    bc2 = 1.0 / (1.0 - ADAM_B2 ** ADAM_STEP)

    def body(w_ref, m_ref, v_ref, g_ref, d_ref, mo_ref, vo_ref, *go_ref):
        gv = g_ref[...]
        mn = ADAM_B1 * m_ref[...] + (1.0 - ADAM_B1) * gv
        vn = ADAM_B2 * v_ref[...] + (1.0 - ADAM_B2) * (gv * gv)
        mo_ref[...] = mn
        vo_ref[...] = vn
        d_ref[...] = -ADAM_LR * ((mn * bc1) / (jnp.sqrt(vn * bc2) + ADAM_EPS) + ADAM_WD * w_ref[...])
        if copy_g:
            go_ref[0][...] = gv

    blk = pl.BlockSpec((tr, cc), lambda i: (i, 0))
    shp = jax.ShapeDtypeStruct((r, cc), F32)
    nout = 4 if copy_g else 3
    return pl.pallas_call(
        body, name=name, grid=(r // tr,),
        out_shape=(shp,) * nout, in_specs=[blk] * 4, out_specs=(blk,) * nout,
        compiler_params=_cparams(("arbitrary",)),
    )(w, m, v, g)


GATHER_FIRST = ("ffn1_w_gate", "ffn1_w_up", "ffn1_w_down")
GATHER_ON = {"ffn1_fwd": ("w_in",),
             "inproj_fwd": ("s5_glu_val", "s5_glu_gate", "hg_w_out", "w_merge_out"),
             "s5_fwd": ("ffn2_w_gate", "ffn2_w_up"),
             "hgrn_fwd": ("ffn2_w_down", "ple_w_gate", "ple_w_proj")}
REDUCE = ((("ple_w_gate", "ple_w_proj", "ffn2_w_gate", "ffn2_w_up", "ffn2_w_down"), "merge_bwd", "hgrn_bwd"),
          (("w_merge_out", "s5_glu_val", "s5_glu_gate", "hg_w_out"), "s5_bwd", "inproj_bwd"),
          (("w_in",), None, "ffn1_bwd"),
          (("ffn1_w_gate",), "g_ffn1_w_up", "g_ffn1_w_down"),
          (("ffn1_w_up",), "g_ffn1_w_down", None),
          (("ffn1_w_down",), None, None))


def merge_comms(comms):
    if len(comms) == 1:
        return comms[0], [len(comms[0].outs)]
    bufs, outs, sems, alias, spans = [], [], [], {}, []
    for c in comms:
        spans.append((len(bufs), len(bufs) + len(c.bufs), len(outs), len(outs) + len(c.outs),
                      len(sems), len(sems) + len(c.sems)))
        alias.update({len(bufs) + i: len(outs) + o for i, o in c.alias.items()})
        bufs, outs, sems = bufs + c.bufs, outs + c.outs, sems + c.sems

    def hook(which):
        def run(b, o, s):
            for c, (b0, b1, o0, o1, s0, s1) in zip(comms, spans):
                if which in c.hooks:
                    c.hooks[which](b[b0:b1], o[o0:o1], s[s0:s1])
        return run

    hooks = {w: hook(w) for w in ("first", "mid", "last") if any(w in c.hooks for c in comms)}
    return Comm(bufs, outs, alias, sems, hooks), [len(c.outs) for c in comms]


class DistSchedule(Schedule):
    def __init__(self, w_rows, chip, core):
        first = gather_comm([_gather_buffer(k, w_rows[k], chip) for k in GATHER_FIRST])
        later = [k for k in BIG if k not in GATHER_FIRST]
        placed, gathered = place_shards([w_rows[k] for k in later], [BIG_SHARD[k][0] for k in later], first,
                                        "place_shards_gather_ffn1")
        super().__init__(zip(GATHER_FIRST, gathered))
        self.bufs = dict(zip(later, placed))
        self.place = jnp.stack([chip, core])
        self.sums, self.halves = {}, {}

    def _exchange(self, names):
        return exchange_comm([self.grads[k] for k in names])

    def _scatter(self, names):
        return scatter_comm([self.sums[k][1] for k in names])

    def _pair_sums(self, names, recv):
        for k, r in zip(names, recv):
            self.sums[k] = add_own_half(self.place, self.grads[k], r, "pair_sum_" + k)

    def _chip_sums(self, names, recv):
        for k, r in zip(names, recv):
            self.halves[k] = add_chip_sums(self.place, self.sums[k][0], r, "chip_sum_" + k)

    def before(self, kernel_name):
        comms, takers = [], []
        if kernel_name in GATHER_ON:
            names = GATHER_ON[kernel_name]
            comms.append(gather_comm([self.bufs[k] for k in names]))
            takers.append(lambda res, names=names: self.wts.update(zip(names, res)))
        for names, exchange_on, scatter_on in REDUCE:
            if kernel_name == exchange_on:
                comms.append(self._exchange(names))
                takers.append(lambda res, names=names: self._pair_sums(names, res))
            if kernel_name == scatter_on:
                if exchange_on is None:
                    self._pair_sums(names, run_comm(self._exchange(names), "exchange_" + names[0]))
                comms.append(self._scatter(names))
                takers.append(lambda res, names=names: self._chip_sums(names, res))
        if not comms:
            return None
        merged, counts = merge_comms(comms)
        self.pending = (takers, counts)
        return merged

    def after(self, kernel_name, results):
        if not results:
            return
        takers, counts = self.pending
        start = 0
        for take, count in zip(takers, counts):
            take(results[start:start + count])
            start += count

    def finish(self):
        tail = [names for names, _, scatter_on in REDUCE if scatter_on is None]
        alone = [k for names, exchange_on, scatter_on in REDUCE if scatter_on is None and exchange_on is None
                 for k in names]
        self._pair_sums(alone, run_comm(self._exchange(alone), "exchange_tail"))
        tail = [k for names in tail for k in names]
        early = [k for k in BIG if k not in tail]
        both, counts = merge_comms([self._scatter(tail), join_comm([self.halves[k] for k in early])])
        res = run_comm(both, "scatter_tail_join_early")
        self._chip_sums(tail, res[:counts[0]])
        full = dict(zip(early, res[counts[0]:]))
        full.update(zip(tail, run_comm(join_comm([self.halves[k] for k in tail]), "join_tail")))
        return full


SMALL = ("ffn1_norm", "mix_norm", "s5_lam_re", "s5_lam_im", "s5_log_dt", "s5_b_re", "s5_b_im", "s5_c_re",
         "s5_c_im", "s5_d", "hg_lower_bound", "hg_out_norm", "ffn2_norm", "ple_norm", "final_norm")
WEIGHTS = ("ffn1_norm", "ffn1_w_gate", "ffn1_w_up", "ffn1_w_down", "mix_norm", "w_in", "s5_lam_re", "s5_lam_im",
           "s5_log_dt", "s5_b_re", "s5_b_im", "s5_c_re", "s5_c_im", "s5_d", "s5_glu_val", "s5_glu_gate",
           "hg_lower_bound", "hg_out_norm", "hg_w_out", "w_merge_out", "ffn2_norm", "ffn2_w_gate", "ffn2_w_up",
           "ffn2_w_down", "ple_norm", "ple_w_gate", "ple_w_proj", "final_norm")


def _as_rows(name, w):
    return jnp.swapaxes(w[0], 0, 1) if name in FFN_T else w[0]


def _from_rows(name, w):
    return (jnp.swapaxes(w, 0, 1) if name in FFN_T else w)[None]


def _gather_buffer(name, w_rows, chip):
    r, c = BIG_SHARD[name]
    shard = jnp.pad(w_rows.astype(BF16), ((0, r - w_rows.shape[0]), (0, 0)))
    return lax.dynamic_update_slice(jnp.zeros((N_SHARD, r, c), BF16), shard[None], (chip, 0, 0))


def _pack(parts):
    flat = jnp.concatenate([jnp.zeros((128,), F32)] + [a.reshape(-1) for a in parts])
    rows = -(-flat.shape[0] // 2048) * 16
    return jnp.pad(flat, (0, rows * 128 - flat.shape[0])).reshape(rows, 128)


def _unpack(vec, likes):
    flat = vec.reshape(-1)
    out, off = [], 128
    for a in likes:
        out.append(flat[off:off + a.size].reshape(a.shape))
        off += a.size
    return out


def _small_view(name, w):
    if name.startswith("s5_") and name != "s5_d":
        return w[0]
    if name == "final_norm":
        return w.reshape(1, D_MODEL)
    return w


def kernel(x, p, ffn1_norm, ffn1_w_gate, ffn1_w_up, ffn1_w_down, mix_norm, w_in, s5_lam_re, s5_lam_im, s5_log_dt, s5_b_re, s5_b_im, s5_c_re, s5_c_im, s5_d, s5_glu_val, s5_glu_gate, hg_lower_bound, hg_out_norm, hg_w_out, w_merge_out, ffn2_norm, ffn2_w_gate, ffn2_w_up, ffn2_w_down, ple_norm, ple_w_gate, ple_w_proj, final_norm, loss_target, m_ffn1_norm, m_ffn1_w_gate, m_ffn1_w_up, m_ffn1_w_down, m_mix_norm, m_w_in, m_s5_lam_re, m_s5_lam_im, m_s5_log_dt, m_s5_b_re, m_s5_b_im, m_s5_c_re, m_s5_c_im, m_s5_d, m_s5_glu_val, m_s5_glu_gate, m_hg_lower_bound, m_hg_out_norm, m_hg_w_out, m_w_merge_out, m_ffn2_norm, m_ffn2_w_gate, m_ffn2_w_up, m_ffn2_w_down, m_ple_norm, m_ple_w_gate, m_ple_w_proj, m_final_norm, v_ffn1_norm, v_ffn1_w_gate, v_ffn1_w_up, v_ffn1_w_down, v_mix_norm, v_w_in, v_s5_lam_re, v_s5_lam_im, v_s5_log_dt, v_s5_b_re, v_s5_b_im, v_s5_c_re, v_s5_c_im, v_s5_d, v_s5_glu_val, v_s5_glu_gate, v_hg_lower_bound, v_hg_out_norm, v_hg_w_out, v_w_merge_out, v_ffn2_norm, v_ffn2_w_gate, v_ffn2_w_up, v_ffn2_w_down, v_ple_norm, v_ple_w_gate, v_ple_w_proj, v_final_norm):
    given = dict(locals())
    wv = {k: given[k] for k in WEIGHTS}
    mv = {k: given["m_" + k] for k in WEIGHTS}
    vv = {k: given["v_" + k] for k in WEIGHTS}

    core = lax.axis_index("c").astype(jnp.int32)
    chip = (2 * lax.axis_index("x") + lax.axis_index("y")).astype(jnp.int32)
    w_rows = {k: _as_rows(k, wv[k]) for k in BIG}
    sched = DistSchedule(w_rows, chip, core)
    sm = {k: _small_view(k, wv[k]) for k in SMALL}

    loss_blk, dx, gsm = local_step(x[0], p[0, 0], loss_target[0], sched, sm)
    full = sched.finish()

    small_likes = [wv[k] for k in SMALL]
    packed = _pack([gsm[k] for k in SMALL])
    packed = packed.at[0, 0].set(loss_blk[0, 0])
    total = allreduce_small(packed)
    loss = total[0, 0]
    gsmall = dict(zip(SMALL, _unpack(total, small_likes)))

    grads, deltas, new_m, new_v = {}, {}, {}, {}
    for k in BIG:
        padded = full[k].shape != w_rows[k].shape
        res = adamw(w_rows[k], _as_rows(k, mv[k]), _as_rows(k, vv[k]), full[k], "adamw_" + k, copy_g=padded)
        grads[k] = _from_rows(k, res[3] if padded else full[k])
        deltas[k], new_m[k], new_v[k] = (_from_rows(k, a) for a in res[:3])
    sw = _pack([wv[k] for k in SMALL])
    smm = _pack([mv[k] for k in SMALL])
    svv = _pack([vv[k] for k in SMALL])
    sd, smn, svn = adamw(sw, smm, svv, total, "adamw_small")
    for k, d, mn, vn in zip(SMALL, _unpack(sd, small_likes), _unpack(smn, small_likes), _unpack(svn, small_likes)):
        grads[k], deltas[k], new_m[k], new_v[k] = gsmall[k], d, mn, vn

    return (loss, dx[None], *[grads[k] for k in WEIGHTS], *[deltas[k] for k in WEIGHTS],
            *[new_m[k] for k in WEIGHTS], *[new_v[k] for k in WEIGHTS])
```

```python
import math

import jax
import jax.numpy as jnp
from jax import lax
from jax.experimental import pallas as pl
from jax.experimental.pallas import tpu as pltpu

F32 = jnp.float32
BF16 = jnp.bfloat16

D_MODEL = 1024
D_FF = 2816
N_SHARD = 4
FF_SHARD = D_FF // N_SHARD
FF_PAD = 768
NORM_EPS = 1e-6
PLE_DIM = 256

S5_WIDTH = 512
S5_GROUPS = 32
S5_GROUP = 16
S5_STATE = 64
S5_N = S5_GROUPS * S5_STATE
S5_KT = 2

HG_HEADS = 8
HG_E = 128
HG_WIDTH = 1024
CHUNK = 64
HG_SUB = 4
IN_COLS = S5_WIDTH + 4 * HG_WIDTH + 2 * D_MODEL
IN_SPLITS = (0, 512, 1536, 2560, 3584, 4608, 5632, 6656)

ADAM_LR = 0.001
ADAM_B1 = 0.9
ADAM_B2 = 0.999
ADAM_EPS = 1e-08
ADAM_WD = 0.01
ADAM_STEP = 10

VMEM_LIMIT = 60 * 1024 * 1024
HIGHEST = lax.Precision.HIGHEST


def _cparams(sem=None, **kw):
    return pltpu.CompilerParams(dimension_semantics=sem, vmem_limit_bytes=VMEM_LIMIT, **kw)


def _const_spec(shape):
    nd = len(shape)
    return pl.BlockSpec(shape, lambda *_: (0,) * nd, pipeline_mode=pl.Buffered(1))


def _dot(a, b):
    return jnp.dot(a, b, preferred_element_type=F32)


def _dot_nt(a, b):
    return lax.dot_general(a, b, (((1,), (1,)), ((), ())), preferred_element_type=F32)


def _dot_tn(a, b):
    return lax.dot_general(a, b, (((0,), (0,)), ((), ())), preferred_element_type=F32)


def _sigmoid(x):
    return 1.0 / (1.0 + jnp.exp(-x))


def _rms_fwd(x, g):
    r = lax.rsqrt(jnp.mean(x * x, axis=-1, keepdims=True) + NORM_EPS)
    return x * r * g, r


def _rms_bwd(x, r, g, dy):
    xh = x * r
    dyg = dy * g
    m = jnp.mean(dyg * xh, axis=-1, keepdims=True)
    return r * (dyg - xh * m), jnp.sum(dy * xh, axis=0, keepdims=True)


def _accum(ref, val, first):
    @pl.when(first)
    def _():
        ref[...] = val

    @pl.when(jnp.logical_not(first))
    def _():
        ref[...] += val


def ffn_fwd(h, gain, wg, wu, wd, name, comm=None, tm=512):
    t = h.shape[0]

    def body(h_ref, g_ref, wg_ref, wu_ref, wd_ref, o_ref, a_ref, b_ref):
        hv = h_ref[...]
        n, _ = _rms_fwd(hv, g_ref[...])
        nb = n.astype(BF16)
        acc = jnp.zeros((tm, D_MODEL), F32)
        for s in range(N_SHARD):
            a = _dot_nt(nb, wg_ref[s])
            b = _dot_nt(nb, wu_ref[s])
            a_ref[s] = a.astype(BF16)
            b_ref[s] = b.astype(BF16)
            sv = (a * _sigmoid(a) * b).astype(BF16)
            acc = acc + _dot(sv, wd_ref[s])
        o_ref[...] = hv + 0.5 * acc

    return _carry(
        body, comm, name=name, steps=t // tm,
        out_shape=(jax.ShapeDtypeStruct((t, D_MODEL), F32),
                   jax.ShapeDtypeStruct((N_SHARD, t, FF_PAD), BF16),
                   jax.ShapeDtypeStruct((N_SHARD, t, FF_PAD), BF16)),
        in_specs=[pl.BlockSpec((tm, D_MODEL), lambda i: (i, 0)),
                  _const_spec((1, D_MODEL)),
                  _const_spec((N_SHARD, FF_PAD, D_MODEL)),
                  _const_spec((N_SHARD, FF_PAD, D_MODEL)),
                  _const_spec((N_SHARD, FF_PAD, D_MODEL))],
        out_specs=(pl.BlockSpec((tm, D_MODEL), lambda i: (i, 0)),
                   pl.BlockSpec((N_SHARD, tm, FF_PAD), lambda i: (0, i, 0)),
                   pl.BlockSpec((N_SHARD, tm, FF_PAD), lambda i: (0, i, 0))),
        args=(h, gain, wg, wu, wd),
    )


def ffn_bwd(dho, h, a, b, gain, wg, wu, wd, name, comm=None, tm=256):
    t = h.shape[0]

    def body(dho_ref, h_ref, a_ref, b_ref, g_ref, wg_ref, wu_ref, wd_ref,
             dh_ref, dg_ref, nb_ref, dhb_ref, da_ref, db_ref, s_ref):
        hv = h_ref[...]
        g = g_ref[...]
        n, r = _rms_fwd(hv, g)
        nb_ref[...] = n.astype(BF16)
        dhalf = (0.5 * dho_ref[...]).astype(BF16)
        dhb_ref[...] = dhalf
        dn = jnp.zeros((tm, D_MODEL), F32)
        for s in range(N_SHARD):
            av = a_ref[s].astype(F32)
            bv = b_ref[s].astype(F32)
            sg = _sigmoid(av)
            sil = av * sg
            s_ref[s] = (sil * bv).astype(BF16)
            ds = _dot_nt(dhalf, wd_ref[s])
            da = (ds * bv * (sg * (1.0 + av * (1.0 - sg)))).astype(BF16)
            db = (ds * sil).astype(BF16)
            da_ref[s] = da
            db_ref[s] = db
            dn = dn + _dot(da, wg_ref[s]) + _dot(db, wu_ref[s])
        dx, dg = _rms_bwd(hv, r, g, dn)
        dh_ref[...] = dho_ref[...] + dx
        _accum(dg_ref, dg, pl.program_id(0) == 0)

    tok = pl.BlockSpec((tm, D_MODEL), lambda i: (i, 0))
    hid = pl.BlockSpec((N_SHARD, tm, FF_PAD), lambda i: (0, i, 0))
    return _carry(
        body, comm, name=name, steps=t // tm,
        out_shape=(jax.ShapeDtypeStruct((t, D_MODEL), F32),
                   jax.ShapeDtypeStruct((1, D_MODEL), F32),
                   jax.ShapeDtypeStruct((t, D_MODEL), BF16),
                   jax.ShapeDtypeStruct((t, D_MODEL), BF16),
                   jax.ShapeDtypeStruct((N_SHARD, t, FF_PAD), BF16),
                   jax.ShapeDtypeStruct((N_SHARD, t, FF_PAD), BF16),
                   jax.ShapeDtypeStruct((N_SHARD, t, FF_PAD), BF16)),
        in_specs=[tok, tok, hid, hid, _const_spec((1, D_MODEL)),
                  _const_spec((N_SHARD, FF_PAD, D_MODEL)),
                  _const_spec((N_SHARD, FF_PAD, D_MODEL)),
                  _const_spec((N_SHARD, FF_PAD, D_MODEL))],
        out_specs=(tok, pl.BlockSpec((1, D_MODEL), lambda i: (0, 0)), tok, tok, hid, hid, hid),
        args=(dho, h, a, b, gain, wg, wu, wd),
    )


TN_VMEM_BUDGET = 44 * 1024 * 1024


def tn_matmul(x, y, name, shard, comm=None):
    x3, y3 = x.ndim == 3, y.ndim == 3
    t = x.shape[-2]
    m = x.shape[-1] // (N_SHARD if (shard == "rows" and not x3) else 1)
    n = y.shape[-1] // (N_SHARD if (shard == "cols" and not y3) else 1)
    per_token = 2 * (m * x.dtype.itemsize + n * y.dtype.itemsize)
    tk = t
    while tk > 512 and tk * per_token + 2 * m * n * 4 > TN_VMEM_BUDGET:
        tk //= 2
    nk = t // tk

    out_shape = jax.ShapeDtypeStruct((N_SHARD, m, n), F32)
    if nk == 1:
        def whole(x_ref, y_ref, o_ref):
            o_ref[...] = _dot_tn(x_ref[...].astype(BF16), y_ref[...].astype(BF16))

        x_one = (pl.BlockSpec((None, t, m), lambda s: (s, 0, 0)) if x3 else
                 pl.BlockSpec((t, m), (lambda s: (0, s)) if shard == "rows" else (lambda s: (0, 0))))
        y_one = (pl.BlockSpec((None, t, n), lambda s: (s, 0, 0)) if y3 else
                 pl.BlockSpec((t, n), (lambda s: (0, s)) if shard == "cols" else (lambda s: (0, 0))))
        return _carry(whole, comm, name=name, steps=N_SHARD, out_shape=(out_shape,), in_specs=[x_one, y_one],
                      out_specs=(pl.BlockSpec((None, m, n), lambda s: (s, 0, 0)),), args=(x, y))
    assert comm is None

    def body(x_ref, y_ref, o_ref):
        _accum(o_ref, _dot_tn(x_ref[...].astype(BF16), y_ref[...].astype(BF16)), pl.program_id(1) == 0)

    if x3:
        x_spec = pl.BlockSpec((None, tk, m), lambda s, k: (s, k, 0))
    elif shard == "rows":
        x_spec = pl.BlockSpec((tk, m), lambda s, k: (k, s))
    else:
        x_spec = pl.BlockSpec((tk, m), lambda s, k: (k, 0))
    if y3:
        y_spec = pl.BlockSpec((None, tk, n), lambda s, k: (s, k, 0))
    elif shard == "cols":
        y_spec = pl.BlockSpec((tk, n), lambda s, k: (k, s))
    else:
        y_spec = pl.BlockSpec((tk, n), lambda s, k: (k, 0))
    res = pl.pallas_call(
        body, name=name, grid=(N_SHARD, nk),
        out_shape=out_shape,
        in_specs=[x_spec, y_spec],
        out_specs=pl.BlockSpec((None, m, n), lambda s, k: (s, 0, 0)),
        compiler_params=_cparams(("arbitrary", "arbitrary")),
    )(x, y)
    return (res,), ()


def inproj_fwd(h, gain, w_in, comm=None, tm=256):
    t = h.shape[0]
    widths = [IN_SPLITS[j + 1] - IN_SPLITS[j] for j in range(7)]
    sh_cols = IN_COLS // N_SHARD

    def body(h_ref, g_ref, w_ref, *outs):
        n, _ = _rms_fwd(h_ref[...], g_ref[...])
        nb = n.astype(BF16)
        proj = jnp.concatenate([_dot(nb, w_ref[s]) for s in range(N_SHARD)], axis=1)
        for j, o_ref in enumerate(outs):
            o_ref[...] = proj[:, IN_SPLITS[j]:IN_SPLITS[j + 1]]

    return _carry(
        body, comm, name="inproj_fwd", steps=t // tm,
        out_shape=tuple(jax.ShapeDtypeStruct((t, w), F32) for w in widths),
        in_specs=[pl.BlockSpec((tm, D_MODEL), lambda i: (i, 0)),
                  _const_spec((1, D_MODEL)),
                  _const_spec((N_SHARD, D_MODEL, sh_cols))],
        out_specs=tuple(pl.BlockSpec((tm, w), lambda i: (i, 0)) for w in widths),
        args=(h, gain, w_in),
    )


def inproj_bwd(dres, h, gain, w_in, dparts, comm=None, tm=256):
    t = h.shape[0]
    widths = [IN_SPLITS[j + 1] - IN_SPLITS[j] for j in range(7)]
    sh_cols = IN_COLS // N_SHARD

    def body(dres_ref, h_ref, g_ref, w_ref, d0, d1, d2, d3, d4, d5, d6, dh_ref, dg_ref, nb_ref, dp_ref):
        hv = h_ref[...]
        g = g_ref[...]
        n, r = _rms_fwd(hv, g)
        nb_ref[...] = n.astype(BF16)
        dproj = jnp.concatenate([d[...] for d in (d0, d1, d2, d3, d4, d5, d6)], axis=1).astype(BF16)
        dp_ref[...] = dproj
        dn = jnp.zeros((tm, D_MODEL), F32)
        for s in range(N_SHARD):
            dn = dn + _dot_nt(dproj[:, s * sh_cols:(s + 1) * sh_cols], w_ref[s])
        dx, dg = _rms_bwd(hv, r, g, dn)
        dh_ref[...] = dres_ref[...] + dx
        _accum(dg_ref, dg, pl.program_id(0) == 0)

    tok = pl.BlockSpec((tm, D_MODEL), lambda i: (i, 0))
    return _carry(
        body, comm, name="inproj_bwd", steps=t // tm,
        out_shape=(jax.ShapeDtypeStruct((t, D_MODEL), F32),
                   jax.ShapeDtypeStruct((1, D_MODEL), F32),
                   jax.ShapeDtypeStruct((t, D_MODEL), BF16),
                   jax.ShapeDtypeStruct((t, IN_COLS), BF16)),
        in_specs=[tok, tok, _const_spec((1, D_MODEL)), _const_spec((N_SHARD, D_MODEL, sh_cols))]
                 + [pl.BlockSpec((tm, w), lambda i: (i, 0)) for w in widths],
        out_specs=(tok, pl.BlockSpec((1, D_MODEL), lambda i: (0, 0)), tok,
                   pl.BlockSpec((tm, IN_COLS), lambda i: (i, 0))),
        args=(dres, h, gain, w_in, *dparts),
    )


def s5_prep(lam_re, lam_im, log_dt, b_re, b_im, c_re, c_im):
    dt = jnp.exp(log_dt)[:, None]
    mag = jnp.exp(lam_re * dt)
    lbr = mag * jnp.cos(lam_im * dt)
    lbi = mag * jnp.sin(lam_im * dt)
    den = lam_re * lam_re + lam_im * lam_im
    nr, ni = lbr - 1.0, lbi
    kr = (nr * lam_re + ni * lam_im) / den
    ki = (ni * lam_re - nr * lam_im) / den
    bbr = kr[..., None] * b_re - ki[..., None] * b_im
    bbi = kr[..., None] * b_im + ki[..., None] * b_re
    eye = jnp.eye(16, dtype=F32)

    def bm(bp):
        return jnp.einsum('kgph,gG->kghGp', bp.reshape(S5_KT, 16, S5_STATE, S5_GROUP), eye).reshape(S5_KT, 256, 1024)

    def cm(cp):
        return jnp.einsum('kghp,gG->kgpGh', cp.reshape(S5_KT, 16, S5_GROUP, S5_STATE), eye).reshape(S5_KT, 1024, 256)

    lam_bar = jnp.stack([lbr.reshape(S5_N), lbi.reshape(S5_N)])
    bmat = jnp.stack([bm(bbr), bm(bbi)])
    cmat = jnp.stack([cm(c_re), -cm(c_im)])
    return lam_bar, bmat, cmat


def _lam_powers(lam_bar):
    lr, li = lam_bar[0], lam_bar[1]
    pr, pi = [lr], [li]
    for _ in range(7):
        pr, pi = pr + [pr[-1] * lr - pi[-1] * li], pi + [pr[-1] * li + pi[-1] * lr]
    return jnp.stack(pr), jnp.stack(pi)


SCAN_SHIFTS = ((1, 0), (2, 1), (4, 3))


def _scan_tables(pw_r, pw_i, reverse):
    rows = jnp.arange(8)[:, None]
    planes_r, planes_i = [], []
    for sh, idx in SCAN_SHIFTS:
        keep = (rows < 8 - sh) if reverse else (rows >= sh)
        planes_r.append(jnp.where(keep, pw_r[idx:idx + 1], 0.0))
        planes_i.append(jnp.where(keep, pw_i[idx:idx + 1], 0.0))
    carry = [pw_r[::-1], pw_i[::-1]] if reverse else [pw_r, pw_i]
    return jnp.stack(planes_r + planes_i + carry)


def s5_fwd(u, tab, bmat, cmat, dvec, comm=None, tm=256):
    t = u.shape[0]
    nch = tm // 8

    def body(u_ref, tab_ref, b_ref, c_ref, d_ref, y_ref, xp_ref, x_scr, carry):
        @pl.when(pl.program_id(0) == 0)
        def _():
            carry[...] = jnp.zeros_like(carry)

        uv = u_ref[...]
        ub = uv.astype(BF16)
        for part in range(2):
            for kt in range(S5_KT):
                x_scr[:, pl.ds(part * S5_N + kt * 1024, 1024)] = _dot(ub[:, kt * 256:(kt + 1) * 256], b_ref[part, kt])
        row = lax.broadcasted_iota(jnp.int32, (8, S5_N), 0)

        def chunk(i, c):
            cr, ci = c
            r0 = pl.multiple_of(i * 8, 8)
            xr = x_scr[pl.ds(r0, 8), pl.ds(0, S5_N)]
            xi = x_scr[pl.ds(r0, 8), pl.ds(S5_N, S5_N)]
            for lvl, (sh, _) in enumerate(SCAN_SHIFTS):
                sr = pltpu.roll(xr, sh, 0)
                si = pltpu.roll(xi, sh, 0)
                lr = tab_ref[lvl]
                li = tab_ref[3 + lvl]
                xr, xi = xr + lr * sr - li * si, xi + lr * si + li * sr
            pwr = tab_ref[6]
            pwi = tab_ref[7]
            xr, xi = xr + pwr * cr - pwi * ci, xi + pwr * ci + pwi * cr
            x_scr[pl.ds(r0, 8), pl.ds(0, S5_N)] = xr
            x_scr[pl.ds(r0, 8), pl.ds(S5_N, S5_N)] = xi
            xp_ref[pl.ds(r0, 8), pl.ds(0, S5_N)] = jnp.where(row == 0, cr, pltpu.roll(xr, 1, 0))
            xp_ref[pl.ds(r0, 8), pl.ds(S5_N, S5_N)] = jnp.where(row == 0, ci, pltpu.roll(xi, 1, 0))
            return xr[7:8, :], xi[7:8, :]

        cr, ci = lax.fori_loop(0, nch, chunk, (carry[0:1, :], carry[1:2, :]))
        carry[0:1, :] = cr
        carry[1:2, :] = ci
        for kt in range(S5_KT):
            acc = jnp.zeros((tm, 256), F32)
            for part in range(2):
                acc = acc + _dot(x_scr[:, pl.ds(part * S5_N + kt * 1024, 1024)].astype(BF16), c_ref[part, kt])
            y_ref[:, pl.ds(kt * 256, 256)] = acc + d_ref[:, pl.ds(kt * 256, 256)] * uv[:, kt * 256:(kt + 1) * 256]

    return _carry(
        body, comm, name="s5_fwd", steps=t // tm,
        out_shape=(jax.ShapeDtypeStruct((t, S5_WIDTH), F32),
                   jax.ShapeDtypeStruct((t, 2 * S5_N), F32)),
        in_specs=[pl.BlockSpec((tm, S5_WIDTH), lambda i: (i, 0)),
                  _const_spec((8, 8, S5_N)),
                  _const_spec((2, S5_KT, 256, 1024)), _const_spec((2, S5_KT, 1024, 256)),
                  _const_spec((1, S5_WIDTH))],
        out_specs=(pl.BlockSpec((tm, S5_WIDTH), lambda i: (i, 0)),
                   pl.BlockSpec((tm, 2 * S5_N), lambda i: (i, 0))),
        scratch_shapes=[pltpu.VMEM((tm, 2 * S5_N), F32), pltpu.VMEM((8, S5_N), F32)],
        args=(u, tab, bmat, cmat, dvec),
    )


def s5_bwd(dy, u, xp, tab, bmat, cmat, dvec, comm=None, tm=256):
    t = u.shape[0]
    nt = t // tm
    nch = tm // 8

    def body(dy_ref, u_ref, xp_ref, tab_ref, b_ref, c_ref, d_ref,
             du_ref, db_ref, dc_ref, dl_ref, dd_ref, g_scr, x_scr, carry):
        first = pl.program_id(0) == 0

        @pl.when(first)
        def _():
            carry[...] = jnp.zeros_like(carry)
            dl_ref[...] = jnp.zeros_like(dl_ref)

        dyv = dy_ref[...]
        uv = u_ref[...]
        dyb = dyv.astype(BF16)
        ub = uv.astype(BF16)
        lr1 = tab_ref[6, 7:8, :]
        li1 = tab_ref[7, 7:8, :]
        for kt in range(S5_KT):
            cols = pl.ds(kt * 1024, 1024)
            colsi = pl.ds(S5_N + kt * 1024, 1024)
            g_scr[:, cols] = _dot_nt(dyb[:, kt * 256:(kt + 1) * 256], c_ref[0, kt])
            g_scr[:, colsi] = _dot_nt(dyb[:, kt * 256:(kt + 1) * 256], c_ref[1, kt])
            bur = _dot(ub[:, kt * 256:(kt + 1) * 256], b_ref[0, kt])
            bui = _dot(ub[:, kt * 256:(kt + 1) * 256], b_ref[1, kt])
            xpr = xp_ref[:, cols]
            xpi = xp_ref[:, colsi]
            lrk = lr1[:, kt * 1024:(kt + 1) * 1024]
            lik = li1[:, kt * 1024:(kt + 1) * 1024]
            x_scr[:, cols] = lrk * xpr - lik * xpi + bur
            x_scr[:, colsi] = lrk * xpi + lik * xpr + bui

        def chunk(j, c):
            cr, ci = c
            r0 = pl.multiple_of((nch - 1 - j) * 8, 8)
            gr = g_scr[pl.ds(r0, 8), pl.ds(0, S5_N)]
            gi = g_scr[pl.ds(r0, 8), pl.ds(S5_N, S5_N)]
            for lvl, (sh, _) in enumerate(SCAN_SHIFTS):
                sr = pltpu.roll(gr, 8 - sh, 0)
                si = pltpu.roll(gi, 8 - sh, 0)
                lr = tab_ref[lvl]
                li = tab_ref[3 + lvl]
                gr, gi = gr + lr * sr + li * si, gi + lr * si - li * sr
            pvr = tab_ref[6]
            pvi = tab_ref[7]
            gr, gi = gr + pvr * cr + pvi * ci, gi + pvr * ci - pvi * cr
            g_scr[pl.ds(r0, 8), pl.ds(0, S5_N)] = gr
            g_scr[pl.ds(r0, 8), pl.ds(S5_N, S5_N)] = gi
            xpr = xp_ref[pl.ds(r0, 8), pl.ds(0, S5_N)]
            xpi = xp_ref[pl.ds(r0, 8), pl.ds(S5_N, S5_N)]
            dl_ref[0] += gr * xpr + gi * xpi
            dl_ref[1] += gi * xpr - gr * xpi
            return gr[0:1, :], gi[0:1, :]

        cr, ci = lax.fori_loop(0, nch, chunk, (carry[0:1, :], carry[1:2, :]))
        carry[0:1, :] = cr
        carry[1:2, :] = ci

        for kt in range(S5_KT):
            du = jnp.zeros((tm, 256), F32)
            ukt = ub[:, kt * 256:(kt + 1) * 256]
            dykt = dyb[:, kt * 256:(kt + 1) * 256]
            for part in range(2):
                gb = g_scr[:, pl.ds(part * S5_N + kt * 1024, 1024)].astype(BF16)
                xb = x_scr[:, pl.ds(part * S5_N + kt * 1024, 1024)].astype(BF16)
                du = du + _dot_nt(gb, b_ref[part, kt])
                dbv = _dot_tn(ukt, gb)
                dcv = _dot_tn(xb, dykt)

                @pl.when(first)
                def _():
                    db_ref[part, kt] = dbv
                    dc_ref[part, kt] = dcv

                @pl.when(jnp.logical_not(first))
                def _():
                    db_ref[part, kt] += dbv
                    dc_ref[part, kt] += dcv
            du_ref[:, pl.ds(kt * 256, 256)] = du + d_ref[:, pl.ds(kt * 256, 256)] * dyv[:, kt * 256:(kt + 1) * 256]
        _accum(dd_ref, jnp.sum(dyv * uv, axis=0, keepdims=True), first)

    rev = lambda i: (nt - 1 - i, 0)
    return _carry(
        body, comm, name="s5_bwd", steps=nt,
        out_shape=(jax.ShapeDtypeStruct((t, S5_WIDTH), F32),
                   jax.ShapeDtypeStruct((2, S5_KT, 256, 1024), F32),
                   jax.ShapeDtypeStruct((2, S5_KT, 1024, 256), F32),
                   jax.ShapeDtypeStruct((2, 8, S5_N), F32),
                   jax.ShapeDtypeStruct((1, S5_WIDTH), F32)),
        in_specs=[pl.BlockSpec((tm, S5_WIDTH), rev), pl.BlockSpec((tm, S5_WIDTH), rev),
                  pl.BlockSpec((tm, 2 * S5_N), rev),
                  _const_spec((8, 8, S5_N)),
                  _const_spec((2, S5_KT, 256, 1024)), _const_spec((2, S5_KT, 1024, 256)),
                  _const_spec((1, S5_WIDTH))],
        out_specs=(pl.BlockSpec((tm, S5_WIDTH), rev),
                   pl.BlockSpec((2, S5_KT, 256, 1024), lambda i: (0, 0, 0, 0)),
                   pl.BlockSpec((2, S5_KT, 1024, 256), lambda i: (0, 0, 0, 0)),
                   pl.BlockSpec((2, 8, S5_N), lambda i: (0, 0, 0)),
                   pl.BlockSpec((1, S5_WIDTH), lambda i: (0, 0))),
        scratch_shapes=[pltpu.VMEM((tm, 2 * S5_N), F32), pltpu.VMEM((tm, 2 * S5_N), F32),
                        pltpu.VMEM((8, S5_N), F32)],
        args=(dy, u, xp, tab, bmat, cmat, dvec),
    )


def _hg_gates(z, lb):
    sg = _sigmoid(z)
    sgn = _sigmoid(-z)
    fg = lb + (1.0 - lb) * sg
    return sg, sgn, fg, jnp.log(fg), (1.0 - lb) * sgn


def _hg_decays(g, tril):
    gc = jnp.dot(tril, g, precision=HIGHEST, preferred_element_type=F32)
    mid = gc[CHUNK // 2 - 1:CHUNK // 2, :]
    last = gc[CHUNK - 1:CHUNK, :]
    return jnp.exp(gc), jnp.exp(gc - mid), jnp.exp(mid - gc), jnp.exp(last - gc), jnp.exp(last)


def _split_bf16(x):
    hi = x.astype(BF16)
    return hi, (x - hi.astype(F32)).astype(BF16)


def _hg_scores(qt, qlo, kt, klo, sl, causal):
    a = _dot_nt(qt[:, sl], kt[:, sl]) + _dot_nt(qt[:, sl], klo[:, sl]) + _dot_nt(qlo[:, sl], kt[:, sl])
    return jnp.where(causal, a, 0.0).astype(BF16)


def hgrn_fwd(q, f, v, lb, comm=None):
    t = q.shape[0]
    nc = t // CHUNK
    scale = HG_E ** -0.5

    def body(q_ref, f_ref, v_ref, lb_ref, o_ref, st_ref, state):
        @pl.when(pl.program_id(0) == 0)
        def _():
            state[...] = jnp.zeros_like(state)

        ri = lax.broadcasted_iota(jnp.int32, (CHUNK, CHUNK), 0)
        ci = lax.broadcasted_iota(jnp.int32, (CHUNK, CHUNK), 1)
        causal = ri >= ci
        tril = causal.astype(F32)
        for sub in range(HG_SUB):
            rows = pl.ds(sub * CHUNK, CHUNK)
            _, _, _, g, k = _hg_gates(f_ref[rows, :], lb_ref[...])
            eg, eq, ek, ed, el = _hg_decays(g, tril)
            qs = q_ref[rows, :] * scale
            qg = (qs * eg).astype(BF16)
            qt, qlo = _split_bf16(qs * eq)
            kt, klo = _split_bf16(k * ek)
            kd = (k * ed).astype(BF16)
            vb = v_ref[rows, :].astype(BF16)
            for h in range(HG_HEADS):
                sl = slice(h * HG_E, (h + 1) * HG_E)
                st = state[h]
                a = _hg_scores(qt, qlo, kt, klo, sl, causal)
                o_ref[rows, sl] = _dot(a, vb[:, sl]) + _dot_nt(qg[:, sl], st.astype(BF16))
                st_new = st * el[:, sl] + _dot_tn(vb[:, sl], kd[:, sl])
                state[h] = st_new
                st_ref[sub, h] = st_new

    tok = pl.BlockSpec((HG_SUB * CHUNK, HG_WIDTH), lambda i: (i, 0))
    return _carry(
        body, comm, name="hgrn_fwd", steps=nc // HG_SUB,
        out_shape=(jax.ShapeDtypeStruct((t, HG_WIDTH), F32),
                   jax.ShapeDtypeStruct((nc, HG_HEADS, HG_E, HG_E), F32)),
        in_specs=[tok, tok, tok, _const_spec((1, HG_WIDTH))],
        out_specs=(tok, pl.BlockSpec((HG_SUB, HG_HEADS, HG_E, HG_E), lambda i: (i, 0, 0, 0))),
        scratch_shapes=[pltpu.VMEM((HG_HEADS, HG_E, HG_E), F32)],
        args=(q, f, v, lb),
    )


def hgrn_bwd(do, q, f, v, lb, states, comm=None):
    t = q.shape[0]
    nc = t // CHUNK
    scale = HG_E ** -0.5

    ns = nc // HG_SUB

    def body(do_ref, q_ref, f_ref, v_ref, lb_ref, scur_ref, sprev_ref, dq_ref, df_ref, dv_ref, dlb_ref, dstate):
        first = pl.program_id(0) == 0
        has_prev = jnp.where(pl.program_id(0) < ns - 1, 1.0, 0.0)

        @pl.when(first)
        def _():
            dstate[...] = jnp.zeros_like(dstate)

        ri = lax.broadcasted_iota(jnp.int32, (CHUNK, CHUNK), 0)
        ci = lax.broadcasted_iota(jnp.int32, (CHUNK, CHUNK), 1)
        causal = ri >= ci
        tril = causal.astype(F32)
        triu = (ri <= ci).astype(F32)
        rowc = lax.broadcasted_iota(jnp.int32, (CHUNK, HG_WIDTH), 0)
        lb = lb_ref[...]
        dlb = jnp.zeros((1, HG_WIDTH), F32)
        for sub in reversed(range(HG_SUB)):
            rows = pl.ds(sub * CHUNK, CHUNK)
            sg, sgn, fg, g, k = _hg_gates(f_ref[rows, :], lb)
            eg, eq, ek, ed, el = _hg_decays(g, tril)
            qs = q_ref[rows, :] * scale
            qg = (qs * eg).astype(BF16)
            qt, qlo = _split_bf16(qs * eq)
            kt, klo = _split_bf16(k * ek)
            kd = (k * ed).astype(BF16)
            vb = v_ref[rows, :].astype(BF16)
            dob = do_ref[rows, :].astype(BF16)
            dqs_l, dk_l, dgc_l, dgl_l = [], [], [], []
            for h in range(HG_HEADS):
                sl = slice(h * HG_E, (h + 1) * HG_E)
                s0 = scur_ref[sub - 1, h] if sub > 0 else sprev_ref[HG_SUB - 1, h] * has_prev
                ds1 = dstate[h]
                ds1b = ds1.astype(BF16)
                a = _hg_scores(qt, qlo, kt, klo, sl, causal)
                da = jnp.where(causal, _dot_nt(dob[:, sl], vb[:, sl]), 0.0).astype(BF16)
                dv_ref[rows, sl] = _dot_tn(a, dob[:, sl]) + _dot_nt(kd[:, sl], ds1b)
                dkd = _dot(vb[:, sl], ds1b)
                dqt = _dot(da, kt[:, sl])
                dkt = _dot_tn(da, qt[:, sl])
                dqg = _dot(dob[:, sl], s0.astype(BF16))
                dqs_l.append(dqt * eq[:, sl] + dqg * eg[:, sl])
                dk_l.append(dkt * ek[:, sl] + dkd * ed[:, sl])
                kd_dkd = kd[:, sl].astype(F32) * dkd
                dgc_l.append(qt[:, sl].astype(F32) * dqt - kt[:, sl].astype(F32) * dkt
                             + qg[:, sl].astype(F32) * dqg - kd_dkd)
                dgl_l.append(el[:, sl] * jnp.sum(ds1 * s0, axis=0, keepdims=True)
                             + jnp.sum(kd_dkd, axis=0, keepdims=True))
                dstate[h] = ds1 * el[:, sl] + _dot_tn(dob[:, sl], qg[:, sl])
            dqs = jnp.concatenate(dqs_l, axis=1)
            dk = jnp.concatenate(dk_l, axis=1)
            dgl = jnp.concatenate(dgl_l, axis=1)
            dq_ref[rows, :] = dqs * scale
            dgc = jnp.concatenate(dgc_l, axis=1) + jnp.where(rowc == CHUNK - 1, dgl, 0.0)
            dg = jnp.dot(triu, dgc, precision=HIGHEST, preferred_element_type=F32)
            w = dg / fg - dk
            df_ref[rows, :] = w * (1.0 - lb) * sg * sgn
            dlb = dlb + jnp.sum(w * sgn, axis=0, keepdims=True)
        _accum(dlb_ref, dlb, first)

    rev = lambda i: (ns - 1 - i, 0)
    tok = pl.BlockSpec((HG_SUB * CHUNK, HG_WIDTH), rev)
    st_blk = (HG_SUB, HG_HEADS, HG_E, HG_E)
    return _carry(
        body, comm, name="hgrn_bwd", steps=ns,
        out_shape=(jax.ShapeDtypeStruct((t, HG_WIDTH), F32),
                   jax.ShapeDtypeStruct((t, HG_WIDTH), F32),
                   jax.ShapeDtypeStruct((t, HG_WIDTH), F32),
                   jax.ShapeDtypeStruct((1, HG_WIDTH), F32)),
        in_specs=[tok, tok, tok, tok, _const_spec((1, HG_WIDTH)),
                  pl.BlockSpec(st_blk, lambda i: (ns - 1 - i, 0, 0, 0)),
                  pl.BlockSpec(st_blk, lambda i: (jnp.maximum(ns - 2 - i, 0), 0, 0, 0))],
        out_specs=(tok, tok, tok, pl.BlockSpec((1, HG_WIDTH), lambda i: (0, 0))),
        scratch_shapes=[pltpu.VMEM((HG_HEADS, HG_E, HG_E), F32)],
        args=(do, q, f, v, lb, states, states),
    )


GELU_C = math.sqrt(2.0 / math.pi)


def _gelu(x):
    th = jnp.tanh(GELU_C * (x + 0.044715 * x * x * x))
    return 0.5 * x * (1.0 + th), th


def _merge_core(ys5, o, og, ga, gb, wv_ref, wt_ref, ghg, who_ref):
    ys, th = _gelu(ys5)
    ysb = ys.astype(BF16)
    va = jnp.concatenate([_dot(ysb, wv_ref[s]) for s in range(N_SHARD)], axis=1)
    vt = jnp.concatenate([_dot(ysb, wt_ref[s]) for s in range(N_SHARD)], axis=1)
    svt = _sigmoid(vt)
    ya = va * svt
    rs, ons = [], []
    for h in range(HG_HEADS):
        oh = o[:, h * HG_E:(h + 1) * HG_E]
        r = lax.rsqrt(jnp.mean(oh * oh, axis=-1, keepdims=True) + NORM_EPS)
        rs.append(r)
        ons.append(oh * r)
    on = jnp.concatenate(ons, axis=1)
    sgo = _sigmoid(og)
    o2 = on * ghg * (og * sgo)
    o2b = o2.astype(BF16)
    yb = _dot(o2b, who_ref[...])
    sa = _sigmoid(ga)
    sb = _sigmoid(gb)
    mixed = sa * ya + sb * yb
    return dict(ys=ys, th=th, ysb=ysb, va=va, svt=svt, ya=ya, rs=rs, on=on, sgo=sgo, o2b=o2b, yb=yb,
                sa=sa, sb=sb, mixed=mixed)


def merge_fwd(h, ys5, o, og, ga, gb, wv, wt, ghg, who, wmo, tm=256):
    t = h.shape[0]

    def body(h_ref, ys5_ref, o_ref, og_ref, ga_ref, gb_ref, wv_ref, wt_ref, ghg_ref, who_ref, wmo_ref, out_ref):
        c = _merge_core(ys5_ref[...], o_ref[...], og_ref[...], ga_ref[...], gb_ref[...],
                        wv_ref, wt_ref, ghg_ref[...], who_ref)
        out_ref[...] = h_ref[...] + _dot(c["mixed"].astype(BF16), wmo_ref[...])

    tok = pl.BlockSpec((tm, D_MODEL), lambda i: (i, 0))
    return pl.pallas_call(
        body, name="merge_fwd", grid=(t // tm,),
        out_shape=jax.ShapeDtypeStruct((t, D_MODEL), F32),
        in_specs=[tok, pl.BlockSpec((tm, S5_WIDTH), lambda i: (i, 0)), tok, tok, tok, tok,
                  _const_spec((N_SHARD, S5_WIDTH, 256)), _const_spec((N_SHARD, S5_WIDTH, 256)),
                  _const_spec((1, HG_WIDTH)), _const_spec((HG_WIDTH, D_MODEL)), _const_spec((D_MODEL, D_MODEL))],
        out_specs=tok,
        compiler_params=_cparams(("arbitrary",)),
    )(h, ys5, o, og, ga, gb, wv, wt, ghg, who, wmo)


def merge_bwd(dh, ys5, o, og, ga, gb, wv, wt, ghg, who, wmo, comm=None, tm=256):
    t = dh.shape[0]

    def body(dh_ref, ys5_ref, o_ref, og_ref, ga_ref, gb_ref, wv_ref, wt_ref, ghg_ref, who_ref, wmo_ref,
             dys5_ref, do_ref, dog_ref, dga_ref, dgb_ref, dghg_ref,
             mixb_ref, dhb_ref, ysb_ref, dvab_ref, dvtb_ref, o2b_ref, dybb_ref):
        ys5 = ys5_ref[...]
        o = o_ref[...]
        og = og_ref[...]
        ghg = ghg_ref[...]
        c = _merge_core(ys5, o, og, ga_ref[...], gb_ref[...], wv_ref, wt_ref, ghg, who_ref)
        dhb = dh_ref[...].astype(BF16)
        dhb_ref[...] = dhb
        mixb_ref[...] = c["mixed"].astype(BF16)
        ysb_ref[...] = c["ysb"]
        o2b_ref[...] = c["o2b"]
        dmix = _dot_nt(dhb, wmo_ref[...])
        sa, sb = c["sa"], c["sb"]
        dya = dmix * sa
        dyb = dmix * sb
        dga_ref[...] = dmix * c["ya"] * sa * (1.0 - sa)
        dgb_ref[...] = dmix * c["yb"] * sb * (1.0 - sb)
        svt = c["svt"]
        dva = (dya * svt).astype(BF16)
        dvt = (dya * c["va"] * svt * (1.0 - svt)).astype(BF16)
        dvab_ref[...] = dva
        dvtb_ref[...] = dvt
        dys = jnp.zeros((tm, S5_WIDTH), F32)
        for s in range(N_SHARD):
            dys = dys + _dot_nt(dva[:, s * 256:(s + 1) * 256], wv_ref[s]) + _dot_nt(dvt[:, s * 256:(s + 1) * 256], wt_ref[s])
        th = c["th"]
        dgelu = 0.5 * (1.0 + th) + 0.5 * ys5 * (1.0 - th * th) * GELU_C * (1.0 + 3.0 * 0.044715 * ys5 * ys5)
        dys5_ref[...] = dys * dgelu
        dybb = dyb.astype(BF16)
        dybb_ref[...] = dybb
        do2 = _dot_nt(dybb, who_ref[...])
        sgo = c["sgo"]
        sil = og * sgo
        on = c["on"]
        dog_ref[...] = do2 * on * ghg * (sgo * (1.0 + og * (1.0 - sgo)))
        _accum(dghg_ref, jnp.sum(do2 * on * sil, axis=0, keepdims=True), pl.program_id(0) == 0)
        don = do2 * ghg * sil
        dos = []
        for h in range(HG_HEADS):
            sl = slice(h * HG_E, (h + 1) * HG_E)
            m = jnp.mean(don[:, sl] * on[:, sl], axis=-1, keepdims=True)
            dos.append(c["rs"][h] * (don[:, sl] - on[:, sl] * m))
        do_ref[...] = jnp.concatenate(dos, axis=1)

    tok = pl.BlockSpec((tm, D_MODEL), lambda i: (i, 0))
    s5b = pl.BlockSpec((tm, S5_WIDTH), lambda i: (i, 0))
    f32t = jax.ShapeDtypeStruct((t, D_MODEL), F32)
    bft = jax.ShapeDtypeStruct((t, D_MODEL), BF16)
    return _carry(
        body, comm, name="merge_bwd", steps=t // tm,
        out_shape=(jax.ShapeDtypeStruct((t, S5_WIDTH), F32), f32t, f32t, f32t, f32t,
                   jax.ShapeDtypeStruct((1, HG_WIDTH), F32),
                   bft, bft, jax.ShapeDtypeStruct((t, S5_WIDTH), BF16), bft, bft, bft, bft),
        in_specs=[tok, s5b, tok, tok, tok, tok,
                  _const_spec((N_SHARD, S5_WIDTH, 256)), _const_spec((N_SHARD, S5_WIDTH, 256)),
                  _const_spec((1, HG_WIDTH)), _const_spec((HG_WIDTH, D_MODEL)), _const_spec((D_MODEL, D_MODEL))],
        out_specs=(s5b, tok, tok, tok, tok, pl.BlockSpec((1, HG_WIDTH), lambda i: (0, 0)),
                   tok, tok, s5b, tok, tok, tok, tok),
        args=(dh, ys5, o, og, ga, gb, wv, wt, ghg, who, wmo),
    )


def head_fwd_bwd(h, p, tgt, gple, wpg, wpp, gfin, tm=256):
    t = h.shape[0]

    def body(h_ref, p_ref, tgt_ref, gple_ref, wpg_ref, wpp_ref, gfin_ref,
             loss_ref, dh_ref, dgple_ref, dgfin_ref, nb_ref, dlb_ref, dppb_ref):
        first = pl.program_id(0) == 0
        hv = h_ref[...]
        gple = gple_ref[...]
        gfin = gfin_ref[...]
        n, r3 = _rms_fwd(hv, gple)
        nb = n.astype(BF16)
        nb_ref[...] = nb
        pg = _sigmoid(_dot(nb, wpg_ref[...]))
        pb = p_ref[...].astype(BF16)
        pp = jnp.concatenate([_dot(pb, wpp_ref[s]) for s in range(N_SHARD)], axis=1)
        h4 = hv + pg * pp
        y, r4 = _rms_fwd(h4, gfin)
        err = y - tgt_ref[...]
        lsum = 0.5 * jnp.sum(jnp.sum(err * err, axis=-1, keepdims=True), axis=0, keepdims=True) / D_MODEL
        _accum(loss_ref, jnp.broadcast_to(lsum, (8, 128)), first)
        dy = err * (1.0 / D_MODEL)
        dh4, dgf = _rms_bwd(h4, r4, gfin, dy)
        _accum(dgfin_ref, dgf, first)
        dpp = dh4 * pg
        dppb_ref[...] = dpp.astype(BF16)
        dl = (dh4 * pp * pg * (1.0 - pg)).astype(BF16)
        dlb_ref[...] = dl
        dn = _dot_nt(dl, wpg_ref[...])
        dx, dgp = _rms_bwd(hv, r3, gple, dn)
        _accum(dgple_ref, dgp, first)
        dh_ref[...] = dh4 + dx

    tok = pl.BlockSpec((tm, D_MODEL), lambda i: (i, 0))
    vec = pl.BlockSpec((1, D_MODEL), lambda i: (0, 0))
    bft = jax.ShapeDtypeStruct((t, D_MODEL), BF16)
    return pl.pallas_call(
        body, name="head_fwd_bwd", grid=(t // tm,),
        out_shape=(jax.ShapeDtypeStruct((8, 128), F32), jax.ShapeDtypeStruct((t, D_MODEL), F32),
                   jax.ShapeDtypeStruct((1, D_MODEL), F32), jax.ShapeDtypeStruct((1, D_MODEL), F32),
                   bft, bft, bft),
        in_specs=[tok, pl.BlockSpec((tm, PLE_DIM), lambda i: (i, 0)), tok,
                  _const_spec((1, D_MODEL)), _const_spec((D_MODEL, D_MODEL)),
                  _const_spec((N_SHARD, PLE_DIM, 256)), _const_spec((1, D_MODEL))],
        out_specs=(pl.BlockSpec((8, 128), lambda i: (0, 0)), tok, vec, vec, tok, tok, tok),
        compiler_params=_cparams(("arbitrary",)),
    )(h, p, tgt, gple, wpg, wpp, gfin)


BIG = ("ffn1_w_gate", "ffn1_w_up", "ffn1_w_down", "w_in", "s5_glu_val", "s5_glu_gate", "hg_w_out",
       "w_merge_out", "ffn2_w_gate", "ffn2_w_up", "ffn2_w_down", "ple_w_gate", "ple_w_proj")
FFN_T = ("ffn1_w_gate", "ffn1_w_up", "ffn2_w_gate", "ffn2_w_up")
BIG_SHARD = {
    "ffn1_w_gate": (FF_PAD, D_MODEL), "ffn1_w_up": (FF_PAD, D_MODEL), "ffn1_w_down": (FF_PAD, D_MODEL),
    "ffn2_w_gate": (FF_PAD, D_MODEL), "ffn2_w_up": (FF_PAD, D_MODEL), "ffn2_w_down": (FF_PAD, D_MODEL),
    "w_in": (D_MODEL, IN_COLS // N_SHARD), "s5_glu_val": (S5_WIDTH, 256), "s5_glu_gate": (S5_WIDTH, 256),
    "hg_w_out": (256, D_MODEL), "w_merge_out": (256, D_MODEL), "ple_w_gate": (256, D_MODEL),
    "ple_w_proj": (PLE_DIM, 256),
}


def _lower_bound(hb):
    return jax.nn.softmax(hb, axis=0)[0:1]


class Schedule:
    def __init__(self, wts):
        self.wts = dict(wts)
        self.grads = {}

    def before(self, kernel_name):
        return None

    def after(self, kernel_name, results):
        pass

    def grad(self, name, g):
        self.grads[name] = g


def local_step(x, p, tgt, sched, sm):
    wts = sched.wts
    rows_full = lambda w: w.reshape(N_SHARD * w.shape[1], w.shape[2])

    def carried(kernel_name, fn, *args):
        outs, results = fn(*args, comm=sched.before(kernel_name))
        sched.after(kernel_name, results)
        return outs

    def weight_grad(name, xs, ys, shard):
        kernel_name = "g_" + name
        (g,), results = tn_matmul(xs, ys, kernel_name, shard, comm=sched.before(kernel_name))
        sched.grad(name, g)
        sched.after(kernel_name, results)

    lb, lb_vjp = jax.vjp(_lower_bound, sm["hg_lower_bound"])
    s5_names = ("s5_lam_re", "s5_lam_im", "s5_log_dt", "s5_b_re", "s5_b_im", "s5_c_re", "s5_c_im")
    (lam_bar, bmat, cmat), s5_vjp = jax.vjp(s5_prep, *[sm[k] for k in s5_names])
    pw_r, pw_i = _lam_powers(lam_bar)
    bmat_b = bmat.astype(BF16)
    cmat_b = cmat.astype(BF16)

    h1, a1, b1 = carried("ffn1_fwd", ffn_fwd, x, sm["ffn1_norm"], wts["ffn1_w_gate"], wts["ffn1_w_up"],
                         wts["ffn1_w_down"], "ffn1_fwd")
    s5in, q, f, v, og, ga, gb = carried("inproj_fwd", inproj_fwd, h1, sm["mix_norm"], wts["w_in"])
    ys5, xp = carried("s5_fwd", s5_fwd, s5in, _scan_tables(pw_r, pw_i, False), bmat_b, cmat_b, sm["s5_d"])
    o, states = carried("hgrn_fwd", hgrn_fwd, q, f, v, lb)
    who = rows_full(wts["hg_w_out"])
    wmo = rows_full(wts["w_merge_out"])
    h2 = merge_fwd(h1, ys5, o, og, ga, gb, wts["s5_glu_val"], wts["s5_glu_gate"], sm["hg_out_norm"], who, wmo)
    (h3, a2, b2), _ = ffn_fwd(h2, sm["ffn2_norm"], wts["ffn2_w_gate"], wts["ffn2_w_up"], wts["ffn2_w_down"], "ffn2_fwd")
    loss, dh3, d_ple_norm, d_final_norm, npb, dlgb, dppb = head_fwd_bwd(
        h3, p, tgt, sm["ple_norm"], rows_full(wts["ple_w_gate"]), wts["ple_w_proj"], sm["final_norm"])

    gs = {"ple_norm": d_ple_norm, "final_norm": d_final_norm}
    weight_grad("ple_w_gate", npb, dlgb, "rows")
    weight_grad("ple_w_proj", p, dppb, "cols")

    (dh2, gs["ffn2_norm"], n2b, dhb2, da2, db2, s2), _ = ffn_bwd(
        dh3, h2, a2, b2, sm["ffn2_norm"], wts["ffn2_w_gate"], wts["ffn2_w_up"], wts["ffn2_w_down"], "ffn2_bwd")
    weight_grad("ffn2_w_gate", da2, n2b, "rows")
    weight_grad("ffn2_w_up", db2, n2b, "rows")
    weight_grad("ffn2_w_down", s2, dhb2, "rows")

    dys5, do, dog, dga, dgb, gs["hg_out_norm"], mixb, dh2b, ysb, dvab, dvtb, o2b, dybb = carried(
        "merge_bwd", merge_bwd,
        dh2, ys5, o, og, ga, gb, wts["s5_glu_val"], wts["s5_glu_gate"], sm["hg_out_norm"], who, wmo)
    weight_grad("w_merge_out", mixb, dh2b, "rows")
    weight_grad("s5_glu_val", ysb, dvab, "cols")
    weight_grad("s5_glu_gate", ysb, dvtb, "cols")
    weight_grad("hg_w_out", o2b, dybb, "rows")

    dq, df, dv, dlb = carried("hgrn_bwd", hgrn_bwd, do, q, f, v, lb, states)
    (gs["hg_lower_bound"],) = lb_vjp(dlb)
    du, dbmat, dcmat, dlam8, gs["s5_d"] = carried(
        "s5_bwd", s5_bwd,
        dys5, s5in, xp, _scan_tables(pw_r, pw_i, True), bmat_b, cmat_b, sm["s5_d"])
    for k, g in zip(s5_names, s5_vjp((jnp.sum(dlam8, axis=1), dbmat, dcmat))):
        gs[k] = g

    dh1, gs["mix_norm"], nmb, dprojb = carried(
        "inproj_bwd", inproj_bwd, dh2, h1, sm["mix_norm"], wts["w_in"], (du, dq, df, dv, dog, dga, dgb))
    weight_grad("w_in", nmb, dprojb, "cols")

    dx, gs["ffn1_norm"], n1b, dhb1, da1, db1, s1 = carried(
        "ffn1_bwd", ffn_bwd,
        dh1, x, a1, b1, sm["ffn1_norm"], wts["ffn1_w_gate"], wts["ffn1_w_up"], wts["ffn1_w_down"], "ffn1_bwd")
    weight_grad("ffn1_w_gate", da1, n1b, "rows")
    weight_grad("ffn1_w_up", db1, n1b, "rows")
    weight_grad("ffn1_w_down", s1, dhb1, "rows")
    return loss, dx, gs


MESH = pl.DeviceIdType.MESH
ANY = pl.BlockSpec(memory_space=pl.ANY)


def _place():
    x, y, c = lax.axis_index("x"), lax.axis_index("y"), lax.axis_index("c")
    return x, y, c


def _remote(src, dst, ssem, rsem, dev):
    return pltpu.make_async_remote_copy(src_ref=src, dst_ref=dst, send_sem=ssem, recv_sem=rsem,
                                        device_id=dev, device_id_type=MESH)


class Comm:
    def __init__(self, bufs, outs, alias, sems, hooks):
        self.bufs, self.outs, self.alias, self.sems, self.hooks = list(bufs), list(outs), alias, list(sems), hooks


def run_comm(comm, name):
    nb, no = len(comm.bufs), len(comm.outs)

    def body(*refs):
        for which in ("first", "mid", "last"):
            if which in comm.hooks:
                comm.hooks[which](refs[:nb], refs[nb:nb + no], refs[nb + no:])

    return pl.pallas_call(
        body, name=name, out_shape=tuple(comm.outs), in_specs=[ANY] * nb, out_specs=tuple([ANY] * no),
        input_output_aliases=dict(comm.alias), scratch_shapes=comm.sems,
    )(*comm.bufs)


PLACE_ROWS = {1024: 256, 704: 352, 512: 256, 256: 256}


def place_shards(shards, padded_rows, comm, name):
    n, nb, no = len(shards), len(comm.bufs), len(comm.outs)
    stage_rows = max(PLACE_ROWS.values())
    stage_cols = max(s.shape[1] for s in shards)

    def body(*refs):
        ins, cb = refs[:n], refs[n:n + nb]
        outs, co = refs[n + nb:2 * n + nb], refs[2 * n + nb:2 * n + nb + no]
        stage_f32, stage_bf16, zeros, sem = refs[2 * n + nb + no:2 * n + nb + no + 4]
        cs = refs[2 * n + nb + no + 4:]
        chip = 2 * lax.axis_index("x") + lax.axis_index("y")
        zeros[...] = jnp.zeros_like(zeros)
        comm.hooks["first"](cb, co, cs)
        for w in range(n):
            if w == n // 2:
                comm.hooks["mid"](cb, co, cs)
            r0, cols = ins[w].shape
            step = PLACE_ROWS[r0]
            src32 = stage_f32.at[pl.ds(0, step), pl.ds(0, cols)]
            dst16 = stage_bf16.at[pl.ds(0, step), pl.ds(0, cols)]
            for row in range(0, r0, step):
                pltpu.sync_copy(ins[w].at[pl.ds(row, step), :], src32)
                dst16[...] = src32[...].astype(BF16)
                pltpu.sync_copy(dst16, outs[w].at[chip, pl.ds(row, step), :])
            pad = outs[w].shape[1] - r0
            if pad:
                cp = pltpu.make_async_copy(zeros.at[pl.ds(0, pad), pl.ds(0, cols)],
                                           outs[w].at[chip, pl.ds(r0, pad), :], sem)
                cp.start()
                cp.wait()
        comm.hooks["last"](cb, co, cs)

    res = pl.pallas_call(
        body, name=name,
        out_shape=tuple(jax.ShapeDtypeStruct((N_SHARD, r, s.shape[1]), BF16) for s, r in zip(shards, padded_rows))
        + tuple(comm.outs),
        in_specs=[ANY] * (n + nb), out_specs=tuple([ANY] * (n + no)),
        input_output_aliases={n + i: n + o for i, o in comm.alias.items()},
        scratch_shapes=[pltpu.VMEM((stage_rows, stage_cols), F32), pltpu.VMEM((stage_rows, stage_cols), BF16),
                        pltpu.VMEM((FF_PAD - FF_SHARD, D_MODEL), BF16), pltpu.SemaphoreType.DMA] + comm.sems,
        compiler_params=pltpu.CompilerParams(vmem_limit_bytes=VMEM_LIMIT),
    )(*shards, *comm.bufs)
    return res[:n], res[n:]


def _carry(body, comm, *, name, steps, out_shape, in_specs, out_specs, args, scratch_shapes=()):
    out_shape, out_specs, scratch_shapes = tuple(out_shape), tuple(out_specs), list(scratch_shapes)
    if comm is None:
        res = pl.pallas_call(body, name=name, grid=(steps,), out_shape=out_shape, in_specs=list(in_specs),
                             out_specs=out_specs, scratch_shapes=scratch_shapes,
                             compiler_params=_cparams(("arbitrary",)))(*args)
        return tuple(res), ()
    n_in, n_out, n_scr = len(args), len(out_shape), len(scratch_shapes)
    nb, no = len(comm.bufs), len(comm.outs)

    def wrapped(*refs):
        ins, cb = refs[:n_in], refs[n_in:n_in + nb]
        o0 = n_in + nb
        outs, co = refs[o0:o0 + n_out], refs[o0 + n_out:o0 + n_out + no]
        s0 = o0 + n_out + no
        scr, cs = refs[s0:s0 + n_scr], refs[s0 + n_scr:]
        step = pl.program_id(0)

        def hook(which, at):
            if which in comm.hooks:
                pl.when(step == at)(lambda: comm.hooks[which](cb, co, cs))

        hook("first", 0)
        hook("mid", steps // 2)
        body(*ins, *outs, *scr)
        hook("last", steps - 1)

    res = pl.pallas_call(
        wrapped, name=name, grid=(steps,), out_shape=out_shape + tuple(comm.outs),
        in_specs=list(in_specs) + [ANY] * nb, out_specs=out_specs + (ANY,) * no,
        scratch_shapes=scratch_shapes + comm.sems,
        input_output_aliases={n_in + i: n_out + o for i, o in comm.alias.items()},
        compiler_params=_cparams(("arbitrary",)),
    )(*args, *comm.bufs)
    return tuple(res[:n_out]), tuple(res[n_out:])


def gather_comm(bufs):
    n = len(bufs)

    def copies(outs, sems):
        s_own, r_own, s_fwd, r_fwd, s_sib, r_sib = sems
        x, y, c = _place()
        me = 2 * x + y
        nbr = ((1 - x, y), (x, 1 - y))
        nbr_id = (2 * (1 - x) + y, 2 * x + (1 - y))
        diag_id = 2 * (1 - x) + (1 - y)
        sib = (x, y, 1 - c)

        def rows(w, q=None):
            r = outs[w].shape[1]
            if q is None:
                return pl.ds(pl.multiple_of(c * (r // 2), 16), r // 2)
            return pl.ds(pl.multiple_of(c * (r // 2) + q * (r // 4), 16), r // 4)

        def own(w, j):
            piece = outs[w].at[me, rows(w)]
            return _remote(piece, piece, s_own.at[w, j], r_own.at[w, j], (nbr[j][0], nbr[j][1], c))

        def from_nbr(w, j):
            piece = outs[w].at[nbr_id[j], rows(w)]
            return _remote(piece, piece, s_own.at[w, j], r_own.at[w, j], (nbr[j][0], nbr[j][1], c))

        def fwd(w, j):
            piece = outs[w].at[nbr_id[j], rows(w, j)]
            return _remote(piece, piece, s_fwd.at[w, j], r_fwd.at[w, j], (nbr[1 - j][0], nbr[1 - j][1], c))

        def from_diag(w, j):
            piece = outs[w].at[diag_id, rows(w, j)]
            return _remote(piece, piece, s_fwd.at[w, j], r_fwd.at[w, j], (nbr[1 - j][0], nbr[1 - j][1], c))

        def to_sib(w, k):
            piece = (outs[w].at[nbr_id[k], rows(w)] if k < 2 else outs[w].at[diag_id, rows(w, k - 2)])
            return _remote(piece, piece, s_sib.at[w, k], r_sib.at[w, k], sib)

        def from_sib(w, k):
            r = outs[w].shape[1]
            if k < 2:
                piece = outs[w].at[nbr_id[k], pl.ds(pl.multiple_of((1 - c) * (r // 2), 16), r // 2)]
            else:
                piece = outs[w].at[diag_id, pl.ds(pl.multiple_of((1 - c) * (r // 2) + (k - 2) * (r // 4), 16), r // 4)]
            return _remote(piece, piece, s_sib.at[w, k], r_sib.at[w, k], sib)

        return own, from_nbr, fwd, from_diag, to_sib, from_sib

    def first(_, outs, sems):
        own = copies(outs, sems)[0]
        for w in range(n):
            own(w, 0).start()
            own(w, 1).start()

    def mid(_, outs, sems):
        _, from_nbr, fwd, _, to_sib, _ = copies(outs, sems)
        for w in range(n):
            for j in range(2):
                from_nbr(w, j).wait_recv()
                fwd(w, j).start()
                to_sib(w, j).start()

    def last(_, outs, sems):
        own, _, fwd, from_diag, to_sib, from_sib = copies(outs, sems)
        for w in range(n):
            for j in range(2):
                from_diag(w, j).wait_recv()
                to_sib(w, 2 + j).start()
        for w in range(n):
            for k in range(4):
                from_sib(w, k).wait_recv()
        for w in range(n):
            for j in range(2):
                own(w, j).wait_send()
                fwd(w, j).wait_send()
            for k in range(4):
                to_sib(w, k).wait_send()

    dma = pltpu.SemaphoreType.DMA
    return Comm(bufs, [jax.ShapeDtypeStruct(b.shape, b.dtype) for b in bufs], {w: w for w in range(n)},
                [dma((n, 2)), dma((n, 2)), dma((n, 2)), dma((n, 2)), dma((n, 4)), dma((n, 4))],
                {"first": first, "mid": mid, "last": last})


def _start_wait(make):
    def first(bufs, outs, sems):
        for cp in make(bufs, outs, sems):
            cp.start()

    def last(bufs, outs, sems):
        for cp in make(bufs, outs, sems):
            cp.wait()

    return {"first": first, "last": last}


def exchange_comm(grads):
    n = len(grads)

    def make(ins, outs, sems):
        x, y, c = _place()
        cps = []
        for w in range(n):
            half = ins[w].shape[1] // 2
            src = ins[w].at[:, pl.ds(pl.multiple_of((1 - c) * half, 8), half), :]
            cps.append(_remote(src, outs[w], sems[0].at[w], sems[1].at[w], (x, y, 1 - c)))
        return cps

    dma = pltpu.SemaphoreType.DMA
    return Comm(grads, [jax.ShapeDtypeStruct((N_SHARD, g.shape[1] // 2, g.shape[2]), g.dtype) for g in grads],
                {}, [dma((n,)), dma((n,))], _start_wait(make))


def scatter_comm(sums):
    n = len(sums)

    def make(ins, outs, sems):
        x, y, c = _place()
        chips = ((1 - x, y), (x, 1 - y), (1 - x, 1 - y))
        return [_remote(ins[w].at[2 * ch[0] + ch[1]], outs[w].at[j], sems[0].at[w, j], sems[1].at[w, j],
                        (ch[0], ch[1], c))
                for w in range(n) for j, ch in enumerate(chips)]

    dma = pltpu.SemaphoreType.DMA
    return Comm(sums, [jax.ShapeDtypeStruct((3,) + s.shape[1:], s.dtype) for s in sums],
                {}, [dma((n, 3)), dma((n, 3))], _start_wait(make))


def join_comm(shards):
    n = len(shards)

    def make(_, outs, sems):
        x, y, c = _place()
        cps = []
        for w in range(n):
            half = outs[w].shape[0] // 2
            mine = outs[w].at[pl.ds(pl.multiple_of(c * half, 8), half), :]
            cps.append(_remote(mine, mine, sems[0].at[w], sems[1].at[w], (x, y, 1 - c)))
        return cps

    dma = pltpu.SemaphoreType.DMA
    return Comm(shards, [jax.ShapeDtypeStruct(s.shape, s.dtype) for s in shards], {w: w for w in range(n)},
                [dma((n,)), dma((n,))], _start_wait(make))


def allreduce_small(vec):
    half = vec.shape[0] // 2

    def body(v_ref, o_ref, pair, chips_buf, s1, r1, s2, r2, s3, r3):
        x, y, c = _place()
        chip = 2 * x + y
        sib = (x, y, 1 - c)
        mine = pl.ds(pl.multiple_of(c * half, 8), half)
        other = pl.ds(pl.multiple_of((1 - c) * half, 8), half)
        to_sib = _remote(v_ref.at[other], pair, s1, r1, sib)
        to_sib.start()
        to_sib.wait()
        chips_buf[chip] = v_ref[mine, :] + pair[...]
        sends = [_remote(chips_buf.at[chip], chips_buf.at[chip], s2.at[j], r2.at[j], (ch[0], ch[1], c))
                 for j, ch in enumerate(((1 - x, y), (x, 1 - y), (1 - x, 1 - y)))]
        for cp in sends:
            cp.start()
        for cp in sends:
            cp.wait()
        o_ref[mine, :] = (chips_buf[0] + chips_buf[1]) + (chips_buf[2] + chips_buf[3])
        back = _remote(o_ref.at[mine], o_ref.at[mine], s3, r3, sib)
        back.start()
        back.wait()

    dma = pltpu.SemaphoreType.DMA
    return pl.pallas_call(
        body, name="allreduce_small",
        out_shape=jax.ShapeDtypeStruct(vec.shape, F32),
        in_specs=[pl.BlockSpec(memory_space=pltpu.VMEM)],
        out_specs=pl.BlockSpec(memory_space=pltpu.VMEM),
        scratch_shapes=[pltpu.VMEM((half, 128), F32), pltpu.VMEM((N_SHARD, half, 128), F32),
                        dma, dma, dma((3,)), dma((3,)), dma, dma],
        compiler_params=pltpu.CompilerParams(vmem_limit_bytes=VMEM_LIMIT),
    )(vec)


REDUCE_ROW_BLOCKS = 2


def add_own_half(place, g, recv, name):
    _, r, cc = g.shape
    half = r // 2
    nb = REDUCE_ROW_BLOCKS
    tile = half // nb

    def body(p_ref, g_ref, r_ref, o_ref, ob_ref):
        s = g_ref[...] + r_ref[...]
        ob_ref[...] = s.astype(BF16)

        @pl.when(pl.program_id(1) == p_ref[0])
        def _():
            o_ref[...] = s

    blk = (None, tile, cc)
    return pl.pallas_call(
        body, name=name,
        grid_spec=pltpu.PrefetchScalarGridSpec(
            num_scalar_prefetch=1, grid=(nb, N_SHARD),
            in_specs=[pl.BlockSpec(blk, lambda i, s, p_ref: (s, p_ref[1] * nb + i, 0)),
                      pl.BlockSpec(blk, lambda i, s, p_ref: (s, i, 0))],
            out_specs=(pl.BlockSpec((tile, cc), lambda i, s, p_ref: (i, 0)),
                       pl.BlockSpec(blk, lambda i, s, p_ref: (s, i, 0)))),
        out_shape=(jax.ShapeDtypeStruct((half, cc), F32),
                   jax.ShapeDtypeStruct((N_SHARD, half, cc), BF16)),
        compiler_params=_cparams(("arbitrary", "arbitrary")),
    )(place, g, recv)


def add_chip_sums(place, own, recv, name):
    half, cc = own.shape
    nb = REDUCE_ROW_BLOCKS
    tile = half // nb

    def body(s_ref, o_ref, r_ref, out_ref):
        del s_ref
        acc = o_ref[...] + r_ref[0].astype(F32)
        acc = acc + r_ref[1].astype(F32)
        out_ref[...] = acc + r_ref[2].astype(F32)

    return pl.pallas_call(
        body, name=name,
        grid_spec=pltpu.PrefetchScalarGridSpec(
            num_scalar_prefetch=1, grid=(nb,),
            in_specs=[pl.BlockSpec((tile, cc), lambda i, s_ref: (i, 0)),
                      pl.BlockSpec((3, tile, cc), lambda i, s_ref: (0, i, 0))],
            out_specs=pl.BlockSpec((tile, cc), lambda i, s_ref: (s_ref[1] * nb + i, 0))),
        out_shape=jax.ShapeDtypeStruct((2 * half, cc), F32),
        compiler_params=_cparams(("arbitrary",)),
    )(place, own, recv)


def adamw(w, m, v, g, name, copy_g=False):
    r, cc = w.shape
    tr = next(t for t in (256, 352, r) if r % t == 0)
    bc1 = 1.0 / (1.0 - ADAM_B1 ** ADAM_STEP)
    bc2 = 1.0 / (1.0 - ADAM_B2 ** ADAM_STEP)

    def body(w_ref, m_ref, v_ref, g_ref, d_ref, mo_ref, vo_ref, *go_ref):
        gv = g_ref[...]
        mn = ADAM_B1 * m_ref[...] + (1.0 - ADAM_B1) * gv
        vn = ADAM_B2 * v_ref[...] + (1.0 - ADAM_B2) * (gv * gv)
        mo_ref[...] = mn
        vo_ref[...] = vn
        d_ref[...] = -ADAM_LR * ((mn * bc1) / (jnp.sqrt(vn * bc2) + ADAM_EPS) + ADAM_WD * w_ref[...])
        if copy_g:
            go_ref[0][...] = gv

    blk = pl.BlockSpec((tr, cc), lambda i: (i, 0))
    shp = jax.ShapeDtypeStruct((r, cc), F32)
    nout = 4 if copy_g else 3
    return pl.pallas_call(
        body, name=name, grid=(r // tr,),
        out_shape=(shp,) * nout, in_specs=[blk] * 4, out_specs=(blk,) * nout,
        compiler_params=_cparams(("arbitrary",)),
    )(w, m, v, g)


GATHER_FIRST = ("ffn1_w_gate", "ffn1_w_up", "ffn1_w_down")
GATHER_ON = {"ffn1_fwd": ("w_in",),
             "inproj_fwd": ("s5_glu_val", "s5_glu_gate", "hg_w_out", "w_merge_out"),
             "s5_fwd": ("ffn2_w_gate", "ffn2_w_up"),
             "hgrn_fwd": ("ffn2_w_down", "ple_w_gate", "ple_w_proj")}
REDUCE = ((("ple_w_gate", "ple_w_proj", "ffn2_w_gate", "ffn2_w_up", "ffn2_w_down"), "merge_bwd", "hgrn_bwd"),
          (("w_merge_out", "s5_glu_val", "s5_glu_gate", "hg_w_out"), "s5_bwd", "inproj_bwd"),
          (("w_in",), None, "ffn1_bwd"),
          (("ffn1_w_gate",), "g_ffn1_w_up", "g_ffn1_w_down"),
          (("ffn1_w_up",), "g_ffn1_w_down", None),
          (("ffn1_w_down",), None, None))


def merge_comms(comms):
    if len(comms) == 1:
        return comms[0], [len(comms[0].outs)]
    bufs, outs, sems, alias, spans = [], [], [], {}, []
    for c in comms:
        spans.append((len(bufs), len(bufs) + len(c.bufs), len(outs), len(outs) + len(c.outs),
                      len(sems), len(sems) + len(c.sems)))
        alias.update({len(bufs) + i: len(outs) + o for i, o in c.alias.items()})
        bufs, outs, sems = bufs + c.bufs, outs + c.outs, sems + c.sems

    def hook(which):
        def run(b, o, s):
            for c, (b0, b1, o0, o1, s0, s1) in zip(comms, spans):
                if which in c.hooks:
                    c.hooks[which](b[b0:b1], o[o0:o1], s[s0:s1])
        return run

    hooks = {w: hook(w) for w in ("first", "mid", "last") if any(w in c.hooks for c in comms)}
    return Comm(bufs, outs, alias, sems, hooks), [len(c.outs) for c in comms]


class DistSchedule(Schedule):
    def __init__(self, w_rows, chip, core):
        first = gather_comm([_gather_buffer(k, w_rows[k], chip) for k in GATHER_FIRST])
        later = [k for k in BIG if k not in GATHER_FIRST]
        placed, gathered = place_shards([w_rows[k] for k in later], [BIG_SHARD[k][0] for k in later], first,
                                        "place_shards_gather_ffn1")
        super().__init__(zip(GATHER_FIRST, gathered))
        self.bufs = dict(zip(later, placed))
        self.place = jnp.stack([chip, core])
        self.sums, self.halves = {}, {}

    def _exchange(self, names):
        return exchange_comm([self.grads[k] for k in names])

    def _scatter(self, names):
        return scatter_comm([self.sums[k][1] for k in names])

    def _pair_sums(self, names, recv):
        for k, r in zip(names, recv):
            self.sums[k] = add_own_half(self.place, self.grads[k], r, "pair_sum_" + k)

    def _chip_sums(self, names, recv):
        for k, r in zip(names, recv):
            self.halves[k] = add_chip_sums(self.place, self.sums[k][0], r, "chip_sum_" + k)

    def before(self, kernel_name):
        comms, takers = [], []
        if kernel_name in GATHER_ON:
            names = GATHER_ON[kernel_name]
            comms.append(gather_comm([self.bufs[k] for k in names]))
            takers.append(lambda res, names=names: self.wts.update(zip(names, res)))
        for names, exchange_on, scatter_on in REDUCE:
            if kernel_name == exchange_on:
                comms.append(self._exchange(names))
                takers.append(lambda res, names=names: self._pair_sums(names, res))
            if kernel_name == scatter_on:
                if exchange_on is None:
                    self._pair_sums(names, run_comm(self._exchange(names), "exchange_" + names[0]))
                comms.append(self._scatter(names))
                takers.append(lambda res, names=names: self._chip_sums(names, res))
        if not comms:
            return None
        merged, counts = merge_comms(comms)
        self.pending = (takers, counts)
        return merged

    def after(self, kernel_name, results):
        if not results:
            return
        takers, counts = self.pending
        start = 0
        for take, count in zip(takers, counts):
            take(results[start:start + count])
            start += count

    def finish(self):
        tail = [names for names, _, scatter_on in REDUCE if scatter_on is None]
        alone = [k for names, exchange_on, scatter_on in REDUCE if scatter_on is None and exchange_on is None
                 for k in names]
        self._pair_sums(alone, run_comm(self._exchange(alone), "exchange_tail"))
        tail = [k for names in tail for k in names]
        early = [k for k in BIG if k not in tail]
        both, counts = merge_comms([self._scatter(tail), join_comm([self.halves[k] for k in early])])
        res = run_comm(both, "scatter_tail_join_early")
        self._chip_sums(tail, res[:counts[0]])
        full = dict(zip(early, res[counts[0]:]))
        full.update(zip(tail, run_comm(join_comm([self.halves[k] for k in tail]), "join_tail")))
        return full


SMALL = ("ffn1_norm", "mix_norm", "s5_lam_re", "s5_lam_im", "s5_log_dt", "s5_b_re", "s5_b_im", "s5_c_re",
         "s5_c_im", "s5_d", "hg_lower_bound", "hg_out_norm", "ffn2_norm", "ple_norm", "final_norm")
WEIGHTS = ("ffn1_norm", "ffn1_w_gate", "ffn1_w_up", "ffn1_w_down", "mix_norm", "w_in", "s5_lam_re", "s5_lam_im",
           "s5_log_dt", "s5_b_re", "s5_b_im", "s5_c_re", "s5_c_im", "s5_d", "s5_glu_val", "s5_glu_gate",
           "hg_lower_bound", "hg_out_norm", "hg_w_out", "w_merge_out", "ffn2_norm", "ffn2_w_gate", "ffn2_w_up",
           "ffn2_w_down", "ple_norm", "ple_w_gate", "ple_w_proj", "final_norm")


def _as_rows(name, w):
    return jnp.swapaxes(w[0], 0, 1) if name in FFN_T else w[0]


def _from_rows(name, w):
    return (jnp.swapaxes(w, 0, 1) if name in FFN_T else w)[None]


def _gather_buffer(name, w_rows, chip):
    r, c = BIG_SHARD[name]
    shard = jnp.pad(w_rows.astype(BF16), ((0, r - w_rows.shape[0]), (0, 0)))
    return lax.dynamic_update_slice(jnp.zeros((N_SHARD, r, c), BF16), shard[None], (chip, 0, 0))


def _pack(parts):
    flat = jnp.concatenate([jnp.zeros((128,), F32)] + [a.reshape(-1) for a in parts])
    rows = -(-flat.shape[0] // 2048) * 16
    return jnp.pad(flat, (0, rows * 128 - flat.shape[0])).reshape(rows, 128)


def _unpack(vec, likes):
    flat = vec.reshape(-1)
    out, off = [], 128
    for a in likes:
        out.append(flat[off:off + a.size].reshape(a.shape))
        off += a.size
    return out


def _small_view(name, w):
    if name.startswith("s5_") and name != "s5_d":
        return w[0]
    if name == "final_norm":
        return w.reshape(1, D_MODEL)
    return w


def kernel(x, p, ffn1_norm, ffn1_w_gate, ffn1_w_up, ffn1_w_down, mix_norm, w_in, s5_lam_re, s5_lam_im, s5_log_dt, s5_b_re, s5_b_im, s5_c_re, s5_c_im, s5_d, s5_glu_val, s5_glu_gate, hg_lower_bound, hg_out_norm, hg_w_out, w_merge_out, ffn2_norm, ffn2_w_gate, ffn2_w_up, ffn2_w_down, ple_norm, ple_w_gate, ple_w_proj, final_norm, loss_target, m_ffn1_norm, m_ffn1_w_gate, m_ffn1_w_up, m_ffn1_w_down, m_mix_norm, m_w_in, m_s5_lam_re, m_s5_lam_im, m_s5_log_dt, m_s5_b_re, m_s5_b_im, m_s5_c_re, m_s5_c_im, m_s5_d, m_s5_glu_val, m_s5_glu_gate, m_hg_lower_bound, m_hg_out_norm, m_hg_w_out, m_w_merge_out, m_ffn2_norm, m_ffn2_w_gate, m_ffn2_w_up, m_ffn2_w_down, m_ple_norm, m_ple_w_gate, m_ple_w_proj, m_final_norm, v_ffn1_norm, v_ffn1_w_gate, v_ffn1_w_up, v_ffn1_w_down, v_mix_norm, v_w_in, v_s5_lam_re, v_s5_lam_im, v_s5_log_dt, v_s5_b_re, v_s5_b_im, v_s5_c_re, v_s5_c_im, v_s5_d, v_s5_glu_val, v_s5_glu_gate, v_hg_lower_bound, v_hg_out_norm, v_hg_w_out, v_w_merge_out, v_ffn2_norm, v_ffn2_w_gate, v_ffn2_w_up, v_ffn2_w_down, v_ple_norm, v_ple_w_gate, v_ple_w_proj, v_final_norm):
    given = dict(locals())
    wv = {k: given[k] for k in WEIGHTS}
    mv = {k: given["m_" + k] for k in WEIGHTS}
    vv = {k: given["v_" + k] for k in WEIGHTS}

    core = lax.axis_index("c").astype(jnp.int32)
    chip = (2 * lax.axis_index("x") + lax.axis_index("y")).astype(jnp.int32)
    w_rows = {k: _as_rows(k, wv[k]) for k in BIG}
    sched = DistSchedule(w_rows, chip, core)
    sm = {k: _small_view(k, wv[k]) for k in SMALL}

    loss_blk, dx, gsm = local_step(x[0], p[0, 0], loss_target[0], sched, sm)
    full = sched.finish()

    small_likes = [wv[k] for k in SMALL]
    packed = _pack([gsm[k] for k in SMALL])
    packed = packed.at[0, 0].set(loss_blk[0, 0])
    total = allreduce_small(packed)
    loss = total[0, 0]
    gsmall = dict(zip(SMALL, _unpack(total, small_likes)))

    grads, deltas, new_m, new_v = {}, {}, {}, {}
    for k in BIG:
        padded = full[k].shape != w_rows[k].shape
        res = adamw(w_rows[k], _as_rows(k, mv[k]), _as_rows(k, vv[k]), full[k], "adamw_" + k, copy_g=padded)
        grads[k] = _from_rows(k, res[3] if padded else full[k])
        deltas[k], new_m[k], new_v[k] = (_from_rows(k, a) for a in res[:3])
    sw = _pack([wv[k] for k in SMALL])
    smm = _pack([mv[k] for k in SMALL])
    svv = _pack([vv[k] for k in SMALL])
    sd, smn, svn = adamw(sw, smm, svv, total, "adamw_small")
    for k, d, mn, vn in zip(SMALL, _unpack(sd, small_likes), _unpack(smn, small_likes), _unpack(svn, small_likes)):
        grads[k], deltas[k], new_m[k], new_v[k] = gsmall[k], d, mn, vn

    return (loss, dx[None], *[grads[k] for k in WEIGHTS], *[deltas[k] for k in WEIGHTS],
            *[new_m[k] for k in WEIGHTS], *[new_v[k] for k in WEIGHTS])
```

```python
import math

import jax
import jax.numpy as jnp
from jax import lax
from jax.experimental import pallas as pl
from jax.experimental.pallas import tpu as pltpu

F32 = jnp.float32
BF16 = jnp.bfloat16

D_MODEL = 1024
D_FF = 2816
N_SHARD = 4
FF_SHARD = D_FF // N_SHARD
FF_PAD = 768
NORM_EPS = 1e-6
PLE_DIM = 256

S5_WIDTH = 512
S5_GROUPS = 32
S5_GROUP = 16
S5_STATE = 64
S5_N = S5_GROUPS * S5_STATE
S5_KT = 2

HG_HEADS = 8
HG_E = 128
HG_WIDTH = 1024
CHUNK = 64
HG_SUB = 4
IN_COLS = S5_WIDTH + 4 * HG_WIDTH + 2 * D_MODEL
IN_SPLITS = (0, 512, 1536, 2560, 3584, 4608, 5632, 6656)

ADAM_LR = 0.001
ADAM_B1 = 0.9
ADAM_B2 = 0.999
ADAM_EPS = 1e-08
ADAM_WD = 0.01
ADAM_STEP = 10

VMEM_LIMIT = 60 * 1024 * 1024
HIGHEST = lax.Precision.HIGHEST


def _cparams(sem=None, **kw):
    return pltpu.CompilerParams(dimension_semantics=sem, vmem_limit_bytes=VMEM_LIMIT, **kw)


def _const_spec(shape):
    nd = len(shape)
    return pl.BlockSpec(shape, lambda *_: (0,) * nd, pipeline_mode=pl.Buffered(1))


def _dot(a, b):
    return jnp.dot(a, b, preferred_element_type=F32)


def _dot_nt(a, b):
    return lax.dot_general(a, b, (((1,), (1,)), ((), ())), preferred_element_type=F32)


def _dot_tn(a, b):
    return lax.dot_general(a, b, (((0,), (0,)), ((), ())), preferred_element_type=F32)


def _sigmoid(x):
    return 1.0 / (1.0 + jnp.exp(-x))


def _rms_fwd(x, g):
    r = lax.rsqrt(jnp.mean(x * x, axis=-1, keepdims=True) + NORM_EPS)
    return x * r * g, r


def _rms_bwd(x, r, g, dy):
    xh = x * r
    dyg = dy * g
    m = jnp.mean(dyg * xh, axis=-1, keepdims=True)
    return r * (dyg - xh * m), jnp.sum(dy * xh, axis=0, keepdims=True)


def _accum(ref, val, first):
    @pl.when(first)
    def _():
        ref[...] = val

    @pl.when(jnp.logical_not(first))
    def _():
        ref[...] += val


def ffn_fwd(h, gain, wg, wu, wd, name, comm=None, tm=512):
    t = h.shape[0]

    def body(h_ref, g_ref, wg_ref, wu_ref, wd_ref, o_ref, a_ref, b_ref):
        hv = h_ref[...]
        n, _ = _rms_fwd(hv, g_ref[...])
        nb = n.astype(BF16)
        acc = jnp.zeros((tm, D_MODEL), F32)
        for s in range(N_SHARD):
            a = _dot_nt(nb, wg_ref[s])
            b = _dot_nt(nb, wu_ref[s])
            a_ref[s] = a.astype(BF16)
            b_ref[s] = b.astype(BF16)
            sv = (a * _sigmoid(a) * b).astype(BF16)
            acc = acc + _dot(sv, wd_ref[s])
        o_ref[...] = hv + 0.5 * acc

    return _carry(
        body, comm, name=name, steps=t // tm,
        out_shape=(jax.ShapeDtypeStruct((t, D_MODEL), F32),
                   jax.ShapeDtypeStruct((N_SHARD, t, FF_PAD), BF16),
                   jax.ShapeDtypeStruct((N_SHARD, t, FF_PAD), BF16)),
        in_specs=[pl.BlockSpec((tm, D_MODEL), lambda i: (i, 0)),
                  _const_spec((1, D_MODEL)),
                  _const_spec((N_SHARD, FF_PAD, D_MODEL)),
                  _const_spec((N_SHARD, FF_PAD, D_MODEL)),
                  _const_spec((N_SHARD, FF_PAD, D_MODEL))],
        out_specs=(pl.BlockSpec((tm, D_MODEL), lambda i: (i, 0)),
                   pl.BlockSpec((N_SHARD, tm, FF_PAD), lambda i: (0, i, 0)),
                   pl.BlockSpec((N_SHARD, tm, FF_PAD), lambda i: (0, i, 0))),
        args=(h, gain, wg, wu, wd),
    )


def ffn_bwd(dho, h, a, b, gain, wg, wu, wd, name, comm=None, tm=256):
    t = h.shape[0]

    def body(dho_ref, h_ref, a_ref, b_ref, g_ref, wg_ref, wu_ref, wd_ref,
             dh_ref, dg_ref, nb_ref, dhb_ref, da_ref, db_ref, s_ref):
        hv = h_ref[...]
        g = g_ref[...]
        n, r = _rms_fwd(hv, g)
        nb_ref[...] = n.astype(BF16)
        dhalf = (0.5 * dho_ref[...]).astype(BF16)
        dhb_ref[...] = dhalf
        dn = jnp.zeros((tm, D_MODEL), F32)
        for s in range(N_SHARD):
            av = a_ref[s].astype(F32)
            bv = b_ref[s].astype(F32)
            sg = _sigmoid(av)
            sil = av * sg
            s_ref[s] = (sil * bv).astype(BF16)
            ds = _dot_nt(dhalf, wd_ref[s])
            da = (ds * bv * (sg * (1.0 + av * (1.0 - sg)))).astype(BF16)
            db = (ds * sil).astype(BF16)
            da_ref[s] = da
            db_ref[s] = db
            dn = dn + _dot(da, wg_ref[s]) + _dot(db, wu_ref[s])
        dx, dg = _rms_bwd(hv, r, g, dn)
        dh_ref[...] = dho_ref[...] + dx
        _accum(dg_ref, dg, pl.program_id(0) == 0)

    tok = pl.BlockSpec((tm, D_MODEL), lambda i: (i, 0))
    hid = pl.BlockSpec((N_SHARD, tm, FF_PAD), lambda i: (0, i, 0))
    return _carry(
        body, comm, name=name, steps=t // tm,
        out_shape=(jax.ShapeDtypeStruct((t, D_MODEL), F32),
                   jax.ShapeDtypeStruct((1, D_MODEL), F32),
                   jax.ShapeDtypeStruct((t, D_MODEL), BF16),
                   jax.ShapeDtypeStruct((t, D_MODEL), BF16),
                   jax.ShapeDtypeStruct((N_SHARD, t, FF_PAD), BF16),
                   jax.ShapeDtypeStruct((N_SHARD, t, FF_PAD), BF16),
                   jax.ShapeDtypeStruct((N_SHARD, t, FF_PAD), BF16)),
        in_specs=[tok, tok, hid, hid, _const_spec((1, D_MODEL)),
                  _const_spec((N_SHARD, FF_PAD, D_MODEL)),
                  _const_spec((N_SHARD, FF_PAD, D_MODEL)),
                  _const_spec((N_SHARD, FF_PAD, D_MODEL))],
        out_specs=(tok, pl.BlockSpec((1, D_MODEL), lambda i: (0, 0)), tok, tok, hid, hid, hid),
        args=(dho, h, a, b, gain, wg, wu, wd),
    )


TN_VMEM_BUDGET = 44 * 1024 * 1024


def tn_matmul(x, y, name, shard, comm=None):
    x3, y3 = x.ndim == 3, y.ndim == 3
    t = x.shape[-2]
    m = x.shape[-1] // (N_SHARD if (shard == "rows" and not x3) else 1)
    n = y.shape[-1] // (N_SHARD if (shard == "cols" and not y3) else 1)
    per_token = 2 * (m * x.dtype.itemsize + n * y.dtype.itemsize)
    tk = t
    while tk > 512 and tk * per_token + 2 * m * n * 4 > TN_VMEM_BUDGET:
        tk //= 2
    nk = t // tk

    out_shape = jax.ShapeDtypeStruct((N_SHARD, m, n), F32)
    if nk == 1:
        def whole(x_ref, y_ref, o_ref):
            o_ref[...] = _dot_tn(x_ref[...].astype(BF16), y_ref[...].astype(BF16))

        x_one = (pl.BlockSpec((None, t, m), lambda s: (s, 0, 0)) if x3 else
                 pl.BlockSpec((t, m), (lambda s: (0, s)) if shard == "rows" else (lambda s: (0, 0))))
        y_one = (pl.BlockSpec((None, t, n), lambda s: (s, 0, 0)) if y3 else
                 pl.BlockSpec((t, n), (lambda s: (0, s)) if shard == "cols" else (lambda s: (0, 0))))
        return _carry(whole, comm, name=name, steps=N_SHARD, out_shape=(out_shape,), in_specs=[x_one, y_one],
                      out_specs=(pl.BlockSpec((None, m, n), lambda s: (s, 0, 0)),), args=(x, y))
    assert comm is None

    def body(x_ref, y_ref, o_ref):
        _accum(o_ref, _dot_tn(x_ref[...].astype(BF16), y_ref[...].astype(BF16)), pl.program_id(1) == 0)

    if x3:
        x_spec = pl.BlockSpec((None, tk, m), lambda s, k: (s, k, 0))
    elif shard == "rows":
        x_spec = pl.BlockSpec((tk, m), lambda s, k: (k, s))
    else:
        x_spec = pl.BlockSpec((tk, m), lambda s, k: (k, 0))
    if y3:
        y_spec = pl.BlockSpec((None, tk, n), lambda s, k: (s, k, 0))
    elif shard == "cols":
        y_spec = pl.BlockSpec((tk, n), lambda s, k: (k, s))
    else:
        y_spec = pl.BlockSpec((tk, n), lambda s, k: (k, 0))
    res = pl.pallas_call(
        body, name=name, grid=(N_SHARD, nk),
        out_shape=out_shape,
        in_specs=[x_spec, y_spec],
        out_specs=pl.BlockSpec((None, m, n), lambda s, k: (s, 0, 0)),
        compiler_params=_cparams(("arbitrary", "arbitrary")),
    )(x, y)
    return (res,), ()


def inproj_fwd(h, gain, w_in, comm=None, tm=256):
    t = h.shape[0]
    widths = [IN_SPLITS[j + 1] - IN_SPLITS[j] for j in range(7)]
    sh_cols = IN_COLS // N_SHARD

    def body(h_ref, g_ref, w_ref, *outs):
        n, _ = _rms_fwd(h_ref[...], g_ref[...])
        nb = n.astype(BF16)
        proj = jnp.concatenate([_dot(nb, w_ref[s]) for s in range(N_SHARD)], axis=1)
        for j, o_ref in enumerate(outs):
            o_ref[...] = proj[:, IN_SPLITS[j]:IN_SPLITS[j + 1]]

    return _carry(
        body, comm, name="inproj_fwd", steps=t // tm,
        out_shape=tuple(jax.ShapeDtypeStruct((t, w), F32) for w in widths),
        in_specs=[pl.BlockSpec((tm, D_MODEL), lambda i: (i, 0)),
                  _const_spec((1, D_MODEL)),
                  _const_spec((N_SHARD, D_MODEL, sh_cols))],
        out_specs=tuple(pl.BlockSpec((tm, w), lambda i: (i, 0)) for w in widths),
        args=(h, gain, w_in),
    )


def inproj_bwd(dres, h, gain, w_in, dparts, comm=None, tm=256):
    t = h.shape[0]
    widths = [IN_SPLITS[j + 1] - IN_SPLITS[j] for j in range(7)]
    sh_cols = IN_COLS // N_SHARD

    def body(dres_ref, h_ref, g_ref, w_ref, d0, d1, d2, d3, d4, d5, d6, dh_ref, dg_ref, nb_ref, dp_ref):
        hv = h_ref[...]
        g = g_ref[...]
        n, r = _rms_fwd(hv, g)
        nb_ref[...] = n.astype(BF16)
        dproj = jnp.concatenate([d[...] for d in (d0, d1, d2, d3, d4, d5, d6)], axis=1).astype(BF16)
        dp_ref[...] = dproj
        dn = jnp.zeros((tm, D_MODEL), F32)
        for s in range(N_SHARD):
            dn = dn + _dot_nt(dproj[:, s * sh_cols:(s + 1) * sh_cols], w_ref[s])
        dx, dg = _rms_bwd(hv, r, g, dn)
        dh_ref[...] = dres_ref[...] + dx
        _accum(dg_ref, dg, pl.program_id(0) == 0)

    tok = pl.BlockSpec((tm, D_MODEL), lambda i: (i, 0))
    return _carry(
        body, comm, name="inproj_bwd", steps=t // tm,
        out_shape=(jax.ShapeDtypeStruct((t, D_MODEL), F32),
                   jax.ShapeDtypeStruct((1, D_MODEL), F32),
                   jax.ShapeDtypeStruct((t, D_MODEL), BF16),
                   jax.ShapeDtypeStruct((t, IN_COLS), BF16)),
        in_specs=[tok, tok, _const_spec((1, D_MODEL)), _const_spec((N_SHARD, D_MODEL, sh_cols))]
                 + [pl.BlockSpec((tm, w), lambda i: (i, 0)) for w in widths],
        out_specs=(tok, pl.BlockSpec((1, D_MODEL), lambda i: (0, 0)), tok,
                   pl.BlockSpec((tm, IN_COLS), lambda i: (i, 0))),
        args=(dres, h, gain, w_in, *dparts),
    )


def s5_prep(lam_re, lam_im, log_dt, b_re, b_im, c_re, c_im):
    dt = jnp.exp(log_dt)[:, None]
    mag = jnp.exp(lam_re * dt)
    lbr = mag * jnp.cos(lam_im * dt)
    lbi = mag * jnp.sin(lam_im * dt)
    den = lam_re * lam_re + lam_im * lam_im
    nr, ni = lbr - 1.0, lbi
    kr = (nr * lam_re + ni * lam_im) / den
    ki = (ni * lam_re - nr * lam_im) / den
    bbr = kr[..., None] * b_re - ki[..., None] * b_im
    bbi = kr[..., None] * b_im + ki[..., None] * b_re
    eye = jnp.eye(16, dtype=F32)

    def bm(bp):
        return jnp.einsum('kgph,gG->kghGp', bp.reshape(S5_KT, 16, S5_STATE, S5_GROUP), eye).reshape(S5_KT, 256, 1024)

    def cm(cp):
        return jnp.einsum('kghp,gG->kgpGh', cp.reshape(S5_KT, 16, S5_GROUP, S5_STATE), eye).reshape(S5_KT, 1024, 256)

    lam_bar = jnp.stack([lbr.reshape(S5_N), lbi.reshape(S5_N)])
    bmat = jnp.stack([bm(bbr), bm(bbi)])
    cmat = jnp.stack([cm(c_re), -cm(c_im)])
    return lam_bar, bmat, cmat


def _lam_powers(lam_bar):
    lr, li = lam_bar[0], lam_bar[1]
    pr, pi = [lr], [li]
    for _ in range(7):
        pr, pi = pr + [pr[-1] * lr - pi[-1] * li], pi + [pr[-1] * li + pi[-1] * lr]
    return jnp.stack(pr), jnp.stack(pi)


SCAN_SHIFTS = ((1, 0), (2, 1), (4, 3))


def _scan_tables(pw_r, pw_i, reverse):
    rows = jnp.arange(8)[:, None]
    planes_r, planes_i = [], []
    for sh, idx in SCAN_SHIFTS:
        keep = (rows < 8 - sh) if reverse else (rows >= sh)
        planes_r.append(jnp.where(keep, pw_r[idx:idx + 1], 0.0))
        planes_i.append(jnp.where(keep, pw_i[idx:idx + 1], 0.0))
    carry = [pw_r[::-1], pw_i[::-1]] if reverse else [pw_r, pw_i]
    return jnp.stack(planes_r + planes_i + carry)


def s5_fwd(u, tab, bmat, cmat, dvec, comm=None, tm=256):
    t = u.shape[0]
    nch = tm // 8

    def body(u_ref, tab_ref, b_ref, c_ref, d_ref, y_ref, xp_ref, x_scr, carry):
        @pl.when(pl.program_id(0) == 0)
        def _():
            carry[...] = jnp.zeros_like(carry)

        uv = u_ref[...]
        ub = uv.astype(BF16)
        for part in range(2):
            for kt in range(S5_KT):
                x_scr[:, pl.ds(part * S5_N + kt * 1024, 1024)] = _dot(ub[:, kt * 256:(kt + 1) * 256], b_ref[part, kt])
        row = lax.broadcasted_iota(jnp.int32, (8, S5_N), 0)

        def chunk(i, c):
            cr, ci = c
            r0 = pl.multiple_of(i * 8, 8)
            xr = x_scr[pl.ds(r0, 8), pl.ds(0, S5_N)]
            xi = x_scr[pl.ds(r0, 8), pl.ds(S5_N, S5_N)]
            for lvl, (sh, _) in enumerate(SCAN_SHIFTS):
                sr = pltpu.roll(xr, sh, 0)
                si = pltpu.roll(xi, sh, 0)
                lr = tab_ref[lvl]
                li = tab_ref[3 + lvl]
                xr, xi = xr + lr * sr - li * si, xi + lr * si + li * sr
            pwr = tab_ref[6]
            pwi = tab_ref[7]
            xr, xi = xr + pwr * cr - pwi * ci, xi + pwr * ci + pwi * cr
            x_scr[pl.ds(r0, 8), pl.ds(0, S5_N)] = xr
            x_scr[pl.ds(r0, 8), pl.ds(S5_N, S5_N)] = xi
            xp_ref[pl.ds(r0, 8), pl.ds(0, S5_N)] = jnp.where(row == 0, cr, pltpu.roll(xr, 1, 0))
            xp_ref[pl.ds(r0, 8), pl.ds(S5_N, S5_N)] = jnp.where(row == 0, ci, pltpu.roll(xi, 1, 0))
            return xr[7:8, :], xi[7:8, :]

        cr, ci = lax.fori_loop(0, nch, chunk, (carry[0:1, :], carry[1:2, :]))
        carry[0:1, :] = cr
        carry[1:2, :] = ci
        for kt in range(S5_KT):
            acc = jnp.zeros((tm, 256), F32)
            for part in range(2):
                acc = acc + _dot(x_scr[:, pl.ds(part * S5_N + kt * 1024, 1024)].astype(BF16), c_ref[part, kt])
            y_ref[:, pl.ds(kt * 256, 256)] = acc + d_ref[:, pl.ds(kt * 256, 256)] * uv[:, kt * 256:(kt + 1) * 256]

    return _carry(
        body, comm, name="s5_fwd", steps=t // tm,
        out_shape=(jax.ShapeDtypeStruct((t, S5_WIDTH), F32),
                   jax.ShapeDtypeStruct((t, 2 * S5_N), F32)),
        in_specs=[pl.BlockSpec((tm, S5_WIDTH), lambda i: (i, 0)),
                  _const_spec((8, 8, S5_N)),
                  _const_spec((2, S5_KT, 256, 1024)), _const_spec((2, S5_KT, 1024, 256)),
                  _const_spec((1, S5_WIDTH))],
        out_specs=(pl.BlockSpec((tm, S5_WIDTH), lambda i: (i, 0)),
                   pl.BlockSpec((tm, 2 * S5_N), lambda i: (i, 0))),
        scratch_shapes=[pltpu.VMEM((tm, 2 * S5_N), F32), pltpu.VMEM((8, S5_N), F32)],
        args=(u, tab, bmat, cmat, dvec),
    )


def s5_bwd(dy, u, xp, tab, bmat, cmat, dvec, comm=None, tm=256):
    t = u.shape[0]
    nt = t // tm
    nch = tm // 8

    def body(dy_ref, u_ref, xp_ref, tab_ref, b_ref, c_ref, d_ref,
             du_ref, db_ref, dc_ref, dl_ref, dd_ref, g_scr, x_scr, carry):
        first = pl.program_id(0) == 0

        @pl.when(first)
        def _():
            carry[...] = jnp.zeros_like(carry)
            dl_ref[...] = jnp.zeros_like(dl_ref)

        dyv = dy_ref[...]
        uv = u_ref[...]
        dyb = dyv.astype(BF16)
        ub = uv.astype(BF16)
        lr1 = tab_ref[6, 7:8, :]
        li1 = tab_ref[7, 7:8, :]
        for kt in range(S5_KT):
            cols = pl.ds(kt * 1024, 1024)
            colsi = pl.ds(S5_N + kt * 1024, 1024)
            g_scr[:, cols] = _dot_nt(dyb[:, kt * 256:(kt + 1) * 256], c_ref[0, kt])
            g_scr[:, colsi] = _dot_nt(dyb[:, kt * 256:(kt + 1) * 256], c_ref[1, kt])
            bur = _dot(ub[:, kt * 256:(kt + 1) * 256], b_ref[0, kt])
            bui = _dot(ub[:, kt * 256:(kt + 1) * 256], b_ref[1, kt])
            xpr = xp_ref[:, cols]
            xpi = xp_ref[:, colsi]
            lrk = lr1[:, kt * 1024:(kt + 1) * 1024]
            lik = li1[:, kt * 1024:(kt + 1) * 1024]
            x_scr[:, cols] = lrk * xpr - lik * xpi + bur
            x_scr[:, colsi] = lrk * xpi + lik * xpr + bui

        def chunk(j, c):
            cr, ci = c
            r0 = pl.multiple_of((nch - 1 - j) * 8, 8)
            gr = g_scr[pl.ds(r0, 8), pl.ds(0, S5_N)]
            gi = g_scr[pl.ds(r0, 8), pl.ds(S5_N, S5_N)]
            for lvl, (sh, _) in enumerate(SCAN_SHIFTS):
                sr = pltpu.roll(gr, 8 - sh, 0)
                si = pltpu.roll(gi, 8 - sh, 0)
                lr = tab_ref[lvl]
                li = tab_ref[3 + lvl]
                gr, gi = gr + lr * sr + li * si, gi + lr * si - li * sr
            pvr = tab_ref[6]
            pvi = tab_ref[7]
            gr, gi = gr + pvr * cr + pvi * ci, gi + pvr * ci - pvi * cr
            g_scr[pl.ds(r0, 8), pl.ds(0, S5_N)] = gr
            g_scr[pl.ds(r0, 8), pl.ds(S5_N, S5_N)] = gi
            xpr = xp_ref[pl.ds(r0, 8), pl.ds(0, S5_N)]
            xpi = xp_ref[pl.ds(r0, 8), pl.ds(S5_N, S5_N)]
            dl_ref[0] += gr * xpr + gi * xpi
            dl_ref[1] += gi * xpr - gr * xpi
            return gr[0:1, :], gi[0:1, :]

        cr, ci = lax.fori_loop(0, nch, chunk, (carry[0:1, :], carry[1:2, :]))
        carry[0:1, :] = cr
        carry[1:2, :] = ci

        for kt in range(S5_KT):
            du = jnp.zeros((tm, 256), F32)
            ukt = ub[:, kt * 256:(kt + 1) * 256]
            dykt = dyb[:, kt * 256:(kt + 1) * 256]
            for part in range(2):
                gb = g_scr[:, pl.ds(part * S5_N + kt * 1024, 1024)].astype(BF16)
                xb = x_scr[:, pl.ds(part * S5_N + kt * 1024, 1024)].astype(BF16)
                du = du + _dot_nt(gb, b_ref[part, kt])
                dbv = _dot_tn(ukt, gb)
                dcv = _dot_tn(xb, dykt)

                @pl.when(first)
                def _():
                    db_ref[part, kt] = dbv
                    dc_ref[part, kt] = dcv

                @pl.when(jnp.logical_not(first))
                def _():
                    db_ref[part, kt] += dbv
                    dc_ref[part, kt] += dcv
            du_ref[:, pl.ds(kt * 256, 256)] = du + d_ref[:, pl.ds(kt * 256, 256)] * dyv[:, kt * 256:(kt + 1) * 256]
        _accum(dd_ref, jnp.sum(dyv * uv, axis=0, keepdims=True), first)

    rev = lambda i: (nt - 1 - i, 0)
    return _carry(
        body, comm, name="s5_bwd", steps=nt,
        out_shape=(jax.ShapeDtypeStruct((t, S5_WIDTH), F32),
                   jax.ShapeDtypeStruct((2, S5_KT, 256, 1024), F32),
                   jax.ShapeDtypeStruct((2, S5_KT, 1024, 256), F32),
                   jax.ShapeDtypeStruct((2, 8, S5_N), F32),
                   jax.ShapeDtypeStruct((1, S5_WIDTH), F32)),
        in_specs=[pl.BlockSpec((tm, S5_WIDTH), rev), pl.BlockSpec((tm, S5_WIDTH), rev),
                  pl.BlockSpec((tm, 2 * S5_N), rev),
                  _const_spec((8, 8, S5_N)),
                  _const_spec((2, S5_KT, 256, 1024)), _const_spec((2, S5_KT, 1024, 256)),
                  _const_spec((1, S5_WIDTH))],
        out_specs=(pl.BlockSpec((tm, S5_WIDTH), rev),
                   pl.BlockSpec((2, S5_KT, 256, 1024), lambda i: (0, 0, 0, 0)),
                   pl.BlockSpec((2, S5_KT, 1024, 256), lambda i: (0, 0, 0, 0)),
                   pl.BlockSpec((2, 8, S5_N), lambda i: (0, 0, 0)),
                   pl.BlockSpec((1, S5_WIDTH), lambda i: (0, 0))),
        scratch_shapes=[pltpu.VMEM((tm, 2 * S5_N), F32), pltpu.VMEM((tm, 2 * S5_N), F32),
                        pltpu.VMEM((8, S5_N), F32)],
        args=(dy, u, xp, tab, bmat, cmat, dvec),
    )


def _hg_gates(z, lb):
    sg = _sigmoid(z)
    sgn = _sigmoid(-z)
    fg = lb + (1.0 - lb) * sg
    return sg, sgn, fg, jnp.log(fg), (1.0 - lb) * sgn


def _hg_decays(g, tril):
    gc = jnp.dot(tril, g, precision=HIGHEST, preferred_element_type=F32)
    mid = gc[CHUNK // 2 - 1:CHUNK // 2, :]
    last = gc[CHUNK - 1:CHUNK, :]
    return jnp.exp(gc), jnp.exp(gc - mid), jnp.exp(mid - gc), jnp.exp(last - gc), jnp.exp(last)


def _split_bf16(x):
    hi = x.astype(BF16)
    return hi, (x - hi.astype(F32)).astype(BF16)


def _hg_scores(qt, qlo, kt, klo, sl, causal):
    a = _dot_nt(qt[:, sl], kt[:, sl]) + _dot_nt(qt[:, sl], klo[:, sl]) + _dot_nt(qlo[:, sl], kt[:, sl])
    return jnp.where(causal, a, 0.0).astype(BF16)


def hgrn_fwd(q, f, v, lb, comm=None):
    t = q.shape[0]
    nc = t // CHUNK
    scale = HG_E ** -0.5

    def body(q_ref, f_ref, v_ref, lb_ref, o_ref, st_ref, state):
        @pl.when(pl.program_id(0) == 0)
        def _():
            state[...] = jnp.zeros_like(state)

        ri = lax.broadcasted_iota(jnp.int32, (CHUNK, CHUNK), 0)
        ci = lax.broadcasted_iota(jnp.int32, (CHUNK, CHUNK), 1)
        causal = ri >= ci
        tril = causal.astype(F32)
        for sub in range(HG_SUB):
            rows = pl.ds(sub * CHUNK, CHUNK)
            _, _, _, g, k = _hg_gates(f_ref[rows, :], lb_ref[...])
            eg, eq, ek, ed, el = _hg_decays(g, tril)
            qs = q_ref[rows, :] * scale
            qg = (qs * eg).astype(BF16)
            qt, qlo = _split_bf16(qs * eq)
            kt, klo = _split_bf16(k * ek)
            kd = (k * ed).astype(BF16)
            vb = v_ref[rows, :].astype(BF16)
            for h in range(HG_HEADS):
                sl = slice(h * HG_E, (h + 1) * HG_E)
                st = state[h]
                a = _hg_scores(qt, qlo, kt, klo, sl, causal)
                o_ref[rows, sl] = _dot(a, vb[:, sl]) + _dot_nt(qg[:, sl], st.astype(BF16))
                st_new = st * el[:, sl] + _dot_tn(vb[:, sl], kd[:, sl])
                state[h] = st_new
                st_ref[sub, h] = st_new

    tok = pl.BlockSpec((HG_SUB * CHUNK, HG_WIDTH), lambda i: (i, 0))
    return _carry(
        body, comm, name="hgrn_fwd", steps=nc // HG_SUB,
        out_shape=(jax.ShapeDtypeStruct((t, HG_WIDTH), F32),
                   jax.ShapeDtypeStruct((nc, HG_HEADS, HG_E, HG_E), F32)),
        in_specs=[tok, tok, tok, _const_spec((1, HG_WIDTH))],
        out_specs=(tok, pl.BlockSpec((HG_SUB, HG_HEADS, HG_E, HG_E), lambda i: (i, 0, 0, 0))),
        scratch_shapes=[pltpu.VMEM((HG_HEADS, HG_E, HG_E), F32)],
        args=(q, f, v, lb),
    )


def hgrn_bwd(do, q, f, v, lb, states, comm=None):
    t = q.shape[0]
    nc = t // CHUNK
    scale = HG_E ** -0.5

    ns = nc // HG_SUB

    def body(do_ref, q_ref, f_ref, v_ref, lb_ref, scur_ref, sprev_ref, dq_ref, df_ref, dv_ref, dlb_ref, dstate):
        first = pl.program_id(0) == 0
        has_prev = jnp.where(pl.program_id(0) < ns - 1, 1.0, 0.0)

        @pl.when(first)
        def _():
            dstate[...] = jnp.zeros_like(dstate)

        ri = lax.broadcasted_iota(jnp.int32, (CHUNK, CHUNK), 0)
        ci = lax.broadcasted_iota(jnp.int32, (CHUNK, CHUNK), 1)
        causal = ri >= ci
        tril = causal.astype(F32)
        triu = (ri <= ci).astype(F32)
        rowc = lax.broadcasted_iota(jnp.int32, (CHUNK, HG_WIDTH), 0)
        lb = lb_ref[...]
        dlb = jnp.zeros((1, HG_WIDTH), F32)
        for sub in reversed(range(HG_SUB)):
            rows = pl.ds(sub * CHUNK, CHUNK)
            sg, sgn, fg, g, k = _hg_gates(f_ref[rows, :], lb)
            eg, eq, ek, ed, el = _hg_decays(g, tril)
            qs = q_ref[rows, :] * scale
            qg = (qs * eg).astype(BF16)
            qt, qlo = _split_bf16(qs * eq)
            kt, klo = _split_bf16(k * ek)
            kd = (k * ed).astype(BF16)
            vb = v_ref[rows, :].astype(BF16)
            dob = do_ref[rows, :].astype(BF16)
            dqs_l, dk_l, dgc_l, dgl_l = [], [], [], []
            for h in range(HG_HEADS):
                sl = slice(h * HG_E, (h + 1) * HG_E)
                s0 = scur_ref[sub - 1, h] if sub > 0 else sprev_ref[HG_SUB - 1, h] * has_prev
                ds1 = dstate[h]
                ds1b = ds1.astype(BF16)
                a = _hg_scores(qt, qlo, kt, klo, sl, causal)
                da = jnp.where(causal, _dot_nt(dob[:, sl], vb[:, sl]), 0.0).astype(BF16)
                dv_ref[rows, sl] = _dot_tn(a, dob[:, sl]) + _dot_nt(kd[:, sl], ds1b)
                dkd = _dot(vb[:, sl], ds1b)
                dqt = _dot(da, kt[:, sl])
                dkt = _dot_tn(da, qt[:, sl])
                dqg = _dot(dob[:, sl], s0.astype(BF16))
                dqs_l.append(dqt * eq[:, sl] + dqg * eg[:, sl])
                dk_l.append(dkt * ek[:, sl] + dkd * ed[:, sl])
                kd_dkd = kd[:, sl].astype(F32) * dkd
                dgc_l.append(qt[:, sl].astype(F32) * dqt - kt[:, sl].astype(F32) * dkt
                             + qg[:, sl].astype(F32) * dqg - kd_dkd)
                dgl_l.append(el[:, sl] * jnp.sum(ds1 * s0, axis=0, keepdims=True)
                             + jnp.sum(kd_dkd, axis=0, keepdims=True))
                dstate[h] = ds1 * el[:, sl] + _dot_tn(dob[:, sl], qg[:, sl])
            dqs = jnp.concatenate(dqs_l, axis=1)
            dk = jnp.concatenate(dk_l, axis=1)
            dgl = jnp.concatenate(dgl_l, axis=1)
            dq_ref[rows, :] = dqs * scale
            dgc = jnp.concatenate(dgc_l, axis=1) + jnp.where(rowc == CHUNK - 1, dgl, 0.0)
            dg = jnp.dot(triu, dgc, precision=HIGHEST, preferred_element_type=F32)
            w = dg / fg - dk
            df_ref[rows, :] = w * (1.0 - lb) * sg * sgn
            dlb = dlb + jnp.sum(w * sgn, axis=0, keepdims=True)
        _accum(dlb_ref, dlb, first)

    rev = lambda i: (ns - 1 - i, 0)
    tok = pl.BlockSpec((HG_SUB * CHUNK, HG_WIDTH), rev)
    st_blk = (HG_SUB, HG_HEADS, HG_E, HG_E)
    return _carry(
        body, comm, name="hgrn_bwd", steps=ns,
        out_shape=(jax.ShapeDtypeStruct((t, HG_WIDTH), F32),
                   jax.ShapeDtypeStruct((t, HG_WIDTH), F32),
                   jax.ShapeDtypeStruct((t, HG_WIDTH), F32),
                   jax.ShapeDtypeStruct((1, HG_WIDTH), F32)),
        in_specs=[tok, tok, tok, tok, _const_spec((1, HG_WIDTH)),
                  pl.BlockSpec(st_blk, lambda i: (ns - 1 - i, 0, 0, 0)),
                  pl.BlockSpec(st_blk, lambda i: (jnp.maximum(ns - 2 - i, 0), 0, 0, 0))],
        out_specs=(tok, tok, tok, pl.BlockSpec((1, HG_WIDTH), lambda i: (0, 0))),
        scratch_shapes=[pltpu.VMEM((HG_HEADS, HG_E, HG_E), F32)],
        args=(do, q, f, v, lb, states, states),
    )


GELU_C = math.sqrt(2.0 / math.pi)


def _gelu(x):
    th = jnp.tanh(GELU_C * (x + 0.044715 * x * x * x))
    return 0.5 * x * (1.0 + th), th


def _merge_core(ys5, o, og, ga, gb, wv_ref, wt_ref, ghg, who_ref):
    ys, th = _gelu(ys5)
    ysb = ys.astype(BF16)
    va = jnp.concatenate([_dot(ysb, wv_ref[s]) for s in range(N_SHARD)], axis=1)
    vt = jnp.concatenate([_dot(ysb, wt_ref[s]) for s in range(N_SHARD)], axis=1)
    svt = _sigmoid(vt)
    ya = va * svt
    rs, ons = [], []
    for h in range(HG_HEADS):
        oh = o[:, h * HG_E:(h + 1) * HG_E]
        r = lax.rsqrt(jnp.mean(oh * oh, axis=-1, keepdims=True) + NORM_EPS)
        rs.append(r)
        ons.append(oh * r)
    on = jnp.concatenate(ons, axis=1)
    sgo = _sigmoid(og)
    o2 = on * ghg * (og * sgo)
    o2b = o2.astype(BF16)
    yb = _dot(o2b, who_ref[...])
    sa = _sigmoid(ga)
    sb = _sigmoid(gb)
    mixed = sa * ya + sb * yb
    return dict(ys=ys, th=th, ysb=ysb, va=va, svt=svt, ya=ya, rs=rs, on=on, sgo=sgo, o2b=o2b, yb=yb,
                sa=sa, sb=sb, mixed=mixed)


def merge_fwd(h, ys5, o, og, ga, gb, wv, wt, ghg, who, wmo, tm=256):
    t = h.shape[0]

    def body(h_ref, ys5_ref, o_ref, og_ref, ga_ref, gb_ref, wv_ref, wt_ref, ghg_ref, who_ref, wmo_ref, out_ref):
        c = _merge_core(ys5_ref[...], o_ref[...], og_ref[...], ga_ref[...], gb_ref[...],
                        wv_ref, wt_ref, ghg_ref[...], who_ref)
        out_ref[...] = h_ref[...] + _dot(c["mixed"].astype(BF16), wmo_ref[...])

    tok = pl.BlockSpec((tm, D_MODEL), lambda i: (i, 0))
    return pl.pallas_call(
        body, name="merge_fwd", grid=(t // tm,),
        out_shape=jax.ShapeDtypeStruct((t, D_MODEL), F32),
        in_specs=[tok, pl.BlockSpec((tm, S5_WIDTH), lambda i: (i, 0)), tok, tok, tok, tok,
                  _const_spec((N_SHARD, S5_WIDTH, 256)), _const_spec((N_SHARD, S5_WIDTH, 256)),
                  _const_spec((1, HG_WIDTH)), _const_spec((HG_WIDTH, D_MODEL)), _const_spec((D_MODEL, D_MODEL))],
        out_specs=tok,
        compiler_params=_cparams(("arbitrary",)),
    )(h, ys5, o, og, ga, gb, wv, wt, ghg, who, wmo)


def merge_bwd(dh, ys5, o, og, ga, gb, wv, wt, ghg, who, wmo, comm=None, tm=256):
    t = dh.shape[0]

    def body(dh_ref, ys5_ref, o_ref, og_ref, ga_ref, gb_ref, wv_ref, wt_ref, ghg_ref, who_ref, wmo_ref,
             dys5_ref, do_ref, dog_ref, dga_ref, dgb_ref, dghg_ref,
             mixb_ref, dhb_ref, ysb_ref, dvab_ref, dvtb_ref, o2b_ref, dybb_ref):
        ys5 = ys5_ref[...]
        o = o_ref[...]
        og = og_ref[...]
        ghg = ghg_ref[...]
        c = _merge_core(ys5, o, og, ga_ref[...], gb_ref[...], wv_ref, wt_ref, ghg, who_ref)
        dhb = dh_ref[...].astype(BF16)
        dhb_ref[...] = dhb
        mixb_ref[...] = c["mixed"].astype(BF16)
        ysb_ref[...] = c["ysb"]
        o2b_ref[...] = c["o2b"]
        dmix = _dot_nt(dhb, wmo_ref[...])
        sa, sb = c["sa"], c["sb"]
        dya = dmix * sa
        dyb = dmix * sb
        dga_ref[...] = dmix * c["ya"] * sa * (1.0 - sa)
        dgb_ref[...] = dmix * c["yb"] * sb * (1.0 - sb)
        svt = c["svt"]
        dva = (dya * svt).astype(BF16)
        dvt = (dya * c["va"] * svt * (1.0 - svt)).astype(BF16)
        dvab_ref[...] = dva
        dvtb_ref[...] = dvt
        dys = jnp.zeros((tm, S5_WIDTH), F32)
        for s in range(N_SHARD):
            dys = dys + _dot_nt(dva[:, s * 256:(s + 1) * 256], wv_ref[s]) + _dot_nt(dvt[:, s * 256:(s + 1) * 256], wt_ref[s])
        th = c["th"]
        dgelu = 0.5 * (1.0 + th) + 0.5 * ys5 * (1.0 - th * th) * GELU_C * (1.0 + 3.0 * 0.044715 * ys5 * ys5)
        dys5_ref[...] = dys * dgelu
        dybb = dyb.astype(BF16)
        dybb_ref[...] = dybb
        do2 = _dot_nt(dybb, who_ref[...])
        sgo = c["sgo"]
        sil = og * sgo
        on = c["on"]
        dog_ref[...] = do2 * on * ghg * (sgo * (1.0 + og * (1.0 - sgo)))
        _accum(dghg_ref, jnp.sum(do2 * on * sil, axis=0, keepdims=True), pl.program_id(0) == 0)
        don = do2 * ghg * sil
        dos = []
        for h in range(HG_HEADS):
            sl = slice(h * HG_E, (h + 1) * HG_E)
            m = jnp.mean(don[:, sl] * on[:, sl], axis=-1, keepdims=True)
            dos.append(c["rs"][h] * (don[:, sl] - on[:, sl] * m))
        do_ref[...] = jnp.concatenate(dos, axis=1)

    tok = pl.BlockSpec((tm, D_MODEL), lambda i: (i, 0))
    s5b = pl.BlockSpec((tm, S5_WIDTH), lambda i: (i, 0))
    f32t = jax.ShapeDtypeStruct((t, D_MODEL), F32)
    bft = jax.ShapeDtypeStruct((t, D_MODEL), BF16)
    return _carry(
        body, comm, name="merge_bwd", steps=t // tm,
        out_shape=(jax.ShapeDtypeStruct((t, S5_WIDTH), F32), f32t, f32t, f32t, f32t,
                   jax.ShapeDtypeStruct((1, HG_WIDTH), F32),
                   bft, bft, jax.ShapeDtypeStruct((t, S5_WIDTH), BF16), bft, bft, bft, bft),
        in_specs=[tok, s5b, tok, tok, tok, tok,
                  _const_spec((N_SHARD, S5_WIDTH, 256)), _const_spec((N_SHARD, S5_WIDTH, 256)),
                  _const_spec((1, HG_WIDTH)), _const_spec((HG_WIDTH, D_MODEL)), _const_spec((D_MODEL, D_MODEL))],
        out_specs=(s5b, tok, tok, tok, tok, pl.BlockSpec((1, HG_WIDTH), lambda i: (0, 0)),
                   tok, tok, s5b, tok, tok, tok, tok),
        args=(dh, ys5, o, og, ga, gb, wv, wt, ghg, who, wmo),
    )


def head_fwd_bwd(h, p, tgt, gple, wpg, wpp, gfin, tm=256):
    t = h.shape[0]

    def body(h_ref, p_ref, tgt_ref, gple_ref, wpg_ref, wpp_ref, gfin_ref,
             loss_ref, dh_ref, dgple_ref, dgfin_ref, nb_ref, dlb_ref, dppb_ref):
        first = pl.program_id(0) == 0
        hv = h_ref[...]
        gple = gple_ref[...]
        gfin = gfin_ref[...]
        n, r3 = _rms_fwd(hv, gple)
        nb = n.astype(BF16)
        nb_ref[...] = nb
        pg = _sigmoid(_dot(nb, wpg_ref[...]))
        pb = p_ref[...].astype(BF16)
        pp = jnp.concatenate([_dot(pb, wpp_ref[s]) for s in range(N_SHARD)], axis=1)
        h4 = hv + pg * pp
        y, r4 = _rms_fwd(h4, gfin)
        err = y - tgt_ref[...]
        lsum = 0.5 * jnp.sum(jnp.sum(err * err, axis=-1, keepdims=True), axis=0, keepdims=True) / D_MODEL
        _accum(loss_ref, jnp.broadcast_to(lsum, (8, 128)), first)
        dy = err * (1.0 / D_MODEL)
        dh4, dgf = _rms_bwd(h4, r4, gfin, dy)
        _accum(dgfin_ref, dgf, first)
        dpp = dh4 * pg
        dppb_ref[...] = dpp.astype(BF16)
        dl = (dh4 * pp * pg * (1.0 - pg)).astype(BF16)
        dlb_ref[...] = dl
        dn = _dot_nt(dl, wpg_ref[...])
        dx, dgp = _rms_bwd(hv, r3, gple, dn)
        _accum(dgple_ref, dgp, first)
        dh_ref[...] = dh4 + dx

    tok = pl.BlockSpec((tm, D_MODEL), lambda i: (i, 0))
    vec = pl.BlockSpec((1, D_MODEL), lambda i: (0, 0))
    bft = jax.ShapeDtypeStruct((t, D_MODEL), BF16)
    return pl.pallas_call(
        body, name="head_fwd_bwd", grid=(t // tm,),
        out_shape=(jax.ShapeDtypeStruct((8, 128), F32), jax.ShapeDtypeStruct((t, D_MODEL), F32),
                   jax.ShapeDtypeStruct((1, D_MODEL), F32), jax.ShapeDtypeStruct((1, D_MODEL), F32),
                   bft, bft, bft),
        in_specs=[tok, pl.BlockSpec((tm, PLE_DIM), lambda i: (i, 0)), tok,
                  _const_spec((1, D_MODEL)), _const_spec((D_MODEL, D_MODEL)),
                  _const_spec((N_SHARD, PLE_DIM, 256)), _const_spec((1, D_MODEL))],
        out_specs=(pl.BlockSpec((8, 128), lambda i: (0, 0)), tok, vec, vec, tok, tok, tok),
        compiler_params=_cparams(("arbitrary",)),
    )(h, p, tgt, gple, wpg, wpp, gfin)


BIG = ("ffn1_w_gate", "ffn1_w_up", "ffn1_w_down", "w_in", "s5_glu_val", "s5_glu_gate", "hg_w_out",
       "w_merge_out", "ffn2_w_gate", "ffn2_w_up", "ffn2_w_down", "ple_w_gate", "ple_w_proj")
FFN_T = ("ffn1_w_gate", "ffn1_w_up", "ffn2_w_gate", "ffn2_w_up")
BIG_SHARD = {
    "ffn1_w_gate": (FF_PAD, D_MODEL), "ffn1_w_up": (FF_PAD, D_MODEL), "ffn1_w_down": (FF_PAD, D_MODEL),
    "ffn2_w_gate": (FF_PAD, D_MODEL), "ffn2_w_up": (FF_PAD, D_MODEL), "ffn2_w_down": (FF_PAD, D_MODEL),
    "w_in": (D_MODEL, IN_COLS // N_SHARD), "s5_glu_val": (S5_WIDTH, 256), "s5_glu_gate": (S5_WIDTH, 256),
    "hg_w_out": (256, D_MODEL), "w_merge_out": (256, D_MODEL), "ple_w_gate": (256, D_MODEL),
    "ple_w_proj": (PLE_DIM, 256),
}


def _lower_bound(hb):
    return jax.nn.softmax(hb, axis=0)[0:1]


class Schedule:
    def __init__(self, wts):
        self.wts = dict(wts)
        self.grads = {}

    def before(self, kernel_name):
        return None

    def after(self, kernel_name, results):
        pass

    def grad(self, name, g):
        self.grads[name] = g


def local_step(x, p, tgt, sched, sm):
    wts = sched.wts
    rows_full = lambda w: w.reshape(N_SHARD * w.shape[1], w.shape[2])

    def carried(kernel_name, fn, *args):
        outs, results = fn(*args, comm=sched.before(kernel_name))
        sched.after(kernel_name, results)
        return outs

    def weight_grad(name, xs, ys, shard):
        kernel_name = "g_" + name
        (g,), results = tn_matmul(xs, ys, kernel_name, shard, comm=sched.before(kernel_name))
        sched.grad(name, g)
        sched.after(kernel_name, results)

    lb, lb_vjp = jax.vjp(_lower_bound, sm["hg_lower_bound"])
    s5_names = ("s5_lam_re", "s5_lam_im", "s5_log_dt", "s5_b_re", "s5_b_im", "s5_c_re", "s5_c_im")
    (lam_bar, bmat, cmat), s5_vjp = jax.vjp(s5_prep, *[sm[k] for k in s5_names])
    pw_r, pw_i = _lam_powers(lam_bar)
    bmat_b = bmat.astype(BF16)
    cmat_b = cmat.astype(BF16)

    h1, a1, b1 = carried("ffn1_fwd", ffn_fwd, x, sm["ffn1_norm"], wts["ffn1_w_gate"], wts["ffn1_w_up"],
                         wts["ffn1_w_down"], "ffn1_fwd")
    s5in, q, f, v, og, ga, gb = carried("inproj_fwd", inproj_fwd, h1, sm["mix_norm"], wts["w_in"])
    ys5, xp = carried("s5_fwd", s5_fwd, s5in, _scan_tables(pw_r, pw_i, False), bmat_b, cmat_b, sm["s5_d"])
    o, states = carried("hgrn_fwd", hgrn_fwd, q, f, v, lb)
    who = rows_full(wts["hg_w_out"])
    wmo = rows_full(wts["w_merge_out"])
    h2 = merge_fwd(h1, ys5, o, og, ga, gb, wts["s5_glu_val"], wts["s5_glu_gate"], sm["hg_out_norm"], who, wmo)
    (h3, a2, b2), _ = ffn_fwd(h2, sm["ffn2_norm"], wts["ffn2_w_gate"], wts["ffn2_w_up"], wts["ffn2_w_down"], "ffn2_fwd")
    loss, dh3, d_ple_norm, d_final_norm, npb, dlgb, dppb = head_fwd_bwd(
        h3, p, tgt, sm["ple_norm"], rows_full(wts["ple_w_gate"]), wts["ple_w_proj"], sm["final_norm"])

    gs = {"ple_norm": d_ple_norm, "final_norm": d_final_norm}
    weight_grad("ple_w_gate", npb, dlgb, "rows")
    weight_grad("ple_w_proj", p, dppb, "cols")

    (dh2, gs["ffn2_norm"], n2b, dhb2, da2, db2, s2), _ = ffn_bwd(
        dh3, h2, a2, b2, sm["ffn2_norm"], wts["ffn2_w_gate"], wts["ffn2_w_up"], wts["ffn2_w_down"], "ffn2_bwd")
    weight_grad("ffn2_w_gate", da2, n2b, "rows")
    weight_grad("ffn2_w_up", db2, n2b, "rows")
    weight_grad("ffn2_w_down", s2, dhb2, "rows")

    dys5, do, dog, dga, dgb, gs["hg_out_norm"], mixb, dh2b, ysb, dvab, dvtb, o2b, dybb = carried(
        "merge_bwd", merge_bwd,
        dh2, ys5, o, og, ga, gb, wts["s5_glu_val"], wts["s5_glu_gate"], sm["hg_out_norm"], who, wmo)
    weight_grad("w_merge_out", mixb, dh2b, "rows")
    weight_grad("s5_glu_val", ysb, dvab, "cols")
    weight_grad("s5_glu_gate", ysb, dvtb, "cols")
    weight_grad("hg_w_out", o2b, dybb, "rows")

    dq, df, dv, dlb = carried("hgrn_bwd", hgrn_bwd, do, q, f, v, lb, states)
    (gs["hg_lower_bound"],) = lb_vjp(dlb)
    du, dbmat, dcmat, dlam8, gs["s5_d"] = carried(
        "s5_bwd", s5_bwd,
        dys5, s5in, xp, _scan_tables(pw_r, pw_i, True), bmat_b, cmat_b, sm["s5_d"])
    for k, g in zip(s5_names, s5_vjp((jnp.sum(dlam8, axis=1), dbmat, dcmat))):
        gs[k] = g

    dh1, gs["mix_norm"], nmb, dprojb = carried(
        "inproj_bwd", inproj_bwd, dh2, h1, sm["mix_norm"], wts["w_in"], (du, dq, df, dv, dog, dga, dgb))
    weight_grad("w_in", nmb, dprojb, "cols")

    dx, gs["ffn1_norm"], n1b, dhb1, da1, db1, s1 = carried(
        "ffn1_bwd", ffn_bwd,
        dh1, x, a1, b1, sm["ffn1_norm"], wts["ffn1_w_gate"], wts["ffn1_w_up"], wts["ffn1_w_down"], "ffn1_bwd")
    weight_grad("ffn1_w_gate", da1, n1b, "rows")
    weight_grad("ffn1_w_up", db1, n1b, "rows")
    weight_grad("ffn1_w_down", s1, dhb1, "rows")
    return loss, dx, gs


MESH = pl.DeviceIdType.MESH
ANY = pl.BlockSpec(memory_space=pl.ANY)


def _place():
    x, y, c = lax.axis_index("x"), lax.axis_index("y"), lax.axis_index("c")
    return x, y, c


def _remote(src, dst, ssem, rsem, dev):
    return pltpu.make_async_remote_copy(src_ref=src, dst_ref=dst, send_sem=ssem, recv_sem=rsem,
                                        device_id=dev, device_id_type=MESH)


class Comm:
    def __init__(self, bufs, outs, alias, sems, hooks):
        self.bufs, self.outs, self.alias, self.sems, self.hooks = list(bufs), list(outs), alias, list(sems), hooks


def run_comm(comm, name):
    nb, no = len(comm.bufs), len(comm.outs)

    def body(*refs):
        for which in ("first", "mid", "last"):
            if which in comm.hooks:
                comm.hooks[which](refs[:nb], refs[nb:nb + no], refs[nb + no:])

    return pl.pallas_call(
        body, name=name, out_shape=tuple(comm.outs), in_specs=[ANY] * nb, out_specs=tuple([ANY] * no),
        input_output_aliases=dict(comm.alias), scratch_shapes=comm.sems,
    )(*comm.bufs)


PLACE_ROWS = {1024: 256, 704: 352, 512: 256, 256: 256}


def place_shards(shards, padded_rows, comm, name):
    n, nb, no = len(shards), len(comm.bufs), len(comm.outs)
    stage_rows = max(PLACE_ROWS.values())
    stage_cols = max(s.shape[1] for s in shards)

    def body(*refs):
        ins, cb = refs[:n], refs[n:n + nb]
        outs, co = refs[n + nb:2 * n + nb], refs[2 * n + nb:2 * n + nb + no]
        stage_f32, stage_bf16, zeros, sem = refs[2 * n + nb + no:2 * n + nb + no + 4]
        cs = refs[2 * n + nb + no + 4:]
        chip = 2 * lax.axis_index("x") + lax.axis_index("y")
        zeros[...] = jnp.zeros_like(zeros)
        comm.hooks["first"](cb, co, cs)
        for w in range(n):
            if w == n // 2:
                comm.hooks["mid"](cb, co, cs)
            r0, cols = ins[w].shape
            step = PLACE_ROWS[r0]
            src32 = stage_f32.at[pl.ds(0, step), pl.ds(0, cols)]
            dst16 = stage_bf16.at[pl.ds(0, step), pl.ds(0, cols)]
            for row in range(0, r0, step):
                pltpu.sync_copy(ins[w].at[pl.ds(row, step), :], src32)
                dst16[...] = src32[...].astype(BF16)
                pltpu.sync_copy(dst16, outs[w].at[chip, pl.ds(row, step), :])
            pad = outs[w].shape[1] - r0
            if pad:
                cp = pltpu.make_async_copy(zeros.at[pl.ds(0, pad), pl.ds(0, cols)],
                                           outs[w].at[chip, pl.ds(r0, pad), :], sem)
                cp.start()
                cp.wait()
        comm.hooks["last"](cb, co, cs)

    res = pl.pallas_call(
        body, name=name,
        out_shape=tuple(jax.ShapeDtypeStruct((N_SHARD, r, s.shape[1]), BF16) for s, r in zip(shards, padded_rows))
        + tuple(comm.outs),
        in_specs=[ANY] * (n + nb), out_specs=tuple([ANY] * (n + no)),
        input_output_aliases={n + i: n + o for i, o in comm.alias.items()},
        scratch_shapes=[pltpu.VMEM((stage_rows, stage_cols), F32), pltpu.VMEM((stage_rows, stage_cols), BF16),
                        pltpu.VMEM((FF_PAD - FF_SHARD, D_MODEL), BF16), pltpu.SemaphoreType.DMA] + comm.sems,
        compiler_params=pltpu.CompilerParams(vmem_limit_bytes=VMEM_LIMIT),
    )(*shards, *comm.bufs)
    return res[:n], res[n:]


def _carry(body, comm, *, name, steps, out_shape, in_specs, out_specs, args, scratch_shapes=()):
    out_shape, out_specs, scratch_shapes = tuple(out_shape), tuple(out_specs), list(scratch_shapes)
    if comm is None:
        res = pl.pallas_call(body, name=name, grid=(steps,), out_shape=out_shape, in_specs=list(in_specs),
                             out_specs=out_specs, scratch_shapes=scratch_shapes,
                             compiler_params=_cparams(("arbitrary",)))(*args)
        return tuple(res), ()
    n_in, n_out, n_scr = len(args), len(out_shape), len(scratch_shapes)
    nb, no = len(comm.bufs), len(comm.outs)

    def wrapped(*refs):
        ins, cb = refs[:n_in], refs[n_in:n_in + nb]
        o0 = n_in + nb
        outs, co = refs[o0:o0 + n_out], refs[o0 + n_out:o0 + n_out + no]
        s0 = o0 + n_out + no
        scr, cs = refs[s0:s0 + n_scr], refs[s0 + n_scr:]
        step = pl.program_id(0)

        def hook(which, at):
            if which in comm.hooks:
                pl.when(step == at)(lambda: comm.hooks[which](cb, co, cs))

        hook("first", 0)
        hook("mid", steps // 2)
        body(*ins, *outs, *scr)
        hook("last", steps - 1)

    res = pl.pallas_call(
        wrapped, name=name, grid=(steps,), out_shape=out_shape + tuple(comm.outs),
        in_specs=list(in_specs) + [ANY] * nb, out_specs=out_specs + (ANY,) * no,
        scratch_shapes=scratch_shapes + comm.sems,
        input_output_aliases={n_in + i: n_out + o for i, o in comm.alias.items()},
        compiler_params=_cparams(("arbitrary",)),
    )(*args, *comm.bufs)
    return tuple(res[:n_out]), tuple(res[n_out:])


def gather_comm(bufs):
    n = len(bufs)

    def copies(outs, sems):
        s_own, r_own, s_fwd, r_fwd, s_sib, r_sib = sems
        x, y, c = _place()
        me = 2 * x + y
        nbr = ((1 - x, y), (x, 1 - y))
        nbr_id = (2 * (1 - x) + y, 2 * x + (1 - y))
        diag_id = 2 * (1 - x) + (1 - y)
        sib = (x, y, 1 - c)

        def rows(w, q=None):
            r = outs[w].shape[1]
            if q is None:
                return pl.ds(pl.multiple_of(c * (r // 2), 16), r // 2)
            return pl.ds(pl.multiple_of(c * (r // 2) + q * (r // 4), 16), r // 4)

        def own(w, j):
            piece = outs[w].at[me, rows(w)]
            return _remote(piece, piece, s_own.at[w, j], r_own.at[w, j], (nbr[j][0], nbr[j][1], c))

        def from_nbr(w, j):
            piece = outs[w].at[nbr_id[j], rows(w)]
            return _remote(piece, piece, s_own.at[w, j], r_own.at[w, j], (nbr[j][0], nbr[j][1], c))

        def fwd(w, j):
            piece = outs[w].at[nbr_id[j], rows(w, j)]
            return _remote(piece, piece, s_fwd.at[w, j], r_fwd.at[w, j], (nbr[1 - j][0], nbr[1 - j][1], c))

        def from_diag(w, j):
            piece = outs[w].at[diag_id, rows(w, j)]
            return _remote(piece, piece, s_fwd.at[w, j], r_fwd.at[w, j], (nbr[1 - j][0], nbr[1 - j][1], c))

        def to_sib(w, k):
            piece = (outs[w].at[nbr_id[k], rows(w)] if k < 2 else outs[w].at[diag_id, rows(w, k - 2)])
            return _remote(piece, piece, s_sib.at[w, k], r_sib.at[w, k], sib)

        def from_sib(w, k):
            r = outs[w].shape[1]
            if k < 2:
                piece = outs[w].at[nbr_id[k], pl.ds(pl.multiple_of((1 - c) * (r // 2), 16), r // 2)]
            else:
                piece = outs[w].at[diag_id, pl.ds(pl.multiple_of((1 - c) * (r // 2) + (k - 2) * (r // 4), 16), r // 4)]
            return _remote(piece, piece, s_sib.at[w, k], r_sib.at[w, k], sib)

        return own, from_nbr, fwd, from_diag, to_sib, from_sib

    def first(_, outs, sems):
        own = copies(outs, sems)[0]
        for w in range(n):
            own(w, 0).start()
            own(w, 1).start()

    def mid(_, outs, sems):
        _, from_nbr, fwd, _, to_sib, _ = copies(outs, sems)
        for w in range(n):
            for j in range(2):
                from_nbr(w, j).wait_recv()
                fwd(w, j).start()
                to_sib(w, j).start()

    def last(_, outs, sems):
        own, _, fwd, from_diag, to_sib, from_sib = copies(outs, sems)
        for w in range(n):
            for j in range(2):
                from_diag(w, j).wait_recv()
                to_sib(w, 2 + j).start()
        for w in range(n):
            for k in range(4):
                from_sib(w, k).wait_recv()
        for w in range(n):
            for j in range(2):
                own(w, j).wait_send()
                fwd(w, j).wait_send()
            for k in range(4):
                to_sib(w, k).wait_send()

    dma = pltpu.SemaphoreType.DMA
    return Comm(bufs, [jax.ShapeDtypeStruct(b.shape, b.dtype) for b in bufs], {w: w for w in range(n)},
                [dma((n, 2)), dma((n, 2)), dma((n, 2)), dma((n, 2)), dma((n, 4)), dma((n, 4))],
                {"first": first, "mid": mid, "last": last})


def _start_wait(make):
    def first(bufs, outs, sems):
        for cp in make(bufs, outs, sems):
            cp.start()

    def last(bufs, outs, sems):
        for cp in make(bufs, outs, sems):
            cp.wait()

    return {"first": first, "last": last}


def exchange_comm(grads):
    n = len(grads)

    def make(ins, outs, sems):
        x, y, c = _place()
        cps = []
        for w in range(n):
            half = ins[w].shape[1] // 2
            src = ins[w].at[:, pl.ds(pl.multiple_of((1 - c) * half, 8), half), :]
            cps.append(_remote(src, outs[w], sems[0].at[w], sems[1].at[w], (x, y, 1 - c)))
        return cps

    dma = pltpu.SemaphoreType.DMA
    return Comm(grads, [jax.ShapeDtypeStruct((N_SHARD, g.shape[1] // 2, g.shape[2]), g.dtype) for g in grads],
                {}, [dma((n,)), dma((n,))], _start_wait(make))


def scatter_comm(sums):
    n = len(sums)

    def make(ins, outs, sems):
        x, y, c = _place()
        chips = ((1 - x, y), (x, 1 - y), (1 - x, 1 - y))
        return [_remote(ins[w].at[2 * ch[0] + ch[1]], outs[w].at[j], sems[0].at[w, j], sems[1].at[w, j],
                        (ch[0], ch[1], c))
                for w in range(n) for j, ch in enumerate(chips)]

    dma = pltpu.SemaphoreType.DMA
    return Comm(sums, [jax.ShapeDtypeStruct((3,) + s.shape[1:], s.dtype) for s in sums],
                {}, [dma((n, 3)), dma((n, 3))], _start_wait(make))


def join_comm(shards):
    n = len(shards)

    def make(_, outs, sems):
        x, y, c = _place()
        cps = []
        for w in range(n):
            half = outs[w].shape[0] // 2
            mine = outs[w].at[pl.ds(pl.multiple_of(c * half, 8), half), :]
            cps.append(_remote(mine, mine, sems[0].at[w], sems[1].at[w], (x, y, 1 - c)))
        return cps

    dma = pltpu.SemaphoreType.DMA
    return Comm(shards, [jax.ShapeDtypeStruct(s.shape, s.dtype) for s in shards], {w: w for w in range(n)},
                [dma((n,)), dma((n,))], _start_wait(make))


def allreduce_small(vec, comm):
    half = vec.shape[0] // 2
    nb, no = len(comm.bufs), len(comm.outs)

    def body(*refs):
        v_ref, cb, o_ref, co = refs[0], refs[1:1 + nb], refs[1 + nb], refs[2 + nb:2 + nb + no]
        pair, chips_buf, s1, r1, s2, r2, s3, r3 = refs[2 + nb + no:10 + nb + no]
        cs = refs[10 + nb + no:]
        comm.hooks["first"](cb, co, cs)
        x, y, c = _place()
        chip = 2 * x + y
        sib = (x, y, 1 - c)
        mine = pl.ds(pl.multiple_of(c * half, 8), half)
        other = pl.ds(pl.multiple_of((1 - c) * half, 8), half)
        to_sib = _remote(v_ref.at[other], pair, s1, r1, sib)
        to_sib.start()
        to_sib.wait()
        chips_buf[chip] = v_ref[mine, :] + pair[...]
        sends = [_remote(chips_buf.at[chip], chips_buf.at[chip], s2.at[j], r2.at[j], (ch[0], ch[1], c))
                 for j, ch in enumerate(((1 - x, y), (x, 1 - y), (1 - x, 1 - y)))]
        for cp in sends:
            cp.start()
        for cp in sends:
            cp.wait()
        o_ref[mine, :] = (chips_buf[0] + chips_buf[1]) + (chips_buf[2] + chips_buf[3])
        back = _remote(o_ref.at[mine], o_ref.at[mine], s3, r3, sib)
        back.start()
        back.wait()
        comm.hooks["last"](cb, co, cs)

    dma = pltpu.SemaphoreType.DMA
    vmem = pl.BlockSpec(memory_space=pltpu.VMEM)
    res = pl.pallas_call(
        body, name="allreduce_small",
        out_shape=(jax.ShapeDtypeStruct(vec.shape, F32),) + tuple(comm.outs),
        in_specs=[vmem] + [ANY] * nb,
        out_specs=(vmem,) + (ANY,) * no,
        input_output_aliases={1 + i: 1 + o for i, o in comm.alias.items()},
        scratch_shapes=[pltpu.VMEM((half, 128), F32), pltpu.VMEM((N_SHARD, half, 128), F32),
                        dma, dma, dma((3,)), dma((3,)), dma, dma] + comm.sems,
        compiler_params=pltpu.CompilerParams(vmem_limit_bytes=VMEM_LIMIT),
    )(vec, *comm.bufs)
    return res[0], res[1:]


REDUCE_ROW_BLOCKS = 2


def add_own_half(place, g, recv, name):
    _, r, cc = g.shape
    half = r // 2
    nb = REDUCE_ROW_BLOCKS
    tile = half // nb

    def body(p_ref, g_ref, r_ref, o_ref, ob_ref):
        s = g_ref[...] + r_ref[...]
        ob_ref[...] = s.astype(BF16)

        @pl.when(pl.program_id(1) == p_ref[0])
        def _():
            o_ref[...] = s

    blk = (None, tile, cc)
    return pl.pallas_call(
        body, name=name,
        grid_spec=pltpu.PrefetchScalarGridSpec(
            num_scalar_prefetch=1, grid=(nb, N_SHARD),
            in_specs=[pl.BlockSpec(blk, lambda i, s, p_ref: (s, p_ref[1] * nb + i, 0)),
                      pl.BlockSpec(blk, lambda i, s, p_ref: (s, i, 0))],
            out_specs=(pl.BlockSpec((tile, cc), lambda i, s, p_ref: (i, 0)),
                       pl.BlockSpec(blk, lambda i, s, p_ref: (s, i, 0)))),
        out_shape=(jax.ShapeDtypeStruct((half, cc), F32),
                   jax.ShapeDtypeStruct((N_SHARD, half, cc), BF16)),
        compiler_params=_cparams(("arbitrary", "arbitrary")),
    )(place, g, recv)


def add_chip_sums(place, own, recv, name):
    half, cc = own.shape
    nb = REDUCE_ROW_BLOCKS
    tile = half // nb

    def body(s_ref, o_ref, r_ref, out_ref):
        del s_ref
        acc = o_ref[...] + r_ref[0].astype(F32)
        acc = acc + r_ref[1].astype(F32)
        out_ref[...] = acc + r_ref[2].astype(F32)

    return pl.pallas_call(
        body, name=name,
        grid_spec=pltpu.PrefetchScalarGridSpec(
            num_scalar_prefetch=1, grid=(nb,),
            in_specs=[pl.BlockSpec((tile, cc), lambda i, s_ref: (i, 0)),
                      pl.BlockSpec((3, tile, cc), lambda i, s_ref: (0, i, 0))],
            out_specs=pl.BlockSpec((tile, cc), lambda i, s_ref: (s_ref[1] * nb + i, 0))),
        out_shape=jax.ShapeDtypeStruct((2 * half, cc), F32),
        compiler_params=_cparams(("arbitrary",)),
    )(place, own, recv)


def adamw(w, m, v, g, name, copy_g=False):
    r, cc = w.shape
    tr = next(t for t in (256, 352, r) if r % t == 0)
    bc1 = 1.0 / (1.0 - ADAM_B1 ** ADAM_STEP)
    bc2 = 1.0 / (1.0 - ADAM_B2 ** ADAM_STEP)

    def body(w_ref, m_ref, v_ref, g_ref, d_ref, mo_ref, vo_ref, *go_ref):
        gv = g_ref[...]
        mn = ADAM_B1 * m_ref[...] + (1.0 - ADAM_B1) * gv
        vn = ADAM_B2 * v_ref[...] + (1.0 - ADAM_B2) * (gv * gv)
        mo_ref[...] = mn
        vo_ref[...] = vn
        d_ref[...] = -ADAM_LR * ((mn * bc1) / (jnp.sqrt(vn * bc2) + ADAM_EPS) + ADAM_WD * w_ref[...])
        if copy_g:
            go_ref[0][...] = gv

    blk = pl.BlockSpec((tr, cc), lambda i: (i, 0))
    shp = jax.ShapeDtypeStruct((r, cc), F32)
    nout = 4 if copy_g else 3
    return pl.pallas_call(
        body, name=name, grid=(r // tr,),
        out_shape=(shp,) * nout, in_specs=[blk] * 4, out_specs=(blk,) * nout,
        compiler_params=_cparams(("arbitrary",)),
    )(w, m, v, g)


GATHER_FIRST = ("ffn1_w_gate", "ffn1_w_up", "ffn1_w_down")
GATHER_ON = {"ffn1_fwd": ("w_in",),
             "inproj_fwd": ("s5_glu_val", "s5_glu_gate", "hg_w_out", "w_merge_out"),
             "s5_fwd": ("ffn2_w_gate", "ffn2_w_up"),
             "hgrn_fwd": ("ffn2_w_down", "ple_w_gate", "ple_w_proj")}
REDUCE = ((("ple_w_gate", "ple_w_proj", "ffn2_w_gate", "ffn2_w_up", "ffn2_w_down"), "merge_bwd", "hgrn_bwd"),
          (("w_merge_out", "s5_glu_val", "s5_glu_gate", "hg_w_out"), "s5_bwd", "inproj_bwd"),
          (("w_in",), None, "ffn1_bwd"),
          (("ffn1_w_gate",), "g_ffn1_w_up", "g_ffn1_w_down"),
          (("ffn1_w_up",), "g_ffn1_w_down", None),
          (("ffn1_w_down",), None, None))


def merge_comms(comms):
    if len(comms) == 1:
        return comms[0], [len(comms[0].outs)]
    bufs, outs, sems, alias, spans = [], [], [], {}, []
    for c in comms:
        spans.append((len(bufs), len(bufs) + len(c.bufs), len(outs), len(outs) + len(c.outs),
                      len(sems), len(sems) + len(c.sems)))
        alias.update({len(bufs) + i: len(outs) + o for i, o in c.alias.items()})
        bufs, outs, sems = bufs + c.bufs, outs + c.outs, sems + c.sems

    def hook(which):
        def run(b, o, s):
            for c, (b0, b1, o0, o1, s0, s1) in zip(comms, spans):
                if which in c.hooks:
                    c.hooks[which](b[b0:b1], o[o0:o1], s[s0:s1])
        return run

    hooks = {w: hook(w) for w in ("first", "mid", "last") if any(w in c.hooks for c in comms)}
    return Comm(bufs, outs, alias, sems, hooks), [len(c.outs) for c in comms]


class DistSchedule(Schedule):
    def __init__(self, w_rows, chip, core):
        first = gather_comm([_gather_buffer(k, w_rows[k], chip) for k in GATHER_FIRST])
        later = [k for k in BIG if k not in GATHER_FIRST]
        placed, gathered = place_shards([w_rows[k] for k in later], [BIG_SHARD[k][0] for k in later], first,
                                        "place_shards_gather_ffn1")
        super().__init__(zip(GATHER_FIRST, gathered))
        self.bufs = dict(zip(later, placed))
        self.place = jnp.stack([chip, core])
        self.sums, self.halves = {}, {}

    def _exchange(self, names):
        return exchange_comm([self.grads[k] for k in names])

    def _scatter(self, names):
        return scatter_comm([self.sums[k][1] for k in names])

    def _pair_sums(self, names, recv):
        for k, r in zip(names, recv):
            self.sums[k] = add_own_half(self.place, self.grads[k], r, "pair_sum_" + k)

    def _chip_sums(self, names, recv):
        for k, r in zip(names, recv):
            self.halves[k] = add_chip_sums(self.place, self.sums[k][0], r, "chip_sum_" + k)

    def before(self, kernel_name):
        comms, takers = [], []
        if kernel_name in GATHER_ON:
            names = GATHER_ON[kernel_name]
            comms.append(gather_comm([self.bufs[k] for k in names]))
            takers.append(lambda res, names=names: self.wts.update(zip(names, res)))
        for names, exchange_on, scatter_on in REDUCE:
            if kernel_name == exchange_on:
                comms.append(self._exchange(names))
                takers.append(lambda res, names=names: self._pair_sums(names, res))
            if kernel_name == scatter_on:
                if exchange_on is None:
                    self._pair_sums(names, run_comm(self._exchange(names), "exchange_" + names[0]))
                comms.append(self._scatter(names))
                takers.append(lambda res, names=names: self._chip_sums(names, res))
        if not comms:
            return None
        merged, counts = merge_comms(comms)
        self.pending = (takers, counts)
        return merged

    def after(self, kernel_name, results):
        if not results:
            return
        takers, counts = self.pending
        start = 0
        for take, count in zip(takers, counts):
            take(results[start:start + count])
            start += count

    def finish(self, small):
        tail = [names for names, _, scatter_on in REDUCE if scatter_on is None]
        alone = [k for names, exchange_on, scatter_on in REDUCE if scatter_on is None and exchange_on is None
                 for k in names]
        self._pair_sums(alone, run_comm(self._exchange(alone), "exchange_tail"))
        tail = [k for names in tail for k in names]
        early = [k for k in BIG if k not in tail]
        both, counts = merge_comms([self._scatter(tail), join_comm([self.halves[k] for k in early])])
        total, res = allreduce_small(small, both)
        self._chip_sums(tail, res[:counts[0]])
        full = dict(zip(early, res[counts[0]:]))
        full.update(zip(tail, run_comm(join_comm([self.halves[k] for k in tail]), "join_tail")))
        return full, total


SMALL = ("ffn1_norm", "mix_norm", "s5_lam_re", "s5_lam_im", "s5_log_dt", "s5_b_re", "s5_b_im", "s5_c_re",
         "s5_c_im", "s5_d", "hg_lower_bound", "hg_out_norm", "ffn2_norm", "ple_norm", "final_norm")
WEIGHTS = ("ffn1_norm", "ffn1_w_gate", "ffn1_w_up", "ffn1_w_down", "mix_norm", "w_in", "s5_lam_re", "s5_lam_im",
           "s5_log_dt", "s5_b_re", "s5_b_im", "s5_c_re", "s5_c_im", "s5_d", "s5_glu_val", "s5_glu_gate",
           "hg_lower_bound", "hg_out_norm", "hg_w_out", "w_merge_out", "ffn2_norm", "ffn2_w_gate", "ffn2_w_up",
           "ffn2_w_down", "ple_norm", "ple_w_gate", "ple_w_proj", "final_norm")


def _as_rows(name, w):
    return jnp.swapaxes(w[0], 0, 1) if name in FFN_T else w[0]


def _from_rows(name, w):
    return (jnp.swapaxes(w, 0, 1) if name in FFN_T else w)[None]


def _gather_buffer(name, w_rows, chip):
    r, c = BIG_SHARD[name]
    shard = jnp.pad(w_rows.astype(BF16), ((0, r - w_rows.shape[0]), (0, 0)))
    return lax.dynamic_update_slice(jnp.zeros((N_SHARD, r, c), BF16), shard[None], (chip, 0, 0))


def _pack(parts):
    flat = jnp.concatenate([jnp.zeros((128,), F32)] + [a.reshape(-1) for a in parts])
    rows = -(-flat.shape[0] // 2048) * 16
    return jnp.pad(flat, (0, rows * 128 - flat.shape[0])).reshape(rows, 128)


def _unpack(vec, likes):
    flat = vec.reshape(-1)
    out, off = [], 128
    for a in likes:
        out.append(flat[off:off + a.size].reshape(a.shape))
        off += a.size
    return out


def _small_view(name, w):
    if name.startswith("s5_") and name != "s5_d":
        return w[0]
    if name == "final_norm":
        return w.reshape(1, D_MODEL)
    return w


def kernel(x, p, ffn1_norm, ffn1_w_gate, ffn1_w_up, ffn1_w_down, mix_norm, w_in, s5_lam_re, s5_lam_im, s5_log_dt, s5_b_re, s5_b_im, s5_c_re, s5_c_im, s5_d, s5_glu_val, s5_glu_gate, hg_lower_bound, hg_out_norm, hg_w_out, w_merge_out, ffn2_norm, ffn2_w_gate, ffn2_w_up, ffn2_w_down, ple_norm, ple_w_gate, ple_w_proj, final_norm, loss_target, m_ffn1_norm, m_ffn1_w_gate, m_ffn1_w_up, m_ffn1_w_down, m_mix_norm, m_w_in, m_s5_lam_re, m_s5_lam_im, m_s5_log_dt, m_s5_b_re, m_s5_b_im, m_s5_c_re, m_s5_c_im, m_s5_d, m_s5_glu_val, m_s5_glu_gate, m_hg_lower_bound, m_hg_out_norm, m_hg_w_out, m_w_merge_out, m_ffn2_norm, m_ffn2_w_gate, m_ffn2_w_up, m_ffn2_w_down, m_ple_norm, m_ple_w_gate, m_ple_w_proj, m_final_norm, v_ffn1_norm, v_ffn1_w_gate, v_ffn1_w_up, v_ffn1_w_down, v_mix_norm, v_w_in, v_s5_lam_re, v_s5_lam_im, v_s5_log_dt, v_s5_b_re, v_s5_b_im, v_s5_c_re, v_s5_c_im, v_s5_d, v_s5_glu_val, v_s5_glu_gate, v_hg_lower_bound, v_hg_out_norm, v_hg_w_out, v_w_merge_out, v_ffn2_norm, v_ffn2_w_gate, v_ffn2_w_up, v_ffn2_w_down, v_ple_norm, v_ple_w_gate, v_ple_w_proj, v_final_norm):
    given = dict(locals())
    wv = {k: given[k] for k in WEIGHTS}
    mv = {k: given["m_" + k] for k in WEIGHTS}
    vv = {k: given["v_" + k] for k in WEIGHTS}

    core = lax.axis_index("c").astype(jnp.int32)
    chip = (2 * lax.axis_index("x") + lax.axis_index("y")).astype(jnp.int32)
    w_rows = {k: _as_rows(k, wv[k]) for k in BIG}
    sched = DistSchedule(w_rows, chip, core)
    sm = {k: _small_view(k, wv[k]) for k in SMALL}

    loss_blk, dx, gsm = local_step(x[0], p[0, 0], loss_target[0], sched, sm)

    small_likes = [wv[k] for k in SMALL]
    packed = _pack([gsm[k] for k in SMALL])
    packed = packed.at[0, 0].set(loss_blk[0, 0])
    full, total = sched.finish(packed)
    loss = total[0, 0]
    gsmall = dict(zip(SMALL, _unpack(total, small_likes)))

    grads, deltas, new_m, new_v = {}, {}, {}, {}
    for k in BIG:
        padded = full[k].shape != w_rows[k].shape
        res = adamw(w_rows[k], _as_rows(k, mv[k]), _as_rows(k, vv[k]), full[k], "adamw_" + k, copy_g=padded)
        grads[k] = _from_rows(k, res[3] if padded else full[k])
        deltas[k], new_m[k], new_v[k] = (_from_rows(k, a) for a in res[:3])
    sw = _pack([wv[k] for k in SMALL])
    smm = _pack([mv[k] for k in SMALL])
    svv = _pack([vv[k] for k in SMALL])
    sd, smn, svn = adamw(sw, smm, svv, total, "adamw_small")
    for k, d, mn, vn in zip(SMALL, _unpack(sd, small_likes), _unpack(smn, small_likes), _unpack(svn, small_likes)):
        grads[k], deltas[k], new_m[k], new_v[k] = gsmall[k], d, mn, vn

    return (loss, dx[None], *[grads[k] for k in WEIGHTS], *[deltas[k] for k in WEIGHTS],
            *[new_m[k] for k in WEIGHTS], *[new_v[k] for k in WEIGHTS])
```

```python
import math

import jax
import jax.numpy as jnp
from jax import lax
from jax.experimental import pallas as pl
from jax.experimental.pallas import tpu as pltpu

F32 = jnp.float32
BF16 = jnp.bfloat16

D_MODEL = 1024
D_FF = 2816
N_SHARD = 4
FF_SHARD = D_FF // N_SHARD
FF_PAD = 768
NORM_EPS = 1e-6
PLE_DIM = 256

S5_WIDTH = 512
S5_GROUPS = 32
S5_GROUP = 16
S5_STATE = 64
S5_N = S5_GROUPS * S5_STATE
S5_KT = 2

HG_HEADS = 8
HG_E = 128
HG_WIDTH = 1024
CHUNK = 64
HG_SUB = 4
IN_COLS = S5_WIDTH + 4 * HG_WIDTH + 2 * D_MODEL
IN_SPLITS = (0, 512, 1536, 2560, 3584, 4608, 5632, 6656)

ADAM_LR = 0.001
ADAM_B1 = 0.9
ADAM_B2 = 0.999
ADAM_EPS = 1e-08
ADAM_WD = 0.01
ADAM_STEP = 10

VMEM_LIMIT = 60 * 1024 * 1024
HIGHEST = lax.Precision.HIGHEST


def _cparams(sem=None, **kw):
    return pltpu.CompilerParams(dimension_semantics=sem, vmem_limit_bytes=VMEM_LIMIT, **kw)


def _const_spec(shape):
    nd = len(shape)
    return pl.BlockSpec(shape, lambda *_: (0,) * nd, pipeline_mode=pl.Buffered(1))


def _dot(a, b):
    return jnp.dot(a, b, preferred_element_type=F32)


def _dot_nt(a, b):
    return lax.dot_general(a, b, (((1,), (1,)), ((), ())), preferred_element_type=F32)


def _dot_tn(a, b):
    return lax.dot_general(a, b, (((0,), (0,)), ((), ())), preferred_element_type=F32)


def _sigmoid(x):
    return 1.0 / (1.0 + jnp.exp(-x))


def _rms_fwd(x, g):
    r = lax.rsqrt(jnp.mean(x * x, axis=-1, keepdims=True) + NORM_EPS)
    return x * r * g, r


def _rms_bwd(x, r, g, dy):
    xh = x * r
    dyg = dy * g
    m = jnp.mean(dyg * xh, axis=-1, keepdims=True)
    return r * (dyg - xh * m), jnp.sum(dy * xh, axis=0, keepdims=True)


def _accum(ref, val, first):
    @pl.when(first)
    def _():
        ref[...] = val

    @pl.when(jnp.logical_not(first))
    def _():
        ref[...] += val


def ffn_fwd(h, gain, wg, wu, wd, name, comm=None, tm=512):
    t = h.shape[0]

    def body(h_ref, g_ref, wg_ref, wu_ref, wd_ref, o_ref, a_ref, b_ref):
        hv = h_ref[...]
        n, _ = _rms_fwd(hv, g_ref[...])
        nb = n.astype(BF16)
        acc = jnp.zeros((tm, D_MODEL), F32)
        for s in range(N_SHARD):
            a = _dot_nt(nb, wg_ref[s])
            b = _dot_nt(nb, wu_ref[s])
            a_ref[s] = a.astype(BF16)
            b_ref[s] = b.astype(BF16)
            sv = (a * _sigmoid(a) * b).astype(BF16)
            acc = acc + _dot(sv, wd_ref[s])
        o_ref[...] = hv + 0.5 * acc

    return _carry(
        body, comm, name=name, steps=t // tm,
        out_shape=(jax.ShapeDtypeStruct((t, D_MODEL), F32),
                   jax.ShapeDtypeStruct((N_SHARD, t, FF_PAD), BF16),
                   jax.ShapeDtypeStruct((N_SHARD, t, FF_PAD), BF16)),
        in_specs=[pl.BlockSpec((tm, D_MODEL), lambda i: (i, 0)),
                  _const_spec((1, D_MODEL)),
                  _const_spec((N_SHARD, FF_PAD, D_MODEL)),
                  _const_spec((N_SHARD, FF_PAD, D_MODEL)),
                  _const_spec((N_SHARD, FF_PAD, D_MODEL))],
        out_specs=(pl.BlockSpec((tm, D_MODEL), lambda i: (i, 0)),
                   pl.BlockSpec((N_SHARD, tm, FF_PAD), lambda i: (0, i, 0)),
                   pl.BlockSpec((N_SHARD, tm, FF_PAD), lambda i: (0, i, 0))),
        args=(h, gain, wg, wu, wd),
    )


def ffn_bwd(dho, h, a, b, gain, wg, wu, wd, name, comm=None, tm=256):
    t = h.shape[0]

    def body(dho_ref, h_ref, a_ref, b_ref, g_ref, wg_ref, wu_ref, wd_ref,
             dh_ref, dg_ref, nb_ref, dhb_ref, da_ref, db_ref, s_ref):
        hv = h_ref[...]
        g = g_ref[...]
        n, r = _rms_fwd(hv, g)
        nb_ref[...] = n.astype(BF16)
        dhalf = (0.5 * dho_ref[...]).astype(BF16)
        dhb_ref[...] = dhalf
        dn = jnp.zeros((tm, D_MODEL), F32)
        for s in range(N_SHARD):
            av = a_ref[s].astype(F32)
            bv = b_ref[s].astype(F32)
            sg = _sigmoid(av)
            sil = av * sg
            s_ref[s] = (sil * bv).astype(BF16)
            ds = _dot_nt(dhalf, wd_ref[s])
            da = (ds * bv * (sg * (1.0 + av * (1.0 - sg)))).astype(BF16)
            db = (ds * sil).astype(BF16)
            da_ref[s] = da
            db_ref[s] = db
            dn = dn + _dot(da, wg_ref[s]) + _dot(db, wu_ref[s])
        dx, dg = _rms_bwd(hv, r, g, dn)
        dh_ref[...] = dho_ref[...] + dx
        _accum(dg_ref, dg, pl.program_id(0) == 0)

    tok = pl.BlockSpec((tm, D_MODEL), lambda i: (i, 0))
    hid = pl.BlockSpec((N_SHARD, tm, FF_PAD), lambda i: (0, i, 0))
    return _carry(
        body, comm, name=name, steps=t // tm,
        out_shape=(jax.ShapeDtypeStruct((t, D_MODEL), F32),
                   jax.ShapeDtypeStruct((1, D_MODEL), F32),
                   jax.ShapeDtypeStruct((t, D_MODEL), BF16),
                   jax.ShapeDtypeStruct((t, D_MODEL), BF16),
                   jax.ShapeDtypeStruct((N_SHARD, t, FF_PAD), BF16),
                   jax.ShapeDtypeStruct((N_SHARD, t, FF_PAD), BF16),
                   jax.ShapeDtypeStruct((N_SHARD, t, FF_PAD), BF16)),
        in_specs=[tok, tok, hid, hid, _const_spec((1, D_MODEL)),
                  _const_spec((N_SHARD, FF_PAD, D_MODEL)),
                  _const_spec((N_SHARD, FF_PAD, D_MODEL)),
                  _const_spec((N_SHARD, FF_PAD, D_MODEL))],
        out_specs=(tok, pl.BlockSpec((1, D_MODEL), lambda i: (0, 0)), tok, tok, hid, hid, hid),
        args=(dho, h, a, b, gain, wg, wu, wd),
    )


TN_VMEM_BUDGET = 44 * 1024 * 1024


def tn_matmul(x, y, name, shard, comm=None):
    x3, y3 = x.ndim == 3, y.ndim == 3
    t = x.shape[-2]
    m = x.shape[-1] // (N_SHARD if (shard == "rows" and not x3) else 1)
    n = y.shape[-1] // (N_SHARD if (shard == "cols" and not y3) else 1)
    per_token = 2 * (m * x.dtype.itemsize + n * y.dtype.itemsize)
    tk = t
    while tk > 512 and tk * per_token + 2 * m * n * 4 > TN_VMEM_BUDGET:
        tk //= 2
    nk = t // tk

    out_shape = jax.ShapeDtypeStruct((N_SHARD, m, n), F32)
    if nk == 1:
        def whole(x_ref, y_ref, o_ref):
            o_ref[...] = _dot_tn(x_ref[...].astype(BF16), y_ref[...].astype(BF16))

        x_one = (pl.BlockSpec((None, t, m), lambda s: (s, 0, 0)) if x3 else
                 pl.BlockSpec((t, m), (lambda s: (0, s)) if shard == "rows" else (lambda s: (0, 0))))
        y_one = (pl.BlockSpec((None, t, n), lambda s: (s, 0, 0)) if y3 else
                 pl.BlockSpec((t, n), (lambda s: (0, s)) if shard == "cols" else (lambda s: (0, 0))))
        return _carry(whole, comm, name=name, steps=N_SHARD, out_shape=(out_shape,), in_specs=[x_one, y_one],
                      out_specs=(pl.BlockSpec((None, m, n), lambda s: (s, 0, 0)),), args=(x, y))
    assert comm is None

    def body(x_ref, y_ref, o_ref):
        _accum(o_ref, _dot_tn(x_ref[...].astype(BF16), y_ref[...].astype(BF16)), pl.program_id(1) == 0)

    if x3:
        x_spec = pl.BlockSpec((None, tk, m), lambda s, k: (s, k, 0))
    elif shard == "rows":
        x_spec = pl.BlockSpec((tk, m), lambda s, k: (k, s))
    else:
        x_spec = pl.BlockSpec((tk, m), lambda s, k: (k, 0))
    if y3:
        y_spec = pl.BlockSpec((None, tk, n), lambda s, k: (s, k, 0))
    elif shard == "cols":
        y_spec = pl.BlockSpec((tk, n), lambda s, k: (k, s))
    else:
        y_spec = pl.BlockSpec((tk, n), lambda s, k: (k, 0))
    res = pl.pallas_call(
        body, name=name, grid=(N_SHARD, nk),
        out_shape=out_shape,
        in_specs=[x_spec, y_spec],
        out_specs=pl.BlockSpec((None, m, n), lambda s, k: (s, 0, 0)),
        compiler_params=_cparams(("arbitrary", "arbitrary")),
    )(x, y)
    return (res,), ()


def inproj_fwd(h, gain, w_in, comm=None, tm=256):
    t = h.shape[0]
    widths = [IN_SPLITS[j + 1] - IN_SPLITS[j] for j in range(7)]
    sh_cols = IN_COLS // N_SHARD

    def body(h_ref, g_ref, w_ref, *outs):
        n, _ = _rms_fwd(h_ref[...], g_ref[...])
        nb = n.astype(BF16)
        proj = jnp.concatenate([_dot(nb, w_ref[s]) for s in range(N_SHARD)], axis=1)
        for j, o_ref in enumerate(outs):
            o_ref[...] = proj[:, IN_SPLITS[j]:IN_SPLITS[j + 1]]

    return _carry(
        body, comm, name="inproj_fwd", steps=t // tm,
        out_shape=tuple(jax.ShapeDtypeStruct((t, w), F32) for w in widths),
        in_specs=[pl.BlockSpec((tm, D_MODEL), lambda i: (i, 0)),
                  _const_spec((1, D_MODEL)),
                  _const_spec((N_SHARD, D_MODEL, sh_cols))],
        out_specs=tuple(pl.BlockSpec((tm, w), lambda i: (i, 0)) for w in widths),
        args=(h, gain, w_in),
    )


def inproj_bwd(dres, h, gain, w_in, dparts, comm=None, tm=256):
    t = h.shape[0]
    widths = [IN_SPLITS[j + 1] - IN_SPLITS[j] for j in range(7)]
    sh_cols = IN_COLS // N_SHARD

    def body(dres_ref, h_ref, g_ref, w_ref, d0, d1, d2, d3, d4, d5, d6, dh_ref, dg_ref, nb_ref, dp_ref):
        hv = h_ref[...]
        g = g_ref[...]
        n, r = _rms_fwd(hv, g)
        nb_ref[...] = n.astype(BF16)
        dproj = jnp.concatenate([d[...] for d in (d0, d1, d2, d3, d4, d5, d6)], axis=1).astype(BF16)
        dp_ref[...] = dproj
        dn = jnp.zeros((tm, D_MODEL), F32)
        for s in range(N_SHARD):
            dn = dn + _dot_nt(dproj[:, s * sh_cols:(s + 1) * sh_cols], w_ref[s])
        dx, dg = _rms_bwd(hv, r, g, dn)
        dh_ref[...] = dres_ref[...] + dx
        _accum(dg_ref, dg, pl.program_id(0) == 0)

    tok = pl.BlockSpec((tm, D_MODEL), lambda i: (i, 0))
    return _carry(
        body, comm, name="inproj_bwd", steps=t // tm,
        out_shape=(jax.ShapeDtypeStruct((t, D_MODEL), F32),
                   jax.ShapeDtypeStruct((1, D_MODEL), F32),
                   jax.ShapeDtypeStruct((t, D_MODEL), BF16),
                   jax.ShapeDtypeStruct((t, IN_COLS), BF16)),
        in_specs=[tok, tok, _const_spec((1, D_MODEL)), _const_spec((N_SHARD, D_MODEL, sh_cols))]
                 + [pl.BlockSpec((tm, w), lambda i: (i, 0)) for w in widths],
        out_specs=(tok, pl.BlockSpec((1, D_MODEL), lambda i: (0, 0)), tok,
                   pl.BlockSpec((tm, IN_COLS), lambda i: (i, 0))),
        args=(dres, h, gain, w_in, *dparts),
    )


def s5_prep(lam_re, lam_im, log_dt, b_re, b_im, c_re, c_im):
    dt = jnp.exp(log_dt)[:, None]
    mag = jnp.exp(lam_re * dt)
    lbr = mag * jnp.cos(lam_im * dt)
    lbi = mag * jnp.sin(lam_im * dt)
    den = lam_re * lam_re + lam_im * lam_im
    nr, ni = lbr - 1.0, lbi
    kr = (nr * lam_re + ni * lam_im) / den
    ki = (ni * lam_re - nr * lam_im) / den
    bbr = kr[..., None] * b_re - ki[..., None] * b_im
    bbi = kr[..., None] * b_im + ki[..., None] * b_re
    eye = jnp.eye(16, dtype=F32)

    def bm(bp):
        return jnp.einsum('kgph,gG->kghGp', bp.reshape(S5_KT, 16, S5_STATE, S5_GROUP), eye).reshape(S5_KT, 256, 1024)

    def cm(cp):
        return jnp.einsum('kghp,gG->kgpGh', cp.reshape(S5_KT, 16, S5_GROUP, S5_STATE), eye).reshape(S5_KT, 1024, 256)

    lam_bar = jnp.stack([lbr.reshape(S5_N), lbi.reshape(S5_N)])
    bmat = jnp.stack([bm(bbr), bm(bbi)])
    cmat = jnp.stack([cm(c_re), -cm(c_im)])
    return lam_bar, bmat, cmat


def _lam_powers(lam_bar):
    lr, li = lam_bar[0], lam_bar[1]
    pr, pi = [lr], [li]
    for _ in range(7):
        pr, pi = pr + [pr[-1] * lr - pi[-1] * li], pi + [pr[-1] * li + pi[-1] * lr]
    return jnp.stack(pr), jnp.stack(pi)


SCAN_SHIFTS = ((1, 0), (2, 1), (4, 3))


def _scan_tables(pw_r, pw_i, reverse):
    rows = jnp.arange(8)[:, None]
    planes_r, planes_i = [], []
    for sh, idx in SCAN_SHIFTS:
        keep = (rows < 8 - sh) if reverse else (rows >= sh)
        planes_r.append(jnp.where(keep, pw_r[idx:idx + 1], 0.0))
        planes_i.append(jnp.where(keep, pw_i[idx:idx + 1], 0.0))
    carry = [pw_r[::-1], pw_i[::-1]] if reverse else [pw_r, pw_i]
    return jnp.stack(planes_r + planes_i + carry)


def s5_fwd(u, tab, bmat, cmat, dvec, comm=None, tm=256):
    t = u.shape[0]
    nch = tm // 8

    def body(u_ref, tab_ref, b_ref, c_ref, d_ref, y_ref, xp_ref, x_scr, carry):
        @pl.when(pl.program_id(0) == 0)
        def _():
            carry[...] = jnp.zeros_like(carry)

        uv = u_ref[...]
        ub = uv.astype(BF16)
        for part in range(2):
            for kt in range(S5_KT):
                x_scr[:, pl.ds(part * S5_N + kt * 1024, 1024)] = _dot(ub[:, kt * 256:(kt + 1) * 256], b_ref[part, kt])
        row = lax.broadcasted_iota(jnp.int32, (8, S5_N), 0)

        def chunk(i, c):
            cr, ci = c
            r0 = pl.multiple_of(i * 8, 8)
            xr = x_scr[pl.ds(r0, 8), pl.ds(0, S5_N)]
            xi = x_scr[pl.ds(r0, 8), pl.ds(S5_N, S5_N)]
            for lvl, (sh, _) in enumerate(SCAN_SHIFTS):
                sr = pltpu.roll(xr, sh, 0)
                si = pltpu.roll(xi, sh, 0)
                lr = tab_ref[lvl]
                li = tab_ref[3 + lvl]
                xr, xi = xr + lr * sr - li * si, xi + lr * si + li * sr
            pwr = tab_ref[6]
            pwi = tab_ref[7]
            xr, xi = xr + pwr * cr - pwi * ci, xi + pwr * ci + pwi * cr
            x_scr[pl.ds(r0, 8), pl.ds(0, S5_N)] = xr
            x_scr[pl.ds(r0, 8), pl.ds(S5_N, S5_N)] = xi
            xp_ref[pl.ds(r0, 8), pl.ds(0, S5_N)] = jnp.where(row == 0, cr, pltpu.roll(xr, 1, 0))
            xp_ref[pl.ds(r0, 8), pl.ds(S5_N, S5_N)] = jnp.where(row == 0, ci, pltpu.roll(xi, 1, 0))
            return xr[7:8, :], xi[7:8, :]

        cr, ci = lax.fori_loop(0, nch, chunk, (carry[0:1, :], carry[1:2, :]))
        carry[0:1, :] = cr
        carry[1:2, :] = ci
        for kt in range(S5_KT):
            acc = jnp.zeros((tm, 256), F32)
            for part in range(2):
                acc = acc + _dot(x_scr[:, pl.ds(part * S5_N + kt * 1024, 1024)].astype(BF16), c_ref[part, kt])
            y_ref[:, pl.ds(kt * 256, 256)] = acc + d_ref[:, pl.ds(kt * 256, 256)] * uv[:, kt * 256:(kt + 1) * 256]

    return _carry(
        body, comm, name="s5_fwd", steps=t // tm,
        out_shape=(jax.ShapeDtypeStruct((t, S5_WIDTH), F32),
                   jax.ShapeDtypeStruct((t, 2 * S5_N), F32)),
        in_specs=[pl.BlockSpec((tm, S5_WIDTH), lambda i: (i, 0)),
                  _const_spec((8, 8, S5_N)),
                  _const_spec((2, S5_KT, 256, 1024)), _const_spec((2, S5_KT, 1024, 256)),
                  _const_spec((1, S5_WIDTH))],
        out_specs=(pl.BlockSpec((tm, S5_WIDTH), lambda i: (i, 0)),
                   pl.BlockSpec((tm, 2 * S5_N), lambda i: (i, 0))),
        scratch_shapes=[pltpu.VMEM((tm, 2 * S5_N), F32), pltpu.VMEM((8, S5_N), F32)],
        args=(u, tab, bmat, cmat, dvec),
    )


def s5_bwd(dy, u, xp, tab, bmat, cmat, dvec, comm=None, tm=256):
    t = u.shape[0]
    nt = t // tm
    nch = tm // 8

    def body(dy_ref, u_ref, xp_ref, tab_ref, b_ref, c_ref, d_ref,
             du_ref, db_ref, dc_ref, dl_ref, dd_ref, g_scr, x_scr, carry):
        first = pl.program_id(0) == 0

        @pl.when(first)
        def _():
            carry[...] = jnp.zeros_like(carry)
            dl_ref[...] = jnp.zeros_like(dl_ref)

        dyv = dy_ref[...]
        uv = u_ref[...]
        dyb = dyv.astype(BF16)
        ub = uv.astype(BF16)
        lr1 = tab_ref[6, 7:8, :]
        li1 = tab_ref[7, 7:8, :]
        for kt in range(S5_KT):
            cols = pl.ds(kt * 1024, 1024)
            colsi = pl.ds(S5_N + kt * 1024, 1024)
            g_scr[:, cols] = _dot_nt(dyb[:, kt * 256:(kt + 1) * 256], c_ref[0, kt])
            g_scr[:, colsi] = _dot_nt(dyb[:, kt * 256:(kt + 1) * 256], c_ref[1, kt])
            bur = _dot(ub[:, kt * 256:(kt + 1) * 256], b_ref[0, kt])
            bui = _dot(ub[:, kt * 256:(kt + 1) * 256], b_ref[1, kt])
            xpr = xp_ref[:, cols]
            xpi = xp_ref[:, colsi]
            lrk = lr1[:, kt * 1024:(kt + 1) * 1024]
            lik = li1[:, kt * 1024:(kt + 1) * 1024]
            x_scr[:, cols] = lrk * xpr - lik * xpi + bur
            x_scr[:, colsi] = lrk * xpi + lik * xpr + bui

        def chunk(j, c):
            cr, ci = c
            r0 = pl.multiple_of((nch - 1 - j) * 8, 8)
            gr = g_scr[pl.ds(r0, 8), pl.ds(0, S5_N)]
            gi = g_scr[pl.ds(r0, 8), pl.ds(S5_N, S5_N)]
            for lvl, (sh, _) in enumerate(SCAN_SHIFTS):
                sr = pltpu.roll(gr, 8 - sh, 0)
                si = pltpu.roll(gi, 8 - sh, 0)
                lr = tab_ref[lvl]
                li = tab_ref[3 + lvl]
                gr, gi = gr + lr * sr + li * si, gi + lr * si - li * sr
            pvr = tab_ref[6]
            pvi = tab_ref[7]
            gr, gi = gr + pvr * cr + pvi * ci, gi + pvr * ci - pvi * cr
            g_scr[pl.ds(r0, 8), pl.ds(0, S5_N)] = gr
            g_scr[pl.ds(r0, 8), pl.ds(S5_N, S5_N)] = gi
            xpr = xp_ref[pl.ds(r0, 8), pl.ds(0, S5_N)]
            xpi = xp_ref[pl.ds(r0, 8), pl.ds(S5_N, S5_N)]
            dl_ref[0] += gr * xpr + gi * xpi
            dl_ref[1] += gi * xpr - gr * xpi
            return gr[0:1, :], gi[0:1, :]

        cr, ci = lax.fori_loop(0, nch, chunk, (carry[0:1, :], carry[1:2, :]))
        carry[0:1, :] = cr
        carry[1:2, :] = ci

        for kt in range(S5_KT):
            du = jnp.zeros((tm, 256), F32)
            ukt = ub[:, kt * 256:(kt + 1) * 256]
            dykt = dyb[:, kt * 256:(kt + 1) * 256]
            for part in range(2):
                gb = g_scr[:, pl.ds(part * S5_N + kt * 1024, 1024)].astype(BF16)
                xb = x_scr[:, pl.ds(part * S5_N + kt * 1024, 1024)].astype(BF16)
                du = du + _dot_nt(gb, b_ref[part, kt])
                dbv = _dot_tn(ukt, gb)
                dcv = _dot_tn(xb, dykt)

                @pl.when(first)
                def _():
                    db_ref[part, kt] = dbv
                    dc_ref[part, kt] = dcv

                @pl.when(jnp.logical_not(first))
                def _():
                    db_ref[part, kt] += dbv
                    dc_ref[part, kt] += dcv
            du_ref[:, pl.ds(kt * 256, 256)] = du + d_ref[:, pl.ds(kt * 256, 256)] * dyv[:, kt * 256:(kt + 1) * 256]
        _accum(dd_ref, jnp.sum(dyv * uv, axis=0, keepdims=True), first)

    rev = lambda i: (nt - 1 - i, 0)
    return _carry(
        body, comm, name="s5_bwd", steps=nt,
        out_shape=(jax.ShapeDtypeStruct((t, S5_WIDTH), F32),
                   jax.ShapeDtypeStruct((2, S5_KT, 256, 1024), F32),
                   jax.ShapeDtypeStruct((2, S5_KT, 1024, 256), F32),
                   jax.ShapeDtypeStruct((2, 8, S5_N), F32),
                   jax.ShapeDtypeStruct((1, S5_WIDTH), F32)),
        in_specs=[pl.BlockSpec((tm, S5_WIDTH), rev), pl.BlockSpec((tm, S5_WIDTH), rev),
                  pl.BlockSpec((tm, 2 * S5_N), rev),
                  _const_spec((8, 8, S5_N)),
                  _const_spec((2, S5_KT, 256, 1024)), _const_spec((2, S5_KT, 1024, 256)),
                  _const_spec((1, S5_WIDTH))],
        out_specs=(pl.BlockSpec((tm, S5_WIDTH), rev),
                   pl.BlockSpec((2, S5_KT, 256, 1024), lambda i: (0, 0, 0, 0)),
                   pl.BlockSpec((2, S5_KT, 1024, 256), lambda i: (0, 0, 0, 0)),
                   pl.BlockSpec((2, 8, S5_N), lambda i: (0, 0, 0)),
                   pl.BlockSpec((1, S5_WIDTH), lambda i: (0, 0))),
        scratch_shapes=[pltpu.VMEM((tm, 2 * S5_N), F32), pltpu.VMEM((tm, 2 * S5_N), F32),
                        pltpu.VMEM((8, S5_N), F32)],
        args=(dy, u, xp, tab, bmat, cmat, dvec),
    )


def _hg_gates(z, lb):
    sg = _sigmoid(z)
    sgn = _sigmoid(-z)
    fg = lb + (1.0 - lb) * sg
    return sg, sgn, fg, jnp.log(fg), (1.0 - lb) * sgn


def _hg_decays(g, tril):
    gc = jnp.dot(tril, g, precision=HIGHEST, preferred_element_type=F32)
    mid = gc[CHUNK // 2 - 1:CHUNK // 2, :]
    last = gc[CHUNK - 1:CHUNK, :]
    return jnp.exp(gc), jnp.exp(gc - mid), jnp.exp(mid - gc), jnp.exp(last - gc), jnp.exp(last)


def _split_bf16(x):
    hi = x.astype(BF16)
    return hi, (x - hi.astype(F32)).astype(BF16)


def _hg_scores(qt, qlo, kt, klo, sl, causal):
    a = _dot_nt(qt[:, sl], kt[:, sl]) + _dot_nt(qt[:, sl], klo[:, sl]) + _dot_nt(qlo[:, sl], kt[:, sl])
    return jnp.where(causal, a, 0.0).astype(BF16)


def hgrn_fwd(q, f, v, lb, comm=None):
    t = q.shape[0]
    nc = t // CHUNK
    scale = HG_E ** -0.5

    def body(q_ref, f_ref, v_ref, lb_ref, o_ref, st_ref, state):
        @pl.when(pl.program_id(0) == 0)
        def _():
            state[...] = jnp.zeros_like(state)

        ri = lax.broadcasted_iota(jnp.int32, (CHUNK, CHUNK), 0)
        ci = lax.broadcasted_iota(jnp.int32, (CHUNK, CHUNK), 1)
        causal = ri >= ci
        tril = causal.astype(F32)
        for sub in range(HG_SUB):
            rows = pl.ds(sub * CHUNK, CHUNK)
            _, _, _, g, k = _hg_gates(f_ref[rows, :], lb_ref[...])
            eg, eq, ek, ed, el = _hg_decays(g, tril)
            qs = q_ref[rows, :] * scale
            qg = (qs * eg).astype(BF16)
            qt, qlo = _split_bf16(qs * eq)
            kt, klo = _split_bf16(k * ek)
            kd = (k * ed).astype(BF16)
            vb = v_ref[rows, :].astype(BF16)
            for h in range(HG_HEADS):
                sl = slice(h * HG_E, (h + 1) * HG_E)
                st = state[h]
                a = _hg_scores(qt, qlo, kt, klo, sl, causal)
                o_ref[rows, sl] = _dot(a, vb[:, sl]) + _dot_nt(qg[:, sl], st.astype(BF16))
                st_new = st * el[:, sl] + _dot_tn(vb[:, sl], kd[:, sl])
                state[h] = st_new
                st_ref[sub, h] = st_new

    tok = pl.BlockSpec((HG_SUB * CHUNK, HG_WIDTH), lambda i: (i, 0))
    return _carry(
        body, comm, name="hgrn_fwd", steps=nc // HG_SUB,
        out_shape=(jax.ShapeDtypeStruct((t, HG_WIDTH), F32),
                   jax.ShapeDtypeStruct((nc, HG_HEADS, HG_E, HG_E), F32)),
        in_specs=[tok, tok, tok, _const_spec((1, HG_WIDTH))],
        out_specs=(tok, pl.BlockSpec((HG_SUB, HG_HEADS, HG_E, HG_E), lambda i: (i, 0, 0, 0))),
        scratch_shapes=[pltpu.VMEM((HG_HEADS, HG_E, HG_E), F32)],
        args=(q, f, v, lb),
    )


def hgrn_bwd(do, q, f, v, lb, states, comm=None):
    t = q.shape[0]
    nc = t // CHUNK
    scale = HG_E ** -0.5

    ns = nc // HG_SUB

    def body(do_ref, q_ref, f_ref, v_ref, lb_ref, scur_ref, sprev_ref, dq_ref, df_ref, dv_ref, dlb_ref, dstate):
        first = pl.program_id(0) == 0
        has_prev = jnp.where(pl.program_id(0) < ns - 1, 1.0, 0.0)

        @pl.when(first)
        def _():
            dstate[...] = jnp.zeros_like(dstate)

        ri = lax.broadcasted_iota(jnp.int32, (CHUNK, CHUNK), 0)
        ci = lax.broadcasted_iota(jnp.int32, (CHUNK, CHUNK), 1)
        causal = ri >= ci
        tril = causal.astype(F32)
        triu = (ri <= ci).astype(F32)
        rowc = lax.broadcasted_iota(jnp.int32, (CHUNK, HG_WIDTH), 0)
        lb = lb_ref[...]
        dlb = jnp.zeros((1, HG_WIDTH), F32)
        for sub in reversed(range(HG_SUB)):
            rows = pl.ds(sub * CHUNK, CHUNK)
            sg, sgn, fg, g, k = _hg_gates(f_ref[rows, :], lb)
            eg, eq, ek, ed, el = _hg_decays(g, tril)
            qs = q_ref[rows, :] * scale
            qg = (qs * eg).astype(BF16)
            qt, qlo = _split_bf16(qs * eq)
            kt, klo = _split_bf16(k * ek)
            kd = (k * ed).astype(BF16)
            vb = v_ref[rows, :].astype(BF16)
            dob = do_ref[rows, :].astype(BF16)
            dqs_l, dk_l, dgc_l, dgl_l = [], [], [], []
            for h in range(HG_HEADS):
                sl = slice(h * HG_E, (h + 1) * HG_E)
                s0 = scur_ref[sub - 1, h] if sub > 0 else sprev_ref[HG_SUB - 1, h] * has_prev
                ds1 = dstate[h]
                ds1b = ds1.astype(BF16)
                a = _hg_scores(qt, qlo, kt, klo, sl, causal)
                da = jnp.where(causal, _dot_nt(dob[:, sl], vb[:, sl]), 0.0).astype(BF16)
                dv_ref[rows, sl] = _dot_tn(a, dob[:, sl]) + _dot_nt(kd[:, sl], ds1b)
                dkd = _dot(vb[:, sl], ds1b)
                dqt = _dot(da, kt[:, sl])
                dkt = _dot_tn(da, qt[:, sl])
                dqg = _dot(dob[:, sl], s0.astype(BF16))
                dqs_l.append(dqt * eq[:, sl] + dqg * eg[:, sl])
                dk_l.append(dkt * ek[:, sl] + dkd * ed[:, sl])
                kd_dkd = kd[:, sl].astype(F32) * dkd
                dgc_l.append(qt[:, sl].astype(F32) * dqt - kt[:, sl].astype(F32) * dkt
                             + qg[:, sl].astype(F32) * dqg - kd_dkd)
                dgl_l.append(el[:, sl] * jnp.sum(ds1 * s0, axis=0, keepdims=True)
                             + jnp.sum(kd_dkd, axis=0, keepdims=True))
                dstate[h] = ds1 * el[:, sl] + _dot_tn(dob[:, sl], qg[:, sl])
            dqs = jnp.concatenate(dqs_l, axis=1)
            dk = jnp.concatenate(dk_l, axis=1)
            dgl = jnp.concatenate(dgl_l, axis=1)
            dq_ref[rows, :] = dqs * scale
            dgc = jnp.concatenate(dgc_l, axis=1) + jnp.where(rowc == CHUNK - 1, dgl, 0.0)
            dg = jnp.dot(triu, dgc, precision=HIGHEST, preferred_element_type=F32)
            w = dg / fg - dk
            df_ref[rows, :] = w * (1.0 - lb) * sg * sgn
            dlb = dlb + jnp.sum(w * sgn, axis=0, keepdims=True)
        _accum(dlb_ref, dlb, first)

    rev = lambda i: (ns - 1 - i, 0)
    tok = pl.BlockSpec((HG_SUB * CHUNK, HG_WIDTH), rev)
    st_blk = (HG_SUB, HG_HEADS, HG_E, HG_E)
    return _carry(
        body, comm, name="hgrn_bwd", steps=ns,
        out_shape=(jax.ShapeDtypeStruct((t, HG_WIDTH), F32),
                   jax.ShapeDtypeStruct((t, HG_WIDTH), F32),
                   jax.ShapeDtypeStruct((t, HG_WIDTH), F32),
                   jax.ShapeDtypeStruct((1, HG_WIDTH), F32)),
        in_specs=[tok, tok, tok, tok, _const_spec((1, HG_WIDTH)),
                  pl.BlockSpec(st_blk, lambda i: (ns - 1 - i, 0, 0, 0)),
                  pl.BlockSpec(st_blk, lambda i: (jnp.maximum(ns - 2 - i, 0), 0, 0, 0))],
        out_specs=(tok, tok, tok, pl.BlockSpec((1, HG_WIDTH), lambda i: (0, 0))),
        scratch_shapes=[pltpu.VMEM((HG_HEADS, HG_E, HG_E), F32)],
        args=(do, q, f, v, lb, states, states),
    )


GELU_C = math.sqrt(2.0 / math.pi)


def _gelu(x):
    th = jnp.tanh(GELU_C * (x + 0.044715 * x * x * x))
    return 0.5 * x * (1.0 + th), th


def _merge_core(ys5, o, og, ga, gb, wv_ref, wt_ref, ghg, who_ref):
    ys, th = _gelu(ys5)
    ysb = ys.astype(BF16)
    va = jnp.concatenate([_dot(ysb, wv_ref[s]) for s in range(N_SHARD)], axis=1)
    vt = jnp.concatenate([_dot(ysb, wt_ref[s]) for s in range(N_SHARD)], axis=1)
    svt = _sigmoid(vt)
    ya = va * svt
    rs, ons = [], []
    for h in range(HG_HEADS):
        oh = o[:, h * HG_E:(h + 1) * HG_E]
        r = lax.rsqrt(jnp.mean(oh * oh, axis=-1, keepdims=True) + NORM_EPS)
        rs.append(r)
        ons.append(oh * r)
    on = jnp.concatenate(ons, axis=1)
    sgo = _sigmoid(og)
    o2 = on * ghg * (og * sgo)
    o2b = o2.astype(BF16)
    yb = _dot(o2b, who_ref[...])
    sa = _sigmoid(ga)
    sb = _sigmoid(gb)
    mixed = sa * ya + sb * yb
    return dict(ys=ys, th=th, ysb=ysb, va=va, svt=svt, ya=ya, rs=rs, on=on, sgo=sgo, o2b=o2b, yb=yb,
                sa=sa, sb=sb, mixed=mixed)


def merge_fwd(h, ys5, o, og, ga, gb, wv, wt, ghg, who, wmo, tm=256):
    t = h.shape[0]

    def body(h_ref, ys5_ref, o_ref, og_ref, ga_ref, gb_ref, wv_ref, wt_ref, ghg_ref, who_ref, wmo_ref, out_ref):
        c = _merge_core(ys5_ref[...], o_ref[...], og_ref[...], ga_ref[...], gb_ref[...],
                        wv_ref, wt_ref, ghg_ref[...], who_ref)
        out_ref[...] = h_ref[...] + _dot(c["mixed"].astype(BF16), wmo_ref[...])

    tok = pl.BlockSpec((tm, D_MODEL), lambda i: (i, 0))
    return pl.pallas_call(
        body, name="merge_fwd", grid=(t // tm,),
        out_shape=jax.ShapeDtypeStruct((t, D_MODEL), F32),
        in_specs=[tok, pl.BlockSpec((tm, S5_WIDTH), lambda i: (i, 0)), tok, tok, tok, tok,
                  _const_spec((N_SHARD, S5_WIDTH, 256)), _const_spec((N_SHARD, S5_WIDTH, 256)),
                  _const_spec((1, HG_WIDTH)), _const_spec((HG_WIDTH, D_MODEL)), _const_spec((D_MODEL, D_MODEL))],
        out_specs=tok,
        compiler_params=_cparams(("arbitrary",)),
    )(h, ys5, o, og, ga, gb, wv, wt, ghg, who, wmo)


def merge_bwd(dh, ys5, o, og, ga, gb, wv, wt, ghg, who, wmo, comm=None, tm=256):
    t = dh.shape[0]

    def body(dh_ref, ys5_ref, o_ref, og_ref, ga_ref, gb_ref, wv_ref, wt_ref, ghg_ref, who_ref, wmo_ref,
             dys5_ref, do_ref, dog_ref, dga_ref, dgb_ref, dghg_ref,
             mixb_ref, dhb_ref, ysb_ref, dvab_ref, dvtb_ref, o2b_ref, dybb_ref):
        ys5 = ys5_ref[...]
        o = o_ref[...]
        og = og_ref[...]
        ghg = ghg_ref[...]
        c = _merge_core(ys5, o, og, ga_ref[...], gb_ref[...], wv_ref, wt_ref, ghg, who_ref)
        dhb = dh_ref[...].astype(BF16)
        dhb_ref[...] = dhb
        mixb_ref[...] = c["mixed"].astype(BF16)
        ysb_ref[...] = c["ysb"]
        o2b_ref[...] = c["o2b"]
        dmix = _dot_nt(dhb, wmo_ref[...])
        sa, sb = c["sa"], c["sb"]
        dya = dmix * sa
        dyb = dmix * sb
        dga_ref[...] = dmix * c["ya"] * sa * (1.0 - sa)
        dgb_ref[...] = dmix * c["yb"] * sb * (1.0 - sb)
        svt = c["svt"]
        dva = (dya * svt).astype(BF16)
        dvt = (dya * c["va"] * svt * (1.0 - svt)).astype(BF16)
        dvab_ref[...] = dva
        dvtb_ref[...] = dvt
        dys = jnp.zeros((tm, S5_WIDTH), F32)
        for s in range(N_SHARD):
            dys = dys + _dot_nt(dva[:, s * 256:(s + 1) * 256], wv_ref[s]) + _dot_nt(dvt[:, s * 256:(s + 1) * 256], wt_ref[s])
        th = c["th"]
        dgelu = 0.5 * (1.0 + th) + 0.5 * ys5 * (1.0 - th * th) * GELU_C * (1.0 + 3.0 * 0.044715 * ys5 * ys5)
        dys5_ref[...] = dys * dgelu
        dybb = dyb.astype(BF16)
        dybb_ref[...] = dybb
        do2 = _dot_nt(dybb, who_ref[...])
        sgo = c["sgo"]
        sil = og * sgo
        on = c["on"]
        dog_ref[...] = do2 * on * ghg * (sgo * (1.0 + og * (1.0 - sgo)))
        _accum(dghg_ref, jnp.sum(do2 * on * sil, axis=0, keepdims=True), pl.program_id(0) == 0)
        don = do2 * ghg * sil
        dos = []
        for h in range(HG_HEADS):
            sl = slice(h * HG_E, (h + 1) * HG_E)
            m = jnp.mean(don[:, sl] * on[:, sl], axis=-1, keepdims=True)
            dos.append(c["rs"][h] * (don[:, sl] - on[:, sl] * m))
        do_ref[...] = jnp.concatenate(dos, axis=1)

    tok = pl.BlockSpec((tm, D_MODEL), lambda i: (i, 0))
    s5b = pl.BlockSpec((tm, S5_WIDTH), lambda i: (i, 0))
    f32t = jax.ShapeDtypeStruct((t, D_MODEL), F32)
    bft = jax.ShapeDtypeStruct((t, D_MODEL), BF16)
    return _carry(
        body, comm, name="merge_bwd", steps=t // tm,
        out_shape=(jax.ShapeDtypeStruct((t, S5_WIDTH), F32), f32t, f32t, f32t, f32t,
                   jax.ShapeDtypeStruct((1, HG_WIDTH), F32),
                   bft, bft, jax.ShapeDtypeStruct((t, S5_WIDTH), BF16), bft, bft, bft, bft),
        in_specs=[tok, s5b, tok, tok, tok, tok,
                  _const_spec((N_SHARD, S5_WIDTH, 256)), _const_spec((N_SHARD, S5_WIDTH, 256)),
                  _const_spec((1, HG_WIDTH)), _const_spec((HG_WIDTH, D_MODEL)), _const_spec((D_MODEL, D_MODEL))],
        out_specs=(s5b, tok, tok, tok, tok, pl.BlockSpec((1, HG_WIDTH), lambda i: (0, 0)),
                   tok, tok, s5b, tok, tok, tok, tok),
        args=(dh, ys5, o, og, ga, gb, wv, wt, ghg, who, wmo),
    )


def head_fwd_bwd(h, p, tgt, gple, wpg, wpp, gfin, tm=256):
    t = h.shape[0]

    def body(h_ref, p_ref, tgt_ref, gple_ref, wpg_ref, wpp_ref, gfin_ref,
             loss_ref, dh_ref, dgple_ref, dgfin_ref, nb_ref, dlb_ref, dppb_ref):
        first = pl.program_id(0) == 0
        hv = h_ref[...]
        gple = gple_ref[...]
        gfin = gfin_ref[...]
        n, r3 = _rms_fwd(hv, gple)
        nb = n.astype(BF16)
        nb_ref[...] = nb
        pg = _sigmoid(_dot(nb, wpg_ref[...]))
        pb = p_ref[...].astype(BF16)
        pp = jnp.concatenate([_dot(pb, wpp_ref[s]) for s in range(N_SHARD)], axis=1)
        h4 = hv + pg * pp
        y, r4 = _rms_fwd(h4, gfin)
        err = y - tgt_ref[...]
        lsum = 0.5 * jnp.sum(jnp.sum(err * err, axis=-1, keepdims=True), axis=0, keepdims=True) / D_MODEL
        _accum(loss_ref, jnp.broadcast_to(lsum, (8, 128)), first)
        dy = err * (1.0 / D_MODEL)
        dh4, dgf = _rms_bwd(h4, r4, gfin, dy)
        _accum(dgfin_ref, dgf, first)
        dpp = dh4 * pg
        dppb_ref[...] = dpp.astype(BF16)
        dl = (dh4 * pp * pg * (1.0 - pg)).astype(BF16)
        dlb_ref[...] = dl
        dn = _dot_nt(dl, wpg_ref[...])
        dx, dgp = _rms_bwd(hv, r3, gple, dn)
        _accum(dgple_ref, dgp, first)
        dh_ref[...] = dh4 + dx

    tok = pl.BlockSpec((tm, D_MODEL), lambda i: (i, 0))
    vec = pl.BlockSpec((1, D_MODEL), lambda i: (0, 0))
    bft = jax.ShapeDtypeStruct((t, D_MODEL), BF16)
    return pl.pallas_call(
        body, name="head_fwd_bwd", grid=(t // tm,),
        out_shape=(jax.ShapeDtypeStruct((8, 128), F32), jax.ShapeDtypeStruct((t, D_MODEL), F32),
                   jax.ShapeDtypeStruct((1, D_MODEL), F32), jax.ShapeDtypeStruct((1, D_MODEL), F32),
                   bft, bft, bft),
        in_specs=[tok, pl.BlockSpec((tm, PLE_DIM), lambda i: (i, 0)), tok,
                  _const_spec((1, D_MODEL)), _const_spec((D_MODEL, D_MODEL)),
                  _const_spec((N_SHARD, PLE_DIM, 256)), _const_spec((1, D_MODEL))],
        out_specs=(pl.BlockSpec((8, 128), lambda i: (0, 0)), tok, vec, vec, tok, tok, tok),
        compiler_params=_cparams(("arbitrary",)),
    )(h, p, tgt, gple, wpg, wpp, gfin)


BIG = ("ffn1_w_gate", "ffn1_w_up", "ffn1_w_down", "w_in", "s5_glu_val", "s5_glu_gate", "hg_w_out",
       "w_merge_out", "ffn2_w_gate", "ffn2_w_up", "ffn2_w_down", "ple_w_gate", "ple_w_proj")
FFN_T = ("ffn1_w_gate", "ffn1_w_up", "ffn2_w_gate", "ffn2_w_up")
BIG_SHARD = {
    "ffn1_w_gate": (FF_PAD, D_MODEL), "ffn1_w_up": (FF_PAD, D_MODEL), "ffn1_w_down": (FF_PAD, D_MODEL),
    "ffn2_w_gate": (FF_PAD, D_MODEL), "ffn2_w_up": (FF_PAD, D_MODEL), "ffn2_w_down": (FF_PAD, D_MODEL),
    "w_in": (D_MODEL, IN_COLS // N_SHARD), "s5_glu_val": (S5_WIDTH, 256), "s5_glu_gate": (S5_WIDTH, 256),
    "hg_w_out": (256, D_MODEL), "w_merge_out": (256, D_MODEL), "ple_w_gate": (256, D_MODEL),
    "ple_w_proj": (PLE_DIM, 256),
}


def _lower_bound(hb):
    return jax.nn.softmax(hb, axis=0)[0:1]


class Schedule:
    def __init__(self, wts):
        self.wts = dict(wts)
        self.grads = {}

    def before(self, kernel_name):
        return None

    def after(self, kernel_name, results):
        pass

    def grad(self, name, g):
        self.grads[name] = g


def local_step(x, p, tgt, sched, sm):
    wts = sched.wts
    rows_full = lambda w: w.reshape(N_SHARD * w.shape[1], w.shape[2])

    def carried(kernel_name, fn, *args):
        outs, results = fn(*args, comm=sched.before(kernel_name))
        sched.after(kernel_name, results)
        return outs

    def weight_grad(name, xs, ys, shard):
        kernel_name = "g_" + name
        (g,), results = tn_matmul(xs, ys, kernel_name, shard, comm=sched.before(kernel_name))
        sched.grad(name, g)
        sched.after(kernel_name, results)

    lb, lb_vjp = jax.vjp(_lower_bound, sm["hg_lower_bound"])
    s5_names = ("s5_lam_re", "s5_lam_im", "s5_log_dt", "s5_b_re", "s5_b_im", "s5_c_re", "s5_c_im")
    (lam_bar, bmat, cmat), s5_vjp = jax.vjp(s5_prep, *[sm[k] for k in s5_names])
    pw_r, pw_i = _lam_powers(lam_bar)
    bmat_b = bmat.astype(BF16)
    cmat_b = cmat.astype(BF16)

    h1, a1, b1 = carried("ffn1_fwd", ffn_fwd, x, sm["ffn1_norm"], wts["ffn1_w_gate"], wts["ffn1_w_up"],
                         wts["ffn1_w_down"], "ffn1_fwd")
    s5in, q, f, v, og, ga, gb = carried("inproj_fwd", inproj_fwd, h1, sm["mix_norm"], wts["w_in"])
    ys5, xp = carried("s5_fwd", s5_fwd, s5in, _scan_tables(pw_r, pw_i, False), bmat_b, cmat_b, sm["s5_d"])
    o, states = carried("hgrn_fwd", hgrn_fwd, q, f, v, lb)
    who = rows_full(wts["hg_w_out"])
    wmo = rows_full(wts["w_merge_out"])
    h2 = merge_fwd(h1, ys5, o, og, ga, gb, wts["s5_glu_val"], wts["s5_glu_gate"], sm["hg_out_norm"], who, wmo)
    (h3, a2, b2), _ = ffn_fwd(h2, sm["ffn2_norm"], wts["ffn2_w_gate"], wts["ffn2_w_up"], wts["ffn2_w_down"], "ffn2_fwd")
    loss, dh3, d_ple_norm, d_final_norm, npb, dlgb, dppb = head_fwd_bwd(
        h3, p, tgt, sm["ple_norm"], rows_full(wts["ple_w_gate"]), wts["ple_w_proj"], sm["final_norm"])

    gs = {"ple_norm": d_ple_norm, "final_norm": d_final_norm}
    weight_grad("ple_w_gate", npb, dlgb, "rows")
    weight_grad("ple_w_proj", p, dppb, "cols")

    (dh2, gs["ffn2_norm"], n2b, dhb2, da2, db2, s2), _ = ffn_bwd(
        dh3, h2, a2, b2, sm["ffn2_norm"], wts["ffn2_w_gate"], wts["ffn2_w_up"], wts["ffn2_w_down"], "ffn2_bwd")
    weight_grad("ffn2_w_gate", da2, n2b, "rows")
    weight_grad("ffn2_w_up", db2, n2b, "rows")
    weight_grad("ffn2_w_down", s2, dhb2, "rows")

    dys5, do, dog, dga, dgb, gs["hg_out_norm"], mixb, dh2b, ysb, dvab, dvtb, o2b, dybb = carried(
        "merge_bwd", merge_bwd,
        dh2, ys5, o, og, ga, gb, wts["s5_glu_val"], wts["s5_glu_gate"], sm["hg_out_norm"], who, wmo)
    weight_grad("w_merge_out", mixb, dh2b, "rows")
    weight_grad("s5_glu_val", ysb, dvab, "cols")
    weight_grad("s5_glu_gate", ysb, dvtb, "cols")
    weight_grad("hg_w_out", o2b, dybb, "rows")

    dq, df, dv, dlb = carried("hgrn_bwd", hgrn_bwd, do, q, f, v, lb, states)
    (gs["hg_lower_bound"],) = lb_vjp(dlb)
    du, dbmat, dcmat, dlam8, gs["s5_d"] = carried(
        "s5_bwd", s5_bwd,
        dys5, s5in, xp, _scan_tables(pw_r, pw_i, True), bmat_b, cmat_b, sm["s5_d"])
    for k, g in zip(s5_names, s5_vjp((jnp.sum(dlam8, axis=1), dbmat, dcmat))):
        gs[k] = g

    dh1, gs["mix_norm"], nmb, dprojb = carried(
        "inproj_bwd", inproj_bwd, dh2, h1, sm["mix_norm"], wts["w_in"], (du, dq, df, dv, dog, dga, dgb))
    weight_grad("w_in", nmb, dprojb, "cols")

    dx, gs["ffn1_norm"], n1b, dhb1, da1, db1, s1 = carried(
        "ffn1_bwd", ffn_bwd,
        dh1, x, a1, b1, sm["ffn1_norm"], wts["ffn1_w_gate"], wts["ffn1_w_up"], wts["ffn1_w_down"], "ffn1_bwd")
    weight_grad("ffn1_w_gate", da1, n1b, "rows")
    weight_grad("ffn1_w_up", db1, n1b, "rows")
    weight_grad("ffn1_w_down", s1, dhb1, "rows")
    return loss, dx, gs


MESH = pl.DeviceIdType.MESH
ANY = pl.BlockSpec(memory_space=pl.ANY)


def _place():
    x, y, c = lax.axis_index("x"), lax.axis_index("y"), lax.axis_index("c")
    return x, y, c


def _remote(src, dst, ssem, rsem, dev):
    return pltpu.make_async_remote_copy(src_ref=src, dst_ref=dst, send_sem=ssem, recv_sem=rsem,
                                        device_id=dev, device_id_type=MESH)


class Comm:
    def __init__(self, bufs, outs, alias, sems, hooks):
        self.bufs, self.outs, self.alias, self.sems, self.hooks = list(bufs), list(outs), alias, list(sems), hooks


def run_comm(comm, name):
    nb, no = len(comm.bufs), len(comm.outs)

    def body(*refs):
        for which in ("first", "mid", "last"):
            if which in comm.hooks:
                comm.hooks[which](refs[:nb], refs[nb:nb + no], refs[nb + no:])

    return pl.pallas_call(
        body, name=name, out_shape=tuple(comm.outs), in_specs=[ANY] * nb, out_specs=tuple([ANY] * no),
        input_output_aliases=dict(comm.alias), scratch_shapes=comm.sems,
    )(*comm.bufs)


PLACE_ROWS = {1024: 256, 704: 352, 512: 256, 256: 256}


def place_shards(shards, padded_rows, comm, name):
    n, nb, no = len(shards), len(comm.bufs), len(comm.outs)
    stage_rows = max(PLACE_ROWS.values())
    stage_cols = max(s.shape[1] for s in shards)

    def body(*refs):
        ins, cb = refs[:n], refs[n:n + nb]
        outs, co = refs[n + nb:2 * n + nb], refs[2 * n + nb:2 * n + nb + no]
        stage_f32, stage_bf16, zeros, sem = refs[2 * n + nb + no:2 * n + nb + no + 4]
        cs = refs[2 * n + nb + no + 4:]
        chip = 2 * lax.axis_index("x") + lax.axis_index("y")
        zeros[...] = jnp.zeros_like(zeros)
        comm.hooks["first"](cb, co, cs)
        for w in range(n):
            if w == n // 2:
                comm.hooks["mid"](cb, co, cs)
            r0, cols = ins[w].shape
            step = PLACE_ROWS[r0]
            src32 = stage_f32.at[pl.ds(0, step), pl.ds(0, cols)]
            dst16 = stage_bf16.at[pl.ds(0, step), pl.ds(0, cols)]
            for row in range(0, r0, step):
                pltpu.sync_copy(ins[w].at[pl.ds(row, step), :], src32)
                dst16[...] = src32[...].astype(BF16)
                pltpu.sync_copy(dst16, outs[w].at[chip, pl.ds(row, step), :])
            pad = outs[w].shape[1] - r0
            if pad:
                cp = pltpu.make_async_copy(zeros.at[pl.ds(0, pad), pl.ds(0, cols)],
                                           outs[w].at[chip, pl.ds(r0, pad), :], sem)
                cp.start()
                cp.wait()
        comm.hooks["last"](cb, co, cs)

    res = pl.pallas_call(
        body, name=name,
        out_shape=tuple(jax.ShapeDtypeStruct((N_SHARD, r, s.shape[1]), BF16) for s, r in zip(shards, padded_rows))
        + tuple(comm.outs),
        in_specs=[ANY] * (n + nb), out_specs=tuple([ANY] * (n + no)),
        input_output_aliases={n + i: n + o for i, o in comm.alias.items()},
        scratch_shapes=[pltpu.VMEM((stage_rows, stage_cols), F32), pltpu.VMEM((stage_rows, stage_cols), BF16),
                        pltpu.VMEM((FF_PAD - FF_SHARD, D_MODEL), BF16), pltpu.SemaphoreType.DMA] + comm.sems,
        compiler_params=pltpu.CompilerParams(vmem_limit_bytes=VMEM_LIMIT),
    )(*shards, *comm.bufs)
    return res[:n], res[n:]


def _carry(body, comm, *, name, steps, out_shape, in_specs, out_specs, args, scratch_shapes=()):
    out_shape, out_specs, scratch_shapes = tuple(out_shape), tuple(out_specs), list(scratch_shapes)
    if comm is None:
        res = pl.pallas_call(body, name=name, grid=(steps,), out_shape=out_shape, in_specs=list(in_specs),
                             out_specs=out_specs, scratch_shapes=scratch_shapes,
                             compiler_params=_cparams(("arbitrary",)))(*args)
        return tuple(res), ()
    n_in, n_out, n_scr = len(args), len(out_shape), len(scratch_shapes)
    nb, no = len(comm.bufs), len(comm.outs)

    def wrapped(*refs):
        ins, cb = refs[:n_in], refs[n_in:n_in + nb]
        o0 = n_in + nb
        outs, co = refs[o0:o0 + n_out], refs[o0 + n_out:o0 + n_out + no]
        s0 = o0 + n_out + no
        scr, cs = refs[s0:s0 + n_scr], refs[s0 + n_scr:]
        step = pl.program_id(0)

        def hook(which, at):
            if which in comm.hooks:
                pl.when(step == at)(lambda: comm.hooks[which](cb, co, cs))

        hook("first", 0)
        hook("mid", steps // 2)
        body(*ins, *outs, *scr)
        hook("last", steps - 1)

    res = pl.pallas_call(
        wrapped, name=name, grid=(steps,), out_shape=out_shape + tuple(comm.outs),
        in_specs=list(in_specs) + [ANY] * nb, out_specs=out_specs + (ANY,) * no,
        scratch_shapes=scratch_shapes + comm.sems,
        input_output_aliases={n_in + i: n_out + o for i, o in comm.alias.items()},
        compiler_params=_cparams(("arbitrary",)),
    )(*args, *comm.bufs)
    return tuple(res[:n_out]), tuple(res[n_out:])


def gather_comm(bufs):
    n = len(bufs)

    def copies(outs, sems):
        s_own, r_own, s_fwd, r_fwd, s_sib, r_sib = sems
        x, y, c = _place()
        me = 2 * x + y
        nbr = ((1 - x, y), (x, 1 - y))
        nbr_id = (2 * (1 - x) + y, 2 * x + (1 - y))
        diag_id = 2 * (1 - x) + (1 - y)
        sib = (x, y, 1 - c)

        def rows(w, q=None):
            r = outs[w].shape[1]
            if q is None:
                return pl.ds(pl.multiple_of(c * (r // 2), 16), r // 2)
            return pl.ds(pl.multiple_of(c * (r // 2) + q * (r // 4), 16), r // 4)

        def own(w, j):
            piece = outs[w].at[me, rows(w)]
            return _remote(piece, piece, s_own.at[w, j], r_own.at[w, j], (nbr[j][0], nbr[j][1], c))

        def from_nbr(w, j):
            piece = outs[w].at[nbr_id[j], rows(w)]
            return _remote(piece, piece, s_own.at[w, j], r_own.at[w, j], (nbr[j][0], nbr[j][1], c))

        def fwd(w, j):
            piece = outs[w].at[nbr_id[j], rows(w, j)]
            return _remote(piece, piece, s_fwd.at[w, j], r_fwd.at[w, j], (nbr[1 - j][0], nbr[1 - j][1], c))

        def from_diag(w, j):
            piece = outs[w].at[diag_id, rows(w, j)]
            return _remote(piece, piece, s_fwd.at[w, j], r_fwd.at[w, j], (nbr[1 - j][0], nbr[1 - j][1], c))

        def to_sib(w, k):
            piece = (outs[w].at[nbr_id[k], rows(w)] if k < 2 else outs[w].at[diag_id, rows(w, k - 2)])
            return _remote(piece, piece, s_sib.at[w, k], r_sib.at[w, k], sib)

        def from_sib(w, k):
            r = outs[w].shape[1]
            if k < 2:
                piece = outs[w].at[nbr_id[k], pl.ds(pl.multiple_of((1 - c) * (r // 2), 16), r // 2)]
            else:
                piece = outs[w].at[diag_id, pl.ds(pl.multiple_of((1 - c) * (r // 2) + (k - 2) * (r // 4), 16), r // 4)]
            return _remote(piece, piece, s_sib.at[w, k], r_sib.at[w, k], sib)

        return own, from_nbr, fwd, from_diag, to_sib, from_sib

    def first(_, outs, sems):
        own = copies(outs, sems)[0]
        for w in range(n):
            own(w, 0).start()
            own(w, 1).start()

    def mid(_, outs, sems):
        _, from_nbr, fwd, _, to_sib, _ = copies(outs, sems)
        for w in range(n):
            for j in range(2):
                from_nbr(w, j).wait_recv()
                fwd(w, j).start()
                to_sib(w, j).start()

    def last(_, outs, sems):
        own, _, fwd, from_diag, to_sib, from_sib = copies(outs, sems)
        for w in range(n):
            for j in range(2):
                from_diag(w, j).wait_recv()
                to_sib(w, 2 + j).start()
        for w in range(n):
            for k in range(4):
                from_sib(w, k).wait_recv()
        for w in range(n):
            for j in range(2):
                own(w, j).wait_send()
                fwd(w, j).wait_send()
            for k in range(4):
                to_sib(w, k).wait_send()

    dma = pltpu.SemaphoreType.DMA
    return Comm(bufs, [jax.ShapeDtypeStruct(b.shape, b.dtype) for b in bufs], {w: w for w in range(n)},
                [dma((n, 2)), dma((n, 2)), dma((n, 2)), dma((n, 2)), dma((n, 4)), dma((n, 4))],
                {"first": first, "mid": mid, "last": last})


def _start_wait(make):
    def first(bufs, outs, sems):
        for cp in make(bufs, outs, sems):
            cp.start()

    def last(bufs, outs, sems):
        for cp in make(bufs, outs, sems):
            cp.wait()

    return {"first": first, "last": last}


def exchange_comm(grads):
    n = len(grads)

    def make(ins, outs, sems):
        x, y, c = _place()
        cps = []
        for w in range(n):
            half = ins[w].shape[1] // 2
            src = ins[w].at[:, pl.ds(pl.multiple_of((1 - c) * half, 8), half), :]
            cps.append(_remote(src, outs[w], sems[0].at[w], sems[1].at[w], (x, y, 1 - c)))
        return cps

    dma = pltpu.SemaphoreType.DMA
    return Comm(grads, [jax.ShapeDtypeStruct((N_SHARD, g.shape[1] // 2, g.shape[2]), g.dtype) for g in grads],
                {}, [dma((n,)), dma((n,))], _start_wait(make))


def scatter_comm(sums):
    n = len(sums)

    def make(ins, outs, sems):
        x, y, c = _place()
        chips = ((1 - x, y), (x, 1 - y), (1 - x, 1 - y))
        return [_remote(ins[w].at[2 * ch[0] + ch[1]], outs[w].at[j], sems[0].at[w, j], sems[1].at[w, j],
                        (ch[0], ch[1], c))
                for w in range(n) for j, ch in enumerate(chips)]

    dma = pltpu.SemaphoreType.DMA
    return Comm(sums, [jax.ShapeDtypeStruct((3,) + s.shape[1:], s.dtype) for s in sums],
                {}, [dma((n, 3)), dma((n, 3))], _start_wait(make))


def join_comm(shards):
    n = len(shards)

    def make(_, outs, sems):
        x, y, c = _place()
        cps = []
        for w in range(n):
            half = outs[w].shape[0] // 2
            mine = outs[w].at[pl.ds(pl.multiple_of(c * half, 8), half), :]
            cps.append(_remote(mine, mine, sems[0].at[w], sems[1].at[w], (x, y, 1 - c)))
        return cps

    dma = pltpu.SemaphoreType.DMA
    return Comm(shards, [jax.ShapeDtypeStruct(s.shape, s.dtype) for s in shards], {w: w for w in range(n)},
                [dma((n,)), dma((n,))], _start_wait(make))


def allreduce_small(vec, comm):
    half = vec.shape[0] // 2
    nb, no = len(comm.bufs), len(comm.outs)

    def body(*refs):
        v_ref, cb, o_ref, co = refs[0], refs[1:1 + nb], refs[1 + nb], refs[2 + nb:2 + nb + no]
        pair, chips_buf, s1, r1, s2, r2, s3, r3 = refs[2 + nb + no:10 + nb + no]
        cs = refs[10 + nb + no:]
        comm.hooks["first"](cb, co, cs)
        x, y, c = _place()
        chip = 2 * x + y
        sib = (x, y, 1 - c)
        mine = pl.ds(pl.multiple_of(c * half, 8), half)
        other = pl.ds(pl.multiple_of((1 - c) * half, 8), half)
        to_sib = _remote(v_ref.at[other], pair, s1, r1, sib)
        to_sib.start()
        to_sib.wait()
        chips_buf[chip] = v_ref[mine, :] + pair[...]
        sends = [_remote(chips_buf.at[chip], chips_buf.at[chip], s2.at[j], r2.at[j], (ch[0], ch[1], c))
                 for j, ch in enumerate(((1 - x, y), (x, 1 - y), (1 - x, 1 - y)))]
        for cp in sends:
            cp.start()
        for cp in sends:
            cp.wait()
        o_ref[mine, :] = (chips_buf[0] + chips_buf[1]) + (chips_buf[2] + chips_buf[3])
        back = _remote(o_ref.at[mine], o_ref.at[mine], s3, r3, sib)
        back.start()
        back.wait()
        comm.hooks["last"](cb, co, cs)

    dma = pltpu.SemaphoreType.DMA
    vmem = pl.BlockSpec(memory_space=pltpu.VMEM)
    res = pl.pallas_call(
        body, name="allreduce_small",
        out_shape=(jax.ShapeDtypeStruct(vec.shape, F32),) + tuple(comm.outs),
        in_specs=[vmem] + [ANY] * nb,
        out_specs=(vmem,) + (ANY,) * no,
        input_output_aliases={1 + i: 1 + o for i, o in comm.alias.items()},
        scratch_shapes=[pltpu.VMEM((half, 128), F32), pltpu.VMEM((N_SHARD, half, 128), F32),
                        dma, dma, dma((3,)), dma((3,)), dma, dma] + comm.sems,
        compiler_params=pltpu.CompilerParams(vmem_limit_bytes=VMEM_LIMIT),
    )(vec, *comm.bufs)
    return res[0], res[1:]


REDUCE_ROW_BLOCKS = 2


def add_own_half(place, g, recv, name):
    _, r, cc = g.shape
    half = r // 2
    depth = 3

    def body(p_ref, g_ref, r_ref, o_ref, ob_ref, gbuf, rbuf, obuf, isem, osem, fsem):
        chip, core = p_ref[0], p_ref[1]
        rows = pl.ds(pl.multiple_of(core * half, 8), half)

        def loads(s):
            slot = s % depth
            return (pltpu.make_async_copy(g_ref.at[s, rows, :], gbuf.at[slot], isem.at[slot, 0]),
                    pltpu.make_async_copy(r_ref.at[s], rbuf.at[slot], isem.at[slot, 1]))

        def store(s):
            return pltpu.make_async_copy(obuf.at[s % depth], ob_ref.at[s], osem.at[s % depth])

        for s in range(depth):
            for cp in loads(s):
                cp.start()
        for s in range(N_SHARD):
            slot = s % depth
            for cp in loads(s):
                cp.wait()
            v = gbuf[slot] + rbuf[slot]
            if s >= depth:
                store(s - depth).wait()
            obuf[slot] = v.astype(BF16)
            store(s).start()

            @pl.when(chip == s)
            def _():
                gbuf[slot] = v
                own = pltpu.make_async_copy(gbuf.at[slot], o_ref, fsem)
                own.start()
                own.wait()

            if s + depth < N_SHARD:
                for cp in loads(s + depth):
                    cp.start()
        for s in range(max(N_SHARD - depth, 0), N_SHARD):
            store(s).wait()

    dma = pltpu.SemaphoreType.DMA
    return pl.pallas_call(
        body, name=name,
        out_shape=(jax.ShapeDtypeStruct((half, cc), F32),
                   jax.ShapeDtypeStruct((N_SHARD, half, cc), BF16)),
        in_specs=[pl.BlockSpec(memory_space=pltpu.SMEM), ANY, ANY],
        out_specs=(ANY, ANY),
        scratch_shapes=[pltpu.VMEM((depth, half, cc), F32), pltpu.VMEM((depth, half, cc), F32),
                        pltpu.VMEM((depth, half, cc), BF16), dma((depth, 2)), dma((depth,)), dma],
        compiler_params=pltpu.CompilerParams(vmem_limit_bytes=VMEM_LIMIT),
    )(place, g, recv)


def add_chip_sums(place, own, recv, name):
    half, cc = own.shape
    nb = REDUCE_ROW_BLOCKS
    tile = half // nb

    def body(s_ref, o_ref, r_ref, out_ref):
        del s_ref
        acc = o_ref[...] + r_ref[0].astype(F32)
        acc = acc + r_ref[1].astype(F32)
        out_ref[...] = acc + r_ref[2].astype(F32)

    return pl.pallas_call(
        body, name=name,
        grid_spec=pltpu.PrefetchScalarGridSpec(
            num_scalar_prefetch=1, grid=(nb,),
            in_specs=[pl.BlockSpec((tile, cc), lambda i, s_ref: (i, 0)),
                      pl.BlockSpec((3, tile, cc), lambda i, s_ref: (0, i, 0))],
            out_specs=pl.BlockSpec((tile, cc), lambda i, s_ref: (s_ref[1] * nb + i, 0))),
        out_shape=jax.ShapeDtypeStruct((2 * half, cc), F32),
        compiler_params=_cparams(("arbitrary",)),
    )(place, own, recv)


def adamw(w, m, v, g, name, copy_g=False):
    r, cc = w.shape
    tr = next(t for t in (256, 352, r) if r % t == 0)
    bc1 = 1.0 / (1.0 - ADAM_B1 ** ADAM_STEP)
    bc2 = 1.0 / (1.0 - ADAM_B2 ** ADAM_STEP)

    def body(w_ref, m_ref, v_ref, g_ref, d_ref, mo_ref, vo_ref, *go_ref):
        gv = g_ref[...]
        mn = ADAM_B1 * m_ref[...] + (1.0 - ADAM_B1) * gv
        vn = ADAM_B2 * v_ref[...] + (1.0 - ADAM_B2) * (gv * gv)
        mo_ref[...] = mn
        vo_ref[...] = vn
        d_ref[...] = -ADAM_LR * ((mn * bc1) / (jnp.sqrt(vn * bc2) + ADAM_EPS) + ADAM_WD * w_ref[...])
        if copy_g:
            go_ref[0][...] = gv

    blk = pl.BlockSpec((tr, cc), lambda i: (i, 0))
    shp = jax.ShapeDtypeStruct((r, cc), F32)
    nout = 4 if copy_g else 3
    return pl.pallas_call(
        body, name=name, grid=(r // tr,),
        out_shape=(shp,) * nout, in_specs=[blk] * 4, out_specs=(blk,) * nout,
        compiler_params=_cparams(("arbitrary",)),
    )(w, m, v, g)


GATHER_FIRST = ("ffn1_w_gate", "ffn1_w_up", "ffn1_w_down")
GATHER_ON = {"ffn1_fwd": ("w_in",),
             "inproj_fwd": ("s5_glu_val", "s5_glu_gate", "hg_w_out", "w_merge_out"),
             "s5_fwd": ("ffn2_w_gate", "ffn2_w_up"),
             "hgrn_fwd": ("ffn2_w_down", "ple_w_gate", "ple_w_proj")}
REDUCE = ((("ple_w_gate", "ple_w_proj", "ffn2_w_gate", "ffn2_w_up", "ffn2_w_down"), "merge_bwd", "hgrn_bwd"),
          (("w_merge_out", "s5_glu_val", "s5_glu_gate", "hg_w_out"), "s5_bwd", "inproj_bwd"),
          (("w_in",), None, "ffn1_bwd"),
          (("ffn1_w_gate",), "g_ffn1_w_up", "g_ffn1_w_down"),
          (("ffn1_w_up",), "g_ffn1_w_down", None),
          (("ffn1_w_down",), None, None))


def merge_comms(comms):
    if len(comms) == 1:
        return comms[0], [len(comms[0].outs)]
    bufs, outs, sems, alias, spans = [], [], [], {}, []
    for c in comms:
        spans.append((len(bufs), len(bufs) + len(c.bufs), len(outs), len(outs) + len(c.outs),
                      len(sems), len(sems) + len(c.sems)))
        alias.update({len(bufs) + i: len(outs) + o for i, o in c.alias.items()})
        bufs, outs, sems = bufs + c.bufs, outs + c.outs, sems + c.sems

    def hook(which):
        def run(b, o, s):
            for c, (b0, b1, o0, o1, s0, s1) in zip(comms, spans):
                if which in c.hooks:
                    c.hooks[which](b[b0:b1], o[o0:o1], s[s0:s1])
        return run

    hooks = {w: hook(w) for w in ("first", "mid", "last") if any(w in c.hooks for c in comms)}
    return Comm(bufs, outs, alias, sems, hooks), [len(c.outs) for c in comms]


class DistSchedule(Schedule):
    def __init__(self, w_rows, chip, core):
        first = gather_comm([_gather_buffer(k, w_rows[k], chip) for k in GATHER_FIRST])
        later = [k for k in BIG if k not in GATHER_FIRST]
        placed, gathered = place_shards([w_rows[k] for k in later], [BIG_SHARD[k][0] for k in later], first,
                                        "place_shards_gather_ffn1")
        super().__init__(zip(GATHER_FIRST, gathered))
        self.bufs = dict(zip(later, placed))
        self.place = jnp.stack([chip, core])
        self.sums, self.halves = {}, {}

    def _exchange(self, names):
        return exchange_comm([self.grads[k] for k in names])

    def _scatter(self, names):
        return scatter_comm([self.sums[k][1] for k in names])

    def _pair_sums(self, names, recv):
        for k, r in zip(names, recv):
            self.sums[k] = add_own_half(self.place, self.grads[k], r, "pair_sum_" + k)

    def _chip_sums(self, names, recv):
        for k, r in zip(names, recv):
            self.halves[k] = add_chip_sums(self.place, self.sums[k][0], r, "chip_sum_" + k)

    def before(self, kernel_name):
        comms, takers = [], []
        if kernel_name in GATHER_ON:
            names = GATHER_ON[kernel_name]
            comms.append(gather_comm([self.bufs[k] for k in names]))
            takers.append(lambda res, names=names: self.wts.update(zip(names, res)))
        for names, exchange_on, scatter_on in REDUCE:
            if kernel_name == exchange_on:
                comms.append(self._exchange(names))
                takers.append(lambda res, names=names: self._pair_sums(names, res))
            if kernel_name == scatter_on:
                if exchange_on is None:
                    self._pair_sums(names, run_comm(self._exchange(names), "exchange_" + names[0]))
                comms.append(self._scatter(names))
                takers.append(lambda res, names=names: self._chip_sums(names, res))
        if not comms:
            return None
        merged, counts = merge_comms(comms)
        self.pending = (takers, counts)
        return merged

    def after(self, kernel_name, results):
        if not results:
            return
        takers, counts = self.pending
        start = 0
        for take, count in zip(takers, counts):
            take(results[start:start + count])
            start += count

    def finish(self, small):
        tail = [names for names, _, scatter_on in REDUCE if scatter_on is None]
        alone = [k for names, exchange_on, scatter_on in REDUCE if scatter_on is None and exchange_on is None
                 for k in names]
        self._pair_sums(alone, run_comm(self._exchange(alone), "exchange_tail"))
        tail = [k for names in tail for k in names]
        early = [k for k in BIG if k not in tail]
        both, counts = merge_comms([self._scatter(tail), join_comm([self.halves[k] for k in early])])
        total, res = allreduce_small(small, both)
        self._chip_sums(tail, res[:counts[0]])
        full = dict(zip(early, res[counts[0]:]))
        full.update(zip(tail, run_comm(join_comm([self.halves[k] for k in tail]), "join_tail")))
        return full, total


SMALL = ("ffn1_norm", "mix_norm", "s5_lam_re", "s5_lam_im", "s5_log_dt", "s5_b_re", "s5_b_im", "s5_c_re",
         "s5_c_im", "s5_d", "hg_lower_bound", "hg_out_norm", "ffn2_norm", "ple_norm", "final_norm")
WEIGHTS = ("ffn1_norm", "ffn1_w_gate", "ffn1_w_up", "ffn1_w_down", "mix_norm", "w_in", "s5_lam_re", "s5_lam_im",
           "s5_log_dt", "s5_b_re", "s5_b_im", "s5_c_re", "s5_c_im", "s5_d", "s5_glu_val", "s5_glu_gate",
           "hg_lower_bound", "hg_out_norm", "hg_w_out", "w_merge_out", "ffn2_norm", "ffn2_w_gate", "ffn2_w_up",
           "ffn2_w_down", "ple_norm", "ple_w_gate", "ple_w_proj", "final_norm")


def _as_rows(name, w):
    return jnp.swapaxes(w[0], 0, 1) if name in FFN_T else w[0]


def _from_rows(name, w):
    return (jnp.swapaxes(w, 0, 1) if name in FFN_T else w)[None]


def _gather_buffer(name, w_rows, chip):
    r, c = BIG_SHARD[name]
    shard = jnp.pad(w_rows.astype(BF16), ((0, r - w_rows.shape[0]), (0, 0)))
    return lax.dynamic_update_slice(jnp.zeros((N_SHARD, r, c), BF16), shard[None], (chip, 0, 0))


def _pack(parts):
    flat = jnp.concatenate([jnp.zeros((128,), F32)] + [a.reshape(-1) for a in parts])
    rows = -(-flat.shape[0] // 2048) * 16
    return jnp.pad(flat, (0, rows * 128 - flat.shape[0])).reshape(rows, 128)


def _unpack(vec, likes):
    flat = vec.reshape(-1)
    out, off = [], 128
    for a in likes:
        out.append(flat[off:off + a.size].reshape(a.shape))
        off += a.size
    return out


def _small_view(name, w):
    if name.startswith("s5_") and name != "s5_d":
        return w[0]
    if name == "final_norm":
        return w.reshape(1, D_MODEL)
    return w


def kernel(x, p, ffn1_norm, ffn1_w_gate, ffn1_w_up, ffn1_w_down, mix_norm, w_in, s5_lam_re, s5_lam_im, s5_log_dt, s5_b_re, s5_b_im, s5_c_re, s5_c_im, s5_d, s5_glu_val, s5_glu_gate, hg_lower_bound, hg_out_norm, hg_w_out, w_merge_out, ffn2_norm, ffn2_w_gate, ffn2_w_up, ffn2_w_down, ple_norm, ple_w_gate, ple_w_proj, final_norm, loss_target, m_ffn1_norm, m_ffn1_w_gate, m_ffn1_w_up, m_ffn1_w_down, m_mix_norm, m_w_in, m_s5_lam_re, m_s5_lam_im, m_s5_log_dt, m_s5_b_re, m_s5_b_im, m_s5_c_re, m_s5_c_im, m_s5_d, m_s5_glu_val, m_s5_glu_gate, m_hg_lower_bound, m_hg_out_norm, m_hg_w_out, m_w_merge_out, m_ffn2_norm, m_ffn2_w_gate, m_ffn2_w_up, m_ffn2_w_down, m_ple_norm, m_ple_w_gate, m_ple_w_proj, m_final_norm, v_ffn1_norm, v_ffn1_w_gate, v_ffn1_w_up, v_ffn1_w_down, v_mix_norm, v_w_in, v_s5_lam_re, v_s5_lam_im, v_s5_log_dt, v_s5_b_re, v_s5_b_im, v_s5_c_re, v_s5_c_im, v_s5_d, v_s5_glu_val, v_s5_glu_gate, v_hg_lower_bound, v_hg_out_norm, v_hg_w_out, v_w_merge_out, v_ffn2_norm, v_ffn2_w_gate, v_ffn2_w_up, v_ffn2_w_down, v_ple_norm, v_ple_w_gate, v_ple_w_proj, v_final_norm):
    given = dict(locals())
    wv = {k: given[k] for k in WEIGHTS}
    mv = {k: given["m_" + k] for k in WEIGHTS}
    vv = {k: given["v_" + k] for k in WEIGHTS}

    core = lax.axis_index("c").astype(jnp.int32)
    chip = (2 * lax.axis_index("x") + lax.axis_index("y")).astype(jnp.int32)
    w_rows = {k: _as_rows(k, wv[k]) for k in BIG}
    sched = DistSchedule(w_rows, chip, core)
    sm = {k: _small_view(k, wv[k]) for k in SMALL}

    loss_blk, dx, gsm = local_step(x[0], p[0, 0], loss_target[0], sched, sm)

    small_likes = [wv[k] for k in SMALL]
    packed = _pack([gsm[k] for k in SMALL])
    packed = packed.at[0, 0].set(loss_blk[0, 0])
    full, total = sched.finish(packed)
    loss = total[0, 0]
    gsmall = dict(zip(SMALL, _unpack(total, small_likes)))

    grads, deltas, new_m, new_v = {}, {}, {}, {}
    for k in BIG:
        padded = full[k].shape != w_rows[k].shape
        res = adamw(w_rows[k], _as_rows(k, mv[k]), _as_rows(k, vv[k]), full[k], "adamw_" + k, copy_g=padded)
        grads[k] = _from_rows(k, res[3] if padded else full[k])
        deltas[k], new_m[k], new_v[k] = (_from_rows(k, a) for a in res[:3])
    sw = _pack([wv[k] for k in SMALL])
    smm = _pack([mv[k] for k in SMALL])
    svv = _pack([vv[k] for k in SMALL])
    sd, smn, svn = adamw(sw, smm, svv, total, "adamw_small")
    for k, d, mn, vn in zip(SMALL, _unpack(sd, small_likes), _unpack(smn, small_likes), _unpack(svn, small_likes)):
        grads[k], deltas[k], new_m[k], new_v[k] = gsmall[k], d, mn, vn

    return (loss, dx[None], *[grads[k] for k in WEIGHTS], *[deltas[k] for k in WEIGHTS],
            *[new_m[k] for k in WEIGHTS], *[new_v[k] for k in WEIGHTS])
```

```python
import math

import jax
import jax.numpy as jnp
from jax import lax
from jax.experimental import pallas as pl
from jax.experimental.pallas import tpu as pltpu

F32 = jnp.float32
BF16 = jnp.bfloat16

D_MODEL = 1024
D_FF = 2816
N_SHARD = 4
FF_SHARD = D_FF // N_SHARD
FF_PAD = 768
NORM_EPS = 1e-6
PLE_DIM = 256

S5_WIDTH = 512
S5_GROUPS = 32
S5_GROUP = 16
S5_STATE = 64
S5_N = S5_GROUPS * S5_STATE
S5_KT = 2

HG_HEADS = 8
HG_E = 128
HG_WIDTH = 1024
CHUNK = 64
HG_SUB = 4
IN_COLS = S5_WIDTH + 4 * HG_WIDTH + 2 * D_MODEL
IN_SPLITS = (0, 512, 1536, 2560, 3584, 4608, 5632, 6656)

ADAM_LR = 0.001
ADAM_B1 = 0.9
ADAM_B2 = 0.999
ADAM_EPS = 1e-08
ADAM_WD = 0.01
ADAM_STEP = 10

VMEM_LIMIT = 60 * 1024 * 1024
HIGHEST = lax.Precision.HIGHEST


def _cparams(sem=None, **kw):
    return pltpu.CompilerParams(dimension_semantics=sem, vmem_limit_bytes=VMEM_LIMIT, **kw)


def _const_spec(shape):
    nd = len(shape)
    return pl.BlockSpec(shape, lambda *_: (0,) * nd, pipeline_mode=pl.Buffered(1))


def _dot(a, b):
    return jnp.dot(a, b, preferred_element_type=F32)


def _dot_nt(a, b):
    return lax.dot_general(a, b, (((1,), (1,)), ((), ())), preferred_element_type=F32)


def _dot_tn(a, b):
    return lax.dot_general(a, b, (((0,), (0,)), ((), ())), preferred_element_type=F32)


def _sigmoid(x):
    return 1.0 / (1.0 + jnp.exp(-x))


def _rms_fwd(x, g):
    r = lax.rsqrt(jnp.mean(x * x, axis=-1, keepdims=True) + NORM_EPS)
    return x * r * g, r


def _rms_bwd(x, r, g, dy):
    xh = x * r
    dyg = dy * g
    m = jnp.mean(dyg * xh, axis=-1, keepdims=True)
    return r * (dyg - xh * m), jnp.sum(dy * xh, axis=0, keepdims=True)


def _accum(ref, val, first):
    @pl.when(first)
    def _():
        ref[...] = val

    @pl.when(jnp.logical_not(first))
    def _():
        ref[...] += val


def ffn_fwd(h, gain, wg, wu, wd, name, comm=None, tm=512):
    t = h.shape[0]

    def body(h_ref, g_ref, wg_ref, wu_ref, wd_ref, o_ref, a_ref, b_ref):
        hv = h_ref[...]
        n, _ = _rms_fwd(hv, g_ref[...])
        nb = n.astype(BF16)
        acc = jnp.zeros((tm, D_MODEL), F32)
        for s in range(N_SHARD):
            a = _dot_nt(nb, wg_ref[s])
            b = _dot_nt(nb, wu_ref[s])
            a_ref[s] = a.astype(BF16)
            b_ref[s] = b.astype(BF16)
            sv = (a * _sigmoid(a) * b).astype(BF16)
            acc = acc + _dot(sv, wd_ref[s])
        o_ref[...] = hv + 0.5 * acc

    return _carry(
        body, comm, name=name, steps=t // tm,
        out_shape=(jax.ShapeDtypeStruct((t, D_MODEL), F32),
                   jax.ShapeDtypeStruct((N_SHARD, t, FF_PAD), BF16),
                   jax.ShapeDtypeStruct((N_SHARD, t, FF_PAD), BF16)),
        in_specs=[pl.BlockSpec((tm, D_MODEL), lambda i: (i, 0)),
                  _const_spec((1, D_MODEL)),
                  _const_spec((N_SHARD, FF_PAD, D_MODEL)),
                  _const_spec((N_SHARD, FF_PAD, D_MODEL)),
                  _const_spec((N_SHARD, FF_PAD, D_MODEL))],
        out_specs=(pl.BlockSpec((tm, D_MODEL), lambda i: (i, 0)),
                   pl.BlockSpec((N_SHARD, tm, FF_PAD), lambda i: (0, i, 0)),
                   pl.BlockSpec((N_SHARD, tm, FF_PAD), lambda i: (0, i, 0))),
        args=(h, gain, wg, wu, wd),
    )


def ffn_bwd(dho, h, a, b, gain, wg, wu, wd, name, comm=None, tm=256):
    t = h.shape[0]

    def body(dho_ref, h_ref, a_ref, b_ref, g_ref, wg_ref, wu_ref, wd_ref,
             dh_ref, dg_ref, nb_ref, dhb_ref, da_ref, db_ref, s_ref):
        hv = h_ref[...]
        g = g_ref[...]
        n, r = _rms_fwd(hv, g)
        nb_ref[...] = n.astype(BF16)
        dhalf = (0.5 * dho_ref[...]).astype(BF16)
        dhb_ref[...] = dhalf
        dn = jnp.zeros((tm, D_MODEL), F32)
        for s in range(N_SHARD):
            av = a_ref[s].astype(F32)
            bv = b_ref[s].astype(F32)
            sg = _sigmoid(av)
            sil = av * sg
            s_ref[s] = (sil * bv).astype(BF16)
            ds = _dot_nt(dhalf, wd_ref[s])
            da = (ds * bv * (sg * (1.0 + av * (1.0 - sg)))).astype(BF16)
            db = (ds * sil).astype(BF16)
            da_ref[s] = da
            db_ref[s] = db
            dn = dn + _dot(da, wg_ref[s]) + _dot(db, wu_ref[s])
        dx, dg = _rms_bwd(hv, r, g, dn)
        dh_ref[...] = dho_ref[...] + dx
        _accum(dg_ref, dg, pl.program_id(0) == 0)

    tok = pl.BlockSpec((tm, D_MODEL), lambda i: (i, 0))
    hid = pl.BlockSpec((N_SHARD, tm, FF_PAD), lambda i: (0, i, 0))
    return _carry(
        body, comm, name=name, steps=t // tm,
        out_shape=(jax.ShapeDtypeStruct((t, D_MODEL), F32),
                   jax.ShapeDtypeStruct((1, D_MODEL), F32),
                   jax.ShapeDtypeStruct((t, D_MODEL), BF16),
                   jax.ShapeDtypeStruct((t, D_MODEL), BF16),
                   jax.ShapeDtypeStruct((N_SHARD, t, FF_PAD), BF16),
                   jax.ShapeDtypeStruct((N_SHARD, t, FF_PAD), BF16),
                   jax.ShapeDtypeStruct((N_SHARD, t, FF_PAD), BF16)),
        in_specs=[tok, tok, hid, hid, _const_spec((1, D_MODEL)),
                  _const_spec((N_SHARD, FF_PAD, D_MODEL)),
                  _const_spec((N_SHARD, FF_PAD, D_MODEL)),
                  _const_spec((N_SHARD, FF_PAD, D_MODEL))],
        out_specs=(tok, pl.BlockSpec((1, D_MODEL), lambda i: (0, 0)), tok, tok, hid, hid, hid),
        args=(dho, h, a, b, gain, wg, wu, wd),
    )


TN_VMEM_BUDGET = 44 * 1024 * 1024


def tn_matmul(x, y, name, shard, comm=None):
    x3, y3 = x.ndim == 3, y.ndim == 3
    t = x.shape[-2]
    m = x.shape[-1] // (N_SHARD if (shard == "rows" and not x3) else 1)
    n = y.shape[-1] // (N_SHARD if (shard == "cols" and not y3) else 1)
    per_token = 2 * (m * x.dtype.itemsize + n * y.dtype.itemsize)
    tk = t
    while tk > 512 and tk * per_token + 2 * m * n * 4 > TN_VMEM_BUDGET:
        tk //= 2
    nk = t // tk

    out_shape = jax.ShapeDtypeStruct((N_SHARD, m, n), F32)
    if nk == 1:
        def whole(x_ref, y_ref, o_ref):
            o_ref[...] = _dot_tn(x_ref[...].astype(BF16), y_ref[...].astype(BF16))

        x_one = (pl.BlockSpec((None, t, m), lambda s: (s, 0, 0)) if x3 else
                 pl.BlockSpec((t, m), (lambda s: (0, s)) if shard == "rows" else (lambda s: (0, 0))))
        y_one = (pl.BlockSpec((None, t, n), lambda s: (s, 0, 0)) if y3 else
                 pl.BlockSpec((t, n), (lambda s: (0, s)) if shard == "cols" else (lambda s: (0, 0))))
        return _carry(whole, comm, name=name, steps=N_SHARD, out_shape=(out_shape,), in_specs=[x_one, y_one],
                      out_specs=(pl.BlockSpec((None, m, n), lambda s: (s, 0, 0)),), args=(x, y))
    assert comm is None

    def body(x_ref, y_ref, o_ref):
        _accum(o_ref, _dot_tn(x_ref[...].astype(BF16), y_ref[...].astype(BF16)), pl.program_id(1) == 0)

    if x3:
        x_spec = pl.BlockSpec((None, tk, m), lambda s, k: (s, k, 0))
    elif shard == "rows":
        x_spec = pl.BlockSpec((tk, m), lambda s, k: (k, s))
    else:
        x_spec = pl.BlockSpec((tk, m), lambda s, k: (k, 0))
    if y3:
        y_spec = pl.BlockSpec((None, tk, n), lambda s, k: (s, k, 0))
    elif shard == "cols":
        y_spec = pl.BlockSpec((tk, n), lambda s, k: (k, s))
    else:
        y_spec = pl.BlockSpec((tk, n), lambda s, k: (k, 0))
    res = pl.pallas_call(
        body, name=name, grid=(N_SHARD, nk),
        out_shape=out_shape,
        in_specs=[x_spec, y_spec],
        out_specs=pl.BlockSpec((None, m, n), lambda s, k: (s, 0, 0)),
        compiler_params=_cparams(("arbitrary", "arbitrary")),
    )(x, y)
    return (res,), ()


def inproj_fwd(h, gain, w_in, comm=None, tm=256):
    t = h.shape[0]
    widths = [IN_SPLITS[j + 1] - IN_SPLITS[j] for j in range(7)]
    sh_cols = IN_COLS // N_SHARD

    def body(h_ref, g_ref, w_ref, *outs):
        n, _ = _rms_fwd(h_ref[...], g_ref[...])
        nb = n.astype(BF16)
        proj = jnp.concatenate([_dot(nb, w_ref[s]) for s in range(N_SHARD)], axis=1)
        for j, o_ref in enumerate(outs):
            o_ref[...] = proj[:, IN_SPLITS[j]:IN_SPLITS[j + 1]]

    return _carry(
        body, comm, name="inproj_fwd", steps=t // tm,
        out_shape=tuple(jax.ShapeDtypeStruct((t, w), F32) for w in widths),
        in_specs=[pl.BlockSpec((tm, D_MODEL), lambda i: (i, 0)),
                  _const_spec((1, D_MODEL)),
                  _const_spec((N_SHARD, D_MODEL, sh_cols))],
        out_specs=tuple(pl.BlockSpec((tm, w), lambda i: (i, 0)) for w in widths),
        args=(h, gain, w_in),
    )


def inproj_bwd(dres, h, gain, w_in, dparts, comm=None, tm=256):
    t = h.shape[0]
    widths = [IN_SPLITS[j + 1] - IN_SPLITS[j] for j in range(7)]
    sh_cols = IN_COLS // N_SHARD

    def body(dres_ref, h_ref, g_ref, w_ref, d0, d1, d2, d3, d4, d5, d6, dh_ref, dg_ref, nb_ref, dp_ref):
        hv = h_ref[...]
        g = g_ref[...]
        n, r = _rms_fwd(hv, g)
        nb_ref[...] = n.astype(BF16)
        dproj = jnp.concatenate([d[...] for d in (d0, d1, d2, d3, d4, d5, d6)], axis=1).astype(BF16)
        dp_ref[...] = dproj
        dn = jnp.zeros((tm, D_MODEL), F32)
        for s in range(N_SHARD):
            dn = dn + _dot_nt(dproj[:, s * sh_cols:(s + 1) * sh_cols], w_ref[s])
        dx, dg = _rms_bwd(hv, r, g, dn)
        dh_ref[...] = dres_ref[...] + dx
        _accum(dg_ref, dg, pl.program_id(0) == 0)

    tok = pl.BlockSpec((tm, D_MODEL), lambda i: (i, 0))
    return _carry(
        body, comm, name="inproj_bwd", steps=t // tm,
        out_shape=(jax.ShapeDtypeStruct((t, D_MODEL), F32),
                   jax.ShapeDtypeStruct((1, D_MODEL), F32),
                   jax.ShapeDtypeStruct((t, D_MODEL), BF16),
                   jax.ShapeDtypeStruct((t, IN_COLS), BF16)),
        in_specs=[tok, tok, _const_spec((1, D_MODEL)), _const_spec((N_SHARD, D_MODEL, sh_cols))]
                 + [pl.BlockSpec((tm, w), lambda i: (i, 0)) for w in widths],
        out_specs=(tok, pl.BlockSpec((1, D_MODEL), lambda i: (0, 0)), tok,
                   pl.BlockSpec((tm, IN_COLS), lambda i: (i, 0))),
        args=(dres, h, gain, w_in, *dparts),
    )


def s5_prep(lam_re, lam_im, log_dt, b_re, b_im, c_re, c_im):
    dt = jnp.exp(log_dt)[:, None]
    mag = jnp.exp(lam_re * dt)
    lbr = mag * jnp.cos(lam_im * dt)
    lbi = mag * jnp.sin(lam_im * dt)
    den = lam_re * lam_re + lam_im * lam_im
    nr, ni = lbr - 1.0, lbi
    kr = (nr * lam_re + ni * lam_im) / den
    ki = (ni * lam_re - nr * lam_im) / den
    bbr = kr[..., None] * b_re - ki[..., None] * b_im
    bbi = kr[..., None] * b_im + ki[..., None] * b_re
    eye = jnp.eye(16, dtype=F32)

    def bm(bp):
        return jnp.einsum('kgph,gG->kghGp', bp.reshape(S5_KT, 16, S5_STATE, S5_GROUP), eye).reshape(S5_KT, 256, 1024)

    def cm(cp):
        return jnp.einsum('kghp,gG->kgpGh', cp.reshape(S5_KT, 16, S5_GROUP, S5_STATE), eye).reshape(S5_KT, 1024, 256)

    lam_bar = jnp.stack([lbr.reshape(S5_N), lbi.reshape(S5_N)])
    bmat = jnp.stack([bm(bbr), bm(bbi)])
    cmat = jnp.stack([cm(c_re), -cm(c_im)])
    return lam_bar, bmat, cmat


def _lam_powers(lam_bar):
    lr, li = lam_bar[0], lam_bar[1]
    pr, pi = [lr], [li]
    for _ in range(7):
        pr, pi = pr + [pr[-1] * lr - pi[-1] * li], pi + [pr[-1] * li + pi[-1] * lr]
    return jnp.stack(pr), jnp.stack(pi)


SCAN_SHIFTS = ((1, 0), (2, 1), (4, 3))


def _scan_tables(pw_r, pw_i, reverse):
    rows = jnp.arange(8)[:, None]
    planes_r, planes_i = [], []
    for sh, idx in SCAN_SHIFTS:
        keep = (rows < 8 - sh) if reverse else (rows >= sh)
        planes_r.append(jnp.where(keep, pw_r[idx:idx + 1], 0.0))
        planes_i.append(jnp.where(keep, pw_i[idx:idx + 1], 0.0))
    carry = [pw_r[::-1], pw_i[::-1]] if reverse else [pw_r, pw_i]
    return jnp.stack(planes_r + planes_i + carry)


def s5_fwd(u, tab, bmat, cmat, dvec, comm=None, tm=256):
    t = u.shape[0]
    nch = tm // 8

    def body(u_ref, tab_ref, b_ref, c_ref, d_ref, y_ref, xp_ref, x_scr, carry):
        @pl.when(pl.program_id(0) == 0)
        def _():
            carry[...] = jnp.zeros_like(carry)

        uv = u_ref[...]
        ub = uv.astype(BF16)
        for part in range(2):
            for kt in range(S5_KT):
                x_scr[:, pl.ds(part * S5_N + kt * 1024, 1024)] = _dot(ub[:, kt * 256:(kt + 1) * 256], b_ref[part, kt])
        row = lax.broadcasted_iota(jnp.int32, (8, S5_N), 0)

        def chunk(i, c):
            cr, ci = c
            r0 = pl.multiple_of(i * 8, 8)
            xr = x_scr[pl.ds(r0, 8), pl.ds(0, S5_N)]
            xi = x_scr[pl.ds(r0, 8), pl.ds(S5_N, S5_N)]
            for lvl, (sh, _) in enumerate(SCAN_SHIFTS):
                sr = pltpu.roll(xr, sh, 0)
                si = pltpu.roll(xi, sh, 0)
                lr = tab_ref[lvl]
                li = tab_ref[3 + lvl]
                xr, xi = xr + lr * sr - li * si, xi + lr * si + li * sr
            pwr = tab_ref[6]
            pwi = tab_ref[7]
            xr, xi = xr + pwr * cr - pwi * ci, xi + pwr * ci + pwi * cr
            x_scr[pl.ds(r0, 8), pl.ds(0, S5_N)] = xr
            x_scr[pl.ds(r0, 8), pl.ds(S5_N, S5_N)] = xi
            xp_ref[pl.ds(r0, 8), pl.ds(0, S5_N)] = jnp.where(row == 0, cr, pltpu.roll(xr, 1, 0))
            xp_ref[pl.ds(r0, 8), pl.ds(S5_N, S5_N)] = jnp.where(row == 0, ci, pltpu.roll(xi, 1, 0))
            return xr[7:8, :], xi[7:8, :]

        cr, ci = lax.fori_loop(0, nch, chunk, (carry[0:1, :], carry[1:2, :]))
        carry[0:1, :] = cr
        carry[1:2, :] = ci
        for kt in range(S5_KT):
            acc = jnp.zeros((tm, 256), F32)
            for part in range(2):
                acc = acc + _dot(x_scr[:, pl.ds(part * S5_N + kt * 1024, 1024)].astype(BF16), c_ref[part, kt])
            y_ref[:, pl.ds(kt * 256, 256)] = acc + d_ref[:, pl.ds(kt * 256, 256)] * uv[:, kt * 256:(kt + 1) * 256]

    return _carry(
        body, comm, name="s5_fwd", steps=t // tm,
        out_shape=(jax.ShapeDtypeStruct((t, S5_WIDTH), F32),
                   jax.ShapeDtypeStruct((t, 2 * S5_N), F32)),
        in_specs=[pl.BlockSpec((tm, S5_WIDTH), lambda i: (i, 0)),
                  _const_spec((8, 8, S5_N)),
                  _const_spec((2, S5_KT, 256, 1024)), _const_spec((2, S5_KT, 1024, 256)),
                  _const_spec((1, S5_WIDTH))],
        out_specs=(pl.BlockSpec((tm, S5_WIDTH), lambda i: (i, 0)),
                   pl.BlockSpec((tm, 2 * S5_N), lambda i: (i, 0))),
        scratch_shapes=[pltpu.VMEM((tm, 2 * S5_N), F32), pltpu.VMEM((8, S5_N), F32)],
        args=(u, tab, bmat, cmat, dvec),
    )


def s5_bwd(dy, u, xp, tab, bmat, cmat, dvec, comm=None, tm=256):
    t = u.shape[0]
    nt = t // tm
    nch = tm // 8

    def body(dy_ref, u_ref, xp_ref, tab_ref, b_ref, c_ref, d_ref,
             du_ref, db_ref, dc_ref, dl_ref, dd_ref, g_scr, x_scr, carry):
        first = pl.program_id(0) == 0

        @pl.when(first)
        def _():
            carry[...] = jnp.zeros_like(carry)
            dl_ref[...] = jnp.zeros_like(dl_ref)

        dyv = dy_ref[...]
        uv = u_ref[...]
        dyb = dyv.astype(BF16)
        ub = uv.astype(BF16)
        lr1 = tab_ref[6, 7:8, :]
        li1 = tab_ref[7, 7:8, :]
        for kt in range(S5_KT):
            cols = pl.ds(kt * 1024, 1024)
            colsi = pl.ds(S5_N + kt * 1024, 1024)
            g_scr[:, cols] = _dot_nt(dyb[:, kt * 256:(kt + 1) * 256], c_ref[0, kt])
            g_scr[:, colsi] = _dot_nt(dyb[:, kt * 256:(kt + 1) * 256], c_ref[1, kt])
            bur = _dot(ub[:, kt * 256:(kt + 1) * 256], b_ref[0, kt])
            bui = _dot(ub[:, kt * 256:(kt + 1) * 256], b_ref[1, kt])
            xpr = xp_ref[:, cols]
            xpi = xp_ref[:, colsi]
            lrk = lr1[:, kt * 1024:(kt + 1) * 1024]
            lik = li1[:, kt * 1024:(kt + 1) * 1024]
            x_scr[:, cols] = lrk * xpr - lik * xpi + bur
            x_scr[:, colsi] = lrk * xpi + lik * xpr + bui

        def chunk(j, c):
            cr, ci = c
            r0 = pl.multiple_of((nch - 1 - j) * 8, 8)
            gr = g_scr[pl.ds(r0, 8), pl.ds(0, S5_N)]
            gi = g_scr[pl.ds(r0, 8), pl.ds(S5_N, S5_N)]
            for lvl, (sh, _) in enumerate(SCAN_SHIFTS):
                sr = pltpu.roll(gr, 8 - sh, 0)
                si = pltpu.roll(gi, 8 - sh, 0)
                lr = tab_ref[lvl]
                li = tab_ref[3 + lvl]
                gr, gi = gr + lr * sr + li * si, gi + lr * si - li * sr
            pvr = tab_ref[6]
            pvi = tab_ref[7]
            gr, gi = gr + pvr * cr + pvi * ci, gi + pvr * ci - pvi * cr
            g_scr[pl.ds(r0, 8), pl.ds(0, S5_N)] = gr
            g_scr[pl.ds(r0, 8), pl.ds(S5_N, S5_N)] = gi
            xpr = xp_ref[pl.ds(r0, 8), pl.ds(0, S5_N)]
            xpi = xp_ref[pl.ds(r0, 8), pl.ds(S5_N, S5_N)]
            dl_ref[0] += gr * xpr + gi * xpi
            dl_ref[1] += gi * xpr - gr * xpi
            return gr[0:1, :], gi[0:1, :]

        cr, ci = lax.fori_loop(0, nch, chunk, (carry[0:1, :], carry[1:2, :]))
        carry[0:1, :] = cr
        carry[1:2, :] = ci

        for kt in range(S5_KT):
            du = jnp.zeros((tm, 256), F32)
            ukt = ub[:, kt * 256:(kt + 1) * 256]
            dykt = dyb[:, kt * 256:(kt + 1) * 256]
            for part in range(2):
                gb = g_scr[:, pl.ds(part * S5_N + kt * 1024, 1024)].astype(BF16)
                xb = x_scr[:, pl.ds(part * S5_N + kt * 1024, 1024)].astype(BF16)
                du = du + _dot_nt(gb, b_ref[part, kt])
                dbv = _dot_tn(ukt, gb)
                dcv = _dot_tn(xb, dykt)

                @pl.when(first)
                def _():
                    db_ref[part, kt] = dbv
                    dc_ref[part, kt] = dcv

                @pl.when(jnp.logical_not(first))
                def _():
                    db_ref[part, kt] += dbv
                    dc_ref[part, kt] += dcv
            du_ref[:, pl.ds(kt * 256, 256)] = du + d_ref[:, pl.ds(kt * 256, 256)] * dyv[:, kt * 256:(kt + 1) * 256]
        _accum(dd_ref, jnp.sum(dyv * uv, axis=0, keepdims=True), first)

    rev = lambda i: (nt - 1 - i, 0)
    return _carry(
        body, comm, name="s5_bwd", steps=nt,
        out_shape=(jax.ShapeDtypeStruct((t, S5_WIDTH), F32),
                   jax.ShapeDtypeStruct((2, S5_KT, 256, 1024), F32),
                   jax.ShapeDtypeStruct((2, S5_KT, 1024, 256), F32),
                   jax.ShapeDtypeStruct((2, 8, S5_N), F32),
                   jax.ShapeDtypeStruct((1, S5_WIDTH), F32)),
        in_specs=[pl.BlockSpec((tm, S5_WIDTH), rev), pl.BlockSpec((tm, S5_WIDTH), rev),
                  pl.BlockSpec((tm, 2 * S5_N), rev),
                  _const_spec((8, 8, S5_N)),
                  _const_spec((2, S5_KT, 256, 1024)), _const_spec((2, S5_KT, 1024, 256)),
                  _const_spec((1, S5_WIDTH))],
        out_specs=(pl.BlockSpec((tm, S5_WIDTH), rev),
                   pl.BlockSpec((2, S5_KT, 256, 1024), lambda i: (0, 0, 0, 0)),
                   pl.BlockSpec((2, S5_KT, 1024, 256), lambda i: (0, 0, 0, 0)),
                   pl.BlockSpec((2, 8, S5_N), lambda i: (0, 0, 0)),
                   pl.BlockSpec((1, S5_WIDTH), lambda i: (0, 0))),
        scratch_shapes=[pltpu.VMEM((tm, 2 * S5_N), F32), pltpu.VMEM((tm, 2 * S5_N), F32),
                        pltpu.VMEM((8, S5_N), F32)],
        args=(dy, u, xp, tab, bmat, cmat, dvec),
    )


def _hg_gates(z, lb):
    sg = _sigmoid(z)
    sgn = _sigmoid(-z)
    fg = lb + (1.0 - lb) * sg
    return sg, sgn, fg, jnp.log(fg), (1.0 - lb) * sgn


def _hg_decays(g, tril):
    gc = jnp.dot(tril, g, precision=HIGHEST, preferred_element_type=F32)
    mid = gc[CHUNK // 2 - 1:CHUNK // 2, :]
    last = gc[CHUNK - 1:CHUNK, :]
    return jnp.exp(gc), jnp.exp(gc - mid), jnp.exp(mid - gc), jnp.exp(last - gc), jnp.exp(last)


def _split_bf16(x):
    hi = x.astype(BF16)
    return hi, (x - hi.astype(F32)).astype(BF16)


def _hg_scores(qt, qlo, kt, klo, sl, causal):
    a = _dot_nt(qt[:, sl], kt[:, sl]) + _dot_nt(qt[:, sl], klo[:, sl]) + _dot_nt(qlo[:, sl], kt[:, sl])
    return jnp.where(causal, a, 0.0).astype(BF16)


def hgrn_fwd(q, f, v, lb, comm=None):
    t = q.shape[0]
    nc = t // CHUNK
    scale = HG_E ** -0.5

    def body(q_ref, f_ref, v_ref, lb_ref, o_ref, st_ref, state):
        @pl.when(pl.program_id(0) == 0)
        def _():
            state[...] = jnp.zeros_like(state)

        ri = lax.broadcasted_iota(jnp.int32, (CHUNK, CHUNK), 0)
        ci = lax.broadcasted_iota(jnp.int32, (CHUNK, CHUNK), 1)
        causal = ri >= ci
        tril = causal.astype(F32)
        for sub in range(HG_SUB):
            rows = pl.ds(sub * CHUNK, CHUNK)
            _, _, _, g, k = _hg_gates(f_ref[rows, :], lb_ref[...])
            eg, eq, ek, ed, el = _hg_decays(g, tril)
            qs = q_ref[rows, :] * scale
            qg = (qs * eg).astype(BF16)
            qt, qlo = _split_bf16(qs * eq)
            kt, klo = _split_bf16(k * ek)
            kd = (k * ed).astype(BF16)
            vb = v_ref[rows, :].astype(BF16)
            for h in range(HG_HEADS):
                sl = slice(h * HG_E, (h + 1) * HG_E)
                st = state[h]
                a = _hg_scores(qt, qlo, kt, klo, sl, causal)
                o_ref[rows, sl] = _dot(a, vb[:, sl]) + _dot_nt(qg[:, sl], st.astype(BF16))
                st_new = st * el[:, sl] + _dot_tn(vb[:, sl], kd[:, sl])
                state[h] = st_new
                st_ref[sub, h] = st_new

    tok = pl.BlockSpec((HG_SUB * CHUNK, HG_WIDTH), lambda i: (i, 0))
    return _carry(
        body, comm, name="hgrn_fwd", steps=nc // HG_SUB,
        out_shape=(jax.ShapeDtypeStruct((t, HG_WIDTH), F32),
                   jax.ShapeDtypeStruct((nc, HG_HEADS, HG_E, HG_E), F32)),
        in_specs=[tok, tok, tok, _const_spec((1, HG_WIDTH))],
        out_specs=(tok, pl.BlockSpec((HG_SUB, HG_HEADS, HG_E, HG_E), lambda i: (i, 0, 0, 0))),
        scratch_shapes=[pltpu.VMEM((HG_HEADS, HG_E, HG_E), F32)],
        args=(q, f, v, lb),
    )


def hgrn_bwd(do, q, f, v, lb, states, comm=None):
    t = q.shape[0]
    nc = t // CHUNK
    scale = HG_E ** -0.5

    ns = nc // HG_SUB

    def body(do_ref, q_ref, f_ref, v_ref, lb_ref, scur_ref, sprev_ref, dq_ref, df_ref, dv_ref, dlb_ref, dstate):
        first = pl.program_id(0) == 0
        has_prev = jnp.where(pl.program_id(0) < ns - 1, 1.0, 0.0)

        @pl.when(first)
        def _():
            dstate[...] = jnp.zeros_like(dstate)

        ri = lax.broadcasted_iota(jnp.int32, (CHUNK, CHUNK), 0)
        ci = lax.broadcasted_iota(jnp.int32, (CHUNK, CHUNK), 1)
        causal = ri >= ci
        tril = causal.astype(F32)
        triu = (ri <= ci).astype(F32)
        rowc = lax.broadcasted_iota(jnp.int32, (CHUNK, HG_WIDTH), 0)
        lb = lb_ref[...]
        dlb = jnp.zeros((1, HG_WIDTH), F32)
        for sub in reversed(range(HG_SUB)):
            rows = pl.ds(sub * CHUNK, CHUNK)
            sg, sgn, fg, g, k = _hg_gates(f_ref[rows, :], lb)
            eg, eq, ek, ed, el = _hg_decays(g, tril)
            qs = q_ref[rows, :] * scale
            qg = (qs * eg).astype(BF16)
            qt, qlo = _split_bf16(qs * eq)
            kt, klo = _split_bf16(k * ek)
            kd = (k * ed).astype(BF16)
            vb = v_ref[rows, :].astype(BF16)
            dob = do_ref[rows, :].astype(BF16)
            dqs_l, dk_l, dgc_l, dgl_l = [], [], [], []
            for h in range(HG_HEADS):
                sl = slice(h * HG_E, (h + 1) * HG_E)
                s0 = scur_ref[sub - 1, h] if sub > 0 else sprev_ref[HG_SUB - 1, h] * has_prev
                ds1 = dstate[h]
                ds1b = ds1.astype(BF16)
                a = _hg_scores(qt, qlo, kt, klo, sl, causal)
                da = jnp.where(causal, _dot_nt(dob[:, sl], vb[:, sl]), 0.0).astype(BF16)
                dv_ref[rows, sl] = _dot_tn(a, dob[:, sl]) + _dot_nt(kd[:, sl], ds1b)
                dkd = _dot(vb[:, sl], ds1b)
                dqt = _dot(da, kt[:, sl])
                dkt = _dot_tn(da, qt[:, sl])
                dqg = _dot(dob[:, sl], s0.astype(BF16))
                dqs_l.append(dqt * eq[:, sl] + dqg * eg[:, sl])
                dk_l.append(dkt * ek[:, sl] + dkd * ed[:, sl])
                kd_dkd = kd[:, sl].astype(F32) * dkd
                dgc_l.append(qt[:, sl].astype(F32) * dqt - kt[:, sl].astype(F32) * dkt
                             + qg[:, sl].astype(F32) * dqg - kd_dkd)
                dgl_l.append(el[:, sl] * jnp.sum(ds1 * s0, axis=0, keepdims=True)
                             + jnp.sum(kd_dkd, axis=0, keepdims=True))
                dstate[h] = ds1 * el[:, sl] + _dot_tn(dob[:, sl], qg[:, sl])
            dqs = jnp.concatenate(dqs_l, axis=1)
            dk = jnp.concatenate(dk_l, axis=1)
            dgl = jnp.concatenate(dgl_l, axis=1)
            dq_ref[rows, :] = dqs * scale
            dgc = jnp.concatenate(dgc_l, axis=1) + jnp.where(rowc == CHUNK - 1, dgl, 0.0)
            dg = jnp.dot(triu, dgc, precision=HIGHEST, preferred_element_type=F32)
            w = dg / fg - dk
            df_ref[rows, :] = w * (1.0 - lb) * sg * sgn
            dlb = dlb + jnp.sum(w * sgn, axis=0, keepdims=True)
        _accum(dlb_ref, dlb, first)

    rev = lambda i: (ns - 1 - i, 0)
    tok = pl.BlockSpec((HG_SUB * CHUNK, HG_WIDTH), rev)
    st_blk = (HG_SUB, HG_HEADS, HG_E, HG_E)
    return _carry(
        body, comm, name="hgrn_bwd", steps=ns,
        out_shape=(jax.ShapeDtypeStruct((t, HG_WIDTH), F32),
                   jax.ShapeDtypeStruct((t, HG_WIDTH), F32),
                   jax.ShapeDtypeStruct((t, HG_WIDTH), F32),
                   jax.ShapeDtypeStruct((1, HG_WIDTH), F32)),
        in_specs=[tok, tok, tok, tok, _const_spec((1, HG_WIDTH)),
                  pl.BlockSpec(st_blk, lambda i: (ns - 1 - i, 0, 0, 0)),
                  pl.BlockSpec(st_blk, lambda i: (jnp.maximum(ns - 2 - i, 0), 0, 0, 0))],
        out_specs=(tok, tok, tok, pl.BlockSpec((1, HG_WIDTH), lambda i: (0, 0))),
        scratch_shapes=[pltpu.VMEM((HG_HEADS, HG_E, HG_E), F32)],
        args=(do, q, f, v, lb, states, states),
    )


GELU_C = math.sqrt(2.0 / math.pi)


def _gelu(x):
    th = jnp.tanh(GELU_C * (x + 0.044715 * x * x * x))
    return 0.5 * x * (1.0 + th), th


def _merge_core(ys5, o, og, ga, gb, wv_ref, wt_ref, ghg, who_ref):
    ys, th = _gelu(ys5)
    ysb = ys.astype(BF16)
    va = jnp.concatenate([_dot(ysb, wv_ref[s]) for s in range(N_SHARD)], axis=1)
    vt = jnp.concatenate([_dot(ysb, wt_ref[s]) for s in range(N_SHARD)], axis=1)
    svt = _sigmoid(vt)
    ya = va * svt
    rs, ons = [], []
    for h in range(HG_HEADS):
        oh = o[:, h * HG_E:(h + 1) * HG_E]
        r = lax.rsqrt(jnp.mean(oh * oh, axis=-1, keepdims=True) + NORM_EPS)
        rs.append(r)
        ons.append(oh * r)
    on = jnp.concatenate(ons, axis=1)
    sgo = _sigmoid(og)
    o2 = on * ghg * (og * sgo)
    o2b = o2.astype(BF16)
    yb = _dot(o2b, who_ref[...])
    sa = _sigmoid(ga)
    sb = _sigmoid(gb)
    mixed = sa * ya + sb * yb
    return dict(ys=ys, th=th, ysb=ysb, va=va, svt=svt, ya=ya, rs=rs, on=on, sgo=sgo, o2b=o2b, yb=yb,
                sa=sa, sb=sb, mixed=mixed)


def merge_fwd(h, ys5, o, og, ga, gb, wv, wt, ghg, who, wmo, tm=256):
    t = h.shape[0]

    def body(h_ref, ys5_ref, o_ref, og_ref, ga_ref, gb_ref, wv_ref, wt_ref, ghg_ref, who_ref, wmo_ref, out_ref):
        c = _merge_core(ys5_ref[...], o_ref[...], og_ref[...], ga_ref[...], gb_ref[...],
                        wv_ref, wt_ref, ghg_ref[...], who_ref)
        out_ref[...] = h_ref[...] + _dot(c["mixed"].astype(BF16), wmo_ref[...])

    tok = pl.BlockSpec((tm, D_MODEL), lambda i: (i, 0))
    return pl.pallas_call(
        body, name="merge_fwd", grid=(t // tm,),
        out_shape=jax.ShapeDtypeStruct((t, D_MODEL), F32),
        in_specs=[tok, pl.BlockSpec((tm, S5_WIDTH), lambda i: (i, 0)), tok, tok, tok, tok,
                  _const_spec((N_SHARD, S5_WIDTH, 256)), _const_spec((N_SHARD, S5_WIDTH, 256)),
                  _const_spec((1, HG_WIDTH)), _const_spec((HG_WIDTH, D_MODEL)), _const_spec((D_MODEL, D_MODEL))],
        out_specs=tok,
        compiler_params=_cparams(("arbitrary",)),
    )(h, ys5, o, og, ga, gb, wv, wt, ghg, who, wmo)


def merge_bwd(dh, ys5, o, og, ga, gb, wv, wt, ghg, who, wmo, comm=None, tm=256):
    t = dh.shape[0]

    def body(dh_ref, ys5_ref, o_ref, og_ref, ga_ref, gb_ref, wv_ref, wt_ref, ghg_ref, who_ref, wmo_ref,
             dys5_ref, do_ref, dog_ref, dga_ref, dgb_ref, dghg_ref,
             mixb_ref, dhb_ref, ysb_ref, dvab_ref, dvtb_ref, o2b_ref, dybb_ref):
        ys5 = ys5_ref[...]
        o = o_ref[...]
        og = og_ref[...]
        ghg = ghg_ref[...]
        c = _merge_core(ys5, o, og, ga_ref[...], gb_ref[...], wv_ref, wt_ref, ghg, who_ref)
        dhb = dh_ref[...].astype(BF16)
        dhb_ref[...] = dhb
        mixb_ref[...] = c["mixed"].astype(BF16)
        ysb_ref[...] = c["ysb"]
        o2b_ref[...] = c["o2b"]
        dmix = _dot_nt(dhb, wmo_ref[...])
        sa, sb = c["sa"], c["sb"]
        dya = dmix * sa
        dyb = dmix * sb
        dga_ref[...] = dmix * c["ya"] * sa * (1.0 - sa)
        dgb_ref[...] = dmix * c["yb"] * sb * (1.0 - sb)
        svt = c["svt"]
        dva = (dya * svt).astype(BF16)
        dvt = (dya * c["va"] * svt * (1.0 - svt)).astype(BF16)
        dvab_ref[...] = dva
        dvtb_ref[...] = dvt
        dys = jnp.zeros((tm, S5_WIDTH), F32)
        for s in range(N_SHARD):
            dys = dys + _dot_nt(dva[:, s * 256:(s + 1) * 256], wv_ref[s]) + _dot_nt(dvt[:, s * 256:(s + 1) * 256], wt_ref[s])
        th = c["th"]
        dgelu = 0.5 * (1.0 + th) + 0.5 * ys5 * (1.0 - th * th) * GELU_C * (1.0 + 3.0 * 0.044715 * ys5 * ys5)
        dys5_ref[...] = dys * dgelu
        dybb = dyb.astype(BF16)
        dybb_ref[...] = dybb
        do2 = _dot_nt(dybb, who_ref[...])
        sgo = c["sgo"]
        sil = og * sgo
        on = c["on"]
        dog_ref[...] = do2 * on * ghg * (sgo * (1.0 + og * (1.0 - sgo)))
        _accum(dghg_ref, jnp.sum(do2 * on * sil, axis=0, keepdims=True), pl.program_id(0) == 0)
        don = do2 * ghg * sil
        dos = []
        for h in range(HG_HEADS):
            sl = slice(h * HG_E, (h + 1) * HG_E)
            m = jnp.mean(don[:, sl] * on[:, sl], axis=-1, keepdims=True)
            dos.append(c["rs"][h] * (don[:, sl] - on[:, sl] * m))
        do_ref[...] = jnp.concatenate(dos, axis=1)

    tok = pl.BlockSpec((tm, D_MODEL), lambda i: (i, 0))
    s5b = pl.BlockSpec((tm, S5_WIDTH), lambda i: (i, 0))
    f32t = jax.ShapeDtypeStruct((t, D_MODEL), F32)
    bft = jax.ShapeDtypeStruct((t, D_MODEL), BF16)
    return _carry(
        body, comm, name="merge_bwd", steps=t // tm,
        out_shape=(jax.ShapeDtypeStruct((t, S5_WIDTH), F32), f32t, f32t, f32t, f32t,
                   jax.ShapeDtypeStruct((1, HG_WIDTH), F32),
                   bft, bft, jax.ShapeDtypeStruct((t, S5_WIDTH), BF16), bft, bft, bft, bft),
        in_specs=[tok, s5b, tok, tok, tok, tok,
                  _const_spec((N_SHARD, S5_WIDTH, 256)), _const_spec((N_SHARD, S5_WIDTH, 256)),
                  _const_spec((1, HG_WIDTH)), _const_spec((HG_WIDTH, D_MODEL)), _const_spec((D_MODEL, D_MODEL))],
        out_specs=(s5b, tok, tok, tok, tok, pl.BlockSpec((1, HG_WIDTH), lambda i: (0, 0)),
                   tok, tok, s5b, tok, tok, tok, tok),
        args=(dh, ys5, o, og, ga, gb, wv, wt, ghg, who, wmo),
    )


def head_fwd_bwd(h, p, tgt, gple, wpg, wpp, gfin, tm=256):
    t = h.shape[0]

    def body(h_ref, p_ref, tgt_ref, gple_ref, wpg_ref, wpp_ref, gfin_ref,
             loss_ref, dh_ref, dgple_ref, dgfin_ref, nb_ref, dlb_ref, dppb_ref):
        first = pl.program_id(0) == 0
        hv = h_ref[...]
        gple = gple_ref[...]
        gfin = gfin_ref[...]
        n, r3 = _rms_fwd(hv, gple)
        nb = n.astype(BF16)
        nb_ref[...] = nb
        pg = _sigmoid(_dot(nb, wpg_ref[...]))
        pb = p_ref[...].astype(BF16)
        pp = jnp.concatenate([_dot(pb, wpp_ref[s]) for s in range(N_SHARD)], axis=1)
        h4 = hv + pg * pp
        y, r4 = _rms_fwd(h4, gfin)
        err = y - tgt_ref[...]
        lsum = 0.5 * jnp.sum(jnp.sum(err * err, axis=-1, keepdims=True), axis=0, keepdims=True) / D_MODEL
        _accum(loss_ref, jnp.broadcast_to(lsum, (8, 128)), first)
        dy = err * (1.0 / D_MODEL)
        dh4, dgf = _rms_bwd(h4, r4, gfin, dy)
        _accum(dgfin_ref, dgf, first)
        dpp = dh4 * pg
        dppb_ref[...] = dpp.astype(BF16)
        dl = (dh4 * pp * pg * (1.0 - pg)).astype(BF16)
        dlb_ref[...] = dl
        dn = _dot_nt(dl, wpg_ref[...])
        dx, dgp = _rms_bwd(hv, r3, gple, dn)
        _accum(dgple_ref, dgp, first)
        dh_ref[...] = dh4 + dx

    tok = pl.BlockSpec((tm, D_MODEL), lambda i: (i, 0))
    vec = pl.BlockSpec((1, D_MODEL), lambda i: (0, 0))
    bft = jax.ShapeDtypeStruct((t, D_MODEL), BF16)
    return pl.pallas_call(
        body, name="head_fwd_bwd", grid=(t // tm,),
        out_shape=(jax.ShapeDtypeStruct((8, 128), F32), jax.ShapeDtypeStruct((t, D_MODEL), F32),
                   jax.ShapeDtypeStruct((1, D_MODEL), F32), jax.ShapeDtypeStruct((1, D_MODEL), F32),
                   bft, bft, bft),
        in_specs=[tok, pl.BlockSpec((tm, PLE_DIM), lambda i: (i, 0)), tok,
                  _const_spec((1, D_MODEL)), _const_spec((D_MODEL, D_MODEL)),
                  _const_spec((N_SHARD, PLE_DIM, 256)), _const_spec((1, D_MODEL))],
        out_specs=(pl.BlockSpec((8, 128), lambda i: (0, 0)), tok, vec, vec, tok, tok, tok),
        compiler_params=_cparams(("arbitrary",)),
    )(h, p, tgt, gple, wpg, wpp, gfin)


BIG = ("ffn1_w_gate", "ffn1_w_up", "ffn1_w_down", "w_in", "s5_glu_val", "s5_glu_gate", "hg_w_out",
       "w_merge_out", "ffn2_w_gate", "ffn2_w_up", "ffn2_w_down", "ple_w_gate", "ple_w_proj")
FFN_T = ("ffn1_w_gate", "ffn1_w_up", "ffn2_w_gate", "ffn2_w_up")
BIG_SHARD = {
    "ffn1_w_gate": (FF_PAD, D_MODEL), "ffn1_w_up": (FF_PAD, D_MODEL), "ffn1_w_down": (FF_PAD, D_MODEL),
    "ffn2_w_gate": (FF_PAD, D_MODEL), "ffn2_w_up": (FF_PAD, D_MODEL), "ffn2_w_down": (FF_PAD, D_MODEL),
    "w_in": (D_MODEL, IN_COLS // N_SHARD), "s5_glu_val": (S5_WIDTH, 256), "s5_glu_gate": (S5_WIDTH, 256),
    "hg_w_out": (256, D_MODEL), "w_merge_out": (256, D_MODEL), "ple_w_gate": (256, D_MODEL),
    "ple_w_proj": (PLE_DIM, 256),
}


def _lower_bound(hb):
    return jax.nn.softmax(hb, axis=0)[0:1]


class Schedule:
    def __init__(self, wts):
        self.wts = dict(wts)
        self.grads = {}

    def before(self, kernel_name):
        return None

    def after(self, kernel_name, results):
        pass

    def grad(self, name, g):
        self.grads[name] = g


def local_step(x, p, tgt, sched, sm):
    wts = sched.wts
    rows_full = lambda w: w.reshape(N_SHARD * w.shape[1], w.shape[2])

    def carried(kernel_name, fn, *args):
        outs, results = fn(*args, comm=sched.before(kernel_name))
        sched.after(kernel_name, results)
        return outs

    def weight_grad(name, xs, ys, shard):
        kernel_name = "g_" + name
        (g,), results = tn_matmul(xs, ys, kernel_name, shard, comm=sched.before(kernel_name))
        sched.grad(name, g)
        sched.after(kernel_name, results)

    lb, lb_vjp = jax.vjp(_lower_bound, sm["hg_lower_bound"])
    s5_names = ("s5_lam_re", "s5_lam_im", "s5_log_dt", "s5_b_re", "s5_b_im", "s5_c_re", "s5_c_im")
    (lam_bar, bmat, cmat), s5_vjp = jax.vjp(s5_prep, *[sm[k] for k in s5_names])
    pw_r, pw_i = _lam_powers(lam_bar)
    bmat_b = bmat.astype(BF16)
    cmat_b = cmat.astype(BF16)

    h1, a1, b1 = carried("ffn1_fwd", ffn_fwd, x, sm["ffn1_norm"], wts["ffn1_w_gate"], wts["ffn1_w_up"],
                         wts["ffn1_w_down"], "ffn1_fwd")
    s5in, q, f, v, og, ga, gb = carried("inproj_fwd", inproj_fwd, h1, sm["mix_norm"], wts["w_in"])
    ys5, xp = carried("s5_fwd", s5_fwd, s5in, _scan_tables(pw_r, pw_i, False), bmat_b, cmat_b, sm["s5_d"])
    o, states = carried("hgrn_fwd", hgrn_fwd, q, f, v, lb)
    who = rows_full(wts["hg_w_out"])
    wmo = rows_full(wts["w_merge_out"])
    h2 = merge_fwd(h1, ys5, o, og, ga, gb, wts["s5_glu_val"], wts["s5_glu_gate"], sm["hg_out_norm"], who, wmo)
    (h3, a2, b2), _ = ffn_fwd(h2, sm["ffn2_norm"], wts["ffn2_w_gate"], wts["ffn2_w_up"], wts["ffn2_w_down"], "ffn2_fwd")
    loss, dh3, d_ple_norm, d_final_norm, npb, dlgb, dppb = head_fwd_bwd(
        h3, p, tgt, sm["ple_norm"], rows_full(wts["ple_w_gate"]), wts["ple_w_proj"], sm["final_norm"])

    gs = {"ple_norm": d_ple_norm, "final_norm": d_final_norm}
    weight_grad("ple_w_gate", npb, dlgb, "rows")
    weight_grad("ple_w_proj", p, dppb, "cols")

    (dh2, gs["ffn2_norm"], n2b, dhb2, da2, db2, s2), _ = ffn_bwd(
        dh3, h2, a2, b2, sm["ffn2_norm"], wts["ffn2_w_gate"], wts["ffn2_w_up"], wts["ffn2_w_down"], "ffn2_bwd")
    weight_grad("ffn2_w_gate", da2, n2b, "rows")
    weight_grad("ffn2_w_up", db2, n2b, "rows")
    weight_grad("ffn2_w_down", s2, dhb2, "rows")

    dys5, do, dog, dga, dgb, gs["hg_out_norm"], mixb, dh2b, ysb, dvab, dvtb, o2b, dybb = carried(
        "merge_bwd", merge_bwd,
        dh2, ys5, o, og, ga, gb, wts["s5_glu_val"], wts["s5_glu_gate"], sm["hg_out_norm"], who, wmo)
    weight_grad("w_merge_out", mixb, dh2b, "rows")
    weight_grad("s5_glu_val", ysb, dvab, "cols")
    weight_grad("s5_glu_gate", ysb, dvtb, "cols")
    weight_grad("hg_w_out", o2b, dybb, "rows")

    dq, df, dv, dlb = carried("hgrn_bwd", hgrn_bwd, do, q, f, v, lb, states)
    (gs["hg_lower_bound"],) = lb_vjp(dlb)
    du, dbmat, dcmat, dlam8, gs["s5_d"] = carried(
        "s5_bwd", s5_bwd,
        dys5, s5in, xp, _scan_tables(pw_r, pw_i, True), bmat_b, cmat_b, sm["s5_d"])
    for k, g in zip(s5_names, s5_vjp((jnp.sum(dlam8, axis=1), dbmat, dcmat))):
        gs[k] = g

    dh1, gs["mix_norm"], nmb, dprojb = carried(
        "inproj_bwd", inproj_bwd, dh2, h1, sm["mix_norm"], wts["w_in"], (du, dq, df, dv, dog, dga, dgb))
    weight_grad("w_in", nmb, dprojb, "cols")

    dx, gs["ffn1_norm"], n1b, dhb1, da1, db1, s1 = carried(
        "ffn1_bwd", ffn_bwd,
        dh1, x, a1, b1, sm["ffn1_norm"], wts["ffn1_w_gate"], wts["ffn1_w_up"], wts["ffn1_w_down"], "ffn1_bwd")
    weight_grad("ffn1_w_gate", da1, n1b, "rows")
    weight_grad("ffn1_w_up", db1, n1b, "rows")
    weight_grad("ffn1_w_down", s1, dhb1, "rows")
    return loss, dx, gs


MESH = pl.DeviceIdType.MESH
ANY = pl.BlockSpec(memory_space=pl.ANY)


def _place():
    x, y, c = lax.axis_index("x"), lax.axis_index("y"), lax.axis_index("c")
    return x, y, c


def _remote(src, dst, ssem, rsem, dev):
    return pltpu.make_async_remote_copy(src_ref=src, dst_ref=dst, send_sem=ssem, recv_sem=rsem,
                                        device_id=dev, device_id_type=MESH)


class Comm:
    def __init__(self, bufs, outs, alias, sems, hooks):
        self.bufs, self.outs, self.alias, self.sems, self.hooks = list(bufs), list(outs), alias, list(sems), hooks


def run_comm(comm, name):
    nb, no = len(comm.bufs), len(comm.outs)

    def body(*refs):
        for which in ("first", "mid", "last"):
            if which in comm.hooks:
                comm.hooks[which](refs[:nb], refs[nb:nb + no], refs[nb + no:])

    return pl.pallas_call(
        body, name=name, out_shape=tuple(comm.outs), in_specs=[ANY] * nb, out_specs=tuple([ANY] * no),
        input_output_aliases=dict(comm.alias), scratch_shapes=comm.sems,
    )(*comm.bufs)


PLACE_ROWS = {1024: 256, 704: 352, 512: 256, 256: 256}


def place_shards(shards, padded_rows, comm, name):
    n, nb, no = len(shards), len(comm.bufs), len(comm.outs)
    stage_rows = max(PLACE_ROWS.values())
    stage_cols = max(s.shape[1] for s in shards)

    def body(*refs):
        ins, cb = refs[:n], refs[n:n + nb]
        outs, co = refs[n + nb:2 * n + nb], refs[2 * n + nb:2 * n + nb + no]
        stage_f32, stage_bf16, zeros, sem = refs[2 * n + nb + no:2 * n + nb + no + 4]
        cs = refs[2 * n + nb + no + 4:]
        chip = 2 * lax.axis_index("x") + lax.axis_index("y")
        zeros[...] = jnp.zeros_like(zeros)
        comm.hooks["first"](cb, co, cs)
        for w in range(n):
            if w == n // 2:
                comm.hooks["mid"](cb, co, cs)
            r0, cols = ins[w].shape
            step = PLACE_ROWS[r0]
            src32 = stage_f32.at[pl.ds(0, step), pl.ds(0, cols)]
            dst16 = stage_bf16.at[pl.ds(0, step), pl.ds(0, cols)]
            for row in range(0, r0, step):
                pltpu.sync_copy(ins[w].at[pl.ds(row, step), :], src32)
                dst16[...] = src32[...].astype(BF16)
                pltpu.sync_copy(dst16, outs[w].at[chip, pl.ds(row, step), :])
            pad = outs[w].shape[1] - r0
            if pad:
                cp = pltpu.make_async_copy(zeros.at[pl.ds(0, pad), pl.ds(0, cols)],
                                           outs[w].at[chip, pl.ds(r0, pad), :], sem)
                cp.start()
                cp.wait()
        comm.hooks["last"](cb, co, cs)

    res = pl.pallas_call(
        body, name=name,
        out_shape=tuple(jax.ShapeDtypeStruct((N_SHARD, r, s.shape[1]), BF16) for s, r in zip(shards, padded_rows))
        + tuple(comm.outs),
        in_specs=[ANY] * (n + nb), out_specs=tuple([ANY] * (n + no)),
        input_output_aliases={n + i: n + o for i, o in comm.alias.items()},
        scratch_shapes=[pltpu.VMEM((stage_rows, stage_cols), F32), pltpu.VMEM((stage_rows, stage_cols), BF16),
                        pltpu.VMEM((FF_PAD - FF_SHARD, D_MODEL), BF16), pltpu.SemaphoreType.DMA] + comm.sems,
        compiler_params=pltpu.CompilerParams(vmem_limit_bytes=VMEM_LIMIT),
    )(*shards, *comm.bufs)
    return res[:n], res[n:]


def _carry(body, comm, *, name, steps, out_shape, in_specs, out_specs, args, scratch_shapes=()):
    out_shape, out_specs, scratch_shapes = tuple(out_shape), tuple(out_specs), list(scratch_shapes)
    if comm is None:
        res = pl.pallas_call(body, name=name, grid=(steps,), out_shape=out_shape, in_specs=list(in_specs),
                             out_specs=out_specs, scratch_shapes=scratch_shapes,
                             compiler_params=_cparams(("arbitrary",)))(*args)
        return tuple(res), ()
    n_in, n_out, n_scr = len(args), len(out_shape), len(scratch_shapes)
    nb, no = len(comm.bufs), len(comm.outs)

    def wrapped(*refs):
        ins, cb = refs[:n_in], refs[n_in:n_in + nb]
        o0 = n_in + nb
        outs, co = refs[o0:o0 + n_out], refs[o0 + n_out:o0 + n_out + no]
        s0 = o0 + n_out + no
        scr, cs = refs[s0:s0 + n_scr], refs[s0 + n_scr:]
        step = pl.program_id(0)

        def hook(which, at):
            if which in comm.hooks:
                pl.when(step == at)(lambda: comm.hooks[which](cb, co, cs))

        hook("first", 0)
        hook("mid", steps // 2)
        body(*ins, *outs, *scr)
        hook("last", steps - 1)

    res = pl.pallas_call(
        wrapped, name=name, grid=(steps,), out_shape=out_shape + tuple(comm.outs),
        in_specs=list(in_specs) + [ANY] * nb, out_specs=out_specs + (ANY,) * no,
        scratch_shapes=scratch_shapes + comm.sems,
        input_output_aliases={n_in + i: n_out + o for i, o in comm.alias.items()},
        compiler_params=_cparams(("arbitrary",)),
    )(*args, *comm.bufs)
    return tuple(res[:n_out]), tuple(res[n_out:])


def gather_comm(bufs):
    n = len(bufs)

    def copies(outs, sems):
        s_own, r_own, s_fwd, r_fwd, s_sib, r_sib = sems
        x, y, c = _place()
        me = 2 * x + y
        nbr = ((1 - x, y), (x, 1 - y))
        nbr_id = (2 * (1 - x) + y, 2 * x + (1 - y))
        diag_id = 2 * (1 - x) + (1 - y)
        sib = (x, y, 1 - c)

        def rows(w, q=None):
            r = outs[w].shape[1]
            if q is None:
                return pl.ds(pl.multiple_of(c * (r // 2), 16), r // 2)
            return pl.ds(pl.multiple_of(c * (r // 2) + q * (r // 4), 16), r // 4)

        def own(w, j):
            piece = outs[w].at[me, rows(w)]
            return _remote(piece, piece, s_own.at[w, j], r_own.at[w, j], (nbr[j][0], nbr[j][1], c))

        def from_nbr(w, j):
            piece = outs[w].at[nbr_id[j], rows(w)]
            return _remote(piece, piece, s_own.at[w, j], r_own.at[w, j], (nbr[j][0], nbr[j][1], c))

        def fwd(w, j):
            piece = outs[w].at[nbr_id[j], rows(w, j)]
            return _remote(piece, piece, s_fwd.at[w, j], r_fwd.at[w, j], (nbr[1 - j][0], nbr[1 - j][1], c))

        def from_diag(w, j):
            piece = outs[w].at[diag_id, rows(w, j)]
            return _remote(piece, piece, s_fwd.at[w, j], r_fwd.at[w, j], (nbr[1 - j][0], nbr[1 - j][1], c))

        def to_sib(w, k):
            piece = (outs[w].at[nbr_id[k], rows(w)] if k < 2 else outs[w].at[diag_id, rows(w, k - 2)])
            return _remote(piece, piece, s_sib.at[w, k], r_sib.at[w, k], sib)

        def from_sib(w, k):
            r = outs[w].shape[1]
            if k < 2:
                piece = outs[w].at[nbr_id[k], pl.ds(pl.multiple_of((1 - c) * (r // 2), 16), r // 2)]
            else:
                piece = outs[w].at[diag_id, pl.ds(pl.multiple_of((1 - c) * (r // 2) + (k - 2) * (r // 4), 16), r // 4)]
            return _remote(piece, piece, s_sib.at[w, k], r_sib.at[w, k], sib)

        return own, from_nbr, fwd, from_diag, to_sib, from_sib

    def first(_, outs, sems):
        own = copies(outs, sems)[0]
        for w in range(n):
            own(w, 0).start()
            own(w, 1).start()

    def mid(_, outs, sems):
        _, from_nbr, fwd, _, to_sib, _ = copies(outs, sems)
        for w in range(n):
            for j in range(2):
                from_nbr(w, j).wait_recv()
                fwd(w, j).start()
                to_sib(w, j).start()

    def last(_, outs, sems):
        own, _, fwd, from_diag, to_sib, from_sib = copies(outs, sems)
        for w in range(n):
            for j in range(2):
                from_diag(w, j).wait_recv()
                to_sib(w, 2 + j).start()
        for w in range(n):
            for k in range(4):
                from_sib(w, k).wait_recv()
        for w in range(n):
            for j in range(2):
                own(w, j).wait_send()
                fwd(w, j).wait_send()
            for k in range(4):
                to_sib(w, k).wait_send()

    dma = pltpu.SemaphoreType.DMA
    return Comm(bufs, [jax.ShapeDtypeStruct(b.shape, b.dtype) for b in bufs], {w: w for w in range(n)},
                [dma((n, 2)), dma((n, 2)), dma((n, 2)), dma((n, 2)), dma((n, 4)), dma((n, 4))],
                {"first": first, "mid": mid, "last": last})


def _start_wait(make):
    def first(bufs, outs, sems):
        for cp in make(bufs, outs, sems):
            cp.start()

    def last(bufs, outs, sems):
        for cp in make(bufs, outs, sems):
            cp.wait()

    return {"first": first, "last": last}


def exchange_comm(grads):
    n = len(grads)

    def make(ins, outs, sems):
        x, y, c = _place()
        cps = []
        for w in range(n):
            half = ins[w].shape[1] // 2
            src = ins[w].at[:, pl.ds(pl.multiple_of((1 - c) * half, 8), half), :]
            cps.append(_remote(src, outs[w], sems[0].at[w], sems[1].at[w], (x, y, 1 - c)))
        return cps

    dma = pltpu.SemaphoreType.DMA
    return Comm(grads, [jax.ShapeDtypeStruct((N_SHARD, g.shape[1] // 2, g.shape[2]), g.dtype) for g in grads],
                {}, [dma((n,)), dma((n,))], _start_wait(make))


def scatter_comm(sums):
    n = len(sums)

    def make(ins, outs, sems):
        x, y, c = _place()
        chips = ((1 - x, y), (x, 1 - y), (1 - x, 1 - y))
        return [_remote(ins[w].at[2 * ch[0] + ch[1]], outs[w].at[j], sems[0].at[w, j], sems[1].at[w, j],
                        (ch[0], ch[1], c))
                for w in range(n) for j, ch in enumerate(chips)]

    dma = pltpu.SemaphoreType.DMA
    return Comm(sums, [jax.ShapeDtypeStruct((3,) + s.shape[1:], s.dtype) for s in sums],
                {}, [dma((n, 3)), dma((n, 3))], _start_wait(make))


def join_comm(shards):
    n = len(shards)

    def make(_, outs, sems):
        x, y, c = _place()
        cps = []
        for w in range(n):
            half = outs[w].shape[0] // 2
            mine = outs[w].at[pl.ds(pl.multiple_of(c * half, 8), half), :]
            cps.append(_remote(mine, mine, sems[0].at[w], sems[1].at[w], (x, y, 1 - c)))
        return cps

    dma = pltpu.SemaphoreType.DMA
    return Comm(shards, [jax.ShapeDtypeStruct(s.shape, s.dtype) for s in shards], {w: w for w in range(n)},
                [dma((n,)), dma((n,))], _start_wait(make))


def allreduce_small(vec, comm):
    half = vec.shape[0] // 2
    nb, no = len(comm.bufs), len(comm.outs)

    def body(*refs):
        v_ref, cb, o_ref, co = refs[0], refs[1:1 + nb], refs[1 + nb], refs[2 + nb:2 + nb + no]
        pair, chips_buf, s1, r1, s2, r2, s3, r3 = refs[2 + nb + no:10 + nb + no]
        cs = refs[10 + nb + no:]
        comm.hooks["first"](cb, co, cs)
        x, y, c = _place()
        chip = 2 * x + y
        sib = (x, y, 1 - c)
        mine = pl.ds(pl.multiple_of(c * half, 8), half)
        other = pl.ds(pl.multiple_of((1 - c) * half, 8), half)
        to_sib = _remote(v_ref.at[other], pair, s1, r1, sib)
        to_sib.start()
        to_sib.wait()
        chips_buf[chip] = v_ref[mine, :] + pair[...]
        sends = [_remote(chips_buf.at[chip], chips_buf.at[chip], s2.at[j], r2.at[j], (ch[0], ch[1], c))
                 for j, ch in enumerate(((1 - x, y), (x, 1 - y), (1 - x, 1 - y)))]
        for cp in sends:
            cp.start()
        for cp in sends:
            cp.wait()
        o_ref[mine, :] = (chips_buf[0] + chips_buf[1]) + (chips_buf[2] + chips_buf[3])
        back = _remote(o_ref.at[mine], o_ref.at[mine], s3, r3, sib)
        back.start()
        back.wait()
        comm.hooks["last"](cb, co, cs)

    dma = pltpu.SemaphoreType.DMA
    vmem = pl.BlockSpec(memory_space=pltpu.VMEM)
    res = pl.pallas_call(
        body, name="allreduce_small",
        out_shape=(jax.ShapeDtypeStruct(vec.shape, F32),) + tuple(comm.outs),
        in_specs=[vmem] + [ANY] * nb,
        out_specs=(vmem,) + (ANY,) * no,
        input_output_aliases={1 + i: 1 + o for i, o in comm.alias.items()},
        scratch_shapes=[pltpu.VMEM((half, 128), F32), pltpu.VMEM((N_SHARD, half, 128), F32),
                        dma, dma, dma((3,)), dma((3,)), dma, dma] + comm.sems,
        compiler_params=pltpu.CompilerParams(vmem_limit_bytes=VMEM_LIMIT),
    )(vec, *comm.bufs)
    return res[0], res[1:]


REDUCE_ROW_BLOCKS = 2


def add_own_half(place, g, recv, name):
    _, r, cc = g.shape
    half = r // 2
    depth = 3

    def body(p_ref, g_ref, r_ref, o_ref, ob_ref, gbuf, rbuf, obuf, isem, osem, fsem):
        chip, core = p_ref[0], p_ref[1]
        rows = pl.ds(pl.multiple_of(core * half, 8), half)

        def loads(s):
            slot = s % depth
            return (pltpu.make_async_copy(g_ref.at[s, rows, :], gbuf.at[slot], isem.at[slot, 0]),
                    pltpu.make_async_copy(r_ref.at[s], rbuf.at[slot], isem.at[slot, 1]))

        def store(s):
            return pltpu.make_async_copy(obuf.at[s % depth], ob_ref.at[s], osem.at[s % depth])

        for s in range(depth):
            for cp in loads(s):
                cp.start()
        for s in range(N_SHARD):
            slot = s % depth
            for cp in loads(s):
                cp.wait()
            v = gbuf[slot] + rbuf[slot]
            if s >= depth:
                store(s - depth).wait()
            obuf[slot] = v.astype(BF16)
            store(s).start()

            @pl.when(chip == s)
            def _():
                gbuf[slot] = v
                own = pltpu.make_async_copy(gbuf.at[slot], o_ref, fsem)
                own.start()
                own.wait()

            if s + depth < N_SHARD:
                for cp in loads(s + depth):
                    cp.start()
        for s in range(max(N_SHARD - depth, 0), N_SHARD):
            store(s).wait()

    dma = pltpu.SemaphoreType.DMA
    return pl.pallas_call(
        body, name=name,
        out_shape=(jax.ShapeDtypeStruct((half, cc), F32),
                   jax.ShapeDtypeStruct((N_SHARD, half, cc), BF16)),
        in_specs=[pl.BlockSpec(memory_space=pltpu.SMEM), ANY, ANY],
        out_specs=(ANY, ANY),
        scratch_shapes=[pltpu.VMEM((depth, half, cc), F32), pltpu.VMEM((depth, half, cc), F32),
                        pltpu.VMEM((depth, half, cc), BF16), dma((depth, 2)), dma((depth,)), dma],
        compiler_params=pltpu.CompilerParams(vmem_limit_bytes=VMEM_LIMIT),
    )(place, g, recv)


def add_chip_sums(place, own, recv, name):
    half, cc = own.shape
    nb, depth = 4, 3
    tile = half // nb

    def body(p_ref, o_ref, r_ref, out_ref, obuf, rbuf, sbuf, isem, osem):
        core = p_ref[1]

        def loads(i):
            slot, rows = i % depth, pl.ds(i * tile, tile)
            return (pltpu.make_async_copy(o_ref.at[rows, :], obuf.at[slot], isem.at[slot, 0]),
                    pltpu.make_async_copy(r_ref.at[:, rows, :], rbuf.at[slot], isem.at[slot, 1]))

        def store(i):
            rows = pl.ds(pl.multiple_of(core * half + i * tile, 8), tile)
            return pltpu.make_async_copy(sbuf.at[i % depth], out_ref.at[rows, :], osem.at[i % depth])

        for i in range(depth):
            for cp in loads(i):
                cp.start()
        for i in range(nb):
            slot = i % depth
            for cp in loads(i):
                cp.wait()
            acc = obuf[slot] + rbuf[slot, 0].astype(F32)
            acc = acc + rbuf[slot, 1].astype(F32)
            acc = acc + rbuf[slot, 2].astype(F32)
            if i >= depth:
                store(i - depth).wait()
            sbuf[slot] = acc
            store(i).start()
            if i + depth < nb:
                for cp in loads(i + depth):
                    cp.start()
        for i in range(max(nb - depth, 0), nb):
            store(i).wait()

    dma = pltpu.SemaphoreType.DMA
    return pl.pallas_call(
        body, name=name,
        out_shape=jax.ShapeDtypeStruct((2 * half, cc), F32),
        in_specs=[pl.BlockSpec(memory_space=pltpu.SMEM), ANY, ANY], out_specs=ANY,
        scratch_shapes=[pltpu.VMEM((depth, tile, cc), F32), pltpu.VMEM((depth, 3, tile, cc), BF16),
                        pltpu.VMEM((depth, tile, cc), F32), dma((depth, 2)), dma((depth,))],
        compiler_params=pltpu.CompilerParams(vmem_limit_bytes=VMEM_LIMIT),
    )(place, own, recv)


def adamw(w, m, v, g, name, copy_g=False):
    r, cc = w.shape
    tr = next(t for t in (256, 352, r) if r % t == 0)
    bc1 = 1.0 / (1.0 - ADAM_B1 ** ADAM_STEP)
    bc2 = 1.0 / (1.0 - ADAM_B2 ** ADAM_STEP)

    def body(w_ref, m_ref, v_ref, g_ref, d_ref, mo_ref, vo_ref, *go_ref):
        gv = g_ref[...]
        mn = ADAM_B1 * m_ref[...] + (1.0 - ADAM_B1) * gv
        vn = ADAM_B2 * v_ref[...] + (1.0 - ADAM_B2) * (gv * gv)
        mo_ref[...] = mn
        vo_ref[...] = vn
        d_ref[...] = -ADAM_LR * ((mn * bc1) / (jnp.sqrt(vn * bc2) + ADAM_EPS) + ADAM_WD * w_ref[...])
        if copy_g:
            go_ref[0][...] = gv

    blk = pl.BlockSpec((tr, cc), lambda i: (i, 0))
    shp = jax.ShapeDtypeStruct((r, cc), F32)
    nout = 4 if copy_g else 3
    return pl.pallas_call(
        body, name=name, grid=(r // tr,),
        out_shape=(shp,) * nout, in_specs=[blk] * 4, out_specs=(blk,) * nout,
        compiler_params=_cparams(("arbitrary",)),
    )(w, m, v, g)


GATHER_FIRST = ("ffn1_w_gate", "ffn1_w_up", "ffn1_w_down")
GATHER_ON = {"ffn1_fwd": ("w_in",),
             "inproj_fwd": ("s5_glu_val", "s5_glu_gate", "hg_w_out", "w_merge_out"),
             "s5_fwd": ("ffn2_w_gate", "ffn2_w_up"),
             "hgrn_fwd": ("ffn2_w_down", "ple_w_gate", "ple_w_proj")}
REDUCE = ((("ple_w_gate", "ple_w_proj", "ffn2_w_gate", "ffn2_w_up", "ffn2_w_down"), "merge_bwd", "hgrn_bwd"),
          (("w_merge_out", "s5_glu_val", "s5_glu_gate", "hg_w_out"), "s5_bwd", "inproj_bwd"),
          (("w_in",), None, "ffn1_bwd"),
          (("ffn1_w_gate",), "g_ffn1_w_up", "g_ffn1_w_down"),
          (("ffn1_w_up",), "g_ffn1_w_down", None),
          (("ffn1_w_down",), None, None))


def merge_comms(comms):
    if len(comms) == 1:
        return comms[0], [len(comms[0].outs)]
    bufs, outs, sems, alias, spans = [], [], [], {}, []
    for c in comms:
        spans.append((len(bufs), len(bufs) + len(c.bufs), len(outs), len(outs) + len(c.outs),
                      len(sems), len(sems) + len(c.sems)))
        alias.update({len(bufs) + i: len(outs) + o for i, o in c.alias.items()})
        bufs, outs, sems = bufs + c.bufs, outs + c.outs, sems + c.sems

    def hook(which):
        def run(b, o, s):
            for c, (b0, b1, o0, o1, s0, s1) in zip(comms, spans):
                if which in c.hooks:
                    c.hooks[which](b[b0:b1], o[o0:o1], s[s0:s1])
        return run

    hooks = {w: hook(w) for w in ("first", "mid", "last") if any(w in c.hooks for c in comms)}
    return Comm(bufs, outs, alias, sems, hooks), [len(c.outs) for c in comms]


class DistSchedule(Schedule):
    def __init__(self, w_rows, chip, core):
        first = gather_comm([_gather_buffer(k, w_rows[k], chip) for k in GATHER_FIRST])
        later = [k for k in BIG if k not in GATHER_FIRST]
        placed, gathered = place_shards([w_rows[k] for k in later], [BIG_SHARD[k][0] for k in later], first,
                                        "place_shards_gather_ffn1")
        super().__init__(zip(GATHER_FIRST, gathered))
        self.bufs = dict(zip(later, placed))
        self.place = jnp.stack([chip, core])
        self.sums, self.halves = {}, {}

    def _exchange(self, names):
        return exchange_comm([self.grads[k] for k in names])

    def _scatter(self, names):
        return scatter_comm([self.sums[k][1] for k in names])

    def _pair_sums(self, names, recv):
        for k, r in zip(names, recv):
            self.sums[k] = add_own_half(self.place, self.grads[k], r, "pair_sum_" + k)

    def _chip_sums(self, names, recv):
        for k, r in zip(names, recv):
            self.halves[k] = add_chip_sums(self.place, self.sums[k][0], r, "chip_sum_" + k)

    def before(self, kernel_name):
        comms, takers = [], []
        if kernel_name in GATHER_ON:
            names = GATHER_ON[kernel_name]
            comms.append(gather_comm([self.bufs[k] for k in names]))
            takers.append(lambda res, names=names: self.wts.update(zip(names, res)))
        for names, exchange_on, scatter_on in REDUCE:
            if kernel_name == exchange_on:
                comms.append(self._exchange(names))
                takers.append(lambda res, names=names: self._pair_sums(names, res))
            if kernel_name == scatter_on:
                if exchange_on is None:
                    self._pair_sums(names, run_comm(self._exchange(names), "exchange_" + names[0]))
                comms.append(self._scatter(names))
                takers.append(lambda res, names=names: self._chip_sums(names, res))
        if not comms:
            return None
        merged, counts = merge_comms(comms)
        self.pending = (takers, counts)
        return merged

    def after(self, kernel_name, results):
        if not results:
            return
        takers, counts = self.pending
        start = 0
        for take, count in zip(takers, counts):
            take(results[start:start + count])
            start += count

    def finish(self, small):
        tail = [names for names, _, scatter_on in REDUCE if scatter_on is None]
        alone = [k for names, exchange_on, scatter_on in REDUCE if scatter_on is None and exchange_on is None
                 for k in names]
        self._pair_sums(alone, run_comm(self._exchange(alone), "exchange_tail"))
        tail = [k for names in tail for k in names]
        early = [k for k in BIG if k not in tail]
        both, counts = merge_comms([self._scatter(tail), join_comm([self.halves[k] for k in early])])
        total, res = allreduce_small(small, both)
        self._chip_sums(tail, res[:counts[0]])
        full = dict(zip(early, res[counts[0]:]))
        full.update(zip(tail, run_comm(join_comm([self.halves[k] for k in tail]), "join_tail")))
        return full, total


SMALL = ("ffn1_norm", "mix_norm", "s5_lam_re", "s5_lam_im", "s5_log_dt", "s5_b_re", "s5_b_im", "s5_c_re",
         "s5_c_im", "s5_d", "hg_lower_bound", "hg_out_norm", "ffn2_norm", "ple_norm", "final_norm")
WEIGHTS = ("ffn1_norm", "ffn1_w_gate", "ffn1_w_up", "ffn1_w_down", "mix_norm", "w_in", "s5_lam_re", "s5_lam_im",
           "s5_log_dt", "s5_b_re", "s5_b_im", "s5_c_re", "s5_c_im", "s5_d", "s5_glu_val", "s5_glu_gate",
           "hg_lower_bound", "hg_out_norm", "hg_w_out", "w_merge_out", "ffn2_norm", "ffn2_w_gate", "ffn2_w_up",
           "ffn2_w_down", "ple_norm", "ple_w_gate", "ple_w_proj", "final_norm")


def _as_rows(name, w):
    return jnp.swapaxes(w[0], 0, 1) if name in FFN_T else w[0]


def _from_rows(name, w):
    return (jnp.swapaxes(w, 0, 1) if name in FFN_T else w)[None]


def _gather_buffer(name, w_rows, chip):
    r, c = BIG_SHARD[name]
    shard = jnp.pad(w_rows.astype(BF16), ((0, r - w_rows.shape[0]), (0, 0)))
    return lax.dynamic_update_slice(jnp.zeros((N_SHARD, r, c), BF16), shard[None], (chip, 0, 0))


def _pack(parts):
    flat = jnp.concatenate([jnp.zeros((128,), F32)] + [a.reshape(-1) for a in parts])
    rows = -(-flat.shape[0] // 2048) * 16
    return jnp.pad(flat, (0, rows * 128 - flat.shape[0])).reshape(rows, 128)


def _unpack(vec, likes):
    flat = vec.reshape(-1)
    out, off = [], 128
    for a in likes:
        out.append(flat[off:off + a.size].reshape(a.shape))
        off += a.size
    return out


def _small_view(name, w):
    if name.startswith("s5_") and name != "s5_d":
        return w[0]
    if name == "final_norm":
        return w.reshape(1, D_MODEL)
    return w


def kernel(x, p, ffn1_norm, ffn1_w_gate, ffn1_w_up, ffn1_w_down, mix_norm, w_in, s5_lam_re, s5_lam_im, s5_log_dt, s5_b_re, s5_b_im, s5_c_re, s5_c_im, s5_d, s5_glu_val, s5_glu_gate, hg_lower_bound, hg_out_norm, hg_w_out, w_merge_out, ffn2_norm, ffn2_w_gate, ffn2_w_up, ffn2_w_down, ple_norm, ple_w_gate, ple_w_proj, final_norm, loss_target, m_ffn1_norm, m_ffn1_w_gate, m_ffn1_w_up, m_ffn1_w_down, m_mix_norm, m_w_in, m_s5_lam_re, m_s5_lam_im, m_s5_log_dt, m_s5_b_re, m_s5_b_im, m_s5_c_re, m_s5_c_im, m_s5_d, m_s5_glu_val, m_s5_glu_gate, m_hg_lower_bound, m_hg_out_norm, m_hg_w_out, m_w_merge_out, m_ffn2_norm, m_ffn2_w_gate, m_ffn2_w_up, m_ffn2_w_down, m_ple_norm, m_ple_w_gate, m_ple_w_proj, m_final_norm, v_ffn1_norm, v_ffn1_w_gate, v_ffn1_w_up, v_ffn1_w_down, v_mix_norm, v_w_in, v_s5_lam_re, v_s5_lam_im, v_s5_log_dt, v_s5_b_re, v_s5_b_im, v_s5_c_re, v_s5_c_im, v_s5_d, v_s5_glu_val, v_s5_glu_gate, v_hg_lower_bound, v_hg_out_norm, v_hg_w_out, v_w_merge_out, v_ffn2_norm, v_ffn2_w_gate, v_ffn2_w_up, v_ffn2_w_down, v_ple_norm, v_ple_w_gate, v_ple_w_proj, v_final_norm):
    given = dict(locals())
    wv = {k: given[k] for k in WEIGHTS}
    mv = {k: given["m_" + k] for k in WEIGHTS}
    vv = {k: given["v_" + k] for k in WEIGHTS}

    core = lax.axis_index("c").astype(jnp.int32)
    chip = (2 * lax.axis_index("x") + lax.axis_index("y")).astype(jnp.int32)
    w_rows = {k: _as_rows(k, wv[k]) for k in BIG}
    sched = DistSchedule(w_rows, chip, core)
    sm = {k: _small_view(k, wv[k]) for k in SMALL}

    loss_blk, dx, gsm = local_step(x[0], p[0, 0], loss_target[0], sched, sm)

    small_likes = [wv[k] for k in SMALL]
    packed = _pack([gsm[k] for k in SMALL])
    packed = packed.at[0, 0].set(loss_blk[0, 0])
    full, total = sched.finish(packed)
    loss = total[0, 0]
    gsmall = dict(zip(SMALL, _unpack(total, small_likes)))

    grads, deltas, new_m, new_v = {}, {}, {}, {}
    for k in BIG:
        padded = full[k].shape != w_rows[k].shape
        res = adamw(w_rows[k], _as_rows(k, mv[k]), _as_rows(k, vv[k]), full[k], "adamw_" + k, copy_g=padded)
        grads[k] = _from_rows(k, res[3] if padded else full[k])
        deltas[k], new_m[k], new_v[k] = (_from_rows(k, a) for a in res[:3])
    sw = _pack([wv[k] for k in SMALL])
    smm = _pack([mv[k] for k in SMALL])
    svv = _pack([vv[k] for k in SMALL])
    sd, smn, svn = adamw(sw, smm, svv, total, "adamw_small")
    for k, d, mn, vn in zip(SMALL, _unpack(sd, small_likes), _unpack(smn, small_likes), _unpack(svn, small_likes)):
        grads[k], deltas[k], new_m[k], new_v[k] = gsmall[k], d, mn, vn

    return (loss, dx[None], *[grads[k] for k in WEIGHTS], *[deltas[k] for k in WEIGHTS],
            *[new_m[k] for k in WEIGHTS], *[new_v[k] for k in WEIGHTS])
```
